```python
import jax, jax.numpy as jnp
from jax import lax
import numpy as np

D_MODEL = 2048
BATCH = 8
SEQ = 4096
DEPTH = 4

HEAD_DIM = 128
ATTN_WIDTH = D_MODEL // 2
N_Q_HEADS = ATTN_WIDTH // HEAD_DIM
N_KV_HEADS = N_Q_HEADS // 4
KV_WIDTH = N_KV_HEADS * HEAD_DIM
CONV_WIDTH = D_MODEL // 4
CONV_KERNEL = 31
SGU_WIDTH = D_MODEL // 4
SGU_HEAD_DIM = 128
SGU_HEADS = SGU_WIDTH // SGU_HEAD_DIM
CHUNK = 128
MIX_WIDTH = ATTN_WIDTH + CONV_WIDTH + SGU_WIDTH
IN_WIDTH = ATTN_WIDTH + 2 * KV_WIDTH + 2 * CONV_WIDTH + 2 * SGU_WIDTH
WINDOW = 128
BLOCK = 128
ROPE_THETA = 500000.0
ROT_DIM = HEAD_DIM // 4
D_FF = ((8 * D_MODEL // 3 + 255) // 256) * 256
EPS = 1e-6

kernel_name = "hybrid_parallel_groups_encoder"


def rms_norm(x, g):
    xf = x.astype(jnp.float32)
    y = xf * lax.rsqrt(jnp.mean(xf * xf, axis=-1, keepdims=True) + EPS)
    return (y * g.astype(jnp.float32)).astype(x.dtype)


def layer_norm(x, g, b):
    xf = x.astype(jnp.float32)
    mu = jnp.mean(xf, axis=-1, keepdims=True)
    var = jnp.mean(jnp.square(xf - mu), axis=-1, keepdims=True)
    y = (xf - mu) * lax.rsqrt(var + EPS)
    return (y * g.astype(jnp.float32) + b.astype(jnp.float32)).astype(x.dtype)


def rope_tables(seq):
    pos = jnp.arange(seq, dtype=jnp.float32)
    inv = ROPE_THETA ** (-jnp.arange(0, ROT_DIM, 2, dtype=jnp.float32) / ROT_DIM)
    ang = pos[:, None] * inv[None, :]
    return jnp.cos(ang), jnp.sin(ang)


def partial_rope(t, cos, sin):
    half = ROT_DIM // 2
    t1 = t[..., :half].astype(jnp.float32)
    t2 = t[..., half:ROT_DIM].astype(jnp.float32)
    c = cos[None, :, None, :]
    s = sin[None, :, None, :]
    rot = jnp.concatenate([t1 * c - t2 * s, t2 * c + t1 * s], axis=-1).astype(t.dtype)
    return jnp.concatenate([rot, t[..., ROT_DIM:]], axis=-1)


def windowed_gqa_sink(q, k, v, sink):
    B, S, H, Dh = q.shape
    G = k.shape[2]
    R = H // G
    nb = S // BLOCK
    pad = ((0, 0), (BLOCK, BLOCK), (0, 0), (0, 0))
    kp = jnp.pad(k, pad).reshape(B, nb + 2, BLOCK, G, Dh)
    vp = jnp.pad(v, pad).reshape(B, nb + 2, BLOCK, G, Dh)
    kw = jnp.concatenate([kp[:, :-2], kp[:, 1:-1], kp[:, 2:]], axis=2)
    vw = jnp.concatenate([vp[:, :-2], vp[:, 1:-1], vp[:, 2:]], axis=2)
    qb = q.reshape(B, nb, BLOCK, G, R, Dh)
    scale = 1.0 / float(np.sqrt(Dh))
    s = jnp.einsum('bnqgrd,bnkgd->bngrqk', qb, kw).astype(jnp.float32) * scale
    qpos = jnp.arange(S).reshape(nb, BLOCK)
    kpos = jnp.arange(nb)[:, None] * BLOCK - BLOCK + jnp.arange(3 * BLOCK)[None, :]
    valid = ((kpos[:, None, :] >= 0) & (kpos[:, None, :] < S)
             & (jnp.abs(qpos[:, :, None] - kpos[:, None, :]) <= WINDOW))
    s = jnp.where(valid[None, :, None, None], s, jnp.finfo(jnp.float32).min)
    sk = jnp.broadcast_to(sink.astype(jnp.float32).reshape(1, 1, G, R, 1, 1), s.shape[:-1] + (1,))
    p = jax.nn.softmax(jnp.concatenate([s, sk], axis=-1), axis=-1)[..., :-1]
    o = jnp.einsum('bngrqk,bnkgd->bnqgrd', p.astype(v.dtype), vw)
    return o.reshape(B, S, H * Dh)


def conformer_conv(a, gate, dw_w, dw_b, ln_g, ln_b):
    c = a * jax.nn.sigmoid(gate)
    C = c.shape[-1]
    c = lax.conv_general_dilated(
        c, dw_w[:, None, :].astype(c.dtype), window_strides=(1,),
        padding=[((CONV_KERNEL - 1) // 2, (CONV_KERNEL - 1) // 2)],
        dimension_numbers=('NWC', 'WIO', 'NWC'), feature_group_count=C) + dw_b
    c = layer_norm(c, ln_g, ln_b)
    return jax.nn.silu(c)


def spatial_gating(uv, ln_g, ln_b, w_s, b_s):
    B, S, _ = uv.shape
    uv = jax.nn.gelu(uv, approximate=False)
    u, v = jnp.split(uv, 2, axis=-1)
    v = layer_norm(v, ln_g, ln_b)
    vc = v.reshape(B, S // CHUNK, CHUNK, SGU_HEADS, SGU_HEAD_DIM)
    sp = jnp.einsum('hpq,bcqhe->bcphe', w_s, vc) + b_s.T[None, None, :, :, None]
    return u * sp.reshape(B, S, SGU_WIDTH)


def hybrid_mixer(h, w_in, sink, dw_w, dw_b, cln_g, cln_b, sln_g, sln_b, sgu_w, sgu_b, w_out, cos, sin):
    B, S, _ = h.shape
    z = h @ w_in
    offs = np.cumsum([ATTN_WIDTH, KV_WIDTH, KV_WIDTH, CONV_WIDTH, CONV_WIDTH]).tolist()
    q, k, v, ca, cg, uv = jnp.split(z, offs, axis=-1)
    q = partial_rope(q.reshape(B, S, N_Q_HEADS, HEAD_DIM), cos, sin)
    k = partial_rope(k.reshape(B, S, N_KV_HEADS, HEAD_DIM), cos, sin)
    v = v.reshape(B, S, N_KV_HEADS, HEAD_DIM)
    attn = windowed_gqa_sink(q, k, v, sink)
    conv = conformer_conv(ca, cg, dw_w, dw_b, cln_g, cln_b)
    sgu = spatial_gating(uv, sln_g, sln_b, sgu_w, sgu_b)
    return jnp.concatenate([attn.astype(h.dtype), conv.astype(h.dtype), sgu.astype(h.dtype)], axis=-1) @ w_out


def swiglu(h, w_gate, w_up, w_down):
    return (jax.nn.silu(h @ w_gate) * (h @ w_up)) @ w_down


def _fwd_setup_inputs(seed: int = 0) -> dict:
    key = jax.random.key(seed)
    ks = jax.random.split(key, 20)
    f32 = jnp.float32
    nrm = lambda k, shape, sc: jax.random.normal(k, shape, f32) * sc
    return {
        "x": nrm(ks[0], (BATCH, SEQ, D_MODEL), 1.0),
        "mix_norm_g": 1.0 + nrm(ks[1], (DEPTH, D_MODEL), 0.02),
        "w_in": nrm(ks[2], (DEPTH, D_MODEL, IN_WIDTH), D_MODEL ** -0.5),
        "sink": nrm(ks[3], (DEPTH, N_Q_HEADS), 0.5),
        "conv_dw_w": nrm(ks[4], (DEPTH, CONV_KERNEL, CONV_WIDTH), CONV_KERNEL ** -0.5),
        "conv_dw_b": nrm(ks[5], (DEPTH, CONV_WIDTH), 0.02),
        "conv_ln_g": 1.0 + nrm(ks[6], (DEPTH, CONV_WIDTH), 0.02),
        "conv_ln_b": nrm(ks[7], (DEPTH, CONV_WIDTH), 0.02),
        "sgu_ln_g": 1.0 + nrm(ks[8], (DEPTH, SGU_WIDTH), 0.02),
        "sgu_ln_b": nrm(ks[9], (DEPTH, SGU_WIDTH), 0.02),
        "sgu_w": nrm(ks[10], (DEPTH, SGU_HEADS, CHUNK, CHUNK), CHUNK ** -0.5),
        "sgu_b": 1.0 + nrm(ks[11], (DEPTH, SGU_HEADS, CHUNK), 0.02),
        "w_out": nrm(ks[12], (DEPTH, MIX_WIDTH, D_MODEL), MIX_WIDTH ** -0.5),
        "ffn_norm_g": 1.0 + nrm(ks[13], (DEPTH, D_MODEL), 0.02),
        "w_gate": nrm(ks[14], (DEPTH, D_MODEL, D_FF), D_MODEL ** -0.5),
        "w_up": nrm(ks[15], (DEPTH, D_MODEL, D_FF), D_MODEL ** -0.5),
        "w_down": nrm(ks[16], (DEPTH, D_FF, D_MODEL), D_FF ** -0.5),
        "final_norm_g": 1.0 + nrm(ks[17], (D_MODEL,), 0.02),
    }


def _fwd_reference(x, mix_norm_g, w_in, sink, conv_dw_w, conv_dw_b, conv_ln_g, conv_ln_b,
              sgu_ln_g, sgu_ln_b, sgu_w, sgu_b, w_out, ffn_norm_g, w_gate, w_up, w_down,
              final_norm_g):
    cos, sin = rope_tables(x.shape[1])
    for l in range(DEPTH):
        h = rms_norm(x, mix_norm_g[l])
        x = x + hybrid_mixer(h, w_in[l], sink[l], conv_dw_w[l], conv_dw_b[l], conv_ln_g[l],
                             conv_ln_b[l], sgu_ln_g[l], sgu_ln_b[l], sgu_w[l], sgu_b[l],
                             w_out[l], cos, sin)
        h = rms_norm(x, ffn_norm_g[l])
        x = x + swiglu(h, w_gate[l], w_up[l], w_down[l])
    return rms_norm(x, final_norm_g)


import jax as _jax
import jax.numpy as _jnp

TWIN_FORMAT = 'train_step'
FWD_PARAMS = ['x', 'mix_norm_g', 'w_in', 'sink', 'conv_dw_w', 'conv_dw_b', 'conv_ln_g', 'conv_ln_b', 'sgu_ln_g', 'sgu_ln_b', 'sgu_w', 'sgu_b', 'w_out', 'ffn_norm_g', 'w_gate', 'w_up', 'w_down', 'final_norm_g']
TWIN_WEIGHTS = ['mix_norm_g', 'w_in', 'sink', 'conv_dw_w', 'conv_dw_b', 'conv_ln_g', 'conv_ln_b', 'sgu_ln_g', 'sgu_ln_b', 'sgu_w', 'sgu_b', 'w_out', 'ffn_norm_g', 'w_gate', 'w_up', 'w_down', 'final_norm_g']
TWIN_DIFF_INPUT = 'x'
TWIN_INPUTS = ['x', 'mix_norm_g', 'w_in', 'sink', 'conv_dw_w', 'conv_dw_b', 'conv_ln_g', 'conv_ln_b', 'sgu_ln_g', 'sgu_ln_b', 'sgu_w', 'sgu_b', 'w_out', 'ffn_norm_g', 'w_gate', 'w_up', 'w_down', 'final_norm_g', 'loss_target', 'm_mix_norm_g', 'm_w_in', 'm_sink', 'm_conv_dw_w', 'm_conv_dw_b', 'm_conv_ln_g', 'm_conv_ln_b', 'm_sgu_ln_g', 'm_sgu_ln_b', 'm_sgu_w', 'm_sgu_b', 'm_w_out', 'm_ffn_norm_g', 'm_w_gate', 'm_w_up', 'm_w_down', 'm_final_norm_g', 'v_mix_norm_g', 'v_w_in', 'v_sink', 'v_conv_dw_w', 'v_conv_dw_b', 'v_conv_ln_g', 'v_conv_ln_b', 'v_sgu_ln_g', 'v_sgu_ln_b', 'v_sgu_w', 'v_sgu_b', 'v_w_out', 'v_ffn_norm_g', 'v_w_gate', 'v_w_up', 'v_w_down', 'v_final_norm_g']
TWIN_OUTPUTS = ['loss', 'grad_x', 'grad_mix_norm_g', 'grad_w_in', 'grad_sink', 'grad_conv_dw_w', 'grad_conv_dw_b', 'grad_conv_ln_g', 'grad_conv_ln_b', 'grad_sgu_ln_g', 'grad_sgu_ln_b', 'grad_sgu_w', 'grad_sgu_b', 'grad_w_out', 'grad_ffn_norm_g', 'grad_w_gate', 'grad_w_up', 'grad_w_down', 'grad_final_norm_g', 'delta_mix_norm_g', 'delta_w_in', 'delta_sink', 'delta_conv_dw_w', 'delta_conv_dw_b', 'delta_conv_ln_g', 'delta_conv_ln_b', 'delta_sgu_ln_g', 'delta_sgu_ln_b', 'delta_sgu_w', 'delta_sgu_b', 'delta_w_out', 'delta_ffn_norm_g', 'delta_w_gate', 'delta_w_up', 'delta_w_down', 'delta_final_norm_g', 'new_m_mix_norm_g', 'new_m_w_in', 'new_m_sink', 'new_m_conv_dw_w', 'new_m_conv_dw_b', 'new_m_conv_ln_g', 'new_m_conv_ln_b', 'new_m_sgu_ln_g', 'new_m_sgu_ln_b', 'new_m_sgu_w', 'new_m_sgu_b', 'new_m_w_out', 'new_m_ffn_norm_g', 'new_m_w_gate', 'new_m_w_up', 'new_m_w_down', 'new_m_final_norm_g', 'new_v_mix_norm_g', 'new_v_w_in', 'new_v_sink', 'new_v_conv_dw_w', 'new_v_conv_dw_b', 'new_v_conv_ln_g', 'new_v_conv_ln_b', 'new_v_sgu_ln_g', 'new_v_sgu_ln_b', 'new_v_sgu_w', 'new_v_sgu_b', 'new_v_w_out', 'new_v_ffn_norm_g', 'new_v_w_gate', 'new_v_w_up', 'new_v_w_down', 'new_v_final_norm_g']
TWIN_LEAF_KINDS = {'loss': 'loss', 'grad_x': 'grad_x', 'grad_mix_norm_g': 'grad_w', 'grad_w_in': 'grad_w', 'grad_sink': 'grad_w', 'grad_conv_dw_w': 'grad_w', 'grad_conv_dw_b': 'grad_w', 'grad_conv_ln_g': 'grad_w', 'grad_conv_ln_b': 'grad_w', 'grad_sgu_ln_g': 'grad_w', 'grad_sgu_ln_b': 'grad_w', 'grad_sgu_w': 'grad_w', 'grad_sgu_b': 'grad_w', 'grad_w_out': 'grad_w', 'grad_ffn_norm_g': 'grad_w', 'grad_w_gate': 'grad_w', 'grad_w_up': 'grad_w', 'grad_w_down': 'grad_w', 'grad_final_norm_g': 'grad_w', 'delta_mix_norm_g': 'delta_w', 'delta_w_in': 'delta_w', 'delta_sink': 'delta_w', 'delta_conv_dw_w': 'delta_w', 'delta_conv_dw_b': 'delta_w', 'delta_conv_ln_g': 'delta_w', 'delta_conv_ln_b': 'delta_w', 'delta_sgu_ln_g': 'delta_w', 'delta_sgu_ln_b': 'delta_w', 'delta_sgu_w': 'delta_w', 'delta_sgu_b': 'delta_w', 'delta_w_out': 'delta_w', 'delta_ffn_norm_g': 'delta_w', 'delta_w_gate': 'delta_w', 'delta_w_up': 'delta_w', 'delta_w_down': 'delta_w', 'delta_final_norm_g': 'delta_w', 'new_m_mix_norm_g': 'new_m', 'new_m_w_in': 'new_m', 'new_m_sink': 'new_m', 'new_m_conv_dw_w': 'new_m', 'new_m_conv_dw_b': 'new_m', 'new_m_conv_ln_g': 'new_m', 'new_m_conv_ln_b': 'new_m', 'new_m_sgu_ln_g': 'new_m', 'new_m_sgu_ln_b': 'new_m', 'new_m_sgu_w': 'new_m', 'new_m_sgu_b': 'new_m', 'new_m_w_out': 'new_m', 'new_m_ffn_norm_g': 'new_m', 'new_m_w_gate': 'new_m', 'new_m_w_up': 'new_m', 'new_m_w_down': 'new_m', 'new_m_final_norm_g': 'new_m', 'new_v_mix_norm_g': 'new_v', 'new_v_w_in': 'new_v', 'new_v_sink': 'new_v', 'new_v_conv_dw_w': 'new_v', 'new_v_conv_dw_b': 'new_v', 'new_v_conv_ln_g': 'new_v', 'new_v_conv_ln_b': 'new_v', 'new_v_sgu_ln_g': 'new_v', 'new_v_sgu_ln_b': 'new_v', 'new_v_sgu_w': 'new_v', 'new_v_sgu_b': 'new_v', 'new_v_w_out': 'new_v', 'new_v_ffn_norm_g': 'new_v', 'new_v_w_gate': 'new_v', 'new_v_w_up': 'new_v', 'new_v_w_down': 'new_v', 'new_v_final_norm_g': 'new_v'}


def _forward(args):
    return _fwd_reference(*[args[k] for k in FWD_PARAMS])


def _output_shape():
    def fwd():
        inp = _fwd_setup_inputs(0)
        return _fwd_reference(*[inp[k] for k in FWD_PARAMS])
    out = _jax.eval_shape(fwd)
    return out.shape, out.dtype

N_MICROBATCH = 1
ADAM_LR = 0.001
ADAM_B1 = 0.9
ADAM_B2 = 0.999
ADAM_EPS = 1e-08
ADAM_WD = 0.01
ADAM_STEP = 10
PER_EXAMPLE_BATCH_AXIS = {'x': 0, 'loss_target': 0}
SHARED_INPUTS = []
_WEIGHT_DTYPES = {'mix_norm_g': _jnp.float32, 'w_in': _jnp.float32, 'sink': _jnp.float32, 'conv_dw_w': _jnp.float32, 'conv_dw_b': _jnp.float32, 'conv_ln_g': _jnp.float32, 'conv_ln_b': _jnp.float32, 'sgu_ln_g': _jnp.float32, 'sgu_ln_b': _jnp.float32, 'sgu_w': _jnp.float32, 'sgu_b': _jnp.float32, 'w_out': _jnp.float32, 'ffn_norm_g': _jnp.float32, 'w_gate': _jnp.float32, 'w_up': _jnp.float32, 'w_down': _jnp.float32, 'final_norm_g': _jnp.float32}
MOMENT_SCALE = {'mix_norm_g': 5.819191e-02, 'w_in': 4.334356e-02, 'sink': 7.178282e-04, 'conv_dw_w': 4.983341e-02, 'conv_dw_b': 1.179036e-01, 'conv_ln_g': 6.386175e-02, 'conv_ln_b': 6.485847e-02, 'sgu_ln_g': 6.023369e-02, 'sgu_ln_b': 5.182484e-02, 'sgu_w': 5.238284e-02, 'sgu_b': 5.338262e-02, 'w_out': 4.784861e-02, 'ffn_norm_g': 5.744550e-02, 'w_gate': 2.499820e-02, 'w_up': 2.426022e-02, 'w_down': 4.023984e-02, 'final_norm_g': 1.604400e+01}


def _to_microbatches(a, axis):
    t = _jnp.moveaxis(a, axis, 0)
    t = t.reshape((N_MICROBATCH, t.shape[0] // N_MICROBATCH) + t.shape[1:])
    return _jnp.moveaxis(t, 1, axis + 1)


def setup_inputs(seed: int = 0) -> dict:
    inp = _fwd_setup_inputs(seed)
    key = _jax.random.fold_in(_jax.random.key(seed), 7919)
    shape, _ = _output_shape()
    out = dict(inp)
    out["loss_target"] = _jax.random.normal(_jax.random.fold_in(key, 0), shape, _jnp.float32)
    for i, name in enumerate(TWIN_WEIGHTS):
        w = inp[name].astype(_jnp.float32)
        if MOMENT_SCALE is None:
            s = _jnp.sqrt(_jnp.mean(_jnp.square(w)) + 1e-30)
        else:
            s = MOMENT_SCALE[name]
        km, kv = _jax.random.split(_jax.random.fold_in(key, i + 1))
        out[name] = w
        out["m_" + name] = s * _jax.random.normal(km, w.shape, _jnp.float32)
        out["v_" + name] = (s * s) * _jax.random.uniform(kv, w.shape, _jnp.float32, 0.5, 1.5)
    if N_MICROBATCH > 1:
        for name, axis in PER_EXAMPLE_BATCH_AXIS.items():
            out[name] = _to_microbatches(out[name], axis)
    return {'x': out['x'], 'mix_norm_g': out['mix_norm_g'], 'w_in': out['w_in'], 'sink': out['sink'], 'conv_dw_w': out['conv_dw_w'], 'conv_dw_b': out['conv_dw_b'], 'conv_ln_g': out['conv_ln_g'], 'conv_ln_b': out['conv_ln_b'], 'sgu_ln_g': out['sgu_ln_g'], 'sgu_ln_b': out['sgu_ln_b'], 'sgu_w': out['sgu_w'], 'sgu_b': out['sgu_b'], 'w_out': out['w_out'], 'ffn_norm_g': out['ffn_norm_g'], 'w_gate': out['w_gate'], 'w_up': out['w_up'], 'w_down': out['w_down'], 'final_norm_g': out['final_norm_g'], 'loss_target': out['loss_target'], 'm_mix_norm_g': out['m_mix_norm_g'], 'm_w_in': out['m_w_in'], 'm_sink': out['m_sink'], 'm_conv_dw_w': out['m_conv_dw_w'], 'm_conv_dw_b': out['m_conv_dw_b'], 'm_conv_ln_g': out['m_conv_ln_g'], 'm_conv_ln_b': out['m_conv_ln_b'], 'm_sgu_ln_g': out['m_sgu_ln_g'], 'm_sgu_ln_b': out['m_sgu_ln_b'], 'm_sgu_w': out['m_sgu_w'], 'm_sgu_b': out['m_sgu_b'], 'm_w_out': out['m_w_out'], 'm_ffn_norm_g': out['m_ffn_norm_g'], 'm_w_gate': out['m_w_gate'], 'm_w_up': out['m_w_up'], 'm_w_down': out['m_w_down'], 'm_final_norm_g': out['m_final_norm_g'], 'v_mix_norm_g': out['v_mix_norm_g'], 'v_w_in': out['v_w_in'], 'v_sink': out['v_sink'], 'v_conv_dw_w': out['v_conv_dw_w'], 'v_conv_dw_b': out['v_conv_dw_b'], 'v_conv_ln_g': out['v_conv_ln_g'], 'v_conv_ln_b': out['v_conv_ln_b'], 'v_sgu_ln_g': out['v_sgu_ln_g'], 'v_sgu_ln_b': out['v_sgu_ln_b'], 'v_sgu_w': out['v_sgu_w'], 'v_sgu_b': out['v_sgu_b'], 'v_w_out': out['v_w_out'], 'v_ffn_norm_g': out['v_ffn_norm_g'], 'v_w_gate': out['v_w_gate'], 'v_w_up': out['v_w_up'], 'v_w_down': out['v_w_down'], 'v_final_norm_g': out['v_final_norm_g']}


def _loss(weights, diff, rest, loss_target):
    with _jax.named_scope("forward"):
        args = {**rest, TWIN_DIFF_INPUT: diff, **{k: w.astype(_WEIGHT_DTYPES[k]) for k, w in weights.items()}}
        y = _forward(args)
    with _jax.named_scope("loss_head"):
        err = _jnp.square(y.astype(_jnp.float32) - loss_target)
        return 0.5 * _jnp.sum(_jnp.mean(err, axis=-1)) if err.ndim else 0.5 * err


def _adamw(w, g, m, v):
    m = ADAM_B1 * m + (1.0 - ADAM_B1) * g
    v = ADAM_B2 * v + (1.0 - ADAM_B2) * _jnp.square(g)
    m_hat = m / (1.0 - ADAM_B1 ** ADAM_STEP)
    v_hat = v / (1.0 - ADAM_B2 ** ADAM_STEP)
    delta = -ADAM_LR * (m_hat / (_jnp.sqrt(v_hat) + ADAM_EPS) + ADAM_WD * w)
    return delta, m, v


def reference(x, mix_norm_g, w_in, sink, conv_dw_w, conv_dw_b, conv_ln_g, conv_ln_b, sgu_ln_g, sgu_ln_b, sgu_w, sgu_b, w_out, ffn_norm_g, w_gate, w_up, w_down, final_norm_g, loss_target, m_mix_norm_g, m_w_in, m_sink, m_conv_dw_w, m_conv_dw_b, m_conv_ln_g, m_conv_ln_b, m_sgu_ln_g, m_sgu_ln_b, m_sgu_w, m_sgu_b, m_w_out, m_ffn_norm_g, m_w_gate, m_w_up, m_w_down, m_final_norm_g, v_mix_norm_g, v_w_in, v_sink, v_conv_dw_w, v_conv_dw_b, v_conv_ln_g, v_conv_ln_b, v_sgu_ln_g, v_sgu_ln_b, v_sgu_w, v_sgu_b, v_w_out, v_ffn_norm_g, v_w_gate, v_w_up, v_w_down, v_final_norm_g):
    given = dict(x=x, mix_norm_g=mix_norm_g, w_in=w_in, sink=sink, conv_dw_w=conv_dw_w, conv_dw_b=conv_dw_b, conv_ln_g=conv_ln_g, conv_ln_b=conv_ln_b, sgu_ln_g=sgu_ln_g, sgu_ln_b=sgu_ln_b, sgu_w=sgu_w, sgu_b=sgu_b, w_out=w_out, ffn_norm_g=ffn_norm_g, w_gate=w_gate, w_up=w_up, w_down=w_down, final_norm_g=final_norm_g, loss_target=loss_target, m_mix_norm_g=m_mix_norm_g, m_w_in=m_w_in, m_sink=m_sink, m_conv_dw_w=m_conv_dw_w, m_conv_dw_b=m_conv_dw_b, m_conv_ln_g=m_conv_ln_g, m_conv_ln_b=m_conv_ln_b, m_sgu_ln_g=m_sgu_ln_g, m_sgu_ln_b=m_sgu_ln_b, m_sgu_w=m_sgu_w, m_sgu_b=m_sgu_b, m_w_out=m_w_out, m_ffn_norm_g=m_ffn_norm_g, m_w_gate=m_w_gate, m_w_up=m_w_up, m_w_down=m_w_down, m_final_norm_g=m_final_norm_g, v_mix_norm_g=v_mix_norm_g, v_w_in=v_w_in, v_sink=v_sink, v_conv_dw_w=v_conv_dw_w, v_conv_dw_b=v_conv_dw_b, v_conv_ln_g=v_conv_ln_g, v_conv_ln_b=v_conv_ln_b, v_sgu_ln_g=v_sgu_ln_g, v_sgu_ln_b=v_sgu_ln_b, v_sgu_w=v_sgu_w, v_sgu_b=v_sgu_b, v_w_out=v_w_out, v_ffn_norm_g=v_ffn_norm_g, v_w_gate=v_w_gate, v_w_up=v_w_up, v_w_down=v_w_down, v_final_norm_g=v_final_norm_g)
    weights = {n: given[n] for n in TWIN_WEIGHTS}
    shared = {n: given[n] for n in SHARED_INPUTS}
    per_example = {n: given[n] for n in ['x']}
    grad_fn = _jax.value_and_grad(_loss, argnums=(0, 1))

    def one_microbatch(ex, loss_target):
        ex = dict(ex)
        diff = ex.pop(TWIN_DIFF_INPUT)
        return grad_fn(weights, diff, {**shared, **ex}, loss_target)

    if N_MICROBATCH == 1:
        loss, (grad_w, grad_x) = one_microbatch(per_example, given["loss_target"])
    else:
        def body(carry, xs):
            loss_sum, grad_sum = carry
            l_k, (gw_k, gx_k) = one_microbatch(xs[0], xs[1])
            with _jax.named_scope("update"):
                return (loss_sum + l_k, _jax.tree.map(_jnp.add, grad_sum, gw_k)), gx_k

        init = (_jnp.zeros((), _jnp.float32), _jax.tree.map(_jnp.zeros_like, weights))
        (loss, grad_w), grad_x = _jax.lax.scan(body, init, (per_example, given["loss_target"]))
    with _jax.named_scope("update"):
        delta_w, new_m, new_v = {}, {}, {}
        for n in TWIN_WEIGHTS:
            delta_w[n], new_m[n], new_v[n] = _adamw(weights[n], grad_w[n], given["m_" + n], given["v_" + n])
    return (loss, grad_x, *[grad_w[n] for n in TWIN_WEIGHTS], *[delta_w[n] for n in TWIN_WEIGHTS],
            *[new_m[n] for n in TWIN_WEIGHTS], *[new_v[n] for n in TWIN_WEIGHTS])
```

```python
import functools
import math

import jax
import jax.numpy as jnp
from jax import lax
from jax.experimental import pallas as pl
from jax.experimental.pallas import tpu as pltpu

F32, BF16 = jnp.float32, jnp.bfloat16

D_MODEL = 2048
DEPTH = 4
HEAD_DIM = 128
N_Q_HEADS = 8
N_KV_HEADS = 2
Q_PER_KV = N_Q_HEADS // N_KV_HEADS
ATTN_WIDTH = N_Q_HEADS * HEAD_DIM
KV_WIDTH = N_KV_HEADS * HEAD_DIM
CONV_WIDTH = 512
CONV_KERNEL = 31
CONV_PAD = 16
SGU_WIDTH = 512
SGU_HEADS = 4
CHUNK = 128
IN_WIDTH = 3584
D_FF = 5632
WINDOW = 128
ROT_DIM = 32
ROPE_THETA = 500000.0
EPS = 1e-6
N_CHIPS = 4
N_DEV = 8
LANES = 128
PACK_ROWS = 64
OFF_K = ATTN_WIDTH
OFF_V = OFF_K + KV_WIDTH
OFF_CA = OFF_V + KV_WIDTH
OFF_CG = OFF_CA + CONV_WIDTH
OFF_U = OFF_CG + CONV_WIDTH
OFF_VV = OFF_U + SGU_WIDTH

ADAM_LR, ADAM_B1, ADAM_B2, ADAM_EPS, ADAM_WD, ADAM_STEP = 0.001, 0.9, 0.999, 1e-08, 0.01, 10

VMEM_LIMIT = 56 * 1024 * 1024
MESH = pl.DeviceIdType.MESH
HBM = pl.BlockSpec(memory_space=pltpu.HBM)
VMEM_SPEC = pl.BlockSpec(memory_space=pltpu.VMEM)


def _call(name, body, *, grid, in_specs, out_specs, out_shape, scratch=(), sem=None):
    params = dict(vmem_limit_bytes=VMEM_LIMIT)
    if sem is not None:
        params["dimension_semantics"] = sem
    return pl.pallas_call(
        body, name=name, grid=grid, in_specs=in_specs, out_specs=out_specs, out_shape=out_shape,
        scratch_shapes=list(scratch), compiler_params=pltpu.CompilerParams(**params))


def _sigmoid(x):
    return 1.0 / (1.0 + jnp.exp(-x))


def rms_fwd(x, g):
    S = x.shape[0]
    tm = min(512, S)

    def body(x_ref, g_ref, o_ref):
        xv = x_ref[...]
        r = lax.rsqrt(jnp.mean(xv * xv, axis=-1, keepdims=True) + EPS)
        o_ref[...] = (xv * r * g_ref[...]).astype(BF16)

    return _call("rms_fwd", body, grid=(S // tm,),
                 in_specs=[pl.BlockSpec((tm, D_MODEL), lambda i: (i, 0)), pl.BlockSpec((1, D_MODEL), lambda i: (0, 0))],
                 out_specs=pl.BlockSpec((tm, D_MODEL), lambda i: (i, 0)),
                 out_shape=jax.ShapeDtypeStruct((S, D_MODEL), BF16), sem=("parallel",))(x, g)


def _rms_bwd_math(xv, gv, dh):
    r = lax.rsqrt(jnp.mean(xv * xv, axis=-1, keepdims=True) + EPS)
    n = xv * r
    dn = dh * gv
    dx = r * (dn - n * jnp.mean(dn * n, axis=-1, keepdims=True))
    dg = jnp.sum(dh * n, axis=0, keepdims=True)
    return dx, dg


def rms_bwd(x, g, dh, dres):
    S = x.shape[0]
    tm = min(256, S)

    def body(x_ref, g_ref, dh_ref, dres_ref, dx_ref, dxb_ref, dg_ref):
        dx, dg = _rms_bwd_math(x_ref[...], g_ref[...], dh_ref[...])
        dx = dx + dres_ref[...]
        dx_ref[...] = dx
        dxb_ref[...] = dx.astype(BF16)

        @pl.when(pl.program_id(0) == 0)
        def _():
            dg_ref[...] = dg

        @pl.when(pl.program_id(0) > 0)
        def _():
            dg_ref[...] += dg

    row = pl.BlockSpec((tm, D_MODEL), lambda i: (i, 0))
    vec = pl.BlockSpec((1, D_MODEL), lambda i: (0, 0))
    return _call("rms_bwd", body, grid=(S // tm,), in_specs=[row, vec, row, row], out_specs=[row, row, vec],
                 out_shape=[jax.ShapeDtypeStruct((S, D_MODEL), F32), jax.ShapeDtypeStruct((S, D_MODEL), BF16),
                            jax.ShapeDtypeStruct((1, D_MODEL), F32)], sem=("arbitrary",))(x, g, dh, dres)


def final_loss(x, g, target):
    S = x.shape[0]
    tm = min(256, S)

    def body(x_ref, g_ref, t_ref, loss_ref, dx_ref, dxb_ref, dg_ref):
        xv = x_ref[...]
        gv = g_ref[...]
        r = lax.rsqrt(jnp.mean(xv * xv, axis=-1, keepdims=True) + EPS)
        err = xv * r * gv - t_ref[...]
        part = 0.5 * jnp.sum(jnp.mean(err * err, axis=-1, keepdims=True), axis=0, keepdims=True)
        dx, dg = _rms_bwd_math(xv, gv, err * (1.0 / D_MODEL))
        dx_ref[...] = dx
        dxb_ref[...] = dx.astype(BF16)

        @pl.when(pl.program_id(0) == 0)
        def _():
            dg_ref[...] = dg
            loss_ref[...] = part

        @pl.when(pl.program_id(0) > 0)
        def _():
            dg_ref[...] += dg
            loss_ref[...] += part

    row = pl.BlockSpec((tm, D_MODEL), lambda i: (i, 0))
    vec = pl.BlockSpec((1, D_MODEL), lambda i: (0, 0))
    one = pl.BlockSpec((1, 1), lambda i: (0, 0))
    return _call("final_loss", body, grid=(S // tm,), in_specs=[row, vec, row], out_specs=[one, row, row, vec],
                 out_shape=[jax.ShapeDtypeStruct((1, 1), F32), jax.ShapeDtypeStruct((S, D_MODEL), F32),
                            jax.ShapeDtypeStruct((S, D_MODEL), BF16), jax.ShapeDtypeStruct((1, D_MODEL), F32)],
                 sem=("arbitrary",))(x, g, target)


NN = (((1,), (0,)), ((), ()))
NT = (((1,), (1,)), ((), ()))
TN = (((0,), (0,)), ((), ()))


def _matmul(name, operands, in_specs, out_shape, out_specs, grid, pairs, dims, acc_shape, epilogue):
    n_in, n_out, nk = len(operands), len(out_shape), grid[-1]

    def body(*refs):
        ins, outs = refs[:n_in], refs[n_in:n_in + n_out]
        part = None
        for ia, ib in pairs:
            d = lax.dot_general(ins[ia][...], ins[ib][...], dims, preferred_element_type=F32)
            part = d if part is None else part + d
        if nk == 1:
            epilogue(part, ins, outs)
        else:
            acc = refs[-1]
            k = pl.program_id(len(grid) - 1)

            @pl.when(k == 0)
            def _():
                acc[...] = part

            @pl.when(k > 0)
            def _():
                acc[...] += part

            @pl.when(k == nk - 1)
            def _():
                epilogue(acc[...], ins, outs)

    scratch = [pltpu.VMEM(acc_shape, F32)] if nk > 1 else []
    sem = ("parallel",) * (len(grid) - 1) + ("arbitrary",)
    return _call(name, body, grid=grid, in_specs=in_specs, out_specs=out_specs, out_shape=out_shape,
                 scratch=scratch, sem=sem)(*operands)


def _store(dtype):
    def epilogue(acc, ins, outs):
        outs[0][...] = acc.astype(dtype)
    return epilogue


def mm_nn_cols(a, w):
    S, K = a.shape
    J, _, Ns = w.shape
    tm = min(512, S)
    return _matmul("mm_nn_cols", (a, w),
                   [pl.BlockSpec((tm, K), lambda j, i, k: (i, 0)), pl.BlockSpec((None, K, Ns), lambda j, i, k: (j, 0, 0))],
                   [jax.ShapeDtypeStruct((S, J * Ns), BF16)], [pl.BlockSpec((tm, Ns), lambda j, i, k: (i, j))],
                   (J, S // tm, 1), [(0, 1)], NN, None, _store(BF16))[0]


def ffn_up(h, wg, wu):
    S, K = h.shape
    J, _, Ns = wg.shape
    tm = min(512, S)

    def body(h_ref, wg_ref, wu_ref, g_ref, u_ref, a_ref):
        hv = h_ref[...]
        gv = jnp.dot(hv, wg_ref[...], preferred_element_type=F32)
        uv = jnp.dot(hv, wu_ref[...], preferred_element_type=F32)
        g_ref[...] = gv.astype(BF16)
        u_ref[...] = uv.astype(BF16)
        a_ref[...] = (gv * _sigmoid(gv) * uv).astype(BF16)

    wspec = pl.BlockSpec((None, K, Ns), lambda j, i: (j, 0, 0))
    ospec = pl.BlockSpec((tm, Ns), lambda j, i: (i, j))
    oshape = jax.ShapeDtypeStruct((S, J * Ns), BF16)
    return _call("ffn_up", body, grid=(J, S // tm), in_specs=[pl.BlockSpec((tm, K), lambda j, i: (i, 0)), wspec, wspec],
                 out_specs=[ospec, ospec, ospec], out_shape=[oshape, oshape, oshape], sem=("parallel", "parallel"))(h, wg, wu)


def mm_nn_rows_res(a, w, res):
    S, K = a.shape
    N = w.shape[1]
    tm, tk = min(1024, S), 512

    def epilogue(acc, ins, outs):
        outs[0][...] = acc + ins[2][...]

    return _matmul("mm_nn_rows_res", (a, w, res),
                   [pl.BlockSpec((tm, tk), lambda i, k: (i, k)), pl.BlockSpec((tk, N), lambda i, k: (k, 0)),
                    pl.BlockSpec((tm, N), lambda i, k: (i, 0))],
                   [jax.ShapeDtypeStruct((S, N), F32)], [pl.BlockSpec((tm, N), lambda i, k: (i, 0))],
                   (S // tm, K // tk), [(0, 1)], NN, (tm, N), epilogue)[0]


def mm_nt_cols(pairs_in, out_dtype):
    dz0, w0 = pairs_in[0]
    S = dz0.shape[0]
    J, K, Ns = w0.shape
    tm = min(512, S)
    operands, specs, pairs = [], [], []
    for dz, w in pairs_in:
        pairs.append((len(operands), len(operands) + 1))
        operands += [dz, w]
        specs += [pl.BlockSpec((tm, Ns), lambda i, j: (i, j)), pl.BlockSpec((None, K, Ns), lambda i, j: (j, 0, 0))]
    return _matmul("mm_nt_cols%d" % len(pairs_in), tuple(operands), specs,
                   [jax.ShapeDtypeStruct((S, K), out_dtype)], [pl.BlockSpec((tm, K), lambda i, j: (i, 0))],
                   (S // tm, J), pairs, NT, (tm, K), _store(out_dtype))[0]


def mm_nt_rows(dy, w):
    S, N = dy.shape
    K = w.shape[0]
    tm, tko = min(1024, S), 512
    return _matmul("mm_nt_rows", (dy, w),
                   [pl.BlockSpec((tm, N), lambda i, kk, z: (i, 0)), pl.BlockSpec((tko, N), lambda i, kk, z: (kk, 0))],
                   [jax.ShapeDtypeStruct((S, K), BF16)], [pl.BlockSpec((tm, tko), lambda i, kk, z: (i, kk))],
                   (S // tm, K // tko, 1), [(0, 1)], NT, None, _store(BF16))[0]


def ffn_down_bwd(dy, w, gate, up):
    S, N = dy.shape
    K = w.shape[0]
    tm, tko = min(1024, S), 512

    def epilogue(acc, ins, outs):
        gv = ins[2][...].astype(F32)
        uv = ins[3][...].astype(F32)
        sg = _sigmoid(gv)
        outs[0][...] = (acc * uv * sg * (1.0 + gv * (1.0 - sg))).astype(BF16)
        outs[1][...] = (acc * gv * sg).astype(BF16)

    tile = pl.BlockSpec((tm, tko), lambda i, kk, z: (i, kk))
    oshape = jax.ShapeDtypeStruct((S, K), BF16)
    return _matmul("ffn_down_bwd", (dy, w, gate, up),
                   [pl.BlockSpec((tm, N), lambda i, kk, z: (i, 0)), pl.BlockSpec((tko, N), lambda i, kk, z: (kk, 0)), tile, tile],
                   [oshape, oshape], [tile, tile], (S // tm, K // tko, 1), [(0, 1)], NT, None, epilogue)


def mm_tn_cols(a, dz, J):
    S, M = a.shape
    Ns = dz.shape[1] // J
    tm, tk = min(1024, M), min(512, S)
    return _matmul("mm_tn_cols", (a, dz),
                   [pl.BlockSpec((tk, tm), lambda j, m, k: (k, m)), pl.BlockSpec((tk, Ns), lambda j, m, k: (k, j))],
                   [jax.ShapeDtypeStruct((J, M, Ns), BF16)], [pl.BlockSpec((None, tm, Ns), lambda j, m, k: (j, m, 0))],
                   (J, M // tm, S // tk), [(0, 1)], TN, (tm, Ns), _store(BF16))[0]


def mm_tn_rows(a, dy):
    S, K = a.shape
    N = dy.shape[1]
    tm, tk = 512, min(512, S)
    return _matmul("mm_tn_rows", (a, dy),
                   [pl.BlockSpec((tk, tm), lambda m, k: (k, m)), pl.BlockSpec((tk, N), lambda m, k: (k, 0))],
                   [jax.ShapeDtypeStruct((K, N), BF16)], [pl.BlockSpec((tm, N), lambda m, k: (m, 0))],
                   (K // tm, S // tk), [(0, 1)], TN, (tm, N), _store(BF16))[0]


def rope_tables(S):
    half = ROT_DIM // 2
    pos = jnp.arange(S, dtype=F32)
    inv = ROPE_THETA ** (-jnp.arange(0, ROT_DIM, 2, dtype=F32) / ROT_DIM)
    ang = pos[:, None] * inv[None, :]
    cos, sin = jnp.cos(ang), jnp.sin(ang)
    zeros = jnp.zeros((S, HEAD_DIM - ROT_DIM), F32)
    c = jnp.concatenate([cos, cos, jnp.ones((S, HEAD_DIM - ROT_DIM), F32)], axis=1)
    s_lo = jnp.concatenate([-sin, jnp.zeros((S, half), F32), zeros], axis=1)
    s_hi = jnp.concatenate([jnp.zeros((S, half), F32), sin, zeros], axis=1)
    return c, s_lo, s_hi


def _rope(t, c, s_lo, s_hi):
    half = ROT_DIM // 2
    return t * c + pltpu.roll(t, HEAD_DIM - half, 1) * s_lo + pltpu.roll(t, half, 1) * s_hi


def _unrope(d, c, s_lo, s_hi):
    half = ROT_DIM // 2
    return d * c + pltpu.roll(d * s_lo, half, 1) + pltpu.roll(d * s_hi, HEAD_DIM - half, 1)


def rope_fwd(z, tabs):
    S = z.shape[0]
    nb = S // CHUNK

    def body(q_ref, kv_ref, c_ref, sl_ref, sh_ref, qr_ref, kp_ref, vp_ref):
        i = pl.program_id(0)

        @pl.when(i == 0)
        def _():
            zero = jnp.zeros((CHUNK, KV_WIDTH), BF16)
            kp_ref[0:CHUNK, :] = zero
            vp_ref[0:CHUNK, :] = zero
            kp_ref[S + CHUNK:S + 2 * CHUNK, :] = zero
            vp_ref[S + CHUNK:S + 2 * CHUNK, :] = zero

        c, sl, sh = c_ref[...], sl_ref[...], sh_ref[...]
        for h in range(N_Q_HEADS):
            cols = slice(h * HEAD_DIM, (h + 1) * HEAD_DIM)
            qr_ref[:, cols] = _rope(q_ref[:, cols].astype(F32), c, sl, sh).astype(BF16)
        rows = pl.ds(pl.multiple_of(CHUNK + i * CHUNK, CHUNK), CHUNK)
        for g in range(N_KV_HEADS):
            cols = slice(g * HEAD_DIM, (g + 1) * HEAD_DIM)
            kp_ref[rows, cols] = _rope(kv_ref[:, cols].astype(F32), c, sl, sh).astype(BF16)
        vp_ref[rows, :] = kv_ref[:, KV_WIDTH:2 * KV_WIDTH]

    tab = pl.BlockSpec((CHUNK, HEAD_DIM), lambda i: (i, 0))
    pad = pl.BlockSpec((S + 2 * CHUNK, KV_WIDTH), lambda i: (0, 0))
    return _call("rope_fwd", body, grid=(nb,),
                 in_specs=[pl.BlockSpec((CHUNK, ATTN_WIDTH), lambda i: (i, 0)),
                           pl.BlockSpec((CHUNK, 2 * KV_WIDTH), lambda i: (i, OFF_K // (2 * KV_WIDTH))), tab, tab, tab],
                 out_specs=[pl.BlockSpec((CHUNK, ATTN_WIDTH), lambda i: (i, 0)), pad, pad],
                 out_shape=[jax.ShapeDtypeStruct((S, ATTN_WIDTH), BF16), jax.ShapeDtypeStruct((S + 2 * CHUNK, KV_WIDTH), BF16),
                            jax.ShapeDtypeStruct((S + 2 * CHUNK, KV_WIDTH), BF16)], sem=("arbitrary",))(z, z, *tabs)


def rope_bwd(dq, dkp, dvp, tabs):
    S = dq.shape[0]

    def body(dq_ref, dk_ref, dv_ref, c_ref, sl_ref, sh_ref, o_ref):
        c, sl, sh = c_ref[...], sl_ref[...], sh_ref[...]
        for h in range(N_Q_HEADS):
            cols = slice(h * HEAD_DIM, (h + 1) * HEAD_DIM)
            o_ref[:, cols] = _unrope(dq_ref[:, cols], c, sl, sh).astype(BF16)
        for g in range(N_KV_HEADS):
            cols = slice(g * HEAD_DIM, (g + 1) * HEAD_DIM)
            o_ref[:, OFF_K + g * HEAD_DIM:OFF_K + (g + 1) * HEAD_DIM] = _unrope(dk_ref[:, cols], c, sl, sh).astype(BF16)
        o_ref[:, OFF_V:OFF_V + KV_WIDTH] = dv_ref[...].astype(BF16)

    tab = pl.BlockSpec((CHUNK, HEAD_DIM), lambda i: (i, 0))
    pad = pl.BlockSpec((CHUNK, KV_WIDTH), lambda i: (i + 1, 0))
    return _call("rope_bwd", body, grid=(S // CHUNK,),
                 in_specs=[pl.BlockSpec((CHUNK, ATTN_WIDTH), lambda i: (i, 0)), pad, pad, tab, tab, tab],
                 out_specs=pl.BlockSpec((CHUNK, OFF_CA), lambda i: (i, 0)),
                 out_shape=jax.ShapeDtypeStruct((S, OFF_CA), BF16), sem=("parallel",))(dq, dkp, dvp, *tabs)


def _attn_probs(q, kb, sk, n, S):
    scale = 1.0 / math.sqrt(HEAD_DIM)
    s = lax.dot_general(q, kb, NT, preferred_element_type=F32) * scale
    row = lax.broadcasted_iota(jnp.int32, (CHUNK, 3 * CHUNK), 0)
    col = lax.broadcasted_iota(jnp.int32, (CHUNK, 3 * CHUNK), 1)
    kpos = (n - 1) * CHUNK + col
    valid = (jnp.abs(col - CHUNK - row) <= WINDOW) & (kpos >= 0) & (kpos < S)
    s = jnp.where(valid, s, jnp.finfo(F32).min)
    m = jnp.maximum(jnp.max(s, axis=1, keepdims=True), sk)
    e = jnp.exp(s - m)
    es = jnp.exp(sk - m)
    inv = 1.0 / (jnp.sum(e, axis=1, keepdims=True) + es)
    return e * inv, es * inv


def attn_fwd(qr, kp, vp, sink3):
    S = qr.shape[0]
    tq = min(512, S)
    gw = Q_PER_KV * HEAD_DIM

    def body(q_ref, k_ref, v_ref, s_ref, o_ref):
        i = pl.program_id(1)
        for b in range(tq // CHUNK):
            n = i * (tq // CHUNK) + b
            win = pl.ds(pl.multiple_of(n * CHUNK, CHUNK), 3 * CHUNK)
            kb, vb = k_ref[win, :], v_ref[win, :]
            for r in range(Q_PER_KV):
                rows, cols = slice(b * CHUNK, (b + 1) * CHUNK), slice(r * HEAD_DIM, (r + 1) * HEAD_DIM)
                p, _ = _attn_probs(q_ref[rows, cols], kb, s_ref[r:r + 1, 0:1], n, S)
                o_ref[rows, cols] = jnp.dot(p.astype(BF16), vb, preferred_element_type=F32).astype(BF16)

    kv = pl.BlockSpec((S + 2 * CHUNK, HEAD_DIM), lambda g, i: (0, g))
    return _call("attn_fwd", body, grid=(N_KV_HEADS, S // tq),
                 in_specs=[pl.BlockSpec((tq, gw), lambda g, i: (i, g)), kv, kv,
                           pl.BlockSpec((None, Q_PER_KV, LANES), lambda g, i: (g, 0, 0))],
                 out_specs=pl.BlockSpec((tq, gw), lambda g, i: (i, g)),
                 out_shape=jax.ShapeDtypeStruct((S, ATTN_WIDTH), BF16), sem=("parallel", "arbitrary"))(qr, kp, vp, sink3)


def attn_bwd(qr, kp, vp, sink3, dmix):
    S = qr.shape[0]
    tq = min(512, S)
    gw = Q_PER_KV * HEAD_DIM
    scale = 1.0 / math.sqrt(HEAD_DIM)

    def body(q_ref, k_ref, v_ref, s_ref, do_ref, dq_ref, dk_ref, dv_ref, ds_ref):
        i = pl.program_id(1)

        @pl.when(i == 0)
        def _():
            dk_ref[...] = jnp.zeros_like(dk_ref)
            dv_ref[...] = jnp.zeros_like(dv_ref)
            ds_ref[...] = jnp.zeros_like(ds_ref)

        for b in range(tq // CHUNK):
            n = i * (tq // CHUNK) + b
            win = pl.ds(pl.multiple_of(n * CHUNK, CHUNK), 3 * CHUNK)
            kb, vb = k_ref[win, :], v_ref[win, :]
            dk_acc = jnp.zeros((3 * CHUNK, HEAD_DIM), F32)
            dv_acc = jnp.zeros((3 * CHUNK, HEAD_DIM), F32)
            for r in range(Q_PER_KV):
                rows, cols = slice(b * CHUNK, (b + 1) * CHUNK), slice(r * HEAD_DIM, (r + 1) * HEAD_DIM)
                q = q_ref[rows, cols]
                do = do_ref[rows, cols]
                p, p_sink = _attn_probs(q, kb, s_ref[r:r + 1, 0:1], n, S)
                dp = lax.dot_general(do, vb, NT, preferred_element_type=F32)
                delta = jnp.sum(p * dp, axis=1, keepdims=True)
                dsc = (p * (dp - delta) * scale).astype(BF16)
                dq_ref[rows, cols] = jnp.dot(dsc, kb, preferred_element_type=F32)
                dk_acc = dk_acc + lax.dot_general(dsc, q, TN, preferred_element_type=F32)
                dv_acc = dv_acc + lax.dot_general(p.astype(BF16), do, TN, preferred_element_type=F32)
                dsink = jnp.sum(-p_sink * delta, axis=0, keepdims=True)
                ds_ref[r:r + 1, :] += jnp.broadcast_to(dsink, (1, LANES))
            dk_ref[win, :] += dk_acc
            dv_ref[win, :] += dv_acc

    kv = pl.BlockSpec((S + 2 * CHUNK, HEAD_DIM), lambda g, i: (0, g))
    qspec = pl.BlockSpec((tq, gw), lambda g, i: (i, g))
    sspec = pl.BlockSpec((None, Q_PER_KV, LANES), lambda g, i: (g, 0, 0))
    padshape = jax.ShapeDtypeStruct((S + 2 * CHUNK, KV_WIDTH), F32)
    return _call("attn_bwd", body, grid=(N_KV_HEADS, S // tq),
                 in_specs=[qspec, kv, kv, sspec, qspec],
                 out_specs=[qspec, kv, kv, sspec],
                 out_shape=[jax.ShapeDtypeStruct((S, ATTN_WIDTH), F32), padshape, padshape,
                            jax.ShapeDtypeStruct((N_KV_HEADS, Q_PER_KV, LANES), F32)],
                 sem=("parallel", "arbitrary"))(qr, kp, vp, sink3, dmix)


CONV_TILE = 256


def _fill_padded(dst_ref, value, S):
    zero = jnp.zeros((CONV_PAD, LANES), F32)
    dst_ref[0:CONV_PAD, :] = zero
    dst_ref[CONV_PAD + S:2 * CONV_PAD + S, :] = zero
    dst_ref[CONV_PAD:CONV_PAD + S, :] = value


def conv_dw_fwd(z, w32, b):
    S = z.shape[0]
    T = min(CONV_TILE, S)
    lo = CONV_PAD - (CONV_KERNEL - 1) // 2

    def body(a_ref, g_ref, w_ref, b_ref, o_ref, c0_ref):
        _fill_padded(c0_ref, a_ref[...].astype(F32) * _sigmoid(g_ref[...].astype(F32)), S)

        def tile(t, carry):
            base = pl.multiple_of(t * T, T)
            acc = jnp.broadcast_to(b_ref[...], (T, LANES))
            for j in range(CONV_KERNEL):
                acc = acc + w_ref[j:j + 1, :] * c0_ref[pl.ds(base + lo + j, T), :]
            o_ref[pl.ds(base, T), :] = acc
            return carry

        lax.fori_loop(0, S // T, tile, 0)

    nca, ncg = OFF_CA // LANES, OFF_CG // LANES
    return _call("conv_dw_fwd", body, grid=(CONV_WIDTH // LANES,),
                 in_specs=[pl.BlockSpec((S, LANES), lambda cb: (0, nca + cb)), pl.BlockSpec((S, LANES), lambda cb: (0, ncg + cb)),
                           pl.BlockSpec((32, LANES), lambda cb: (0, cb)), pl.BlockSpec((1, LANES), lambda cb: (0, cb))],
                 out_specs=pl.BlockSpec((S, LANES), lambda cb: (0, cb)),
                 out_shape=jax.ShapeDtypeStruct((S, CONV_WIDTH), F32),
                 scratch=[pltpu.VMEM((S + 2 * CONV_PAD, LANES), F32)], sem=("parallel",))(z, z, w32, b)


def _ln_stats(x):
    mu = jnp.mean(x, axis=-1, keepdims=True)
    xc = x - mu
    rs = lax.rsqrt(jnp.mean(xc * xc, axis=-1, keepdims=True) + EPS)
    return xc * rs, rs


def _ln_bwd(dy, xh, rs, g):
    dxh = dy * g
    return rs * (dxh - jnp.mean(dxh, axis=-1, keepdims=True) - xh * jnp.mean(dxh * xh, axis=-1, keepdims=True))


def conv_ln_fwd(c1, g, b):
    S = c1.shape[0]
    T = min(512, S)

    def body(x_ref, g_ref, b_ref, o_ref):
        xh, _ = _ln_stats(x_ref[...])
        y = xh * g_ref[...] + b_ref[...]
        o_ref[...] = (y * _sigmoid(y)).astype(BF16)

    row = pl.BlockSpec((T, CONV_WIDTH), lambda i: (i, 0))
    vec = pl.BlockSpec((1, CONV_WIDTH), lambda i: (0, 0))
    return _call("conv_ln_fwd", body, grid=(S // T,), in_specs=[row, vec, vec], out_specs=row,
                 out_shape=jax.ShapeDtypeStruct((S, CONV_WIDTH), BF16), sem=("parallel",))(c1, g, b)


def _acc_out(ref, value):
    @pl.when(pl.program_id(0) == 0)
    def _():
        ref[...] = value

    @pl.when(pl.program_id(0) > 0)
    def _():
        ref[...] += value


def conv_ln_bwd(dmix, c1, g, b):
    S = c1.shape[0]
    T = min(512, S)

    def body(d_ref, x_ref, g_ref, b_ref, dx_ref, dg_ref, db_ref):
        xh, rs = _ln_stats(x_ref[...])
        gv = g_ref[...]
        y = xh * gv + b_ref[...]
        sg = _sigmoid(y)
        dy = d_ref[...].astype(F32) * sg * (1.0 + y * (1.0 - sg))
        dx_ref[...] = _ln_bwd(dy, xh, rs, gv)
        _acc_out(dg_ref, jnp.sum(dy * xh, axis=0, keepdims=True))
        _acc_out(db_ref, jnp.sum(dy, axis=0, keepdims=True))

    row = pl.BlockSpec((T, CONV_WIDTH), lambda i: (i, 0))
    vec = pl.BlockSpec((1, CONV_WIDTH), lambda i: (0, 0))
    vshape = jax.ShapeDtypeStruct((1, CONV_WIDTH), F32)
    return _call("conv_ln_bwd", body, grid=(S // T,),
                 in_specs=[pl.BlockSpec((T, CONV_WIDTH), lambda i: (i, ATTN_WIDTH // CONV_WIDTH)), row, vec, vec],
                 out_specs=[row, vec, vec], out_shape=[jax.ShapeDtypeStruct((S, CONV_WIDTH), F32), vshape, vshape],
                 sem=("arbitrary",))(dmix, c1, g, b)


def conv_dw_bwd(dc1, z, w32):
    S = z.shape[0]
    T = min(CONV_TILE, S)
    half = (CONV_KERNEL - 1) // 2
    lo = CONV_PAD - half

    def body(d_ref, a_ref, g_ref, w_ref, da_ref, dg_ref, dw_ref, db_ref, c0_ref, d1_ref, wacc_ref):
        av = a_ref[...].astype(F32)
        sg = _sigmoid(g_ref[...].astype(F32))
        _fill_padded(c0_ref, av * sg, S)
        _fill_padded(d1_ref, d_ref[...], S)
        wacc_ref[...] = jnp.zeros_like(wacc_ref)

        def tile(t, carry):
            base = pl.multiple_of(t * T, T)
            d1 = d_ref[pl.ds(base, T), :]
            acc = jnp.zeros((T, LANES), F32)
            for j in range(CONV_KERNEL):
                acc = acc + w_ref[j:j + 1, :] * d1_ref[pl.ds(base + CONV_PAD + half - j, T), :]
                prod = d1 * c0_ref[pl.ds(base + lo + j, T), :]
                wacc_ref[j] += jnp.sum(prod.reshape(T // 8, 8, LANES), axis=0)
            rows = pl.ds(base, T)
            a_t = a_ref[rows, :].astype(F32)
            s_t = _sigmoid(g_ref[rows, :].astype(F32))
            da_ref[rows, :] = (acc * s_t).astype(BF16)
            dg_ref[rows, :] = (acc * a_t * s_t * (1.0 - s_t)).astype(BF16)
            return carry

        lax.fori_loop(0, S // T, tile, 0)
        dw_ref[...] = jnp.sum(wacc_ref[...], axis=1)
        db_ref[...] = jnp.sum(d_ref[...], axis=0, keepdims=True)

    nca, ncg = OFF_CA // LANES, OFF_CG // LANES
    col = pl.BlockSpec((S, LANES), lambda cb: (0, cb))
    oshape = jax.ShapeDtypeStruct((S, CONV_WIDTH), BF16)
    return _call("conv_dw_bwd", body, grid=(CONV_WIDTH // LANES,),
                 in_specs=[col, pl.BlockSpec((S, LANES), lambda cb: (0, nca + cb)), pl.BlockSpec((S, LANES), lambda cb: (0, ncg + cb)),
                           pl.BlockSpec((32, LANES), lambda cb: (0, cb))],
                 out_specs=[col, col, pl.BlockSpec((32, LANES), lambda cb: (0, cb)), pl.BlockSpec((1, LANES), lambda cb: (0, cb))],
                 out_shape=[oshape, oshape, jax.ShapeDtypeStruct((32, CONV_WIDTH), F32), jax.ShapeDtypeStruct((1, CONV_WIDTH), F32)],
                 scratch=[pltpu.VMEM((S + 2 * CONV_PAD, LANES), F32), pltpu.VMEM((S + 2 * CONV_PAD, LANES), F32),
                          pltpu.VMEM((32, 8, LANES), F32)], sem=("parallel",))(dc1, z, z, w32)


_INV_SQRT2 = 1.0 / math.sqrt(2.0)
_INV_SQRT2PI = 1.0 / math.sqrt(2.0 * math.pi)


def _gelu(x):
    return 0.5 * x * (1.0 + lax.erf(x * _INV_SQRT2))


def _gelu_grad(x):
    return 0.5 * (1.0 + lax.erf(x * _INV_SQRT2)) + x * jnp.exp(-0.5 * x * x) * _INV_SQRT2PI


def sgu_fwd(z, g, b, ws, bs):
    S = z.shape[0]
    T = min(512, S)

    def body(u_ref, v_ref, g_ref, b_ref, ws_ref, bs_ref, o_ref):
        xh, _ = _ln_stats(_gelu(v_ref[...].astype(F32)))
        vn = (xh * g_ref[...] + b_ref[...]).astype(BF16)
        for ch in range(T // CHUNK):
            rows = slice(ch * CHUNK, (ch + 1) * CHUNK)
            for h in range(SGU_HEADS):
                cols = slice(h * HEAD_DIM, (h + 1) * HEAD_DIM)
                sp = jnp.dot(ws_ref[h], vn[rows, cols], preferred_element_type=F32) + bs_ref[h]
                o_ref[rows, cols] = (_gelu(u_ref[rows, cols].astype(F32)) * sp).astype(BF16)

    vec = pl.BlockSpec((1, SGU_WIDTH), lambda i: (0, 0))
    full = pl.BlockSpec((SGU_HEADS, CHUNK, CHUNK), lambda i: (0, 0, 0))
    return _call("sgu_fwd", body, grid=(S // T,),
                 in_specs=[pl.BlockSpec((T, SGU_WIDTH), lambda i: (i, OFF_U // SGU_WIDTH)),
                           pl.BlockSpec((T, SGU_WIDTH), lambda i: (i, OFF_VV // SGU_WIDTH)), vec, vec, full, full],
                 out_specs=pl.BlockSpec((T, SGU_WIDTH), lambda i: (i, 0)),
                 out_shape=jax.ShapeDtypeStruct((S, SGU_WIDTH), BF16), sem=("parallel",))(z, z, g, b, ws, bs)


def sgu_bwd(z, dmix, g, b, ws, bs):
    S = z.shape[0]
    T = min(512, S)

    def body(u_ref, v_ref, d_ref, g_ref, b_ref, ws_ref, bs_ref, du_ref, dv_ref, dws_ref, dbs_ref, dg_ref, db_ref, dvn_ref):
        @pl.when(pl.program_id(0) == 0)
        def _():
            dws_ref[...] = jnp.zeros_like(dws_ref)
            dbs_ref[...] = jnp.zeros_like(dbs_ref)

        vraw = v_ref[...].astype(F32)
        xh, rs = _ln_stats(_gelu(vraw))
        gv = g_ref[...]
        vn = (xh * gv + b_ref[...]).astype(BF16)
        for ch in range(T // CHUNK):
            rows = slice(ch * CHUNK, (ch + 1) * CHUNK)
            for h in range(SGU_HEADS):
                cols = slice(h * HEAD_DIM, (h + 1) * HEAD_DIM)
                w = ws_ref[h]
                vb = vn[rows, cols]
                sp = jnp.dot(w, vb, preferred_element_type=F32) + bs_ref[h]
                uraw = u_ref[rows, cols].astype(F32)
                dout = d_ref[rows, cols].astype(F32)
                du_ref[rows, cols] = (dout * sp * _gelu_grad(uraw)).astype(BF16)
                dsp = dout * _gelu(uraw)
                dspb = dsp.astype(BF16)
                dvn_ref[rows, cols] = lax.dot_general(w, dspb, TN, preferred_element_type=F32)
                dws_ref[h] += lax.dot_general(dspb, vb, NT, preferred_element_type=F32)
                dbs_ref[h] += jnp.sum(dsp, axis=1, keepdims=True)
        dvn = dvn_ref[...]
        dv_ref[...] = (_ln_bwd(dvn, xh, rs, gv) * _gelu_grad(vraw)).astype(BF16)
        _acc_out(dg_ref, jnp.sum(dvn * xh, axis=0, keepdims=True))
        _acc_out(db_ref, jnp.sum(dvn, axis=0, keepdims=True))

    vec = pl.BlockSpec((1, SGU_WIDTH), lambda i: (0, 0))
    full = pl.BlockSpec((SGU_HEADS, CHUNK, CHUNK), lambda i: (0, 0, 0))
    row = pl.BlockSpec((T, SGU_WIDTH), lambda i: (i, 0))
    oshape = jax.ShapeDtypeStruct((S, SGU_WIDTH), BF16)
    vshape = jax.ShapeDtypeStruct((1, SGU_WIDTH), F32)
    return _call("sgu_bwd", body, grid=(S // T,),
                 in_specs=[pl.BlockSpec((T, SGU_WIDTH), lambda i: (i, OFF_U // SGU_WIDTH)),
                           pl.BlockSpec((T, SGU_WIDTH), lambda i: (i, OFF_VV // SGU_WIDTH)),
                           pl.BlockSpec((T, SGU_WIDTH), lambda i: (i, (ATTN_WIDTH + CONV_WIDTH) // SGU_WIDTH)), vec, vec, full, full],
                 out_specs=[row, row, full, pl.BlockSpec((SGU_HEADS, CHUNK, 1), lambda i: (0, 0, 0)), vec, vec],
                 out_shape=[oshape, oshape, jax.ShapeDtypeStruct((SGU_HEADS, CHUNK, CHUNK), F32),
                            jax.ShapeDtypeStruct((SGU_HEADS, CHUNK, 1), F32), vshape, vshape],
                 scratch=[pltpu.VMEM((T, SGU_WIDTH), F32)], sem=("arbitrary",))(z, z, dmix, g, b, ws, bs)


def _row_tile(rows, cols, n_arrays):
    budget = (24 * 1024 * 1024) // (n_arrays * 2 * 4 * cols)
    t = min(rows, max(16, budget // 16 * 16))
    while rows % t:
        t -= 16
    return t


def add_sibling_half(grad, recv, c_idx):
    J, R, C = grad.shape
    hr = R // 2
    tr = _row_tile(hr, C, 3)
    nb = hr // tr

    def body(c_ref, g_ref, r_ref, o_ref):
        o_ref[...] = (g_ref[...].astype(F32) + r_ref[...].astype(F32)).astype(BF16)

    grid_spec = pltpu.PrefetchScalarGridSpec(
        num_scalar_prefetch=1, grid=(J, nb),
        in_specs=[pl.BlockSpec((None, tr, C), lambda j, i, c: (j, c[0] * nb + i, 0)),
                  pl.BlockSpec((None, tr, C), lambda j, i, c: (j, i, 0))],
        out_specs=pl.BlockSpec((None, tr, C), lambda j, i, c: (j, i, 0)))
    return pl.pallas_call(body, name="add_sibling_half", grid_spec=grid_spec,
                          out_shape=jax.ShapeDtypeStruct((J, hr, C), BF16),
                          compiler_params=pltpu.CompilerParams(vmem_limit_bytes=VMEM_LIMIT,
                                                               dimension_semantics=("parallel", "parallel")))(c_idx, grad, recv)


def sum_chips(parts):
    J, R, C = parts.shape
    tr = _row_tile(R, C, 4)

    def body(p_ref, o_ref):
        acc = p_ref[0].astype(F32)
        for j in range(1, J):
            acc = acc + p_ref[j].astype(F32)
        o_ref[...] = acc

    return _call("sum_chips", body, grid=(R // tr,), in_specs=[pl.BlockSpec((J, tr, C), lambda i: (0, i, 0))],
                 out_specs=pl.BlockSpec((tr, C), lambda i: (i, 0)), out_shape=jax.ShapeDtypeStruct((R, C), F32),
                 sem=("parallel",))(parts)


def adamw(w, g, m, v):
    R, C = w.shape
    tr = _row_tile(R, C, 7)

    def body(w_ref, g_ref, m_ref, v_ref, d_ref, nm_ref, nv_ref):
        gv = g_ref[...]
        nm = ADAM_B1 * m_ref[...] + (1.0 - ADAM_B1) * gv
        nv = ADAM_B2 * v_ref[...] + (1.0 - ADAM_B2) * (gv * gv)
        m_hat = nm / (1.0 - ADAM_B1 ** ADAM_STEP)
        v_hat = nv / (1.0 - ADAM_B2 ** ADAM_STEP)
        d_ref[...] = -ADAM_LR * (m_hat / (jnp.sqrt(v_hat) + ADAM_EPS) + ADAM_WD * w_ref[...])
        nm_ref[...] = nm
        nv_ref[...] = nv

    spec = pl.BlockSpec((tr, C), lambda i: (i, 0))
    shape = jax.ShapeDtypeStruct((R, C), F32)
    return _call("adamw", body, grid=(R // tr,), in_specs=[spec] * 4, out_specs=[spec] * 3, out_shape=[shape] * 3,
                 sem=("parallel",))(w, g, m, v)


def _place():
    x, y, c = lax.axis_index("x"), lax.axis_index("y"), lax.axis_index("c")
    chips = [(1 - x, y), (x, 1 - y), (1 - x, 1 - y)]
    return x, y, c, chips


def _remote(src, dst, send_sem, recv_sem, dev):
    return pltpu.make_async_remote_copy(src_ref=src, dst_ref=dst, send_sem=send_sem, recv_sem=recv_sem,
                                        device_id=dev, device_id_type=MESH)


def gather_weights(shards):
    n = len(shards)

    def body(*refs):
        ins, outs = refs[:n], refs[n:2 * n]
        send_sems, recv_sems, local_sems = refs[2 * n:]
        x, y, c, chips = _place()
        me = 2 * x + y
        sibling = (x, y, 1 - c)
        started, local = [], []
        for k in range(n):
            hr = ins[k].shape[0] // 2
            mine = pl.ds(pl.multiple_of(c * hr, 8), hr)
            cp = pltpu.make_async_copy(ins[k], outs[k].at[me], local_sems.at[k])
            cp.start()
            local.append(cp)
            for t, (px, py) in enumerate(chips):
                cp = _remote(ins[k].at[mine, :], outs[k].at[me, mine, :], send_sems.at[k, t], recv_sems.at[k, t], (px, py, c))
                cp.start()
                started.append(cp)
        for k in range(n):
            hr = ins[k].shape[0] // 2
            mine = pl.ds(pl.multiple_of(c * hr, 8), hr)
            for t, (px, py) in enumerate(chips):
                landed = outs[k].at[2 * px + py, mine, :]
                _remote(landed, landed, send_sems.at[k, t], recv_sems.at[k, t], (px, py, c)).wait_recv()
                cp = _remote(landed, landed, send_sems.at[k, 3 + t], recv_sems.at[k, 3 + t], sibling)
                cp.start()
                started.append(cp)
        for k in range(n):
            hr = ins[k].shape[0] // 2
            other = pl.ds(pl.multiple_of((1 - c) * hr, 8), hr)
            for t, (px, py) in enumerate(chips):
                landed = outs[k].at[2 * px + py, other, :]
                _remote(landed, landed, send_sems.at[k, 3 + t], recv_sems.at[k, 3 + t], sibling).wait_recv()
        for cp in started:
            cp.wait_send()
        for cp in local:
            cp.wait()

    return pl.pallas_call(
        body, name="gather_weights", in_specs=[HBM] * n, out_specs=[HBM] * n,
        out_shape=[jax.ShapeDtypeStruct((N_CHIPS,) + s.shape, s.dtype) for s in shards],
        scratch_shapes=[pltpu.SemaphoreType.DMA((n, 6)), pltpu.SemaphoreType.DMA((n, 6)), pltpu.SemaphoreType.DMA((n,))],
    )(*shards)


def gather_small(block):
    def body(in_ref, out_ref, send_sems, recv_sems):
        x, y, c, chips = _place()
        me = 2 * x + y
        out_ref[me] = in_ref[...]
        sends = []
        for t, (px, py) in enumerate(chips):
            cp = _remote(in_ref, out_ref.at[me], send_sems.at[t], recv_sems.at[t], (px, py, c))
            cp.start()
            sends.append(cp)
        for t, (px, py) in enumerate(chips):
            landed = out_ref.at[2 * px + py]
            _remote(landed, landed, send_sems.at[t], recv_sems.at[t], (px, py, c)).wait_recv()
        for cp in sends:
            cp.wait_send()

    return pl.pallas_call(
        body, name="gather_small", in_specs=[VMEM_SPEC], out_specs=VMEM_SPEC,
        out_shape=jax.ShapeDtypeStruct((N_CHIPS,) + block.shape, block.dtype),
        scratch_shapes=[pltpu.SemaphoreType.DMA((3,)), pltpu.SemaphoreType.DMA((3,))],
    )(block)


def exchange_sibling_halves(grads):
    n = len(grads)

    def body(*refs):
        ins, outs = refs[:n], refs[n:2 * n]
        send_sems, recv_sems = refs[2 * n:]
        x, y, c, _ = _place()
        copies = []
        for k in range(n):
            hr = ins[k].shape[1] // 2
            theirs = pl.ds(pl.multiple_of((1 - c) * hr, 8), hr)
            cp = _remote(ins[k].at[:, theirs, :], outs[k], send_sems.at[k], recv_sems.at[k], (x, y, 1 - c))
            cp.start()
            copies.append(cp)
        for cp in copies:
            cp.wait()

    return pl.pallas_call(
        body, name="exchange_sibling_halves", in_specs=[HBM] * n, out_specs=[HBM] * n,
        out_shape=[jax.ShapeDtypeStruct((g.shape[0], g.shape[1] // 2, g.shape[2]), g.dtype) for g in grads],
        scratch_shapes=[pltpu.SemaphoreType.DMA((n,)), pltpu.SemaphoreType.DMA((n,))],
    )(*grads)


def exchange_chip_parts(parts):
    n = len(parts)

    def body(*refs):
        ins, outs = refs[:n], refs[n:2 * n]
        send_sems, recv_sems, local_sems = refs[2 * n:]
        x, y, c, chips = _place()
        me = 2 * x + y
        sends, local = [], []
        for k in range(n):
            cp = pltpu.make_async_copy(ins[k].at[me], outs[k].at[me], local_sems.at[k])
            cp.start()
            local.append(cp)
            for t, (px, py) in enumerate(chips):
                cp = _remote(ins[k].at[2 * px + py], outs[k].at[me], send_sems.at[k, t], recv_sems.at[k, t], (px, py, c))
                cp.start()
                sends.append(cp)
        for k in range(n):
            for t, (px, py) in enumerate(chips):
                landed = outs[k].at[2 * px + py]
                _remote(landed, landed, send_sems.at[k, t], recv_sems.at[k, t], (px, py, c)).wait_recv()
        for cp in sends:
            cp.wait_send()
        for cp in local:
            cp.wait()

    return pl.pallas_call(
        body, name="exchange_chip_parts", in_specs=[HBM] * n, out_specs=[HBM] * n,
        out_shape=[jax.ShapeDtypeStruct(p.shape, p.dtype) for p in parts],
        scratch_shapes=[pltpu.SemaphoreType.DMA((n, 3)), pltpu.SemaphoreType.DMA((n, 3)), pltpu.SemaphoreType.DMA((n,))],
    )(*parts)


def share_final_halves(halves, n_kinds):
    n = len(halves)
    n_layers = n // n_kinds

    def body(*refs):
        ins, outs = refs[:n], refs[n:n + n_kinds]
        send_sems, recv_sems, local_sems = refs[n + n_kinds:]
        x, y, c, _ = _place()
        sends, local = [], []
        for i in range(n):
            l, k = divmod(i, n_kinds)
            hr = ins[i].shape[0]
            mine = pl.ds(pl.multiple_of(c * hr, 8), hr)
            cp = pltpu.make_async_copy(ins[i], outs[k].at[l, mine, :], local_sems.at[i])
            cp.start()
            local.append(cp)
            cp = _remote(ins[i], outs[k].at[l, mine, :], send_sems.at[i], recv_sems.at[i], (x, y, 1 - c))
            cp.start()
            sends.append(cp)
        for i in range(n):
            l, k = divmod(i, n_kinds)
            hr = ins[i].shape[0]
            theirs = outs[k].at[l, pl.ds(pl.multiple_of((1 - c) * hr, 8), hr), :]
            _remote(theirs, theirs, send_sems.at[i], recv_sems.at[i], (x, y, 1 - c)).wait_recv()
        for cp in sends:
            cp.wait_send()
        for cp in local:
            cp.wait()

    return pl.pallas_call(
        body, name="share_final_halves", in_specs=[HBM] * n, out_specs=[HBM] * n_kinds,
        out_shape=[jax.ShapeDtypeStruct((n_layers, 2 * halves[k].shape[0], halves[k].shape[1]), F32) for k in range(n_kinds)],
        scratch_shapes=[pltpu.SemaphoreType.DMA((n,)), pltpu.SemaphoreType.DMA((n,)), pltpu.SemaphoreType.DMA((n,))],
    )(*halves)


def allreduce_small(packed):
    R = packed.shape[0]

    def body(x_ref, sum_ref, all_ref, send_sems, recv_sems):
        x, y, c, chips = _place()
        me, sibling = (x, y, c), (x, y, 1 - c)

        def rows(px, py, pc):
            return all_ref.at[4 * px + 2 * py + pc]

        def copy(k, block, to, src=None):
            return _remote(rows(*block) if src is None else src, rows(*block), send_sems.at[k], recv_sems.at[k], to)

        all_ref[4 * x + 2 * y + c] = x_ref[...]
        first = [copy(0, me, sibling, src=x_ref)]
        first += [copy(1 + j, me, (*chip, c), src=x_ref) for j, chip in enumerate(chips)]
        for cp in first:
            cp.start()
        passed = [copy(4 + j, (*chip, c), sibling) for j, chip in enumerate(chips)]
        for j, chip in enumerate(chips):
            copy(1 + j, (*chip, c), me).wait_recv()
            passed[j].start()
        copy(0, sibling, me).wait_recv()
        for j, chip in enumerate(chips):
            copy(4 + j, (*chip, 1 - c), me).wait_recv()
        for cp in first + passed:
            cp.wait_send()

        def chunk(i, carry):
            rws = pl.ds(pl.multiple_of(i * PACK_ROWS, PACK_ROWS), PACK_ROWS)
            acc = all_ref[0, rws, :]
            for d in range(1, N_DEV):
                acc = acc + all_ref[d, rws, :]
            sum_ref[rws, :] = acc
            return carry

        lax.fori_loop(0, R // PACK_ROWS, chunk, 0)

    return pl.pallas_call(
        body, name="allreduce_small", in_specs=[VMEM_SPEC], out_specs=VMEM_SPEC,
        out_shape=jax.ShapeDtypeStruct((R, LANES), F32),
        scratch_shapes=[pltpu.VMEM((N_DEV, R, LANES), F32), pltpu.SemaphoreType.DMA((7,)), pltpu.SemaphoreType.DMA((7,))],
        compiler_params=pltpu.CompilerParams(vmem_limit_bytes=VMEM_LIMIT),
    )(packed)


def _layer_fwd(x, p, tabs):
    h = rms_fwd(x, p["mix_norm_g"])
    z = mm_nn_cols(h, p["w_in"])
    qr, kp, vp = rope_fwd(z, tabs)
    attn = attn_fwd(qr, kp, vp, p["sink3"])
    c1 = conv_dw_fwd(z, p["conv_w32"], p["conv_dw_b"])
    conv = conv_ln_fwd(c1, p["conv_ln_g"], p["conv_ln_b"])
    sgu = sgu_fwd(z, p["sgu_ln_g"], p["sgu_ln_b"], p["sgu_w16"], p["sgu_b3"])
    mix = jnp.concatenate([attn, conv, sgu], axis=1)
    x_mid = mm_nn_rows_res(mix, p["w_out"], x)
    h2 = rms_fwd(x_mid, p["ffn_norm_g"])
    gate, up, act = ffn_up(h2, p["w_gate"], p["w_up"])
    x_out = mm_nn_rows_res(act, p["w_down"], x_mid)
    saved = dict(x=x, h=h, z=z, qr=qr, kp=kp, vp=vp, c1=c1, mix=mix, x_mid=x_mid, h2=h2, gate=gate, up=up, act=act)
    return x_out, saved


def _layer_bwd(dx, dxb, p, s, tabs):
    dgate, dup = ffn_down_bwd(dxb, p["w_down"], s["gate"], s["up"])
    g_down = mm_tn_rows(s["act"], dxb)
    dh2 = mm_nt_cols([(dgate, p["w_gate"]), (dup, p["w_up"])], F32)
    g_gate = mm_tn_cols(s["h2"], dgate, N_CHIPS)
    g_up = mm_tn_cols(s["h2"], dup, N_CHIPS)
    dmid, dmidb, g_ffn_norm = rms_bwd(s["x_mid"], p["ffn_norm_g"], dh2, dx)
    dmix = mm_nt_rows(dmidb, p["w_out"])
    g_out = mm_tn_rows(s["mix"], dmidb)
    dq, dkp, dvp, dsink = attn_bwd(s["qr"], s["kp"], s["vp"], p["sink3"], dmix)
    dqkv = rope_bwd(dq, dkp, dvp, tabs)
    dc1, g_cln_g, g_cln_b = conv_ln_bwd(dmix, s["c1"], p["conv_ln_g"], p["conv_ln_b"])
    dca, dcg, g_cw, g_cb = conv_dw_bwd(dc1, s["z"], p["conv_w32"])
    du, dv, g_sw, g_sb, g_sln_g, g_sln_b = sgu_bwd(s["z"], dmix, p["sgu_ln_g"], p["sgu_ln_b"], p["sgu_w16"], p["sgu_b3"])
    dz = jnp.concatenate([dqkv, dca, dcg, du, dv], axis=1)
    dh = mm_nt_cols([(dz, p["w_in"])], F32)
    g_in = mm_tn_cols(s["h"], dz, N_CHIPS)
    dx_in, dxb_in, g_mix_norm = rms_bwd(s["x"], p["mix_norm_g"], dh, dmid)
    big = [g_in, g_out.reshape(N_CHIPS, -1, D_MODEL), g_gate, g_up, g_down.reshape(N_CHIPS, -1, D_MODEL)]
    small = dict(mix_norm_g=g_mix_norm, sink=dsink[:, :, 0].reshape(1, N_Q_HEADS), conv_dw_w=g_cw[:CONV_KERNEL],
                 conv_dw_b=g_cb, conv_ln_g=g_cln_g, conv_ln_b=g_cln_b, sgu_ln_g=g_sln_g, sgu_ln_b=g_sln_b,
                 sgu_w=g_sw, sgu_b=g_sb[:, :, 0], ffn_norm_g=g_ffn_norm)
    return dx_in, dxb_in, big, small


def _layer_params(l, gathered, conv_w_full, mix_norm_g, sink, conv_dw_b, conv_ln_g, conv_ln_b, sgu_ln_g, sgu_ln_b, sgu_w,
                  sgu_b, ffn_norm_g):
    w_in, w_out, w_gate, w_up, w_down = gathered
    return dict(
        w_in=w_in, w_out=w_out.reshape(-1, D_MODEL), w_gate=w_gate, w_up=w_up, w_down=w_down.reshape(-1, D_MODEL),
        mix_norm_g=mix_norm_g[l:l + 1], ffn_norm_g=ffn_norm_g[l:l + 1],
        sink3=jnp.broadcast_to(sink[l].reshape(N_KV_HEADS, Q_PER_KV, 1), (N_KV_HEADS, Q_PER_KV, LANES)),
        conv_w32=jnp.pad(conv_w_full[l], ((0, 32 - CONV_KERNEL), (0, 0))),
        conv_dw_b=conv_dw_b[l:l + 1], conv_ln_g=conv_ln_g[l:l + 1], conv_ln_b=conv_ln_b[l:l + 1],
        sgu_ln_g=sgu_ln_g[l:l + 1], sgu_ln_b=sgu_ln_b[l:l + 1], sgu_w16=sgu_w[l].astype(BF16),
        sgu_b3=jnp.broadcast_to(sgu_b[l][:, :, None], (SGU_HEADS, CHUNK, CHUNK)))


_SMALL = ["mix_norm_g", "sink", "conv_dw_b", "conv_ln_g", "conv_ln_b", "sgu_ln_g", "sgu_ln_b", "sgu_w", "sgu_b", "ffn_norm_g",
          "final_norm_g"]


def _pack_rows(arrays):
    rows, counts = [], []
    for a in arrays:
        flat = a.reshape(-1)
        n = -(-flat.shape[0] // LANES)
        rows.append(jnp.pad(flat, (0, n * LANES - flat.shape[0])).reshape(n, LANES))
        counts.append(n)
    packed = jnp.concatenate(rows, axis=0)
    pad = -packed.shape[0] % PACK_ROWS
    return jnp.pad(packed, ((0, pad), (0, 0))), counts


def _unpack_rows(packed, counts, shapes):
    out, r = [], 0
    for n, shape in zip(counts, shapes):
        size = math.prod(shape)
        out.append(packed[r:r + n].reshape(-1)[:size].reshape(shape))
        r += n
    return out


def kernel(x, mix_norm_g, w_in, sink, conv_dw_w, conv_dw_b, conv_ln_g, conv_ln_b, sgu_ln_g, sgu_ln_b, sgu_w, sgu_b, w_out, ffn_norm_g, w_gate, w_up, w_down, final_norm_g, loss_target, m_mix_norm_g, m_w_in, m_sink, m_conv_dw_w, m_conv_dw_b, m_conv_ln_g, m_conv_ln_b, m_sgu_ln_g, m_sgu_ln_b, m_sgu_w, m_sgu_b, m_w_out, m_ffn_norm_g, m_w_gate, m_w_up, m_w_down, m_final_norm_g, v_mix_norm_g, v_w_in, v_sink, v_conv_dw_w, v_conv_dw_b, v_conv_ln_g, v_conv_ln_b, v_sgu_ln_g, v_sgu_ln_b, v_sgu_w, v_sgu_b, v_w_out, v_ffn_norm_g, v_w_gate, v_w_up, v_w_down, v_final_norm_g):
    S = x.shape[1]
    my_chip = 2 * lax.axis_index("x") + lax.axis_index("y")
    c_idx = lax.axis_index("c").astype(jnp.int32).reshape(1)
    big_w = [w_in, w_out, w_gate, w_up, w_down]
    big_m = [m_w_in, m_w_out, m_w_gate, m_w_up, m_w_down]
    big_v = [v_w_in, v_w_out, v_w_gate, v_w_up, v_w_down]
    n_kinds = len(big_w)

    gathered = [gather_weights([w[l].astype(BF16) for w in big_w]) for l in range(DEPTH)]
    conv_w_all = gather_small(conv_dw_w)
    conv_w_full = jnp.transpose(conv_w_all, (1, 2, 0, 3)).reshape(DEPTH, CONV_KERNEL, CONV_WIDTH)

    tabs = rope_tables(S)
    params = [_layer_params(l, gathered[l], conv_w_full, mix_norm_g, sink, conv_dw_b, conv_ln_g, conv_ln_b, sgu_ln_g,
                            sgu_ln_b, sgu_w, sgu_b, ffn_norm_g) for l in range(DEPTH)]

    act = x[0]
    saved = []
    for l in range(DEPTH):
        act, s = _layer_fwd(act, params[l], tabs)
        saved.append(s)
    loss_part, dx, dxb, g_final = final_loss(act, final_norm_g.reshape(1, D_MODEL), loss_target[0])
    loss = lax.psum(loss_part[0, 0], ("x", "y", "c"))

    halves = [None] * (DEPTH * n_kinds)
    small_grads = [None] * DEPTH
    for l in reversed(range(DEPTH)):
        dx, dxb, big, small_grads[l] = _layer_bwd(dx, dxb, params[l], saved[l], tabs)
        recv = exchange_sibling_halves(big)
        chip_sum = [add_sibling_half(g, r, c_idx) for g, r in zip(big, recv)]
        parts = exchange_chip_parts(chip_sum)
        for k in range(n_kinds):
            halves[l * n_kinds + k] = sum_chips(parts[k])
    big_grads = share_final_halves(halves, n_kinds)

    stacked = {n: jnp.stack([small_grads[l][n] for l in range(DEPTH)]) for n in small_grads[0]}
    stacked["final_norm_g"] = g_final
    packed, counts = _pack_rows([stacked[n] for n in _SMALL] + [stacked["conv_dw_w"]])
    reduced = allreduce_small(packed)
    small_w = dict(mix_norm_g=mix_norm_g, sink=sink, conv_dw_b=conv_dw_b, conv_ln_g=conv_ln_g, conv_ln_b=conv_ln_b,
                   sgu_ln_g=sgu_ln_g, sgu_ln_b=sgu_ln_b, sgu_w=sgu_w, sgu_b=sgu_b, ffn_norm_g=ffn_norm_g,
                   final_norm_g=final_norm_g)
    small_m = dict(mix_norm_g=m_mix_norm_g, sink=m_sink, conv_dw_b=m_conv_dw_b, conv_ln_g=m_conv_ln_g,
                   conv_ln_b=m_conv_ln_b, sgu_ln_g=m_sgu_ln_g, sgu_ln_b=m_sgu_ln_b, sgu_w=m_sgu_w, sgu_b=m_sgu_b,
                   ffn_norm_g=m_ffn_norm_g, final_norm_g=m_final_norm_g)
    small_v = dict(mix_norm_g=v_mix_norm_g, sink=v_sink, conv_dw_b=v_conv_dw_b, conv_ln_g=v_conv_ln_g,
                   conv_ln_b=v_conv_ln_b, sgu_ln_g=v_sgu_ln_g, sgu_ln_b=v_sgu_ln_b, sgu_w=v_sgu_w, sgu_b=v_sgu_b,
                   ffn_norm_g=v_ffn_norm_g, final_norm_g=v_final_norm_g)
    shapes = [small_w[n].shape for n in _SMALL] + [(DEPTH, CONV_KERNEL, CONV_WIDTH)]
    red = _unpack_rows(reduced, counts, shapes)
    g_small = dict(zip(_SMALL, red[:-1]))
    g_small["conv_dw_w"] = lax.dynamic_slice_in_dim(red[-1], my_chip * LANES, LANES, axis=2)
    small_w["conv_dw_w"], small_m["conv_dw_w"], small_v["conv_dw_w"] = conv_dw_w, m_conv_dw_w, v_conv_dw_w
    names = _SMALL + ["conv_dw_w"]
    pw, cnt = _pack_rows([small_w[n] for n in names])
    pg, _ = _pack_rows([g_small[n] for n in names])
    pm, _ = _pack_rows([small_m[n] for n in names])
    pv, _ = _pack_rows([small_v[n] for n in names])
    sd, sm, sv = adamw(pw, pg, pm, pv)
    shp = [small_w[n].shape for n in names]
    d_small = dict(zip(names, _unpack_rows(sd, cnt, shp)))
    m_small = dict(zip(names, _unpack_rows(sm, cnt, shp)))
    v_small = dict(zip(names, _unpack_rows(sv, cnt, shp)))

    big_names = ["w_in", "w_out", "w_gate", "w_up", "w_down"]
    g_big, d_big, m_big, v_big = {}, {}, {}, {}
    for k, n in enumerate(big_names):
        shape = big_w[k].shape
        flat = lambda a: a.reshape(-1, shape[-1])
        d, nm, nv = adamw(flat(big_w[k]), flat(big_grads[k]), flat(big_m[k]), flat(big_v[k]))
        g_big[n], d_big[n], m_big[n], v_big[n] = big_grads[k], d.reshape(shape), nm.reshape(shape), nv.reshape(shape)

    order = ["mix_norm_g", "w_in", "sink", "conv_dw_w", "conv_dw_b", "conv_ln_g", "conv_ln_b", "sgu_ln_g", "sgu_ln_b",
             "sgu_w", "sgu_b", "w_out", "ffn_norm_g", "w_gate", "w_up", "w_down", "final_norm_g"]
    grads = {**g_small, **g_big}
    deltas = {**d_small, **d_big}
    new_m = {**m_small, **m_big}
    new_v = {**v_small, **v_big}
    return (loss, dx[None], *[grads[n] for n in order], *[deltas[n] for n in order],
            *[new_m[n] for n in order], *[new_v[n] for n in order])
```

```python
import functools
import math

import jax
import jax.numpy as jnp
from jax import lax
from jax.experimental import pallas as pl
from jax.experimental.pallas import tpu as pltpu

F32, BF16 = jnp.float32, jnp.bfloat16

D_MODEL = 2048
DEPTH = 4
HEAD_DIM = 128
N_Q_HEADS = 8
N_KV_HEADS = 2
Q_PER_KV = N_Q_HEADS // N_KV_HEADS
ATTN_WIDTH = N_Q_HEADS * HEAD_DIM
KV_WIDTH = N_KV_HEADS * HEAD_DIM
CONV_WIDTH = 512
CONV_KERNEL = 31
CONV_PAD = 16
SGU_WIDTH = 512
SGU_HEADS = 4
CHUNK = 128
IN_WIDTH = 3584
D_FF = 5632
WINDOW = 128
ROT_DIM = 32
ROPE_THETA = 500000.0
EPS = 1e-6
N_CHIPS = 4
N_DEV = 8
LANES = 128
PACK_ROWS = 64
OFF_K = ATTN_WIDTH
OFF_V = OFF_K + KV_WIDTH
OFF_CA = OFF_V + KV_WIDTH
OFF_CG = OFF_CA + CONV_WIDTH
OFF_U = OFF_CG + CONV_WIDTH
OFF_VV = OFF_U + SGU_WIDTH

ADAM_LR, ADAM_B1, ADAM_B2, ADAM_EPS, ADAM_WD, ADAM_STEP = 0.001, 0.9, 0.999, 1e-08, 0.01, 10

VMEM_LIMIT = 56 * 1024 * 1024
MESH = pl.DeviceIdType.MESH
HBM = pl.BlockSpec(memory_space=pltpu.HBM)
VMEM_SPEC = pl.BlockSpec(memory_space=pltpu.VMEM)


def _call(name, body, *, grid, in_specs, out_specs, out_shape, scratch=(), sem=None):
    params = dict(vmem_limit_bytes=VMEM_LIMIT)
    if sem is not None:
        params["dimension_semantics"] = sem
    return pl.pallas_call(
        body, name=name, grid=grid, in_specs=in_specs, out_specs=out_specs, out_shape=out_shape,
        scratch_shapes=list(scratch), compiler_params=pltpu.CompilerParams(**params))


def _sigmoid(x):
    return 1.0 / (1.0 + jnp.exp(-x))


def rms_fwd(x, g):
    S = x.shape[0]
    tm = min(512, S)

    def body(x_ref, g_ref, o_ref):
        xv = x_ref[...]
        r = lax.rsqrt(jnp.mean(xv * xv, axis=-1, keepdims=True) + EPS)
        o_ref[...] = (xv * r * g_ref[...]).astype(BF16)

    return _call("rms_fwd", body, grid=(S // tm,),
                 in_specs=[pl.BlockSpec((tm, D_MODEL), lambda i: (i, 0)), pl.BlockSpec((1, D_MODEL), lambda i: (0, 0))],
                 out_specs=pl.BlockSpec((tm, D_MODEL), lambda i: (i, 0)),
                 out_shape=jax.ShapeDtypeStruct((S, D_MODEL), BF16), sem=("parallel",))(x, g)


def _rms_bwd_math(xv, gv, dh):
    r = lax.rsqrt(jnp.mean(xv * xv, axis=-1, keepdims=True) + EPS)
    n = xv * r
    dn = dh * gv
    dx = r * (dn - n * jnp.mean(dn * n, axis=-1, keepdims=True))
    dg = jnp.sum(dh * n, axis=0, keepdims=True)
    return dx, dg


def rms_bwd(x, g, dh, dres):
    S = x.shape[0]
    tm = min(256, S)

    def body(x_ref, g_ref, dh_ref, dres_ref, dx_ref, dxb_ref, dg_ref):
        dx, dg = _rms_bwd_math(x_ref[...], g_ref[...], dh_ref[...])
        dx = dx + dres_ref[...]
        dx_ref[...] = dx
        dxb_ref[...] = dx.astype(BF16)

        @pl.when(pl.program_id(0) == 0)
        def _():
            dg_ref[...] = dg

        @pl.when(pl.program_id(0) > 0)
        def _():
            dg_ref[...] += dg

    row = pl.BlockSpec((tm, D_MODEL), lambda i: (i, 0))
    vec = pl.BlockSpec((1, D_MODEL), lambda i: (0, 0))
    return _call("rms_bwd", body, grid=(S // tm,), in_specs=[row, vec, row, row], out_specs=[row, row, vec],
                 out_shape=[jax.ShapeDtypeStruct((S, D_MODEL), F32), jax.ShapeDtypeStruct((S, D_MODEL), BF16),
                            jax.ShapeDtypeStruct((1, D_MODEL), F32)], sem=("arbitrary",))(x, g, dh, dres)


def final_loss(x, g, target):
    S = x.shape[0]
    tm = min(256, S)

    def body(x_ref, g_ref, t_ref, loss_ref, dx_ref, dxb_ref, dg_ref):
        xv = x_ref[...]
        gv = g_ref[...]
        r = lax.rsqrt(jnp.mean(xv * xv, axis=-1, keepdims=True) + EPS)
        err = xv * r * gv - t_ref[...]
        part = 0.5 * jnp.sum(jnp.mean(err * err, axis=-1, keepdims=True), axis=0, keepdims=True)
        dx, dg = _rms_bwd_math(xv, gv, err * (1.0 / D_MODEL))
        dx_ref[...] = dx
        dxb_ref[...] = dx.astype(BF16)

        @pl.when(pl.program_id(0) == 0)
        def _():
            dg_ref[...] = dg
            loss_ref[...] = part

        @pl.when(pl.program_id(0) > 0)
        def _():
            dg_ref[...] += dg
            loss_ref[...] += part

    row = pl.BlockSpec((tm, D_MODEL), lambda i: (i, 0))
    vec = pl.BlockSpec((1, D_MODEL), lambda i: (0, 0))
    one = pl.BlockSpec((1, 1), lambda i: (0, 0))
    return _call("final_loss", body, grid=(S // tm,), in_specs=[row, vec, row], out_specs=[one, row, row, vec],
                 out_shape=[jax.ShapeDtypeStruct((1, 1), F32), jax.ShapeDtypeStruct((S, D_MODEL), F32),
                            jax.ShapeDtypeStruct((S, D_MODEL), BF16), jax.ShapeDtypeStruct((1, D_MODEL), F32)],
                 sem=("arbitrary",))(x, g, target)


NN = (((1,), (0,)), ((), ()))
NT = (((1,), (1,)), ((), ()))
TN = (((0,), (0,)), ((), ()))


def _matmul(name, operands, in_specs, out_shape, out_specs, grid, pairs, dims, acc_shape, epilogue):
    n_in, n_out, nk = len(operands), len(out_shape), grid[-1]

    def body(*refs):
        ins, outs = refs[:n_in], refs[n_in:n_in + n_out]
        part = None
        for ia, ib in pairs:
            d = lax.dot_general(ins[ia][...], ins[ib][...], dims, preferred_element_type=F32)
            part = d if part is None else part + d
        if nk == 1:
            epilogue(part, ins, outs)
        else:
            acc = refs[-1]
            k = pl.program_id(len(grid) - 1)

            @pl.when(k == 0)
            def _():
                acc[...] = part

            @pl.when(k > 0)
            def _():
                acc[...] += part

            @pl.when(k == nk - 1)
            def _():
                epilogue(acc[...], ins, outs)

    scratch = [pltpu.VMEM(acc_shape, F32)] if nk > 1 else []
    sem = ("parallel",) * (len(grid) - 1) + ("arbitrary",)
    return _call(name, body, grid=grid, in_specs=in_specs, out_specs=out_specs, out_shape=out_shape,
                 scratch=scratch, sem=sem)(*operands)


def _store(dtype):
    def epilogue(acc, ins, outs):
        outs[0][...] = acc.astype(dtype)
    return epilogue


def mm_nn_cols(a, w):
    S, K = a.shape
    J, _, Ns = w.shape
    tm = min(512, S)
    return _matmul("mm_nn_cols", (a, w),
                   [pl.BlockSpec((tm, K), lambda j, i, k: (i, 0)), pl.BlockSpec((None, K, Ns), lambda j, i, k: (j, 0, 0))],
                   [jax.ShapeDtypeStruct((S, J * Ns), BF16)], [pl.BlockSpec((tm, Ns), lambda j, i, k: (i, j))],
                   (J, S // tm, 1), [(0, 1)], NN, None, _store(BF16))[0]


def ffn_up(h, wg, wu):
    S, K = h.shape
    J, _, Ns = wg.shape
    tm = min(512, S)

    def body(h_ref, wg_ref, wu_ref, g_ref, u_ref, a_ref):
        hv = h_ref[...]
        gv = jnp.dot(hv, wg_ref[...], preferred_element_type=F32)
        uv = jnp.dot(hv, wu_ref[...], preferred_element_type=F32)
        g_ref[...] = gv.astype(BF16)
        u_ref[...] = uv.astype(BF16)
        a_ref[...] = (gv * _sigmoid(gv) * uv).astype(BF16)

    wspec = pl.BlockSpec((None, K, Ns), lambda j, i: (j, 0, 0))
    ospec = pl.BlockSpec((tm, Ns), lambda j, i: (i, j))
    oshape = jax.ShapeDtypeStruct((S, J * Ns), BF16)
    return _call("ffn_up", body, grid=(J, S // tm), in_specs=[pl.BlockSpec((tm, K), lambda j, i: (i, 0)), wspec, wspec],
                 out_specs=[ospec, ospec, ospec], out_shape=[oshape, oshape, oshape], sem=("parallel", "parallel"))(h, wg, wu)


def mm_nn_rows_res(a, w, res):
    S, K = a.shape
    N = w.shape[1]
    tm, tk = min(1024, S), 512

    def epilogue(acc, ins, outs):
        outs[0][...] = acc + ins[2][...]

    return _matmul("mm_nn_rows_res", (a, w, res),
                   [pl.BlockSpec((tm, tk), lambda i, k: (i, k)), pl.BlockSpec((tk, N), lambda i, k: (k, 0)),
                    pl.BlockSpec((tm, N), lambda i, k: (i, 0))],
                   [jax.ShapeDtypeStruct((S, N), F32)], [pl.BlockSpec((tm, N), lambda i, k: (i, 0))],
                   (S // tm, K // tk), [(0, 1)], NN, (tm, N), epilogue)[0]


def mm_nt_cols(pairs_in, out_dtype):
    dz0, w0 = pairs_in[0]
    S = dz0.shape[0]
    J, K, Ns = w0.shape
    tm = min(512, S)
    operands, specs, pairs = [], [], []
    for dz, w in pairs_in:
        pairs.append((len(operands), len(operands) + 1))
        operands += [dz, w]
        specs += [pl.BlockSpec((tm, Ns), lambda i, j: (i, j)), pl.BlockSpec((None, K, Ns), lambda i, j: (j, 0, 0))]
    return _matmul("mm_nt_cols%d" % len(pairs_in), tuple(operands), specs,
                   [jax.ShapeDtypeStruct((S, K), out_dtype)], [pl.BlockSpec((tm, K), lambda i, j: (i, 0))],
                   (S // tm, J), pairs, NT, (tm, K), _store(out_dtype))[0]


def mm_nt_rows(dy, w):
    S, N = dy.shape
    K = w.shape[0]
    tm, tko = min(1024, S), 512
    return _matmul("mm_nt_rows", (dy, w),
                   [pl.BlockSpec((tm, N), lambda i, kk, z: (i, 0)), pl.BlockSpec((tko, N), lambda i, kk, z: (kk, 0))],
                   [jax.ShapeDtypeStruct((S, K), BF16)], [pl.BlockSpec((tm, tko), lambda i, kk, z: (i, kk))],
                   (S // tm, K // tko, 1), [(0, 1)], NT, None, _store(BF16))[0]


def ffn_down_bwd(dy, w, gate, up):
    S, N = dy.shape
    K = w.shape[0]
    tm, tko = min(1024, S), 512

    def epilogue(acc, ins, outs):
        gv = ins[2][...].astype(F32)
        uv = ins[3][...].astype(F32)
        sg = _sigmoid(gv)
        outs[0][...] = (acc * uv * sg * (1.0 + gv * (1.0 - sg))).astype(BF16)
        outs[1][...] = (acc * gv * sg).astype(BF16)

    tile = pl.BlockSpec((tm, tko), lambda i, kk, z: (i, kk))
    oshape = jax.ShapeDtypeStruct((S, K), BF16)
    return _matmul("ffn_down_bwd", (dy, w, gate, up),
                   [pl.BlockSpec((tm, N), lambda i, kk, z: (i, 0)), pl.BlockSpec((tko, N), lambda i, kk, z: (kk, 0)), tile, tile],
                   [oshape, oshape], [tile, tile], (S // tm, K // tko, 1), [(0, 1)], NT, None, epilogue)


def mm_tn_cols(a, dz, J):
    S, M = a.shape
    Ns = dz.shape[1] // J
    tm, tk = min(1024, M), min(512, S)
    return _matmul("mm_tn_cols", (a, dz),
                   [pl.BlockSpec((tk, tm), lambda j, m, k: (k, m)), pl.BlockSpec((tk, Ns), lambda j, m, k: (k, j))],
                   [jax.ShapeDtypeStruct((J, M, Ns), BF16)], [pl.BlockSpec((None, tm, Ns), lambda j, m, k: (j, m, 0))],
                   (J, M // tm, S // tk), [(0, 1)], TN, (tm, Ns), _store(BF16))[0]


def mm_tn_rows(a, dy):
    S, K = a.shape
    N = dy.shape[1]
    tm, tk = 512, min(512, S)
    return _matmul("mm_tn_rows", (a, dy),
                   [pl.BlockSpec((tk, tm), lambda m, k: (k, m)), pl.BlockSpec((tk, N), lambda m, k: (k, 0))],
                   [jax.ShapeDtypeStruct((K, N), BF16)], [pl.BlockSpec((tm, N), lambda m, k: (m, 0))],
                   (K // tm, S // tk), [(0, 1)], TN, (tm, N), _store(BF16))[0]


def rope_tables(S):
    half = ROT_DIM // 2
    pos = jnp.arange(S, dtype=F32)
    inv = ROPE_THETA ** (-jnp.arange(0, ROT_DIM, 2, dtype=F32) / ROT_DIM)
    ang = pos[:, None] * inv[None, :]
    cos, sin = jnp.cos(ang), jnp.sin(ang)
    zeros = jnp.zeros((S, HEAD_DIM - ROT_DIM), F32)
    c = jnp.concatenate([cos, cos, jnp.ones((S, HEAD_DIM - ROT_DIM), F32)], axis=1)
    s_lo = jnp.concatenate([-sin, jnp.zeros((S, half), F32), zeros], axis=1)
    s_hi = jnp.concatenate([jnp.zeros((S, half), F32), sin, zeros], axis=1)
    return c, s_lo, s_hi


def _rope(t, c, s_lo, s_hi):
    half = ROT_DIM // 2
    return t * c + pltpu.roll(t, HEAD_DIM - half, 1) * s_lo + pltpu.roll(t, half, 1) * s_hi


def _unrope(d, c, s_lo, s_hi):
    half = ROT_DIM // 2
    return d * c + pltpu.roll(d * s_lo, half, 1) + pltpu.roll(d * s_hi, HEAD_DIM - half, 1)


def rope_fwd(z, tabs):
    S = z.shape[0]
    nb = S // CHUNK

    def body(q_ref, kv_ref, c_ref, sl_ref, sh_ref, qr_ref, kp_ref, vp_ref):
        i = pl.program_id(0)

        @pl.when(i == 0)
        def _():
            zero = jnp.zeros((CHUNK, KV_WIDTH), BF16)
            kp_ref[0:CHUNK, :] = zero
            vp_ref[0:CHUNK, :] = zero
            kp_ref[S + CHUNK:S + 2 * CHUNK, :] = zero
            vp_ref[S + CHUNK:S + 2 * CHUNK, :] = zero

        c, sl, sh = c_ref[...], sl_ref[...], sh_ref[...]
        for h in range(N_Q_HEADS):
            cols = slice(h * HEAD_DIM, (h + 1) * HEAD_DIM)
            qr_ref[:, cols] = _rope(q_ref[:, cols].astype(F32), c, sl, sh).astype(BF16)
        rows = pl.ds(pl.multiple_of(CHUNK + i * CHUNK, CHUNK), CHUNK)
        for g in range(N_KV_HEADS):
            cols = slice(g * HEAD_DIM, (g + 1) * HEAD_DIM)
            kp_ref[rows, cols] = _rope(kv_ref[:, cols].astype(F32), c, sl, sh).astype(BF16)
        vp_ref[rows, :] = kv_ref[:, KV_WIDTH:2 * KV_WIDTH]

    tab = pl.BlockSpec((CHUNK, HEAD_DIM), lambda i: (i, 0))
    pad = pl.BlockSpec((S + 2 * CHUNK, KV_WIDTH), lambda i: (0, 0))
    return _call("rope_fwd", body, grid=(nb,),
                 in_specs=[pl.BlockSpec((CHUNK, ATTN_WIDTH), lambda i: (i, 0)),
                           pl.BlockSpec((CHUNK, 2 * KV_WIDTH), lambda i: (i, OFF_K // (2 * KV_WIDTH))), tab, tab, tab],
                 out_specs=[pl.BlockSpec((CHUNK, ATTN_WIDTH), lambda i: (i, 0)), pad, pad],
                 out_shape=[jax.ShapeDtypeStruct((S, ATTN_WIDTH), BF16), jax.ShapeDtypeStruct((S + 2 * CHUNK, KV_WIDTH), BF16),
                            jax.ShapeDtypeStruct((S + 2 * CHUNK, KV_WIDTH), BF16)], sem=("arbitrary",))(z, z, *tabs)


def rope_bwd(dq, dkp, dvp, tabs):
    S = dq.shape[0]

    def body(dq_ref, dk_ref, dv_ref, c_ref, sl_ref, sh_ref, o_ref):
        c, sl, sh = c_ref[...], sl_ref[...], sh_ref[...]
        for h in range(N_Q_HEADS):
            cols = slice(h * HEAD_DIM, (h + 1) * HEAD_DIM)
            o_ref[:, cols] = _unrope(dq_ref[:, cols], c, sl, sh).astype(BF16)
        for g in range(N_KV_HEADS):
            cols = slice(g * HEAD_DIM, (g + 1) * HEAD_DIM)
            o_ref[:, OFF_K + g * HEAD_DIM:OFF_K + (g + 1) * HEAD_DIM] = _unrope(dk_ref[:, cols], c, sl, sh).astype(BF16)
        o_ref[:, OFF_V:OFF_V + KV_WIDTH] = dv_ref[...].astype(BF16)

    tab = pl.BlockSpec((CHUNK, HEAD_DIM), lambda i: (i, 0))
    pad = pl.BlockSpec((CHUNK, KV_WIDTH), lambda i: (i + 1, 0))
    return _call("rope_bwd", body, grid=(S // CHUNK,),
                 in_specs=[pl.BlockSpec((CHUNK, ATTN_WIDTH), lambda i: (i, 0)), pad, pad, tab, tab, tab],
                 out_specs=pl.BlockSpec((CHUNK, OFF_CA), lambda i: (i, 0)),
                 out_shape=jax.ShapeDtypeStruct((S, OFF_CA), BF16), sem=("parallel",))(dq, dkp, dvp, *tabs)


def _attn_probs(q, kb, sk, n, S):
    scale = 1.0 / math.sqrt(HEAD_DIM)
    s = lax.dot_general(q, kb, NT, preferred_element_type=F32) * scale
    row = lax.broadcasted_iota(jnp.int32, (CHUNK, 3 * CHUNK), 0)
    col = lax.broadcasted_iota(jnp.int32, (CHUNK, 3 * CHUNK), 1)
    kpos = (n - 1) * CHUNK + col
    valid = (jnp.abs(col - CHUNK - row) <= WINDOW) & (kpos >= 0) & (kpos < S)
    s = jnp.where(valid, s, jnp.finfo(F32).min)
    m = jnp.maximum(jnp.max(s, axis=1, keepdims=True), sk)
    e = jnp.exp(s - m)
    es = jnp.exp(sk - m)
    inv = 1.0 / (jnp.sum(e, axis=1, keepdims=True) + es)
    return e * inv, es * inv


def attn_fwd(qr, kp, vp, sink3):
    S = qr.shape[0]
    tq = min(512, S)
    gw = Q_PER_KV * HEAD_DIM

    def body(q_ref, k_ref, v_ref, s_ref, o_ref):
        i = pl.program_id(1)
        for b in range(tq // CHUNK):
            n = i * (tq // CHUNK) + b
            win = pl.ds(pl.multiple_of(n * CHUNK, CHUNK), 3 * CHUNK)
            kb, vb = k_ref[win, :], v_ref[win, :]
            for r in range(Q_PER_KV):
                rows, cols = slice(b * CHUNK, (b + 1) * CHUNK), slice(r * HEAD_DIM, (r + 1) * HEAD_DIM)
                p, _ = _attn_probs(q_ref[rows, cols], kb, s_ref[r:r + 1, 0:1], n, S)
                o_ref[rows, cols] = jnp.dot(p.astype(BF16), vb, preferred_element_type=F32).astype(BF16)

    kv = pl.BlockSpec((S + 2 * CHUNK, HEAD_DIM), lambda g, i: (0, g))
    return _call("attn_fwd", body, grid=(N_KV_HEADS, S // tq),
                 in_specs=[pl.BlockSpec((tq, gw), lambda g, i: (i, g)), kv, kv,
                           pl.BlockSpec((None, Q_PER_KV, LANES), lambda g, i: (g, 0, 0))],
                 out_specs=pl.BlockSpec((tq, gw), lambda g, i: (i, g)),
                 out_shape=jax.ShapeDtypeStruct((S, ATTN_WIDTH), BF16), sem=("parallel", "arbitrary"))(qr, kp, vp, sink3)


def attn_bwd(qr, kp, vp, sink3, dmix):
    S = qr.shape[0]
    tq = min(512, S)
    gw = Q_PER_KV * HEAD_DIM
    scale = 1.0 / math.sqrt(HEAD_DIM)

    def body(q_ref, k_ref, v_ref, s_ref, do_ref, dq_ref, dk_ref, dv_ref, ds_ref):
        i = pl.program_id(1)

        @pl.when(i == 0)
        def _():
            dk_ref[...] = jnp.zeros_like(dk_ref)
            dv_ref[...] = jnp.zeros_like(dv_ref)
            ds_ref[...] = jnp.zeros_like(ds_ref)

        for b in range(tq // CHUNK):
            n = i * (tq // CHUNK) + b
            win = pl.ds(pl.multiple_of(n * CHUNK, CHUNK), 3 * CHUNK)
            kb, vb = k_ref[win, :], v_ref[win, :]
            dk_acc = jnp.zeros((3 * CHUNK, HEAD_DIM), F32)
            dv_acc = jnp.zeros((3 * CHUNK, HEAD_DIM), F32)
            for r in range(Q_PER_KV):
                rows, cols = slice(b * CHUNK, (b + 1) * CHUNK), slice(r * HEAD_DIM, (r + 1) * HEAD_DIM)
                q = q_ref[rows, cols]
                do = do_ref[rows, cols]
                p, p_sink = _attn_probs(q, kb, s_ref[r:r + 1, 0:1], n, S)
                dp = lax.dot_general(do, vb, NT, preferred_element_type=F32)
                delta = jnp.sum(p * dp, axis=1, keepdims=True)
                dsc = (p * (dp - delta) * scale).astype(BF16)
                dq_ref[rows, cols] = jnp.dot(dsc, kb, preferred_element_type=F32)
                dk_acc = dk_acc + lax.dot_general(dsc, q, TN, preferred_element_type=F32)
                dv_acc = dv_acc + lax.dot_general(p.astype(BF16), do, TN, preferred_element_type=F32)
                dsink = jnp.sum(-p_sink * delta, axis=0, keepdims=True)
                ds_ref[r:r + 1, :] += jnp.broadcast_to(dsink, (1, LANES))
            dk_ref[win, :] += dk_acc
            dv_ref[win, :] += dv_acc

    kv = pl.BlockSpec((S + 2 * CHUNK, HEAD_DIM), lambda g, i: (0, g))
    qspec = pl.BlockSpec((tq, gw), lambda g, i: (i, g))
    sspec = pl.BlockSpec((None, Q_PER_KV, LANES), lambda g, i: (g, 0, 0))
    padshape = jax.ShapeDtypeStruct((S + 2 * CHUNK, KV_WIDTH), F32)
    return _call("attn_bwd", body, grid=(N_KV_HEADS, S // tq),
                 in_specs=[qspec, kv, kv, sspec, qspec],
                 out_specs=[qspec, kv, kv, sspec],
                 out_shape=[jax.ShapeDtypeStruct((S, ATTN_WIDTH), F32), padshape, padshape,
                            jax.ShapeDtypeStruct((N_KV_HEADS, Q_PER_KV, LANES), F32)],
                 sem=("parallel", "arbitrary"))(qr, kp, vp, sink3, dmix)


CONV_TILE = 256


def _fill_padded(dst_ref, value, S):
    zero = jnp.zeros((CONV_PAD, LANES), F32)
    dst_ref[0:CONV_PAD, :] = zero
    dst_ref[CONV_PAD + S:2 * CONV_PAD + S, :] = zero
    dst_ref[CONV_PAD:CONV_PAD + S, :] = value


def conv_dw_fwd(z, w32, b):
    S = z.shape[0]
    T = min(CONV_TILE, S)
    lo = CONV_PAD - (CONV_KERNEL - 1) // 2

    def body(a_ref, g_ref, w_ref, b_ref, o_ref, c0_ref):
        _fill_padded(c0_ref, a_ref[...].astype(F32) * _sigmoid(g_ref[...].astype(F32)), S)

        def tile(t, carry):
            base = pl.multiple_of(t * T, T)
            acc = jnp.broadcast_to(b_ref[...], (T, LANES))
            for j in range(CONV_KERNEL):
                acc = acc + w_ref[j:j + 1, :] * c0_ref[pl.ds(base + lo + j, T), :]
            o_ref[pl.ds(base, T), :] = acc
            return carry

        lax.fori_loop(0, S // T, tile, 0)

    nca, ncg = OFF_CA // LANES, OFF_CG // LANES
    return _call("conv_dw_fwd", body, grid=(CONV_WIDTH // LANES,),
                 in_specs=[pl.BlockSpec((S, LANES), lambda cb: (0, nca + cb)), pl.BlockSpec((S, LANES), lambda cb: (0, ncg + cb)),
                           pl.BlockSpec((32, LANES), lambda cb: (0, cb)), pl.BlockSpec((1, LANES), lambda cb: (0, cb))],
                 out_specs=pl.BlockSpec((S, LANES), lambda cb: (0, cb)),
                 out_shape=jax.ShapeDtypeStruct((S, CONV_WIDTH), F32),
                 scratch=[pltpu.VMEM((S + 2 * CONV_PAD, LANES), F32)], sem=("parallel",))(z, z, w32, b)


def _ln_stats(x):
    mu = jnp.mean(x, axis=-1, keepdims=True)
    xc = x - mu
    rs = lax.rsqrt(jnp.mean(xc * xc, axis=-1, keepdims=True) + EPS)
    return xc * rs, rs


def _ln_bwd(dy, xh, rs, g):
    dxh = dy * g
    return rs * (dxh - jnp.mean(dxh, axis=-1, keepdims=True) - xh * jnp.mean(dxh * xh, axis=-1, keepdims=True))


def conv_ln_fwd(c1, g, b):
    S = c1.shape[0]
    T = min(512, S)

    def body(x_ref, g_ref, b_ref, o_ref):
        xh, _ = _ln_stats(x_ref[...])
        y = xh * g_ref[...] + b_ref[...]
        o_ref[...] = (y * _sigmoid(y)).astype(BF16)

    row = pl.BlockSpec((T, CONV_WIDTH), lambda i: (i, 0))
    vec = pl.BlockSpec((1, CONV_WIDTH), lambda i: (0, 0))
    return _call("conv_ln_fwd", body, grid=(S // T,), in_specs=[row, vec, vec], out_specs=row,
                 out_shape=jax.ShapeDtypeStruct((S, CONV_WIDTH), BF16), sem=("parallel",))(c1, g, b)


def _acc_out(ref, value):
    @pl.when(pl.program_id(0) == 0)
    def _():
        ref[...] = value

    @pl.when(pl.program_id(0) > 0)
    def _():
        ref[...] += value


def conv_ln_bwd(dmix, c1, g, b):
    S = c1.shape[0]
    T = min(512, S)

    def body(d_ref, x_ref, g_ref, b_ref, dx_ref, dg_ref, db_ref):
        xh, rs = _ln_stats(x_ref[...])
        gv = g_ref[...]
        y = xh * gv + b_ref[...]
        sg = _sigmoid(y)
        dy = d_ref[...].astype(F32) * sg * (1.0 + y * (1.0 - sg))
        dx_ref[...] = _ln_bwd(dy, xh, rs, gv)
        _acc_out(dg_ref, jnp.sum(dy * xh, axis=0, keepdims=True))
        _acc_out(db_ref, jnp.sum(dy, axis=0, keepdims=True))

    row = pl.BlockSpec((T, CONV_WIDTH), lambda i: (i, 0))
    vec = pl.BlockSpec((1, CONV_WIDTH), lambda i: (0, 0))
    vshape = jax.ShapeDtypeStruct((1, CONV_WIDTH), F32)
    return _call("conv_ln_bwd", body, grid=(S // T,),
                 in_specs=[pl.BlockSpec((T, CONV_WIDTH), lambda i: (i, ATTN_WIDTH // CONV_WIDTH)), row, vec, vec],
                 out_specs=[row, vec, vec], out_shape=[jax.ShapeDtypeStruct((S, CONV_WIDTH), F32), vshape, vshape],
                 sem=("arbitrary",))(dmix, c1, g, b)


def conv_dw_bwd(dc1, z, w32):
    S = z.shape[0]
    T = min(CONV_TILE, S)
    half = (CONV_KERNEL - 1) // 2
    lo = CONV_PAD - half

    def body(d_ref, a_ref, g_ref, w_ref, da_ref, dg_ref, dw_ref, db_ref, c0_ref, d1_ref, wacc_ref):
        av = a_ref[...].astype(F32)
        sg = _sigmoid(g_ref[...].astype(F32))
        _fill_padded(c0_ref, av * sg, S)
        _fill_padded(d1_ref, d_ref[...], S)
        wacc_ref[...] = jnp.zeros_like(wacc_ref)

        def tile(t, carry):
            base = pl.multiple_of(t * T, T)
            d1 = d_ref[pl.ds(base, T), :]
            acc = jnp.zeros((T, LANES), F32)
            for j in range(CONV_KERNEL):
                acc = acc + w_ref[j:j + 1, :] * d1_ref[pl.ds(base + CONV_PAD + half - j, T), :]
                prod = d1 * c0_ref[pl.ds(base + lo + j, T), :]
                wacc_ref[j] += jnp.sum(prod.reshape(T // 8, 8, LANES), axis=0)
            rows = pl.ds(base, T)
            a_t = a_ref[rows, :].astype(F32)
            s_t = _sigmoid(g_ref[rows, :].astype(F32))
            da_ref[rows, :] = (acc * s_t).astype(BF16)
            dg_ref[rows, :] = (acc * a_t * s_t * (1.0 - s_t)).astype(BF16)
            return carry

        lax.fori_loop(0, S // T, tile, 0)
        dw_ref[...] = jnp.sum(wacc_ref[...], axis=1)
        db_ref[...] = jnp.sum(d_ref[...], axis=0, keepdims=True)

    nca, ncg = OFF_CA // LANES, OFF_CG // LANES
    col = pl.BlockSpec((S, LANES), lambda cb: (0, cb))
    oshape = jax.ShapeDtypeStruct((S, CONV_WIDTH), BF16)
    return _call("conv_dw_bwd", body, grid=(CONV_WIDTH // LANES,),
                 in_specs=[col, pl.BlockSpec((S, LANES), lambda cb: (0, nca + cb)), pl.BlockSpec((S, LANES), lambda cb: (0, ncg + cb)),
                           pl.BlockSpec((32, LANES), lambda cb: (0, cb))],
                 out_specs=[col, col, pl.BlockSpec((32, LANES), lambda cb: (0, cb)), pl.BlockSpec((1, LANES), lambda cb: (0, cb))],
                 out_shape=[oshape, oshape, jax.ShapeDtypeStruct((32, CONV_WIDTH), F32), jax.ShapeDtypeStruct((1, CONV_WIDTH), F32)],
                 scratch=[pltpu.VMEM((S + 2 * CONV_PAD, LANES), F32), pltpu.VMEM((S + 2 * CONV_PAD, LANES), F32),
                          pltpu.VMEM((32, 8, LANES), F32)], sem=("parallel",))(dc1, z, z, w32)


_INV_SQRT2 = 1.0 / math.sqrt(2.0)
_INV_SQRT2PI = 1.0 / math.sqrt(2.0 * math.pi)


def _gelu(x):
    return 0.5 * x * (1.0 + lax.erf(x * _INV_SQRT2))


def _gelu_grad(x):
    return 0.5 * (1.0 + lax.erf(x * _INV_SQRT2)) + x * jnp.exp(-0.5 * x * x) * _INV_SQRT2PI


def sgu_fwd(z, g, b, ws, bs):
    S = z.shape[0]
    T = min(512, S)

    def body(u_ref, v_ref, g_ref, b_ref, ws_ref, bs_ref, o_ref):
        xh, _ = _ln_stats(_gelu(v_ref[...].astype(F32)))
        vn = (xh * g_ref[...] + b_ref[...]).astype(BF16)
        for ch in range(T // CHUNK):
            rows = slice(ch * CHUNK, (ch + 1) * CHUNK)
            for h in range(SGU_HEADS):
                cols = slice(h * HEAD_DIM, (h + 1) * HEAD_DIM)
                sp = jnp.dot(ws_ref[h], vn[rows, cols], preferred_element_type=F32) + bs_ref[h]
                o_ref[rows, cols] = (_gelu(u_ref[rows, cols].astype(F32)) * sp).astype(BF16)

    vec = pl.BlockSpec((1, SGU_WIDTH), lambda i: (0, 0))
    full = pl.BlockSpec((SGU_HEADS, CHUNK, CHUNK), lambda i: (0, 0, 0))
    return _call("sgu_fwd", body, grid=(S // T,),
                 in_specs=[pl.BlockSpec((T, SGU_WIDTH), lambda i: (i, OFF_U // SGU_WIDTH)),
                           pl.BlockSpec((T, SGU_WIDTH), lambda i: (i, OFF_VV // SGU_WIDTH)), vec, vec, full, full],
                 out_specs=pl.BlockSpec((T, SGU_WIDTH), lambda i: (i, 0)),
                 out_shape=jax.ShapeDtypeStruct((S, SGU_WIDTH), BF16), sem=("parallel",))(z, z, g, b, ws, bs)


def sgu_bwd(z, dmix, g, b, ws, bs):
    S = z.shape[0]
    T = min(512, S)

    def body(u_ref, v_ref, d_ref, g_ref, b_ref, ws_ref, bs_ref, du_ref, dv_ref, dws_ref, dbs_ref, dg_ref, db_ref, dvn_ref):
        @pl.when(pl.program_id(0) == 0)
        def _():
            dws_ref[...] = jnp.zeros_like(dws_ref)
            dbs_ref[...] = jnp.zeros_like(dbs_ref)

        vraw = v_ref[...].astype(F32)
        xh, rs = _ln_stats(_gelu(vraw))
        gv = g_ref[...]
        vn = (xh * gv + b_ref[...]).astype(BF16)
        for ch in range(T // CHUNK):
            rows = slice(ch * CHUNK, (ch + 1) * CHUNK)
            for h in range(SGU_HEADS):
                cols = slice(h * HEAD_DIM, (h + 1) * HEAD_DIM)
                w = ws_ref[h]
                vb = vn[rows, cols]
                sp = jnp.dot(w, vb, preferred_element_type=F32) + bs_ref[h]
                uraw = u_ref[rows, cols].astype(F32)
                dout = d_ref[rows, cols].astype(F32)
                du_ref[rows, cols] = (dout * sp * _gelu_grad(uraw)).astype(BF16)
                dsp = dout * _gelu(uraw)
                dspb = dsp.astype(BF16)
                dvn_ref[rows, cols] = lax.dot_general(w, dspb, TN, preferred_element_type=F32)
                dws_ref[h] += lax.dot_general(dspb, vb, NT, preferred_element_type=F32)
                dbs_ref[h] += jnp.sum(dsp, axis=1, keepdims=True)
        dvn = dvn_ref[...]
        dv_ref[...] = (_ln_bwd(dvn, xh, rs, gv) * _gelu_grad(vraw)).astype(BF16)
        _acc_out(dg_ref, jnp.sum(dvn * xh, axis=0, keepdims=True))
        _acc_out(db_ref, jnp.sum(dvn, axis=0, keepdims=True))

    vec = pl.BlockSpec((1, SGU_WIDTH), lambda i: (0, 0))
    full = pl.BlockSpec((SGU_HEADS, CHUNK, CHUNK), lambda i: (0, 0, 0))
    row = pl.BlockSpec((T, SGU_WIDTH), lambda i: (i, 0))
    oshape = jax.ShapeDtypeStruct((S, SGU_WIDTH), BF16)
    vshape = jax.ShapeDtypeStruct((1, SGU_WIDTH), F32)
    return _call("sgu_bwd", body, grid=(S // T,),
                 in_specs=[pl.BlockSpec((T, SGU_WIDTH), lambda i: (i, OFF_U // SGU_WIDTH)),
                           pl.BlockSpec((T, SGU_WIDTH), lambda i: (i, OFF_VV // SGU_WIDTH)),
                           pl.BlockSpec((T, SGU_WIDTH), lambda i: (i, (ATTN_WIDTH + CONV_WIDTH) // SGU_WIDTH)), vec, vec, full, full],
                 out_specs=[row, row, full, pl.BlockSpec((SGU_HEADS, CHUNK, 1), lambda i: (0, 0, 0)), vec, vec],
                 out_shape=[oshape, oshape, jax.ShapeDtypeStruct((SGU_HEADS, CHUNK, CHUNK), F32),
                            jax.ShapeDtypeStruct((SGU_HEADS, CHUNK, 1), F32), vshape, vshape],
                 scratch=[pltpu.VMEM((T, SGU_WIDTH), F32)], sem=("arbitrary",))(z, z, dmix, g, b, ws, bs)


def _row_tile(rows, cols, n_arrays):
    budget = (24 * 1024 * 1024) // (n_arrays * 2 * 4 * cols)
    t = min(rows, max(16, budget // 16 * 16))
    while rows % t:
        t -= 16
    return t


def add_sibling_half(grad, recv, c_idx):
    J, R, C = grad.shape
    hr = R // 2
    tr = _row_tile(hr, C, 3)
    nb = hr // tr

    def body(c_ref, g_ref, r_ref, o_ref):
        o_ref[...] = (g_ref[...].astype(F32) + r_ref[...].astype(F32)).astype(BF16)

    grid_spec = pltpu.PrefetchScalarGridSpec(
        num_scalar_prefetch=1, grid=(J, nb),
        in_specs=[pl.BlockSpec((None, tr, C), lambda j, i, c: (j, c[0] * nb + i, 0)),
                  pl.BlockSpec((None, tr, C), lambda j, i, c: (j, i, 0))],
        out_specs=pl.BlockSpec((None, tr, C), lambda j, i, c: (j, i, 0)))
    return pl.pallas_call(body, name="add_sibling_half", grid_spec=grid_spec,
                          out_shape=jax.ShapeDtypeStruct((J, hr, C), BF16),
                          compiler_params=pltpu.CompilerParams(vmem_limit_bytes=VMEM_LIMIT,
                                                               dimension_semantics=("parallel", "parallel")))(c_idx, grad, recv)


def sum_chips(own, others, stack, chip_layer):
    R, C = own.shape[1:]
    tr = _row_tile(R, C, 4)

    def body(idx_ref, own_ref, oth_ref, stack_ref, o_ref):
        acc = own_ref[...].astype(F32)
        for j in range(3):
            acc = acc + oth_ref[j].astype(F32)
        o_ref[...] = acc

    grid_spec = pltpu.PrefetchScalarGridSpec(
        num_scalar_prefetch=1, grid=(R // tr,),
        in_specs=[pl.BlockSpec((None, tr, C), lambda i, idx: (idx[0], i, 0)),
                  pl.BlockSpec((3, tr, C), lambda i, idx: (0, i, 0)),
                  pl.BlockSpec(memory_space=pl.ANY)],
        out_specs=pl.BlockSpec((None, tr, C), lambda i, idx: (idx[1], i, 0)))
    return pl.pallas_call(body, name="sum_chips", grid_spec=grid_spec,
                          out_shape=jax.ShapeDtypeStruct(stack.shape, F32), input_output_aliases={3: 0},
                          compiler_params=pltpu.CompilerParams(vmem_limit_bytes=VMEM_LIMIT,
                                                               dimension_semantics=("parallel",)))(chip_layer, own, others, stack)


def adamw_halves(w, mine, theirs, m, v, c_idx):
    L, R, C = w.shape
    hr = R // 2
    tr = _row_tile(hr, C, 9)
    nb = hr // tr

    def body(c_ref, w_ref, a_ref, b_ref, m_ref, v_ref, g_ref, d_ref, nm_ref, nv_ref):
        gv = jnp.where(pl.program_id(1) == c_ref[0], a_ref[...], b_ref[...])
        g_ref[...] = gv
        nm = ADAM_B1 * m_ref[...] + (1.0 - ADAM_B1) * gv
        nv = ADAM_B2 * v_ref[...] + (1.0 - ADAM_B2) * (gv * gv)
        m_hat = nm / (1.0 - ADAM_B1 ** ADAM_STEP)
        v_hat = nv / (1.0 - ADAM_B2 ** ADAM_STEP)
        d_ref[...] = -ADAM_LR * (m_hat / (jnp.sqrt(v_hat) + ADAM_EPS) + ADAM_WD * w_ref[...])
        nm_ref[...] = nm
        nv_ref[...] = nv

    full = pl.BlockSpec((None, tr, C), lambda l, h, i, c: (l, h * nb + i, 0))
    a_spec = pl.BlockSpec((None, tr, C), lambda l, h, i, c: (l, jnp.where(h == c[0], i, 0), 0))
    b_spec = pl.BlockSpec((None, tr, C), lambda l, h, i, c: (l, jnp.where(h == c[0], 0, i), 0))
    grid_spec = pltpu.PrefetchScalarGridSpec(num_scalar_prefetch=1, grid=(L, 2, nb),
                                             in_specs=[full, a_spec, b_spec, full, full], out_specs=[full] * 4)
    shape = jax.ShapeDtypeStruct((L, R, C), F32)
    return pl.pallas_call(body, name="adamw_halves", grid_spec=grid_spec, out_shape=[shape] * 4,
                          compiler_params=pltpu.CompilerParams(vmem_limit_bytes=VMEM_LIMIT,
                                                               dimension_semantics=("parallel", "arbitrary", "arbitrary")))(
        c_idx, w, mine, theirs, m, v)


def adamw(w, g, m, v):
    R, C = w.shape
    tr = _row_tile(R, C, 7)

    def body(w_ref, g_ref, m_ref, v_ref, d_ref, nm_ref, nv_ref):
        gv = g_ref[...]
        nm = ADAM_B1 * m_ref[...] + (1.0 - ADAM_B1) * gv
        nv = ADAM_B2 * v_ref[...] + (1.0 - ADAM_B2) * (gv * gv)
        m_hat = nm / (1.0 - ADAM_B1 ** ADAM_STEP)
        v_hat = nv / (1.0 - ADAM_B2 ** ADAM_STEP)
        d_ref[...] = -ADAM_LR * (m_hat / (jnp.sqrt(v_hat) + ADAM_EPS) + ADAM_WD * w_ref[...])
        nm_ref[...] = nm
        nv_ref[...] = nv

    spec = pl.BlockSpec((tr, C), lambda i: (i, 0))
    shape = jax.ShapeDtypeStruct((R, C), F32)
    return _call("adamw", body, grid=(R // tr,), in_specs=[spec] * 4, out_specs=[spec] * 3, out_shape=[shape] * 3,
                 sem=("parallel",))(w, g, m, v)


def _place():
    x, y, c = lax.axis_index("x"), lax.axis_index("y"), lax.axis_index("c")
    chips = [(1 - x, y), (x, 1 - y), (1 - x, 1 - y)]
    return x, y, c, chips


def _remote(src, dst, send_sem, recv_sem, dev):
    return pltpu.make_async_remote_copy(src_ref=src, dst_ref=dst, send_sem=send_sem, recv_sem=recv_sem,
                                        device_id=dev, device_id_type=MESH)


def gather_weights(shards):
    n = len(shards)
    own = 6

    def body(*refs):
        ins, outs = refs[:n], refs[n:2 * n]
        send_sems, recv_sems = refs[2 * n:]
        x, y, c, chips = _place()
        me = 2 * x + y
        sibling = (x, y, 1 - c)
        started = []
        for k in range(n):
            hr = ins[k].shape[0] // 2
            mine = pl.ds(pl.multiple_of(c * hr, 8), hr)
            cp = _remote(ins[k], outs[k].at[me], send_sems.at[k, own], recv_sems.at[k, own], sibling)
            cp.start()
            started.append(cp)
            for t, (px, py) in enumerate(chips):
                cp = _remote(ins[k].at[mine, :], outs[k].at[me, mine, :], send_sems.at[k, t], recv_sems.at[k, t], (px, py, c))
                cp.start()
                started.append(cp)
        for k in range(n):
            hr = ins[k].shape[0] // 2
            mine = pl.ds(pl.multiple_of(c * hr, 8), hr)
            for t, (px, py) in enumerate(chips):
                landed = outs[k].at[2 * px + py, mine, :]
                _remote(landed, landed, send_sems.at[k, t], recv_sems.at[k, t], (px, py, c)).wait_recv()
                cp = _remote(landed, landed, send_sems.at[k, 3 + t], recv_sems.at[k, 3 + t], sibling)
                cp.start()
                started.append(cp)
        for k in range(n):
            hr = ins[k].shape[0] // 2
            other = pl.ds(pl.multiple_of((1 - c) * hr, 8), hr)
            for t, (px, py) in enumerate(chips):
                landed = outs[k].at[2 * px + py, other, :]
                _remote(landed, landed, send_sems.at[k, 3 + t], recv_sems.at[k, 3 + t], sibling).wait_recv()
            landed = outs[k].at[me]
            _remote(landed, landed, send_sems.at[k, own], recv_sems.at[k, own], sibling).wait_recv()
        for cp in started:
            cp.wait_send()

    return pl.pallas_call(
        body, name="gather_weights", in_specs=[HBM] * n, out_specs=[HBM] * n,
        out_shape=[jax.ShapeDtypeStruct((N_CHIPS,) + s.shape, s.dtype) for s in shards],
        scratch_shapes=[pltpu.SemaphoreType.DMA((n, 7)), pltpu.SemaphoreType.DMA((n, 7))],
    )(*shards)


def gather_small(block):
    def body(in_ref, out_ref, send_sems, recv_sems):
        x, y, c, chips = _place()
        me = 2 * x + y
        out_ref[me] = in_ref[...]
        sends = []
        for t, (px, py) in enumerate(chips):
            cp = _remote(in_ref, out_ref.at[me], send_sems.at[t], recv_sems.at[t], (px, py, c))
            cp.start()
            sends.append(cp)
        for t, (px, py) in enumerate(chips):
            landed = out_ref.at[2 * px + py]
            _remote(landed, landed, send_sems.at[t], recv_sems.at[t], (px, py, c)).wait_recv()
        for cp in sends:
            cp.wait_send()

    return pl.pallas_call(
        body, name="gather_small", in_specs=[VMEM_SPEC], out_specs=VMEM_SPEC,
        out_shape=jax.ShapeDtypeStruct((N_CHIPS,) + block.shape, block.dtype),
        scratch_shapes=[pltpu.SemaphoreType.DMA((3,)), pltpu.SemaphoreType.DMA((3,))],
    )(block)


def exchange_sibling_halves(grads):
    n = len(grads)

    def body(*refs):
        ins, outs = refs[:n], refs[n:2 * n]
        send_sems, recv_sems = refs[2 * n:]
        x, y, c, _ = _place()
        copies = []
        for k in range(n):
            hr = ins[k].shape[1] // 2
            theirs = pl.ds(pl.multiple_of((1 - c) * hr, 8), hr)
            cp = _remote(ins[k].at[:, theirs, :], outs[k], send_sems.at[k], recv_sems.at[k], (x, y, 1 - c))
            cp.start()
            copies.append(cp)
        for cp in copies:
            cp.wait()

    return pl.pallas_call(
        body, name="exchange_sibling_halves", in_specs=[HBM] * n, out_specs=[HBM] * n,
        out_shape=[jax.ShapeDtypeStruct((g.shape[0], g.shape[1] // 2, g.shape[2]), g.dtype) for g in grads],
        scratch_shapes=[pltpu.SemaphoreType.DMA((n,)), pltpu.SemaphoreType.DMA((n,))],
    )(*grads)


def exchange_chip_parts(parts):
    n = len(parts)

    def body(*refs):
        ins, outs = refs[:n], refs[n:2 * n]
        send_sems, recv_sems = refs[2 * n:]
        x, y, c, chips = _place()
        copies = []
        for k in range(n):
            for t, (px, py) in enumerate(chips):
                cp = _remote(ins[k].at[2 * px + py], outs[k].at[t], send_sems.at[k, t], recv_sems.at[k, t], (px, py, c))
                cp.start()
                copies.append(cp)
        for cp in copies:
            cp.wait()

    return pl.pallas_call(
        body, name="exchange_chip_parts", in_specs=[HBM] * n, out_specs=[HBM] * n,
        out_shape=[jax.ShapeDtypeStruct((3,) + p.shape[1:], p.dtype) for p in parts],
        scratch_shapes=[pltpu.SemaphoreType.DMA((n, 3)), pltpu.SemaphoreType.DMA((n, 3))],
    )(*parts)


def exchange_final_halves(halves):
    n = len(halves)

    def body(*refs):
        ins, outs = refs[:n], refs[n:2 * n]
        send_sems, recv_sems = refs[2 * n:]
        x, y, c, _ = _place()
        copies = []
        for k in range(n):
            cp = _remote(ins[k], outs[k], send_sems.at[k], recv_sems.at[k], (x, y, 1 - c))
            cp.start()
            copies.append(cp)
        for cp in copies:
            cp.wait()

    return pl.pallas_call(
        body, name="exchange_final_halves", in_specs=[HBM] * n, out_specs=[HBM] * n,
        out_shape=[jax.ShapeDtypeStruct(h.shape, h.dtype) for h in halves],
        scratch_shapes=[pltpu.SemaphoreType.DMA((n,)), pltpu.SemaphoreType.DMA((n,))],
    )(*halves)


def allreduce_small(packed):
    R = packed.shape[0]

    def body(x_ref, sum_ref, all_ref, send_sems, recv_sems):
        x, y, c, chips = _place()
        me, sibling = (x, y, c), (x, y, 1 - c)

        def rows(px, py, pc):
            return all_ref.at[4 * px + 2 * py + pc]

        def copy(k, block, to, src=None):
            return _remote(rows(*block) if src is None else src, rows(*block), send_sems.at[k], recv_sems.at[k], to)

        all_ref[4 * x + 2 * y + c] = x_ref[...]
        first = [copy(0, me, sibling, src=x_ref)]
        first += [copy(1 + j, me, (*chip, c), src=x_ref) for j, chip in enumerate(chips)]
        for cp in first:
            cp.start()
        passed = [copy(4 + j, (*chip, c), sibling) for j, chip in enumerate(chips)]
        for j, chip in enumerate(chips):
            copy(1 + j, (*chip, c), me).wait_recv()
            passed[j].start()
        copy(0, sibling, me).wait_recv()
        for j, chip in enumerate(chips):
            copy(4 + j, (*chip, 1 - c), me).wait_recv()
        for cp in first + passed:
            cp.wait_send()

        def chunk(i, carry):
            rws = pl.ds(pl.multiple_of(i * PACK_ROWS, PACK_ROWS), PACK_ROWS)
            acc = all_ref[0, rws, :]
            for d in range(1, N_DEV):
                acc = acc + all_ref[d, rws, :]
            sum_ref[rws, :] = acc
            return carry

        lax.fori_loop(0, R // PACK_ROWS, chunk, 0)

    return pl.pallas_call(
        body, name="allreduce_small", in_specs=[VMEM_SPEC], out_specs=VMEM_SPEC,
        out_shape=jax.ShapeDtypeStruct((R, LANES), F32),
        scratch_shapes=[pltpu.VMEM((N_DEV, R, LANES), F32), pltpu.SemaphoreType.DMA((7,)), pltpu.SemaphoreType.DMA((7,))],
        compiler_params=pltpu.CompilerParams(vmem_limit_bytes=VMEM_LIMIT),
    )(packed)


def _layer_fwd(x, p, tabs):
    h = rms_fwd(x, p["mix_norm_g"])
    z = mm_nn_cols(h, p["w_in"])
    qr, kp, vp = rope_fwd(z, tabs)
    attn = attn_fwd(qr, kp, vp, p["sink3"])
    c1 = conv_dw_fwd(z, p["conv_w32"], p["conv_dw_b"])
    conv = conv_ln_fwd(c1, p["conv_ln_g"], p["conv_ln_b"])
    sgu = sgu_fwd(z, p["sgu_ln_g"], p["sgu_ln_b"], p["sgu_w16"], p["sgu_b3"])
    mix = jnp.concatenate([attn, conv, sgu], axis=1)
    x_mid = mm_nn_rows_res(mix, p["w_out"], x)
    h2 = rms_fwd(x_mid, p["ffn_norm_g"])
    gate, up, act = ffn_up(h2, p["w_gate"], p["w_up"])
    x_out = mm_nn_rows_res(act, p["w_down"], x_mid)
    saved = dict(x=x, h=h, z=z, qr=qr, kp=kp, vp=vp, c1=c1, mix=mix, x_mid=x_mid, h2=h2, gate=gate, up=up, act=act)
    return x_out, saved


def _layer_bwd(dx, dxb, p, s, tabs):
    dgate, dup = ffn_down_bwd(dxb, p["w_down"], s["gate"], s["up"])
    g_down = mm_tn_rows(s["act"], dxb)
    dh2 = mm_nt_cols([(dgate, p["w_gate"]), (dup, p["w_up"])], F32)
    g_gate = mm_tn_cols(s["h2"], dgate, N_CHIPS)
    g_up = mm_tn_cols(s["h2"], dup, N_CHIPS)
    dmid, dmidb, g_ffn_norm = rms_bwd(s["x_mid"], p["ffn_norm_g"], dh2, dx)
    dmix = mm_nt_rows(dmidb, p["w_out"])
    g_out = mm_tn_rows(s["mix"], dmidb)
    dq, dkp, dvp, dsink = attn_bwd(s["qr"], s["kp"], s["vp"], p["sink3"], dmix)
    dqkv = rope_bwd(dq, dkp, dvp, tabs)
    dc1, g_cln_g, g_cln_b = conv_ln_bwd(dmix, s["c1"], p["conv_ln_g"], p["conv_ln_b"])
    dca, dcg, g_cw, g_cb = conv_dw_bwd(dc1, s["z"], p["conv_w32"])
    du, dv, g_sw, g_sb, g_sln_g, g_sln_b = sgu_bwd(s["z"], dmix, p["sgu_ln_g"], p["sgu_ln_b"], p["sgu_w16"], p["sgu_b3"])
    dz = jnp.concatenate([dqkv, dca, dcg, du, dv], axis=1)
    dh = mm_nt_cols([(dz, p["w_in"])], F32)
    g_in = mm_tn_cols(s["h"], dz, N_CHIPS)
    dx_in, dxb_in, g_mix_norm = rms_bwd(s["x"], p["mix_norm_g"], dh, dmid)
    big = [g_in, g_out.reshape(N_CHIPS, -1, D_MODEL), g_gate, g_up, g_down.reshape(N_CHIPS, -1, D_MODEL)]
    small = dict(mix_norm_g=g_mix_norm, sink=dsink[:, :, 0].reshape(1, N_Q_HEADS), conv_dw_w=g_cw[:CONV_KERNEL],
                 conv_dw_b=g_cb, conv_ln_g=g_cln_g, conv_ln_b=g_cln_b, sgu_ln_g=g_sln_g, sgu_ln_b=g_sln_b,
                 sgu_w=g_sw, sgu_b=g_sb[:, :, 0], ffn_norm_g=g_ffn_norm)
    return dx_in, dxb_in, big, small


def _layer_params(l, gathered, conv_w_full, mix_norm_g, sink, conv_dw_b, conv_ln_g, conv_ln_b, sgu_ln_g, sgu_ln_b, sgu_w,
                  sgu_b, ffn_norm_g):
    w_in, w_out, w_gate, w_up, w_down = gathered
    return dict(
        w_in=w_in, w_out=w_out.reshape(-1, D_MODEL), w_gate=w_gate, w_up=w_up, w_down=w_down.reshape(-1, D_MODEL),
        mix_norm_g=mix_norm_g[l:l + 1], ffn_norm_g=ffn_norm_g[l:l + 1],
        sink3=jnp.broadcast_to(sink[l].reshape(N_KV_HEADS, Q_PER_KV, 1), (N_KV_HEADS, Q_PER_KV, LANES)),
        conv_w32=jnp.pad(conv_w_full[l], ((0, 32 - CONV_KERNEL), (0, 0))),
        conv_dw_b=conv_dw_b[l:l + 1], conv_ln_g=conv_ln_g[l:l + 1], conv_ln_b=conv_ln_b[l:l + 1],
        sgu_ln_g=sgu_ln_g[l:l + 1], sgu_ln_b=sgu_ln_b[l:l + 1], sgu_w16=sgu_w[l].astype(BF16),
        sgu_b3=jnp.broadcast_to(sgu_b[l][:, :, None], (SGU_HEADS, CHUNK, CHUNK)))


_SMALL = ["mix_norm_g", "sink", "conv_dw_b", "conv_ln_g", "conv_ln_b", "sgu_ln_g", "sgu_ln_b", "sgu_w", "sgu_b", "ffn_norm_g",
          "final_norm_g"]


def _pack_rows(arrays):
    rows, counts = [], []
    for a in arrays:
        flat = a.reshape(-1)
        n = -(-flat.shape[0] // LANES)
        rows.append(jnp.pad(flat, (0, n * LANES - flat.shape[0])).reshape(n, LANES))
        counts.append(n)
    packed = jnp.concatenate(rows, axis=0)
    pad = -packed.shape[0] % PACK_ROWS
    return jnp.pad(packed, ((0, pad), (0, 0))), counts


def _unpack_rows(packed, counts, shapes):
    out, r = [], 0
    for n, shape in zip(counts, shapes):
        size = math.prod(shape)
        out.append(packed[r:r + n].reshape(-1)[:size].reshape(shape))
        r += n
    return out


def kernel(x, mix_norm_g, w_in, sink, conv_dw_w, conv_dw_b, conv_ln_g, conv_ln_b, sgu_ln_g, sgu_ln_b, sgu_w, sgu_b, w_out, ffn_norm_g, w_gate, w_up, w_down, final_norm_g, loss_target, m_mix_norm_g, m_w_in, m_sink, m_conv_dw_w, m_conv_dw_b, m_conv_ln_g, m_conv_ln_b, m_sgu_ln_g, m_sgu_ln_b, m_sgu_w, m_sgu_b, m_w_out, m_ffn_norm_g, m_w_gate, m_w_up, m_w_down, m_final_norm_g, v_mix_norm_g, v_w_in, v_sink, v_conv_dw_w, v_conv_dw_b, v_conv_ln_g, v_conv_ln_b, v_sgu_ln_g, v_sgu_ln_b, v_sgu_w, v_sgu_b, v_w_out, v_ffn_norm_g, v_w_gate, v_w_up, v_w_down, v_final_norm_g):
    S = x.shape[1]
    my_chip = 2 * lax.axis_index("x") + lax.axis_index("y")
    c_idx = lax.axis_index("c").astype(jnp.int32).reshape(1)
    big_w = [w_in, w_out, w_gate, w_up, w_down]
    big_m = [m_w_in, m_w_out, m_w_gate, m_w_up, m_w_down]
    big_v = [v_w_in, v_w_out, v_w_gate, v_w_up, v_w_down]
    n_kinds = len(big_w)

    gathered = [gather_weights([w[l].astype(BF16) for w in big_w]) for l in range(DEPTH)]
    conv_w_all = gather_small(conv_dw_w)
    conv_w_full = jnp.transpose(conv_w_all, (1, 2, 0, 3)).reshape(DEPTH, CONV_KERNEL, CONV_WIDTH)

    tabs = rope_tables(S)
    params = [_layer_params(l, gathered[l], conv_w_full, mix_norm_g, sink, conv_dw_b, conv_ln_g, conv_ln_b, sgu_ln_g,
                            sgu_ln_b, sgu_w, sgu_b, ffn_norm_g) for l in range(DEPTH)]

    act = x[0]
    saved = []
    for l in range(DEPTH):
        act, s = _layer_fwd(act, params[l], tabs)
        saved.append(s)
    loss_part, dx, dxb, g_final = final_loss(act, final_norm_g.reshape(1, D_MODEL), loss_target[0])
    loss = lax.psum(loss_part[0, 0], ("x", "y", "c"))

    halves = [jnp.zeros((DEPTH, w.shape[1] // 2, w.shape[2]), F32) for w in big_w]
    small_grads = [None] * DEPTH
    for l in reversed(range(DEPTH)):
        dx, dxb, big, small_grads[l] = _layer_bwd(dx, dxb, params[l], saved[l], tabs)
        recv = exchange_sibling_halves(big)
        chip_sum = [add_sibling_half(g, r, c_idx) for g, r in zip(big, recv)]
        others = exchange_chip_parts(chip_sum)
        chip_layer = jnp.stack([my_chip, l]).astype(jnp.int32)
        halves = [sum_chips(chip_sum[k], others[k], halves[k], chip_layer) for k in range(n_kinds)]
    sibling_halves = exchange_final_halves(halves)

    stacked = {n: jnp.stack([small_grads[l][n] for l in range(DEPTH)]) for n in small_grads[0]}
    stacked["final_norm_g"] = g_final
    packed, counts = _pack_rows([stacked[n] for n in _SMALL] + [stacked["conv_dw_w"]])
    reduced = allreduce_small(packed)
    small_w = dict(mix_norm_g=mix_norm_g, sink=sink, conv_dw_b=conv_dw_b, conv_ln_g=conv_ln_g, conv_ln_b=conv_ln_b,
                   sgu_ln_g=sgu_ln_g, sgu_ln_b=sgu_ln_b, sgu_w=sgu_w, sgu_b=sgu_b, ffn_norm_g=ffn_norm_g,
                   final_norm_g=final_norm_g)
    small_m = dict(mix_norm_g=m_mix_norm_g, sink=m_sink, conv_dw_b=m_conv_dw_b, conv_ln_g=m_conv_ln_g,
                   conv_ln_b=m_conv_ln_b, sgu_ln_g=m_sgu_ln_g, sgu_ln_b=m_sgu_ln_b, sgu_w=m_sgu_w, sgu_b=m_sgu_b,
                   ffn_norm_g=m_ffn_norm_g, final_norm_g=m_final_norm_g)
    small_v = dict(mix_norm_g=v_mix_norm_g, sink=v_sink, conv_dw_b=v_conv_dw_b, conv_ln_g=v_conv_ln_g,
                   conv_ln_b=v_conv_ln_b, sgu_ln_g=v_sgu_ln_g, sgu_ln_b=v_sgu_ln_b, sgu_w=v_sgu_w, sgu_b=v_sgu_b,
                   ffn_norm_g=v_ffn_norm_g, final_norm_g=v_final_norm_g)
    shapes = [small_w[n].shape for n in _SMALL] + [(DEPTH, CONV_KERNEL, CONV_WIDTH)]
    red = _unpack_rows(reduced, counts, shapes)
    g_small = dict(zip(_SMALL, red[:-1]))
    g_small["conv_dw_w"] = lax.dynamic_slice_in_dim(red[-1], my_chip * LANES, LANES, axis=2)
    small_w["conv_dw_w"], small_m["conv_dw_w"], small_v["conv_dw_w"] = conv_dw_w, m_conv_dw_w, v_conv_dw_w
    names = _SMALL + ["conv_dw_w"]
    pw, cnt = _pack_rows([small_w[n] for n in names])
    pg, _ = _pack_rows([g_small[n] for n in names])
    pm, _ = _pack_rows([small_m[n] for n in names])
    pv, _ = _pack_rows([small_v[n] for n in names])
    sd, sm, sv = adamw(pw, pg, pm, pv)
    shp = [small_w[n].shape for n in names]
    d_small = dict(zip(names, _unpack_rows(sd, cnt, shp)))
    m_small = dict(zip(names, _unpack_rows(sm, cnt, shp)))
    v_small = dict(zip(names, _unpack_rows(sv, cnt, shp)))

    big_names = ["w_in", "w_out", "w_gate", "w_up", "w_down"]
    g_big, d_big, m_big, v_big = {}, {}, {}, {}
    for k, n in enumerate(big_names):
        g_big[n], d_big[n], m_big[n], v_big[n] = adamw_halves(big_w[k], halves[k], sibling_halves[k], big_m[k], big_v[k], c_idx)

    order = ["mix_norm_g", "w_in", "sink", "conv_dw_w", "conv_dw_b", "conv_ln_g", "conv_ln_b", "sgu_ln_g", "sgu_ln_b",
             "sgu_w", "sgu_b", "w_out", "ffn_norm_g", "w_gate", "w_up", "w_down", "final_norm_g"]
    grads = {**g_small, **g_big}
    deltas = {**d_small, **d_big}
    new_m = {**m_small, **m_big}
    new_v = {**v_small, **v_big}
    return (loss, dx[None], *[grads[n] for n in order], *[deltas[n] for n in order],
            *[new_m[n] for n in order], *[new_v[n] for n in order])
```

```python
import functools
import math

import jax
import jax.numpy as jnp
from jax import lax
from jax.experimental import pallas as pl
from jax.experimental.pallas import tpu as pltpu

F32, BF16 = jnp.float32, jnp.bfloat16

D_MODEL = 2048
DEPTH = 4
HEAD_DIM = 128
N_Q_HEADS = 8
N_KV_HEADS = 2
Q_PER_KV = N_Q_HEADS // N_KV_HEADS
ATTN_WIDTH = N_Q_HEADS * HEAD_DIM
KV_WIDTH = N_KV_HEADS * HEAD_DIM
CONV_WIDTH = 512
CONV_KERNEL = 31
CONV_PAD = 16
SGU_WIDTH = 512
SGU_HEADS = 4
CHUNK = 128
IN_WIDTH = 3584
D_FF = 5632
WINDOW = 128
ROT_DIM = 32
ROPE_THETA = 500000.0
EPS = 1e-6
N_CHIPS = 4
N_DEV = 8
LANES = 128
PACK_ROWS = 64
OFF_K = ATTN_WIDTH
OFF_V = OFF_K + KV_WIDTH
OFF_CA = OFF_V + KV_WIDTH
OFF_CG = OFF_CA + CONV_WIDTH
OFF_U = OFF_CG + CONV_WIDTH
OFF_VV = OFF_U + SGU_WIDTH

ADAM_LR, ADAM_B1, ADAM_B2, ADAM_EPS, ADAM_WD, ADAM_STEP = 0.001, 0.9, 0.999, 1e-08, 0.01, 10

VMEM_LIMIT = 56 * 1024 * 1024
MESH = pl.DeviceIdType.MESH
HBM = pl.BlockSpec(memory_space=pltpu.HBM)
VMEM_SPEC = pl.BlockSpec(memory_space=pltpu.VMEM)


def _call(name, body, *, grid, in_specs, out_specs, out_shape, scratch=(), sem=None):
    params = dict(vmem_limit_bytes=VMEM_LIMIT)
    if sem is not None:
        params["dimension_semantics"] = sem
    return pl.pallas_call(
        body, name=name, grid=grid, in_specs=in_specs, out_specs=out_specs, out_shape=out_shape,
        scratch_shapes=list(scratch), compiler_params=pltpu.CompilerParams(**params))


def _sigmoid(x):
    return 1.0 / (1.0 + jnp.exp(-x))


def rms_fwd(x, g, token):
    S = x.shape[0]
    tm = min(512, S)

    def body(x_ref, g_ref, token_ref, o_ref):
        xv = x_ref[...]
        r = lax.rsqrt(jnp.mean(xv * xv, axis=-1, keepdims=True) + EPS)
        o_ref[...] = (xv * r * g_ref[...]).astype(BF16)

    return _call("rms_fwd", body, grid=(S // tm,),
                 in_specs=[pl.BlockSpec((tm, D_MODEL), lambda i: (i, 0)), pl.BlockSpec((1, D_MODEL), lambda i: (0, 0)),
                           pl.BlockSpec((8, LANES), lambda i: (0, 0))],
                 out_specs=pl.BlockSpec((tm, D_MODEL), lambda i: (i, 0)),
                 out_shape=jax.ShapeDtypeStruct((S, D_MODEL), BF16), sem=("parallel",))(x, g, token)


def _rms_bwd_math(xv, gv, dh):
    r = lax.rsqrt(jnp.mean(xv * xv, axis=-1, keepdims=True) + EPS)
    n = xv * r
    dn = dh * gv
    dx = r * (dn - n * jnp.mean(dn * n, axis=-1, keepdims=True))
    dg = jnp.sum(dh * n, axis=0, keepdims=True)
    return dx, dg


def rms_bwd(x, g, dh, dres):
    S = x.shape[0]
    tm = min(256, S)

    def body(x_ref, g_ref, dh_ref, dres_ref, dx_ref, dxb_ref, dg_ref):
        dx, dg = _rms_bwd_math(x_ref[...], g_ref[...], dh_ref[...])
        dx = dx + dres_ref[...]
        dx_ref[...] = dx
        dxb_ref[...] = dx.astype(BF16)

        @pl.when(pl.program_id(0) == 0)
        def _():
            dg_ref[...] = dg

        @pl.when(pl.program_id(0) > 0)
        def _():
            dg_ref[...] += dg

    row = pl.BlockSpec((tm, D_MODEL), lambda i: (i, 0))
    vec = pl.BlockSpec((1, D_MODEL), lambda i: (0, 0))
    return _call("rms_bwd", body, grid=(S // tm,), in_specs=[row, vec, row, row], out_specs=[row, row, vec],
                 out_shape=[jax.ShapeDtypeStruct((S, D_MODEL), F32), jax.ShapeDtypeStruct((S, D_MODEL), BF16),
                            jax.ShapeDtypeStruct((1, D_MODEL), F32)], sem=("arbitrary",))(x, g, dh, dres)


def final_loss(x, g, target):
    S = x.shape[0]
    tm = min(256, S)

    def body(x_ref, g_ref, t_ref, loss_ref, dx_ref, dxb_ref, dg_ref):
        xv = x_ref[...]
        gv = g_ref[...]
        r = lax.rsqrt(jnp.mean(xv * xv, axis=-1, keepdims=True) + EPS)
        err = xv * r * gv - t_ref[...]
        part = 0.5 * jnp.sum(jnp.mean(err * err, axis=-1, keepdims=True), axis=0, keepdims=True)
        dx, dg = _rms_bwd_math(xv, gv, err * (1.0 / D_MODEL))
        dx_ref[...] = dx
        dxb_ref[...] = dx.astype(BF16)

        @pl.when(pl.program_id(0) == 0)
        def _():
            dg_ref[...] = dg
            loss_ref[...] = part

        @pl.when(pl.program_id(0) > 0)
        def _():
            dg_ref[...] += dg
            loss_ref[...] += part

    row = pl.BlockSpec((tm, D_MODEL), lambda i: (i, 0))
    vec = pl.BlockSpec((1, D_MODEL), lambda i: (0, 0))
    one = pl.BlockSpec((1, 1), lambda i: (0, 0))
    return _call("final_loss", body, grid=(S // tm,), in_specs=[row, vec, row], out_specs=[one, row, row, vec],
                 out_shape=[jax.ShapeDtypeStruct((1, 1), F32), jax.ShapeDtypeStruct((S, D_MODEL), F32),
                            jax.ShapeDtypeStruct((S, D_MODEL), BF16), jax.ShapeDtypeStruct((1, D_MODEL), F32)],
                 sem=("arbitrary",))(x, g, target)


NN = (((1,), (0,)), ((), ()))
NT = (((1,), (1,)), ((), ()))
TN = (((0,), (0,)), ((), ()))


def _matmul(name, operands, in_specs, out_shape, out_specs, grid, pairs, dims, acc_shape, epilogue):
    n_in, n_out, nk = len(operands), len(out_shape), grid[-1]

    def body(*refs):
        ins, outs = refs[:n_in], refs[n_in:n_in + n_out]
        part = None
        for ia, ib in pairs:
            d = lax.dot_general(ins[ia][...], ins[ib][...], dims, preferred_element_type=F32)
            part = d if part is None else part + d
        if nk == 1:
            epilogue(part, ins, outs)
        else:
            acc = refs[-1]
            k = pl.program_id(len(grid) - 1)

            @pl.when(k == 0)
            def _():
                acc[...] = part

            @pl.when(k > 0)
            def _():
                acc[...] += part

            @pl.when(k == nk - 1)
            def _():
                epilogue(acc[...], ins, outs)

    scratch = [pltpu.VMEM(acc_shape, F32)] if nk > 1 else []
    sem = ("parallel",) * (len(grid) - 1) + ("arbitrary",)
    return _call(name, body, grid=grid, in_specs=in_specs, out_specs=out_specs, out_shape=out_shape,
                 scratch=scratch, sem=sem)(*operands)


def _store(dtype):
    def epilogue(acc, ins, outs):
        outs[0][...] = acc.astype(dtype)
    return epilogue


def mm_nn_cols(a, w):
    S, K = a.shape
    J, _, Ns = w.shape
    tm = min(512, S)
    return _matmul("mm_nn_cols", (a, w),
                   [pl.BlockSpec((tm, K), lambda j, i, k: (i, 0)), pl.BlockSpec((None, K, Ns), lambda j, i, k: (j, 0, 0))],
                   [jax.ShapeDtypeStruct((S, J * Ns), BF16)], [pl.BlockSpec((tm, Ns), lambda j, i, k: (i, j))],
                   (J, S // tm, 1), [(0, 1)], NN, None, _store(BF16))[0]


def ffn_up(h, wg, wu):
    S, K = h.shape
    J, _, Ns = wg.shape
    tm = min(512, S)

    def body(h_ref, wg_ref, wu_ref, g_ref, u_ref, a_ref):
        hv = h_ref[...]
        gv = jnp.dot(hv, wg_ref[...], preferred_element_type=F32)
        uv = jnp.dot(hv, wu_ref[...], preferred_element_type=F32)
        g_ref[...] = gv.astype(BF16)
        u_ref[...] = uv.astype(BF16)
        a_ref[...] = (gv * _sigmoid(gv) * uv).astype(BF16)

    wspec = pl.BlockSpec((None, K, Ns), lambda j, i: (j, 0, 0))
    ospec = pl.BlockSpec((tm, Ns), lambda j, i: (i, j))
    oshape = jax.ShapeDtypeStruct((S, J * Ns), BF16)
    return _call("ffn_up", body, grid=(J, S // tm), in_specs=[pl.BlockSpec((tm, K), lambda j, i: (i, 0)), wspec, wspec],
                 out_specs=[ospec, ospec, ospec], out_shape=[oshape, oshape, oshape], sem=("parallel", "parallel"))(h, wg, wu)


def mm_nn_rows_res(a, w, res):
    S, K = a.shape
    N = w.shape[1]
    tm, tk = min(1024, S), 512

    def epilogue(acc, ins, outs):
        outs[0][...] = acc + ins[2][...]

    return _matmul("mm_nn_rows_res", (a, w, res),
                   [pl.BlockSpec((tm, tk), lambda i, k: (i, k)), pl.BlockSpec((tk, N), lambda i, k: (k, 0)),
                    pl.BlockSpec((tm, N), lambda i, k: (i, 0))],
                   [jax.ShapeDtypeStruct((S, N), F32)], [pl.BlockSpec((tm, N), lambda i, k: (i, 0))],
                   (S // tm, K // tk), [(0, 1)], NN, (tm, N), epilogue)[0]


def mm_nt_cols(pairs_in, out_dtype):
    dz0, w0 = pairs_in[0]
    S = dz0.shape[0]
    J, K, Ns = w0.shape
    tm = min(512, S)
    operands, specs, pairs = [], [], []
    for dz, w in pairs_in:
        pairs.append((len(operands), len(operands) + 1))
        operands += [dz, w]
        specs += [pl.BlockSpec((tm, Ns), lambda i, j: (i, j)), pl.BlockSpec((None, K, Ns), lambda i, j: (j, 0, 0))]
    return _matmul("mm_nt_cols%d" % len(pairs_in), tuple(operands), specs,
                   [jax.ShapeDtypeStruct((S, K), out_dtype)], [pl.BlockSpec((tm, K), lambda i, j: (i, 0))],
                   (S // tm, J), pairs, NT, (tm, K), _store(out_dtype))[0]


def mm_nt_rows(dy, w):
    S, N = dy.shape
    K = w.shape[0]
    tm, tko = min(1024, S), 512
    return _matmul("mm_nt_rows", (dy, w),
                   [pl.BlockSpec((tm, N), lambda i, kk, z: (i, 0)), pl.BlockSpec((tko, N), lambda i, kk, z: (kk, 0))],
                   [jax.ShapeDtypeStruct((S, K), BF16)], [pl.BlockSpec((tm, tko), lambda i, kk, z: (i, kk))],
                   (S // tm, K // tko, 1), [(0, 1)], NT, None, _store(BF16))[0]


def ffn_down_bwd(dy, w, gate, up, token):
    S, N = dy.shape
    K = w.shape[0]
    tm, tko = min(1024, S), 512

    def epilogue(acc, ins, outs):
        gv = ins[2][...].astype(F32)
        uv = ins[3][...].astype(F32)
        sg = _sigmoid(gv)
        outs[0][...] = (acc * uv * sg * (1.0 + gv * (1.0 - sg))).astype(BF16)
        outs[1][...] = (acc * gv * sg).astype(BF16)

    tile = pl.BlockSpec((tm, tko), lambda i, kk, z: (i, kk))
    oshape = jax.ShapeDtypeStruct((S, K), BF16)
    return _matmul("ffn_down_bwd", (dy, w, gate, up, token),
                   [pl.BlockSpec((tm, N), lambda i, kk, z: (i, 0)), pl.BlockSpec((tko, N), lambda i, kk, z: (kk, 0)), tile, tile,
                    pl.BlockSpec((8, LANES), lambda i, kk, z: (0, 0))],
                   [oshape, oshape], [tile, tile], (S // tm, K // tko, 1), [(0, 1)], NT, None, epilogue)


def mm_tn_cols(a, dz, J):
    S, M = a.shape
    Ns = dz.shape[1] // J
    tm, tk = min(1024, M), min(512, S)
    return _matmul("mm_tn_cols", (a, dz),
                   [pl.BlockSpec((tk, tm), lambda j, m, k: (k, m)), pl.BlockSpec((tk, Ns), lambda j, m, k: (k, j))],
                   [jax.ShapeDtypeStruct((J, M, Ns), BF16)], [pl.BlockSpec((None, tm, Ns), lambda j, m, k: (j, m, 0))],
                   (J, M // tm, S // tk), [(0, 1)], TN, (tm, Ns), _store(BF16))[0]


def mm_tn_rows(a, dy):
    S, K = a.shape
    N = dy.shape[1]
    tm, tk = 512, min(512, S)
    return _matmul("mm_tn_rows", (a, dy),
                   [pl.BlockSpec((tk, tm), lambda m, k: (k, m)), pl.BlockSpec((tk, N), lambda m, k: (k, 0))],
                   [jax.ShapeDtypeStruct((K, N), BF16)], [pl.BlockSpec((tm, N), lambda m, k: (m, 0))],
                   (K // tm, S // tk), [(0, 1)], TN, (tm, N), _store(BF16))[0]


def rope_tables(S):
    half = ROT_DIM // 2
    pos = jnp.arange(S, dtype=F32)
    inv = ROPE_THETA ** (-jnp.arange(0, ROT_DIM, 2, dtype=F32) / ROT_DIM)
    ang = pos[:, None] * inv[None, :]
    cos, sin = jnp.cos(ang), jnp.sin(ang)
    zeros = jnp.zeros((S, HEAD_DIM - ROT_DIM), F32)
    c = jnp.concatenate([cos, cos, jnp.ones((S, HEAD_DIM - ROT_DIM), F32)], axis=1)
    s_lo = jnp.concatenate([-sin, jnp.zeros((S, half), F32), zeros], axis=1)
    s_hi = jnp.concatenate([jnp.zeros((S, half), F32), sin, zeros], axis=1)
    return c, s_lo, s_hi


def _rope(t, c, s_lo, s_hi):
    half = ROT_DIM // 2
    return t * c + pltpu.roll(t, HEAD_DIM - half, 1) * s_lo + pltpu.roll(t, half, 1) * s_hi


def _unrope(d, c, s_lo, s_hi):
    half = ROT_DIM // 2
    return d * c + pltpu.roll(d * s_lo, half, 1) + pltpu.roll(d * s_hi, HEAD_DIM - half, 1)


def rope_fwd(z, tabs):
    S = z.shape[0]
    nb = S // CHUNK

    def body(q_ref, kv_ref, c_ref, sl_ref, sh_ref, qr_ref, kp_ref, vp_ref):
        i = pl.program_id(0)

        @pl.when(i == 0)
        def _():
            zero = jnp.zeros((CHUNK, KV_WIDTH), BF16)
            kp_ref[0:CHUNK, :] = zero
            vp_ref[0:CHUNK, :] = zero
            kp_ref[S + CHUNK:S + 2 * CHUNK, :] = zero
            vp_ref[S + CHUNK:S + 2 * CHUNK, :] = zero

        c, sl, sh = c_ref[...], sl_ref[...], sh_ref[...]
        for h in range(N_Q_HEADS):
            cols = slice(h * HEAD_DIM, (h + 1) * HEAD_DIM)
            qr_ref[:, cols] = _rope(q_ref[:, cols].astype(F32), c, sl, sh).astype(BF16)
        rows = pl.ds(pl.multiple_of(CHUNK + i * CHUNK, CHUNK), CHUNK)
        for g in range(N_KV_HEADS):
            cols = slice(g * HEAD_DIM, (g + 1) * HEAD_DIM)
            kp_ref[rows, cols] = _rope(kv_ref[:, cols].astype(F32), c, sl, sh).astype(BF16)
        vp_ref[rows, :] = kv_ref[:, KV_WIDTH:2 * KV_WIDTH]

    tab = pl.BlockSpec((CHUNK, HEAD_DIM), lambda i: (i, 0))
    pad = pl.BlockSpec((S + 2 * CHUNK, KV_WIDTH), lambda i: (0, 0))
    return _call("rope_fwd", body, grid=(nb,),
                 in_specs=[pl.BlockSpec((CHUNK, ATTN_WIDTH), lambda i: (i, 0)),
                           pl.BlockSpec((CHUNK, 2 * KV_WIDTH), lambda i: (i, OFF_K // (2 * KV_WIDTH))), tab, tab, tab],
                 out_specs=[pl.BlockSpec((CHUNK, ATTN_WIDTH), lambda i: (i, 0)), pad, pad],
                 out_shape=[jax.ShapeDtypeStruct((S, ATTN_WIDTH), BF16), jax.ShapeDtypeStruct((S + 2 * CHUNK, KV_WIDTH), BF16),
                            jax.ShapeDtypeStruct((S + 2 * CHUNK, KV_WIDTH), BF16)], sem=("arbitrary",))(z, z, *tabs)


def rope_bwd(dq, dkp, dvp, tabs):
    S = dq.shape[0]

    def body(dq_ref, dk_ref, dv_ref, c_ref, sl_ref, sh_ref, o_ref):
        c, sl, sh = c_ref[...], sl_ref[...], sh_ref[...]
        for h in range(N_Q_HEADS):
            cols = slice(h * HEAD_DIM, (h + 1) * HEAD_DIM)
            o_ref[:, cols] = _unrope(dq_ref[:, cols], c, sl, sh).astype(BF16)
        for g in range(N_KV_HEADS):
            cols = slice(g * HEAD_DIM, (g + 1) * HEAD_DIM)
            o_ref[:, OFF_K + g * HEAD_DIM:OFF_K + (g + 1) * HEAD_DIM] = _unrope(dk_ref[:, cols], c, sl, sh).astype(BF16)
        o_ref[:, OFF_V:OFF_V + KV_WIDTH] = dv_ref[...].astype(BF16)

    tab = pl.BlockSpec((CHUNK, HEAD_DIM), lambda i: (i, 0))
    pad = pl.BlockSpec((CHUNK, KV_WIDTH), lambda i: (i + 1, 0))
    return _call("rope_bwd", body, grid=(S // CHUNK,),
                 in_specs=[pl.BlockSpec((CHUNK, ATTN_WIDTH), lambda i: (i, 0)), pad, pad, tab, tab, tab],
                 out_specs=pl.BlockSpec((CHUNK, OFF_CA), lambda i: (i, 0)),
                 out_shape=jax.ShapeDtypeStruct((S, OFF_CA), BF16), sem=("parallel",))(dq, dkp, dvp, *tabs)


def _attn_probs(q, kb, sk, n, S):
    scale = 1.0 / math.sqrt(HEAD_DIM)
    s = lax.dot_general(q, kb, NT, preferred_element_type=F32) * scale
    row = lax.broadcasted_iota(jnp.int32, (CHUNK, 3 * CHUNK), 0)
    col = lax.broadcasted_iota(jnp.int32, (CHUNK, 3 * CHUNK), 1)
    kpos = (n - 1) * CHUNK + col
    valid = (jnp.abs(col - CHUNK - row) <= WINDOW) & (kpos >= 0) & (kpos < S)
    s = jnp.where(valid, s, jnp.finfo(F32).min)
    m = jnp.maximum(jnp.max(s, axis=1, keepdims=True), sk)
    e = jnp.exp(s - m)
    es = jnp.exp(sk - m)
    inv = 1.0 / (jnp.sum(e, axis=1, keepdims=True) + es)
    return e * inv, es * inv


def attn_fwd(qr, kp, vp, sink3):
    S = qr.shape[0]
    tq = min(512, S)
    gw = Q_PER_KV * HEAD_DIM

    def body(q_ref, k_ref, v_ref, s_ref, o_ref):
        i = pl.program_id(1)
        for b in range(tq // CHUNK):
            n = i * (tq // CHUNK) + b
            win = pl.ds(pl.multiple_of(n * CHUNK, CHUNK), 3 * CHUNK)
            kb, vb = k_ref[win, :], v_ref[win, :]
            for r in range(Q_PER_KV):
                rows, cols = slice(b * CHUNK, (b + 1) * CHUNK), slice(r * HEAD_DIM, (r + 1) * HEAD_DIM)
                p, _ = _attn_probs(q_ref[rows, cols], kb, s_ref[r:r + 1, 0:1], n, S)
                o_ref[rows, cols] = jnp.dot(p.astype(BF16), vb, preferred_element_type=F32).astype(BF16)

    kv = pl.BlockSpec((S + 2 * CHUNK, HEAD_DIM), lambda g, i: (0, g))
    return _call("attn_fwd", body, grid=(N_KV_HEADS, S // tq),
                 in_specs=[pl.BlockSpec((tq, gw), lambda g, i: (i, g)), kv, kv,
                           pl.BlockSpec((None, Q_PER_KV, LANES), lambda g, i: (g, 0, 0))],
                 out_specs=pl.BlockSpec((tq, gw), lambda g, i: (i, g)),
                 out_shape=jax.ShapeDtypeStruct((S, ATTN_WIDTH), BF16), sem=("parallel", "arbitrary"))(qr, kp, vp, sink3)


def attn_bwd(qr, kp, vp, sink3, dmix):
    S = qr.shape[0]
    tq = min(512, S)
    gw = Q_PER_KV * HEAD_DIM
    scale = 1.0 / math.sqrt(HEAD_DIM)

    def body(q_ref, k_ref, v_ref, s_ref, do_ref, dq_ref, dk_ref, dv_ref, ds_ref):
        i = pl.program_id(1)

        @pl.when(i == 0)
        def _():
            dk_ref[...] = jnp.zeros_like(dk_ref)
            dv_ref[...] = jnp.zeros_like(dv_ref)
            ds_ref[...] = jnp.zeros_like(ds_ref)

        for b in range(tq // CHUNK):
            n = i * (tq // CHUNK) + b
            win = pl.ds(pl.multiple_of(n * CHUNK, CHUNK), 3 * CHUNK)
            kb, vb = k_ref[win, :], v_ref[win, :]
            dk_acc = jnp.zeros((3 * CHUNK, HEAD_DIM), F32)
            dv_acc = jnp.zeros((3 * CHUNK, HEAD_DIM), F32)
            for r in range(Q_PER_KV):
                rows, cols = slice(b * CHUNK, (b + 1) * CHUNK), slice(r * HEAD_DIM, (r + 1) * HEAD_DIM)
                q = q_ref[rows, cols]
                do = do_ref[rows, cols]
                p, p_sink = _attn_probs(q, kb, s_ref[r:r + 1, 0:1], n, S)
                dp = lax.dot_general(do, vb, NT, preferred_element_type=F32)
                delta = jnp.sum(p * dp, axis=1, keepdims=True)
                dsc = (p * (dp - delta) * scale).astype(BF16)
                dq_ref[rows, cols] = jnp.dot(dsc, kb, preferred_element_type=F32)
                dk_acc = dk_acc + lax.dot_general(dsc, q, TN, preferred_element_type=F32)
                dv_acc = dv_acc + lax.dot_general(p.astype(BF16), do, TN, preferred_element_type=F32)
                dsink = jnp.sum(-p_sink * delta, axis=0, keepdims=True)
                ds_ref[r:r + 1, :] += jnp.broadcast_to(dsink, (1, LANES))
            dk_ref[win, :] += dk_acc
            dv_ref[win, :] += dv_acc

    kv = pl.BlockSpec((S + 2 * CHUNK, HEAD_DIM), lambda g, i: (0, g))
    qspec = pl.BlockSpec((tq, gw), lambda g, i: (i, g))
    sspec = pl.BlockSpec((None, Q_PER_KV, LANES), lambda g, i: (g, 0, 0))
    padshape = jax.ShapeDtypeStruct((S + 2 * CHUNK, KV_WIDTH), F32)
    return _call("attn_bwd", body, grid=(N_KV_HEADS, S // tq),
                 in_specs=[qspec, kv, kv, sspec, qspec],
                 out_specs=[qspec, kv, kv, sspec],
                 out_shape=[jax.ShapeDtypeStruct((S, ATTN_WIDTH), F32), padshape, padshape,
                            jax.ShapeDtypeStruct((N_KV_HEADS, Q_PER_KV, LANES), F32)],
                 sem=("parallel", "arbitrary"))(qr, kp, vp, sink3, dmix)


CONV_TILE = 256


def _fill_padded(dst_ref, value, S):
    zero = jnp.zeros((CONV_PAD, LANES), F32)
    dst_ref[0:CONV_PAD, :] = zero
    dst_ref[CONV_PAD + S:2 * CONV_PAD + S, :] = zero
    dst_ref[CONV_PAD:CONV_PAD + S, :] = value


def conv_dw_fwd(z, w32, b):
    S = z.shape[0]
    T = min(CONV_TILE, S)
    lo = CONV_PAD - (CONV_KERNEL - 1) // 2

    def body(a_ref, g_ref, w_ref, b_ref, o_ref, c0_ref):
        _fill_padded(c0_ref, a_ref[...].astype(F32) * _sigmoid(g_ref[...].astype(F32)), S)

        def tile(t, carry):
            base = pl.multiple_of(t * T, T)
            acc = jnp.broadcast_to(b_ref[...], (T, LANES))
            for j in range(CONV_KERNEL):
                acc = acc + w_ref[j:j + 1, :] * c0_ref[pl.ds(base + lo + j, T), :]
            o_ref[pl.ds(base, T), :] = acc
            return carry

        lax.fori_loop(0, S // T, tile, 0)

    nca, ncg = OFF_CA // LANES, OFF_CG // LANES
    return _call("conv_dw_fwd", body, grid=(CONV_WIDTH // LANES,),
                 in_specs=[pl.BlockSpec((S, LANES), lambda cb: (0, nca + cb)), pl.BlockSpec((S, LANES), lambda cb: (0, ncg + cb)),
                           pl.BlockSpec((32, LANES), lambda cb: (0, cb)), pl.BlockSpec((1, LANES), lambda cb: (0, cb))],
                 out_specs=pl.BlockSpec((S, LANES), lambda cb: (0, cb)),
                 out_shape=jax.ShapeDtypeStruct((S, CONV_WIDTH), F32),
                 scratch=[pltpu.VMEM((S + 2 * CONV_PAD, LANES), F32)], sem=("parallel",))(z, z, w32, b)


def _ln_stats(x):
    mu = jnp.mean(x, axis=-1, keepdims=True)
    xc = x - mu
    rs = lax.rsqrt(jnp.mean(xc * xc, axis=-1, keepdims=True) + EPS)
    return xc * rs, rs


def _ln_bwd(dy, xh, rs, g):
    dxh = dy * g
    return rs * (dxh - jnp.mean(dxh, axis=-1, keepdims=True) - xh * jnp.mean(dxh * xh, axis=-1, keepdims=True))


def conv_ln_fwd(c1, g, b):
    S = c1.shape[0]
    T = min(512, S)

    def body(x_ref, g_ref, b_ref, o_ref):
        xh, _ = _ln_stats(x_ref[...])
        y = xh * g_ref[...] + b_ref[...]
        o_ref[...] = (y * _sigmoid(y)).astype(BF16)

    row = pl.BlockSpec((T, CONV_WIDTH), lambda i: (i, 0))
    vec = pl.BlockSpec((1, CONV_WIDTH), lambda i: (0, 0))
    return _call("conv_ln_fwd", body, grid=(S // T,), in_specs=[row, vec, vec], out_specs=row,
                 out_shape=jax.ShapeDtypeStruct((S, CONV_WIDTH), BF16), sem=("parallel",))(c1, g, b)


def _acc_out(ref, value):
    @pl.when(pl.program_id(0) == 0)
    def _():
        ref[...] = value

    @pl.when(pl.program_id(0) > 0)
    def _():
        ref[...] += value


def conv_ln_bwd(dmix, c1, g, b):
    S = c1.shape[0]
    T = min(512, S)

    def body(d_ref, x_ref, g_ref, b_ref, dx_ref, dg_ref, db_ref):
        xh, rs = _ln_stats(x_ref[...])
        gv = g_ref[...]
        y = xh * gv + b_ref[...]
        sg = _sigmoid(y)
        dy = d_ref[...].astype(F32) * sg * (1.0 + y * (1.0 - sg))
        dx_ref[...] = _ln_bwd(dy, xh, rs, gv)
        _acc_out(dg_ref, jnp.sum(dy * xh, axis=0, keepdims=True))
        _acc_out(db_ref, jnp.sum(dy, axis=0, keepdims=True))

    row = pl.BlockSpec((T, CONV_WIDTH), lambda i: (i, 0))
    vec = pl.BlockSpec((1, CONV_WIDTH), lambda i: (0, 0))
    vshape = jax.ShapeDtypeStruct((1, CONV_WIDTH), F32)
    return _call("conv_ln_bwd", body, grid=(S // T,),
                 in_specs=[pl.BlockSpec((T, CONV_WIDTH), lambda i: (i, ATTN_WIDTH // CONV_WIDTH)), row, vec, vec],
                 out_specs=[row, vec, vec], out_shape=[jax.ShapeDtypeStruct((S, CONV_WIDTH), F32), vshape, vshape],
                 sem=("arbitrary",))(dmix, c1, g, b)


def conv_dw_bwd(dc1, z, w32):
    S = z.shape[0]
    T = min(CONV_TILE, S)
    half = (CONV_KERNEL - 1) // 2
    lo = CONV_PAD - half

    def body(d_ref, a_ref, g_ref, w_ref, da_ref, dg_ref, dw_ref, db_ref, c0_ref, d1_ref, wacc_ref):
        av = a_ref[...].astype(F32)
        sg = _sigmoid(g_ref[...].astype(F32))
        _fill_padded(c0_ref, av * sg, S)
        _fill_padded(d1_ref, d_ref[...], S)
        wacc_ref[...] = jnp.zeros_like(wacc_ref)

        def tile(t, carry):
            base = pl.multiple_of(t * T, T)
            d1 = d_ref[pl.ds(base, T), :]
            acc = jnp.zeros((T, LANES), F32)
            for j in range(CONV_KERNEL):
                acc = acc + w_ref[j:j + 1, :] * d1_ref[pl.ds(base + CONV_PAD + half - j, T), :]
                prod = d1 * c0_ref[pl.ds(base + lo + j, T), :]
                wacc_ref[j] += jnp.sum(prod.reshape(T // 8, 8, LANES), axis=0)
            rows = pl.ds(base, T)
            a_t = a_ref[rows, :].astype(F32)
            s_t = _sigmoid(g_ref[rows, :].astype(F32))
            da_ref[rows, :] = (acc * s_t).astype(BF16)
            dg_ref[rows, :] = (acc * a_t * s_t * (1.0 - s_t)).astype(BF16)
            return carry

        lax.fori_loop(0, S // T, tile, 0)
        dw_ref[...] = jnp.sum(wacc_ref[...], axis=1)
        db_ref[...] = jnp.sum(d_ref[...], axis=0, keepdims=True)

    nca, ncg = OFF_CA // LANES, OFF_CG // LANES
    col = pl.BlockSpec((S, LANES), lambda cb: (0, cb))
    oshape = jax.ShapeDtypeStruct((S, CONV_WIDTH), BF16)
    return _call("conv_dw_bwd", body, grid=(CONV_WIDTH // LANES,),
                 in_specs=[col, pl.BlockSpec((S, LANES), lambda cb: (0, nca + cb)), pl.BlockSpec((S, LANES), lambda cb: (0, ncg + cb)),
                           pl.BlockSpec((32, LANES), lambda cb: (0, cb))],
                 out_specs=[col, col, pl.BlockSpec((32, LANES), lambda cb: (0, cb)), pl.BlockSpec((1, LANES), lambda cb: (0, cb))],
                 out_shape=[oshape, oshape, jax.ShapeDtypeStruct((32, CONV_WIDTH), F32), jax.ShapeDtypeStruct((1, CONV_WIDTH), F32)],
                 scratch=[pltpu.VMEM((S + 2 * CONV_PAD, LANES), F32), pltpu.VMEM((S + 2 * CONV_PAD, LANES), F32),
                          pltpu.VMEM((32, 8, LANES), F32)], sem=("parallel",))(dc1, z, z, w32)


_INV_SQRT2 = 1.0 / math.sqrt(2.0)
_INV_SQRT2PI = 1.0 / math.sqrt(2.0 * math.pi)


def _gelu(x):
    return 0.5 * x * (1.0 + lax.erf(x * _INV_SQRT2))


def _gelu_grad(x):
    return 0.5 * (1.0 + lax.erf(x * _INV_SQRT2)) + x * jnp.exp(-0.5 * x * x) * _INV_SQRT2PI


def sgu_fwd(z, g, b, ws, bs):
    S = z.shape[0]
    T = min(512, S)

    def body(u_ref, v_ref, g_ref, b_ref, ws_ref, bs_ref, o_ref):
        xh, _ = _ln_stats(_gelu(v_ref[...].astype(F32)))
        vn = (xh * g_ref[...] + b_ref[...]).astype(BF16)
        for ch in range(T // CHUNK):
            rows = slice(ch * CHUNK, (ch + 1) * CHUNK)
            for h in range(SGU_HEADS):
                cols = slice(h * HEAD_DIM, (h + 1) * HEAD_DIM)
                sp = jnp.dot(ws_ref[h], vn[rows, cols], preferred_element_type=F32) + bs_ref[h]
                o_ref[rows, cols] = (_gelu(u_ref[rows, cols].astype(F32)) * sp).astype(BF16)

    vec = pl.BlockSpec((1, SGU_WIDTH), lambda i: (0, 0))
    full = pl.BlockSpec((SGU_HEADS, CHUNK, CHUNK), lambda i: (0, 0, 0))
    return _call("sgu_fwd", body, grid=(S // T,),
                 in_specs=[pl.BlockSpec((T, SGU_WIDTH), lambda i: (i, OFF_U // SGU_WIDTH)),
                           pl.BlockSpec((T, SGU_WIDTH), lambda i: (i, OFF_VV // SGU_WIDTH)), vec, vec, full, full],
                 out_specs=pl.BlockSpec((T, SGU_WIDTH), lambda i: (i, 0)),
                 out_shape=jax.ShapeDtypeStruct((S, SGU_WIDTH), BF16), sem=("parallel",))(z, z, g, b, ws, bs)


def sgu_bwd(z, dmix, g, b, ws, bs):
    S = z.shape[0]
    T = min(512, S)

    def body(u_ref, v_ref, d_ref, g_ref, b_ref, ws_ref, bs_ref, du_ref, dv_ref, dws_ref, dbs_ref, dg_ref, db_ref, dvn_ref):
        @pl.when(pl.program_id(0) == 0)
        def _():
            dws_ref[...] = jnp.zeros_like(dws_ref)
            dbs_ref[...] = jnp.zeros_like(dbs_ref)

        vraw = v_ref[...].astype(F32)
        xh, rs = _ln_stats(_gelu(vraw))
        gv = g_ref[...]
        vn = (xh * gv + b_ref[...]).astype(BF16)
        for ch in range(T // CHUNK):
            rows = slice(ch * CHUNK, (ch + 1) * CHUNK)
            for h in range(SGU_HEADS):
                cols = slice(h * HEAD_DIM, (h + 1) * HEAD_DIM)
                w = ws_ref[h]
                vb = vn[rows, cols]
                sp = jnp.dot(w, vb, preferred_element_type=F32) + bs_ref[h]
                uraw = u_ref[rows, cols].astype(F32)
                dout = d_ref[rows, cols].astype(F32)
                du_ref[rows, cols] = (dout * sp * _gelu_grad(uraw)).astype(BF16)
                dsp = dout * _gelu(uraw)
                dspb = dsp.astype(BF16)
                dvn_ref[rows, cols] = lax.dot_general(w, dspb, TN, preferred_element_type=F32)
                dws_ref[h] += lax.dot_general(dspb, vb, NT, preferred_element_type=F32)
                dbs_ref[h] += jnp.sum(dsp, axis=1, keepdims=True)
        dvn = dvn_ref[...]
        dv_ref[...] = (_ln_bwd(dvn, xh, rs, gv) * _gelu_grad(vraw)).astype(BF16)
        _acc_out(dg_ref, jnp.sum(dvn * xh, axis=0, keepdims=True))
        _acc_out(db_ref, jnp.sum(dvn, axis=0, keepdims=True))

    vec = pl.BlockSpec((1, SGU_WIDTH), lambda i: (0, 0))
    full = pl.BlockSpec((SGU_HEADS, CHUNK, CHUNK), lambda i: (0, 0, 0))
    row = pl.BlockSpec((T, SGU_WIDTH), lambda i: (i, 0))
    oshape = jax.ShapeDtypeStruct((S, SGU_WIDTH), BF16)
    vshape = jax.ShapeDtypeStruct((1, SGU_WIDTH), F32)
    return _call("sgu_bwd", body, grid=(S // T,),
                 in_specs=[pl.BlockSpec((T, SGU_WIDTH), lambda i: (i, OFF_U // SGU_WIDTH)),
                           pl.BlockSpec((T, SGU_WIDTH), lambda i: (i, OFF_VV // SGU_WIDTH)),
                           pl.BlockSpec((T, SGU_WIDTH), lambda i: (i, (ATTN_WIDTH + CONV_WIDTH) // SGU_WIDTH)), vec, vec, full, full],
                 out_specs=[row, row, full, pl.BlockSpec((SGU_HEADS, CHUNK, 1), lambda i: (0, 0, 0)), vec, vec],
                 out_shape=[oshape, oshape, jax.ShapeDtypeStruct((SGU_HEADS, CHUNK, CHUNK), F32),
                            jax.ShapeDtypeStruct((SGU_HEADS, CHUNK, 1), F32), vshape, vshape],
                 scratch=[pltpu.VMEM((T, SGU_WIDTH), F32)], sem=("arbitrary",))(z, z, dmix, g, b, ws, bs)


def _row_tile(rows, cols, n_arrays):
    budget = (24 * 1024 * 1024) // (n_arrays * 2 * 4 * cols)
    t = min(rows, max(16, budget // 16 * 16))
    while rows % t:
        t -= 16
    return t


def add_sibling_half(grad, recv, c_idx):
    J, R, C = grad.shape
    hr = R // 2
    tr = _row_tile(hr, C, 3)
    nb = hr // tr

    def body(c_ref, g_ref, r_ref, o_ref):
        o_ref[...] = (g_ref[...].astype(F32) + r_ref[...].astype(F32)).astype(BF16)

    grid_spec = pltpu.PrefetchScalarGridSpec(
        num_scalar_prefetch=1, grid=(J, nb),
        in_specs=[pl.BlockSpec((None, tr, C), lambda j, i, c: (j, c[0] * nb + i, 0)),
                  pl.BlockSpec((None, tr, C), lambda j, i, c: (j, i, 0))],
        out_specs=pl.BlockSpec((None, tr, C), lambda j, i, c: (j, i, 0)))
    return pl.pallas_call(body, name="add_sibling_half", grid_spec=grid_spec,
                          out_shape=jax.ShapeDtypeStruct((J, hr, C), BF16),
                          compiler_params=pltpu.CompilerParams(vmem_limit_bytes=VMEM_LIMIT,
                                                               dimension_semantics=("parallel", "parallel")))(c_idx, grad, recv)


def sum_chips(own, others, stack, x_idx, y_idx, layer):
    R, C = own.shape[1:]
    tr = _row_tile(R, C, 4)

    def body(x_ref, y_ref, own_ref, oth_ref, stack_ref, o_ref):
        acc = own_ref[...].astype(F32)
        for j in range(3):
            acc = acc + oth_ref[j].astype(F32)
        o_ref[...] = acc

    grid_spec = pltpu.PrefetchScalarGridSpec(
        num_scalar_prefetch=2, grid=(R // tr,),
        in_specs=[pl.BlockSpec((None, tr, C), lambda i, xr, yr: (2 * xr[0] + yr[0], i, 0)),
                  pl.BlockSpec((3, tr, C), lambda i, xr, yr: (0, i, 0)),
                  pl.BlockSpec(memory_space=pl.ANY)],
        out_specs=pl.BlockSpec((None, tr, C), lambda i, xr, yr: (layer, i, 0)))
    return pl.pallas_call(body, name="sum_chips", grid_spec=grid_spec,
                          out_shape=jax.ShapeDtypeStruct(stack.shape, F32), input_output_aliases={4: 0},
                          compiler_params=pltpu.CompilerParams(vmem_limit_bytes=VMEM_LIMIT,
                                                               dimension_semantics=("parallel",)))(x_idx, y_idx, own, others, stack)


def adamw_halves(w, mine, theirs, m, v, c_idx):
    L, R, C = w.shape
    hr = R // 2
    tr = _row_tile(hr, C, 9)
    nb = hr // tr

    def body(c_ref, w_ref, a_ref, b_ref, m_ref, v_ref, g_ref, d_ref, nm_ref, nv_ref):
        gv = jnp.where(pl.program_id(1) == c_ref[0], a_ref[...], b_ref[...])
        g_ref[...] = gv
        nm = ADAM_B1 * m_ref[...] + (1.0 - ADAM_B1) * gv
        nv = ADAM_B2 * v_ref[...] + (1.0 - ADAM_B2) * (gv * gv)
        m_hat = nm / (1.0 - ADAM_B1 ** ADAM_STEP)
        v_hat = nv / (1.0 - ADAM_B2 ** ADAM_STEP)
        d_ref[...] = -ADAM_LR * (m_hat / (jnp.sqrt(v_hat) + ADAM_EPS) + ADAM_WD * w_ref[...])
        nm_ref[...] = nm
        nv_ref[...] = nv

    full = pl.BlockSpec((None, tr, C), lambda l, h, i, c: (l, h * nb + i, 0))
    a_spec = pl.BlockSpec((None, tr, C), lambda l, h, i, c: (l, jnp.where(h == c[0], i, 0), 0))
    b_spec = pl.BlockSpec((None, tr, C), lambda l, h, i, c: (l, jnp.where(h == c[0], 0, i), 0))
    grid_spec = pltpu.PrefetchScalarGridSpec(num_scalar_prefetch=1, grid=(L, 2, nb),
                                             in_specs=[full, a_spec, b_spec, full, full], out_specs=[full] * 4)
    shape = jax.ShapeDtypeStruct((L, R, C), F32)
    return pl.pallas_call(body, name="adamw_halves", grid_spec=grid_spec, out_shape=[shape] * 4,
                          compiler_params=pltpu.CompilerParams(vmem_limit_bytes=VMEM_LIMIT,
                                                               dimension_semantics=("parallel", "arbitrary", "arbitrary")))(
        c_idx, w, mine, theirs, m, v)


def adamw(w, g, m, v):
    R, C = w.shape
    tr = _row_tile(R, C, 7)

    def body(w_ref, g_ref, m_ref, v_ref, d_ref, nm_ref, nv_ref):
        gv = g_ref[...]
        nm = ADAM_B1 * m_ref[...] + (1.0 - ADAM_B1) * gv
        nv = ADAM_B2 * v_ref[...] + (1.0 - ADAM_B2) * (gv * gv)
        m_hat = nm / (1.0 - ADAM_B1 ** ADAM_STEP)
        v_hat = nv / (1.0 - ADAM_B2 ** ADAM_STEP)
        d_ref[...] = -ADAM_LR * (m_hat / (jnp.sqrt(v_hat) + ADAM_EPS) + ADAM_WD * w_ref[...])
        nm_ref[...] = nm
        nv_ref[...] = nv

    spec = pl.BlockSpec((tr, C), lambda i: (i, 0))
    shape = jax.ShapeDtypeStruct((R, C), F32)
    return _call("adamw", body, grid=(R // tr,), in_specs=[spec] * 4, out_specs=[spec] * 3, out_shape=[shape] * 3,
                 sem=("parallel",))(w, g, m, v)


def _place():
    x, y, c = lax.axis_index("x"), lax.axis_index("y"), lax.axis_index("c")
    chips = [(1 - x, y), (x, 1 - y), (1 - x, 1 - y)]
    return x, y, c, chips


def _remote(src, dst, send_sem, recv_sem, dev):
    return pltpu.make_async_remote_copy(src_ref=src, dst_ref=dst, send_sem=send_sem, recv_sem=recv_sem,
                                        device_id=dev, device_id_type=MESH)


EFFECT = pltpu.SideEffectType.DATAFLOW_SIDE_EFFECTING
SEM = pl.BlockSpec(memory_space=pltpu.SEMAPHORE)
ANY = pl.BlockSpec(memory_space=pl.ANY)
TOKEN = jax.ShapeDtypeStruct((8, LANES), F32)


def _in_hbm(a):
    return pltpu.with_memory_space_constraint(a, pltpu.HBM)


def _gather_copies(shards, lands, send_sems, recv_sems):
    x, y, c, chips = _place()
    me = 2 * x + y
    copies = []
    for k in range(len(shards)):
        hr = shards[k].shape[0] // 2
        mine = pl.ds(pl.multiple_of(c * hr, 8), hr)
        for t, (px, py) in enumerate(chips):
            copies.append(_remote(shards[k].at[mine, :], lands[k].at[me, mine, :], send_sems.at[4 * k + t], recv_sems.at[4 * k + t],
                                  (px, py, c)))
        copies.append(_remote(shards[k], lands[k].at[me], send_sems.at[4 * k + 3], recv_sems.at[4 * k + 3], (x, y, 1 - c)))
    return copies


def _gather_landings(lands, send_sems, recv_sems):
    x, y, c, chips = _place()
    me = 2 * x + y
    landings = []
    for k in range(len(lands)):
        hr = lands[k].shape[1] // 2
        mine = pl.ds(pl.multiple_of(c * hr, 8), hr)
        for t, (px, py) in enumerate(chips):
            dst = lands[k].at[2 * px + py, mine, :]
            landings.append(_remote(dst, dst, send_sems.at[4 * k + t], recv_sems.at[4 * k + t], (px, py, c)))
        dst = lands[k].at[me]
        landings.append(_remote(dst, dst, send_sems.at[4 * k + 3], recv_sems.at[4 * k + 3], (x, y, 1 - c)))
    return landings


def gather_start(shards, after):
    n = len(shards)

    def body(*refs):
        srcs, lands_in = refs[:n], refs[n:2 * n]
        send_sems, recv_sems = refs[2 * n + 1], refs[2 * n + 2]
        token = refs[-1]
        for cp in _gather_copies(srcs, lands_in, send_sems, recv_sems):
            cp.start()
        token[...] = jnp.zeros_like(token)

    lands = [lax.empty((N_CHIPS,) + s.shape, s.dtype) for s in shards]
    outs = pl.pallas_call(
        body, name="gather_start", in_specs=[HBM] * (2 * n) + [ANY],
        out_specs=[SEM, SEM] + [HBM] * (2 * n) + [VMEM_SPEC],
        out_shape=[pltpu.SemaphoreType.DMA((4 * n,)), pltpu.SemaphoreType.DMA((4 * n,))]
        + [pltpu.HBM(s.shape, s.dtype) for s in shards] + [pltpu.HBM(l.shape, l.dtype) for l in lands] + [TOKEN],
        input_output_aliases={i: 2 + i for i in range(2 * n)},
        compiler_params=pltpu.CompilerParams(has_side_effects=EFFECT),
    )(*[_in_hbm(s) for s in shards], *[_in_hbm(l) for l in lands], after)
    return outs[0], outs[1], outs[2:2 + n], outs[2 + n:2 + 2 * n], outs[-1]


def gather_wait(send_sems, recv_sems, shards, lands, after):
    n = len(shards)

    def body(*refs):
        srcs, lands_in = refs[:n], refs[n:2 * n]
        send, recv = refs[2 * n], refs[2 * n + 1]
        for cp in _gather_copies(srcs, lands_in, send, recv):
            cp.wait_send()
        for cp in _gather_landings(lands_in, send, recv):
            cp.wait_recv()

    outs = pl.pallas_call(
        body, name="gather_wait", in_specs=[HBM] * (2 * n) + [SEM, SEM, ANY], out_specs=[HBM] * (2 * n),
        out_shape=[pltpu.HBM(s.shape, s.dtype) for s in shards] + [pltpu.HBM(l.shape, l.dtype) for l in lands],
        input_output_aliases={i: i for i in range(2 * n)},
        compiler_params=pltpu.CompilerParams(has_side_effects=EFFECT),
    )(*shards, *lands, send_sems, recv_sems, after)
    return outs[n:]


def forward_halves(lands):
    n = len(lands)

    def body(*refs):
        ins, outs = refs[:n], refs[n:2 * n]
        send_sems, recv_sems = refs[2 * n:]
        x, y, c, chips = _place()
        sibling = (x, y, 1 - c)
        sends = []
        for k in range(n):
            hr = ins[k].shape[1] // 2
            mine = pl.ds(pl.multiple_of(c * hr, 8), hr)
            for t, (px, py) in enumerate(chips):
                cp = _remote(ins[k].at[2 * px + py, mine, :], outs[k].at[2 * px + py, mine, :],
                             send_sems.at[k, t], recv_sems.at[k, t], sibling)
                cp.start()
                sends.append(cp)
        for k in range(n):
            hr = ins[k].shape[1] // 2
            other = pl.ds(pl.multiple_of((1 - c) * hr, 8), hr)
            for t, (px, py) in enumerate(chips):
                dst = outs[k].at[2 * px + py, other, :]
                _remote(dst, dst, send_sems.at[k, t], recv_sems.at[k, t], sibling).wait_recv()
        for cp in sends:
            cp.wait_send()

    return pl.pallas_call(
        body, name="forward_halves", in_specs=[HBM] * n, out_specs=[HBM] * n,
        out_shape=[jax.ShapeDtypeStruct(l.shape, l.dtype) for l in lands],
        input_output_aliases={i: i for i in range(n)},
        scratch_shapes=[pltpu.SemaphoreType.DMA((n, 3)), pltpu.SemaphoreType.DMA((n, 3))],
    )(*lands)


def gather_small(block):
    def body(in_ref, out_ref, send_sems, recv_sems):
        x, y, c, chips = _place()
        me = 2 * x + y
        out_ref[me] = in_ref[...]
        sends = []
        for t, (px, py) in enumerate(chips):
            cp = _remote(in_ref, out_ref.at[me], send_sems.at[t], recv_sems.at[t], (px, py, c))
            cp.start()
            sends.append(cp)
        for t, (px, py) in enumerate(chips):
            landed = out_ref.at[2 * px + py]
            _remote(landed, landed, send_sems.at[t], recv_sems.at[t], (px, py, c)).wait_recv()
        for cp in sends:
            cp.wait_send()

    return pl.pallas_call(
        body, name="gather_small", in_specs=[VMEM_SPEC], out_specs=VMEM_SPEC,
        out_shape=jax.ShapeDtypeStruct((N_CHIPS,) + block.shape, block.dtype),
        scratch_shapes=[pltpu.SemaphoreType.DMA((3,)), pltpu.SemaphoreType.DMA((3,))],
    )(block)


def exchange_sibling_halves(grads):
    n = len(grads)

    def body(*refs):
        ins, outs = refs[:n], refs[n:2 * n]
        send_sems, recv_sems = refs[2 * n:]
        x, y, c, _ = _place()
        copies = []
        for k in range(n):
            hr = ins[k].shape[1] // 2
            theirs = pl.ds(pl.multiple_of((1 - c) * hr, 8), hr)
            cp = _remote(ins[k].at[:, theirs, :], outs[k], send_sems.at[k], recv_sems.at[k], (x, y, 1 - c))
            cp.start()
            copies.append(cp)
        for cp in copies:
            cp.wait()

    return pl.pallas_call(
        body, name="exchange_sibling_halves", in_specs=[HBM] * n, out_specs=[HBM] * n,
        out_shape=[jax.ShapeDtypeStruct((g.shape[0], g.shape[1] // 2, g.shape[2]), g.dtype) for g in grads],
        scratch_shapes=[pltpu.SemaphoreType.DMA((n,)), pltpu.SemaphoreType.DMA((n,))],
    )(*grads)


def _chip_copies(parts, lands, send_sems, recv_sems):
    x, y, c, chips = _place()
    return [_remote(parts[k].at[2 * px + py], lands[k].at[t], send_sems.at[3 * k + t], recv_sems.at[3 * k + t], (px, py, c))
            for k in range(len(parts)) for t, (px, py) in enumerate(chips)]


def chip_parts_start(parts):
    n = len(parts)

    def body(*refs):
        srcs, lands_in = refs[:n], refs[n:2 * n]
        send_sems, recv_sems = refs[2 * n], refs[2 * n + 1]
        token = refs[-1]
        for cp in _chip_copies(srcs, lands_in, send_sems, recv_sems):
            cp.start()
        token[...] = jnp.zeros_like(token)

    lands = [lax.empty((3,) + p.shape[1:], p.dtype) for p in parts]
    outs = pl.pallas_call(
        body, name="chip_parts_start", in_specs=[HBM] * (2 * n), out_specs=[SEM, SEM] + [HBM] * (2 * n) + [VMEM_SPEC],
        out_shape=[pltpu.SemaphoreType.DMA((3 * n,)), pltpu.SemaphoreType.DMA((3 * n,))]
        + [pltpu.HBM(p.shape, p.dtype) for p in parts] + [pltpu.HBM(l.shape, l.dtype) for l in lands] + [TOKEN],
        input_output_aliases={i: 2 + i for i in range(2 * n)},
        compiler_params=pltpu.CompilerParams(has_side_effects=EFFECT),
    )(*[_in_hbm(p) for p in parts], *[_in_hbm(l) for l in lands])
    return outs[0], outs[1], outs[2:2 + n], outs[2 + n:2 + 2 * n], outs[-1]


def chip_parts_wait(send_sems, recv_sems, parts, lands, after):
    n = len(parts)

    def body(*refs):
        srcs, lands_in = refs[:n], refs[n:2 * n]
        send, recv = refs[2 * n], refs[2 * n + 1]
        for cp in _chip_copies(srcs, lands_in, send, recv):
            cp.wait_send()
            cp.wait_recv()

    outs = pl.pallas_call(
        body, name="chip_parts_wait", in_specs=[HBM] * (2 * n) + [SEM, SEM, ANY], out_specs=[HBM] * (2 * n),
        out_shape=[pltpu.HBM(p.shape, p.dtype) for p in parts] + [pltpu.HBM(l.shape, l.dtype) for l in lands],
        input_output_aliases={i: i for i in range(2 * n)},
        compiler_params=pltpu.CompilerParams(has_side_effects=EFFECT),
    )(*parts, *lands, send_sems, recv_sems, after)
    return outs[:n], outs[n:]


def exchange_final_halves(halves):
    n = len(halves)

    def body(*refs):
        ins, outs = refs[:n], refs[n:2 * n]
        send_sems, recv_sems = refs[2 * n:]
        x, y, c, _ = _place()
        copies = []
        for k in range(n):
            cp = _remote(ins[k], outs[k], send_sems.at[k], recv_sems.at[k], (x, y, 1 - c))
            cp.start()
            copies.append(cp)
        for cp in copies:
            cp.wait()

    return pl.pallas_call(
        body, name="exchange_final_halves", in_specs=[HBM] * n, out_specs=[HBM] * n,
        out_shape=[jax.ShapeDtypeStruct(h.shape, h.dtype) for h in halves],
        scratch_shapes=[pltpu.SemaphoreType.DMA((n,)), pltpu.SemaphoreType.DMA((n,))],
    )(*halves)


def allreduce_small(packed):
    R = packed.shape[0]

    def body(x_ref, sum_ref, all_ref, send_sems, recv_sems):
        x, y, c, chips = _place()
        me, sibling = (x, y, c), (x, y, 1 - c)

        def rows(px, py, pc):
            return all_ref.at[4 * px + 2 * py + pc]

        def copy(k, block, to, src=None):
            return _remote(rows(*block) if src is None else src, rows(*block), send_sems.at[k], recv_sems.at[k], to)

        all_ref[4 * x + 2 * y + c] = x_ref[...]
        first = [copy(0, me, sibling, src=x_ref)]
        first += [copy(1 + j, me, (*chip, c), src=x_ref) for j, chip in enumerate(chips)]
        for cp in first:
            cp.start()
        passed = [copy(4 + j, (*chip, c), sibling) for j, chip in enumerate(chips)]
        for j, chip in enumerate(chips):
            copy(1 + j, (*chip, c), me).wait_recv()
            passed[j].start()
        copy(0, sibling, me).wait_recv()
        for j, chip in enumerate(chips):
            copy(4 + j, (*chip, 1 - c), me).wait_recv()
        for cp in first + passed:
            cp.wait_send()

        def chunk(i, carry):
            rws = pl.ds(pl.multiple_of(i * PACK_ROWS, PACK_ROWS), PACK_ROWS)
            acc = all_ref[0, rws, :]
            for d in range(1, N_DEV):
                acc = acc + all_ref[d, rws, :]
            sum_ref[rws, :] = acc
            return carry

        lax.fori_loop(0, R // PACK_ROWS, chunk, 0)

    return pl.pallas_call(
        body, name="allreduce_small", in_specs=[VMEM_SPEC], out_specs=VMEM_SPEC,
        out_shape=jax.ShapeDtypeStruct((R, LANES), F32),
        scratch_shapes=[pltpu.VMEM((N_DEV, R, LANES), F32), pltpu.SemaphoreType.DMA((7,)), pltpu.SemaphoreType.DMA((7,))],
        compiler_params=pltpu.CompilerParams(vmem_limit_bytes=VMEM_LIMIT),
    )(packed)


def _layer_fwd(x, p, tabs, token):
    h = rms_fwd(x, p["mix_norm_g"], token)
    z = mm_nn_cols(h, p["w_in"])
    qr, kp, vp = rope_fwd(z, tabs)
    attn = attn_fwd(qr, kp, vp, p["sink3"])
    c1 = conv_dw_fwd(z, p["conv_w32"], p["conv_dw_b"])
    conv = conv_ln_fwd(c1, p["conv_ln_g"], p["conv_ln_b"])
    sgu = sgu_fwd(z, p["sgu_ln_g"], p["sgu_ln_b"], p["sgu_w16"], p["sgu_b3"])
    mix = jnp.concatenate([attn, conv, sgu], axis=1)
    x_mid = mm_nn_rows_res(mix, p["w_out"], x)
    h2 = rms_fwd(x_mid, p["ffn_norm_g"], token)
    gate, up, act = ffn_up(h2, p["w_gate"], p["w_up"])
    x_out = mm_nn_rows_res(act, p["w_down"], x_mid)
    saved = dict(x=x, h=h, z=z, qr=qr, kp=kp, vp=vp, c1=c1, mix=mix, x_mid=x_mid, h2=h2, gate=gate, up=up, act=act)
    return x_out, saved


def _layer_bwd(dx, dxb, p, s, tabs, token):
    dgate, dup = ffn_down_bwd(dxb, p["w_down"], s["gate"], s["up"], token)
    g_down = mm_tn_rows(s["act"], dxb)
    dh2 = mm_nt_cols([(dgate, p["w_gate"]), (dup, p["w_up"])], F32)
    g_gate = mm_tn_cols(s["h2"], dgate, N_CHIPS)
    g_up = mm_tn_cols(s["h2"], dup, N_CHIPS)
    dmid, dmidb, g_ffn_norm = rms_bwd(s["x_mid"], p["ffn_norm_g"], dh2, dx)
    dmix = mm_nt_rows(dmidb, p["w_out"])
    g_out = mm_tn_rows(s["mix"], dmidb)
    dq, dkp, dvp, dsink = attn_bwd(s["qr"], s["kp"], s["vp"], p["sink3"], dmix)
    dqkv = rope_bwd(dq, dkp, dvp, tabs)
    dc1, g_cln_g, g_cln_b = conv_ln_bwd(dmix, s["c1"], p["conv_ln_g"], p["conv_ln_b"])
    dca, dcg, g_cw, g_cb = conv_dw_bwd(dc1, s["z"], p["conv_w32"])
    du, dv, g_sw, g_sb, g_sln_g, g_sln_b = sgu_bwd(s["z"], dmix, p["sgu_ln_g"], p["sgu_ln_b"], p["sgu_w16"], p["sgu_b3"])
    dz = jnp.concatenate([dqkv, dca, dcg, du, dv], axis=1)
    dh = mm_nt_cols([(dz, p["w_in"])], F32)
    g_in = mm_tn_cols(s["h"], dz, N_CHIPS)
    dx_in, dxb_in, g_mix_norm = rms_bwd(s["x"], p["mix_norm_g"], dh, dmid)
    big = [g_in, g_out.reshape(N_CHIPS, -1, D_MODEL), g_gate, g_up, g_down.reshape(N_CHIPS, -1, D_MODEL)]
    small = dict(mix_norm_g=g_mix_norm, sink=dsink[:, :, 0].reshape(1, N_Q_HEADS), conv_dw_w=g_cw[:CONV_KERNEL],
                 conv_dw_b=g_cb, conv_ln_g=g_cln_g, conv_ln_b=g_cln_b, sgu_ln_g=g_sln_g, sgu_ln_b=g_sln_b,
                 sgu_w=g_sw, sgu_b=g_sb[:, :, 0], ffn_norm_g=g_ffn_norm)
    return dx_in, dxb_in, big, small


def _layer_params(l, gathered, conv_w_full, mix_norm_g, sink, conv_dw_b, conv_ln_g, conv_ln_b, sgu_ln_g, sgu_ln_b, sgu_w,
                  sgu_b, ffn_norm_g):
    w_in, w_out, w_gate, w_up, w_down = gathered
    return dict(
        w_in=w_in, w_out=w_out.reshape(-1, D_MODEL), w_gate=w_gate, w_up=w_up, w_down=w_down.reshape(-1, D_MODEL),
        mix_norm_g=mix_norm_g[l:l + 1], ffn_norm_g=ffn_norm_g[l:l + 1],
        sink3=jnp.broadcast_to(sink[l].reshape(N_KV_HEADS, Q_PER_KV, 1), (N_KV_HEADS, Q_PER_KV, LANES)),
        conv_w32=jnp.pad(conv_w_full[l], ((0, 32 - CONV_KERNEL), (0, 0))),
        conv_dw_b=conv_dw_b[l:l + 1], conv_ln_g=conv_ln_g[l:l + 1], conv_ln_b=conv_ln_b[l:l + 1],
        sgu_ln_g=sgu_ln_g[l:l + 1], sgu_ln_b=sgu_ln_b[l:l + 1], sgu_w16=sgu_w[l].astype(BF16),
        sgu_b3=jnp.broadcast_to(sgu_b[l][:, :, None], (SGU_HEADS, CHUNK, CHUNK)))


_SMALL = ["mix_norm_g", "sink", "conv_dw_b", "conv_ln_g", "conv_ln_b", "sgu_ln_g", "sgu_ln_b", "sgu_w", "sgu_b", "ffn_norm_g",
          "final_norm_g"]


def _pack_rows(arrays):
    rows, counts = [], []
    for a in arrays:
        flat = a.reshape(-1)
        n = -(-flat.shape[0] // LANES)
        rows.append(jnp.pad(flat, (0, n * LANES - flat.shape[0])).reshape(n, LANES))
        counts.append(n)
    packed = jnp.concatenate(rows, axis=0)
    pad = -packed.shape[0] % PACK_ROWS
    return jnp.pad(packed, ((0, pad), (0, 0))), counts


def _unpack_rows(packed, counts, shapes):
    out, r = [], 0
    for n, shape in zip(counts, shapes):
        size = math.prod(shape)
        out.append(packed[r:r + n].reshape(-1)[:size].reshape(shape))
        r += n
    return out


def kernel(x, mix_norm_g, w_in, sink, conv_dw_w, conv_dw_b, conv_ln_g, conv_ln_b, sgu_ln_g, sgu_ln_b, sgu_w, sgu_b, w_out, ffn_norm_g, w_gate, w_up, w_down, final_norm_g, loss_target, m_mix_norm_g, m_w_in, m_sink, m_conv_dw_w, m_conv_dw_b, m_conv_ln_g, m_conv_ln_b, m_sgu_ln_g, m_sgu_ln_b, m_sgu_w, m_sgu_b, m_w_out, m_ffn_norm_g, m_w_gate, m_w_up, m_w_down, m_final_norm_g, v_mix_norm_g, v_w_in, v_sink, v_conv_dw_w, v_conv_dw_b, v_conv_ln_g, v_conv_ln_b, v_sgu_ln_g, v_sgu_ln_b, v_sgu_w, v_sgu_b, v_w_out, v_ffn_norm_g, v_w_gate, v_w_up, v_w_down, v_final_norm_g):
    S = x.shape[1]
    my_chip = 2 * lax.axis_index("x") + lax.axis_index("y")
    c_idx = lax.axis_index("c").astype(jnp.int32).reshape(1)
    big_w = [w_in, w_out, w_gate, w_up, w_down]
    big_m = [m_w_in, m_w_out, m_w_gate, m_w_up, m_w_down]
    big_v = [v_w_in, v_w_out, v_w_gate, v_w_up, v_w_down]
    n_kinds = len(big_w)

    x_idx = lax.axis_index("x").astype(jnp.int32).reshape(1)
    y_idx = lax.axis_index("y").astype(jnp.int32).reshape(1)
    conv_w_all = gather_small(conv_dw_w)
    conv_w_full = jnp.transpose(conv_w_all, (1, 2, 0, 3)).reshape(DEPTH, CONV_KERNEL, CONV_WIDTH)
    tabs = rope_tables(S)
    no_token = jnp.zeros(TOKEN.shape, TOKEN.dtype)

    shards = [[w[l].astype(BF16) for w in big_w] for l in range(DEPTH)]
    pending = gather_start(shards[0], no_token)
    act = x[0]
    saved, params = [], []
    for l in range(DEPTH):
        send_sems, recv_sems, srcs, lands, _ = pending
        gathered = forward_halves(gather_wait(send_sems, recv_sems, srcs, lands, act))
        token = no_token
        if l + 1 < DEPTH:
            pending = gather_start(shards[l + 1], gathered[0])
            token = pending[4]
        params.append(_layer_params(l, gathered, conv_w_full, mix_norm_g, sink, conv_dw_b, conv_ln_g, conv_ln_b, sgu_ln_g,
                                    sgu_ln_b, sgu_w, sgu_b, ffn_norm_g))
        act, s = _layer_fwd(act, params[l], tabs, token)
        saved.append(s)
    loss_part, dx, dxb, g_final = final_loss(act, final_norm_g.reshape(1, D_MODEL), loss_target[0])
    loss = lax.psum(loss_part[0, 0], ("x", "y", "c"))

    halves = [jnp.zeros((DEPTH, w.shape[1] // 2, w.shape[2]), F32) for w in big_w]
    small_grads = [None] * DEPTH

    def finish(pending, halves, after):
        layer, send_sems, recv_sems, parts, lands = pending
        parts, others = chip_parts_wait(send_sems, recv_sems, parts, lands, after)
        return [sum_chips(parts[k], others[k], halves[k], x_idx, y_idx, layer) for k in range(n_kinds)]

    pending, token = None, no_token
    for l in reversed(range(DEPTH)):
        dx, dxb, big, small_grads[l] = _layer_bwd(dx, dxb, params[l], saved[l], tabs, token)
        if pending is not None:
            halves = finish(pending, halves, dx)
        recv = exchange_sibling_halves(big)
        chip_sum = [add_sibling_half(g, r, c_idx) for g, r in zip(big, recv)]
        send_sems, recv_sems, parts, lands, token = chip_parts_start(chip_sum)
        pending = (l, send_sems, recv_sems, parts, lands)
    halves = finish(pending, halves, token)
    sibling_halves = exchange_final_halves(halves)

    stacked = {n: jnp.stack([small_grads[l][n] for l in range(DEPTH)]) for n in small_grads[0]}
    stacked["final_norm_g"] = g_final
    packed, counts = _pack_rows([stacked[n] for n in _SMALL] + [stacked["conv_dw_w"]])
    reduced = allreduce_small(packed)
    small_w = dict(mix_norm_g=mix_norm_g, sink=sink, conv_dw_b=conv_dw_b, conv_ln_g=conv_ln_g, conv_ln_b=conv_ln_b,
                   sgu_ln_g=sgu_ln_g, sgu_ln_b=sgu_ln_b, sgu_w=sgu_w, sgu_b=sgu_b, ffn_norm_g=ffn_norm_g,
                   final_norm_g=final_norm_g)
    small_m = dict(mix_norm_g=m_mix_norm_g, sink=m_sink, conv_dw_b=m_conv_dw_b, conv_ln_g=m_conv_ln_g,
                   conv_ln_b=m_conv_ln_b, sgu_ln_g=m_sgu_ln_g, sgu_ln_b=m_sgu_ln_b, sgu_w=m_sgu_w, sgu_b=m_sgu_b,
                   ffn_norm_g=m_ffn_norm_g, final_norm_g=m_final_norm_g)
    small_v = dict(mix_norm_g=v_mix_norm_g, sink=v_sink, conv_dw_b=v_conv_dw_b, conv_ln_g=v_conv_ln_g,
                   conv_ln_b=v_conv_ln_b, sgu_ln_g=v_sgu_ln_g, sgu_ln_b=v_sgu_ln_b, sgu_w=v_sgu_w, sgu_b=v_sgu_b,
                   ffn_norm_g=v_ffn_norm_g, final_norm_g=v_final_norm_g)
    shapes = [small_w[n].shape for n in _SMALL] + [(DEPTH, CONV_KERNEL, CONV_WIDTH)]
    red = _unpack_rows(reduced, counts, shapes)
    g_small = dict(zip(_SMALL, red[:-1]))
    g_small["conv_dw_w"] = lax.dynamic_slice_in_dim(red[-1], my_chip * LANES, LANES, axis=2)
    small_w["conv_dw_w"], small_m["conv_dw_w"], small_v["conv_dw_w"] = conv_dw_w, m_conv_dw_w, v_conv_dw_w
    names = _SMALL + ["conv_dw_w"]
    pw, cnt = _pack_rows([small_w[n] for n in names])
    pg, _ = _pack_rows([g_small[n] for n in names])
    pm, _ = _pack_rows([small_m[n] for n in names])
    pv, _ = _pack_rows([small_v[n] for n in names])
    sd, sm, sv = adamw(pw, pg, pm, pv)
    shp = [small_w[n].shape for n in names]
    d_small = dict(zip(names, _unpack_rows(sd, cnt, shp)))
    m_small = dict(zip(names, _unpack_rows(sm, cnt, shp)))
    v_small = dict(zip(names, _unpack_rows(sv, cnt, shp)))

    big_names = ["w_in", "w_out", "w_gate", "w_up", "w_down"]
    g_big, d_big, m_big, v_big = {}, {}, {}, {}
    for k, n in enumerate(big_names):
        g_big[n], d_big[n], m_big[n], v_big[n] = adamw_halves(big_w[k], halves[k], sibling_halves[k], big_m[k], big_v[k], c_idx)

    order = ["mix_norm_g", "w_in", "sink", "conv_dw_w", "conv_dw_b", "conv_ln_g", "conv_ln_b", "sgu_ln_g", "sgu_ln_b",
             "sgu_w", "sgu_b", "w_out", "ffn_norm_g", "w_gate", "w_up", "w_down", "final_norm_g"]
    grads = {**g_small, **g_big}
    deltas = {**d_small, **d_big}
    new_m = {**m_small, **m_big}
    new_v = {**v_small, **v_big}
    return (loss, dx[None], *[grads[n] for n in order], *[deltas[n] for n in order],
            *[new_m[n] for n in order], *[new_v[n] for n in order])
```

```python
import functools
import math

import jax
import jax.numpy as jnp
from jax import lax
from jax.experimental import pallas as pl
from jax.experimental.pallas import tpu as pltpu

F32, BF16 = jnp.float32, jnp.bfloat16

D_MODEL = 2048
DEPTH = 4
HEAD_DIM = 128
N_Q_HEADS = 8
N_KV_HEADS = 2
Q_PER_KV = N_Q_HEADS // N_KV_HEADS
ATTN_WIDTH = N_Q_HEADS * HEAD_DIM
KV_WIDTH = N_KV_HEADS * HEAD_DIM
CONV_WIDTH = 512
CONV_KERNEL = 31
CONV_PAD = 16
SGU_WIDTH = 512
SGU_HEADS = 4
CHUNK = 128
IN_WIDTH = 3584
D_FF = 5632
WINDOW = 128
ROT_DIM = 32
ROPE_THETA = 500000.0
EPS = 1e-6
N_CHIPS = 4
N_DEV = 8
LANES = 128
PACK_ROWS = 64
OFF_K = ATTN_WIDTH
OFF_V = OFF_K + KV_WIDTH
OFF_CA = OFF_V + KV_WIDTH
OFF_CG = OFF_CA + CONV_WIDTH
OFF_U = OFF_CG + CONV_WIDTH
OFF_VV = OFF_U + SGU_WIDTH

ADAM_LR, ADAM_B1, ADAM_B2, ADAM_EPS, ADAM_WD, ADAM_STEP = 0.001, 0.9, 0.999, 1e-08, 0.01, 10

VMEM_LIMIT = 56 * 1024 * 1024
MESH = pl.DeviceIdType.MESH
HBM = pl.BlockSpec(memory_space=pltpu.HBM)
VMEM_SPEC = pl.BlockSpec(memory_space=pltpu.VMEM)


def _call(name, body, *, grid, in_specs, out_specs, out_shape, scratch=(), sem=None):
    params = dict(vmem_limit_bytes=VMEM_LIMIT)
    if sem is not None:
        params["dimension_semantics"] = sem
    return pl.pallas_call(
        body, name=name, grid=grid, in_specs=in_specs, out_specs=out_specs, out_shape=out_shape,
        scratch_shapes=list(scratch), compiler_params=pltpu.CompilerParams(**params))


def _sigmoid(x):
    return 1.0 / (1.0 + jnp.exp(-x))


def rms_fwd(x, g, token):
    S = x.shape[0]
    tm = min(512, S)

    def body(x_ref, g_ref, token_ref, o_ref):
        xv = x_ref[...]
        r = lax.rsqrt(jnp.mean(xv * xv, axis=-1, keepdims=True) + EPS)
        o_ref[...] = (xv * r * g_ref[...]).astype(BF16)

    return _call("rms_fwd", body, grid=(S // tm,),
                 in_specs=[pl.BlockSpec((tm, D_MODEL), lambda i: (i, 0)), pl.BlockSpec((1, D_MODEL), lambda i: (0, 0)),
                           pl.BlockSpec((8, LANES), lambda i: (0, 0))],
                 out_specs=pl.BlockSpec((tm, D_MODEL), lambda i: (i, 0)),
                 out_shape=jax.ShapeDtypeStruct((S, D_MODEL), BF16), sem=("parallel",))(x, g, token)


def _rms_bwd_math(xv, gv, dh):
    r = lax.rsqrt(jnp.mean(xv * xv, axis=-1, keepdims=True) + EPS)
    n = xv * r
    dn = dh * gv
    dx = r * (dn - n * jnp.mean(dn * n, axis=-1, keepdims=True))
    dg = jnp.sum(dh * n, axis=0, keepdims=True)
    return dx, dg


def rms_bwd(x, g, dh, dres):
    S = x.shape[0]
    tm = min(256, S)

    def body(x_ref, g_ref, dh_ref, dres_ref, dx_ref, dxb_ref, dg_ref):
        dx, dg = _rms_bwd_math(x_ref[...], g_ref[...], dh_ref[...])
        dx = dx + dres_ref[...]
        dx_ref[...] = dx
        dxb_ref[...] = dx.astype(BF16)

        @pl.when(pl.program_id(0) == 0)
        def _():
            dg_ref[...] = dg

        @pl.when(pl.program_id(0) > 0)
        def _():
            dg_ref[...] += dg

    row = pl.BlockSpec((tm, D_MODEL), lambda i: (i, 0))
    vec = pl.BlockSpec((1, D_MODEL), lambda i: (0, 0))
    return _call("rms_bwd", body, grid=(S // tm,), in_specs=[row, vec, row, row], out_specs=[row, row, vec],
                 out_shape=[jax.ShapeDtypeStruct((S, D_MODEL), F32), jax.ShapeDtypeStruct((S, D_MODEL), BF16),
                            jax.ShapeDtypeStruct((1, D_MODEL), F32)], sem=("arbitrary",))(x, g, dh, dres)


def final_loss(x, g, target):
    S = x.shape[0]
    tm = min(256, S)

    def body(x_ref, g_ref, t_ref, loss_ref, dx_ref, dxb_ref, dg_ref):
        xv = x_ref[...]
        gv = g_ref[...]
        r = lax.rsqrt(jnp.mean(xv * xv, axis=-1, keepdims=True) + EPS)
        err = xv * r * gv - t_ref[...]
        part = 0.5 * jnp.sum(jnp.mean(err * err, axis=-1, keepdims=True), axis=0, keepdims=True)
        dx, dg = _rms_bwd_math(xv, gv, err * (1.0 / D_MODEL))
        dx_ref[...] = dx
        dxb_ref[...] = dx.astype(BF16)

        @pl.when(pl.program_id(0) == 0)
        def _():
            dg_ref[...] = dg
            loss_ref[...] = part

        @pl.when(pl.program_id(0) > 0)
        def _():
            dg_ref[...] += dg
            loss_ref[...] += part

    row = pl.BlockSpec((tm, D_MODEL), lambda i: (i, 0))
    vec = pl.BlockSpec((1, D_MODEL), lambda i: (0, 0))
    one = pl.BlockSpec((1, 1), lambda i: (0, 0))
    return _call("final_loss", body, grid=(S // tm,), in_specs=[row, vec, row], out_specs=[one, row, row, vec],
                 out_shape=[jax.ShapeDtypeStruct((1, 1), F32), jax.ShapeDtypeStruct((S, D_MODEL), F32),
                            jax.ShapeDtypeStruct((S, D_MODEL), BF16), jax.ShapeDtypeStruct((1, D_MODEL), F32)],
                 sem=("arbitrary",))(x, g, target)


NN = (((1,), (0,)), ((), ()))
NT = (((1,), (1,)), ((), ()))
TN = (((0,), (0,)), ((), ()))


def _matmul(name, operands, in_specs, out_shape, out_specs, grid, pairs, dims, acc_shape, epilogue):
    n_in, n_out, nk = len(operands), len(out_shape), grid[-1]

    def body(*refs):
        ins, outs = refs[:n_in], refs[n_in:n_in + n_out]
        part = None
        for ia, ib in pairs:
            d = lax.dot_general(ins[ia][...], ins[ib][...], dims, preferred_element_type=F32)
            part = d if part is None else part + d
        if nk == 1:
            epilogue(part, ins, outs)
        else:
            acc = refs[-1]
            k = pl.program_id(len(grid) - 1)

            @pl.when(k == 0)
            def _():
                acc[...] = part

            @pl.when(k > 0)
            def _():
                acc[...] += part

            @pl.when(k == nk - 1)
            def _():
                epilogue(acc[...], ins, outs)

    scratch = [pltpu.VMEM(acc_shape, F32)] if nk > 1 else []
    sem = ("parallel",) * (len(grid) - 1) + ("arbitrary",)
    return _call(name, body, grid=grid, in_specs=in_specs, out_specs=out_specs, out_shape=out_shape,
                 scratch=scratch, sem=sem)(*operands)


def _store(dtype):
    def epilogue(acc, ins, outs):
        outs[0][...] = acc.astype(dtype)
    return epilogue


def mm_nn_cols(a, w):
    S, K = a.shape
    J, _, Ns = w.shape
    tm = min(512, S)
    return _matmul("mm_nn_cols", (a, w),
                   [pl.BlockSpec((tm, K), lambda j, i, k: (i, 0)), pl.BlockSpec((None, K, Ns), lambda j, i, k: (j, 0, 0))],
                   [jax.ShapeDtypeStruct((S, J * Ns), BF16)], [pl.BlockSpec((tm, Ns), lambda j, i, k: (i, j))],
                   (J, S // tm, 1), [(0, 1)], NN, None, _store(BF16))[0]


def ffn_up(h, wg, wu):
    S, K = h.shape
    J, _, Ns = wg.shape
    tm = min(512, S)

    def body(h_ref, wg_ref, wu_ref, g_ref, u_ref, a_ref):
        hv = h_ref[...]
        gv = jnp.dot(hv, wg_ref[...], preferred_element_type=F32)
        uv = jnp.dot(hv, wu_ref[...], preferred_element_type=F32)
        g_ref[...] = gv.astype(BF16)
        u_ref[...] = uv.astype(BF16)
        a_ref[...] = (gv * _sigmoid(gv) * uv).astype(BF16)

    wspec = pl.BlockSpec((None, K, Ns), lambda j, i: (j, 0, 0))
    ospec = pl.BlockSpec((tm, Ns), lambda j, i: (i, j))
    oshape = jax.ShapeDtypeStruct((S, J * Ns), BF16)
    return _call("ffn_up", body, grid=(J, S // tm), in_specs=[pl.BlockSpec((tm, K), lambda j, i: (i, 0)), wspec, wspec],
                 out_specs=[ospec, ospec, ospec], out_shape=[oshape, oshape, oshape], sem=("parallel", "parallel"))(h, wg, wu)


def mm_nn_rows_res(a, w, res):
    S, K = a.shape
    N = w.shape[1]
    tm = min(512, S)
    tk = K if K <= 2048 else K // 4

    def epilogue(acc, ins, outs):
        outs[0][...] = acc + ins[2][...]

    return _matmul("mm_nn_rows_res", (a, w, res),
                   [pl.BlockSpec((tm, tk), lambda i, k: (i, k)), pl.BlockSpec((tk, N), lambda i, k: (k, 0)),
                    pl.BlockSpec((tm, N), lambda i, k: (i, 0))],
                   [jax.ShapeDtypeStruct((S, N), F32)], [pl.BlockSpec((tm, N), lambda i, k: (i, 0))],
                   (S // tm, K // tk), [(0, 1)], NN, (tm, N), epilogue)[0]


def mm_nt_cols(pairs_in, out_dtype):
    dz0, w0 = pairs_in[0]
    S = dz0.shape[0]
    J, K, Ns = w0.shape
    tm = min(512, S)
    operands, specs, pairs = [], [], []
    for dz, w in pairs_in:
        pairs.append((len(operands), len(operands) + 1))
        operands += [dz, w]
        specs += [pl.BlockSpec((tm, Ns), lambda i, j: (i, j)), pl.BlockSpec((None, K, Ns), lambda i, j: (j, 0, 0))]
    return _matmul("mm_nt_cols%d" % len(pairs_in), tuple(operands), specs,
                   [jax.ShapeDtypeStruct((S, K), out_dtype)], [pl.BlockSpec((tm, K), lambda i, j: (i, 0))],
                   (S // tm, J), pairs, NT, (tm, K), _store(out_dtype))[0]


def mm_nt_rows(dy, w, token):
    S, N = dy.shape
    K = w.shape[0]
    tm, tko = min(1024, S), 512
    return _matmul("mm_nt_rows", (dy, w, token),
                   [pl.BlockSpec((tm, N), lambda i, kk, z: (i, 0)), pl.BlockSpec((tko, N), lambda i, kk, z: (kk, 0)),
                    pl.BlockSpec((8, LANES), lambda i, kk, z: (0, 0))],
                   [jax.ShapeDtypeStruct((S, K), BF16)], [pl.BlockSpec((tm, tko), lambda i, kk, z: (i, kk))],
                   (S // tm, K // tko, 1), [(0, 1)], NT, None, _store(BF16))[0]


def ffn_down_bwd(dy, w, gate, up, token):
    S, N = dy.shape
    K = w.shape[0]
    tm, tko = min(1024, S), 512

    def epilogue(acc, ins, outs):
        gv = ins[2][...].astype(F32)
        uv = ins[3][...].astype(F32)
        sg = _sigmoid(gv)
        outs[0][...] = (acc * uv * sg * (1.0 + gv * (1.0 - sg))).astype(BF16)
        outs[1][...] = (acc * gv * sg).astype(BF16)

    tile = pl.BlockSpec((tm, tko), lambda i, kk, z: (i, kk))
    oshape = jax.ShapeDtypeStruct((S, K), BF16)
    return _matmul("ffn_down_bwd", (dy, w, gate, up, token),
                   [pl.BlockSpec((tm, N), lambda i, kk, z: (i, 0)), pl.BlockSpec((tko, N), lambda i, kk, z: (kk, 0)), tile, tile,
                    pl.BlockSpec((8, LANES), lambda i, kk, z: (0, 0))],
                   [oshape, oshape], [tile, tile], (S // tm, K // tko, 1), [(0, 1)], NT, None, epilogue)


def mm_tn_cols(a, dz, J):
    S, M = a.shape
    Ns = dz.shape[1] // J
    tm, tk = 512, S
    return _matmul("mm_tn_cols", (a, dz),
                   [pl.BlockSpec((tk, tm), lambda j, m, k: (k, m)), pl.BlockSpec((tk, Ns), lambda j, m, k: (k, j))],
                   [jax.ShapeDtypeStruct((J, M, Ns), BF16)], [pl.BlockSpec((None, tm, Ns), lambda j, m, k: (j, m, 0))],
                   (J, M // tm, S // tk), [(0, 1)], TN, (tm, Ns), _store(BF16))[0]


def mm_tn_rows(a, dy):
    S, K = a.shape
    N = dy.shape[1]
    tm, tk = 512, min(2048, S)
    return _matmul("mm_tn_rows", (a, dy),
                   [pl.BlockSpec((tk, tm), lambda m, k: (k, m)), pl.BlockSpec((tk, N), lambda m, k: (k, 0))],
                   [jax.ShapeDtypeStruct((K, N), BF16)], [pl.BlockSpec((tm, N), lambda m, k: (m, 0))],
                   (K // tm, S // tk), [(0, 1)], TN, (tm, N), _store(BF16))[0]


def rope_tables(S):
    half = ROT_DIM // 2
    pos = jnp.arange(S, dtype=F32)
    inv = ROPE_THETA ** (-jnp.arange(0, ROT_DIM, 2, dtype=F32) / ROT_DIM)
    ang = pos[:, None] * inv[None, :]
    cos, sin = jnp.cos(ang), jnp.sin(ang)
    zeros = jnp.zeros((S, HEAD_DIM - ROT_DIM), F32)
    c = jnp.concatenate([cos, cos, jnp.ones((S, HEAD_DIM - ROT_DIM), F32)], axis=1)
    s_lo = jnp.concatenate([-sin, jnp.zeros((S, half), F32), zeros], axis=1)
    s_hi = jnp.concatenate([jnp.zeros((S, half), F32), sin, zeros], axis=1)
    return c, s_lo, s_hi


def _rope(t, c, s_lo, s_hi):
    half = ROT_DIM // 2
    return t * c + pltpu.roll(t, HEAD_DIM - half, 1) * s_lo + pltpu.roll(t, half, 1) * s_hi


def _unrope(d, c, s_lo, s_hi):
    half = ROT_DIM // 2
    return d * c + pltpu.roll(d * s_lo, half, 1) + pltpu.roll(d * s_hi, HEAD_DIM - half, 1)


def rope_fwd(z, tabs):
    S = z.shape[0]
    nb = S // CHUNK

    def body(q_ref, kv_ref, c_ref, sl_ref, sh_ref, qr_ref, kp_ref, vp_ref):
        i = pl.program_id(0)

        @pl.when(i == 0)
        def _():
            zero = jnp.zeros((CHUNK, KV_WIDTH), BF16)
            kp_ref[0:CHUNK, :] = zero
            vp_ref[0:CHUNK, :] = zero
            kp_ref[S + CHUNK:S + 2 * CHUNK, :] = zero
            vp_ref[S + CHUNK:S + 2 * CHUNK, :] = zero

        c, sl, sh = c_ref[...], sl_ref[...], sh_ref[...]
        for h in range(N_Q_HEADS):
            cols = slice(h * HEAD_DIM, (h + 1) * HEAD_DIM)
            qr_ref[:, cols] = _rope(q_ref[:, cols].astype(F32), c, sl, sh).astype(BF16)
        rows = pl.ds(pl.multiple_of(CHUNK + i * CHUNK, CHUNK), CHUNK)
        for g in range(N_KV_HEADS):
            cols = slice(g * HEAD_DIM, (g + 1) * HEAD_DIM)
            kp_ref[rows, cols] = _rope(kv_ref[:, cols].astype(F32), c, sl, sh).astype(BF16)
        vp_ref[rows, :] = kv_ref[:, KV_WIDTH:2 * KV_WIDTH]

    tab = pl.BlockSpec((CHUNK, HEAD_DIM), lambda i: (i, 0))
    pad = pl.BlockSpec((S + 2 * CHUNK, KV_WIDTH), lambda i: (0, 0))
    return _call("rope_fwd", body, grid=(nb,),
                 in_specs=[pl.BlockSpec((CHUNK, ATTN_WIDTH), lambda i: (i, 0)),
                           pl.BlockSpec((CHUNK, 2 * KV_WIDTH), lambda i: (i, OFF_K // (2 * KV_WIDTH))), tab, tab, tab],
                 out_specs=[pl.BlockSpec((CHUNK, ATTN_WIDTH), lambda i: (i, 0)), pad, pad],
                 out_shape=[jax.ShapeDtypeStruct((S, ATTN_WIDTH), BF16), jax.ShapeDtypeStruct((S + 2 * CHUNK, KV_WIDTH), BF16),
                            jax.ShapeDtypeStruct((S + 2 * CHUNK, KV_WIDTH), BF16)], sem=("arbitrary",))(z, z, *tabs)


def rope_bwd(dq, dkp, dvp, tabs):
    S = dq.shape[0]

    def body(dq_ref, dk_ref, dv_ref, c_ref, sl_ref, sh_ref, o_ref):
        c, sl, sh = c_ref[...], sl_ref[...], sh_ref[...]
        for h in range(N_Q_HEADS):
            cols = slice(h * HEAD_DIM, (h + 1) * HEAD_DIM)
            o_ref[:, cols] = _unrope(dq_ref[:, cols], c, sl, sh).astype(BF16)
        for g in range(N_KV_HEADS):
            cols = slice(g * HEAD_DIM, (g + 1) * HEAD_DIM)
            o_ref[:, OFF_K + g * HEAD_DIM:OFF_K + (g + 1) * HEAD_DIM] = _unrope(dk_ref[:, cols], c, sl, sh).astype(BF16)
        o_ref[:, OFF_V:OFF_V + KV_WIDTH] = dv_ref[...].astype(BF16)

    tab = pl.BlockSpec((CHUNK, HEAD_DIM), lambda i: (i, 0))
    pad = pl.BlockSpec((CHUNK, KV_WIDTH), lambda i: (i + 1, 0))
    return _call("rope_bwd", body, grid=(S // CHUNK,),
                 in_specs=[pl.BlockSpec((CHUNK, ATTN_WIDTH), lambda i: (i, 0)), pad, pad, tab, tab, tab],
                 out_specs=pl.BlockSpec((CHUNK, OFF_CA), lambda i: (i, 0)),
                 out_shape=jax.ShapeDtypeStruct((S, OFF_CA), BF16), sem=("parallel",))(dq, dkp, dvp, *tabs)


STACK = Q_PER_KV * CHUNK


def _stack_heads(ref, rows):
    return jnp.concatenate([ref[rows, r * HEAD_DIM:(r + 1) * HEAD_DIM] for r in range(Q_PER_KV)], axis=0)


def _stack_sinks(s_ref):
    return jnp.concatenate([jnp.broadcast_to(s_ref[r:r + 1, 0:1], (CHUNK, 1)) for r in range(Q_PER_KV)], axis=0)


def _attn_probs(q, kb, sk, n, S):
    scale = 1.0 / math.sqrt(HEAD_DIM)
    s = lax.dot_general(q, kb, NT, preferred_element_type=F32) * scale
    row = lax.broadcasted_iota(jnp.int32, (STACK, 3 * CHUNK), 0) & (CHUNK - 1)
    col = lax.broadcasted_iota(jnp.int32, (STACK, 3 * CHUNK), 1)
    kpos = (n - 1) * CHUNK + col
    valid = (jnp.abs(col - CHUNK - row) <= WINDOW) & (kpos >= 0) & (kpos < S)
    s = jnp.where(valid, s, jnp.finfo(F32).min)
    m = jnp.maximum(jnp.max(s, axis=1, keepdims=True), sk)
    e = jnp.exp(s - m)
    es = jnp.exp(sk - m)
    inv = 1.0 / (jnp.sum(e, axis=1, keepdims=True) + es)
    return e * inv, es * inv


def attn_fwd(qr, kp, vp, sink3):
    S = qr.shape[0]
    tq = min(512, S)
    gw = Q_PER_KV * HEAD_DIM

    def body(q_ref, k_ref, v_ref, s_ref, o_ref):
        i = pl.program_id(1)
        sk = _stack_sinks(s_ref)
        for b in range(tq // CHUNK):
            n = i * (tq // CHUNK) + b
            win = pl.ds(pl.multiple_of(n * CHUNK, CHUNK), 3 * CHUNK)
            kb, vb = k_ref[win, :], v_ref[win, :]
            rows = slice(b * CHUNK, (b + 1) * CHUNK)
            p, _ = _attn_probs(_stack_heads(q_ref, rows), kb, sk, n, S)
            o = jnp.dot(p.astype(BF16), vb, preferred_element_type=F32).astype(BF16)
            for r in range(Q_PER_KV):
                o_ref[rows, r * HEAD_DIM:(r + 1) * HEAD_DIM] = o[r * CHUNK:(r + 1) * CHUNK]

    kv = pl.BlockSpec((S + 2 * CHUNK, HEAD_DIM), lambda g, i: (0, g))
    return _call("attn_fwd", body, grid=(N_KV_HEADS, S // tq),
                 in_specs=[pl.BlockSpec((tq, gw), lambda g, i: (i, g)), kv, kv,
                           pl.BlockSpec((None, Q_PER_KV, LANES), lambda g, i: (g, 0, 0))],
                 out_specs=pl.BlockSpec((tq, gw), lambda g, i: (i, g)),
                 out_shape=jax.ShapeDtypeStruct((S, ATTN_WIDTH), BF16), sem=("parallel", "arbitrary"))(qr, kp, vp, sink3)


def attn_bwd(qr, kp, vp, sink3, dmix):
    S = qr.shape[0]
    tq = min(512, S)
    gw = Q_PER_KV * HEAD_DIM
    scale = 1.0 / math.sqrt(HEAD_DIM)

    def body(q_ref, k_ref, v_ref, s_ref, do_ref, dq_ref, dk_ref, dv_ref, ds_ref):
        i = pl.program_id(1)

        @pl.when(i == 0)
        def _():
            dk_ref[...] = jnp.zeros_like(dk_ref)
            dv_ref[...] = jnp.zeros_like(dv_ref)
            ds_ref[...] = jnp.zeros_like(ds_ref)

        sk = _stack_sinks(s_ref)
        for b in range(tq // CHUNK):
            n = i * (tq // CHUNK) + b
            win = pl.ds(pl.multiple_of(n * CHUNK, CHUNK), 3 * CHUNK)
            kb, vb = k_ref[win, :], v_ref[win, :]
            rows = slice(b * CHUNK, (b + 1) * CHUNK)
            q = _stack_heads(q_ref, rows)
            do = _stack_heads(do_ref, rows)
            p, p_sink = _attn_probs(q, kb, sk, n, S)
            dp = lax.dot_general(do, vb, NT, preferred_element_type=F32)
            delta = jnp.sum(p * dp, axis=1, keepdims=True)
            dsc = (p * (dp - delta) * scale).astype(BF16)
            dq = jnp.dot(dsc, kb, preferred_element_type=F32)
            dsink = -p_sink * delta
            for r in range(Q_PER_KV):
                head = slice(r * CHUNK, (r + 1) * CHUNK)
                dq_ref[rows, r * HEAD_DIM:(r + 1) * HEAD_DIM] = dq[head]
                ds_ref[r:r + 1, :] += jnp.broadcast_to(jnp.sum(dsink[head], axis=0, keepdims=True), (1, LANES))
            dk_ref[win, :] += lax.dot_general(dsc, q, TN, preferred_element_type=F32)
            dv_ref[win, :] += lax.dot_general(p.astype(BF16), do, TN, preferred_element_type=F32)

    kv = pl.BlockSpec((S + 2 * CHUNK, HEAD_DIM), lambda g, i: (0, g))
    qspec = pl.BlockSpec((tq, gw), lambda g, i: (i, g))
    sspec = pl.BlockSpec((None, Q_PER_KV, LANES), lambda g, i: (g, 0, 0))
    padshape = jax.ShapeDtypeStruct((S + 2 * CHUNK, KV_WIDTH), F32)
    return _call("attn_bwd", body, grid=(N_KV_HEADS, S // tq),
                 in_specs=[qspec, kv, kv, sspec, qspec],
                 out_specs=[qspec, kv, kv, sspec],
                 out_shape=[jax.ShapeDtypeStruct((S, ATTN_WIDTH), F32), padshape, padshape,
                            jax.ShapeDtypeStruct((N_KV_HEADS, Q_PER_KV, LANES), F32)],
                 sem=("parallel", "arbitrary"))(qr, kp, vp, sink3, dmix)


CONV_TILE = 256


def _fill_padded(dst_ref, value, S):
    zero = jnp.zeros((CONV_PAD, LANES), F32)
    dst_ref[0:CONV_PAD, :] = zero
    dst_ref[CONV_PAD + S:2 * CONV_PAD + S, :] = zero
    dst_ref[CONV_PAD:CONV_PAD + S, :] = value


def conv_dw_fwd(z, w32, b):
    S = z.shape[0]
    T = min(CONV_TILE, S)
    lo = CONV_PAD - (CONV_KERNEL - 1) // 2

    def body(a_ref, g_ref, w_ref, b_ref, o_ref, c0_ref):
        _fill_padded(c0_ref, a_ref[...].astype(F32) * _sigmoid(g_ref[...].astype(F32)), S)

        def tile(t, carry):
            base = pl.multiple_of(t * T, T)
            acc = jnp.broadcast_to(b_ref[...], (T, LANES))
            for j in range(CONV_KERNEL):
                acc = acc + w_ref[j:j + 1, :] * c0_ref[pl.ds(base + lo + j, T), :]
            o_ref[pl.ds(base, T), :] = acc
            return carry

        lax.fori_loop(0, S // T, tile, 0)

    nca, ncg = OFF_CA // LANES, OFF_CG // LANES
    return _call("conv_dw_fwd", body, grid=(CONV_WIDTH // LANES,),
                 in_specs=[pl.BlockSpec((S, LANES), lambda cb: (0, nca + cb)), pl.BlockSpec((S, LANES), lambda cb: (0, ncg + cb)),
                           pl.BlockSpec((32, LANES), lambda cb: (0, cb)), pl.BlockSpec((1, LANES), lambda cb: (0, cb))],
                 out_specs=pl.BlockSpec((S, LANES), lambda cb: (0, cb)),
                 out_shape=jax.ShapeDtypeStruct((S, CONV_WIDTH), F32),
                 scratch=[pltpu.VMEM((S + 2 * CONV_PAD, LANES), F32)], sem=("parallel",))(z, z, w32, b)


def _ln_stats(x):
    mu = jnp.mean(x, axis=-1, keepdims=True)
    xc = x - mu
    rs = lax.rsqrt(jnp.mean(xc * xc, axis=-1, keepdims=True) + EPS)
    return xc * rs, rs


def _ln_bwd(dy, xh, rs, g):
    dxh = dy * g
    return rs * (dxh - jnp.mean(dxh, axis=-1, keepdims=True) - xh * jnp.mean(dxh * xh, axis=-1, keepdims=True))


def conv_ln_fwd(c1, g, b):
    S = c1.shape[0]
    T = min(512, S)

    def body(x_ref, g_ref, b_ref, o_ref):
        xh, _ = _ln_stats(x_ref[...])
        y = xh * g_ref[...] + b_ref[...]
        o_ref[...] = (y * _sigmoid(y)).astype(BF16)

    row = pl.BlockSpec((T, CONV_WIDTH), lambda i: (i, 0))
    vec = pl.BlockSpec((1, CONV_WIDTH), lambda i: (0, 0))
    return _call("conv_ln_fwd", body, grid=(S // T,), in_specs=[row, vec, vec], out_specs=row,
                 out_shape=jax.ShapeDtypeStruct((S, CONV_WIDTH), BF16), sem=("parallel",))(c1, g, b)


def _acc_out(ref, value):
    @pl.when(pl.program_id(0) == 0)
    def _():
        ref[...] = value

    @pl.when(pl.program_id(0) > 0)
    def _():
        ref[...] += value


def conv_ln_bwd(dmix, c1, g, b):
    S = c1.shape[0]
    T = min(512, S)

    def body(d_ref, x_ref, g_ref, b_ref, dx_ref, dg_ref, db_ref):
        xh, rs = _ln_stats(x_ref[...])
        gv = g_ref[...]
        y = xh * gv + b_ref[...]
        sg = _sigmoid(y)
        dy = d_ref[...].astype(F32) * sg * (1.0 + y * (1.0 - sg))
        dx_ref[...] = _ln_bwd(dy, xh, rs, gv)
        _acc_out(dg_ref, jnp.sum(dy * xh, axis=0, keepdims=True))
        _acc_out(db_ref, jnp.sum(dy, axis=0, keepdims=True))

    row = pl.BlockSpec((T, CONV_WIDTH), lambda i: (i, 0))
    vec = pl.BlockSpec((1, CONV_WIDTH), lambda i: (0, 0))
    vshape = jax.ShapeDtypeStruct((1, CONV_WIDTH), F32)
    return _call("conv_ln_bwd", body, grid=(S // T,),
                 in_specs=[pl.BlockSpec((T, CONV_WIDTH), lambda i: (i, ATTN_WIDTH // CONV_WIDTH)), row, vec, vec],
                 out_specs=[row, vec, vec], out_shape=[jax.ShapeDtypeStruct((S, CONV_WIDTH), F32), vshape, vshape],
                 sem=("arbitrary",))(dmix, c1, g, b)


def conv_dw_bwd(dc1, z, w32):
    S = z.shape[0]
    T = min(CONV_TILE, S)
    half = (CONV_KERNEL - 1) // 2
    lo = CONV_PAD - half

    def body(d_ref, a_ref, g_ref, w_ref, da_ref, dg_ref, dw_ref, db_ref, c0_ref, d1_ref, wacc_ref):
        av = a_ref[...].astype(F32)
        sg = _sigmoid(g_ref[...].astype(F32))
        _fill_padded(c0_ref, av * sg, S)
        _fill_padded(d1_ref, d_ref[...], S)
        wacc_ref[...] = jnp.zeros_like(wacc_ref)

        def tile(t, carry):
            base = pl.multiple_of(t * T, T)
            d1 = d_ref[pl.ds(base, T), :]
            acc = jnp.zeros((T, LANES), F32)
            for j in range(CONV_KERNEL):
                acc = acc + w_ref[j:j + 1, :] * d1_ref[pl.ds(base + CONV_PAD + half - j, T), :]
                prod = d1 * c0_ref[pl.ds(base + lo + j, T), :]
                wacc_ref[j] += jnp.sum(prod.reshape(T // 8, 8, LANES), axis=0)
            rows = pl.ds(base, T)
            a_t = a_ref[rows, :].astype(F32)
            s_t = _sigmoid(g_ref[rows, :].astype(F32))
            da_ref[rows, :] = (acc * s_t).astype(BF16)
            dg_ref[rows, :] = (acc * a_t * s_t * (1.0 - s_t)).astype(BF16)
            return carry

        lax.fori_loop(0, S // T, tile, 0)
        dw_ref[...] = jnp.sum(wacc_ref[...], axis=1)
        db_ref[...] = jnp.sum(d_ref[...], axis=0, keepdims=True)

    nca, ncg = OFF_CA // LANES, OFF_CG // LANES
    col = pl.BlockSpec((S, LANES), lambda cb: (0, cb))
    oshape = jax.ShapeDtypeStruct((S, CONV_WIDTH), BF16)
    return _call("conv_dw_bwd", body, grid=(CONV_WIDTH // LANES,),
                 in_specs=[col, pl.BlockSpec((S, LANES), lambda cb: (0, nca + cb)), pl.BlockSpec((S, LANES), lambda cb: (0, ncg + cb)),
                           pl.BlockSpec((32, LANES), lambda cb: (0, cb))],
                 out_specs=[col, col, pl.BlockSpec((32, LANES), lambda cb: (0, cb)), pl.BlockSpec((1, LANES), lambda cb: (0, cb))],
                 out_shape=[oshape, oshape, jax.ShapeDtypeStruct((32, CONV_WIDTH), F32), jax.ShapeDtypeStruct((1, CONV_WIDTH), F32)],
                 scratch=[pltpu.VMEM((S + 2 * CONV_PAD, LANES), F32), pltpu.VMEM((S + 2 * CONV_PAD, LANES), F32),
                          pltpu.VMEM((32, 8, LANES), F32)], sem=("parallel",))(dc1, z, z, w32)


_INV_SQRT2 = 1.0 / math.sqrt(2.0)
_INV_SQRT2PI = 1.0 / math.sqrt(2.0 * math.pi)


def _gelu(x):
    return 0.5 * x * (1.0 + lax.erf(x * _INV_SQRT2))


def _gelu_grad(x):
    return 0.5 * (1.0 + lax.erf(x * _INV_SQRT2)) + x * jnp.exp(-0.5 * x * x) * _INV_SQRT2PI


def sgu_fwd(z, g, b, ws, bs):
    S = z.shape[0]
    T = min(512, S)

    def body(u_ref, v_ref, g_ref, b_ref, ws_ref, bs_ref, o_ref):
        xh, _ = _ln_stats(_gelu(v_ref[...].astype(F32)))
        vn = (xh * g_ref[...] + b_ref[...]).astype(BF16)
        for ch in range(T // CHUNK):
            rows = slice(ch * CHUNK, (ch + 1) * CHUNK)
            for h in range(SGU_HEADS):
                cols = slice(h * HEAD_DIM, (h + 1) * HEAD_DIM)
                sp = jnp.dot(ws_ref[h], vn[rows, cols], preferred_element_type=F32) + bs_ref[h]
                o_ref[rows, cols] = (_gelu(u_ref[rows, cols].astype(F32)) * sp).astype(BF16)

    vec = pl.BlockSpec((1, SGU_WIDTH), lambda i: (0, 0))
    full = pl.BlockSpec((SGU_HEADS, CHUNK, CHUNK), lambda i: (0, 0, 0))
    return _call("sgu_fwd", body, grid=(S // T,),
                 in_specs=[pl.BlockSpec((T, SGU_WIDTH), lambda i: (i, OFF_U // SGU_WIDTH)),
                           pl.BlockSpec((T, SGU_WIDTH), lambda i: (i, OFF_VV // SGU_WIDTH)), vec, vec, full, full],
                 out_specs=pl.BlockSpec((T, SGU_WIDTH), lambda i: (i, 0)),
                 out_shape=jax.ShapeDtypeStruct((S, SGU_WIDTH), BF16), sem=("parallel",))(z, z, g, b, ws, bs)


def sgu_bwd(z, dmix, g, b, ws, bs):
    S = z.shape[0]
    T = min(512, S)

    def body(u_ref, v_ref, d_ref, g_ref, b_ref, ws_ref, bs_ref, du_ref, dv_ref, dws_ref, dbs_ref, dg_ref, db_ref, dvn_ref):
        @pl.when(pl.program_id(0) == 0)
        def _():
            dws_ref[...] = jnp.zeros_like(dws_ref)
            dbs_ref[...] = jnp.zeros_like(dbs_ref)

        vraw = v_ref[...].astype(F32)
        xh, rs = _ln_stats(_gelu(vraw))
        gv = g_ref[...]
        vn = (xh * gv + b_ref[...]).astype(BF16)
        for ch in range(T // CHUNK):
            rows = slice(ch * CHUNK, (ch + 1) * CHUNK)
            for h in range(SGU_HEADS):
                cols = slice(h * HEAD_DIM, (h + 1) * HEAD_DIM)
                w = ws_ref[h]
                vb = vn[rows, cols]
                sp = jnp.dot(w, vb, preferred_element_type=F32) + bs_ref[h]
                uraw = u_ref[rows, cols].astype(F32)
                dout = d_ref[rows, cols].astype(F32)
                du_ref[rows, cols] = (dout * sp * _gelu_grad(uraw)).astype(BF16)
                dsp = dout * _gelu(uraw)
                dspb = dsp.astype(BF16)
                dvn_ref[rows, cols] = lax.dot_general(w, dspb, TN, preferred_element_type=F32)
                dws_ref[h] += lax.dot_general(dspb, vb, NT, preferred_element_type=F32)
                dbs_ref[h] += jnp.sum(dsp, axis=1, keepdims=True)
        dvn = dvn_ref[...]
        dv_ref[...] = (_ln_bwd(dvn, xh, rs, gv) * _gelu_grad(vraw)).astype(BF16)
        _acc_out(dg_ref, jnp.sum(dvn * xh, axis=0, keepdims=True))
        _acc_out(db_ref, jnp.sum(dvn, axis=0, keepdims=True))

    vec = pl.BlockSpec((1, SGU_WIDTH), lambda i: (0, 0))
    full = pl.BlockSpec((SGU_HEADS, CHUNK, CHUNK), lambda i: (0, 0, 0))
    row = pl.BlockSpec((T, SGU_WIDTH), lambda i: (i, 0))
    oshape = jax.ShapeDtypeStruct((S, SGU_WIDTH), BF16)
    vshape = jax.ShapeDtypeStruct((1, SGU_WIDTH), F32)
    return _call("sgu_bwd", body, grid=(S // T,),
                 in_specs=[pl.BlockSpec((T, SGU_WIDTH), lambda i: (i, OFF_U // SGU_WIDTH)),
                           pl.BlockSpec((T, SGU_WIDTH), lambda i: (i, OFF_VV // SGU_WIDTH)),
                           pl.BlockSpec((T, SGU_WIDTH), lambda i: (i, (ATTN_WIDTH + CONV_WIDTH) // SGU_WIDTH)), vec, vec, full, full],
                 out_specs=[row, row, full, pl.BlockSpec((SGU_HEADS, CHUNK, 1), lambda i: (0, 0, 0)), vec, vec],
                 out_shape=[oshape, oshape, jax.ShapeDtypeStruct((SGU_HEADS, CHUNK, CHUNK), F32),
                            jax.ShapeDtypeStruct((SGU_HEADS, CHUNK, 1), F32), vshape, vshape],
                 scratch=[pltpu.VMEM((T, SGU_WIDTH), F32)], sem=("arbitrary",))(z, z, dmix, g, b, ws, bs)


def _row_tile(rows, cols, n_arrays):
    budget = (24 * 1024 * 1024) // (n_arrays * 2 * 4 * cols)
    t = min(rows, max(16, budget // 16 * 16))
    while rows % t:
        t -= 16
    return t


def add_sibling_half(grad, recv, c_idx):
    J, R, C = grad.shape
    hr = R // 2
    tr = _row_tile(hr, C, 3)
    nb = hr // tr

    def body(c_ref, g_ref, r_ref, o_ref):
        o_ref[...] = (g_ref[...].astype(F32) + r_ref[...].astype(F32)).astype(BF16)

    grid_spec = pltpu.PrefetchScalarGridSpec(
        num_scalar_prefetch=1, grid=(J, nb),
        in_specs=[pl.BlockSpec((None, tr, C), lambda j, i, c: (j, c[0] * nb + i, 0)),
                  pl.BlockSpec((None, tr, C), lambda j, i, c: (j, i, 0))],
        out_specs=pl.BlockSpec((None, tr, C), lambda j, i, c: (j, i, 0)))
    return pl.pallas_call(body, name="add_sibling_half", grid_spec=grid_spec,
                          out_shape=jax.ShapeDtypeStruct((J, hr, C), BF16),
                          compiler_params=pltpu.CompilerParams(vmem_limit_bytes=VMEM_LIMIT,
                                                               dimension_semantics=("parallel", "parallel")))(c_idx, grad, recv)


def sum_chips(own, others, stack, x_idx, y_idx, layer):
    R, C = own.shape[1:]
    tr = _row_tile(R, C, 4)

    def body(x_ref, y_ref, own_ref, oth_ref, stack_ref, o_ref):
        acc = own_ref[...].astype(F32)
        for j in range(3):
            acc = acc + oth_ref[j].astype(F32)
        o_ref[...] = acc

    grid_spec = pltpu.PrefetchScalarGridSpec(
        num_scalar_prefetch=2, grid=(R // tr,),
        in_specs=[pl.BlockSpec((None, tr, C), lambda i, xr, yr: (2 * xr[0] + yr[0], i, 0)),
                  pl.BlockSpec((3, tr, C), lambda i, xr, yr: (0, i, 0)),
                  pl.BlockSpec(memory_space=pl.ANY)],
        out_specs=pl.BlockSpec((None, tr, C), lambda i, xr, yr: (layer, i, 0)))
    return pl.pallas_call(body, name="sum_chips", grid_spec=grid_spec,
                          out_shape=jax.ShapeDtypeStruct(stack.shape, F32), input_output_aliases={4: 0},
                          compiler_params=pltpu.CompilerParams(vmem_limit_bytes=VMEM_LIMIT,
                                                               dimension_semantics=("parallel",)))(x_idx, y_idx, own, others, stack)


def adamw_halves(w, mine, theirs, m, v, c_idx):
    L, R, C = w.shape
    hr = R // 2
    tr = _row_tile(hr, C, 9)
    nb = hr // tr

    def body(c_ref, w_ref, a_ref, b_ref, m_ref, v_ref, g_ref, d_ref, nm_ref, nv_ref):
        gv = jnp.where(pl.program_id(1) == c_ref[0], a_ref[...], b_ref[...])
        g_ref[...] = gv
        nm = ADAM_B1 * m_ref[...] + (1.0 - ADAM_B1) * gv
        nv = ADAM_B2 * v_ref[...] + (1.0 - ADAM_B2) * (gv * gv)
        m_hat = nm / (1.0 - ADAM_B1 ** ADAM_STEP)
        v_hat = nv / (1.0 - ADAM_B2 ** ADAM_STEP)
        d_ref[...] = -ADAM_LR * (m_hat / (jnp.sqrt(v_hat) + ADAM_EPS) + ADAM_WD * w_ref[...])
        nm_ref[...] = nm
        nv_ref[...] = nv

    full = pl.BlockSpec((None, tr, C), lambda l, h, i, c: (l, h * nb + i, 0))
    a_spec = pl.BlockSpec((None, tr, C), lambda l, h, i, c: (l, jnp.where(h == c[0], i, 0), 0))
    b_spec = pl.BlockSpec((None, tr, C), lambda l, h, i, c: (l, jnp.where(h == c[0], 0, i), 0))
    grid_spec = pltpu.PrefetchScalarGridSpec(num_scalar_prefetch=1, grid=(L, 2, nb),
                                             in_specs=[full, a_spec, b_spec, full, full], out_specs=[full] * 4)
    shape = jax.ShapeDtypeStruct((L, R, C), F32)
    return pl.pallas_call(body, name="adamw_halves", grid_spec=grid_spec, out_shape=[shape] * 4,
                          compiler_params=pltpu.CompilerParams(vmem_limit_bytes=VMEM_LIMIT,
                                                               dimension_semantics=("parallel", "arbitrary", "arbitrary")))(
        c_idx, w, mine, theirs, m, v)


def adamw(w, g, m, v):
    R, C = w.shape
    tr = _row_tile(R, C, 7)

    def body(w_ref, g_ref, m_ref, v_ref, d_ref, nm_ref, nv_ref):
        gv = g_ref[...]
        nm = ADAM_B1 * m_ref[...] + (1.0 - ADAM_B1) * gv
        nv = ADAM_B2 * v_ref[...] + (1.0 - ADAM_B2) * (gv * gv)
        m_hat = nm / (1.0 - ADAM_B1 ** ADAM_STEP)
        v_hat = nv / (1.0 - ADAM_B2 ** ADAM_STEP)
        d_ref[...] = -ADAM_LR * (m_hat / (jnp.sqrt(v_hat) + ADAM_EPS) + ADAM_WD * w_ref[...])
        nm_ref[...] = nm
        nv_ref[...] = nv

    spec = pl.BlockSpec((tr, C), lambda i: (i, 0))
    shape = jax.ShapeDtypeStruct((R, C), F32)
    return _call("adamw", body, grid=(R // tr,), in_specs=[spec] * 4, out_specs=[spec] * 3, out_shape=[shape] * 3,
                 sem=("parallel",))(w, g, m, v)


def _place():
    x, y, c = lax.axis_index("x"), lax.axis_index("y"), lax.axis_index("c")
    chips = [(1 - x, y), (x, 1 - y), (1 - x, 1 - y)]
    return x, y, c, chips


def _remote(src, dst, send_sem, recv_sem, dev):
    return pltpu.make_async_remote_copy(src_ref=src, dst_ref=dst, send_sem=send_sem, recv_sem=recv_sem,
                                        device_id=dev, device_id_type=MESH)


EFFECT = pltpu.SideEffectType.DATAFLOW_SIDE_EFFECTING
SEM = pl.BlockSpec(memory_space=pltpu.SEMAPHORE)
ANY = pl.BlockSpec(memory_space=pl.ANY)
TOKEN = jax.ShapeDtypeStruct((8, LANES), F32)


def _in_hbm(a):
    return pltpu.with_memory_space_constraint(a, pltpu.HBM)


def _gather_copies(shards, lands, send_sems, recv_sems):
    x, y, c, chips = _place()
    me = 2 * x + y
    copies = []
    for k in range(len(shards)):
        hr = shards[k].shape[0] // 2
        mine = pl.ds(pl.multiple_of(c * hr, 8), hr)
        for t, (px, py) in enumerate(chips):
            copies.append(_remote(shards[k].at[mine, :], lands[k].at[me, mine, :], send_sems.at[4 * k + t], recv_sems.at[4 * k + t],
                                  (px, py, c)))
        copies.append(_remote(shards[k], lands[k].at[me], send_sems.at[4 * k + 3], recv_sems.at[4 * k + 3], (x, y, 1 - c)))
    return copies


def _gather_landings(lands, send_sems, recv_sems):
    x, y, c, chips = _place()
    me = 2 * x + y
    landings = []
    for k in range(len(lands)):
        hr = lands[k].shape[1] // 2
        mine = pl.ds(pl.multiple_of(c * hr, 8), hr)
        for t, (px, py) in enumerate(chips):
            dst = lands[k].at[2 * px + py, mine, :]
            landings.append(_remote(dst, dst, send_sems.at[4 * k + t], recv_sems.at[4 * k + t], (px, py, c)))
        dst = lands[k].at[me]
        landings.append(_remote(dst, dst, send_sems.at[4 * k + 3], recv_sems.at[4 * k + 3], (x, y, 1 - c)))
    return landings


def gather_start(shards, after):
    n = len(shards)

    def body(*refs):
        srcs, lands_in = refs[:n], refs[n:2 * n]
        send_sems, recv_sems = refs[2 * n + 1], refs[2 * n + 2]
        token = refs[-1]
        for cp in _gather_copies(srcs, lands_in, send_sems, recv_sems):
            cp.start()
        token[...] = jnp.zeros_like(token)

    lands = [lax.empty((N_CHIPS,) + s.shape, s.dtype) for s in shards]
    outs = pl.pallas_call(
        body, name="gather_start", in_specs=[HBM] * (2 * n) + [ANY],
        out_specs=[SEM, SEM] + [HBM] * (2 * n) + [VMEM_SPEC],
        out_shape=[pltpu.SemaphoreType.DMA((4 * n,)), pltpu.SemaphoreType.DMA((4 * n,))]
        + [pltpu.HBM(s.shape, s.dtype) for s in shards] + [pltpu.HBM(l.shape, l.dtype) for l in lands] + [TOKEN],
        input_output_aliases={i: 2 + i for i in range(2 * n)},
        compiler_params=pltpu.CompilerParams(has_side_effects=EFFECT),
    )(*[_in_hbm(s) for s in shards], *[_in_hbm(l) for l in lands], after)
    return outs[0], outs[1], outs[2:2 + n], outs[2 + n:2 + 2 * n], outs[-1]


def gather_wait(send_sems, recv_sems, shards, lands, after):
    n = len(shards)

    def body(*refs):
        srcs, lands_in = refs[:n], refs[n:2 * n]
        send, recv = refs[2 * n], refs[2 * n + 1]
        for cp in _gather_copies(srcs, lands_in, send, recv):
            cp.wait_send()
        for cp in _gather_landings(lands_in, send, recv):
            cp.wait_recv()

    outs = pl.pallas_call(
        body, name="gather_wait", in_specs=[HBM] * (2 * n) + [SEM, SEM, ANY], out_specs=[HBM] * (2 * n),
        out_shape=[pltpu.HBM(s.shape, s.dtype) for s in shards] + [pltpu.HBM(l.shape, l.dtype) for l in lands],
        input_output_aliases={i: i for i in range(2 * n)},
        compiler_params=pltpu.CompilerParams(has_side_effects=EFFECT),
    )(*shards, *lands, send_sems, recv_sems, after)
    return outs[n:]


def forward_halves(lands):
    n = len(lands)

    def body(*refs):
        ins, outs = refs[:n], refs[n:2 * n]
        send_sems, recv_sems = refs[2 * n:]
        x, y, c, chips = _place()
        sibling = (x, y, 1 - c)
        sends = []
        for k in range(n):
            hr = ins[k].shape[1] // 2
            mine = pl.ds(pl.multiple_of(c * hr, 8), hr)
            for t, (px, py) in enumerate(chips):
                cp = _remote(ins[k].at[2 * px + py, mine, :], outs[k].at[2 * px + py, mine, :],
                             send_sems.at[k, t], recv_sems.at[k, t], sibling)
                cp.start()
                sends.append(cp)
        for k in range(n):
            hr = ins[k].shape[1] // 2
            other = pl.ds(pl.multiple_of((1 - c) * hr, 8), hr)
            for t, (px, py) in enumerate(chips):
                dst = outs[k].at[2 * px + py, other, :]
                _remote(dst, dst, send_sems.at[k, t], recv_sems.at[k, t], sibling).wait_recv()
        for cp in sends:
            cp.wait_send()

    return pl.pallas_call(
        body, name="forward_halves", in_specs=[HBM] * n, out_specs=[HBM] * n,
        out_shape=[jax.ShapeDtypeStruct(l.shape, l.dtype) for l in lands],
        input_output_aliases={i: i for i in range(n)},
        scratch_shapes=[pltpu.SemaphoreType.DMA((n, 3)), pltpu.SemaphoreType.DMA((n, 3))],
    )(*lands)


def gather_small(block):
    def body(in_ref, out_ref, send_sems, recv_sems):
        x, y, c, chips = _place()
        me = 2 * x + y
        out_ref[me] = in_ref[...]
        sends = []
        for t, (px, py) in enumerate(chips):
            cp = _remote(in_ref, out_ref.at[me], send_sems.at[t], recv_sems.at[t], (px, py, c))
            cp.start()
            sends.append(cp)
        for t, (px, py) in enumerate(chips):
            landed = out_ref.at[2 * px + py]
            _remote(landed, landed, send_sems.at[t], recv_sems.at[t], (px, py, c)).wait_recv()
        for cp in sends:
            cp.wait_send()

    return pl.pallas_call(
        body, name="gather_small", in_specs=[VMEM_SPEC], out_specs=VMEM_SPEC,
        out_shape=jax.ShapeDtypeStruct((N_CHIPS,) + block.shape, block.dtype),
        scratch_shapes=[pltpu.SemaphoreType.DMA((3,)), pltpu.SemaphoreType.DMA((3,))],
    )(block)


def exchange_sibling_halves(grads):
    n = len(grads)

    def body(*refs):
        ins, outs = refs[:n], refs[n:2 * n]
        send_sems, recv_sems = refs[2 * n:]
        x, y, c, _ = _place()
        copies = []
        for k in range(n):
            hr = ins[k].shape[1] // 2
            theirs = pl.ds(pl.multiple_of((1 - c) * hr, 8), hr)
            cp = _remote(ins[k].at[:, theirs, :], outs[k], send_sems.at[k], recv_sems.at[k], (x, y, 1 - c))
            cp.start()
            copies.append(cp)
        for cp in copies:
            cp.wait()

    return pl.pallas_call(
        body, name="exchange_sibling_halves", in_specs=[HBM] * n, out_specs=[HBM] * n,
        out_shape=[jax.ShapeDtypeStruct((g.shape[0], g.shape[1] // 2, g.shape[2]), g.dtype) for g in grads],
        scratch_shapes=[pltpu.SemaphoreType.DMA((n,)), pltpu.SemaphoreType.DMA((n,))],
    )(*grads)


def _chip_copies(parts, lands, send_sems, recv_sems):
    x, y, c, chips = _place()
    return [_remote(parts[k].at[2 * px + py], lands[k].at[t], send_sems.at[3 * k + t], recv_sems.at[3 * k + t], (px, py, c))
            for k in range(len(parts)) for t, (px, py) in enumerate(chips)]


def chip_parts_start(parts):
    n = len(parts)

    def body(*refs):
        srcs, lands_in = refs[:n], refs[n:2 * n]
        send_sems, recv_sems = refs[2 * n], refs[2 * n + 1]
        token = refs[-1]
        for cp in _chip_copies(srcs, lands_in, send_sems, recv_sems):
            cp.start()
        token[...] = jnp.zeros_like(token)

    lands = [lax.empty((3,) + p.shape[1:], p.dtype) for p in parts]
    outs = pl.pallas_call(
        body, name="chip_parts_start", in_specs=[HBM] * (2 * n), out_specs=[SEM, SEM] + [HBM] * (2 * n) + [VMEM_SPEC],
        out_shape=[pltpu.SemaphoreType.DMA((3 * n,)), pltpu.SemaphoreType.DMA((3 * n,))]
        + [pltpu.HBM(p.shape, p.dtype) for p in parts] + [pltpu.HBM(l.shape, l.dtype) for l in lands] + [TOKEN],
        input_output_aliases={i: 2 + i for i in range(2 * n)},
        compiler_params=pltpu.CompilerParams(has_side_effects=EFFECT),
    )(*[_in_hbm(p) for p in parts], *[_in_hbm(l) for l in lands])
    return outs[0], outs[1], outs[2:2 + n], outs[2 + n:2 + 2 * n], outs[-1]


def chip_parts_wait(send_sems, recv_sems, parts, lands, after):
    n = len(parts)

    def body(*refs):
        srcs, lands_in = refs[:n], refs[n:2 * n]
        send, recv = refs[2 * n], refs[2 * n + 1]
        for cp in _chip_copies(srcs, lands_in, send, recv):
            cp.wait_send()
            cp.wait_recv()

    outs = pl.pallas_call(
        body, name="chip_parts_wait", in_specs=[HBM] * (2 * n) + [SEM, SEM, ANY], out_specs=[HBM] * (2 * n),
        out_shape=[pltpu.HBM(p.shape, p.dtype) for p in parts] + [pltpu.HBM(l.shape, l.dtype) for l in lands],
        input_output_aliases={i: i for i in range(2 * n)},
        compiler_params=pltpu.CompilerParams(has_side_effects=EFFECT),
    )(*parts, *lands, send_sems, recv_sems, after)
    return outs[:n], outs[n:]


def exchange_final_halves(halves):
    n = len(halves)

    def body(*refs):
        ins, outs = refs[:n], refs[n:2 * n]
        send_sems, recv_sems = refs[2 * n:]
        x, y, c, _ = _place()
        copies = []
        for k in range(n):
            cp = _remote(ins[k], outs[k], send_sems.at[k], recv_sems.at[k], (x, y, 1 - c))
            cp.start()
            copies.append(cp)
        for cp in copies:
            cp.wait()

    return pl.pallas_call(
        body, name="exchange_final_halves", in_specs=[HBM] * n, out_specs=[HBM] * n,
        out_shape=[jax.ShapeDtypeStruct(h.shape, h.dtype) for h in halves],
        scratch_shapes=[pltpu.SemaphoreType.DMA((n,)), pltpu.SemaphoreType.DMA((n,))],
    )(*halves)


def allreduce_small(packed):
    R = packed.shape[0]

    def body(x_ref, sum_ref, all_ref, send_sems, recv_sems):
        x, y, c, chips = _place()
        me, sibling = (x, y, c), (x, y, 1 - c)

        def rows(px, py, pc):
            return all_ref.at[4 * px + 2 * py + pc]

        def copy(k, block, to, src=None):
            return _remote(rows(*block) if src is None else src, rows(*block), send_sems.at[k], recv_sems.at[k], to)

        all_ref[4 * x + 2 * y + c] = x_ref[...]
        first = [copy(0, me, sibling, src=x_ref)]
        first += [copy(1 + j, me, (*chip, c), src=x_ref) for j, chip in enumerate(chips)]
        for cp in first:
            cp.start()
        passed = [copy(4 + j, (*chip, c), sibling) for j, chip in enumerate(chips)]
        for j, chip in enumerate(chips):
            copy(1 + j, (*chip, c), me).wait_recv()
            passed[j].start()
        copy(0, sibling, me).wait_recv()
        for j, chip in enumerate(chips):
            copy(4 + j, (*chip, 1 - c), me).wait_recv()
        for cp in first + passed:
            cp.wait_send()

        def chunk(i, carry):
            rws = pl.ds(pl.multiple_of(i * PACK_ROWS, PACK_ROWS), PACK_ROWS)
            acc = all_ref[0, rws, :]
            for d in range(1, N_DEV):
                acc = acc + all_ref[d, rws, :]
            sum_ref[rws, :] = acc
            return carry

        lax.fori_loop(0, R // PACK_ROWS, chunk, 0)

    return pl.pallas_call(
        body, name="allreduce_small", in_specs=[VMEM_SPEC], out_specs=VMEM_SPEC,
        out_shape=jax.ShapeDtypeStruct((R, LANES), F32),
        scratch_shapes=[pltpu.VMEM((N_DEV, R, LANES), F32), pltpu.SemaphoreType.DMA((7,)), pltpu.SemaphoreType.DMA((7,))],
        compiler_params=pltpu.CompilerParams(vmem_limit_bytes=VMEM_LIMIT),
    )(packed)


def _mixer_fwd(x, p, tabs, token):
    h = rms_fwd(x, p["mix_norm_g"], token)
    z = mm_nn_cols(h, p["w_in"])
    qr, kp, vp = rope_fwd(z, tabs)
    attn = attn_fwd(qr, kp, vp, p["sink3"])
    c1 = conv_dw_fwd(z, p["conv_w32"], p["conv_dw_b"])
    conv = conv_ln_fwd(c1, p["conv_ln_g"], p["conv_ln_b"])
    sgu = sgu_fwd(z, p["sgu_ln_g"], p["sgu_ln_b"], p["sgu_w16"], p["sgu_b3"])
    mix = jnp.concatenate([attn, conv, sgu], axis=1)
    x_mid = mm_nn_rows_res(mix, p["w_out"], x)
    return x_mid, dict(x=x, h=h, z=z, qr=qr, kp=kp, vp=vp, c1=c1, mix=mix, x_mid=x_mid)


def _ffn_fwd(x_mid, p, token):
    h2 = rms_fwd(x_mid, p["ffn_norm_g"], token)
    gate, up, act = ffn_up(h2, p["w_gate"], p["w_up"])
    x_out = mm_nn_rows_res(act, p["w_down"], x_mid)
    return x_out, dict(h2=h2, gate=gate, up=up, act=act)


def _layer_fwd(x, p, tabs, token):
    x_mid, s_mix = _mixer_fwd(x, p, tabs, token)
    x_out, s_ffn = _ffn_fwd(x_mid, p, token)
    return x_out, {**s_mix, **s_ffn}


def _ffn_bwd(dx, dxb, p, s, token):
    dgate, dup = ffn_down_bwd(dxb, p["w_down"], s["gate"], s["up"], token)
    g_down = mm_tn_rows(s["act"], dxb)
    dh2 = mm_nt_cols([(dgate, p["w_gate"]), (dup, p["w_up"])], F32)
    g_gate = mm_tn_cols(s["h2"], dgate, N_CHIPS)
    g_up = mm_tn_cols(s["h2"], dup, N_CHIPS)
    dmid, dmidb, g_ffn_norm = rms_bwd(s["x_mid"], p["ffn_norm_g"], dh2, dx)
    return dmid, dmidb, [g_gate, g_up, g_down.reshape(N_CHIPS, -1, D_MODEL)], g_ffn_norm


def _mixer_bwd(dmid, dmidb, p, s, tabs, token):
    dmix = mm_nt_rows(dmidb, p["w_out"], token)
    g_out = mm_tn_rows(s["mix"], dmidb)
    dq, dkp, dvp, dsink = attn_bwd(s["qr"], s["kp"], s["vp"], p["sink3"], dmix)
    dqkv = rope_bwd(dq, dkp, dvp, tabs)
    dc1, g_cln_g, g_cln_b = conv_ln_bwd(dmix, s["c1"], p["conv_ln_g"], p["conv_ln_b"])
    dca, dcg, g_cw, g_cb = conv_dw_bwd(dc1, s["z"], p["conv_w32"])
    du, dv, g_sw, g_sb, g_sln_g, g_sln_b = sgu_bwd(s["z"], dmix, p["sgu_ln_g"], p["sgu_ln_b"], p["sgu_w16"], p["sgu_b3"])
    dz = jnp.concatenate([dqkv, dca, dcg, du, dv], axis=1)
    dh = mm_nt_cols([(dz, p["w_in"])], F32)
    g_in = mm_tn_cols(s["h"], dz, N_CHIPS)
    dx_in, dxb_in, g_mix_norm = rms_bwd(s["x"], p["mix_norm_g"], dh, dmid)
    small = dict(mix_norm_g=g_mix_norm, sink=dsink[:, :, 0].reshape(1, N_Q_HEADS), conv_dw_w=g_cw[:CONV_KERNEL],
                 conv_dw_b=g_cb, conv_ln_g=g_cln_g, conv_ln_b=g_cln_b, sgu_ln_g=g_sln_g, sgu_ln_b=g_sln_b,
                 sgu_w=g_sw, sgu_b=g_sb[:, :, 0])
    return dx_in, dxb_in, [g_in, g_out.reshape(N_CHIPS, -1, D_MODEL)], small


def _layer_bwd(dx, dxb, p, s, tabs, token):
    dmid, dmidb, ffn_big, g_ffn_norm = _ffn_bwd(dx, dxb, p, s, token)
    dx_in, dxb_in, mix_big, small = _mixer_bwd(dmid, dmidb, p, s, tabs, token)
    return dx_in, dxb_in, mix_big + ffn_big, dict(small, ffn_norm_g=g_ffn_norm)


def _mixer_weights(gathered):
    w_in, w_out = gathered
    return dict(w_in=w_in, w_out=w_out.reshape(-1, D_MODEL))


def _ffn_weights(gathered):
    w_gate, w_up, w_down = gathered
    return dict(w_gate=w_gate, w_up=w_up, w_down=w_down.reshape(-1, D_MODEL))


def _small_params(l, conv_w_full, mix_norm_g, sink, conv_dw_b, conv_ln_g, conv_ln_b, sgu_ln_g, sgu_ln_b, sgu_w, sgu_b,
                  ffn_norm_g):
    return dict(
        mix_norm_g=mix_norm_g[l:l + 1], ffn_norm_g=ffn_norm_g[l:l + 1],
        sink3=jnp.broadcast_to(sink[l].reshape(N_KV_HEADS, Q_PER_KV, 1), (N_KV_HEADS, Q_PER_KV, LANES)),
        conv_w32=jnp.pad(conv_w_full[l], ((0, 32 - CONV_KERNEL), (0, 0))),
        conv_dw_b=conv_dw_b[l:l + 1], conv_ln_g=conv_ln_g[l:l + 1], conv_ln_b=conv_ln_b[l:l + 1],
        sgu_ln_g=sgu_ln_g[l:l + 1], sgu_ln_b=sgu_ln_b[l:l + 1], sgu_w16=sgu_w[l].astype(BF16),
        sgu_b3=jnp.broadcast_to(sgu_b[l][:, :, None], (SGU_HEADS, CHUNK, CHUNK)))


_SMALL = ["mix_norm_g", "sink", "conv_dw_b", "conv_ln_g", "conv_ln_b", "sgu_ln_g", "sgu_ln_b", "sgu_w", "sgu_b", "ffn_norm_g",
          "final_norm_g"]


def _pack_rows(arrays):
    rows, counts = [], []
    for a in arrays:
        flat = a.reshape(-1)
        n = -(-flat.shape[0] // LANES)
        rows.append(jnp.pad(flat, (0, n * LANES - flat.shape[0])).reshape(n, LANES))
        counts.append(n)
    packed = jnp.concatenate(rows, axis=0)
    pad = -packed.shape[0] % PACK_ROWS
    return jnp.pad(packed, ((0, pad), (0, 0))), counts


def _unpack_rows(packed, counts, shapes):
    out, r = [], 0
    for n, shape in zip(counts, shapes):
        size = math.prod(shape)
        out.append(packed[r:r + n].reshape(-1)[:size].reshape(shape))
        r += n
    return out


def kernel(x, mix_norm_g, w_in, sink, conv_dw_w, conv_dw_b, conv_ln_g, conv_ln_b, sgu_ln_g, sgu_ln_b, sgu_w, sgu_b, w_out, ffn_norm_g, w_gate, w_up, w_down, final_norm_g, loss_target, m_mix_norm_g, m_w_in, m_sink, m_conv_dw_w, m_conv_dw_b, m_conv_ln_g, m_conv_ln_b, m_sgu_ln_g, m_sgu_ln_b, m_sgu_w, m_sgu_b, m_w_out, m_ffn_norm_g, m_w_gate, m_w_up, m_w_down, m_final_norm_g, v_mix_norm_g, v_w_in, v_sink, v_conv_dw_w, v_conv_dw_b, v_conv_ln_g, v_conv_ln_b, v_sgu_ln_g, v_sgu_ln_b, v_sgu_w, v_sgu_b, v_w_out, v_ffn_norm_g, v_w_gate, v_w_up, v_w_down, v_final_norm_g):
    S = x.shape[1]
    my_chip = 2 * lax.axis_index("x") + lax.axis_index("y")
    c_idx = lax.axis_index("c").astype(jnp.int32).reshape(1)
    big_w = [w_in, w_out, w_gate, w_up, w_down]
    big_m = [m_w_in, m_w_out, m_w_gate, m_w_up, m_w_down]
    big_v = [v_w_in, v_w_out, v_w_gate, v_w_up, v_w_down]
    n_kinds = len(big_w)

    x_idx = lax.axis_index("x").astype(jnp.int32).reshape(1)
    y_idx = lax.axis_index("y").astype(jnp.int32).reshape(1)
    conv_w_all = gather_small(conv_dw_w)
    conv_w_full = jnp.transpose(conv_w_all, (1, 2, 0, 3)).reshape(DEPTH, CONV_KERNEL, CONV_WIDTH)
    tabs = rope_tables(S)
    no_token = jnp.zeros(TOKEN.shape, TOKEN.dtype)

    mixer_kinds, ffn_kinds = [0, 1], [2, 3, 4]
    shards = [[w[l].astype(BF16) for w in big_w] for l in range(DEPTH)]

    def fetch(pending, after):
        send_sems, recv_sems, srcs, lands, _ = pending
        return forward_halves(gather_wait(send_sems, recv_sems, srcs, lands, after))

    first_mixer = gather_start([shards[0][k] for k in mixer_kinds], conv_w_all)
    first_ffn = gather_start([shards[0][k] for k in ffn_kinds], first_mixer[4])
    act = x[0]
    saved, params = [], []
    for l in range(DEPTH):
        p = _small_params(l, conv_w_full, mix_norm_g, sink, conv_dw_b, conv_ln_g, conv_ln_b, sgu_ln_g, sgu_ln_b, sgu_w, sgu_b,
                          ffn_norm_g)
        if l == 0:
            p.update(_mixer_weights(fetch(first_mixer, act)))
            x_mid, s_mix = _mixer_fwd(act, p, tabs, no_token)
            ffn_w = fetch(first_ffn, x_mid)
        else:
            gathered = fetch(pending, act)
            p.update(_mixer_weights(gathered[:2]))
            ffn_w = gathered[2:]
        token = no_token
        if l + 1 < DEPTH:
            pending = gather_start(shards[l + 1], ffn_w[0])
            token = pending[4]
        p.update(_ffn_weights(ffn_w))
        if l > 0:
            x_mid, s_mix = _mixer_fwd(act, p, tabs, token)
        act, s_ffn = _ffn_fwd(x_mid, p, token)
        params.append(p)
        saved.append({**s_mix, **s_ffn})
    loss_part, dx, dxb, g_final = final_loss(act, final_norm_g.reshape(1, D_MODEL), loss_target[0])
    loss = lax.psum(loss_part[0, 0], ("x", "y", "c"))

    halves = [jnp.zeros((DEPTH, w.shape[1] // 2, w.shape[2]), F32) for w in big_w]
    small_grads = [None] * DEPTH

    def reduce_start(layer, kinds, grads):
        recv = exchange_sibling_halves(grads)
        chip_sum = [add_sibling_half(g, r, c_idx) for g, r in zip(grads, recv)]
        send_sems, recv_sems, parts, lands, token = chip_parts_start(chip_sum)
        return (layer, kinds, send_sems, recv_sems, parts, lands), token

    def reduce_finish(pending, halves, after):
        layer, kinds, send_sems, recv_sems, parts, lands = pending
        parts, others = chip_parts_wait(send_sems, recv_sems, parts, lands, after)
        halves = list(halves)
        for i, k in enumerate(kinds):
            halves[k] = sum_chips(parts[i], others[i], halves[k], x_idx, y_idx, layer)
        return halves

    pending, token = None, no_token
    for l in reversed(range(DEPTH)):
        dmid, dmidb, ffn_big, g_ffn_norm = _ffn_bwd(dx, dxb, params[l], saved[l], token)
        mixer_token = no_token
        if l == 0:
            last_ffn, mixer_token = reduce_start(l, ffn_kinds, ffn_big)
        dx, dxb, mix_big, small = _mixer_bwd(dmid, dmidb, params[l], saved[l], tabs, mixer_token)
        small_grads[l] = dict(small, ffn_norm_g=g_ffn_norm)
        if pending is not None:
            halves = reduce_finish(pending, halves, dx)
        if l == 0:
            last_mixer, token = reduce_start(l, mixer_kinds, mix_big)
            halves = reduce_finish(last_ffn, halves, token)
            halves = reduce_finish(last_mixer, halves, halves[ffn_kinds[0]])
        else:
            pending, token = reduce_start(l, mixer_kinds + ffn_kinds, mix_big + ffn_big)
    sibling_halves = exchange_final_halves(halves)

    stacked = {n: jnp.stack([small_grads[l][n] for l in range(DEPTH)]) for n in small_grads[0]}
    stacked["final_norm_g"] = g_final
    packed, counts = _pack_rows([stacked[n] for n in _SMALL] + [stacked["conv_dw_w"]])
    reduced = allreduce_small(packed)
    small_w = dict(mix_norm_g=mix_norm_g, sink=sink, conv_dw_b=conv_dw_b, conv_ln_g=conv_ln_g, conv_ln_b=conv_ln_b,
                   sgu_ln_g=sgu_ln_g, sgu_ln_b=sgu_ln_b, sgu_w=sgu_w, sgu_b=sgu_b, ffn_norm_g=ffn_norm_g,
                   final_norm_g=final_norm_g)
    small_m = dict(mix_norm_g=m_mix_norm_g, sink=m_sink, conv_dw_b=m_conv_dw_b, conv_ln_g=m_conv_ln_g,
                   conv_ln_b=m_conv_ln_b, sgu_ln_g=m_sgu_ln_g, sgu_ln_b=m_sgu_ln_b, sgu_w=m_sgu_w, sgu_b=m_sgu_b,
                   ffn_norm_g=m_ffn_norm_g, final_norm_g=m_final_norm_g)
    small_v = dict(mix_norm_g=v_mix_norm_g, sink=v_sink, conv_dw_b=v_conv_dw_b, conv_ln_g=v_conv_ln_g,
                   conv_ln_b=v_conv_ln_b, sgu_ln_g=v_sgu_ln_g, sgu_ln_b=v_sgu_ln_b, sgu_w=v_sgu_w, sgu_b=v_sgu_b,
                   ffn_norm_g=v_ffn_norm_g, final_norm_g=v_final_norm_g)
    shapes = [small_w[n].shape for n in _SMALL] + [(DEPTH, CONV_KERNEL, CONV_WIDTH)]
    red = _unpack_rows(reduced, counts, shapes)
    g_small = dict(zip(_SMALL, red[:-1]))
    g_small["conv_dw_w"] = lax.dynamic_slice_in_dim(red[-1], my_chip * LANES, LANES, axis=2)
    small_w["conv_dw_w"], small_m["conv_dw_w"], small_v["conv_dw_w"] = conv_dw_w, m_conv_dw_w, v_conv_dw_w
    names = _SMALL + ["conv_dw_w"]
    pw, cnt = _pack_rows([small_w[n] for n in names])
    pg, _ = _pack_rows([g_small[n] for n in names])
    pm, _ = _pack_rows([small_m[n] for n in names])
    pv, _ = _pack_rows([small_v[n] for n in names])
    sd, sm, sv = adamw(pw, pg, pm, pv)
    shp = [small_w[n].shape for n in names]
    d_small = dict(zip(names, _unpack_rows(sd, cnt, shp)))
    m_small = dict(zip(names, _unpack_rows(sm, cnt, shp)))
    v_small = dict(zip(names, _unpack_rows(sv, cnt, shp)))

    big_names = ["w_in", "w_out", "w_gate", "w_up", "w_down"]
    g_big, d_big, m_big, v_big = {}, {}, {}, {}
    for k, n in enumerate(big_names):
        g_big[n], d_big[n], m_big[n], v_big[n] = adamw_halves(big_w[k], halves[k], sibling_halves[k], big_m[k], big_v[k], c_idx)

    order = ["mix_norm_g", "w_in", "sink", "conv_dw_w", "conv_dw_b", "conv_ln_g", "conv_ln_b", "sgu_ln_g", "sgu_ln_b",
             "sgu_w", "sgu_b", "w_out", "ffn_norm_g", "w_gate", "w_up", "w_down", "final_norm_g"]
    grads = {**g_small, **g_big}
    deltas = {**d_small, **d_big}
    new_m = {**m_small, **m_big}
    new_v = {**v_small, **v_big}
    return (loss, dx[None], *[grads[n] for n in order], *[deltas[n] for n in order],
            *[new_m[n] for n in order], *[new_v[n] for n in order])
```

```python
import functools
import math

import jax
import jax.numpy as jnp
from jax import lax
from jax.experimental import pallas as pl
from jax.experimental.pallas import tpu as pltpu

F32, BF16 = jnp.float32, jnp.bfloat16

D_MODEL = 2048
DEPTH = 4
HEAD_DIM = 128
N_Q_HEADS = 8
N_KV_HEADS = 2
Q_PER_KV = N_Q_HEADS // N_KV_HEADS
ATTN_WIDTH = N_Q_HEADS * HEAD_DIM
KV_WIDTH = N_KV_HEADS * HEAD_DIM
CONV_WIDTH = 512
CONV_KERNEL = 31
CONV_PAD = 16
SGU_WIDTH = 512
SGU_HEADS = 4
CHUNK = 128
IN_WIDTH = 3584
D_FF = 5632
WINDOW = 128
ROT_DIM = 32
ROPE_THETA = 500000.0
EPS = 1e-6
N_CHIPS = 4
N_DEV = 8
LANES = 128
PACK_ROWS = 64
OFF_K = ATTN_WIDTH
OFF_V = OFF_K + KV_WIDTH
OFF_CA = OFF_V + KV_WIDTH
OFF_CG = OFF_CA + CONV_WIDTH
OFF_U = OFF_CG + CONV_WIDTH
OFF_VV = OFF_U + SGU_WIDTH

ADAM_LR, ADAM_B1, ADAM_B2, ADAM_EPS, ADAM_WD, ADAM_STEP = 0.001, 0.9, 0.999, 1e-08, 0.01, 10

VMEM_LIMIT = 56 * 1024 * 1024
MESH = pl.DeviceIdType.MESH
HBM = pl.BlockSpec(memory_space=pltpu.HBM)
VMEM_SPEC = pl.BlockSpec(memory_space=pltpu.VMEM)


def _call(name, body, *, grid, in_specs, out_specs, out_shape, scratch=(), sem=None):
    params = dict(vmem_limit_bytes=VMEM_LIMIT)
    if sem is not None:
        params["dimension_semantics"] = sem
    return pl.pallas_call(
        body, name=name, grid=grid, in_specs=in_specs, out_specs=out_specs, out_shape=out_shape,
        scratch_shapes=list(scratch), compiler_params=pltpu.CompilerParams(**params))


def _sigmoid(x):
    return 1.0 / (1.0 + jnp.exp(-x))


def rms_fwd(x, g, token):
    S = x.shape[0]
    tm = min(512, S)

    def body(x_ref, g_ref, token_ref, o_ref):
        xv = x_ref[...]
        r = lax.rsqrt(jnp.mean(xv * xv, axis=-1, keepdims=True) + EPS)
        o_ref[...] = (xv * r * g_ref[...]).astype(BF16)

    return _call("rms_fwd", body, grid=(S // tm,),
                 in_specs=[pl.BlockSpec((tm, D_MODEL), lambda i: (i, 0)), pl.BlockSpec((1, D_MODEL), lambda i: (0, 0)),
                           pl.BlockSpec((8, LANES), lambda i: (0, 0))],
                 out_specs=pl.BlockSpec((tm, D_MODEL), lambda i: (i, 0)),
                 out_shape=jax.ShapeDtypeStruct((S, D_MODEL), BF16), sem=("parallel",))(x, g, token)


def _rms_bwd_math(xv, gv, dh):
    r = lax.rsqrt(jnp.mean(xv * xv, axis=-1, keepdims=True) + EPS)
    n = xv * r
    dn = dh * gv
    dx = r * (dn - n * jnp.mean(dn * n, axis=-1, keepdims=True))
    dg = jnp.sum(dh * n, axis=0, keepdims=True)
    return dx, dg


def rms_bwd(x, g, dh, dres):
    S = x.shape[0]
    tm = min(256, S)

    def body(x_ref, g_ref, dh_ref, dres_ref, dx_ref, dxb_ref, dg_ref):
        dx, dg = _rms_bwd_math(x_ref[...], g_ref[...], dh_ref[...])
        dx = dx + dres_ref[...]
        dx_ref[...] = dx
        dxb_ref[...] = dx.astype(BF16)

        @pl.when(pl.program_id(0) == 0)
        def _():
            dg_ref[...] = dg

        @pl.when(pl.program_id(0) > 0)
        def _():
            dg_ref[...] += dg

    row = pl.BlockSpec((tm, D_MODEL), lambda i: (i, 0))
    vec = pl.BlockSpec((1, D_MODEL), lambda i: (0, 0))
    return _call("rms_bwd", body, grid=(S // tm,), in_specs=[row, vec, row, row], out_specs=[row, row, vec],
                 out_shape=[jax.ShapeDtypeStruct((S, D_MODEL), F32), jax.ShapeDtypeStruct((S, D_MODEL), BF16),
                            jax.ShapeDtypeStruct((1, D_MODEL), F32)], sem=("arbitrary",))(x, g, dh, dres)


def final_loss(x, g, target):
    S = x.shape[0]
    tm = min(256, S)

    def body(x_ref, g_ref, t_ref, loss_ref, dx_ref, dxb_ref, dg_ref):
        xv = x_ref[...]
        gv = g_ref[...]
        r = lax.rsqrt(jnp.mean(xv * xv, axis=-1, keepdims=True) + EPS)
        err = xv * r * gv - t_ref[...]
        part = 0.5 * jnp.sum(jnp.mean(err * err, axis=-1, keepdims=True), axis=0, keepdims=True)
        dx, dg = _rms_bwd_math(xv, gv, err * (1.0 / D_MODEL))
        dx_ref[...] = dx
        dxb_ref[...] = dx.astype(BF16)

        @pl.when(pl.program_id(0) == 0)
        def _():
            dg_ref[...] = dg
            loss_ref[...] = part

        @pl.when(pl.program_id(0) > 0)
        def _():
            dg_ref[...] += dg
            loss_ref[...] += part

    row = pl.BlockSpec((tm, D_MODEL), lambda i: (i, 0))
    vec = pl.BlockSpec((1, D_MODEL), lambda i: (0, 0))
    one = pl.BlockSpec((1, 1), lambda i: (0, 0))
    return _call("final_loss", body, grid=(S // tm,), in_specs=[row, vec, row], out_specs=[one, row, row, vec],
                 out_shape=[jax.ShapeDtypeStruct((1, 1), F32), jax.ShapeDtypeStruct((S, D_MODEL), F32),
                            jax.ShapeDtypeStruct((S, D_MODEL), BF16), jax.ShapeDtypeStruct((1, D_MODEL), F32)],
                 sem=("arbitrary",))(x, g, target)


EPILOGUE_ROWS = 256
NN = (((1,), (0,)), ((), ()))
NT = (((1,), (1,)), ((), ()))
TN = (((0,), (0,)), ((), ()))


def _matmul(name, operands, in_specs, out_shape, out_specs, grid, pairs, dims, acc_shape, epilogue):
    n_in, n_out, nk = len(operands), len(out_shape), grid[-1]

    def body(*refs):
        ins, outs = refs[:n_in], refs[n_in:n_in + n_out]
        part = None
        for ia, ib in pairs:
            d = lax.dot_general(ins[ia][...], ins[ib][...], dims, preferred_element_type=F32)
            part = d if part is None else part + d
        if nk == 1:
            epilogue(part, ins, outs)
        else:
            acc = refs[-1]
            k = pl.program_id(len(grid) - 1)

            @pl.when(k == 0)
            def _():
                acc[...] = part

            @pl.when(k > 0)
            def _():
                acc[...] += part

            @pl.when(k == nk - 1)
            def _():
                epilogue(acc[...], ins, outs)

    scratch = [pltpu.VMEM(acc_shape, F32)] if nk > 1 else []
    sem = ("parallel",) * (len(grid) - 1) + ("arbitrary",)
    return _call(name, body, grid=grid, in_specs=in_specs, out_specs=out_specs, out_shape=out_shape,
                 scratch=scratch, sem=sem)(*operands)


def _store(dtype):
    def epilogue(acc, ins, outs):
        outs[0][...] = acc.astype(dtype)
    return epilogue


def mm_nn_cols(a, w):
    S, K = a.shape
    J, _, Ns = w.shape
    tm = min(512, S)
    return _matmul("mm_nn_cols", (a, w),
                   [pl.BlockSpec((tm, K), lambda j, i, k: (i, 0)), pl.BlockSpec((None, K, Ns), lambda j, i, k: (j, 0, 0))],
                   [jax.ShapeDtypeStruct((S, J * Ns), BF16)], [pl.BlockSpec((tm, Ns), lambda j, i, k: (i, j))],
                   (J, S // tm, 1), [(0, 1)], NN, None, _store(BF16))[0]


def ffn_up(h, wg, wu):
    S, K = h.shape
    J, _, Ns = wg.shape
    tm = min(512, S)

    sub = min(EPILOGUE_ROWS, tm)

    def body(h_ref, wg_ref, wu_ref, g_ref, u_ref, a_ref):
        for r in range(tm // sub):
            rows = slice(r * sub, (r + 1) * sub)
            hv = h_ref[rows, :]
            gv = jnp.dot(hv, wg_ref[...], preferred_element_type=F32)
            uv = jnp.dot(hv, wu_ref[...], preferred_element_type=F32)
            g_ref[rows, :] = gv.astype(BF16)
            u_ref[rows, :] = uv.astype(BF16)
            a_ref[rows, :] = (gv * _sigmoid(gv) * uv).astype(BF16)

    wspec = pl.BlockSpec((None, K, Ns), lambda j, i: (j, 0, 0))
    ospec = pl.BlockSpec((tm, Ns), lambda j, i: (i, j))
    oshape = jax.ShapeDtypeStruct((S, J * Ns), BF16)
    return _call("ffn_up", body, grid=(J, S // tm), in_specs=[pl.BlockSpec((tm, K), lambda j, i: (i, 0)), wspec, wspec],
                 out_specs=[ospec, ospec, ospec], out_shape=[oshape, oshape, oshape], sem=("parallel", "parallel"))(h, wg, wu)


def mm_nn_rows_res(a, w, res):
    S, K = a.shape
    N = w.shape[1]
    tm = min(512, S)
    tk = K if K <= 2048 else K // 4

    def epilogue(acc, ins, outs):
        outs[0][...] = acc + ins[2][...]

    return _matmul("mm_nn_rows_res", (a, w, res),
                   [pl.BlockSpec((tm, tk), lambda i, k: (i, k)), pl.BlockSpec((tk, N), lambda i, k: (k, 0)),
                    pl.BlockSpec((tm, N), lambda i, k: (i, 0))],
                   [jax.ShapeDtypeStruct((S, N), F32)], [pl.BlockSpec((tm, N), lambda i, k: (i, 0))],
                   (S // tm, K // tk), [(0, 1)], NN, (tm, N), epilogue)[0]


def mm_nt_cols(pairs_in, out_dtype):
    dz0, w0 = pairs_in[0]
    S = dz0.shape[0]
    J, K, Ns = w0.shape
    tm = min(512, S)
    operands, specs, pairs = [], [], []
    for dz, w in pairs_in:
        pairs.append((len(operands), len(operands) + 1))
        operands += [dz, w]
        specs += [pl.BlockSpec((tm, Ns), lambda i, j: (i, j)), pl.BlockSpec((None, K, Ns), lambda i, j: (j, 0, 0))]
    return _matmul("mm_nt_cols%d" % len(pairs_in), tuple(operands), specs,
                   [jax.ShapeDtypeStruct((S, K), out_dtype)], [pl.BlockSpec((tm, K), lambda i, j: (i, 0))],
                   (S // tm, J), pairs, NT, (tm, K), _store(out_dtype))[0]


def mm_nt_rows(dy, w, token):
    S, N = dy.shape
    K = w.shape[0]
    tm, tko = min(1024, S), 512
    return _matmul("mm_nt_rows", (dy, w, token),
                   [pl.BlockSpec((tm, N), lambda i, kk, z: (i, 0)), pl.BlockSpec((tko, N), lambda i, kk, z: (kk, 0)),
                    pl.BlockSpec((8, LANES), lambda i, kk, z: (0, 0))],
                   [jax.ShapeDtypeStruct((S, K), BF16)], [pl.BlockSpec((tm, tko), lambda i, kk, z: (i, kk))],
                   (S // tm, K // tko, 1), [(0, 1)], NT, None, _store(BF16))[0]


def ffn_down_bwd(dy, w, gate, up, token):
    S, N = dy.shape
    K = w.shape[0]
    tm, tko = min(1024, S), 512
    sub = min(EPILOGUE_ROWS, tm)

    def body(dy_ref, w_ref, g_ref, u_ref, token_ref, dg_ref, du_ref):
        for r in range(tm // sub):
            rows = slice(r * sub, (r + 1) * sub)
            dact = lax.dot_general(dy_ref[rows, :], w_ref[...], NT, preferred_element_type=F32)
            gv = g_ref[rows, :].astype(F32)
            uv = u_ref[rows, :].astype(F32)
            sg = _sigmoid(gv)
            dg_ref[rows, :] = (dact * uv * sg * (1.0 + gv * (1.0 - sg))).astype(BF16)
            du_ref[rows, :] = (dact * gv * sg).astype(BF16)

    tile = pl.BlockSpec((tm, tko), lambda i, kk: (i, kk))
    oshape = jax.ShapeDtypeStruct((S, K), BF16)
    return _call("ffn_down_bwd", body, grid=(S // tm, K // tko),
                 in_specs=[pl.BlockSpec((tm, N), lambda i, kk: (i, 0)), pl.BlockSpec((tko, N), lambda i, kk: (kk, 0)), tile, tile,
                           pl.BlockSpec((8, LANES), lambda i, kk: (0, 0))],
                 out_specs=[tile, tile], out_shape=[oshape, oshape], sem=("parallel", "parallel"))(dy, w, gate, up, token)


def mm_tn_cols(a, dz, J):
    S, M = a.shape
    Ns = dz.shape[1] // J
    tm, tk = 512, S
    return _matmul("mm_tn_cols", (a, dz),
                   [pl.BlockSpec((tk, tm), lambda j, m, k: (k, m)), pl.BlockSpec((tk, Ns), lambda j, m, k: (k, j))],
                   [jax.ShapeDtypeStruct((J, M, Ns), BF16)], [pl.BlockSpec((None, tm, Ns), lambda j, m, k: (j, m, 0))],
                   (J, M // tm, S // tk), [(0, 1)], TN, (tm, Ns), _store(BF16))[0]


def mm_tn_rows(a, dy):
    S, K = a.shape
    N = dy.shape[1]
    tm, tk = 512, min(2048, S)
    return _matmul("mm_tn_rows", (a, dy),
                   [pl.BlockSpec((tk, tm), lambda m, k: (k, m)), pl.BlockSpec((tk, N), lambda m, k: (k, 0))],
                   [jax.ShapeDtypeStruct((K, N), BF16)], [pl.BlockSpec((tm, N), lambda m, k: (m, 0))],
                   (K // tm, S // tk), [(0, 1)], TN, (tm, N), _store(BF16))[0]


def rope_tables(S):
    half = ROT_DIM // 2
    pos = jnp.arange(S, dtype=F32)
    inv = ROPE_THETA ** (-jnp.arange(0, ROT_DIM, 2, dtype=F32) / ROT_DIM)
    ang = pos[:, None] * inv[None, :]
    cos, sin = jnp.cos(ang), jnp.sin(ang)
    zeros = jnp.zeros((S, HEAD_DIM - ROT_DIM), F32)
    c = jnp.concatenate([cos, cos, jnp.ones((S, HEAD_DIM - ROT_DIM), F32)], axis=1)
    s_lo = jnp.concatenate([-sin, jnp.zeros((S, half), F32), zeros], axis=1)
    s_hi = jnp.concatenate([jnp.zeros((S, half), F32), sin, zeros], axis=1)
    return c, s_lo, s_hi


def _rope(t, c, s_lo, s_hi):
    half = ROT_DIM // 2
    return t * c + pltpu.roll(t, HEAD_DIM - half, 1) * s_lo + pltpu.roll(t, half, 1) * s_hi


def _unrope(d, c, s_lo, s_hi):
    half = ROT_DIM // 2
    return d * c + pltpu.roll(d * s_lo, half, 1) + pltpu.roll(d * s_hi, HEAD_DIM - half, 1)


def rope_fwd(z, tabs):
    S = z.shape[0]
    nb = S // CHUNK

    def body(q_ref, kv_ref, c_ref, sl_ref, sh_ref, qr_ref, kp_ref, vp_ref):
        i = pl.program_id(0)

        @pl.when(i == 0)
        def _():
            zero = jnp.zeros((CHUNK, KV_WIDTH), BF16)
            kp_ref[0:CHUNK, :] = zero
            vp_ref[0:CHUNK, :] = zero
            kp_ref[S + CHUNK:S + 2 * CHUNK, :] = zero
            vp_ref[S + CHUNK:S + 2 * CHUNK, :] = zero

        c, sl, sh = c_ref[...], sl_ref[...], sh_ref[...]
        for h in range(N_Q_HEADS):
            cols = slice(h * HEAD_DIM, (h + 1) * HEAD_DIM)
            qr_ref[:, cols] = _rope(q_ref[:, cols].astype(F32), c, sl, sh).astype(BF16)
        rows = pl.ds(pl.multiple_of(CHUNK + i * CHUNK, CHUNK), CHUNK)
        for g in range(N_KV_HEADS):
            cols = slice(g * HEAD_DIM, (g + 1) * HEAD_DIM)
            kp_ref[rows, cols] = _rope(kv_ref[:, cols].astype(F32), c, sl, sh).astype(BF16)
        vp_ref[rows, :] = kv_ref[:, KV_WIDTH:2 * KV_WIDTH]

    tab = pl.BlockSpec((CHUNK, HEAD_DIM), lambda i: (i, 0))
    pad = pl.BlockSpec((S + 2 * CHUNK, KV_WIDTH), lambda i: (0, 0))
    return _call("rope_fwd", body, grid=(nb,),
                 in_specs=[pl.BlockSpec((CHUNK, ATTN_WIDTH), lambda i: (i, 0)),
                           pl.BlockSpec((CHUNK, 2 * KV_WIDTH), lambda i: (i, OFF_K // (2 * KV_WIDTH))), tab, tab, tab],
                 out_specs=[pl.BlockSpec((CHUNK, ATTN_WIDTH), lambda i: (i, 0)), pad, pad],
                 out_shape=[jax.ShapeDtypeStruct((S, ATTN_WIDTH), BF16), jax.ShapeDtypeStruct((S + 2 * CHUNK, KV_WIDTH), BF16),
                            jax.ShapeDtypeStruct((S + 2 * CHUNK, KV_WIDTH), BF16)], sem=("arbitrary",))(z, z, *tabs)


def rope_bwd(dq, dkp, dvp, tabs):
    S = dq.shape[0]

    def body(dq_ref, dk_ref, dv_ref, c_ref, sl_ref, sh_ref, o_ref):
        c, sl, sh = c_ref[...], sl_ref[...], sh_ref[...]
        for h in range(N_Q_HEADS):
            cols = slice(h * HEAD_DIM, (h + 1) * HEAD_DIM)
            o_ref[:, cols] = _unrope(dq_ref[:, cols], c, sl, sh).astype(BF16)
        for g in range(N_KV_HEADS):
            cols = slice(g * HEAD_DIM, (g + 1) * HEAD_DIM)
            o_ref[:, OFF_K + g * HEAD_DIM:OFF_K + (g + 1) * HEAD_DIM] = _unrope(dk_ref[:, cols], c, sl, sh).astype(BF16)
        o_ref[:, OFF_V:OFF_V + KV_WIDTH] = dv_ref[...].astype(BF16)

    tab = pl.BlockSpec((CHUNK, HEAD_DIM), lambda i: (i, 0))
    pad = pl.BlockSpec((CHUNK, KV_WIDTH), lambda i: (i + 1, 0))
    return _call("rope_bwd", body, grid=(S // CHUNK,),
                 in_specs=[pl.BlockSpec((CHUNK, ATTN_WIDTH), lambda i: (i, 0)), pad, pad, tab, tab, tab],
                 out_specs=pl.BlockSpec((CHUNK, OFF_CA), lambda i: (i, 0)),
                 out_shape=jax.ShapeDtypeStruct((S, OFF_CA), BF16), sem=("parallel",))(dq, dkp, dvp, *tabs)


STACK = Q_PER_KV * CHUNK


def _stack_heads(ref, rows):
    return jnp.concatenate([ref[rows, r * HEAD_DIM:(r + 1) * HEAD_DIM] for r in range(Q_PER_KV)], axis=0)


def _stack_sinks(s_ref):
    return jnp.concatenate([jnp.broadcast_to(s_ref[r:r + 1, 0:1], (CHUNK, 1)) for r in range(Q_PER_KV)], axis=0)


def _attn_probs(q, kb, sk, n, S):
    scale = 1.0 / math.sqrt(HEAD_DIM)
    s = lax.dot_general(q, kb, NT, preferred_element_type=F32) * scale
    row = lax.broadcasted_iota(jnp.int32, (STACK, 3 * CHUNK), 0) & (CHUNK - 1)
    col = lax.broadcasted_iota(jnp.int32, (STACK, 3 * CHUNK), 1)
    kpos = (n - 1) * CHUNK + col
    valid = (jnp.abs(col - CHUNK - row) <= WINDOW) & (kpos >= 0) & (kpos < S)
    s = jnp.where(valid, s, jnp.finfo(F32).min)
    m = jnp.maximum(jnp.max(s, axis=1, keepdims=True), sk)
    e = jnp.exp(s - m)
    es = jnp.exp(sk - m)
    inv = 1.0 / (jnp.sum(e, axis=1, keepdims=True) + es)
    return e * inv, es * inv


def attn_fwd(qr, kp, vp, sink3):
    S = qr.shape[0]
    tq = min(512, S)
    gw = Q_PER_KV * HEAD_DIM

    def body(q_ref, k_ref, v_ref, s_ref, o_ref):
        i = pl.program_id(1)
        sk = _stack_sinks(s_ref)
        for b in range(tq // CHUNK):
            n = i * (tq // CHUNK) + b
            win = pl.ds(pl.multiple_of(n * CHUNK, CHUNK), 3 * CHUNK)
            kb, vb = k_ref[win, :], v_ref[win, :]
            rows = slice(b * CHUNK, (b + 1) * CHUNK)
            p, _ = _attn_probs(_stack_heads(q_ref, rows), kb, sk, n, S)
            o = jnp.dot(p.astype(BF16), vb, preferred_element_type=F32).astype(BF16)
            for r in range(Q_PER_KV):
                o_ref[rows, r * HEAD_DIM:(r + 1) * HEAD_DIM] = o[r * CHUNK:(r + 1) * CHUNK]

    kv = pl.BlockSpec((S + 2 * CHUNK, HEAD_DIM), lambda g, i: (0, g))
    return _call("attn_fwd", body, grid=(N_KV_HEADS, S // tq),
                 in_specs=[pl.BlockSpec((tq, gw), lambda g, i: (i, g)), kv, kv,
                           pl.BlockSpec((None, Q_PER_KV, LANES), lambda g, i: (g, 0, 0))],
                 out_specs=pl.BlockSpec((tq, gw), lambda g, i: (i, g)),
                 out_shape=jax.ShapeDtypeStruct((S, ATTN_WIDTH), BF16), sem=("parallel", "arbitrary"))(qr, kp, vp, sink3)


def attn_bwd(qr, kp, vp, sink3, dmix):
    S = qr.shape[0]
    tq = min(512, S)
    gw = Q_PER_KV * HEAD_DIM
    scale = 1.0 / math.sqrt(HEAD_DIM)

    def body(q_ref, k_ref, v_ref, s_ref, do_ref, dq_ref, dk_ref, dv_ref, ds_ref):
        i = pl.program_id(1)

        @pl.when(i == 0)
        def _():
            dk_ref[...] = jnp.zeros_like(dk_ref)
            dv_ref[...] = jnp.zeros_like(dv_ref)
            ds_ref[...] = jnp.zeros_like(ds_ref)

        sk = _stack_sinks(s_ref)
        for b in range(tq // CHUNK):
            n = i * (tq // CHUNK) + b
            win = pl.ds(pl.multiple_of(n * CHUNK, CHUNK), 3 * CHUNK)
            kb, vb = k_ref[win, :], v_ref[win, :]
            rows = slice(b * CHUNK, (b + 1) * CHUNK)
            q = _stack_heads(q_ref, rows)
            do = _stack_heads(do_ref, rows)
            p, p_sink = _attn_probs(q, kb, sk, n, S)
            dp = lax.dot_general(do, vb, NT, preferred_element_type=F32)
            delta = jnp.sum(p * dp, axis=1, keepdims=True)
            dsc = (p * (dp - delta) * scale).astype(BF16)
            dq = jnp.dot(dsc, kb, preferred_element_type=F32)
            dsink = -p_sink * delta
            for r in range(Q_PER_KV):
                head = slice(r * CHUNK, (r + 1) * CHUNK)
                dq_ref[rows, r * HEAD_DIM:(r + 1) * HEAD_DIM] = dq[head]
                ds_ref[r:r + 1, :] += jnp.broadcast_to(jnp.sum(dsink[head], axis=0, keepdims=True), (1, LANES))
            dk_ref[win, :] += lax.dot_general(dsc, q, TN, preferred_element_type=F32)
            dv_ref[win, :] += lax.dot_general(p.astype(BF16), do, TN, preferred_element_type=F32)

    kv = pl.BlockSpec((S + 2 * CHUNK, HEAD_DIM), lambda g, i: (0, g))
    qspec = pl.BlockSpec((tq, gw), lambda g, i: (i, g))
    sspec = pl.BlockSpec((None, Q_PER_KV, LANES), lambda g, i: (g, 0, 0))
    padshape = jax.ShapeDtypeStruct((S + 2 * CHUNK, KV_WIDTH), F32)
    return _call("attn_bwd", body, grid=(N_KV_HEADS, S // tq),
                 in_specs=[qspec, kv, kv, sspec, qspec],
                 out_specs=[qspec, kv, kv, sspec],
                 out_shape=[jax.ShapeDtypeStruct((S, ATTN_WIDTH), F32), padshape, padshape,
                            jax.ShapeDtypeStruct((N_KV_HEADS, Q_PER_KV, LANES), F32)],
                 sem=("parallel", "arbitrary"))(qr, kp, vp, sink3, dmix)


CONV_TILE = 256


def _fill_padded(dst_ref, value, S):
    zero = jnp.zeros((CONV_PAD, LANES), F32)
    dst_ref[0:CONV_PAD, :] = zero
    dst_ref[CONV_PAD + S:2 * CONV_PAD + S, :] = zero
    dst_ref[CONV_PAD:CONV_PAD + S, :] = value


def conv_dw_fwd(z, w32, b):
    S = z.shape[0]
    T = min(CONV_TILE, S)
    lo = CONV_PAD - (CONV_KERNEL - 1) // 2

    def body(a_ref, g_ref, w_ref, b_ref, o_ref, c0_ref):
        _fill_padded(c0_ref, a_ref[...].astype(F32) * _sigmoid(g_ref[...].astype(F32)), S)

        def tile(t, carry):
            base = pl.multiple_of(t * T, T)
            acc = jnp.broadcast_to(b_ref[...], (T, LANES))
            for j in range(CONV_KERNEL):
                acc = acc + w_ref[j:j + 1, :] * c0_ref[pl.ds(base + lo + j, T), :]
            o_ref[pl.ds(base, T), :] = acc
            return carry

        lax.fori_loop(0, S // T, tile, 0)

    nca, ncg = OFF_CA // LANES, OFF_CG // LANES
    return _call("conv_dw_fwd", body, grid=(CONV_WIDTH // LANES,),
                 in_specs=[pl.BlockSpec((S, LANES), lambda cb: (0, nca + cb)), pl.BlockSpec((S, LANES), lambda cb: (0, ncg + cb)),
                           pl.BlockSpec((32, LANES), lambda cb: (0, cb)), pl.BlockSpec((1, LANES), lambda cb: (0, cb))],
                 out_specs=pl.BlockSpec((S, LANES), lambda cb: (0, cb)),
                 out_shape=jax.ShapeDtypeStruct((S, CONV_WIDTH), F32),
                 scratch=[pltpu.VMEM((S + 2 * CONV_PAD, LANES), F32)], sem=("parallel",))(z, z, w32, b)


def _ln_stats(x):
    mu = jnp.mean(x, axis=-1, keepdims=True)
    xc = x - mu
    rs = lax.rsqrt(jnp.mean(xc * xc, axis=-1, keepdims=True) + EPS)
    return xc * rs, rs


def _ln_bwd(dy, xh, rs, g):
    dxh = dy * g
    return rs * (dxh - jnp.mean(dxh, axis=-1, keepdims=True) - xh * jnp.mean(dxh * xh, axis=-1, keepdims=True))


def conv_ln_fwd(c1, g, b):
    S = c1.shape[0]
    T = min(512, S)

    def body(x_ref, g_ref, b_ref, o_ref):
        xh, _ = _ln_stats(x_ref[...])
        y = xh * g_ref[...] + b_ref[...]
        o_ref[...] = (y * _sigmoid(y)).astype(BF16)

    row = pl.BlockSpec((T, CONV_WIDTH), lambda i: (i, 0))
    vec = pl.BlockSpec((1, CONV_WIDTH), lambda i: (0, 0))
    return _call("conv_ln_fwd", body, grid=(S // T,), in_specs=[row, vec, vec], out_specs=row,
                 out_shape=jax.ShapeDtypeStruct((S, CONV_WIDTH), BF16), sem=("parallel",))(c1, g, b)


def _acc_out(ref, value):
    @pl.when(pl.program_id(0) == 0)
    def _():
        ref[...] = value

    @pl.when(pl.program_id(0) > 0)
    def _():
        ref[...] += value


def conv_ln_bwd(dmix, c1, g, b):
    S = c1.shape[0]
    T = min(512, S)

    def body(d_ref, x_ref, g_ref, b_ref, dx_ref, dg_ref, db_ref):
        xh, rs = _ln_stats(x_ref[...])
        gv = g_ref[...]
        y = xh * gv + b_ref[...]
        sg = _sigmoid(y)
        dy = d_ref[...].astype(F32) * sg * (1.0 + y * (1.0 - sg))
        dx_ref[...] = _ln_bwd(dy, xh, rs, gv)
        _acc_out(dg_ref, jnp.sum(dy * xh, axis=0, keepdims=True))
        _acc_out(db_ref, jnp.sum(dy, axis=0, keepdims=True))

    row = pl.BlockSpec((T, CONV_WIDTH), lambda i: (i, 0))
    vec = pl.BlockSpec((1, CONV_WIDTH), lambda i: (0, 0))
    vshape = jax.ShapeDtypeStruct((1, CONV_WIDTH), F32)
    return _call("conv_ln_bwd", body, grid=(S // T,),
                 in_specs=[pl.BlockSpec((T, CONV_WIDTH), lambda i: (i, ATTN_WIDTH // CONV_WIDTH)), row, vec, vec],
                 out_specs=[row, vec, vec], out_shape=[jax.ShapeDtypeStruct((S, CONV_WIDTH), F32), vshape, vshape],
                 sem=("arbitrary",))(dmix, c1, g, b)


def conv_dw_bwd(dc1, z, w32):
    S = z.shape[0]
    T = min(CONV_TILE, S)
    half = (CONV_KERNEL - 1) // 2
    lo = CONV_PAD - half

    def body(d_ref, a_ref, g_ref, w_ref, da_ref, dg_ref, dw_ref, db_ref, c0_ref, d1_ref, wacc_ref):
        av = a_ref[...].astype(F32)
        sg = _sigmoid(g_ref[...].astype(F32))
        _fill_padded(c0_ref, av * sg, S)
        _fill_padded(d1_ref, d_ref[...], S)
        wacc_ref[...] = jnp.zeros_like(wacc_ref)

        def tile(t, carry):
            base = pl.multiple_of(t * T, T)
            d1 = d_ref[pl.ds(base, T), :]
            acc = jnp.zeros((T, LANES), F32)
            for j in range(CONV_KERNEL):
                acc = acc + w_ref[j:j + 1, :] * d1_ref[pl.ds(base + CONV_PAD + half - j, T), :]
                prod = d1 * c0_ref[pl.ds(base + lo + j, T), :]
                wacc_ref[j] += jnp.sum(prod.reshape(T // 8, 8, LANES), axis=0)
            rows = pl.ds(base, T)
            a_t = a_ref[rows, :].astype(F32)
            s_t = _sigmoid(g_ref[rows, :].astype(F32))
            da_ref[rows, :] = (acc * s_t).astype(BF16)
            dg_ref[rows, :] = (acc * a_t * s_t * (1.0 - s_t)).astype(BF16)
            return carry

        lax.fori_loop(0, S // T, tile, 0)
        dw_ref[...] = jnp.sum(wacc_ref[...], axis=1)
        db_ref[...] = jnp.sum(d_ref[...], axis=0, keepdims=True)

    nca, ncg = OFF_CA // LANES, OFF_CG // LANES
    col = pl.BlockSpec((S, LANES), lambda cb: (0, cb))
    oshape = jax.ShapeDtypeStruct((S, CONV_WIDTH), BF16)
    return _call("conv_dw_bwd", body, grid=(CONV_WIDTH // LANES,),
                 in_specs=[col, pl.BlockSpec((S, LANES), lambda cb: (0, nca + cb)), pl.BlockSpec((S, LANES), lambda cb: (0, ncg + cb)),
                           pl.BlockSpec((32, LANES), lambda cb: (0, cb))],
                 out_specs=[col, col, pl.BlockSpec((32, LANES), lambda cb: (0, cb)), pl.BlockSpec((1, LANES), lambda cb: (0, cb))],
                 out_shape=[oshape, oshape, jax.ShapeDtypeStruct((32, CONV_WIDTH), F32), jax.ShapeDtypeStruct((1, CONV_WIDTH), F32)],
                 scratch=[pltpu.VMEM((S + 2 * CONV_PAD, LANES), F32), pltpu.VMEM((S + 2 * CONV_PAD, LANES), F32),
                          pltpu.VMEM((32, 8, LANES), F32)], sem=("parallel",))(dc1, z, z, w32)


_INV_SQRT2 = 1.0 / math.sqrt(2.0)
_INV_SQRT2PI = 1.0 / math.sqrt(2.0 * math.pi)


def _gelu(x):
    return 0.5 * x * (1.0 + lax.erf(x * _INV_SQRT2))


def _gelu_grad(x):
    return 0.5 * (1.0 + lax.erf(x * _INV_SQRT2)) + x * jnp.exp(-0.5 * x * x) * _INV_SQRT2PI


def sgu_fwd(z, g, b, ws, bs):
    S = z.shape[0]
    T = min(512, S)

    def body(u_ref, v_ref, g_ref, b_ref, ws_ref, bs_ref, o_ref):
        xh, _ = _ln_stats(_gelu(v_ref[...].astype(F32)))
        vn = (xh * g_ref[...] + b_ref[...]).astype(BF16)
        for ch in range(T // CHUNK):
            rows = slice(ch * CHUNK, (ch + 1) * CHUNK)
            for h in range(SGU_HEADS):
                cols = slice(h * HEAD_DIM, (h + 1) * HEAD_DIM)
                sp = jnp.dot(ws_ref[h], vn[rows, cols], preferred_element_type=F32) + bs_ref[h]
                o_ref[rows, cols] = (_gelu(u_ref[rows, cols].astype(F32)) * sp).astype(BF16)

    vec = pl.BlockSpec((1, SGU_WIDTH), lambda i: (0, 0))
    full = pl.BlockSpec((SGU_HEADS, CHUNK, CHUNK), lambda i: (0, 0, 0))
    return _call("sgu_fwd", body, grid=(S // T,),
                 in_specs=[pl.BlockSpec((T, SGU_WIDTH), lambda i: (i, OFF_U // SGU_WIDTH)),
                           pl.BlockSpec((T, SGU_WIDTH), lambda i: (i, OFF_VV // SGU_WIDTH)), vec, vec, full, full],
                 out_specs=pl.BlockSpec((T, SGU_WIDTH), lambda i: (i, 0)),
                 out_shape=jax.ShapeDtypeStruct((S, SGU_WIDTH), BF16), sem=("parallel",))(z, z, g, b, ws, bs)


def sgu_bwd(z, dmix, g, b, ws, bs):
    S = z.shape[0]
    T = min(512, S)

    def body(u_ref, v_ref, d_ref, g_ref, b_ref, ws_ref, bs_ref, du_ref, dv_ref, dws_ref, dbs_ref, dg_ref, db_ref, dvn_ref):
        @pl.when(pl.program_id(0) == 0)
        def _():
            dws_ref[...] = jnp.zeros_like(dws_ref)
            dbs_ref[...] = jnp.zeros_like(dbs_ref)

        vraw = v_ref[...].astype(F32)
        xh, rs = _ln_stats(_gelu(vraw))
        gv = g_ref[...]
        vn = (xh * gv + b_ref[...]).astype(BF16)
        for ch in range(T // CHUNK):
            rows = slice(ch * CHUNK, (ch + 1) * CHUNK)
            for h in range(SGU_HEADS):
                cols = slice(h * HEAD_DIM, (h + 1) * HEAD_DIM)
                w = ws_ref[h]
                vb = vn[rows, cols]
                sp = jnp.dot(w, vb, preferred_element_type=F32) + bs_ref[h]
                uraw = u_ref[rows, cols].astype(F32)
                dout = d_ref[rows, cols].astype(F32)
                du_ref[rows, cols] = (dout * sp * _gelu_grad(uraw)).astype(BF16)
                dsp = dout * _gelu(uraw)
                dspb = dsp.astype(BF16)
                dvn_ref[rows, cols] = lax.dot_general(w, dspb, TN, preferred_element_type=F32)
                dws_ref[h] += lax.dot_general(dspb, vb, NT, preferred_element_type=F32)
                dbs_ref[h] += jnp.sum(dsp, axis=1, keepdims=True)
        dvn = dvn_ref[...]
        dv_ref[...] = (_ln_bwd(dvn, xh, rs, gv) * _gelu_grad(vraw)).astype(BF16)
        _acc_out(dg_ref, jnp.sum(dvn * xh, axis=0, keepdims=True))
        _acc_out(db_ref, jnp.sum(dvn, axis=0, keepdims=True))

    vec = pl.BlockSpec((1, SGU_WIDTH), lambda i: (0, 0))
    full = pl.BlockSpec((SGU_HEADS, CHUNK, CHUNK), lambda i: (0, 0, 0))
    row = pl.BlockSpec((T, SGU_WIDTH), lambda i: (i, 0))
    oshape = jax.ShapeDtypeStruct((S, SGU_WIDTH), BF16)
    vshape = jax.ShapeDtypeStruct((1, SGU_WIDTH), F32)
    return _call("sgu_bwd", body, grid=(S // T,),
                 in_specs=[pl.BlockSpec((T, SGU_WIDTH), lambda i: (i, OFF_U // SGU_WIDTH)),
                           pl.BlockSpec((T, SGU_WIDTH), lambda i: (i, OFF_VV // SGU_WIDTH)),
                           pl.BlockSpec((T, SGU_WIDTH), lambda i: (i, (ATTN_WIDTH + CONV_WIDTH) // SGU_WIDTH)), vec, vec, full, full],
                 out_specs=[row, row, full, pl.BlockSpec((SGU_HEADS, CHUNK, 1), lambda i: (0, 0, 0)), vec, vec],
                 out_shape=[oshape, oshape, jax.ShapeDtypeStruct((SGU_HEADS, CHUNK, CHUNK), F32),
                            jax.ShapeDtypeStruct((SGU_HEADS, CHUNK, 1), F32), vshape, vshape],
                 scratch=[pltpu.VMEM((T, SGU_WIDTH), F32)], sem=("arbitrary",))(z, z, dmix, g, b, ws, bs)


def _row_tile(rows, cols, n_arrays):
    budget = (24 * 1024 * 1024) // (n_arrays * 2 * 4 * cols)
    t = min(rows, max(16, budget // 16 * 16))
    while rows % t:
        t -= 16
    return t


def add_sibling_half(grad, recv, c_idx):
    J, R, C = grad.shape
    hr = R // 2
    tr = _row_tile(hr, C, 3)
    nb = hr // tr

    def body(c_ref, g_ref, r_ref, o_ref):
        o_ref[...] = (g_ref[...].astype(F32) + r_ref[...].astype(F32)).astype(BF16)

    grid_spec = pltpu.PrefetchScalarGridSpec(
        num_scalar_prefetch=1, grid=(J, nb),
        in_specs=[pl.BlockSpec((None, tr, C), lambda j, i, c: (j, c[0] * nb + i, 0)),
                  pl.BlockSpec((None, tr, C), lambda j, i, c: (j, i, 0))],
        out_specs=pl.BlockSpec((None, tr, C), lambda j, i, c: (j, i, 0)))
    return pl.pallas_call(body, name="add_sibling_half", grid_spec=grid_spec,
                          out_shape=jax.ShapeDtypeStruct((J, hr, C), BF16),
                          compiler_params=pltpu.CompilerParams(vmem_limit_bytes=VMEM_LIMIT,
                                                               dimension_semantics=("parallel", "parallel")))(c_idx, grad, recv)


def sum_chips(own, others, stack, x_idx, y_idx, layer):
    R, C = own.shape[1:]
    tr = _row_tile(R, C, 4)

    def body(x_ref, y_ref, own_ref, oth_ref, stack_ref, o_ref):
        acc = own_ref[...].astype(F32)
        for j in range(3):
            acc = acc + oth_ref[j].astype(F32)
        o_ref[...] = acc

    grid_spec = pltpu.PrefetchScalarGridSpec(
        num_scalar_prefetch=2, grid=(R // tr,),
        in_specs=[pl.BlockSpec((None, tr, C), lambda i, xr, yr: (2 * xr[0] + yr[0], i, 0)),
                  pl.BlockSpec((3, tr, C), lambda i, xr, yr: (0, i, 0)),
                  pl.BlockSpec(memory_space=pl.ANY)],
        out_specs=pl.BlockSpec((None, tr, C), lambda i, xr, yr: (layer, i, 0)))
    return pl.pallas_call(body, name="sum_chips", grid_spec=grid_spec,
                          out_shape=jax.ShapeDtypeStruct(stack.shape, F32), input_output_aliases={4: 0},
                          compiler_params=pltpu.CompilerParams(vmem_limit_bytes=VMEM_LIMIT,
                                                               dimension_semantics=("parallel",)))(x_idx, y_idx, own, others, stack)


def adamw_halves(w, mine, theirs, m, v, c_idx):
    L, R, C = w.shape
    hr = R // 2
    tr = _row_tile(hr, C, 9)
    nb = hr // tr

    def body(c_ref, w_ref, a_ref, b_ref, m_ref, v_ref, g_ref, d_ref, nm_ref, nv_ref):
        gv = jnp.where(pl.program_id(1) == c_ref[0], a_ref[...], b_ref[...])
        g_ref[...] = gv
        nm = ADAM_B1 * m_ref[...] + (1.0 - ADAM_B1) * gv
        nv = ADAM_B2 * v_ref[...] + (1.0 - ADAM_B2) * (gv * gv)
        m_hat = nm / (1.0 - ADAM_B1 ** ADAM_STEP)
        v_hat = nv / (1.0 - ADAM_B2 ** ADAM_STEP)
        d_ref[...] = -ADAM_LR * (m_hat / (jnp.sqrt(v_hat) + ADAM_EPS) + ADAM_WD * w_ref[...])
        nm_ref[...] = nm
        nv_ref[...] = nv

    full = pl.BlockSpec((None, tr, C), lambda l, h, i, c: (l, h * nb + i, 0))
    a_spec = pl.BlockSpec((None, tr, C), lambda l, h, i, c: (l, jnp.where(h == c[0], i, 0), 0))
    b_spec = pl.BlockSpec((None, tr, C), lambda l, h, i, c: (l, jnp.where(h == c[0], 0, i), 0))
    grid_spec = pltpu.PrefetchScalarGridSpec(num_scalar_prefetch=1, grid=(L, 2, nb),
                                             in_specs=[full, a_spec, b_spec, full, full], out_specs=[full] * 4)
    shape = jax.ShapeDtypeStruct((L, R, C), F32)
    return pl.pallas_call(body, name="adamw_halves", grid_spec=grid_spec, out_shape=[shape] * 4,
                          compiler_params=pltpu.CompilerParams(vmem_limit_bytes=VMEM_LIMIT,
                                                               dimension_semantics=("parallel", "arbitrary", "arbitrary")))(
        c_idx, w, mine, theirs, m, v)


def adamw(w, g, m, v):
    R, C = w.shape
    tr = _row_tile(R, C, 7)

    def body(w_ref, g_ref, m_ref, v_ref, d_ref, nm_ref, nv_ref):
        gv = g_ref[...]
        nm = ADAM_B1 * m_ref[...] + (1.0 - ADAM_B1) * gv
        nv = ADAM_B2 * v_ref[...] + (1.0 - ADAM_B2) * (gv * gv)
        m_hat = nm / (1.0 - ADAM_B1 ** ADAM_STEP)
        v_hat = nv / (1.0 - ADAM_B2 ** ADAM_STEP)
        d_ref[...] = -ADAM_LR * (m_hat / (jnp.sqrt(v_hat) + ADAM_EPS) + ADAM_WD * w_ref[...])
        nm_ref[...] = nm
        nv_ref[...] = nv

    spec = pl.BlockSpec((tr, C), lambda i: (i, 0))
    shape = jax.ShapeDtypeStruct((R, C), F32)
    return _call("adamw", body, grid=(R // tr,), in_specs=[spec] * 4, out_specs=[spec] * 3, out_shape=[shape] * 3,
                 sem=("parallel",))(w, g, m, v)


def _place():
    x, y, c = lax.axis_index("x"), lax.axis_index("y"), lax.axis_index("c")
    chips = [(1 - x, y), (x, 1 - y), (1 - x, 1 - y)]
    return x, y, c, chips


def _remote(src, dst, send_sem, recv_sem, dev):
    return pltpu.make_async_remote_copy(src_ref=src, dst_ref=dst, send_sem=send_sem, recv_sem=recv_sem,
                                        device_id=dev, device_id_type=MESH)


EFFECT = pltpu.SideEffectType.DATAFLOW_SIDE_EFFECTING
SEM = pl.BlockSpec(memory_space=pltpu.SEMAPHORE)
ANY = pl.BlockSpec(memory_space=pl.ANY)
TOKEN = jax.ShapeDtypeStruct((8, LANES), F32)


def _in_hbm(a):
    return pltpu.with_memory_space_constraint(a, pltpu.HBM)


def _gather_copies(shards, lands, send_sems, recv_sems):
    x, y, c, chips = _place()
    me = 2 * x + y
    copies = []
    for k in range(len(shards)):
        hr = shards[k].shape[0] // 2
        mine = pl.ds(pl.multiple_of(c * hr, 8), hr)
        for t, (px, py) in enumerate(chips):
            copies.append(_remote(shards[k].at[mine, :], lands[k].at[me, mine, :], send_sems.at[4 * k + t], recv_sems.at[4 * k + t],
                                  (px, py, c)))
        copies.append(_remote(shards[k], lands[k].at[me], send_sems.at[4 * k + 3], recv_sems.at[4 * k + 3], (x, y, 1 - c)))
    return copies


def _gather_landings(lands, send_sems, recv_sems):
    x, y, c, chips = _place()
    me = 2 * x + y
    landings = []
    for k in range(len(lands)):
        hr = lands[k].shape[1] // 2
        mine = pl.ds(pl.multiple_of(c * hr, 8), hr)
        for t, (px, py) in enumerate(chips):
            dst = lands[k].at[2 * px + py, mine, :]
            landings.append(_remote(dst, dst, send_sems.at[4 * k + t], recv_sems.at[4 * k + t], (px, py, c)))
        dst = lands[k].at[me]
        landings.append(_remote(dst, dst, send_sems.at[4 * k + 3], recv_sems.at[4 * k + 3], (x, y, 1 - c)))
    return landings


def gather_start(shards, after):
    n = len(shards)

    def body(*refs):
        srcs, lands_in = refs[:n], refs[n:2 * n]
        send_sems, recv_sems = refs[2 * n + 1], refs[2 * n + 2]
        token = refs[-1]
        for cp in _gather_copies(srcs, lands_in, send_sems, recv_sems):
            cp.start()
        token[...] = jnp.zeros_like(token)

    lands = [lax.empty((N_CHIPS,) + s.shape, s.dtype) for s in shards]
    outs = pl.pallas_call(
        body, name="gather_start", in_specs=[HBM] * (2 * n) + [ANY],
        out_specs=[SEM, SEM] + [HBM] * (2 * n) + [VMEM_SPEC],
        out_shape=[pltpu.SemaphoreType.DMA((4 * n,)), pltpu.SemaphoreType.DMA((4 * n,))]
        + [pltpu.HBM(s.shape, s.dtype) for s in shards] + [pltpu.HBM(l.shape, l.dtype) for l in lands] + [TOKEN],
        input_output_aliases={i: 2 + i for i in range(2 * n)},
        compiler_params=pltpu.CompilerParams(has_side_effects=EFFECT),
    )(*[_in_hbm(s) for s in shards], *[_in_hbm(l) for l in lands], after)
    return outs[0], outs[1], outs[2:2 + n], outs[2 + n:2 + 2 * n], outs[-1]


def gather_wait(send_sems, recv_sems, shards, lands, after):
    n = len(shards)

    def body(*refs):
        srcs, lands_in = refs[:n], refs[n:2 * n]
        send, recv = refs[2 * n], refs[2 * n + 1]
        for cp in _gather_copies(srcs, lands_in, send, recv):
            cp.wait_send()
        for cp in _gather_landings(lands_in, send, recv):
            cp.wait_recv()

    outs = pl.pallas_call(
        body, name="gather_wait", in_specs=[HBM] * (2 * n) + [SEM, SEM, ANY], out_specs=[HBM] * (2 * n),
        out_shape=[pltpu.HBM(s.shape, s.dtype) for s in shards] + [pltpu.HBM(l.shape, l.dtype) for l in lands],
        input_output_aliases={i: i for i in range(2 * n)},
        compiler_params=pltpu.CompilerParams(has_side_effects=EFFECT),
    )(*shards, *lands, send_sems, recv_sems, after)
    return outs[n:]


def forward_halves(lands):
    n = len(lands)

    def body(*refs):
        ins, outs = refs[:n], refs[n:2 * n]
        send_sems, recv_sems = refs[2 * n:]
        x, y, c, chips = _place()
        sibling = (x, y, 1 - c)
        sends = []
        for k in range(n):
            hr = ins[k].shape[1] // 2
            mine = pl.ds(pl.multiple_of(c * hr, 8), hr)
            for t, (px, py) in enumerate(chips):
                cp = _remote(ins[k].at[2 * px + py, mine, :], outs[k].at[2 * px + py, mine, :],
                             send_sems.at[k, t], recv_sems.at[k, t], sibling)
                cp.start()
                sends.append(cp)
        for k in range(n):
            hr = ins[k].shape[1] // 2
            other = pl.ds(pl.multiple_of((1 - c) * hr, 8), hr)
            for t, (px, py) in enumerate(chips):
                dst = outs[k].at[2 * px + py, other, :]
                _remote(dst, dst, send_sems.at[k, t], recv_sems.at[k, t], sibling).wait_recv()
        for cp in sends:
            cp.wait_send()

    return pl.pallas_call(
        body, name="forward_halves", in_specs=[HBM] * n, out_specs=[HBM] * n,
        out_shape=[jax.ShapeDtypeStruct(l.shape, l.dtype) for l in lands],
        input_output_aliases={i: i for i in range(n)},
        scratch_shapes=[pltpu.SemaphoreType.DMA((n, 3)), pltpu.SemaphoreType.DMA((n, 3))],
    )(*lands)


def gather_small(block):
    def body(in_ref, out_ref, send_sems, recv_sems):
        x, y, c, chips = _place()
        me = 2 * x + y
        out_ref[me] = in_ref[...]
        sends = []
        for t, (px, py) in enumerate(chips):
            cp = _remote(in_ref, out_ref.at[me], send_sems.at[t], recv_sems.at[t], (px, py, c))
            cp.start()
            sends.append(cp)
        for t, (px, py) in enumerate(chips):
            landed = out_ref.at[2 * px + py]
            _remote(landed, landed, send_sems.at[t], recv_sems.at[t], (px, py, c)).wait_recv()
        for cp in sends:
            cp.wait_send()

    return pl.pallas_call(
        body, name="gather_small", in_specs=[VMEM_SPEC], out_specs=VMEM_SPEC,
        out_shape=jax.ShapeDtypeStruct((N_CHIPS,) + block.shape, block.dtype),
        scratch_shapes=[pltpu.SemaphoreType.DMA((3,)), pltpu.SemaphoreType.DMA((3,))],
    )(block)


def exchange_sibling_halves(grads):
    n = len(grads)

    def body(*refs):
        ins, outs = refs[:n], refs[n:2 * n]
        send_sems, recv_sems = refs[2 * n:]
        x, y, c, _ = _place()
        copies = []
        for k in range(n):
            hr = ins[k].shape[1] // 2
            theirs = pl.ds(pl.multiple_of((1 - c) * hr, 8), hr)
            cp = _remote(ins[k].at[:, theirs, :], outs[k], send_sems.at[k], recv_sems.at[k], (x, y, 1 - c))
            cp.start()
            copies.append(cp)
        for cp in copies:
            cp.wait()

    return pl.pallas_call(
        body, name="exchange_sibling_halves", in_specs=[HBM] * n, out_specs=[HBM] * n,
        out_shape=[jax.ShapeDtypeStruct((g.shape[0], g.shape[1] // 2, g.shape[2]), g.dtype) for g in grads],
        scratch_shapes=[pltpu.SemaphoreType.DMA((n,)), pltpu.SemaphoreType.DMA((n,))],
    )(*grads)


def _sibling_half_copies(grads, lands, send_sems, recv_sems):
    x, y, c, _ = _place()
    copies = []
    for k in range(len(grads)):
        hr = grads[k].shape[1] // 2
        theirs = pl.ds(pl.multiple_of((1 - c) * hr, 8), hr)
        copies.append(_remote(grads[k].at[:, theirs, :], lands[k], send_sems.at[k], recv_sems.at[k], (x, y, 1 - c)))
    return copies


def _sibling_whole_copies(srcs, lands, send_sems, recv_sems):
    x, y, c, _ = _place()
    return [_remote(srcs[k], lands[k], send_sems.at[k], recv_sems.at[k], (x, y, 1 - c)) for k in range(len(srcs))]


def pair_start(name, make_copies, srcs, land_shapes):
    n = len(srcs)

    def body(*refs):
        src_refs, land_refs = refs[:n], refs[n:2 * n]
        send_sems, recv_sems = refs[2 * n], refs[2 * n + 1]
        token = refs[-1]
        for cp in make_copies(src_refs, land_refs, send_sems, recv_sems):
            cp.start()
        token[...] = jnp.zeros_like(token)

    lands = [lax.empty(shape, s.dtype) for shape, s in zip(land_shapes, srcs)]
    outs = pl.pallas_call(
        body, name=name, in_specs=[HBM] * (2 * n), out_specs=[SEM, SEM] + [HBM] * (2 * n) + [VMEM_SPEC],
        out_shape=[pltpu.SemaphoreType.DMA((n,)), pltpu.SemaphoreType.DMA((n,))]
        + [pltpu.HBM(s.shape, s.dtype) for s in srcs] + [pltpu.HBM(l.shape, l.dtype) for l in lands] + [TOKEN],
        input_output_aliases={i: 2 + i for i in range(2 * n)},
        compiler_params=pltpu.CompilerParams(has_side_effects=EFFECT),
    )(*[_in_hbm(s) for s in srcs], *[_in_hbm(l) for l in lands])
    return outs[0], outs[1], outs[2:2 + n], outs[2 + n:2 + 2 * n], outs[-1]


def pair_wait_one(name, send_sems, recv_sems, src, land, after, index):
    def body(src_ref, land_ref, send, recv, after_ref, src_out, land_out):
        x, y, c, _ = _place()
        cp = _remote(src_ref, land_ref, send.at[index], recv.at[index], (x, y, 1 - c))
        cp.wait_send()
        cp.wait_recv()

    return pl.pallas_call(
        body, name=name, in_specs=[HBM, HBM, SEM, SEM, ANY], out_specs=[HBM, HBM],
        out_shape=[pltpu.HBM(src.shape, src.dtype), pltpu.HBM(land.shape, land.dtype)],
        input_output_aliases={0: 0, 1: 1},
        compiler_params=pltpu.CompilerParams(has_side_effects=EFFECT),
    )(src, land, send_sems, recv_sems, after)


def pair_wait(name, make_copies, send_sems, recv_sems, srcs, lands, after):
    n = len(srcs)

    def body(*refs):
        src_refs, land_refs = refs[:n], refs[n:2 * n]
        for cp in make_copies(src_refs, land_refs, refs[2 * n], refs[2 * n + 1]):
            cp.wait_send()
            cp.wait_recv()

    outs = pl.pallas_call(
        body, name=name, in_specs=[HBM] * (2 * n) + [SEM, SEM, ANY], out_specs=[HBM] * (2 * n),
        out_shape=[pltpu.HBM(s.shape, s.dtype) for s in srcs] + [pltpu.HBM(l.shape, l.dtype) for l in lands],
        input_output_aliases={i: i for i in range(2 * n)},
        compiler_params=pltpu.CompilerParams(has_side_effects=EFFECT),
    )(*srcs, *lands, send_sems, recv_sems, after)
    return outs[:n], outs[n:]


def _chip_copies(parts, lands, send_sems, recv_sems):
    x, y, c, chips = _place()
    return [_remote(parts[k].at[2 * px + py], lands[k].at[t], send_sems.at[3 * k + t], recv_sems.at[3 * k + t], (px, py, c))
            for k in range(len(parts)) for t, (px, py) in enumerate(chips)]


def chip_parts_start(parts):
    n = len(parts)

    def body(*refs):
        srcs, lands_in = refs[:n], refs[n:2 * n]
        send_sems, recv_sems = refs[2 * n], refs[2 * n + 1]
        token = refs[-1]
        for cp in _chip_copies(srcs, lands_in, send_sems, recv_sems):
            cp.start()
        token[...] = jnp.zeros_like(token)

    lands = [lax.empty((3,) + p.shape[1:], p.dtype) for p in parts]
    outs = pl.pallas_call(
        body, name="chip_parts_start", in_specs=[HBM] * (2 * n), out_specs=[SEM, SEM] + [HBM] * (2 * n) + [VMEM_SPEC],
        out_shape=[pltpu.SemaphoreType.DMA((3 * n,)), pltpu.SemaphoreType.DMA((3 * n,))]
        + [pltpu.HBM(p.shape, p.dtype) for p in parts] + [pltpu.HBM(l.shape, l.dtype) for l in lands] + [TOKEN],
        input_output_aliases={i: 2 + i for i in range(2 * n)},
        compiler_params=pltpu.CompilerParams(has_side_effects=EFFECT),
    )(*[_in_hbm(p) for p in parts], *[_in_hbm(l) for l in lands])
    return outs[0], outs[1], outs[2:2 + n], outs[2 + n:2 + 2 * n], outs[-1]


def chip_parts_wait(send_sems, recv_sems, parts, lands, after):
    n = len(parts)

    def body(*refs):
        srcs, lands_in = refs[:n], refs[n:2 * n]
        send, recv = refs[2 * n], refs[2 * n + 1]
        for cp in _chip_copies(srcs, lands_in, send, recv):
            cp.wait_send()
            cp.wait_recv()

    outs = pl.pallas_call(
        body, name="chip_parts_wait", in_specs=[HBM] * (2 * n) + [SEM, SEM, ANY], out_specs=[HBM] * (2 * n),
        out_shape=[pltpu.HBM(p.shape, p.dtype) for p in parts] + [pltpu.HBM(l.shape, l.dtype) for l in lands],
        input_output_aliases={i: i for i in range(2 * n)},
        compiler_params=pltpu.CompilerParams(has_side_effects=EFFECT),
    )(*parts, *lands, send_sems, recv_sems, after)
    return outs[:n], outs[n:]


def allreduce_small(packed):
    R = packed.shape[0]

    def body(x_ref, sum_ref, all_ref, send_sems, recv_sems):
        x, y, c, chips = _place()
        me, sibling = (x, y, c), (x, y, 1 - c)

        def rows(px, py, pc):
            return all_ref.at[4 * px + 2 * py + pc]

        def copy(k, block, to, src=None):
            return _remote(rows(*block) if src is None else src, rows(*block), send_sems.at[k], recv_sems.at[k], to)

        all_ref[4 * x + 2 * y + c] = x_ref[...]
        first = [copy(0, me, sibling, src=x_ref)]
        first += [copy(1 + j, me, (*chip, c), src=x_ref) for j, chip in enumerate(chips)]
        for cp in first:
            cp.start()
        passed = [copy(4 + j, (*chip, c), sibling) for j, chip in enumerate(chips)]
        for j, chip in enumerate(chips):
            copy(1 + j, (*chip, c), me).wait_recv()
            passed[j].start()
        copy(0, sibling, me).wait_recv()
        for j, chip in enumerate(chips):
            copy(4 + j, (*chip, 1 - c), me).wait_recv()
        for cp in first + passed:
            cp.wait_send()

        def chunk(i, carry):
            rws = pl.ds(pl.multiple_of(i * PACK_ROWS, PACK_ROWS), PACK_ROWS)
            acc = all_ref[0, rws, :]
            for d in range(1, N_DEV):
                acc = acc + all_ref[d, rws, :]
            sum_ref[rws, :] = acc
            return carry

        lax.fori_loop(0, R // PACK_ROWS, chunk, 0)

    return pl.pallas_call(
        body, name="allreduce_small", in_specs=[VMEM_SPEC], out_specs=VMEM_SPEC,
        out_shape=jax.ShapeDtypeStruct((R, LANES), F32),
        scratch_shapes=[pltpu.VMEM((N_DEV, R, LANES), F32), pltpu.SemaphoreType.DMA((7,)), pltpu.SemaphoreType.DMA((7,))],
        compiler_params=pltpu.CompilerParams(vmem_limit_bytes=VMEM_LIMIT),
    )(packed)


def _mixer_fwd(x, p, tabs, token):
    h = rms_fwd(x, p["mix_norm_g"], token)
    z = mm_nn_cols(h, p["w_in"])
    qr, kp, vp = rope_fwd(z, tabs)
    attn = attn_fwd(qr, kp, vp, p["sink3"])
    c1 = conv_dw_fwd(z, p["conv_w32"], p["conv_dw_b"])
    conv = conv_ln_fwd(c1, p["conv_ln_g"], p["conv_ln_b"])
    sgu = sgu_fwd(z, p["sgu_ln_g"], p["sgu_ln_b"], p["sgu_w16"], p["sgu_b3"])
    mix = jnp.concatenate([attn, conv, sgu], axis=1)
    x_mid = mm_nn_rows_res(mix, p["w_out"], x)
    return x_mid, dict(x=x, h=h, z=z, qr=qr, kp=kp, vp=vp, c1=c1, mix=mix, x_mid=x_mid)


def _ffn_fwd(x_mid, p, token):
    h2 = rms_fwd(x_mid, p["ffn_norm_g"], token)
    gate, up, act = ffn_up(h2, p["w_gate"], p["w_up"])
    x_out = mm_nn_rows_res(act, p["w_down"], x_mid)
    return x_out, dict(h2=h2, gate=gate, up=up, act=act)


def _layer_fwd(x, p, tabs, token):
    x_mid, s_mix = _mixer_fwd(x, p, tabs, token)
    x_out, s_ffn = _ffn_fwd(x_mid, p, token)
    return x_out, {**s_mix, **s_ffn}


def _ffn_bwd(dx, dxb, p, s, token):
    dgate, dup = ffn_down_bwd(dxb, p["w_down"], s["gate"], s["up"], token)
    g_down = mm_tn_rows(s["act"], dxb)
    dh2 = mm_nt_cols([(dgate, p["w_gate"]), (dup, p["w_up"])], F32)
    g_gate = mm_tn_cols(s["h2"], dgate, N_CHIPS)
    g_up = mm_tn_cols(s["h2"], dup, N_CHIPS)
    dmid, dmidb, g_ffn_norm = rms_bwd(s["x_mid"], p["ffn_norm_g"], dh2, dx)
    return dmid, dmidb, [g_gate, g_up, g_down.reshape(N_CHIPS, -1, D_MODEL)], g_ffn_norm


def _mixer_bwd(dmid, dmidb, p, s, tabs, token):
    dmix = mm_nt_rows(dmidb, p["w_out"], token)
    g_out = mm_tn_rows(s["mix"], dmidb)
    dq, dkp, dvp, dsink = attn_bwd(s["qr"], s["kp"], s["vp"], p["sink3"], dmix)
    dqkv = rope_bwd(dq, dkp, dvp, tabs)
    dc1, g_cln_g, g_cln_b = conv_ln_bwd(dmix, s["c1"], p["conv_ln_g"], p["conv_ln_b"])
    dca, dcg, g_cw, g_cb = conv_dw_bwd(dc1, s["z"], p["conv_w32"])
    du, dv, g_sw, g_sb, g_sln_g, g_sln_b = sgu_bwd(s["z"], dmix, p["sgu_ln_g"], p["sgu_ln_b"], p["sgu_w16"], p["sgu_b3"])
    dz = jnp.concatenate([dqkv, dca, dcg, du, dv], axis=1)
    dh = mm_nt_cols([(dz, p["w_in"])], F32)
    g_in = mm_tn_cols(s["h"], dz, N_CHIPS)
    dx_in, dxb_in, g_mix_norm = rms_bwd(s["x"], p["mix_norm_g"], dh, dmid)
    small = dict(mix_norm_g=g_mix_norm, sink=dsink[:, :, 0].reshape(1, N_Q_HEADS), conv_dw_w=g_cw[:CONV_KERNEL],
                 conv_dw_b=g_cb, conv_ln_g=g_cln_g, conv_ln_b=g_cln_b, sgu_ln_g=g_sln_g, sgu_ln_b=g_sln_b,
                 sgu_w=g_sw, sgu_b=g_sb[:, :, 0])
    return dx_in, dxb_in, [g_in, g_out.reshape(N_CHIPS, -1, D_MODEL)], small


def _layer_bwd(dx, dxb, p, s, tabs, token):
    dmid, dmidb, ffn_big, g_ffn_norm = _ffn_bwd(dx, dxb, p, s, token)
    dx_in, dxb_in, mix_big, small = _mixer_bwd(dmid, dmidb, p, s, tabs, token)
    return dx_in, dxb_in, mix_big + ffn_big, dict(small, ffn_norm_g=g_ffn_norm)


def _mixer_weights(gathered):
    w_in, w_out = gathered
    return dict(w_in=w_in, w_out=w_out.reshape(-1, D_MODEL))


def _ffn_weights(gathered):
    w_gate, w_up, w_down = gathered
    return dict(w_gate=w_gate, w_up=w_up, w_down=w_down.reshape(-1, D_MODEL))


def _small_params(l, conv_w_full, mix_norm_g, sink, conv_dw_b, conv_ln_g, conv_ln_b, sgu_ln_g, sgu_ln_b, sgu_w, sgu_b,
                  ffn_norm_g):
    return dict(
        mix_norm_g=mix_norm_g[l:l + 1], ffn_norm_g=ffn_norm_g[l:l + 1],
        sink3=jnp.broadcast_to(sink[l].reshape(N_KV_HEADS, Q_PER_KV, 1), (N_KV_HEADS, Q_PER_KV, LANES)),
        conv_w32=jnp.pad(conv_w_full[l], ((0, 32 - CONV_KERNEL), (0, 0))),
        conv_dw_b=conv_dw_b[l:l + 1], conv_ln_g=conv_ln_g[l:l + 1], conv_ln_b=conv_ln_b[l:l + 1],
        sgu_ln_g=sgu_ln_g[l:l + 1], sgu_ln_b=sgu_ln_b[l:l + 1], sgu_w16=sgu_w[l].astype(BF16),
        sgu_b3=jnp.broadcast_to(sgu_b[l][:, :, None], (SGU_HEADS, CHUNK, CHUNK)))


_SMALL = ["mix_norm_g", "sink", "conv_dw_b", "conv_ln_g", "conv_ln_b", "sgu_ln_g", "sgu_ln_b", "sgu_w", "sgu_b", "ffn_norm_g",
          "final_norm_g"]


def _pack_rows(arrays):
    rows, counts = [], []
    for a in arrays:
        flat = a.reshape(-1)
        n = -(-flat.shape[0] // LANES)
        rows.append(jnp.pad(flat, (0, n * LANES - flat.shape[0])).reshape(n, LANES))
        counts.append(n)
    packed = jnp.concatenate(rows, axis=0)
    pad = -packed.shape[0] % PACK_ROWS
    return jnp.pad(packed, ((0, pad), (0, 0))), counts


def _unpack_rows(packed, counts, shapes):
    out, r = [], 0
    for n, shape in zip(counts, shapes):
        size = math.prod(shape)
        out.append(packed[r:r + n].reshape(-1)[:size].reshape(shape))
        r += n
    return out


def kernel(x, mix_norm_g, w_in, sink, conv_dw_w, conv_dw_b, conv_ln_g, conv_ln_b, sgu_ln_g, sgu_ln_b, sgu_w, sgu_b, w_out, ffn_norm_g, w_gate, w_up, w_down, final_norm_g, loss_target, m_mix_norm_g, m_w_in, m_sink, m_conv_dw_w, m_conv_dw_b, m_conv_ln_g, m_conv_ln_b, m_sgu_ln_g, m_sgu_ln_b, m_sgu_w, m_sgu_b, m_w_out, m_ffn_norm_g, m_w_gate, m_w_up, m_w_down, m_final_norm_g, v_mix_norm_g, v_w_in, v_sink, v_conv_dw_w, v_conv_dw_b, v_conv_ln_g, v_conv_ln_b, v_sgu_ln_g, v_sgu_ln_b, v_sgu_w, v_sgu_b, v_w_out, v_ffn_norm_g, v_w_gate, v_w_up, v_w_down, v_final_norm_g):
    S = x.shape[1]
    my_chip = 2 * lax.axis_index("x") + lax.axis_index("y")
    c_idx = lax.axis_index("c").astype(jnp.int32).reshape(1)
    big_w = [w_in, w_out, w_gate, w_up, w_down]
    big_m = [m_w_in, m_w_out, m_w_gate, m_w_up, m_w_down]
    big_v = [v_w_in, v_w_out, v_w_gate, v_w_up, v_w_down]
    n_kinds = len(big_w)

    x_idx = lax.axis_index("x").astype(jnp.int32).reshape(1)
    y_idx = lax.axis_index("y").astype(jnp.int32).reshape(1)
    conv_w_all = gather_small(conv_dw_w)
    conv_w_full = jnp.transpose(conv_w_all, (1, 2, 0, 3)).reshape(DEPTH, CONV_KERNEL, CONV_WIDTH)
    tabs = rope_tables(S)
    no_token = jnp.zeros(TOKEN.shape, TOKEN.dtype)

    mixer_kinds, ffn_kinds = [0, 1], [2, 3, 4]
    shards = [[w[l].astype(BF16) for w in big_w] for l in range(DEPTH)]

    def fetch(pending, after):
        send_sems, recv_sems, srcs, lands, _ = pending
        return forward_halves(gather_wait(send_sems, recv_sems, srcs, lands, after))

    first_mixer = gather_start([shards[0][k] for k in mixer_kinds], conv_w_all)
    first_ffn = gather_start([shards[0][k] for k in ffn_kinds], first_mixer[4])
    act = x[0]
    saved, params = [], []
    for l in range(DEPTH):
        p = _small_params(l, conv_w_full, mix_norm_g, sink, conv_dw_b, conv_ln_g, conv_ln_b, sgu_ln_g, sgu_ln_b, sgu_w, sgu_b,
                          ffn_norm_g)
        if l == 0:
            p.update(_mixer_weights(fetch(first_mixer, act)))
            x_mid, s_mix = _mixer_fwd(act, p, tabs, first_ffn[4])
            ffn_w = fetch(first_ffn, x_mid)
        else:
            gathered = fetch(pending, act)
            p.update(_mixer_weights(gathered[:2]))
            ffn_w = gathered[2:]
        token = no_token
        if l + 1 < DEPTH:
            pending = gather_start(shards[l + 1], ffn_w[0])
            token = pending[4]
        p.update(_ffn_weights(ffn_w))
        if l > 0:
            x_mid, s_mix = _mixer_fwd(act, p, tabs, token)
        act, s_ffn = _ffn_fwd(x_mid, p, token)
        params.append(p)
        saved.append({**s_mix, **s_ffn})
    loss_part, dx, dxb, g_final = final_loss(act, final_norm_g.reshape(1, D_MODEL), loss_target[0])
    loss = lax.psum(loss_part[0, 0], ("x", "y", "c"))

    halves = [jnp.zeros((DEPTH, w.shape[1] // 2, w.shape[2]), F32) for w in big_w]
    small_grads = [None] * DEPTH

    def chip_start(layer, kinds, grads, recv):
        chip_sum = [add_sibling_half(g, r, c_idx) for g, r in zip(grads, recv)]
        send_sems, recv_sems, parts, lands, token = chip_parts_start(chip_sum)
        return (layer, kinds, send_sems, recv_sems, parts, lands), token

    def reduce_start(layer, kinds, grads):
        return chip_start(layer, kinds, grads, exchange_sibling_halves(grads))

    def reduce_finish(pending, halves, after):
        layer, kinds, send_sems, recv_sems, parts, lands = pending
        parts, others = chip_parts_wait(send_sems, recv_sems, parts, lands, after)
        halves = list(halves)
        for i, k in enumerate(kinds):
            halves[k] = sum_chips(parts[i], others[i], halves[k], x_idx, y_idx, layer)
        return halves

    pending, token = None, no_token
    for l in reversed(range(DEPTH)):
        dmid, dmidb, ffn_big, g_ffn_norm = _ffn_bwd(dx, dxb, params[l], saved[l], token)
        if l == 0:
            last_ffn, mixer_token = reduce_start(l, ffn_kinds, ffn_big)
        else:
            half_shapes = [(g.shape[0], g.shape[1] // 2, g.shape[2]) for g in ffn_big]
            sib_send, sib_recv, ffn_big, ffn_lands, mixer_token = pair_start("sibling_start", _sibling_half_copies, ffn_big, half_shapes)
        dx, dxb, mix_big, small = _mixer_bwd(dmid, dmidb, params[l], saved[l], tabs, mixer_token)
        small_grads[l] = dict(small, ffn_norm_g=g_ffn_norm)
        if pending is not None:
            halves = reduce_finish(pending, halves, dx)
        if l == 0:
            last_mixer, token = reduce_start(l, mixer_kinds, mix_big)
            halves = reduce_finish(last_ffn, halves, token)
            halves = reduce_finish(last_mixer, halves, halves[ffn_kinds[0]])
        else:
            ffn_big, ffn_recv = pair_wait("sibling_wait", _sibling_half_copies, sib_send, sib_recv, ffn_big, ffn_lands, dx)
            mix_recv = exchange_sibling_halves(mix_big)
            pending, token = chip_start(l, mixer_kinds + ffn_kinds, list(mix_big) + list(ffn_big), list(mix_recv) + list(ffn_recv))

    fin_send, fin_recv, halves, fin_lands, _ = pair_start("final_start", _sibling_whole_copies, halves, [h.shape for h in halves])

    stacked = {n: jnp.stack([small_grads[l][n] for l in range(DEPTH)]) for n in small_grads[0]}
    stacked["final_norm_g"] = g_final
    packed, counts = _pack_rows([stacked[n] for n in _SMALL] + [stacked["conv_dw_w"]])
    reduced = allreduce_small(packed)
    small_w = dict(mix_norm_g=mix_norm_g, sink=sink, conv_dw_b=conv_dw_b, conv_ln_g=conv_ln_g, conv_ln_b=conv_ln_b,
                   sgu_ln_g=sgu_ln_g, sgu_ln_b=sgu_ln_b, sgu_w=sgu_w, sgu_b=sgu_b, ffn_norm_g=ffn_norm_g,
                   final_norm_g=final_norm_g)
    small_m = dict(mix_norm_g=m_mix_norm_g, sink=m_sink, conv_dw_b=m_conv_dw_b, conv_ln_g=m_conv_ln_g,
                   conv_ln_b=m_conv_ln_b, sgu_ln_g=m_sgu_ln_g, sgu_ln_b=m_sgu_ln_b, sgu_w=m_sgu_w, sgu_b=m_sgu_b,
                   ffn_norm_g=m_ffn_norm_g, final_norm_g=m_final_norm_g)
    small_v = dict(mix_norm_g=v_mix_norm_g, sink=v_sink, conv_dw_b=v_conv_dw_b, conv_ln_g=v_conv_ln_g,
                   conv_ln_b=v_conv_ln_b, sgu_ln_g=v_sgu_ln_g, sgu_ln_b=v_sgu_ln_b, sgu_w=v_sgu_w, sgu_b=v_sgu_b,
                   ffn_norm_g=v_ffn_norm_g, final_norm_g=v_final_norm_g)
    shapes = [small_w[n].shape for n in _SMALL] + [(DEPTH, CONV_KERNEL, CONV_WIDTH)]
    red = _unpack_rows(reduced, counts, shapes)
    g_small = dict(zip(_SMALL, red[:-1]))
    g_small["conv_dw_w"] = lax.dynamic_slice_in_dim(red[-1], my_chip * LANES, LANES, axis=2)
    small_w["conv_dw_w"], small_m["conv_dw_w"], small_v["conv_dw_w"] = conv_dw_w, m_conv_dw_w, v_conv_dw_w
    names = _SMALL + ["conv_dw_w"]
    pw, cnt = _pack_rows([small_w[n] for n in names])
    pg, _ = _pack_rows([g_small[n] for n in names])
    pm, _ = _pack_rows([small_m[n] for n in names])
    pv, _ = _pack_rows([small_v[n] for n in names])
    sd, sm, sv = adamw(pw, pg, pm, pv)
    shp = [small_w[n].shape for n in names]
    d_small = dict(zip(names, _unpack_rows(sd, cnt, shp)))
    m_small = dict(zip(names, _unpack_rows(sm, cnt, shp)))
    v_small = dict(zip(names, _unpack_rows(sv, cnt, shp)))

    big_names = ["w_in", "w_out", "w_gate", "w_up", "w_down"]
    g_big, d_big, m_big, v_big = {}, {}, {}, {}
    after = sd
    for k, n in enumerate(big_names):
        mine, theirs = pair_wait_one("final_wait", fin_send, fin_recv, halves[k], fin_lands[k], after, k)
        g_big[n], d_big[n], m_big[n], v_big[n] = adamw_halves(big_w[k], mine, theirs, big_m[k], big_v[k], c_idx)
        after = d_big[n]

    order = ["mix_norm_g", "w_in", "sink", "conv_dw_w", "conv_dw_b", "conv_ln_g", "conv_ln_b", "sgu_ln_g", "sgu_ln_b",
             "sgu_w", "sgu_b", "w_out", "ffn_norm_g", "w_gate", "w_up", "w_down", "final_norm_g"]
    grads = {**g_small, **g_big}
    deltas = {**d_small, **d_big}
    new_m = {**m_small, **m_big}
    new_v = {**v_small, **v_big}
    return (loss, dx[None], *[grads[n] for n in order], *[deltas[n] for n in order],
            *[new_m[n] for n in order], *[new_v[n] for n in order])
```

```python
import functools
import math

import jax
import jax.numpy as jnp
from jax import lax
from jax.experimental import pallas as pl
from jax.experimental.pallas import tpu as pltpu

F32, BF16 = jnp.float32, jnp.bfloat16

D_MODEL = 2048
DEPTH = 4
HEAD_DIM = 128
N_Q_HEADS = 8
N_KV_HEADS = 2
Q_PER_KV = N_Q_HEADS // N_KV_HEADS
ATTN_WIDTH = N_Q_HEADS * HEAD_DIM
KV_WIDTH = N_KV_HEADS * HEAD_DIM
CONV_WIDTH = 512
CONV_KERNEL = 31
CONV_PAD = 16
SGU_WIDTH = 512
SGU_HEADS = 4
CHUNK = 128
IN_WIDTH = 3584
D_FF = 5632
WINDOW = 128
ROT_DIM = 32
ROPE_THETA = 500000.0
EPS = 1e-6
N_CHIPS = 4
N_DEV = 8
LANES = 128
PACK_ROWS = 64
OFF_K = ATTN_WIDTH
OFF_V = OFF_K + KV_WIDTH
OFF_CA = OFF_V + KV_WIDTH
OFF_CG = OFF_CA + CONV_WIDTH
OFF_U = OFF_CG + CONV_WIDTH
OFF_VV = OFF_U + SGU_WIDTH

ADAM_LR, ADAM_B1, ADAM_B2, ADAM_EPS, ADAM_WD, ADAM_STEP = 0.001, 0.9, 0.999, 1e-08, 0.01, 10

VMEM_LIMIT = 56 * 1024 * 1024
MESH = pl.DeviceIdType.MESH
HBM = pl.BlockSpec(memory_space=pltpu.HBM)
VMEM_SPEC = pl.BlockSpec(memory_space=pltpu.VMEM)


def _call(name, body, *, grid, in_specs, out_specs, out_shape, scratch=(), sem=None):
    params = dict(vmem_limit_bytes=VMEM_LIMIT)
    if sem is not None:
        params["dimension_semantics"] = sem
    return pl.pallas_call(
        body, name=name, grid=grid, in_specs=in_specs, out_specs=out_specs, out_shape=out_shape,
        scratch_shapes=list(scratch), compiler_params=pltpu.CompilerParams(**params))


def _sigmoid(x):
    return 1.0 / (1.0 + jnp.exp(-x))


def rms_fwd(x, g, token):
    S = x.shape[0]
    tm = min(512, S)

    def body(x_ref, g_ref, token_ref, o_ref):
        xv = x_ref[...]
        r = lax.rsqrt(jnp.mean(xv * xv, axis=-1, keepdims=True) + EPS)
        o_ref[...] = (xv * r * g_ref[...]).astype(BF16)

    return _call("rms_fwd", body, grid=(S // tm,),
                 in_specs=[pl.BlockSpec((tm, D_MODEL), lambda i: (i, 0)), pl.BlockSpec((1, D_MODEL), lambda i: (0, 0)),
                           pl.BlockSpec((8, LANES), lambda i: (0, 0))],
                 out_specs=pl.BlockSpec((tm, D_MODEL), lambda i: (i, 0)),
                 out_shape=jax.ShapeDtypeStruct((S, D_MODEL), BF16), sem=("parallel",))(x, g, token)


def _rms_bwd_math(xv, gv, dh):
    r = lax.rsqrt(jnp.mean(xv * xv, axis=-1, keepdims=True) + EPS)
    n = xv * r
    dn = dh * gv
    dx = r * (dn - n * jnp.mean(dn * n, axis=-1, keepdims=True))
    dg = jnp.sum(dh * n, axis=0, keepdims=True)
    return dx, dg


def rms_bwd(x, g, dh, dres):
    S = x.shape[0]
    tm = min(512, S)

    def body(x_ref, g_ref, dh_ref, dres_ref, dx_ref, dxb_ref, dg_ref):
        dx, dg = _rms_bwd_math(x_ref[...], g_ref[...], dh_ref[...].astype(F32))
        dx = dx + dres_ref[...]
        dx_ref[...] = dx
        dxb_ref[...] = dx.astype(BF16)

        @pl.when(pl.program_id(0) == 0)
        def _():
            dg_ref[...] = dg

        @pl.when(pl.program_id(0) > 0)
        def _():
            dg_ref[...] += dg

    row = pl.BlockSpec((tm, D_MODEL), lambda i: (i, 0))
    vec = pl.BlockSpec((1, D_MODEL), lambda i: (0, 0))
    return _call("rms_bwd", body, grid=(S // tm,), in_specs=[row, vec, row, row], out_specs=[row, row, vec],
                 out_shape=[jax.ShapeDtypeStruct((S, D_MODEL), F32), jax.ShapeDtypeStruct((S, D_MODEL), BF16),
                            jax.ShapeDtypeStruct((1, D_MODEL), F32)], sem=("arbitrary",))(x, g, dh, dres)


def final_loss(x, g, target):
    S = x.shape[0]
    tm = min(256, S)

    def body(x_ref, g_ref, t_ref, loss_ref, dx_ref, dxb_ref, dg_ref):
        xv = x_ref[...]
        gv = g_ref[...]
        r = lax.rsqrt(jnp.mean(xv * xv, axis=-1, keepdims=True) + EPS)
        err = xv * r * gv - t_ref[...]
        part = 0.5 * jnp.sum(jnp.mean(err * err, axis=-1, keepdims=True), axis=0, keepdims=True)
        dx, dg = _rms_bwd_math(xv, gv, err * (1.0 / D_MODEL))
        dx_ref[...] = dx
        dxb_ref[...] = dx.astype(BF16)

        @pl.when(pl.program_id(0) == 0)
        def _():
            dg_ref[...] = dg
            loss_ref[...] = part

        @pl.when(pl.program_id(0) > 0)
        def _():
            dg_ref[...] += dg
            loss_ref[...] += part

    row = pl.BlockSpec((tm, D_MODEL), lambda i: (i, 0))
    vec = pl.BlockSpec((1, D_MODEL), lambda i: (0, 0))
    one = pl.BlockSpec((1, 1), lambda i: (0, 0))
    return _call("final_loss", body, grid=(S // tm,), in_specs=[row, vec, row], out_specs=[one, row, row, vec],
                 out_shape=[jax.ShapeDtypeStruct((1, 1), F32), jax.ShapeDtypeStruct((S, D_MODEL), F32),
                            jax.ShapeDtypeStruct((S, D_MODEL), BF16), jax.ShapeDtypeStruct((1, D_MODEL), F32)],
                 sem=("arbitrary",))(x, g, target)


EPILOGUE_ROWS = 256
NN = (((1,), (0,)), ((), ()))
NT = (((1,), (1,)), ((), ()))
TN = (((0,), (0,)), ((), ()))


def _matmul(name, operands, in_specs, out_shape, out_specs, grid, pairs, dims, acc_shape, epilogue):
    n_in, n_out, nk = len(operands), len(out_shape), grid[-1]

    def body(*refs):
        ins, outs = refs[:n_in], refs[n_in:n_in + n_out]
        part = None
        for ia, ib in pairs:
            d = lax.dot_general(ins[ia][...], ins[ib][...], dims, preferred_element_type=F32)
            part = d if part is None else part + d
        if nk == 1:
            epilogue(part, ins, outs)
        else:
            acc = refs[-1]
            k = pl.program_id(len(grid) - 1)

            @pl.when(k == 0)
            def _():
                acc[...] = part

            @pl.when(k > 0)
            def _():
                acc[...] += part

            @pl.when(k == nk - 1)
            def _():
                epilogue(acc[...], ins, outs)

    scratch = [pltpu.VMEM(acc_shape, F32)] if nk > 1 else []
    sem = ("parallel",) * (len(grid) - 1) + ("arbitrary",)
    return _call(name, body, grid=grid, in_specs=in_specs, out_specs=out_specs, out_shape=out_shape,
                 scratch=scratch, sem=sem)(*operands)


def _store(dtype):
    def epilogue(acc, ins, outs):
        outs[0][...] = acc.astype(dtype)
    return epilogue


def mm_nn_cols(a, w, token):
    S, K = a.shape
    J, _, Ns = w.shape
    tm = min(512, S)
    return _matmul("mm_nn_cols", (a, w, token),
                   [pl.BlockSpec((tm, K), lambda j, i, k: (i, 0)), pl.BlockSpec((None, K, Ns), lambda j, i, k: (j, 0, 0)),
                    pl.BlockSpec((8, LANES), lambda j, i, k: (0, 0))],
                   [jax.ShapeDtypeStruct((S, J * Ns), BF16)], [pl.BlockSpec((tm, Ns), lambda j, i, k: (i, j))],
                   (J, S // tm, 1), [(0, 1)], NN, None, _store(BF16))[0]


def ffn_up(h, wg, wu, token):
    S, K = h.shape
    J, _, Ns = wg.shape
    tm = min(512, S)

    sub = min(EPILOGUE_ROWS, tm)

    def body(h_ref, wg_ref, wu_ref, token_ref, g_ref, u_ref, a_ref):
        for r in range(tm // sub):
            rows = slice(r * sub, (r + 1) * sub)
            hv = h_ref[rows, :]
            gv = jnp.dot(hv, wg_ref[...], preferred_element_type=F32)
            uv = jnp.dot(hv, wu_ref[...], preferred_element_type=F32)
            g_ref[rows, :] = gv.astype(BF16)
            u_ref[rows, :] = uv.astype(BF16)
            a_ref[rows, :] = (gv * _sigmoid(gv) * uv).astype(BF16)

    wspec = pl.BlockSpec((None, K, Ns), lambda j, i: (j, 0, 0))
    ospec = pl.BlockSpec((tm, Ns), lambda j, i: (i, j))
    oshape = jax.ShapeDtypeStruct((S, J * Ns), BF16)
    return _call("ffn_up", body, grid=(J, S // tm),
                 in_specs=[pl.BlockSpec((tm, K), lambda j, i: (i, 0)), wspec, wspec, pl.BlockSpec((8, LANES), lambda j, i: (0, 0))],
                 out_specs=[ospec, ospec, ospec], out_shape=[oshape, oshape, oshape],
                 sem=("parallel", "parallel"))(h, wg, wu, token)


def mm_nn_rows_res(a, w, res, gain):
    S, K = a.shape
    N = w.shape[1]
    tm = min(512, S)
    tk = K if K <= 2048 else K // 4

    def body(a_ref, w_ref, res_ref, g_ref, x_ref, h_ref, *acc):
        def finish(total):
            xv = total + res_ref[...]
            x_ref[...] = xv
            r = lax.rsqrt(jnp.mean(xv * xv, axis=-1, keepdims=True) + EPS)
            h_ref[...] = (xv * r * g_ref[...]).astype(BF16)

        part = jnp.dot(a_ref[...], w_ref[...], preferred_element_type=F32)
        if K == tk:
            finish(part)
        else:
            k = pl.program_id(1)

            @pl.when(k == 0)
            def _():
                acc[0][...] = part

            @pl.when(k > 0)
            def _():
                acc[0][...] += part

            @pl.when(k == K // tk - 1)
            def _():
                finish(acc[0][...])

    row = pl.BlockSpec((tm, N), lambda i, k: (i, 0))
    return _call("mm_nn_rows_res", body, grid=(S // tm, K // tk),
                 in_specs=[pl.BlockSpec((tm, tk), lambda i, k: (i, k)), pl.BlockSpec((tk, N), lambda i, k: (k, 0)), row,
                           pl.BlockSpec((1, N), lambda i, k: (0, 0))],
                 out_specs=[row, row], out_shape=[jax.ShapeDtypeStruct((S, N), F32), jax.ShapeDtypeStruct((S, N), BF16)],
                 scratch=[pltpu.VMEM((tm, N), F32)] if K != tk else [], sem=("parallel", "arbitrary"))(a, w, res, gain)


def mm_nt_cols(pairs_in, out_dtype):
    dz0, w0 = pairs_in[0]
    S = dz0.shape[0]
    J, K, Ns = w0.shape
    tm = min(512, S)
    operands, specs, pairs = [], [], []
    for dz, w in pairs_in:
        pairs.append((len(operands), len(operands) + 1))
        operands += [dz, w]
        specs += [pl.BlockSpec((tm, Ns), lambda i, j: (i, j)), pl.BlockSpec((None, K, Ns), lambda i, j: (j, 0, 0))]
    return _matmul("mm_nt_cols%d" % len(pairs_in), tuple(operands), specs,
                   [jax.ShapeDtypeStruct((S, K), out_dtype)], [pl.BlockSpec((tm, K), lambda i, j: (i, 0))],
                   (S // tm, J), pairs, NT, (tm, K), _store(out_dtype))[0]


def mm_nt_rows(dy, w, token):
    S, N = dy.shape
    K = w.shape[0]
    tm, tko = min(1024, S), 512
    return _matmul("mm_nt_rows", (dy, w, token),
                   [pl.BlockSpec((tm, N), lambda i, kk, z: (i, 0)), pl.BlockSpec((tko, N), lambda i, kk, z: (kk, 0)),
                    pl.BlockSpec((8, LANES), lambda i, kk, z: (0, 0))],
                   [jax.ShapeDtypeStruct((S, K), BF16)], [pl.BlockSpec((tm, tko), lambda i, kk, z: (i, kk))],
                   (S // tm, K // tko, 1), [(0, 1)], NT, None, _store(BF16))[0]


def ffn_down_bwd(dy, w, gate, up, token):
    S, N = dy.shape
    K = w.shape[0]
    tm, tko = min(1024, S), 512
    sub = min(EPILOGUE_ROWS, tm)

    def body(dy_ref, w_ref, g_ref, u_ref, token_ref, dg_ref, du_ref):
        for r in range(tm // sub):
            rows = slice(r * sub, (r + 1) * sub)
            dact = lax.dot_general(dy_ref[rows, :], w_ref[...], NT, preferred_element_type=F32)
            gv = g_ref[rows, :].astype(F32)
            uv = u_ref[rows, :].astype(F32)
            sg = _sigmoid(gv)
            dg_ref[rows, :] = (dact * uv * sg * (1.0 + gv * (1.0 - sg))).astype(BF16)
            du_ref[rows, :] = (dact * gv * sg).astype(BF16)

    tile = pl.BlockSpec((tm, tko), lambda i, kk: (i, kk))
    oshape = jax.ShapeDtypeStruct((S, K), BF16)
    return _call("ffn_down_bwd", body, grid=(S // tm, K // tko),
                 in_specs=[pl.BlockSpec((tm, N), lambda i, kk: (i, 0)), pl.BlockSpec((tko, N), lambda i, kk: (kk, 0)), tile, tile,
                           pl.BlockSpec((8, LANES), lambda i, kk: (0, 0))],
                 out_specs=[tile, tile], out_shape=[oshape, oshape], sem=("parallel", "parallel"))(dy, w, gate, up, token)


def mm_tn_cols(a, dz, J):
    S, M = a.shape
    Ns = dz.shape[1] // J
    tm, tk = 512, S
    return _matmul("mm_tn_cols", (a, dz),
                   [pl.BlockSpec((tk, tm), lambda j, m, k: (k, m)), pl.BlockSpec((tk, Ns), lambda j, m, k: (k, j))],
                   [jax.ShapeDtypeStruct((J, M, Ns), BF16)], [pl.BlockSpec((None, tm, Ns), lambda j, m, k: (j, m, 0))],
                   (J, M // tm, S // tk), [(0, 1)], TN, (tm, Ns), _store(BF16))[0]


def mm_tn_rows(a, dy):
    S, K = a.shape
    N = dy.shape[1]
    tm, tk = 512, min(2048, S)
    return _matmul("mm_tn_rows", (a, dy),
                   [pl.BlockSpec((tk, tm), lambda m, k: (k, m)), pl.BlockSpec((tk, N), lambda m, k: (k, 0))],
                   [jax.ShapeDtypeStruct((K, N), BF16)], [pl.BlockSpec((tm, N), lambda m, k: (m, 0))],
                   (K // tm, S // tk), [(0, 1)], TN, (tm, N), _store(BF16))[0]


def rope_tables(S):
    half = ROT_DIM // 2
    pos = jnp.arange(S, dtype=F32)
    inv = ROPE_THETA ** (-jnp.arange(0, ROT_DIM, 2, dtype=F32) / ROT_DIM)
    ang = pos[:, None] * inv[None, :]
    cos, sin = jnp.cos(ang), jnp.sin(ang)
    zeros = jnp.zeros((S, HEAD_DIM - ROT_DIM), F32)
    c = jnp.concatenate([cos, cos, jnp.ones((S, HEAD_DIM - ROT_DIM), F32)], axis=1)
    s_lo = jnp.concatenate([-sin, jnp.zeros((S, half), F32), zeros], axis=1)
    s_hi = jnp.concatenate([jnp.zeros((S, half), F32), sin, zeros], axis=1)
    return c, s_lo, s_hi


def _rope(t, c, s_lo, s_hi):
    half = ROT_DIM // 2
    return t * c + pltpu.roll(t, HEAD_DIM - half, 1) * s_lo + pltpu.roll(t, half, 1) * s_hi


def _unrope(d, c, s_lo, s_hi):
    half = ROT_DIM // 2
    return d * c + pltpu.roll(d * s_lo, half, 1) + pltpu.roll(d * s_hi, HEAD_DIM - half, 1)


def rope_fwd(z, tabs):
    S = z.shape[0]
    nb = S // CHUNK

    def body(q_ref, kv_ref, c_ref, sl_ref, sh_ref, qr_ref, kp_ref, vp_ref):
        i = pl.program_id(0)

        @pl.when(i == 0)
        def _():
            zero = jnp.zeros((CHUNK, KV_WIDTH), BF16)
            kp_ref[0:CHUNK, :] = zero
            vp_ref[0:CHUNK, :] = zero
            kp_ref[S + CHUNK:S + 2 * CHUNK, :] = zero
            vp_ref[S + CHUNK:S + 2 * CHUNK, :] = zero

        c, sl, sh = c_ref[...], sl_ref[...], sh_ref[...]
        for h in range(N_Q_HEADS):
            cols = slice(h * HEAD_DIM, (h + 1) * HEAD_DIM)
            qr_ref[:, cols] = _rope(q_ref[:, cols].astype(F32), c, sl, sh).astype(BF16)
        rows = pl.ds(pl.multiple_of(CHUNK + i * CHUNK, CHUNK), CHUNK)
        for g in range(N_KV_HEADS):
            cols = slice(g * HEAD_DIM, (g + 1) * HEAD_DIM)
            kp_ref[rows, cols] = _rope(kv_ref[:, cols].astype(F32), c, sl, sh).astype(BF16)
        vp_ref[rows, :] = kv_ref[:, KV_WIDTH:2 * KV_WIDTH]

    tab = pl.BlockSpec((CHUNK, HEAD_DIM), lambda i: (i, 0))
    pad = pl.BlockSpec((S + 2 * CHUNK, KV_WIDTH), lambda i: (0, 0))
    return _call("rope_fwd", body, grid=(nb,),
                 in_specs=[pl.BlockSpec((CHUNK, ATTN_WIDTH), lambda i: (i, 0)),
                           pl.BlockSpec((CHUNK, 2 * KV_WIDTH), lambda i: (i, OFF_K // (2 * KV_WIDTH))), tab, tab, tab],
                 out_specs=[pl.BlockSpec((CHUNK, ATTN_WIDTH), lambda i: (i, 0)), pad, pad],
                 out_shape=[jax.ShapeDtypeStruct((S, ATTN_WIDTH), BF16), jax.ShapeDtypeStruct((S + 2 * CHUNK, KV_WIDTH), BF16),
                            jax.ShapeDtypeStruct((S + 2 * CHUNK, KV_WIDTH), BF16)], sem=("arbitrary",))(z, z, *tabs)


def rope_bwd(dq, dkp, dvp, tabs):
    S = dq.shape[0]

    def body(dq_ref, dk_ref, dv_ref, c_ref, sl_ref, sh_ref, o_ref):
        c, sl, sh = c_ref[...], sl_ref[...], sh_ref[...]
        for h in range(N_Q_HEADS):
            cols = slice(h * HEAD_DIM, (h + 1) * HEAD_DIM)
            o_ref[:, cols] = _unrope(dq_ref[:, cols], c, sl, sh).astype(BF16)
        for g in range(N_KV_HEADS):
            cols = slice(g * HEAD_DIM, (g + 1) * HEAD_DIM)
            o_ref[:, OFF_K + g * HEAD_DIM:OFF_K + (g + 1) * HEAD_DIM] = _unrope(dk_ref[:, cols], c, sl, sh).astype(BF16)
        o_ref[:, OFF_V:OFF_V + KV_WIDTH] = dv_ref[...].astype(BF16)

    tab = pl.BlockSpec((CHUNK, HEAD_DIM), lambda i: (i, 0))
    pad = pl.BlockSpec((CHUNK, KV_WIDTH), lambda i: (i + 1, 0))
    return _call("rope_bwd", body, grid=(S // CHUNK,),
                 in_specs=[pl.BlockSpec((CHUNK, ATTN_WIDTH), lambda i: (i, 0)), pad, pad, tab, tab, tab],
                 out_specs=pl.BlockSpec((CHUNK, OFF_CA), lambda i: (i, 0)),
                 out_shape=jax.ShapeDtypeStruct((S, OFF_CA), BF16), sem=("parallel",))(dq, dkp, dvp, *tabs)


STACK = Q_PER_KV * CHUNK


def _stack_heads(ref, rows):
    return jnp.concatenate([ref[rows, r * HEAD_DIM:(r + 1) * HEAD_DIM] for r in range(Q_PER_KV)], axis=0)


def _stack_sinks(s_ref):
    return jnp.concatenate([jnp.broadcast_to(s_ref[r:r + 1, 0:1], (CHUNK, 1)) for r in range(Q_PER_KV)], axis=0)


MASKED = -1e30


def _scores(q, kb):
    return lax.dot_general(q, kb, NT, preferred_element_type=F32) * (1.0 / math.sqrt(HEAD_DIM))


def _band_bias():
    row = lax.broadcasted_iota(jnp.int32, (STACK, 3 * CHUNK), 0) & (CHUNK - 1)
    col = lax.broadcasted_iota(jnp.int32, (STACK, 3 * CHUNK), 1)
    return jnp.where(jnp.abs(col - CHUNK - row) <= WINDOW, 0.0, MASKED).astype(F32)


def _edge_bias(n, S):
    kpos = (n - 1) * CHUNK + lax.broadcasted_iota(jnp.int32, (1, 3 * CHUNK), 1)
    return jnp.where((kpos >= 0) & (kpos < S), 0.0, MASKED).astype(F32)


def _softmax_sink(s, sk, bias):
    s = s + bias
    m = jnp.maximum(jnp.max(s, axis=1, keepdims=True), sk)
    e = jnp.exp(s - m)
    es = jnp.exp(sk - m)
    inv = 1.0 / (jnp.sum(e, axis=1, keepdims=True) + es)
    return e * inv, es * inv


def _block_views(i, nblk):
    ns = [i * nblk + b for b in range(nblk)]
    wins = [pl.ds(pl.multiple_of(n * CHUNK, CHUNK), 3 * CHUNK) for n in ns]
    rows = [slice(b * CHUNK, (b + 1) * CHUNK) for b in range(nblk)]
    return ns, wins, rows


def attn_fwd(qr, kp, vp, sink3):
    S = qr.shape[0]
    tq = min(2048, S)
    gw = Q_PER_KV * HEAD_DIM
    nblk = tq // CHUNK

    def body(q_ref, k_ref, v_ref, s_ref, o_ref):
        ns, wins, rows = _block_views(pl.program_id(1), nblk)
        sk = _stack_sinks(s_ref)
        band = _band_bias()
        scores = [_scores(_stack_heads(q_ref, rows[b]), k_ref[wins[b], :]) for b in range(nblk)]
        probs = [_softmax_sink(scores[b], sk, band + _edge_bias(ns[b], S))[0].astype(BF16) for b in range(nblk)]
        outs = [jnp.dot(probs[b], v_ref[wins[b], :], preferred_element_type=F32).astype(BF16) for b in range(nblk)]
        for b in range(nblk):
            for r in range(Q_PER_KV):
                o_ref[rows[b], r * HEAD_DIM:(r + 1) * HEAD_DIM] = outs[b][r * CHUNK:(r + 1) * CHUNK]

    kv = pl.BlockSpec((S + 2 * CHUNK, HEAD_DIM), lambda g, i: (0, g))
    return _call("attn_fwd", body, grid=(N_KV_HEADS, S // tq),
                 in_specs=[pl.BlockSpec((tq, gw), lambda g, i: (i, g)), kv, kv,
                           pl.BlockSpec((None, Q_PER_KV, LANES), lambda g, i: (g, 0, 0))],
                 out_specs=pl.BlockSpec((tq, gw), lambda g, i: (i, g)),
                 out_shape=jax.ShapeDtypeStruct((S, ATTN_WIDTH), BF16), sem=("parallel", "arbitrary"))(qr, kp, vp, sink3)


def attn_bwd(qr, kp, vp, sink3, dmix):
    S = qr.shape[0]
    tq = min(1024, S)
    gw = Q_PER_KV * HEAD_DIM
    scale = 1.0 / math.sqrt(HEAD_DIM)
    nblk = tq // CHUNK

    def body(q_ref, k_ref, v_ref, s_ref, do_ref, dq_ref, dk_ref, dv_ref, ds_ref):
        i = pl.program_id(1)

        @pl.when(i == 0)
        def _():
            dk_ref[...] = jnp.zeros_like(dk_ref)
            dv_ref[...] = jnp.zeros_like(dv_ref)
            ds_ref[...] = jnp.zeros_like(ds_ref)

        blocks = range(nblk)
        ns, wins, rows = _block_views(i, nblk)
        sk = _stack_sinks(s_ref)
        band = _band_bias()
        qs = [_stack_heads(q_ref, rows[b]) for b in blocks]
        dos = [_stack_heads(do_ref, rows[b]) for b in blocks]
        scores = [_scores(qs[b], k_ref[wins[b], :]) for b in blocks]
        dps = [lax.dot_general(dos[b], v_ref[wins[b], :], NT, preferred_element_type=F32) for b in blocks]
        probs = [_softmax_sink(scores[b], sk, band + _edge_bias(ns[b], S)) for b in blocks]
        deltas = [jnp.sum(probs[b][0] * dps[b], axis=1, keepdims=True) for b in blocks]
        dscs = [(probs[b][0] * (dps[b] - deltas[b]) * scale).astype(BF16) for b in blocks]
        dqs = [jnp.dot(dscs[b], k_ref[wins[b], :], preferred_element_type=F32) for b in blocks]
        dks = [lax.dot_general(dscs[b], qs[b], TN, preferred_element_type=F32) for b in blocks]
        dvs = [lax.dot_general(probs[b][0].astype(BF16), dos[b], TN, preferred_element_type=F32) for b in blocks]
        for b in blocks:
            for r in range(Q_PER_KV):
                dq_ref[rows[b], r * HEAD_DIM:(r + 1) * HEAD_DIM] = dqs[b][r * CHUNK:(r + 1) * CHUNK]
        for m in range(nblk + 2):
            parts = [(b, m - b) for b in blocks if 0 <= m - b <= 2]
            krows = pl.ds(pl.multiple_of(i * tq + m * CHUNK, CHUNK), CHUNK)
            dk_ref[krows, :] += sum(dks[b][o * CHUNK:(o + 1) * CHUNK] for b, o in parts)
            dv_ref[krows, :] += sum(dvs[b][o * CHUNK:(o + 1) * CHUNK] for b, o in parts)
        for r in range(Q_PER_KV):
            head = slice(r * CHUNK, (r + 1) * CHUNK)
            dsink = sum(jnp.sum(-probs[b][1][head] * deltas[b][head], axis=0, keepdims=True) for b in blocks)
            ds_ref[r:r + 1, :] += jnp.broadcast_to(dsink, (1, LANES))

    kv = pl.BlockSpec((S + 2 * CHUNK, HEAD_DIM), lambda g, i: (0, g))
    qspec = pl.BlockSpec((tq, gw), lambda g, i: (i, g))
    sspec = pl.BlockSpec((None, Q_PER_KV, LANES), lambda g, i: (g, 0, 0))
    padshape = jax.ShapeDtypeStruct((S + 2 * CHUNK, KV_WIDTH), F32)
    return _call("attn_bwd", body, grid=(N_KV_HEADS, S // tq),
                 in_specs=[qspec, kv, kv, sspec, qspec],
                 out_specs=[qspec, kv, kv, sspec],
                 out_shape=[jax.ShapeDtypeStruct((S, ATTN_WIDTH), F32), padshape, padshape,
                            jax.ShapeDtypeStruct((N_KV_HEADS, Q_PER_KV, LANES), F32)],
                 sem=("parallel", "arbitrary"))(qr, kp, vp, sink3, dmix)


CONV_TILE = 256


def _fill_padded(dst_ref, value, S):
    zero = jnp.zeros((CONV_PAD, LANES), F32)
    dst_ref[0:CONV_PAD, :] = zero
    dst_ref[CONV_PAD + S:2 * CONV_PAD + S, :] = zero
    dst_ref[CONV_PAD:CONV_PAD + S, :] = value


def conv_dw_fwd(z, w32, b):
    S = z.shape[0]
    T = min(CONV_TILE, S)
    lo = CONV_PAD - (CONV_KERNEL - 1) // 2

    def body(a_ref, g_ref, w_ref, b_ref, o_ref, c0_ref):
        _fill_padded(c0_ref, a_ref[...].astype(F32) * _sigmoid(g_ref[...].astype(F32)), S)

        def tile(t, carry):
            base = pl.multiple_of(t * T, T)
            acc = jnp.broadcast_to(b_ref[...], (T, LANES))
            for j in range(CONV_KERNEL):
                acc = acc + w_ref[j:j + 1, :] * c0_ref[pl.ds(base + lo + j, T), :]
            o_ref[pl.ds(base, T), :] = acc
            return carry

        lax.fori_loop(0, S // T, tile, 0)

    nca, ncg = OFF_CA // LANES, OFF_CG // LANES
    return _call("conv_dw_fwd", body, grid=(CONV_WIDTH // LANES,),
                 in_specs=[pl.BlockSpec((S, LANES), lambda cb: (0, nca + cb)), pl.BlockSpec((S, LANES), lambda cb: (0, ncg + cb)),
                           pl.BlockSpec((32, LANES), lambda cb: (0, cb)), pl.BlockSpec((1, LANES), lambda cb: (0, cb))],
                 out_specs=pl.BlockSpec((S, LANES), lambda cb: (0, cb)),
                 out_shape=jax.ShapeDtypeStruct((S, CONV_WIDTH), F32),
                 scratch=[pltpu.VMEM((S + 2 * CONV_PAD, LANES), F32)], sem=("parallel",))(z, z, w32, b)


def _ln_stats(x):
    mu = jnp.mean(x, axis=-1, keepdims=True)
    xc = x - mu
    rs = lax.rsqrt(jnp.mean(xc * xc, axis=-1, keepdims=True) + EPS)
    return xc * rs, rs


def _ln_bwd(dy, xh, rs, g):
    dxh = dy * g
    return rs * (dxh - jnp.mean(dxh, axis=-1, keepdims=True) - xh * jnp.mean(dxh * xh, axis=-1, keepdims=True))


def conv_ln_fwd(c1, g, b):
    S = c1.shape[0]
    T = min(512, S)

    def body(x_ref, g_ref, b_ref, o_ref):
        xh, _ = _ln_stats(x_ref[...])
        y = xh * g_ref[...] + b_ref[...]
        o_ref[...] = (y * _sigmoid(y)).astype(BF16)

    row = pl.BlockSpec((T, CONV_WIDTH), lambda i: (i, 0))
    vec = pl.BlockSpec((1, CONV_WIDTH), lambda i: (0, 0))
    return _call("conv_ln_fwd", body, grid=(S // T,), in_specs=[row, vec, vec], out_specs=row,
                 out_shape=jax.ShapeDtypeStruct((S, CONV_WIDTH), BF16), sem=("parallel",))(c1, g, b)


def _acc_out(ref, value):
    @pl.when(pl.program_id(0) == 0)
    def _():
        ref[...] = value

    @pl.when(pl.program_id(0) > 0)
    def _():
        ref[...] += value


def conv_ln_bwd(dmix, c1, g, b):
    S = c1.shape[0]
    T = min(512, S)

    def body(d_ref, x_ref, g_ref, b_ref, dx_ref, dg_ref, db_ref):
        xh, rs = _ln_stats(x_ref[...])
        gv = g_ref[...]
        y = xh * gv + b_ref[...]
        sg = _sigmoid(y)
        dy = d_ref[...].astype(F32) * sg * (1.0 + y * (1.0 - sg))
        dx_ref[...] = _ln_bwd(dy, xh, rs, gv)
        _acc_out(dg_ref, jnp.sum(dy * xh, axis=0, keepdims=True))
        _acc_out(db_ref, jnp.sum(dy, axis=0, keepdims=True))

    row = pl.BlockSpec((T, CONV_WIDTH), lambda i: (i, 0))
    vec = pl.BlockSpec((1, CONV_WIDTH), lambda i: (0, 0))
    vshape = jax.ShapeDtypeStruct((1, CONV_WIDTH), F32)
    return _call("conv_ln_bwd", body, grid=(S // T,),
                 in_specs=[pl.BlockSpec((T, CONV_WIDTH), lambda i: (i, ATTN_WIDTH // CONV_WIDTH)), row, vec, vec],
                 out_specs=[row, vec, vec], out_shape=[jax.ShapeDtypeStruct((S, CONV_WIDTH), F32), vshape, vshape],
                 sem=("arbitrary",))(dmix, c1, g, b)


def conv_dw_bwd(dc1, z, w32):
    S = z.shape[0]
    T = min(CONV_TILE, S)
    half = (CONV_KERNEL - 1) // 2
    lo = CONV_PAD - half

    def body(d_ref, a_ref, g_ref, w_ref, da_ref, dg_ref, dw_ref, db_ref, c0_ref, d1_ref, wacc_ref):
        av = a_ref[...].astype(F32)
        sg = _sigmoid(g_ref[...].astype(F32))
        _fill_padded(c0_ref, av * sg, S)
        _fill_padded(d1_ref, d_ref[...], S)
        wacc_ref[...] = jnp.zeros_like(wacc_ref)

        def tile(t, carry):
            base = pl.multiple_of(t * T, T)
            d1 = d_ref[pl.ds(base, T), :]
            acc = jnp.zeros((T, LANES), F32)
            for j in range(CONV_KERNEL):
                acc = acc + w_ref[j:j + 1, :] * d1_ref[pl.ds(base + CONV_PAD + half - j, T), :]
                prod = d1 * c0_ref[pl.ds(base + lo + j, T), :]
                wacc_ref[j] += jnp.sum(prod.reshape(T // 8, 8, LANES), axis=0)
            rows = pl.ds(base, T)
            a_t = a_ref[rows, :].astype(F32)
            s_t = _sigmoid(g_ref[rows, :].astype(F32))
            da_ref[rows, :] = (acc * s_t).astype(BF16)
            dg_ref[rows, :] = (acc * a_t * s_t * (1.0 - s_t)).astype(BF16)
            return carry

        lax.fori_loop(0, S // T, tile, 0)
        dw_ref[...] = jnp.sum(wacc_ref[...], axis=1)
        db_ref[...] = jnp.sum(d_ref[...], axis=0, keepdims=True)

    nca, ncg = OFF_CA // LANES, OFF_CG // LANES
    col = pl.BlockSpec((S, LANES), lambda cb: (0, cb))
    oshape = jax.ShapeDtypeStruct((S, CONV_WIDTH), BF16)
    return _call("conv_dw_bwd", body, grid=(CONV_WIDTH // LANES,),
                 in_specs=[col, pl.BlockSpec((S, LANES), lambda cb: (0, nca + cb)), pl.BlockSpec((S, LANES), lambda cb: (0, ncg + cb)),
                           pl.BlockSpec((32, LANES), lambda cb: (0, cb))],
                 out_specs=[col, col, pl.BlockSpec((32, LANES), lambda cb: (0, cb)), pl.BlockSpec((1, LANES), lambda cb: (0, cb))],
                 out_shape=[oshape, oshape, jax.ShapeDtypeStruct((32, CONV_WIDTH), F32), jax.ShapeDtypeStruct((1, CONV_WIDTH), F32)],
                 scratch=[pltpu.VMEM((S + 2 * CONV_PAD, LANES), F32), pltpu.VMEM((S + 2 * CONV_PAD, LANES), F32),
                          pltpu.VMEM((32, 8, LANES), F32)], sem=("parallel",))(dc1, z, z, w32)


_INV_SQRT2 = 1.0 / math.sqrt(2.0)
_INV_SQRT2PI = 1.0 / math.sqrt(2.0 * math.pi)


def _gelu(x):
    return 0.5 * x * (1.0 + lax.erf(x * _INV_SQRT2))


def _gelu_grad(x):
    return 0.5 * (1.0 + lax.erf(x * _INV_SQRT2)) + x * jnp.exp(-0.5 * x * x) * _INV_SQRT2PI


def sgu_fwd(z, g, b, ws, bs):
    S = z.shape[0]
    T = min(512, S)

    def body(u_ref, v_ref, g_ref, b_ref, ws_ref, bs_ref, o_ref):
        xh, _ = _ln_stats(_gelu(v_ref[...].astype(F32)))
        vn = (xh * g_ref[...] + b_ref[...]).astype(BF16)
        for ch in range(T // CHUNK):
            rows = slice(ch * CHUNK, (ch + 1) * CHUNK)
            for h in range(SGU_HEADS):
                cols = slice(h * HEAD_DIM, (h + 1) * HEAD_DIM)
                sp = jnp.dot(ws_ref[h], vn[rows, cols], preferred_element_type=F32) + bs_ref[h]
                o_ref[rows, cols] = (_gelu(u_ref[rows, cols].astype(F32)) * sp).astype(BF16)

    vec = pl.BlockSpec((1, SGU_WIDTH), lambda i: (0, 0))
    full = pl.BlockSpec((SGU_HEADS, CHUNK, CHUNK), lambda i: (0, 0, 0))
    return _call("sgu_fwd", body, grid=(S // T,),
                 in_specs=[pl.BlockSpec((T, SGU_WIDTH), lambda i: (i, OFF_U // SGU_WIDTH)),
                           pl.BlockSpec((T, SGU_WIDTH), lambda i: (i, OFF_VV // SGU_WIDTH)), vec, vec, full, full],
                 out_specs=pl.BlockSpec((T, SGU_WIDTH), lambda i: (i, 0)),
                 out_shape=jax.ShapeDtypeStruct((S, SGU_WIDTH), BF16), sem=("parallel",))(z, z, g, b, ws, bs)


def sgu_bwd(z, dmix, g, b, ws, bs):
    S = z.shape[0]
    T = min(512, S)

    def body(u_ref, v_ref, d_ref, g_ref, b_ref, ws_ref, bs_ref, du_ref, dv_ref, dws_ref, dbs_ref, dg_ref, db_ref, dvn_ref):
        @pl.when(pl.program_id(0) == 0)
        def _():
            dws_ref[...] = jnp.zeros_like(dws_ref)
            dbs_ref[...] = jnp.zeros_like(dbs_ref)

        vraw = v_ref[...].astype(F32)
        xh, rs = _ln_stats(_gelu(vraw))
        gv = g_ref[...]
        vn = (xh * gv + b_ref[...]).astype(BF16)
        for ch in range(T // CHUNK):
            rows = slice(ch * CHUNK, (ch + 1) * CHUNK)
            for h in range(SGU_HEADS):
                cols = slice(h * HEAD_DIM, (h + 1) * HEAD_DIM)
                w = ws_ref[h]
                vb = vn[rows, cols]
                sp = jnp.dot(w, vb, preferred_element_type=F32) + bs_ref[h]
                uraw = u_ref[rows, cols].astype(F32)
                dout = d_ref[rows, cols].astype(F32)
                du_ref[rows, cols] = (dout * sp * _gelu_grad(uraw)).astype(BF16)
                dsp = dout * _gelu(uraw)
                dspb = dsp.astype(BF16)
                dvn_ref[rows, cols] = lax.dot_general(w, dspb, TN, preferred_element_type=F32)
                dws_ref[h] += lax.dot_general(dspb, vb, NT, preferred_element_type=F32)
                dbs_ref[h] += jnp.sum(dsp, axis=1, keepdims=True)
        dvn = dvn_ref[...]
        dv_ref[...] = (_ln_bwd(dvn, xh, rs, gv) * _gelu_grad(vraw)).astype(BF16)
        _acc_out(dg_ref, jnp.sum(dvn * xh, axis=0, keepdims=True))
        _acc_out(db_ref, jnp.sum(dvn, axis=0, keepdims=True))

    vec = pl.BlockSpec((1, SGU_WIDTH), lambda i: (0, 0))
    full = pl.BlockSpec((SGU_HEADS, CHUNK, CHUNK), lambda i: (0, 0, 0))
    row = pl.BlockSpec((T, SGU_WIDTH), lambda i: (i, 0))
    oshape = jax.ShapeDtypeStruct((S, SGU_WIDTH), BF16)
    vshape = jax.ShapeDtypeStruct((1, SGU_WIDTH), F32)
    return _call("sgu_bwd", body, grid=(S // T,),
                 in_specs=[pl.BlockSpec((T, SGU_WIDTH), lambda i: (i, OFF_U // SGU_WIDTH)),
                           pl.BlockSpec((T, SGU_WIDTH), lambda i: (i, OFF_VV // SGU_WIDTH)),
                           pl.BlockSpec((T, SGU_WIDTH), lambda i: (i, (ATTN_WIDTH + CONV_WIDTH) // SGU_WIDTH)), vec, vec, full, full],
                 out_specs=[row, row, full, pl.BlockSpec((SGU_HEADS, CHUNK, 1), lambda i: (0, 0, 0)), vec, vec],
                 out_shape=[oshape, oshape, jax.ShapeDtypeStruct((SGU_HEADS, CHUNK, CHUNK), F32),
                            jax.ShapeDtypeStruct((SGU_HEADS, CHUNK, 1), F32), vshape, vshape],
                 scratch=[pltpu.VMEM((T, SGU_WIDTH), F32)], sem=("arbitrary",))(z, z, dmix, g, b, ws, bs)


def _row_tile(rows, cols, n_arrays, budget_mib=24):
    budget = (budget_mib * 1024 * 1024) // (n_arrays * 2 * 4 * cols)
    t = min(rows, max(16, budget // 16 * 16))
    while rows % t:
        t -= 16
    return t


def add_sibling_half(grad, recv, c_idx):
    J, R, C = grad.shape
    hr = R // 2
    tr = _row_tile(hr, C, 3)
    nb = hr // tr

    def body(c_ref, g_ref, r_ref, o_ref):
        o_ref[...] = (g_ref[...].astype(F32) + r_ref[...].astype(F32)).astype(BF16)

    grid_spec = pltpu.PrefetchScalarGridSpec(
        num_scalar_prefetch=1, grid=(J, nb),
        in_specs=[pl.BlockSpec((None, tr, C), lambda j, i, c: (j, c[0] * nb + i, 0)),
                  pl.BlockSpec((None, tr, C), lambda j, i, c: (j, i, 0))],
        out_specs=pl.BlockSpec((None, tr, C), lambda j, i, c: (j, i, 0)))
    return pl.pallas_call(body, name="add_sibling_half", grid_spec=grid_spec,
                          out_shape=jax.ShapeDtypeStruct((J, hr, C), BF16),
                          compiler_params=pltpu.CompilerParams(vmem_limit_bytes=VMEM_LIMIT,
                                                               dimension_semantics=("parallel", "parallel")))(c_idx, grad, recv)


def sum_chips(own, others, stack, x_idx, y_idx, layer):
    R, C = own.shape[1:]
    tr = _row_tile(R, C, 4)

    def body(x_ref, y_ref, own_ref, oth_ref, stack_ref, o_ref):
        acc = own_ref[...].astype(F32)
        for j in range(3):
            acc = acc + oth_ref[j].astype(F32)
        o_ref[...] = acc

    grid_spec = pltpu.PrefetchScalarGridSpec(
        num_scalar_prefetch=2, grid=(R // tr,),
        in_specs=[pl.BlockSpec((None, tr, C), lambda i, xr, yr: (2 * xr[0] + yr[0], i, 0)),
                  pl.BlockSpec((3, tr, C), lambda i, xr, yr: (0, i, 0)),
                  pl.BlockSpec(memory_space=pl.ANY)],
        out_specs=pl.BlockSpec((None, tr, C), lambda i, xr, yr: (layer, i, 0)))
    return pl.pallas_call(body, name="sum_chips", grid_spec=grid_spec,
                          out_shape=jax.ShapeDtypeStruct(stack.shape, F32), input_output_aliases={4: 0},
                          compiler_params=pltpu.CompilerParams(vmem_limit_bytes=VMEM_LIMIT,
                                                               dimension_semantics=("parallel",)))(x_idx, y_idx, own, others, stack)


def adamw_halves(w, mine, theirs, m, v, c_idx):
    L, R, C = w.shape
    hr = R // 2
    tr = _row_tile(hr, C, 9, budget_mib=40)
    nb = hr // tr

    def body(c_ref, w_ref, a_ref, b_ref, m_ref, v_ref, g_ref, d_ref, nm_ref, nv_ref):
        gv = jnp.where(pl.program_id(1) == c_ref[0], a_ref[...], b_ref[...])
        g_ref[...] = gv
        nm = ADAM_B1 * m_ref[...] + (1.0 - ADAM_B1) * gv
        nv = ADAM_B2 * v_ref[...] + (1.0 - ADAM_B2) * (gv * gv)
        m_hat = nm / (1.0 - ADAM_B1 ** ADAM_STEP)
        v_hat = nv / (1.0 - ADAM_B2 ** ADAM_STEP)
        d_ref[...] = -ADAM_LR * (m_hat / (jnp.sqrt(v_hat) + ADAM_EPS) + ADAM_WD * w_ref[...])
        nm_ref[...] = nm
        nv_ref[...] = nv

    full = pl.BlockSpec((None, tr, C), lambda l, h, i, c: (l, h * nb + i, 0))
    a_spec = pl.BlockSpec((None, tr, C), lambda l, h, i, c: (l, jnp.where(h == c[0], i, 0), 0))
    b_spec = pl.BlockSpec((None, tr, C), lambda l, h, i, c: (l, jnp.where(h == c[0], 0, i), 0))
    grid_spec = pltpu.PrefetchScalarGridSpec(num_scalar_prefetch=1, grid=(L, 2, nb),
                                             in_specs=[full, a_spec, b_spec, full, full], out_specs=[full] * 4)
    shape = jax.ShapeDtypeStruct((L, R, C), F32)
    return pl.pallas_call(body, name="adamw_halves", grid_spec=grid_spec, out_shape=[shape] * 4,
                          compiler_params=pltpu.CompilerParams(vmem_limit_bytes=VMEM_LIMIT,
                                                               dimension_semantics=("parallel", "arbitrary", "arbitrary")))(
        c_idx, w, mine, theirs, m, v)


def adamw(w, g, m, v):
    R, C = w.shape
    tr = _row_tile(R, C, 7)

    def body(w_ref, g_ref, m_ref, v_ref, d_ref, nm_ref, nv_ref):
        gv = g_ref[...]
        nm = ADAM_B1 * m_ref[...] + (1.0 - ADAM_B1) * gv
        nv = ADAM_B2 * v_ref[...] + (1.0 - ADAM_B2) * (gv * gv)
        m_hat = nm / (1.0 - ADAM_B1 ** ADAM_STEP)
        v_hat = nv / (1.0 - ADAM_B2 ** ADAM_STEP)
        d_ref[...] = -ADAM_LR * (m_hat / (jnp.sqrt(v_hat) + ADAM_EPS) + ADAM_WD * w_ref[...])
        nm_ref[...] = nm
        nv_ref[...] = nv

    spec = pl.BlockSpec((tr, C), lambda i: (i, 0))
    shape = jax.ShapeDtypeStruct((R, C), F32)
    return _call("adamw", body, grid=(R // tr,), in_specs=[spec] * 4, out_specs=[spec] * 3, out_shape=[shape] * 3,
                 sem=("parallel",))(w, g, m, v)


def _place():
    x, y, c = lax.axis_index("x"), lax.axis_index("y"), lax.axis_index("c")
    chips = [(1 - x, y), (x, 1 - y), (1 - x, 1 - y)]
    return x, y, c, chips


def _remote(src, dst, send_sem, recv_sem, dev):
    return pltpu.make_async_remote_copy(src_ref=src, dst_ref=dst, send_sem=send_sem, recv_sem=recv_sem,
                                        device_id=dev, device_id_type=MESH)


EFFECT = pltpu.SideEffectType.DATAFLOW_SIDE_EFFECTING
SEM = pl.BlockSpec(memory_space=pltpu.SEMAPHORE)
ANY = pl.BlockSpec(memory_space=pl.ANY)
TOKEN = jax.ShapeDtypeStruct((8, LANES), F32)


def _in_hbm(a):
    return pltpu.with_memory_space_constraint(a, pltpu.HBM)


def _gather_copies(shards, lands, send_sems, recv_sems):
    x, y, c, chips = _place()
    me = 2 * x + y
    copies = []
    for k in range(len(shards)):
        hr = shards[k].shape[0] // 2
        mine = pl.ds(pl.multiple_of(c * hr, 8), hr)
        for t, (px, py) in enumerate(chips):
            copies.append(_remote(shards[k].at[mine, :], lands[k].at[me, mine, :], send_sems.at[4 * k + t], recv_sems.at[4 * k + t],
                                  (px, py, c)))
        copies.append(_remote(shards[k], lands[k].at[me], send_sems.at[4 * k + 3], recv_sems.at[4 * k + 3], (x, y, 1 - c)))
    return copies


def _gather_landings(lands, send_sems, recv_sems):
    x, y, c, chips = _place()
    me = 2 * x + y
    landings = []
    for k in range(len(lands)):
        hr = lands[k].shape[1] // 2
        mine = pl.ds(pl.multiple_of(c * hr, 8), hr)
        for t, (px, py) in enumerate(chips):
            dst = lands[k].at[2 * px + py, mine, :]
            landings.append(_remote(dst, dst, send_sems.at[4 * k + t], recv_sems.at[4 * k + t], (px, py, c)))
        dst = lands[k].at[me]
        landings.append(_remote(dst, dst, send_sems.at[4 * k + 3], recv_sems.at[4 * k + 3], (x, y, 1 - c)))
    return landings


def gather_start(shards, after):
    n = len(shards)

    def body(*refs):
        srcs, lands_in = refs[:n], refs[n:2 * n]
        send_sems, recv_sems = refs[2 * n + 1], refs[2 * n + 2]
        token = refs[-1]
        for cp in _gather_copies(srcs, lands_in, send_sems, recv_sems):
            cp.start()
        token[...] = jnp.zeros_like(token)

    lands = [lax.empty((N_CHIPS,) + s.shape, s.dtype) for s in shards]
    outs = pl.pallas_call(
        body, name="gather_start", in_specs=[HBM] * (2 * n) + [ANY],
        out_specs=[SEM, SEM] + [HBM] * (2 * n) + [VMEM_SPEC],
        out_shape=[pltpu.SemaphoreType.DMA((4 * n,)), pltpu.SemaphoreType.DMA((4 * n,))]
        + [pltpu.HBM(s.shape, s.dtype) for s in shards] + [pltpu.HBM(l.shape, l.dtype) for l in lands] + [TOKEN],
        input_output_aliases={i: 2 + i for i in range(2 * n)},
        compiler_params=pltpu.CompilerParams(has_side_effects=EFFECT),
    )(*[_in_hbm(s) for s in shards], *[_in_hbm(l) for l in lands], after)
    return outs[0], outs[1], outs[2:2 + n], outs[2 + n:2 + 2 * n], outs[-1]


def gather_wait(send_sems, recv_sems, shards, lands, after):
    n = len(shards)

    def body(*refs):
        srcs, lands_in = refs[:n], refs[n:2 * n]
        send, recv = refs[2 * n], refs[2 * n + 1]
        for cp in _gather_copies(srcs, lands_in, send, recv):
            cp.wait_send()
        for cp in _gather_landings(lands_in, send, recv):
            cp.wait_recv()

    outs = pl.pallas_call(
        body, name="gather_wait", in_specs=[HBM] * (2 * n) + [SEM, SEM, ANY], out_specs=[HBM] * (2 * n),
        out_shape=[pltpu.HBM(s.shape, s.dtype) for s in shards] + [pltpu.HBM(l.shape, l.dtype) for l in lands],
        input_output_aliases={i: i for i in range(2 * n)},
        compiler_params=pltpu.CompilerParams(has_side_effects=EFFECT),
    )(*shards, *lands, send_sems, recv_sems, after)
    return outs[n:]


def forward_halves(lands):
    n = len(lands)

    def body(*refs):
        ins, outs = refs[:n], refs[n:2 * n]
        send_sems, recv_sems = refs[2 * n:]
        x, y, c, chips = _place()
        sibling = (x, y, 1 - c)
        sends = []
        for k in range(n):
            hr = ins[k].shape[1] // 2
            mine = pl.ds(pl.multiple_of(c * hr, 8), hr)
            for t, (px, py) in enumerate(chips):
                cp = _remote(ins[k].at[2 * px + py, mine, :], outs[k].at[2 * px + py, mine, :],
                             send_sems.at[k, t], recv_sems.at[k, t], sibling)
                cp.start()
                sends.append(cp)
        for k in range(n):
            hr = ins[k].shape[1] // 2
            other = pl.ds(pl.multiple_of((1 - c) * hr, 8), hr)
            for t, (px, py) in enumerate(chips):
                dst = outs[k].at[2 * px + py, other, :]
                _remote(dst, dst, send_sems.at[k, t], recv_sems.at[k, t], sibling).wait_recv()
        for cp in sends:
            cp.wait_send()

    return pl.pallas_call(
        body, name="forward_halves", in_specs=[HBM] * n, out_specs=[HBM] * n,
        out_shape=[jax.ShapeDtypeStruct(l.shape, l.dtype) for l in lands],
        input_output_aliases={i: i for i in range(n)},
        scratch_shapes=[pltpu.SemaphoreType.DMA((n, 3)), pltpu.SemaphoreType.DMA((n, 3))],
    )(*lands)


def gather_small(block):
    def body(in_ref, out_ref, send_sems, recv_sems):
        x, y, c, chips = _place()
        me = 2 * x + y
        out_ref[me] = in_ref[...]
        sends = []
        for t, (px, py) in enumerate(chips):
            cp = _remote(in_ref, out_ref.at[me], send_sems.at[t], recv_sems.at[t], (px, py, c))
            cp.start()
            sends.append(cp)
        for t, (px, py) in enumerate(chips):
            landed = out_ref.at[2 * px + py]
            _remote(landed, landed, send_sems.at[t], recv_sems.at[t], (px, py, c)).wait_recv()
        for cp in sends:
            cp.wait_send()

    return pl.pallas_call(
        body, name="gather_small", in_specs=[VMEM_SPEC], out_specs=VMEM_SPEC,
        out_shape=jax.ShapeDtypeStruct((N_CHIPS,) + block.shape, block.dtype),
        scratch_shapes=[pltpu.SemaphoreType.DMA((3,)), pltpu.SemaphoreType.DMA((3,))],
    )(block)


def exchange_sibling_halves(grads):
    n = len(grads)

    def body(*refs):
        ins, outs = refs[:n], refs[n:2 * n]
        send_sems, recv_sems = refs[2 * n:]
        x, y, c, _ = _place()
        copies = []
        for k in range(n):
            hr = ins[k].shape[1] // 2
            theirs = pl.ds(pl.multiple_of((1 - c) * hr, 8), hr)
            cp = _remote(ins[k].at[:, theirs, :], outs[k], send_sems.at[k], recv_sems.at[k], (x, y, 1 - c))
            cp.start()
            copies.append(cp)
        for cp in copies:
            cp.wait()

    return pl.pallas_call(
        body, name="exchange_sibling_halves", in_specs=[HBM] * n, out_specs=[HBM] * n,
        out_shape=[jax.ShapeDtypeStruct((g.shape[0], g.shape[1] // 2, g.shape[2]), g.dtype) for g in grads],
        scratch_shapes=[pltpu.SemaphoreType.DMA((n,)), pltpu.SemaphoreType.DMA((n,))],
    )(*grads)


def _sibling_half_copies(grads, lands, send_sems, recv_sems):
    x, y, c, _ = _place()
    copies = []
    for k in range(len(grads)):
        hr = grads[k].shape[1] // 2
        theirs = pl.ds(pl.multiple_of((1 - c) * hr, 8), hr)
        copies.append(_remote(grads[k].at[:, theirs, :], lands[k], send_sems.at[k], recv_sems.at[k], (x, y, 1 - c)))
    return copies


def _sibling_whole_copies(srcs, lands, send_sems, recv_sems):
    x, y, c, _ = _place()
    return [_remote(srcs[k], lands[k], send_sems.at[k], recv_sems.at[k], (x, y, 1 - c)) for k in range(len(srcs))]


def pair_start(name, make_copies, srcs, land_shapes):
    n = len(srcs)

    def body(*refs):
        src_refs, land_refs = refs[:n], refs[n:2 * n]
        send_sems, recv_sems = refs[2 * n], refs[2 * n + 1]
        token = refs[-1]
        for cp in make_copies(src_refs, land_refs, send_sems, recv_sems):
            cp.start()
        token[...] = jnp.zeros_like(token)

    lands = [lax.empty(shape, s.dtype) for shape, s in zip(land_shapes, srcs)]
    outs = pl.pallas_call(
        body, name=name, in_specs=[HBM] * (2 * n), out_specs=[SEM, SEM] + [HBM] * (2 * n) + [VMEM_SPEC],
        out_shape=[pltpu.SemaphoreType.DMA((n,)), pltpu.SemaphoreType.DMA((n,))]
        + [pltpu.HBM(s.shape, s.dtype) for s in srcs] + [pltpu.HBM(l.shape, l.dtype) for l in lands] + [TOKEN],
        input_output_aliases={i: 2 + i for i in range(2 * n)},
        compiler_params=pltpu.CompilerParams(has_side_effects=EFFECT),
    )(*[_in_hbm(s) for s in srcs], *[_in_hbm(l) for l in lands])
    return outs[0], outs[1], outs[2:2 + n], outs[2 + n:2 + 2 * n], outs[-1]


def pair_wait_one(name, send_sems, recv_sems, src, land, after, index):
    def body(src_ref, land_ref, send, recv, after_ref, src_out, land_out):
        x, y, c, _ = _place()
        cp = _remote(src_ref, land_ref, send.at[index], recv.at[index], (x, y, 1 - c))
        cp.wait_send()
        cp.wait_recv()

    return pl.pallas_call(
        body, name=name, in_specs=[HBM, HBM, SEM, SEM, ANY], out_specs=[HBM, HBM],
        out_shape=[pltpu.HBM(src.shape, src.dtype), pltpu.HBM(land.shape, land.dtype)],
        input_output_aliases={0: 0, 1: 1},
        compiler_params=pltpu.CompilerParams(has_side_effects=EFFECT),
    )(src, land, send_sems, recv_sems, after)


def pair_wait(name, make_copies, send_sems, recv_sems, srcs, lands, after):
    n = len(srcs)

    def body(*refs):
        src_refs, land_refs = refs[:n], refs[n:2 * n]
        for cp in make_copies(src_refs, land_refs, refs[2 * n], refs[2 * n + 1]):
            cp.wait_send()
            cp.wait_recv()

    outs = pl.pallas_call(
        body, name=name, in_specs=[HBM] * (2 * n) + [SEM, SEM, ANY], out_specs=[HBM] * (2 * n),
        out_shape=[pltpu.HBM(s.shape, s.dtype) for s in srcs] + [pltpu.HBM(l.shape, l.dtype) for l in lands],
        input_output_aliases={i: i for i in range(2 * n)},
        compiler_params=pltpu.CompilerParams(has_side_effects=EFFECT),
    )(*srcs, *lands, send_sems, recv_sems, after)
    return outs[:n], outs[n:]


def _chip_copies(parts, lands, send_sems, recv_sems):
    x, y, c, chips = _place()
    return [_remote(parts[k].at[2 * px + py], lands[k].at[t], send_sems.at[3 * k + t], recv_sems.at[3 * k + t], (px, py, c))
            for k in range(len(parts)) for t, (px, py) in enumerate(chips)]


def chip_parts_start(parts):
    n = len(parts)

    def body(*refs):
        srcs, lands_in = refs[:n], refs[n:2 * n]
        send_sems, recv_sems = refs[2 * n], refs[2 * n + 1]
        token = refs[-1]
        for cp in _chip_copies(srcs, lands_in, send_sems, recv_sems):
            cp.start()
        token[...] = jnp.zeros_like(token)

    lands = [lax.empty((3,) + p.shape[1:], p.dtype) for p in parts]
    outs = pl.pallas_call(
        body, name="chip_parts_start", in_specs=[HBM] * (2 * n), out_specs=[SEM, SEM] + [HBM] * (2 * n) + [VMEM_SPEC],
        out_shape=[pltpu.SemaphoreType.DMA((3 * n,)), pltpu.SemaphoreType.DMA((3 * n,))]
        + [pltpu.HBM(p.shape, p.dtype) for p in parts] + [pltpu.HBM(l.shape, l.dtype) for l in lands] + [TOKEN],
        input_output_aliases={i: 2 + i for i in range(2 * n)},
        compiler_params=pltpu.CompilerParams(has_side_effects=EFFECT),
    )(*[_in_hbm(p) for p in parts], *[_in_hbm(l) for l in lands])
    return outs[0], outs[1], outs[2:2 + n], outs[2 + n:2 + 2 * n], outs[-1]


def chip_parts_wait(send_sems, recv_sems, parts, lands, after):
    n = len(parts)

    def body(*refs):
        srcs, lands_in = refs[:n], refs[n:2 * n]
        send, recv = refs[2 * n], refs[2 * n + 1]
        for cp in _chip_copies(srcs, lands_in, send, recv):
            cp.wait_send()
            cp.wait_recv()

    outs = pl.pallas_call(
        body, name="chip_parts_wait", in_specs=[HBM] * (2 * n) + [SEM, SEM, ANY], out_specs=[HBM] * (2 * n),
        out_shape=[pltpu.HBM(p.shape, p.dtype) for p in parts] + [pltpu.HBM(l.shape, l.dtype) for l in lands],
        input_output_aliases={i: i for i in range(2 * n)},
        compiler_params=pltpu.CompilerParams(has_side_effects=EFFECT),
    )(*parts, *lands, send_sems, recv_sems, after)
    return outs[:n], outs[n:]


def allreduce_small(packed):
    R = packed.shape[0]

    def body(x_ref, sum_ref, all_ref, send_sems, recv_sems):
        x, y, c, chips = _place()
        me, sibling = (x, y, c), (x, y, 1 - c)

        def rows(px, py, pc):
            return all_ref.at[4 * px + 2 * py + pc]

        def copy(k, block, to, src=None):
            return _remote(rows(*block) if src is None else src, rows(*block), send_sems.at[k], recv_sems.at[k], to)

        all_ref[4 * x + 2 * y + c] = x_ref[...]
        first = [copy(0, me, sibling, src=x_ref)]
        first += [copy(1 + j, me, (*chip, c), src=x_ref) for j, chip in enumerate(chips)]
        for cp in first:
            cp.start()
        passed = [copy(4 + j, (*chip, c), sibling) for j, chip in enumerate(chips)]
        for j, chip in enumerate(chips):
            copy(1 + j, (*chip, c), me).wait_recv()
            passed[j].start()
        copy(0, sibling, me).wait_recv()
        for j, chip in enumerate(chips):
            copy(4 + j, (*chip, 1 - c), me).wait_recv()
        for cp in first + passed:
            cp.wait_send()

        def chunk(i, carry):
            rws = pl.ds(pl.multiple_of(i * PACK_ROWS, PACK_ROWS), PACK_ROWS)
            acc = all_ref[0, rws, :]
            for d in range(1, N_DEV):
                acc = acc + all_ref[d, rws, :]
            sum_ref[rws, :] = acc
            return carry

        lax.fori_loop(0, R // PACK_ROWS, chunk, 0)

    return pl.pallas_call(
        body, name="allreduce_small", in_specs=[VMEM_SPEC], out_specs=VMEM_SPEC,
        out_shape=jax.ShapeDtypeStruct((R, LANES), F32),
        scratch_shapes=[pltpu.VMEM((N_DEV, R, LANES), F32), pltpu.SemaphoreType.DMA((7,)), pltpu.SemaphoreType.DMA((7,))],
        compiler_params=pltpu.CompilerParams(vmem_limit_bytes=VMEM_LIMIT),
    )(packed)


def _mixer_fwd(x, h, p, tabs, token):
    z = mm_nn_cols(h, p["w_in"], token)
    qr, kp, vp = rope_fwd(z, tabs)
    attn = attn_fwd(qr, kp, vp, p["sink3"])
    c1 = conv_dw_fwd(z, p["conv_w32"], p["conv_dw_b"])
    conv = conv_ln_fwd(c1, p["conv_ln_g"], p["conv_ln_b"])
    sgu = sgu_fwd(z, p["sgu_ln_g"], p["sgu_ln_b"], p["sgu_w16"], p["sgu_b3"])
    mix = jnp.concatenate([attn, conv, sgu], axis=1)
    x_mid, h2 = mm_nn_rows_res(mix, p["w_out"], x, p["ffn_norm_g"])
    return x_mid, h2, dict(x=x, h=h, z=z, qr=qr, kp=kp, vp=vp, c1=c1, mix=mix, x_mid=x_mid)


def _ffn_fwd(x_mid, h2, p, next_gain, token):
    gate, up, act = ffn_up(h2, p["w_gate"], p["w_up"], token)
    x_out, h_next = mm_nn_rows_res(act, p["w_down"], x_mid, next_gain)
    return x_out, h_next, dict(h2=h2, gate=gate, up=up, act=act)


def _layer_fwd(x, h, p, next_gain, tabs, token):
    x_mid, h2, s_mix = _mixer_fwd(x, h, p, tabs, token)
    x_out, h_next, s_ffn = _ffn_fwd(x_mid, h2, p, next_gain, token)
    return x_out, h_next, {**s_mix, **s_ffn}


def _ffn_bwd(dx, dxb, p, s, token):
    dgate, dup = ffn_down_bwd(dxb, p["w_down"], s["gate"], s["up"], token)
    g_down = mm_tn_rows(s["act"], dxb)
    dh2 = mm_nt_cols([(dgate, p["w_gate"]), (dup, p["w_up"])], BF16)
    g_gate = mm_tn_cols(s["h2"], dgate, N_CHIPS)
    g_up = mm_tn_cols(s["h2"], dup, N_CHIPS)
    dmid, dmidb, g_ffn_norm = rms_bwd(s["x_mid"], p["ffn_norm_g"], dh2, dx)
    return dmid, dmidb, [g_gate, g_up, g_down.reshape(N_CHIPS, -1, D_MODEL)], g_ffn_norm


def _mixer_bwd(dmid, dmidb, p, s, tabs, token):
    dmix = mm_nt_rows(dmidb, p["w_out"], token)
    g_out = mm_tn_rows(s["mix"], dmidb)
    dq, dkp, dvp, dsink = attn_bwd(s["qr"], s["kp"], s["vp"], p["sink3"], dmix)
    dqkv = rope_bwd(dq, dkp, dvp, tabs)
    dc1, g_cln_g, g_cln_b = conv_ln_bwd(dmix, s["c1"], p["conv_ln_g"], p["conv_ln_b"])
    dca, dcg, g_cw, g_cb = conv_dw_bwd(dc1, s["z"], p["conv_w32"])
    du, dv, g_sw, g_sb, g_sln_g, g_sln_b = sgu_bwd(s["z"], dmix, p["sgu_ln_g"], p["sgu_ln_b"], p["sgu_w16"], p["sgu_b3"])
    dz = jnp.concatenate([dqkv, dca, dcg, du, dv], axis=1)
    dh = mm_nt_cols([(dz, p["w_in"])], BF16)
    g_in = mm_tn_cols(s["h"], dz, N_CHIPS)
    dx_in, dxb_in, g_mix_norm = rms_bwd(s["x"], p["mix_norm_g"], dh, dmid)
    small = dict(mix_norm_g=g_mix_norm, sink=dsink[:, :, 0].reshape(1, N_Q_HEADS), conv_dw_w=g_cw[:CONV_KERNEL],
                 conv_dw_b=g_cb, conv_ln_g=g_cln_g, conv_ln_b=g_cln_b, sgu_ln_g=g_sln_g, sgu_ln_b=g_sln_b,
                 sgu_w=g_sw, sgu_b=g_sb[:, :, 0])
    return dx_in, dxb_in, [g_in, g_out.reshape(N_CHIPS, -1, D_MODEL)], small


def _layer_bwd(dx, dxb, p, s, tabs, token):
    dmid, dmidb, ffn_big, g_ffn_norm = _ffn_bwd(dx, dxb, p, s, token)
    dx_in, dxb_in, mix_big, small = _mixer_bwd(dmid, dmidb, p, s, tabs, token)
    return dx_in, dxb_in, mix_big + ffn_big, dict(small, ffn_norm_g=g_ffn_norm)


def _mixer_weights(gathered):
    w_in, w_out = gathered
    return dict(w_in=w_in, w_out=w_out.reshape(-1, D_MODEL))


def _ffn_weights(gathered):
    w_gate, w_up, w_down = gathered
    return dict(w_gate=w_gate, w_up=w_up, w_down=w_down.reshape(-1, D_MODEL))


def _small_params(l, conv_w_full, mix_norm_g, sink, conv_dw_b, conv_ln_g, conv_ln_b, sgu_ln_g, sgu_ln_b, sgu_w, sgu_b,
                  ffn_norm_g):
    return dict(
        mix_norm_g=mix_norm_g[l:l + 1], ffn_norm_g=ffn_norm_g[l:l + 1],
        sink3=jnp.broadcast_to(sink[l].reshape(N_KV_HEADS, Q_PER_KV, 1), (N_KV_HEADS, Q_PER_KV, LANES)),
        conv_w32=jnp.pad(conv_w_full[l], ((0, 32 - CONV_KERNEL), (0, 0))),
        conv_dw_b=conv_dw_b[l:l + 1], conv_ln_g=conv_ln_g[l:l + 1], conv_ln_b=conv_ln_b[l:l + 1],
        sgu_ln_g=sgu_ln_g[l:l + 1], sgu_ln_b=sgu_ln_b[l:l + 1], sgu_w16=sgu_w[l].astype(BF16),
        sgu_b3=jnp.broadcast_to(sgu_b[l][:, :, None], (SGU_HEADS, CHUNK, CHUNK)))


_SMALL = ["mix_norm_g", "sink", "conv_dw_b", "conv_ln_g", "conv_ln_b", "sgu_ln_g", "sgu_ln_b", "sgu_w", "sgu_b", "ffn_norm_g",
          "final_norm_g"]


def _pack_rows(arrays):
    rows, counts = [], []
    for a in arrays:
        flat = a.reshape(-1)
        n = -(-flat.shape[0] // LANES)
        rows.append(jnp.pad(flat, (0, n * LANES - flat.shape[0])).reshape(n, LANES))
        counts.append(n)
    packed = jnp.concatenate(rows, axis=0)
    pad = -packed.shape[0] % PACK_ROWS
    return jnp.pad(packed, ((0, pad), (0, 0))), counts


def _unpack_rows(packed, counts, shapes):
    out, r = [], 0
    for n, shape in zip(counts, shapes):
        size = math.prod(shape)
        out.append(packed[r:r + n].reshape(-1)[:size].reshape(shape))
        r += n
    return out


def kernel(x, mix_norm_g, w_in, sink, conv_dw_w, conv_dw_b, conv_ln_g, conv_ln_b, sgu_ln_g, sgu_ln_b, sgu_w, sgu_b, w_out, ffn_norm_g, w_gate, w_up, w_down, final_norm_g, loss_target, m_mix_norm_g, m_w_in, m_sink, m_conv_dw_w, m_conv_dw_b, m_conv_ln_g, m_conv_ln_b, m_sgu_ln_g, m_sgu_ln_b, m_sgu_w, m_sgu_b, m_w_out, m_ffn_norm_g, m_w_gate, m_w_up, m_w_down, m_final_norm_g, v_mix_norm_g, v_w_in, v_sink, v_conv_dw_w, v_conv_dw_b, v_conv_ln_g, v_conv_ln_b, v_sgu_ln_g, v_sgu_ln_b, v_sgu_w, v_sgu_b, v_w_out, v_ffn_norm_g, v_w_gate, v_w_up, v_w_down, v_final_norm_g):
    S = x.shape[1]
    my_chip = 2 * lax.axis_index("x") + lax.axis_index("y")
    c_idx = lax.axis_index("c").astype(jnp.int32).reshape(1)
    big_w = [w_in, w_out, w_gate, w_up, w_down]
    big_m = [m_w_in, m_w_out, m_w_gate, m_w_up, m_w_down]
    big_v = [v_w_in, v_w_out, v_w_gate, v_w_up, v_w_down]
    n_kinds = len(big_w)

    x_idx = lax.axis_index("x").astype(jnp.int32).reshape(1)
    y_idx = lax.axis_index("y").astype(jnp.int32).reshape(1)
    conv_w_all = gather_small(conv_dw_w)
    conv_w_full = jnp.transpose(conv_w_all, (1, 2, 0, 3)).reshape(DEPTH, CONV_KERNEL, CONV_WIDTH)
    tabs = rope_tables(S)
    no_token = jnp.zeros(TOKEN.shape, TOKEN.dtype)

    mixer_kinds, ffn_kinds = [0, 1], [2, 3, 4]
    shards = [[w[l].astype(BF16) for w in big_w] for l in range(DEPTH)]

    def fetch(pending, after):
        send_sems, recv_sems, srcs, lands, _ = pending
        return forward_halves(gather_wait(send_sems, recv_sems, srcs, lands, after))

    first_mixer = gather_start([shards[0][k] for k in mixer_kinds], conv_w_all)
    first_ffn = gather_start([shards[0][k] for k in ffn_kinds], first_mixer[4])
    act = x[0]
    h = rms_fwd(act, mix_norm_g[0:1], no_token)
    saved, params = [], []
    for l in range(DEPTH):
        p = _small_params(l, conv_w_full, mix_norm_g, sink, conv_dw_b, conv_ln_g, conv_ln_b, sgu_ln_g, sgu_ln_b, sgu_w, sgu_b,
                          ffn_norm_g)
        next_gain = mix_norm_g[l + 1:l + 2] if l + 1 < DEPTH else final_norm_g.reshape(1, D_MODEL)
        if l == 0:
            p.update(_mixer_weights(fetch(first_mixer, act)))
            x_mid, h2, s_mix = _mixer_fwd(act, h, p, tabs, first_ffn[4])
            ffn_w = fetch(first_ffn, x_mid)
        else:
            gathered = fetch(pending, act)
            p.update(_mixer_weights(gathered[:2]))
            ffn_w = gathered[2:]
        token = no_token
        if l + 1 < DEPTH:
            pending = gather_start(shards[l + 1], ffn_w[0])
            token = pending[4]
        p.update(_ffn_weights(ffn_w))
        if l > 0:
            x_mid, h2, s_mix = _mixer_fwd(act, h, p, tabs, token)
        act, h, s_ffn = _ffn_fwd(x_mid, h2, p, next_gain, token)
        params.append(p)
        saved.append({**s_mix, **s_ffn})
    loss_part, dx, dxb, g_final = final_loss(act, final_norm_g.reshape(1, D_MODEL), loss_target[0])
    loss = lax.psum(loss_part[0, 0], ("x", "y", "c"))

    halves = [jnp.zeros((DEPTH, w.shape[1] // 2, w.shape[2]), F32) for w in big_w]
    small_grads = [None] * DEPTH

    def chip_start(layer, kinds, grads, recv):
        chip_sum = [add_sibling_half(g, r, c_idx) for g, r in zip(grads, recv)]
        send_sems, recv_sems, parts, lands, token = chip_parts_start(chip_sum)
        return (layer, kinds, send_sems, recv_sems, parts, lands), token

    def reduce_start(layer, kinds, grads):
        return chip_start(layer, kinds, grads, exchange_sibling_halves(grads))

    def reduce_finish(pending, halves, after):
        layer, kinds, send_sems, recv_sems, parts, lands = pending
        parts, others = chip_parts_wait(send_sems, recv_sems, parts, lands, after)
        halves = list(halves)
        for i, k in enumerate(kinds):
            halves[k] = sum_chips(parts[i], others[i], halves[k], x_idx, y_idx, layer)
        return halves

    pending, token = None, no_token
    for l in reversed(range(DEPTH)):
        dmid, dmidb, ffn_big, g_ffn_norm = _ffn_bwd(dx, dxb, params[l], saved[l], token)
        if l == 0:
            last_ffn, mixer_token = reduce_start(l, ffn_kinds, ffn_big)
        else:
            half_shapes = [(g.shape[0], g.shape[1] // 2, g.shape[2]) for g in ffn_big]
            sib_send, sib_recv, ffn_big, ffn_lands, mixer_token = pair_start("sibling_start", _sibling_half_copies, ffn_big, half_shapes)
        dx, dxb, mix_big, small = _mixer_bwd(dmid, dmidb, params[l], saved[l], tabs, mixer_token)
        small_grads[l] = dict(small, ffn_norm_g=g_ffn_norm)
        if pending is not None:
            halves = reduce_finish(pending, halves, dx)
        if l == 0:
            last_mixer, token = reduce_start(l, mixer_kinds, mix_big)
            halves = reduce_finish(last_ffn, halves, token)
            halves = reduce_finish(last_mixer, halves, halves[ffn_kinds[0]])
        else:
            ffn_big, ffn_recv = pair_wait("sibling_wait", _sibling_half_copies, sib_send, sib_recv, ffn_big, ffn_lands, dx)
            mix_recv = exchange_sibling_halves(mix_big)
            pending, token = chip_start(l, mixer_kinds + ffn_kinds, list(mix_big) + list(ffn_big), list(mix_recv) + list(ffn_recv))

    fin_send, fin_recv, halves, fin_lands, _ = pair_start("final_start", _sibling_whole_copies, halves, [h.shape for h in halves])

    stacked = {n: jnp.stack([small_grads[l][n] for l in range(DEPTH)]) for n in small_grads[0]}
    stacked["final_norm_g"] = g_final
    packed, counts = _pack_rows([stacked[n] for n in _SMALL] + [stacked["conv_dw_w"]])
    reduced = allreduce_small(packed)
    small_w = dict(mix_norm_g=mix_norm_g, sink=sink, conv_dw_b=conv_dw_b, conv_ln_g=conv_ln_g, conv_ln_b=conv_ln_b,
                   sgu_ln_g=sgu_ln_g, sgu_ln_b=sgu_ln_b, sgu_w=sgu_w, sgu_b=sgu_b, ffn_norm_g=ffn_norm_g,
                   final_norm_g=final_norm_g)
    small_m = dict(mix_norm_g=m_mix_norm_g, sink=m_sink, conv_dw_b=m_conv_dw_b, conv_ln_g=m_conv_ln_g,
                   conv_ln_b=m_conv_ln_b, sgu_ln_g=m_sgu_ln_g, sgu_ln_b=m_sgu_ln_b, sgu_w=m_sgu_w, sgu_b=m_sgu_b,
                   ffn_norm_g=m_ffn_norm_g, final_norm_g=m_final_norm_g)
    small_v = dict(mix_norm_g=v_mix_norm_g, sink=v_sink, conv_dw_b=v_conv_dw_b, conv_ln_g=v_conv_ln_g,
                   conv_ln_b=v_conv_ln_b, sgu_ln_g=v_sgu_ln_g, sgu_ln_b=v_sgu_ln_b, sgu_w=v_sgu_w, sgu_b=v_sgu_b,
                   ffn_norm_g=v_ffn_norm_g, final_norm_g=v_final_norm_g)
    shapes = [small_w[n].shape for n in _SMALL] + [(DEPTH, CONV_KERNEL, CONV_WIDTH)]
    red = _unpack_rows(reduced, counts, shapes)
    g_small = dict(zip(_SMALL, red[:-1]))
    g_small["conv_dw_w"] = lax.dynamic_slice_in_dim(red[-1], my_chip * LANES, LANES, axis=2)
    small_w["conv_dw_w"], small_m["conv_dw_w"], small_v["conv_dw_w"] = conv_dw_w, m_conv_dw_w, v_conv_dw_w
    names = _SMALL + ["conv_dw_w"]
    pw, cnt = _pack_rows([small_w[n] for n in names])
    pg, _ = _pack_rows([g_small[n] for n in names])
    pm, _ = _pack_rows([small_m[n] for n in names])
    pv, _ = _pack_rows([small_v[n] for n in names])
    sd, sm, sv = adamw(pw, pg, pm, pv)
    shp = [small_w[n].shape for n in names]
    d_small = dict(zip(names, _unpack_rows(sd, cnt, shp)))
    m_small = dict(zip(names, _unpack_rows(sm, cnt, shp)))
    v_small = dict(zip(names, _unpack_rows(sv, cnt, shp)))

    big_names = ["w_in", "w_out", "w_gate", "w_up", "w_down"]
    g_big, d_big, m_big, v_big = {}, {}, {}, {}
    after = sd
    for k, n in enumerate(big_names):
        mine, theirs = pair_wait_one("final_wait", fin_send, fin_recv, halves[k], fin_lands[k], after, k)
        g_big[n], d_big[n], m_big[n], v_big[n] = adamw_halves(big_w[k], mine, theirs, big_m[k], big_v[k], c_idx)
        after = d_big[n]

    order = ["mix_norm_g", "w_in", "sink", "conv_dw_w", "conv_dw_b", "conv_ln_g", "conv_ln_b", "sgu_ln_g", "sgu_ln_b",
             "sgu_w", "sgu_b", "w_out", "ffn_norm_g", "w_gate", "w_up", "w_down", "final_norm_g"]
    grads = {**g_small, **g_big}
    deltas = {**d_small, **d_big}
    new_m = {**m_small, **m_big}
    new_v = {**v_small, **v_big}
    return (loss, dx[None], *[grads[n] for n in order], *[deltas[n] for n in order],
            *[new_m[n] for n in order], *[new_v[n] for n in order])
```

```python
import functools
import math

import jax
import jax.numpy as jnp
from jax import lax
from jax.experimental import pallas as pl
from jax.experimental.pallas import tpu as pltpu

F32, BF16 = jnp.float32, jnp.bfloat16

D_MODEL = 2048
DEPTH = 4
HEAD_DIM = 128
N_Q_HEADS = 8
N_KV_HEADS = 2
Q_PER_KV = N_Q_HEADS // N_KV_HEADS
ATTN_WIDTH = N_Q_HEADS * HEAD_DIM
KV_WIDTH = N_KV_HEADS * HEAD_DIM
CONV_WIDTH = 512
CONV_KERNEL = 31
CONV_PAD = 16
SGU_WIDTH = 512
SGU_HEADS = 4
CHUNK = 128
IN_WIDTH = 3584
D_FF = 5632
WINDOW = 128
ROT_DIM = 32
ROPE_THETA = 500000.0
EPS = 1e-6
N_CHIPS = 4
N_DEV = 8
LANES = 128
PACK_ROWS = 64
OFF_K = ATTN_WIDTH
OFF_V = OFF_K + KV_WIDTH
OFF_CA = OFF_V + KV_WIDTH
OFF_CG = OFF_CA + CONV_WIDTH
OFF_U = OFF_CG + CONV_WIDTH
OFF_VV = OFF_U + SGU_WIDTH

ADAM_LR, ADAM_B1, ADAM_B2, ADAM_EPS, ADAM_WD, ADAM_STEP = 0.001, 0.9, 0.999, 1e-08, 0.01, 10

VMEM_LIMIT = 56 * 1024 * 1024
MESH = pl.DeviceIdType.MESH
HBM = pl.BlockSpec(memory_space=pltpu.HBM)
VMEM_SPEC = pl.BlockSpec(memory_space=pltpu.VMEM)


def _call(name, body, *, grid, in_specs, out_specs, out_shape, scratch=(), sem=None, aliases=None):
    params = dict(vmem_limit_bytes=VMEM_LIMIT)
    if sem is not None:
        params["dimension_semantics"] = sem
    return pl.pallas_call(
        body, name=name, grid=grid, in_specs=in_specs, out_specs=out_specs, out_shape=out_shape,
        scratch_shapes=list(scratch), input_output_aliases=aliases or {}, compiler_params=pltpu.CompilerParams(**params))


def _sigmoid(x):
    return 1.0 / (1.0 + jnp.exp(-x))


def rms_fwd(x, g, token):
    S = x.shape[0]
    tm = min(512, S)

    def body(x_ref, g_ref, token_ref, o_ref):
        xv = x_ref[...]
        r = lax.rsqrt(jnp.mean(xv * xv, axis=-1, keepdims=True) + EPS)
        o_ref[...] = (xv * r * g_ref[...]).astype(BF16)

    return _call("rms_fwd", body, grid=(S // tm,),
                 in_specs=[pl.BlockSpec((tm, D_MODEL), lambda i: (i, 0)), pl.BlockSpec((1, D_MODEL), lambda i: (0, 0)),
                           pl.BlockSpec((8, LANES), lambda i: (0, 0))],
                 out_specs=pl.BlockSpec((tm, D_MODEL), lambda i: (i, 0)),
                 out_shape=jax.ShapeDtypeStruct((S, D_MODEL), BF16), sem=("parallel",))(x, g, token)


def _rms_bwd_math(xv, gv, dh):
    r = lax.rsqrt(jnp.mean(xv * xv, axis=-1, keepdims=True) + EPS)
    n = xv * r
    dn = dh * gv
    dx = r * (dn - n * jnp.mean(dn * n, axis=-1, keepdims=True))
    dg = jnp.sum(dh * n, axis=0, keepdims=True)
    return dx, dg


def rms_bwd(x, g, dh, dres):
    S = x.shape[0]
    tm = min(512, S)

    def body(x_ref, g_ref, dh_ref, dres_ref, dx_ref, dxb_ref, dg_ref):
        dx, dg = _rms_bwd_math(x_ref[...], g_ref[...], dh_ref[...].astype(F32))
        dx = dx + dres_ref[...]
        dx_ref[...] = dx
        dxb_ref[...] = dx.astype(BF16)

        @pl.when(pl.program_id(0) == 0)
        def _():
            dg_ref[...] = dg

        @pl.when(pl.program_id(0) > 0)
        def _():
            dg_ref[...] += dg

    row = pl.BlockSpec((tm, D_MODEL), lambda i: (i, 0))
    vec = pl.BlockSpec((1, D_MODEL), lambda i: (0, 0))
    return _call("rms_bwd", body, grid=(S // tm,), in_specs=[row, vec, row, row], out_specs=[row, row, vec],
                 out_shape=[jax.ShapeDtypeStruct((S, D_MODEL), F32), jax.ShapeDtypeStruct((S, D_MODEL), BF16),
                            jax.ShapeDtypeStruct((1, D_MODEL), F32)], sem=("arbitrary",))(x, g, dh, dres)


def final_loss(x, g, target):
    S = x.shape[0]
    tm = min(256, S)

    def body(x_ref, g_ref, t_ref, loss_ref, dx_ref, dxb_ref, dg_ref):
        xv = x_ref[...]
        gv = g_ref[...]
        r = lax.rsqrt(jnp.mean(xv * xv, axis=-1, keepdims=True) + EPS)
        err = xv * r * gv - t_ref[...]
        part = 0.5 * jnp.sum(jnp.mean(err * err, axis=-1, keepdims=True), axis=0, keepdims=True)
        dx, dg = _rms_bwd_math(xv, gv, err * (1.0 / D_MODEL))
        dx_ref[...] = dx
        dxb_ref[...] = dx.astype(BF16)

        @pl.when(pl.program_id(0) == 0)
        def _():
            dg_ref[...] = dg
            loss_ref[...] = part

        @pl.when(pl.program_id(0) > 0)
        def _():
            dg_ref[...] += dg
            loss_ref[...] += part

    row = pl.BlockSpec((tm, D_MODEL), lambda i: (i, 0))
    vec = pl.BlockSpec((1, D_MODEL), lambda i: (0, 0))
    one = pl.BlockSpec((1, 1), lambda i: (0, 0))
    return _call("final_loss", body, grid=(S // tm,), in_specs=[row, vec, row], out_specs=[one, row, row, vec],
                 out_shape=[jax.ShapeDtypeStruct((1, 1), F32), jax.ShapeDtypeStruct((S, D_MODEL), F32),
                            jax.ShapeDtypeStruct((S, D_MODEL), BF16), jax.ShapeDtypeStruct((1, D_MODEL), F32)],
                 sem=("arbitrary",))(x, g, target)


EPILOGUE_ROWS = 256
NN = (((1,), (0,)), ((), ()))
NT = (((1,), (1,)), ((), ()))
TN = (((0,), (0,)), ((), ()))


def _matmul(name, operands, in_specs, out_shape, out_specs, grid, pairs, dims, acc_shape, epilogue):
    n_in, n_out, nk = len(operands), len(out_shape), grid[-1]

    def body(*refs):
        ins, outs = refs[:n_in], refs[n_in:n_in + n_out]
        part = None
        for ia, ib in pairs:
            d = lax.dot_general(ins[ia][...], ins[ib][...], dims, preferred_element_type=F32)
            part = d if part is None else part + d
        if nk == 1:
            epilogue(part, ins, outs)
        else:
            acc = refs[-1]
            k = pl.program_id(len(grid) - 1)

            @pl.when(k == 0)
            def _():
                acc[...] = part

            @pl.when(k > 0)
            def _():
                acc[...] += part

            @pl.when(k == nk - 1)
            def _():
                epilogue(acc[...], ins, outs)

    scratch = [pltpu.VMEM(acc_shape, F32)] if nk > 1 else []
    sem = ("parallel",) * (len(grid) - 1) + ("arbitrary",)
    return _call(name, body, grid=grid, in_specs=in_specs, out_specs=out_specs, out_shape=out_shape,
                 scratch=scratch, sem=sem)(*operands)


def _store(dtype):
    def epilogue(acc, ins, outs):
        outs[0][...] = acc.astype(dtype)
    return epilogue


def mm_nn_cols(a, w, token):
    S, K = a.shape
    J, _, Ns = w.shape
    tm = min(512, S)
    return _matmul("mm_nn_cols", (a, w, token),
                   [pl.BlockSpec((tm, K), lambda j, i, k: (i, 0)), pl.BlockSpec((None, K, Ns), lambda j, i, k: (j, 0, 0)),
                    pl.BlockSpec((8, LANES), lambda j, i, k: (0, 0))],
                   [jax.ShapeDtypeStruct((S, J * Ns), BF16)], [pl.BlockSpec((tm, Ns), lambda j, i, k: (i, j))],
                   (J, S // tm, 1), [(0, 1)], NN, None, _store(BF16))[0]


def ffn_up(h, wg, wu, token):
    S, K = h.shape
    J, _, Ns = wg.shape
    tm = min(512, S)

    sub = min(EPILOGUE_ROWS, tm)

    def body(h_ref, wg_ref, wu_ref, token_ref, g_ref, u_ref, a_ref):
        for r in range(tm // sub):
            rows = slice(r * sub, (r + 1) * sub)
            hv = h_ref[rows, :]
            gv = jnp.dot(hv, wg_ref[...], preferred_element_type=F32)
            uv = jnp.dot(hv, wu_ref[...], preferred_element_type=F32)
            g_ref[rows, :] = gv.astype(BF16)
            u_ref[rows, :] = uv.astype(BF16)
            a_ref[rows, :] = (gv * _sigmoid(gv) * uv).astype(BF16)

    wspec = pl.BlockSpec((None, K, Ns), lambda j, i: (j, 0, 0))
    ospec = pl.BlockSpec((tm, Ns), lambda j, i: (i, j))
    oshape = jax.ShapeDtypeStruct((S, J * Ns), BF16)
    return _call("ffn_up", body, grid=(J, S // tm),
                 in_specs=[pl.BlockSpec((tm, K), lambda j, i: (i, 0)), wspec, wspec, pl.BlockSpec((8, LANES), lambda j, i: (0, 0))],
                 out_specs=[ospec, ospec, ospec], out_shape=[oshape, oshape, oshape],
                 sem=("parallel", "parallel"))(h, wg, wu, token)


def mm_nn_rows_res(a, w, res, gain):
    S, K = a.shape
    N = w.shape[1]
    tm = min(512, S)
    tk = K if K <= 2048 else K // 4

    def body(a_ref, w_ref, res_ref, g_ref, x_ref, h_ref, *acc):
        def finish(total):
            xv = total + res_ref[...]
            x_ref[...] = xv
            r = lax.rsqrt(jnp.mean(xv * xv, axis=-1, keepdims=True) + EPS)
            h_ref[...] = (xv * r * g_ref[...]).astype(BF16)

        part = jnp.dot(a_ref[...], w_ref[...], preferred_element_type=F32)
        if K == tk:
            finish(part)
        else:
            k = pl.program_id(1)

            @pl.when(k == 0)
            def _():
                acc[0][...] = part

            @pl.when(k > 0)
            def _():
                acc[0][...] += part

            @pl.when(k == K // tk - 1)
            def _():
                finish(acc[0][...])

    row = pl.BlockSpec((tm, N), lambda i, k: (i, 0))
    return _call("mm_nn_rows_res", body, grid=(S // tm, K // tk),
                 in_specs=[pl.BlockSpec((tm, tk), lambda i, k: (i, k)), pl.BlockSpec((tk, N), lambda i, k: (k, 0)), row,
                           pl.BlockSpec((1, N), lambda i, k: (0, 0))],
                 out_specs=[row, row], out_shape=[jax.ShapeDtypeStruct((S, N), F32), jax.ShapeDtypeStruct((S, N), BF16)],
                 scratch=[pltpu.VMEM((tm, N), F32)] if K != tk else [], sem=("parallel", "arbitrary"))(a, w, res, gain)


def mm_nt_cols(pairs_in, out_dtype, shards_per_step):
    dz0, w0 = pairs_in[0]
    S = dz0.shape[0]
    J, K, Ns = w0.shape
    tm = min(512, S)
    sps = shards_per_step
    operands, specs, pairs = [], [], []
    for dz, w in pairs_in:
        for s in range(sps):
            pairs.append((len(operands), len(operands) + 1))
            operands += [dz, w]
            specs += [pl.BlockSpec((tm, Ns), lambda i, j, s=s: (i, j * sps + s)),
                      pl.BlockSpec((None, K, Ns), lambda i, j, s=s: (j * sps + s, 0, 0))]
    return _matmul("mm_nt_cols%d" % len(pairs_in), tuple(operands), specs,
                   [jax.ShapeDtypeStruct((S, K), out_dtype)], [pl.BlockSpec((tm, K), lambda i, j: (i, 0))],
                   (S // tm, J // sps), pairs, NT, (tm, K), _store(out_dtype))[0]


def mm_nt_rows(dy, w, token):
    S, N = dy.shape
    K = w.shape[0]
    tm, tko = min(1024, S), 512
    return _matmul("mm_nt_rows", (dy, w, token),
                   [pl.BlockSpec((tm, N), lambda i, kk, z: (i, 0)), pl.BlockSpec((tko, N), lambda i, kk, z: (kk, 0)),
                    pl.BlockSpec((8, LANES), lambda i, kk, z: (0, 0))],
                   [jax.ShapeDtypeStruct((S, K), BF16)], [pl.BlockSpec((tm, tko), lambda i, kk, z: (i, kk))],
                   (S // tm, K // tko, 1), [(0, 1)], NT, None, _store(BF16))[0]


def ffn_down_bwd(dy, w, gate, up, token):
    S, N = dy.shape
    K = w.shape[0]
    tm, tko = min(1024, S), 512
    sub = min(EPILOGUE_ROWS, tm)

    def body(dy_ref, w_ref, g_ref, u_ref, token_ref, dg_ref, du_ref):
        for r in range(tm // sub):
            rows = slice(r * sub, (r + 1) * sub)
            dact = lax.dot_general(dy_ref[rows, :], w_ref[...], NT, preferred_element_type=F32)
            gv = g_ref[rows, :].astype(F32)
            uv = u_ref[rows, :].astype(F32)
            sg = _sigmoid(gv)
            dg_ref[rows, :] = (dact * uv * sg * (1.0 + gv * (1.0 - sg))).astype(BF16)
            du_ref[rows, :] = (dact * gv * sg).astype(BF16)

    tile = pl.BlockSpec((tm, tko), lambda i, kk: (i, kk))
    oshape = jax.ShapeDtypeStruct((S, K), BF16)
    return _call("ffn_down_bwd", body, grid=(S // tm, K // tko),
                 in_specs=[pl.BlockSpec((tm, N), lambda i, kk: (i, 0)), pl.BlockSpec((tko, N), lambda i, kk: (kk, 0)), tile, tile,
                           pl.BlockSpec((8, LANES), lambda i, kk: (0, 0))],
                 out_specs=[tile, tile], out_shape=[oshape, oshape], sem=("parallel", "parallel"))(dy, w, gate, up, token)


def mm_tn_cols(a, dz, J):
    S, M = a.shape
    Ns = dz.shape[1] // J
    tm, tk = 512, S
    return _matmul("mm_tn_cols", (a, dz),
                   [pl.BlockSpec((tk, tm), lambda j, m, k: (k, m)), pl.BlockSpec((tk, Ns), lambda j, m, k: (k, j))],
                   [jax.ShapeDtypeStruct((J, M, Ns), BF16)], [pl.BlockSpec((None, tm, Ns), lambda j, m, k: (j, m, 0))],
                   (J, M // tm, S // tk), [(0, 1)], TN, (tm, Ns), _store(BF16))[0]


def mm_tn_rows(a, dy):
    S, K = a.shape
    N = dy.shape[1]
    tm, tk = 512, min(2048, S)
    return _matmul("mm_tn_rows", (a, dy),
                   [pl.BlockSpec((tk, tm), lambda m, k: (k, m)), pl.BlockSpec((tk, N), lambda m, k: (k, 0))],
                   [jax.ShapeDtypeStruct((K, N), BF16)], [pl.BlockSpec((tm, N), lambda m, k: (m, 0))],
                   (K // tm, S // tk), [(0, 1)], TN, (tm, N), _store(BF16))[0]


def rope_tables(S):
    half = ROT_DIM // 2
    pos = jnp.arange(S, dtype=F32)
    inv = ROPE_THETA ** (-jnp.arange(0, ROT_DIM, 2, dtype=F32) / ROT_DIM)
    ang = pos[:, None] * inv[None, :]
    cos, sin = jnp.cos(ang), jnp.sin(ang)
    zeros = jnp.zeros((S, HEAD_DIM - ROT_DIM), F32)
    c = jnp.concatenate([cos, cos, jnp.ones((S, HEAD_DIM - ROT_DIM), F32)], axis=1)
    s_lo = jnp.concatenate([-sin, jnp.zeros((S, half), F32), zeros], axis=1)
    s_hi = jnp.concatenate([jnp.zeros((S, half), F32), sin, zeros], axis=1)
    return c, s_lo, s_hi


ROPE_ROWS = 512


def _rope(t, c, s_lo, s_hi):
    half = ROT_DIM // 2
    return t * c + pltpu.roll(t, HEAD_DIM - half, 1) * s_lo + pltpu.roll(t, half, 1) * s_hi


def _unrope(d, c, s_lo, s_hi):
    half = ROT_DIM // 2
    return d * c + pltpu.roll(d * s_lo, half, 1) + pltpu.roll(d * s_hi, HEAD_DIM - half, 1)


def rope_fwd(z, tabs):
    S = z.shape[0]
    T = min(ROPE_ROWS, S)

    def body(q_ref, kv_ref, c_ref, sl_ref, sh_ref, qr_ref, kp_ref, vp_ref):
        i = pl.program_id(0)

        @pl.when(i == 0)
        def _():
            zero = jnp.zeros((CHUNK, KV_WIDTH), BF16)
            kp_ref[0:CHUNK, :] = zero
            vp_ref[0:CHUNK, :] = zero
            kp_ref[S + CHUNK:S + 2 * CHUNK, :] = zero
            vp_ref[S + CHUNK:S + 2 * CHUNK, :] = zero

        c, sl, sh = c_ref[...], sl_ref[...], sh_ref[...]
        for h in range(N_Q_HEADS):
            cols = slice(h * HEAD_DIM, (h + 1) * HEAD_DIM)
            qr_ref[:, cols] = _rope(q_ref[:, cols].astype(F32), c, sl, sh).astype(BF16)
        rows = pl.ds(pl.multiple_of(CHUNK + i * T, CHUNK), T)
        for g in range(N_KV_HEADS):
            cols = slice(g * HEAD_DIM, (g + 1) * HEAD_DIM)
            kp_ref[rows, cols] = _rope(kv_ref[:, cols].astype(F32), c, sl, sh).astype(BF16)
        vp_ref[rows, :] = kv_ref[:, KV_WIDTH:2 * KV_WIDTH]

    tab = pl.BlockSpec((T, HEAD_DIM), lambda i: (i, 0))
    pad = pl.BlockSpec((S + 2 * CHUNK, KV_WIDTH), lambda i: (0, 0))
    return _call("rope_fwd", body, grid=(S // T,),
                 in_specs=[pl.BlockSpec((T, ATTN_WIDTH), lambda i: (i, 0)),
                           pl.BlockSpec((T, 2 * KV_WIDTH), lambda i: (i, OFF_K // (2 * KV_WIDTH))), tab, tab, tab],
                 out_specs=[pl.BlockSpec((T, ATTN_WIDTH), lambda i: (i, 0)), pad, pad],
                 out_shape=[jax.ShapeDtypeStruct((S, ATTN_WIDTH), BF16), jax.ShapeDtypeStruct((S + 2 * CHUNK, KV_WIDTH), BF16),
                            jax.ShapeDtypeStruct((S + 2 * CHUNK, KV_WIDTH), BF16)], sem=("arbitrary",))(z, z, *tabs)


def rope_bwd(dq, dkp, dvp, tabs):
    S = dq.shape[0]
    T = min(ROPE_ROWS, S)

    def body(dq_ref, dk_ref, dv_ref, c_ref, sl_ref, sh_ref, o_ref):
        c, sl, sh = c_ref[...], sl_ref[...], sh_ref[...]
        for h in range(N_Q_HEADS):
            cols = slice(h * HEAD_DIM, (h + 1) * HEAD_DIM)
            o_ref[:, cols] = _unrope(dq_ref[:, cols], c, sl, sh).astype(BF16)
        rows = pl.ds(pl.multiple_of(CHUNK + pl.program_id(0) * T, CHUNK), T)
        for g in range(N_KV_HEADS):
            cols = slice(g * HEAD_DIM, (g + 1) * HEAD_DIM)
            o_ref[:, OFF_K + g * HEAD_DIM:OFF_K + (g + 1) * HEAD_DIM] = _unrope(dk_ref[rows, cols], c, sl, sh).astype(BF16)
        o_ref[:, OFF_V:OFF_V + KV_WIDTH] = dv_ref[rows, :].astype(BF16)

    tab = pl.BlockSpec((T, HEAD_DIM), lambda i: (i, 0))
    pad = pl.BlockSpec((S + 2 * CHUNK, KV_WIDTH), lambda i: (0, 0))
    return _call("rope_bwd", body, grid=(S // T,),
                 in_specs=[pl.BlockSpec((T, ATTN_WIDTH), lambda i: (i, 0)), pad, pad, tab, tab, tab],
                 out_specs=pl.BlockSpec((T, OFF_CA), lambda i: (i, 0)),
                 out_shape=jax.ShapeDtypeStruct((S, OFF_CA), BF16), sem=("parallel",))(dq, dkp, dvp, *tabs)


STACK = Q_PER_KV * CHUNK


def _stack_heads(ref, rows):
    return jnp.concatenate([ref[rows, r * HEAD_DIM:(r + 1) * HEAD_DIM] for r in range(Q_PER_KV)], axis=0)


def _stack_sinks(s_ref):
    return jnp.concatenate([jnp.broadcast_to(s_ref[r:r + 1, 0:1], (CHUNK, 1)) for r in range(Q_PER_KV)], axis=0)


MASKED = -1e30


def _scores(q, kb):
    return lax.dot_general(q, kb, NT, preferred_element_type=F32) * (1.0 / math.sqrt(HEAD_DIM))


def _band_bias():
    row = lax.broadcasted_iota(jnp.int32, (STACK, 3 * CHUNK), 0) & (CHUNK - 1)
    col = lax.broadcasted_iota(jnp.int32, (STACK, 3 * CHUNK), 1)
    return jnp.where(jnp.abs(col - CHUNK - row) <= WINDOW, 0.0, MASKED).astype(F32)


def _edge_bias(n, S):
    kpos = (n - 1) * CHUNK + lax.broadcasted_iota(jnp.int32, (1, 3 * CHUNK), 1)
    return jnp.where((kpos >= 0) & (kpos < S), 0.0, MASKED).astype(F32)


def _softmax_sink(s, sk, bias):
    s = s + bias
    m = jnp.maximum(jnp.max(s, axis=1, keepdims=True), sk)
    e = jnp.exp(s - m)
    es = jnp.exp(sk - m)
    inv = 1.0 / (jnp.sum(e, axis=1, keepdims=True) + es)
    return e * inv, es * inv


def _block_views(i, nblk):
    ns = [i * nblk + b for b in range(nblk)]
    wins = [pl.ds(pl.multiple_of(n * CHUNK, CHUNK), 3 * CHUNK) for n in ns]
    rows = [slice(b * CHUNK, (b + 1) * CHUNK) for b in range(nblk)]
    return ns, wins, rows


def attn_fwd(qr, kp, vp, sink3):
    S = qr.shape[0]
    tq = min(2048, S)
    gw = Q_PER_KV * HEAD_DIM
    nblk = tq // CHUNK

    def body(q_ref, k_ref, v_ref, s_ref, o_ref):
        ns, wins, rows = _block_views(pl.program_id(1), nblk)
        sk = _stack_sinks(s_ref)
        band = _band_bias()
        scores = [_scores(_stack_heads(q_ref, rows[b]), k_ref[wins[b], :]) for b in range(nblk)]
        probs = [_softmax_sink(scores[b], sk, band + _edge_bias(ns[b], S))[0].astype(BF16) for b in range(nblk)]
        outs = [jnp.dot(probs[b], v_ref[wins[b], :], preferred_element_type=F32).astype(BF16) for b in range(nblk)]
        for b in range(nblk):
            for r in range(Q_PER_KV):
                o_ref[rows[b], r * HEAD_DIM:(r + 1) * HEAD_DIM] = outs[b][r * CHUNK:(r + 1) * CHUNK]

    kv = pl.BlockSpec((S + 2 * CHUNK, HEAD_DIM), lambda g, i: (0, g))
    return _call("attn_fwd", body, grid=(N_KV_HEADS, S // tq),
                 in_specs=[pl.BlockSpec((tq, gw), lambda g, i: (i, g)), kv, kv,
                           pl.BlockSpec((None, Q_PER_KV, LANES), lambda g, i: (g, 0, 0))],
                 out_specs=pl.BlockSpec((tq, gw), lambda g, i: (i, g)),
                 out_shape=jax.ShapeDtypeStruct((S, D_MODEL), BF16), sem=("parallel", "arbitrary"))(qr, kp, vp, sink3)


def attn_bwd(qr, kp, vp, sink3, dmix):
    S = qr.shape[0]
    tq = min(1024, S)
    gw = Q_PER_KV * HEAD_DIM
    scale = 1.0 / math.sqrt(HEAD_DIM)
    nblk = tq // CHUNK

    def body(q_ref, k_ref, v_ref, s_ref, do_ref, dq_ref, dk_ref, dv_ref, ds_ref):
        i = pl.program_id(1)

        @pl.when(i == 0)
        def _():
            dk_ref[...] = jnp.zeros_like(dk_ref)
            dv_ref[...] = jnp.zeros_like(dv_ref)
            ds_ref[...] = jnp.zeros_like(ds_ref)

        blocks = range(nblk)
        ns, wins, rows = _block_views(i, nblk)
        sk = _stack_sinks(s_ref)
        band = _band_bias()
        qs = [_stack_heads(q_ref, rows[b]) for b in blocks]
        dos = [_stack_heads(do_ref, rows[b]) for b in blocks]
        scores = [_scores(qs[b], k_ref[wins[b], :]) for b in blocks]
        dps = [lax.dot_general(dos[b], v_ref[wins[b], :], NT, preferred_element_type=F32) for b in blocks]
        probs = [_softmax_sink(scores[b], sk, band + _edge_bias(ns[b], S)) for b in blocks]
        deltas = [jnp.sum(probs[b][0] * dps[b], axis=1, keepdims=True) for b in blocks]
        dscs = [(probs[b][0] * (dps[b] - deltas[b]) * scale).astype(BF16) for b in blocks]
        dqs = [jnp.dot(dscs[b], k_ref[wins[b], :], preferred_element_type=F32) for b in blocks]
        dks = [lax.dot_general(dscs[b], qs[b], TN, preferred_element_type=F32) for b in blocks]
        dvs = [lax.dot_general(probs[b][0].astype(BF16), dos[b], TN, preferred_element_type=F32) for b in blocks]
        for b in blocks:
            for r in range(Q_PER_KV):
                dq_ref[rows[b], r * HEAD_DIM:(r + 1) * HEAD_DIM] = dqs[b][r * CHUNK:(r + 1) * CHUNK]
        for m in range(nblk + 2):
            parts = [(b, m - b) for b in blocks if 0 <= m - b <= 2]
            krows = pl.ds(pl.multiple_of(i * tq + m * CHUNK, CHUNK), CHUNK)
            dk_ref[krows, :] += sum(dks[b][o * CHUNK:(o + 1) * CHUNK] for b, o in parts)
            dv_ref[krows, :] += sum(dvs[b][o * CHUNK:(o + 1) * CHUNK] for b, o in parts)
        for r in range(Q_PER_KV):
            head = slice(r * CHUNK, (r + 1) * CHUNK)
            dsink = sum(jnp.sum(-probs[b][1][head] * deltas[b][head], axis=0, keepdims=True) for b in blocks)
            ds_ref[r:r + 1, :] += jnp.broadcast_to(dsink, (1, LANES))

    kv = pl.BlockSpec((S + 2 * CHUNK, HEAD_DIM), lambda g, i: (0, g))
    qspec = pl.BlockSpec((tq, gw), lambda g, i: (i, g))
    sspec = pl.BlockSpec((None, Q_PER_KV, LANES), lambda g, i: (g, 0, 0))
    padshape = jax.ShapeDtypeStruct((S + 2 * CHUNK, KV_WIDTH), F32)
    return _call("attn_bwd", body, grid=(N_KV_HEADS, S // tq),
                 in_specs=[qspec, kv, kv, sspec, qspec],
                 out_specs=[qspec, kv, kv, sspec],
                 out_shape=[jax.ShapeDtypeStruct((S, ATTN_WIDTH), F32), padshape, padshape,
                            jax.ShapeDtypeStruct((N_KV_HEADS, Q_PER_KV, LANES), F32)],
                 sem=("parallel", "arbitrary"))(qr, kp, vp, sink3, dmix)


CONV_TILE = 256


def _fill_padded(dst_ref, value, S):
    zero = jnp.zeros((CONV_PAD, LANES), F32)
    dst_ref[0:CONV_PAD, :] = zero
    dst_ref[CONV_PAD + S:2 * CONV_PAD + S, :] = zero
    dst_ref[CONV_PAD:CONV_PAD + S, :] = value


def conv_dw_fwd(z, w32, b):
    S = z.shape[0]
    T = min(CONV_TILE, S)
    lo = CONV_PAD - (CONV_KERNEL - 1) // 2

    def body(a_ref, g_ref, w_ref, b_ref, o_ref, c0_ref):
        _fill_padded(c0_ref, a_ref[...].astype(F32) * _sigmoid(g_ref[...].astype(F32)), S)

        def tile(t, carry):
            base = pl.multiple_of(t * T, T)
            acc = jnp.broadcast_to(b_ref[...], (T, LANES))
            for j in range(CONV_KERNEL):
                acc = acc + w_ref[j:j + 1, :] * c0_ref[pl.ds(base + lo + j, T), :]
            o_ref[pl.ds(base, T), :] = acc
            return carry

        lax.fori_loop(0, S // T, tile, 0)

    nca, ncg = OFF_CA // LANES, OFF_CG // LANES
    return _call("conv_dw_fwd", body, grid=(CONV_WIDTH // LANES,),
                 in_specs=[pl.BlockSpec((S, LANES), lambda cb: (0, nca + cb)), pl.BlockSpec((S, LANES), lambda cb: (0, ncg + cb)),
                           pl.BlockSpec((32, LANES), lambda cb: (0, cb)), pl.BlockSpec((1, LANES), lambda cb: (0, cb))],
                 out_specs=pl.BlockSpec((S, LANES), lambda cb: (0, cb)),
                 out_shape=jax.ShapeDtypeStruct((S, CONV_WIDTH), F32),
                 scratch=[pltpu.VMEM((S + 2 * CONV_PAD, LANES), F32)], sem=("parallel",))(z, z, w32, b)


def _ln_stats(x):
    mu = jnp.mean(x, axis=-1, keepdims=True)
    xc = x - mu
    rs = lax.rsqrt(jnp.mean(xc * xc, axis=-1, keepdims=True) + EPS)
    return xc * rs, rs


def _ln_bwd(dy, xh, rs, g):
    dxh = dy * g
    return rs * (dxh - jnp.mean(dxh, axis=-1, keepdims=True) - xh * jnp.mean(dxh * xh, axis=-1, keepdims=True))


def conv_ln_fwd(c1, g, b, mix):
    S = c1.shape[0]
    T = min(512, S)

    def body(x_ref, g_ref, b_ref, mix_ref, o_ref):
        xh, _ = _ln_stats(x_ref[...])
        y = xh * g_ref[...] + b_ref[...]
        o_ref[...] = (y * _sigmoid(y)).astype(BF16)

    row = pl.BlockSpec((T, CONV_WIDTH), lambda i: (i, 0))
    vec = pl.BlockSpec((1, CONV_WIDTH), lambda i: (0, 0))
    return _call("conv_ln_fwd", body, grid=(S // T,), in_specs=[row, vec, vec, pl.BlockSpec(memory_space=pl.ANY)],
                 out_specs=pl.BlockSpec((T, CONV_WIDTH), lambda i: (i, ATTN_WIDTH // CONV_WIDTH)),
                 out_shape=jax.ShapeDtypeStruct(mix.shape, BF16), sem=("parallel",), aliases={3: 0})(c1, g, b, mix)


def _acc_out(ref, value):
    @pl.when(pl.program_id(0) == 0)
    def _():
        ref[...] = value

    @pl.when(pl.program_id(0) > 0)
    def _():
        ref[...] += value


def conv_ln_bwd(dmix, c1, g, b):
    S = c1.shape[0]
    T = min(512, S)

    def body(d_ref, x_ref, g_ref, b_ref, dx_ref, dg_ref, db_ref):
        xh, rs = _ln_stats(x_ref[...])
        gv = g_ref[...]
        y = xh * gv + b_ref[...]
        sg = _sigmoid(y)
        dy = d_ref[...].astype(F32) * sg * (1.0 + y * (1.0 - sg))
        dx_ref[...] = _ln_bwd(dy, xh, rs, gv)
        _acc_out(dg_ref, jnp.sum(dy * xh, axis=0, keepdims=True))
        _acc_out(db_ref, jnp.sum(dy, axis=0, keepdims=True))

    row = pl.BlockSpec((T, CONV_WIDTH), lambda i: (i, 0))
    vec = pl.BlockSpec((1, CONV_WIDTH), lambda i: (0, 0))
    vshape = jax.ShapeDtypeStruct((1, CONV_WIDTH), F32)
    return _call("conv_ln_bwd", body, grid=(S // T,),
                 in_specs=[pl.BlockSpec((T, CONV_WIDTH), lambda i: (i, ATTN_WIDTH // CONV_WIDTH)), row, vec, vec],
                 out_specs=[row, vec, vec], out_shape=[jax.ShapeDtypeStruct((S, CONV_WIDTH), F32), vshape, vshape],
                 sem=("arbitrary",))(dmix, c1, g, b)


def conv_dw_bwd(dc1, z, w32):
    S = z.shape[0]
    T = min(CONV_TILE, S)
    half = (CONV_KERNEL - 1) // 2
    lo = CONV_PAD - half

    def body(d_ref, a_ref, g_ref, w_ref, da_ref, dg_ref, dw_ref, db_ref, c0_ref, d1_ref, wacc_ref):
        av = a_ref[...].astype(F32)
        sg = _sigmoid(g_ref[...].astype(F32))
        _fill_padded(c0_ref, av * sg, S)
        _fill_padded(d1_ref, d_ref[...], S)
        wacc_ref[...] = jnp.zeros_like(wacc_ref)

        def tile(t, carry):
            base = pl.multiple_of(t * T, T)
            d1 = d_ref[pl.ds(base, T), :]
            acc = jnp.zeros((T, LANES), F32)
            for j in range(CONV_KERNEL):
                acc = acc + w_ref[j:j + 1, :] * d1_ref[pl.ds(base + CONV_PAD + half - j, T), :]
                prod = d1 * c0_ref[pl.ds(base + lo + j, T), :]
                wacc_ref[j] += jnp.sum(prod.reshape(T // 8, 8, LANES), axis=0)
            rows = pl.ds(base, T)
            a_t = a_ref[rows, :].astype(F32)
            s_t = _sigmoid(g_ref[rows, :].astype(F32))
            da_ref[rows, :] = (acc * s_t).astype(BF16)
            dg_ref[rows, :] = (acc * a_t * s_t * (1.0 - s_t)).astype(BF16)
            return carry

        lax.fori_loop(0, S // T, tile, 0)
        dw_ref[...] = jnp.sum(wacc_ref[...], axis=1)
        db_ref[...] = jnp.sum(d_ref[...], axis=0, keepdims=True)

    nca, ncg = OFF_CA // LANES, OFF_CG // LANES
    col = pl.BlockSpec((S, LANES), lambda cb: (0, cb))
    oshape = jax.ShapeDtypeStruct((S, CONV_WIDTH), BF16)
    return _call("conv_dw_bwd", body, grid=(CONV_WIDTH // LANES,),
                 in_specs=[col, pl.BlockSpec((S, LANES), lambda cb: (0, nca + cb)), pl.BlockSpec((S, LANES), lambda cb: (0, ncg + cb)),
                           pl.BlockSpec((32, LANES), lambda cb: (0, cb))],
                 out_specs=[col, col, pl.BlockSpec((32, LANES), lambda cb: (0, cb)), pl.BlockSpec((1, LANES), lambda cb: (0, cb))],
                 out_shape=[oshape, oshape, jax.ShapeDtypeStruct((32, CONV_WIDTH), F32), jax.ShapeDtypeStruct((1, CONV_WIDTH), F32)],
                 scratch=[pltpu.VMEM((S + 2 * CONV_PAD, LANES), F32), pltpu.VMEM((S + 2 * CONV_PAD, LANES), F32),
                          pltpu.VMEM((32, 8, LANES), F32)], sem=("parallel",))(dc1, z, z, w32)


_INV_SQRT2 = 1.0 / math.sqrt(2.0)
_INV_SQRT2PI = 1.0 / math.sqrt(2.0 * math.pi)


def _gelu(x):
    return 0.5 * x * (1.0 + lax.erf(x * _INV_SQRT2))


def _gelu_grad(x):
    return 0.5 * (1.0 + lax.erf(x * _INV_SQRT2)) + x * jnp.exp(-0.5 * x * x) * _INV_SQRT2PI


def sgu_fwd(z, g, b, ws, bs, mix):
    S = z.shape[0]
    T = min(512, S)

    def body(u_ref, v_ref, g_ref, b_ref, ws_ref, bs_ref, mix_ref, o_ref):
        xh, _ = _ln_stats(_gelu(v_ref[...].astype(F32)))
        vn = (xh * g_ref[...] + b_ref[...]).astype(BF16)
        for ch in range(T // CHUNK):
            rows = slice(ch * CHUNK, (ch + 1) * CHUNK)
            for h in range(SGU_HEADS):
                cols = slice(h * HEAD_DIM, (h + 1) * HEAD_DIM)
                sp = jnp.dot(ws_ref[h], vn[rows, cols], preferred_element_type=F32) + bs_ref[h]
                o_ref[rows, cols] = (_gelu(u_ref[rows, cols].astype(F32)) * sp).astype(BF16)

    vec = pl.BlockSpec((1, SGU_WIDTH), lambda i: (0, 0))
    full = pl.BlockSpec((SGU_HEADS, CHUNK, CHUNK), lambda i: (0, 0, 0))
    return _call("sgu_fwd", body, grid=(S // T,),
                 in_specs=[pl.BlockSpec((T, SGU_WIDTH), lambda i: (i, OFF_U // SGU_WIDTH)),
                           pl.BlockSpec((T, SGU_WIDTH), lambda i: (i, OFF_VV // SGU_WIDTH)), vec, vec, full, full,
                           pl.BlockSpec(memory_space=pl.ANY)],
                 out_specs=pl.BlockSpec((T, SGU_WIDTH), lambda i: (i, (ATTN_WIDTH + CONV_WIDTH) // SGU_WIDTH)),
                 out_shape=jax.ShapeDtypeStruct(mix.shape, BF16), sem=("parallel",), aliases={6: 0})(z, z, g, b, ws, bs, mix)


def sgu_bwd(z, dmix, g, b, ws, bs):
    S = z.shape[0]
    T = min(512, S)

    def body(u_ref, v_ref, d_ref, g_ref, b_ref, ws_ref, bs_ref, du_ref, dv_ref, dws_ref, dbs_ref, dg_ref, db_ref, dvn_ref):
        @pl.when(pl.program_id(0) == 0)
        def _():
            dws_ref[...] = jnp.zeros_like(dws_ref)
            dbs_ref[...] = jnp.zeros_like(dbs_ref)

        vraw = v_ref[...].astype(F32)
        xh, rs = _ln_stats(_gelu(vraw))
        gv = g_ref[...]
        vn = (xh * gv + b_ref[...]).astype(BF16)
        for ch in range(T // CHUNK):
            rows = slice(ch * CHUNK, (ch + 1) * CHUNK)
            for h in range(SGU_HEADS):
                cols = slice(h * HEAD_DIM, (h + 1) * HEAD_DIM)
                w = ws_ref[h]
                vb = vn[rows, cols]
                sp = jnp.dot(w, vb, preferred_element_type=F32) + bs_ref[h]
                uraw = u_ref[rows, cols].astype(F32)
                dout = d_ref[rows, cols].astype(F32)
                du_ref[rows, cols] = (dout * sp * _gelu_grad(uraw)).astype(BF16)
                dsp = dout * _gelu(uraw)
                dspb = dsp.astype(BF16)
                dvn_ref[rows, cols] = lax.dot_general(w, dspb, TN, preferred_element_type=F32)
                dws_ref[h] += lax.dot_general(dspb, vb, NT, preferred_element_type=F32)
                dbs_ref[h] += jnp.sum(dsp, axis=1, keepdims=True)
        dvn = dvn_ref[...]
        dv_ref[...] = (_ln_bwd(dvn, xh, rs, gv) * _gelu_grad(vraw)).astype(BF16)
        _acc_out(dg_ref, jnp.sum(dvn * xh, axis=0, keepdims=True))
        _acc_out(db_ref, jnp.sum(dvn, axis=0, keepdims=True))

    vec = pl.BlockSpec((1, SGU_WIDTH), lambda i: (0, 0))
    full = pl.BlockSpec((SGU_HEADS, CHUNK, CHUNK), lambda i: (0, 0, 0))
    row = pl.BlockSpec((T, SGU_WIDTH), lambda i: (i, 0))
    oshape = jax.ShapeDtypeStruct((S, SGU_WIDTH), BF16)
    vshape = jax.ShapeDtypeStruct((1, SGU_WIDTH), F32)
    return _call("sgu_bwd", body, grid=(S // T,),
                 in_specs=[pl.BlockSpec((T, SGU_WIDTH), lambda i: (i, OFF_U // SGU_WIDTH)),
                           pl.BlockSpec((T, SGU_WIDTH), lambda i: (i, OFF_VV // SGU_WIDTH)),
                           pl.BlockSpec((T, SGU_WIDTH), lambda i: (i, (ATTN_WIDTH + CONV_WIDTH) // SGU_WIDTH)), vec, vec, full, full],
                 out_specs=[row, row, full, pl.BlockSpec((SGU_HEADS, CHUNK, 1), lambda i: (0, 0, 0)), vec, vec],
                 out_shape=[oshape, oshape, jax.ShapeDtypeStruct((SGU_HEADS, CHUNK, CHUNK), F32),
                            jax.ShapeDtypeStruct((SGU_HEADS, CHUNK, 1), F32), vshape, vshape],
                 scratch=[pltpu.VMEM((T, SGU_WIDTH), F32)], sem=("arbitrary",))(z, z, dmix, g, b, ws, bs)


def _row_tile(rows, cols, n_arrays, budget_mib=24):
    budget = (budget_mib * 1024 * 1024) // (n_arrays * 2 * 4 * cols)
    t = min(rows, max(16, budget // 16 * 16))
    while rows % t:
        t -= 16
    return t


def add_sibling_half(grad, recv, c_idx):
    J, R, C = grad.shape
    hr = R // 2
    tr = _row_tile(hr, C, 3)
    nb = hr // tr

    def body(c_ref, g_ref, r_ref, o_ref):
        o_ref[...] = (g_ref[...].astype(F32) + r_ref[...].astype(F32)).astype(BF16)

    grid_spec = pltpu.PrefetchScalarGridSpec(
        num_scalar_prefetch=1, grid=(J, nb),
        in_specs=[pl.BlockSpec((None, tr, C), lambda j, i, c: (j, c[0] * nb + i, 0)),
                  pl.BlockSpec((None, tr, C), lambda j, i, c: (j, i, 0))],
        out_specs=pl.BlockSpec((None, tr, C), lambda j, i, c: (j, i, 0)))
    return pl.pallas_call(body, name="add_sibling_half", grid_spec=grid_spec,
                          out_shape=jax.ShapeDtypeStruct((J, hr, C), BF16),
                          compiler_params=pltpu.CompilerParams(vmem_limit_bytes=VMEM_LIMIT,
                                                               dimension_semantics=("parallel", "parallel")))(c_idx, grad, recv)


def sum_chips(own, others, stack, x_idx, y_idx, layer):
    R, C = own.shape[1:]
    tr = _row_tile(R, C, 4)

    def body(x_ref, y_ref, own_ref, oth_ref, stack_ref, o_ref):
        acc = own_ref[...].astype(F32)
        for j in range(3):
            acc = acc + oth_ref[j].astype(F32)
        o_ref[...] = acc

    grid_spec = pltpu.PrefetchScalarGridSpec(
        num_scalar_prefetch=2, grid=(R // tr,),
        in_specs=[pl.BlockSpec((None, tr, C), lambda i, xr, yr: (2 * xr[0] + yr[0], i, 0)),
                  pl.BlockSpec((3, tr, C), lambda i, xr, yr: (0, i, 0)),
                  pl.BlockSpec(memory_space=pl.ANY)],
        out_specs=pl.BlockSpec((None, tr, C), lambda i, xr, yr: (layer, i, 0)))
    return pl.pallas_call(body, name="sum_chips", grid_spec=grid_spec,
                          out_shape=jax.ShapeDtypeStruct(stack.shape, F32), input_output_aliases={4: 0},
                          compiler_params=pltpu.CompilerParams(vmem_limit_bytes=VMEM_LIMIT,
                                                               dimension_semantics=("parallel",)))(x_idx, y_idx, own, others, stack)


def adamw_halves(w, mine, theirs, m, v, c_idx):
    L, R, C = w.shape
    hr = R // 2
    tr = _row_tile(hr, C, 9, budget_mib=40)
    nb = hr // tr

    def body(c_ref, w_ref, a_ref, b_ref, m_ref, v_ref, g_ref, d_ref, nm_ref, nv_ref):
        gv = jnp.where(pl.program_id(1) == c_ref[0], a_ref[...], b_ref[...])
        g_ref[...] = gv
        nm = ADAM_B1 * m_ref[...] + (1.0 - ADAM_B1) * gv
        nv = ADAM_B2 * v_ref[...] + (1.0 - ADAM_B2) * (gv * gv)
        m_hat = nm / (1.0 - ADAM_B1 ** ADAM_STEP)
        v_hat = nv / (1.0 - ADAM_B2 ** ADAM_STEP)
        d_ref[...] = -ADAM_LR * (m_hat / (jnp.sqrt(v_hat) + ADAM_EPS) + ADAM_WD * w_ref[...])
        nm_ref[...] = nm
        nv_ref[...] = nv

    full = pl.BlockSpec((None, tr, C), lambda l, h, i, c: (l, h * nb + i, 0))
    a_spec = pl.BlockSpec((None, tr, C), lambda l, h, i, c: (l, jnp.where(h == c[0], i, 0), 0))
    b_spec = pl.BlockSpec((None, tr, C), lambda l, h, i, c: (l, jnp.where(h == c[0], 0, i), 0))
    grid_spec = pltpu.PrefetchScalarGridSpec(num_scalar_prefetch=1, grid=(L, 2, nb),
                                             in_specs=[full, a_spec, b_spec, full, full], out_specs=[full] * 4)
    shape = jax.ShapeDtypeStruct((L, R, C), F32)
    return pl.pallas_call(body, name="adamw_halves", grid_spec=grid_spec, out_shape=[shape] * 4,
                          compiler_params=pltpu.CompilerParams(vmem_limit_bytes=VMEM_LIMIT,
                                                               dimension_semantics=("parallel", "arbitrary", "arbitrary")))(
        c_idx, w, mine, theirs, m, v)


def adamw(w, g, m, v):
    R, C = w.shape
    tr = _row_tile(R, C, 7)

    def body(w_ref, g_ref, m_ref, v_ref, d_ref, nm_ref, nv_ref):
        gv = g_ref[...]
        nm = ADAM_B1 * m_ref[...] + (1.0 - ADAM_B1) * gv
        nv = ADAM_B2 * v_ref[...] + (1.0 - ADAM_B2) * (gv * gv)
        m_hat = nm / (1.0 - ADAM_B1 ** ADAM_STEP)
        v_hat = nv / (1.0 - ADAM_B2 ** ADAM_STEP)
        d_ref[...] = -ADAM_LR * (m_hat / (jnp.sqrt(v_hat) + ADAM_EPS) + ADAM_WD * w_ref[...])
        nm_ref[...] = nm
        nv_ref[...] = nv

    spec = pl.BlockSpec((tr, C), lambda i: (i, 0))
    shape = jax.ShapeDtypeStruct((R, C), F32)
    return _call("adamw", body, grid=(R // tr,), in_specs=[spec] * 4, out_specs=[spec] * 3, out_shape=[shape] * 3,
                 sem=("parallel",))(w, g, m, v)


def _place():
    x, y, c = lax.axis_index("x"), lax.axis_index("y"), lax.axis_index("c")
    chips = [(1 - x, y), (x, 1 - y), (1 - x, 1 - y)]
    return x, y, c, chips


def _remote(src, dst, send_sem, recv_sem, dev):
    return pltpu.make_async_remote_copy(src_ref=src, dst_ref=dst, send_sem=send_sem, recv_sem=recv_sem,
                                        device_id=dev, device_id_type=MESH)


EFFECT = pltpu.SideEffectType.DATAFLOW_SIDE_EFFECTING
SEM = pl.BlockSpec(memory_space=pltpu.SEMAPHORE)
ANY = pl.BlockSpec(memory_space=pl.ANY)
TOKEN = jax.ShapeDtypeStruct((8, LANES), F32)


def _in_hbm(a):
    return pltpu.with_memory_space_constraint(a, pltpu.HBM)


def _gather_copies(shards, lands, send_sems, recv_sems):
    x, y, c, chips = _place()
    me = 2 * x + y
    copies = []
    for k in range(len(shards)):
        hr = shards[k].shape[0] // 2
        mine = pl.ds(pl.multiple_of(c * hr, 8), hr)
        for t, (px, py) in enumerate(chips):
            copies.append(_remote(shards[k].at[mine, :], lands[k].at[me, mine, :], send_sems.at[4 * k + t], recv_sems.at[4 * k + t],
                                  (px, py, c)))
        copies.append(_remote(shards[k], lands[k].at[me], send_sems.at[4 * k + 3], recv_sems.at[4 * k + 3], (x, y, 1 - c)))
    return copies


def _gather_landings(lands, send_sems, recv_sems):
    x, y, c, chips = _place()
    me = 2 * x + y
    landings = []
    for k in range(len(lands)):
        hr = lands[k].shape[1] // 2
        mine = pl.ds(pl.multiple_of(c * hr, 8), hr)
        for t, (px, py) in enumerate(chips):
            dst = lands[k].at[2 * px + py, mine, :]
            landings.append(_remote(dst, dst, send_sems.at[4 * k + t], recv_sems.at[4 * k + t], (px, py, c)))
        dst = lands[k].at[me]
        landings.append(_remote(dst, dst, send_sems.at[4 * k + 3], recv_sems.at[4 * k + 3], (x, y, 1 - c)))
    return landings


def gather_start(shards, after):
    n = len(shards)

    def body(*refs):
        srcs, lands_in = refs[:n], refs[n:2 * n]
        send_sems, recv_sems = refs[2 * n + 1], refs[2 * n + 2]
        token = refs[-1]
        for cp in _gather_copies(srcs, lands_in, send_sems, recv_sems):
            cp.start()
        token[...] = jnp.zeros_like(token)

    lands = [lax.empty((N_CHIPS,) + s.shape, s.dtype) for s in shards]
    outs = pl.pallas_call(
        body, name="gather_start", in_specs=[HBM] * (2 * n) + [ANY],
        out_specs=[SEM, SEM] + [HBM] * (2 * n) + [VMEM_SPEC],
        out_shape=[pltpu.SemaphoreType.DMA((4 * n,)), pltpu.SemaphoreType.DMA((4 * n,))]
        + [pltpu.HBM(s.shape, s.dtype) for s in shards] + [pltpu.HBM(l.shape, l.dtype) for l in lands] + [TOKEN],
        input_output_aliases={i: 2 + i for i in range(2 * n)},
        compiler_params=pltpu.CompilerParams(has_side_effects=EFFECT),
    )(*[_in_hbm(s) for s in shards], *[_in_hbm(l) for l in lands], after)
    return outs[0], outs[1], outs[2:2 + n], outs[2 + n:2 + 2 * n], outs[-1]


def gather_wait(send_sems, recv_sems, shards, lands, after):
    n = len(shards)

    def body(*refs):
        srcs, lands_in = refs[:n], refs[n:2 * n]
        send, recv = refs[2 * n], refs[2 * n + 1]
        for cp in _gather_copies(srcs, lands_in, send, recv):
            cp.wait_send()
        for cp in _gather_landings(lands_in, send, recv):
            cp.wait_recv()

    outs = pl.pallas_call(
        body, name="gather_wait", in_specs=[HBM] * (2 * n) + [SEM, SEM, ANY], out_specs=[HBM] * (2 * n),
        out_shape=[pltpu.HBM(s.shape, s.dtype) for s in shards] + [pltpu.HBM(l.shape, l.dtype) for l in lands],
        input_output_aliases={i: i for i in range(2 * n)},
        compiler_params=pltpu.CompilerParams(has_side_effects=EFFECT),
    )(*shards, *lands, send_sems, recv_sems, after)
    return outs[n:]


def _forward_copies(lands, send_sems, recv_sems, received):
    x, y, c, chips = _place()
    copies = []
    for k in range(len(lands)):
        hr = lands[k].shape[1] // 2
        half = (1 - c) if received else c
        rows = pl.ds(pl.multiple_of(half * hr, 8), hr)
        for t, (px, py) in enumerate(chips):
            block = lands[k].at[2 * px + py, rows, :]
            copies.append(_remote(block, block, send_sems.at[3 * k + t], recv_sems.at[3 * k + t], (x, y, 1 - c)))
    return copies


def forward_start(lands):
    n = len(lands)

    def body(*refs):
        for cp in _forward_copies(refs[:n], refs[n], refs[n + 1], received=False):
            cp.start()
        refs[-1][...] = jnp.zeros_like(refs[-1])

    outs = pl.pallas_call(
        body, name="forward_start", in_specs=[HBM] * n, out_specs=[SEM, SEM] + [HBM] * n + [VMEM_SPEC],
        out_shape=[pltpu.SemaphoreType.DMA((3 * n,)), pltpu.SemaphoreType.DMA((3 * n,))]
        + [pltpu.HBM(l.shape, l.dtype) for l in lands] + [TOKEN],
        input_output_aliases={i: 2 + i for i in range(n)},
        compiler_params=pltpu.CompilerParams(has_side_effects=EFFECT),
    )(*[_in_hbm(l) for l in lands])
    return outs[0], outs[1], outs[2:2 + n], outs[-1]


def forward_wait(send_sems, recv_sems, lands, after):
    n = len(lands)

    def body(*refs):
        for cp in _forward_copies(refs[:n], refs[n], refs[n + 1], received=False):
            cp.wait_send()
        for cp in _forward_copies(refs[:n], refs[n], refs[n + 1], received=True):
            cp.wait_recv()

    return pl.pallas_call(
        body, name="forward_wait", in_specs=[HBM] * n + [SEM, SEM, ANY], out_specs=[HBM] * n,
        out_shape=[pltpu.HBM(l.shape, l.dtype) for l in lands],
        input_output_aliases={i: i for i in range(n)},
        compiler_params=pltpu.CompilerParams(has_side_effects=EFFECT),
    )(*lands, send_sems, recv_sems, after)


def forward_halves(lands):
    n = len(lands)

    def body(*refs):
        ins, outs = refs[:n], refs[n:2 * n]
        send_sems, recv_sems = refs[2 * n:]
        x, y, c, chips = _place()
        sibling = (x, y, 1 - c)
        sends = []
        for k in range(n):
            hr = ins[k].shape[1] // 2
            mine = pl.ds(pl.multiple_of(c * hr, 8), hr)
            for t, (px, py) in enumerate(chips):
                cp = _remote(ins[k].at[2 * px + py, mine, :], outs[k].at[2 * px + py, mine, :],
                             send_sems.at[k, t], recv_sems.at[k, t], sibling)
                cp.start()
                sends.append(cp)
        for k in range(n):
            hr = ins[k].shape[1] // 2
            other = pl.ds(pl.multiple_of((1 - c) * hr, 8), hr)
            for t, (px, py) in enumerate(chips):
                dst = outs[k].at[2 * px + py, other, :]
                _remote(dst, dst, send_sems.at[k, t], recv_sems.at[k, t], sibling).wait_recv()
        for cp in sends:
            cp.wait_send()

    return pl.pallas_call(
        body, name="forward_halves", in_specs=[HBM] * n, out_specs=[HBM] * n,
        out_shape=[jax.ShapeDtypeStruct(l.shape, l.dtype) for l in lands],
        input_output_aliases={i: i for i in range(n)},
        scratch_shapes=[pltpu.SemaphoreType.DMA((n, 3)), pltpu.SemaphoreType.DMA((n, 3))],
    )(*lands)


def gather_small(block):
    def body(in_ref, out_ref, send_sems, recv_sems):
        x, y, c, chips = _place()
        me = 2 * x + y
        out_ref[me] = in_ref[...]
        sends = []
        for t, (px, py) in enumerate(chips):
            cp = _remote(in_ref, out_ref.at[me], send_sems.at[t], recv_sems.at[t], (px, py, c))
            cp.start()
            sends.append(cp)
        for t, (px, py) in enumerate(chips):
            landed = out_ref.at[2 * px + py]
            _remote(landed, landed, send_sems.at[t], recv_sems.at[t], (px, py, c)).wait_recv()
        for cp in sends:
            cp.wait_send()

    return pl.pallas_call(
        body, name="gather_small", in_specs=[VMEM_SPEC], out_specs=VMEM_SPEC,
        out_shape=jax.ShapeDtypeStruct((N_CHIPS,) + block.shape, block.dtype),
        scratch_shapes=[pltpu.SemaphoreType.DMA((3,)), pltpu.SemaphoreType.DMA((3,))],
    )(block)


def exchange_sibling_halves(grads):
    n = len(grads)

    def body(*refs):
        ins, outs = refs[:n], refs[n:2 * n]
        send_sems, recv_sems = refs[2 * n:]
        x, y, c, _ = _place()
        copies = []
        for k in range(n):
            hr = ins[k].shape[1] // 2
            theirs = pl.ds(pl.multiple_of((1 - c) * hr, 8), hr)
            cp = _remote(ins[k].at[:, theirs, :], outs[k], send_sems.at[k], recv_sems.at[k], (x, y, 1 - c))
            cp.start()
            copies.append(cp)
        for cp in copies:
            cp.wait()

    return pl.pallas_call(
        body, name="exchange_sibling_halves", in_specs=[HBM] * n, out_specs=[HBM] * n,
        out_shape=[jax.ShapeDtypeStruct((g.shape[0], g.shape[1] // 2, g.shape[2]), g.dtype) for g in grads],
        scratch_shapes=[pltpu.SemaphoreType.DMA((n,)), pltpu.SemaphoreType.DMA((n,))],
    )(*grads)


def _sibling_half_copies(grads, lands, send_sems, recv_sems):
    x, y, c, _ = _place()
    copies = []
    for k in range(len(grads)):
        hr = grads[k].shape[1] // 2
        theirs = pl.ds(pl.multiple_of((1 - c) * hr, 8), hr)
        copies.append(_remote(grads[k].at[:, theirs, :], lands[k], send_sems.at[k], recv_sems.at[k], (x, y, 1 - c)))
    return copies


def _sibling_whole_copies(srcs, lands, send_sems, recv_sems):
    x, y, c, _ = _place()
    return [_remote(srcs[k], lands[k], send_sems.at[k], recv_sems.at[k], (x, y, 1 - c)) for k in range(len(srcs))]


def pair_start(name, make_copies, srcs, land_shapes):
    n = len(srcs)

    def body(*refs):
        src_refs, land_refs = refs[:n], refs[n:2 * n]
        send_sems, recv_sems = refs[2 * n], refs[2 * n + 1]
        token = refs[-1]
        for cp in make_copies(src_refs, land_refs, send_sems, recv_sems):
            cp.start()
        token[...] = jnp.zeros_like(token)

    lands = [lax.empty(shape, s.dtype) for shape, s in zip(land_shapes, srcs)]
    outs = pl.pallas_call(
        body, name=name, in_specs=[HBM] * (2 * n), out_specs=[SEM, SEM] + [HBM] * (2 * n) + [VMEM_SPEC],
        out_shape=[pltpu.SemaphoreType.DMA((n,)), pltpu.SemaphoreType.DMA((n,))]
        + [pltpu.HBM(s.shape, s.dtype) for s in srcs] + [pltpu.HBM(l.shape, l.dtype) for l in lands] + [TOKEN],
        input_output_aliases={i: 2 + i for i in range(2 * n)},
        compiler_params=pltpu.CompilerParams(has_side_effects=EFFECT),
    )(*[_in_hbm(s) for s in srcs], *[_in_hbm(l) for l in lands])
    return outs[0], outs[1], outs[2:2 + n], outs[2 + n:2 + 2 * n], outs[-1]


def pair_wait_one(name, send_sems, recv_sems, src, land, after, index):
    def body(src_ref, land_ref, send, recv, after_ref, src_out, land_out):
        x, y, c, _ = _place()
        cp = _remote(src_ref, land_ref, send.at[index], recv.at[index], (x, y, 1 - c))
        cp.wait_send()
        cp.wait_recv()

    return pl.pallas_call(
        body, name=name, in_specs=[HBM, HBM, SEM, SEM, ANY], out_specs=[HBM, HBM],
        out_shape=[pltpu.HBM(src.shape, src.dtype), pltpu.HBM(land.shape, land.dtype)],
        input_output_aliases={0: 0, 1: 1},
        compiler_params=pltpu.CompilerParams(has_side_effects=EFFECT),
    )(src, land, send_sems, recv_sems, after)


def pair_wait(name, make_copies, send_sems, recv_sems, srcs, lands, after):
    n = len(srcs)

    def body(*refs):
        src_refs, land_refs = refs[:n], refs[n:2 * n]
        for cp in make_copies(src_refs, land_refs, refs[2 * n], refs[2 * n + 1]):
            cp.wait_send()
            cp.wait_recv()

    outs = pl.pallas_call(
        body, name=name, in_specs=[HBM] * (2 * n) + [SEM, SEM, ANY], out_specs=[HBM] * (2 * n),
        out_shape=[pltpu.HBM(s.shape, s.dtype) for s in srcs] + [pltpu.HBM(l.shape, l.dtype) for l in lands],
        input_output_aliases={i: i for i in range(2 * n)},
        compiler_params=pltpu.CompilerParams(has_side_effects=EFFECT),
    )(*srcs, *lands, send_sems, recv_sems, after)
    return outs[:n], outs[n:]


def _chip_copies(parts, lands, send_sems, recv_sems):
    x, y, c, chips = _place()
    return [_remote(parts[k].at[2 * px + py], lands[k].at[t], send_sems.at[3 * k + t], recv_sems.at[3 * k + t], (px, py, c))
            for k in range(len(parts)) for t, (px, py) in enumerate(chips)]


def chip_parts_start(parts):
    n = len(parts)

    def body(*refs):
        srcs, lands_in = refs[:n], refs[n:2 * n]
        send_sems, recv_sems = refs[2 * n], refs[2 * n + 1]
        token = refs[-1]
        for cp in _chip_copies(srcs, lands_in, send_sems, recv_sems):
            cp.start()
        token[...] = jnp.zeros_like(token)

    lands = [lax.empty((3,) + p.shape[1:], p.dtype) for p in parts]
    outs = pl.pallas_call(
        body, name="chip_parts_start", in_specs=[HBM] * (2 * n), out_specs=[SEM, SEM] + [HBM] * (2 * n) + [VMEM_SPEC],
        out_shape=[pltpu.SemaphoreType.DMA((3 * n,)), pltpu.SemaphoreType.DMA((3 * n,))]
        + [pltpu.HBM(p.shape, p.dtype) for p in parts] + [pltpu.HBM(l.shape, l.dtype) for l in lands] + [TOKEN],
        input_output_aliases={i: 2 + i for i in range(2 * n)},
        compiler_params=pltpu.CompilerParams(has_side_effects=EFFECT),
    )(*[_in_hbm(p) for p in parts], *[_in_hbm(l) for l in lands])
    return outs[0], outs[1], outs[2:2 + n], outs[2 + n:2 + 2 * n], outs[-1]


def chip_parts_wait(send_sems, recv_sems, parts, lands, after):
    n = len(parts)

    def body(*refs):
        srcs, lands_in = refs[:n], refs[n:2 * n]
        send, recv = refs[2 * n], refs[2 * n + 1]
        for cp in _chip_copies(srcs, lands_in, send, recv):
            cp.wait_send()
            cp.wait_recv()

    outs = pl.pallas_call(
        body, name="chip_parts_wait", in_specs=[HBM] * (2 * n) + [SEM, SEM, ANY], out_specs=[HBM] * (2 * n),
        out_shape=[pltpu.HBM(p.shape, p.dtype) for p in parts] + [pltpu.HBM(l.shape, l.dtype) for l in lands],
        input_output_aliases={i: i for i in range(2 * n)},
        compiler_params=pltpu.CompilerParams(has_side_effects=EFFECT),
    )(*parts, *lands, send_sems, recv_sems, after)
    return outs[:n], outs[n:]


def allreduce_small(packed):
    R = packed.shape[0]

    def body(x_ref, sum_ref, all_ref, send_sems, recv_sems):
        x, y, c, chips = _place()
        me, sibling = (x, y, c), (x, y, 1 - c)

        def rows(px, py, pc):
            return all_ref.at[4 * px + 2 * py + pc]

        def copy(k, block, to, src=None):
            return _remote(rows(*block) if src is None else src, rows(*block), send_sems.at[k], recv_sems.at[k], to)

        all_ref[4 * x + 2 * y + c] = x_ref[...]
        first = [copy(0, me, sibling, src=x_ref)]
        first += [copy(1 + j, me, (*chip, c), src=x_ref) for j, chip in enumerate(chips)]
        for cp in first:
            cp.start()
        passed = [copy(4 + j, (*chip, c), sibling) for j, chip in enumerate(chips)]
        for j, chip in enumerate(chips):
            copy(1 + j, (*chip, c), me).wait_recv()
            passed[j].start()
        copy(0, sibling, me).wait_recv()
        for j, chip in enumerate(chips):
            copy(4 + j, (*chip, 1 - c), me).wait_recv()
        for cp in first + passed:
            cp.wait_send()

        def chunk(i, carry):
            rws = pl.ds(pl.multiple_of(i * PACK_ROWS, PACK_ROWS), PACK_ROWS)
            acc = all_ref[0, rws, :]
            for d in range(1, N_DEV):
                acc = acc + all_ref[d, rws, :]
            sum_ref[rws, :] = acc
            return carry

        lax.fori_loop(0, R // PACK_ROWS, chunk, 0)

    return pl.pallas_call(
        body, name="allreduce_small", in_specs=[VMEM_SPEC], out_specs=VMEM_SPEC,
        out_shape=jax.ShapeDtypeStruct((R, LANES), F32),
        scratch_shapes=[pltpu.VMEM((N_DEV, R, LANES), F32), pltpu.SemaphoreType.DMA((7,)), pltpu.SemaphoreType.DMA((7,))],
        compiler_params=pltpu.CompilerParams(vmem_limit_bytes=VMEM_LIMIT),
    )(packed)


def _mixer_fwd(x, h, p, tabs, token, late_weights=None):
    z = mm_nn_cols(h, p["w_in"], token)
    qr, kp, vp = rope_fwd(z, tabs)
    mix = attn_fwd(qr, kp, vp, p["sink3"])
    c1 = conv_dw_fwd(z, p["conv_w32"], p["conv_dw_b"])
    mix = conv_ln_fwd(c1, p["conv_ln_g"], p["conv_ln_b"], mix)
    mix = sgu_fwd(z, p["sgu_ln_g"], p["sgu_ln_b"], p["sgu_w16"], p["sgu_b3"], mix)
    if late_weights is not None:
        p.update(late_weights(mix))
    x_mid, h2 = mm_nn_rows_res(mix, p["w_out"], x, p["ffn_norm_g"])
    return x_mid, h2, dict(x=x, h=h, z=z, qr=qr, kp=kp, vp=vp, c1=c1, mix=mix, x_mid=x_mid)


def _ffn_fwd(x_mid, h2, p, next_gain, token):
    gate, up, act = ffn_up(h2, p["w_gate"], p["w_up"], token)
    x_out, h_next = mm_nn_rows_res(act, p["w_down"], x_mid, next_gain)
    return x_out, h_next, dict(h2=h2, gate=gate, up=up, act=act)


def _layer_fwd(x, h, p, next_gain, tabs, token):
    x_mid, h2, s_mix = _mixer_fwd(x, h, p, tabs, token)
    x_out, h_next, s_ffn = _ffn_fwd(x_mid, h2, p, next_gain, token)
    return x_out, h_next, {**s_mix, **s_ffn}


def _ffn_bwd(dx, dxb, p, s, token):
    dgate, dup = ffn_down_bwd(dxb, p["w_down"], s["gate"], s["up"], token)
    g_down = mm_tn_rows(s["act"], dxb)
    dh2 = mm_nt_cols([(dgate, p["w_gate"]), (dup, p["w_up"])], BF16, 1)
    g_gate = mm_tn_cols(s["h2"], dgate, N_CHIPS)
    g_up = mm_tn_cols(s["h2"], dup, N_CHIPS)
    dmid, dmidb, g_ffn_norm = rms_bwd(s["x_mid"], p["ffn_norm_g"], dh2, dx)
    return dmid, dmidb, [g_gate, g_up, g_down.reshape(N_CHIPS, -1, D_MODEL)], g_ffn_norm


def _mixer_bwd(dmid, dmidb, p, s, tabs, token):
    dmix = mm_nt_rows(dmidb, p["w_out"], token)
    g_out = mm_tn_rows(s["mix"], dmidb)
    dq, dkp, dvp, dsink = attn_bwd(s["qr"], s["kp"], s["vp"], p["sink3"], dmix)
    dqkv = rope_bwd(dq, dkp, dvp, tabs)
    dc1, g_cln_g, g_cln_b = conv_ln_bwd(dmix, s["c1"], p["conv_ln_g"], p["conv_ln_b"])
    dca, dcg, g_cw, g_cb = conv_dw_bwd(dc1, s["z"], p["conv_w32"])
    du, dv, g_sw, g_sb, g_sln_g, g_sln_b = sgu_bwd(s["z"], dmix, p["sgu_ln_g"], p["sgu_ln_b"], p["sgu_w16"], p["sgu_b3"])
    dz = jnp.concatenate([dqkv, dca, dcg, du, dv], axis=1)
    dh = mm_nt_cols([(dz, p["w_in"])], BF16, N_CHIPS)
    g_in = mm_tn_cols(s["h"], dz, N_CHIPS)
    dx_in, dxb_in, g_mix_norm = rms_bwd(s["x"], p["mix_norm_g"], dh, dmid)
    small = dict(mix_norm_g=g_mix_norm, sink=dsink[:, :, 0].reshape(1, N_Q_HEADS), conv_dw_w=g_cw[:CONV_KERNEL],
                 conv_dw_b=g_cb, conv_ln_g=g_cln_g, conv_ln_b=g_cln_b, sgu_ln_g=g_sln_g, sgu_ln_b=g_sln_b,
                 sgu_w=g_sw, sgu_b=g_sb[:, :, 0])
    return dx_in, dxb_in, [g_in, g_out.reshape(N_CHIPS, -1, D_MODEL)], small


def _layer_bwd(dx, dxb, p, s, tabs, token):
    dmid, dmidb, ffn_big, g_ffn_norm = _ffn_bwd(dx, dxb, p, s, token)
    dx_in, dxb_in, mix_big, small = _mixer_bwd(dmid, dmidb, p, s, tabs, token)
    return dx_in, dxb_in, mix_big + ffn_big, dict(small, ffn_norm_g=g_ffn_norm)


def _mixer_weights(gathered):
    w_in, w_out = gathered
    return dict(w_in=w_in, w_out=w_out.reshape(-1, D_MODEL))


def _ffn_weights(gathered):
    w_gate, w_up, w_down = gathered
    return dict(w_gate=w_gate, w_up=w_up, w_down=w_down.reshape(-1, D_MODEL))


def _small_params(l, conv_w_full, mix_norm_g, sink, conv_dw_b, conv_ln_g, conv_ln_b, sgu_ln_g, sgu_ln_b, sgu_w, sgu_b,
                  ffn_norm_g):
    return dict(
        mix_norm_g=mix_norm_g[l:l + 1], ffn_norm_g=ffn_norm_g[l:l + 1],
        sink3=jnp.broadcast_to(sink[l].reshape(N_KV_HEADS, Q_PER_KV, 1), (N_KV_HEADS, Q_PER_KV, LANES)),
        conv_w32=jnp.pad(conv_w_full[l], ((0, 32 - CONV_KERNEL), (0, 0))),
        conv_dw_b=conv_dw_b[l:l + 1], conv_ln_g=conv_ln_g[l:l + 1], conv_ln_b=conv_ln_b[l:l + 1],
        sgu_ln_g=sgu_ln_g[l:l + 1], sgu_ln_b=sgu_ln_b[l:l + 1], sgu_w16=sgu_w[l].astype(BF16),
        sgu_b3=jnp.broadcast_to(sgu_b[l][:, :, None], (SGU_HEADS, CHUNK, CHUNK)))


_SMALL = ["mix_norm_g", "sink", "conv_dw_b", "conv_ln_g", "conv_ln_b", "sgu_ln_g", "sgu_ln_b", "sgu_w", "sgu_b", "ffn_norm_g",
          "final_norm_g"]


def _pack_rows(arrays):
    rows, counts = [], []
    for a in arrays:
        flat = a.reshape(-1)
        n = -(-flat.shape[0] // LANES)
        rows.append(jnp.pad(flat, (0, n * LANES - flat.shape[0])).reshape(n, LANES))
        counts.append(n)
    packed = jnp.concatenate(rows, axis=0)
    pad = -packed.shape[0] % PACK_ROWS
    return jnp.pad(packed, ((0, pad), (0, 0))), counts


def _unpack_rows(packed, counts, shapes):
    out, r = [], 0
    for n, shape in zip(counts, shapes):
        size = math.prod(shape)
        out.append(packed[r:r + n].reshape(-1)[:size].reshape(shape))
        r += n
    return out


def kernel(x, mix_norm_g, w_in, sink, conv_dw_w, conv_dw_b, conv_ln_g, conv_ln_b, sgu_ln_g, sgu_ln_b, sgu_w, sgu_b, w_out, ffn_norm_g, w_gate, w_up, w_down, final_norm_g, loss_target, m_mix_norm_g, m_w_in, m_sink, m_conv_dw_w, m_conv_dw_b, m_conv_ln_g, m_conv_ln_b, m_sgu_ln_g, m_sgu_ln_b, m_sgu_w, m_sgu_b, m_w_out, m_ffn_norm_g, m_w_gate, m_w_up, m_w_down, m_final_norm_g, v_mix_norm_g, v_w_in, v_sink, v_conv_dw_w, v_conv_dw_b, v_conv_ln_g, v_conv_ln_b, v_sgu_ln_g, v_sgu_ln_b, v_sgu_w, v_sgu_b, v_w_out, v_ffn_norm_g, v_w_gate, v_w_up, v_w_down, v_final_norm_g):
    S = x.shape[1]
    my_chip = 2 * lax.axis_index("x") + lax.axis_index("y")
    c_idx = lax.axis_index("c").astype(jnp.int32).reshape(1)
    big_w = [w_in, w_out, w_gate, w_up, w_down]
    big_m = [m_w_in, m_w_out, m_w_gate, m_w_up, m_w_down]
    big_v = [v_w_in, v_w_out, v_w_gate, v_w_up, v_w_down]
    n_kinds = len(big_w)

    x_idx = lax.axis_index("x").astype(jnp.int32).reshape(1)
    y_idx = lax.axis_index("y").astype(jnp.int32).reshape(1)
    conv_w_all = gather_small(conv_dw_w)
    conv_w_full = jnp.transpose(conv_w_all, (1, 2, 0, 3)).reshape(DEPTH, CONV_KERNEL, CONV_WIDTH)
    tabs = rope_tables(S)
    no_token = jnp.zeros(TOKEN.shape, TOKEN.dtype)

    mixer_kinds, ffn_kinds = [0, 1], [2, 3, 4]
    shards = [[w[l].astype(BF16) for w in big_w] for l in range(DEPTH)]

    def fetch(pending, after):
        send_sems, recv_sems, srcs, lands, _ = pending
        return forward_halves(gather_wait(send_sems, recv_sems, srcs, lands, after))

    first_mixer = gather_start([shards[0][k] for k in mixer_kinds], conv_w_all)
    first_ffn = gather_start([shards[0][k] for k in ffn_kinds], first_mixer[4])
    act = x[0]
    h = rms_fwd(act, mix_norm_g[0:1], no_token)
    saved, params = [], []
    for l in range(DEPTH):
        p = _small_params(l, conv_w_full, mix_norm_g, sink, conv_dw_b, conv_ln_g, conv_ln_b, sgu_ln_g, sgu_ln_b, sgu_w, sgu_b,
                          ffn_norm_g)
        next_gain = mix_norm_g[l + 1:l + 2] if l + 1 < DEPTH else final_norm_g.reshape(1, D_MODEL)
        if l == 0:
            p.update(_mixer_weights(fetch(first_mixer, act)))
            x_mid, h2, s_mix = _mixer_fwd(act, h, p, tabs, first_ffn[4])
            ffn_w = fetch(first_ffn, x_mid)
            p.update(_ffn_weights(ffn_w))
            late, after, token = None, ffn_w[0], no_token
        else:
            send_sems, recv_sems, srcs, lands, _ = pending
            lands = gather_wait(send_sems, recv_sems, srcs, lands, act)
            w_in_full = forward_halves(lands[:1])[0]
            p.update(w_in=w_in_full)
            fwd_send, fwd_recv, rest, token = forward_start(lands[1:])

            def late(mix, fwd_send=fwd_send, fwd_recv=fwd_recv, rest=rest):
                w_out_full, *ffn_full = forward_wait(fwd_send, fwd_recv, rest, mix)
                return dict(_ffn_weights(ffn_full), w_out=w_out_full.reshape(-1, D_MODEL))

            after = w_in_full
        if l + 1 < DEPTH:
            pending = gather_start(shards[l + 1], after)
            token = token + pending[4]
        if l > 0:
            x_mid, h2, s_mix = _mixer_fwd(act, h, p, tabs, token, late)
        act, h, s_ffn = _ffn_fwd(x_mid, h2, p, next_gain, token)
        params.append(p)
        saved.append({**s_mix, **s_ffn})
    loss_part, dx, dxb, g_final = final_loss(act, final_norm_g.reshape(1, D_MODEL), loss_target[0])
    loss = lax.psum(loss_part[0, 0], ("x", "y", "c"))

    halves = [jnp.zeros((DEPTH, w.shape[1] // 2, w.shape[2]), F32) for w in big_w]
    small_grads = [None] * DEPTH

    def chip_start(layer, kinds, grads, recv):
        chip_sum = [add_sibling_half(g, r, c_idx) for g, r in zip(grads, recv)]
        send_sems, recv_sems, parts, lands, token = chip_parts_start(chip_sum)
        return (layer, kinds, send_sems, recv_sems, parts, lands), token

    def reduce_start(layer, kinds, grads):
        return chip_start(layer, kinds, grads, exchange_sibling_halves(grads))

    def reduce_finish(pending, halves, after):
        layer, kinds, send_sems, recv_sems, parts, lands = pending
        parts, others = chip_parts_wait(send_sems, recv_sems, parts, lands, after)
        halves = list(halves)
        for i, k in enumerate(kinds):
            halves[k] = sum_chips(parts[i], others[i], halves[k], x_idx, y_idx, layer)
        return halves

    pending, token = None, no_token
    for l in reversed(range(DEPTH)):
        dmid, dmidb, ffn_big, g_ffn_norm = _ffn_bwd(dx, dxb, params[l], saved[l], token)
        if l == 0:
            last_ffn, mixer_token = reduce_start(l, ffn_kinds, ffn_big)
        else:
            half_shapes = [(g.shape[0], g.shape[1] // 2, g.shape[2]) for g in ffn_big]
            sib_send, sib_recv, ffn_big, ffn_lands, mixer_token = pair_start("sibling_start", _sibling_half_copies, ffn_big, half_shapes)
        dx, dxb, mix_big, small = _mixer_bwd(dmid, dmidb, params[l], saved[l], tabs, mixer_token)
        small_grads[l] = dict(small, ffn_norm_g=g_ffn_norm)
        if pending is not None:
            halves = reduce_finish(pending, halves, dx)
        if l == 0:
            last_mixer, token = reduce_start(l, mixer_kinds, mix_big)
            halves = reduce_finish(last_ffn, halves, token)
            halves = reduce_finish(last_mixer, halves, halves[ffn_kinds[0]])
        else:
            ffn_big, ffn_recv = pair_wait("sibling_wait", _sibling_half_copies, sib_send, sib_recv, ffn_big, ffn_lands, dx)
            mix_recv = exchange_sibling_halves(mix_big)
            pending, token = chip_start(l, mixer_kinds + ffn_kinds, list(mix_big) + list(ffn_big), list(mix_recv) + list(ffn_recv))

    fin_send, fin_recv, halves, fin_lands, _ = pair_start("final_start", _sibling_whole_copies, halves, [h.shape for h in halves])

    stacked = {n: jnp.stack([small_grads[l][n] for l in range(DEPTH)]) for n in small_grads[0]}
    stacked["final_norm_g"] = g_final
    packed, counts = _pack_rows([stacked[n] for n in _SMALL] + [stacked["conv_dw_w"]])
    reduced = allreduce_small(packed)
    small_w = dict(mix_norm_g=mix_norm_g, sink=sink, conv_dw_b=conv_dw_b, conv_ln_g=conv_ln_g, conv_ln_b=conv_ln_b,
                   sgu_ln_g=sgu_ln_g, sgu_ln_b=sgu_ln_b, sgu_w=sgu_w, sgu_b=sgu_b, ffn_norm_g=ffn_norm_g,
                   final_norm_g=final_norm_g)
    small_m = dict(mix_norm_g=m_mix_norm_g, sink=m_sink, conv_dw_b=m_conv_dw_b, conv_ln_g=m_conv_ln_g,
                   conv_ln_b=m_conv_ln_b, sgu_ln_g=m_sgu_ln_g, sgu_ln_b=m_sgu_ln_b, sgu_w=m_sgu_w, sgu_b=m_sgu_b,
                   ffn_norm_g=m_ffn_norm_g, final_norm_g=m_final_norm_g)
    small_v = dict(mix_norm_g=v_mix_norm_g, sink=v_sink, conv_dw_b=v_conv_dw_b, conv_ln_g=v_conv_ln_g,
                   conv_ln_b=v_conv_ln_b, sgu_ln_g=v_sgu_ln_g, sgu_ln_b=v_sgu_ln_b, sgu_w=v_sgu_w, sgu_b=v_sgu_b,
                   ffn_norm_g=v_ffn_norm_g, final_norm_g=v_final_norm_g)
    shapes = [small_w[n].shape for n in _SMALL] + [(DEPTH, CONV_KERNEL, CONV_WIDTH)]
    red = _unpack_rows(reduced, counts, shapes)
    g_small = dict(zip(_SMALL, red[:-1]))
    g_small["conv_dw_w"] = lax.dynamic_slice_in_dim(red[-1], my_chip * LANES, LANES, axis=2)
    small_w["conv_dw_w"], small_m["conv_dw_w"], small_v["conv_dw_w"] = conv_dw_w, m_conv_dw_w, v_conv_dw_w
    names = _SMALL + ["conv_dw_w"]
    pw, cnt = _pack_rows([small_w[n] for n in names])
    pg, _ = _pack_rows([g_small[n] for n in names])
    pm, _ = _pack_rows([small_m[n] for n in names])
    pv, _ = _pack_rows([small_v[n] for n in names])
    sd, sm, sv = adamw(pw, pg, pm, pv)
    shp = [small_w[n].shape for n in names]
    d_small = dict(zip(names, _unpack_rows(sd, cnt, shp)))
    m_small = dict(zip(names, _unpack_rows(sm, cnt, shp)))
    v_small = dict(zip(names, _unpack_rows(sv, cnt, shp)))

    big_names = ["w_in", "w_out", "w_gate", "w_up", "w_down"]
    g_big, d_big, m_big, v_big = {}, {}, {}, {}
    after = sd
    for k, n in enumerate(big_names):
        mine, theirs = pair_wait_one("final_wait", fin_send, fin_recv, halves[k], fin_lands[k], after, k)
        g_big[n], d_big[n], m_big[n], v_big[n] = adamw_halves(big_w[k], mine, theirs, big_m[k], big_v[k], c_idx)
        after = d_big[n]

    order = ["mix_norm_g", "w_in", "sink", "conv_dw_w", "conv_dw_b", "conv_ln_g", "conv_ln_b", "sgu_ln_g", "sgu_ln_b",
             "sgu_w", "sgu_b", "w_out", "ffn_norm_g", "w_gate", "w_up", "w_down", "final_norm_g"]
    grads = {**g_small, **g_big}
    deltas = {**d_small, **d_big}
    new_m = {**m_small, **m_big}
    new_v = {**v_small, **v_big}
    return (loss, dx[None], *[grads[n] for n in order], *[deltas[n] for n in order],
            *[new_m[n] for n in order], *[new_v[n] for n in order])
```

```python
import functools
import math

import jax
import jax.numpy as jnp
from jax import lax
from jax.experimental import pallas as pl
from jax.experimental.pallas import tpu as pltpu

F32, BF16 = jnp.float32, jnp.bfloat16

D_MODEL = 2048
DEPTH = 4
HEAD_DIM = 128
N_Q_HEADS = 8
N_KV_HEADS = 2
Q_PER_KV = N_Q_HEADS // N_KV_HEADS
ATTN_WIDTH = N_Q_HEADS * HEAD_DIM
KV_WIDTH = N_KV_HEADS * HEAD_DIM
CONV_WIDTH = 512
CONV_KERNEL = 31
CONV_PAD = 16
SGU_WIDTH = 512
SGU_HEADS = 4
CHUNK = 128
IN_WIDTH = 3584
D_FF = 5632
WINDOW = 128
ROT_DIM = 32
ROPE_THETA = 500000.0
EPS = 1e-6
N_CHIPS = 4
N_DEV = 8
LANES = 128
PACK_ROWS = 64
OFF_K = ATTN_WIDTH
OFF_V = OFF_K + KV_WIDTH
OFF_CA = OFF_V + KV_WIDTH
OFF_CG = OFF_CA + CONV_WIDTH
OFF_U = OFF_CG + CONV_WIDTH
OFF_VV = OFF_U + SGU_WIDTH

ADAM_LR, ADAM_B1, ADAM_B2, ADAM_EPS, ADAM_WD, ADAM_STEP = 0.001, 0.9, 0.999, 1e-08, 0.01, 10

VMEM_LIMIT = 56 * 1024 * 1024
MESH = pl.DeviceIdType.MESH
HBM = pl.BlockSpec(memory_space=pltpu.HBM)
VMEM_SPEC = pl.BlockSpec(memory_space=pltpu.VMEM)


def _call(name, body, *, grid, in_specs, out_specs, out_shape, scratch=(), sem=None, aliases=None):
    params = dict(vmem_limit_bytes=VMEM_LIMIT)
    if sem is not None:
        params["dimension_semantics"] = sem
    return pl.pallas_call(
        body, name=name, grid=grid, in_specs=in_specs, out_specs=out_specs, out_shape=out_shape,
        scratch_shapes=list(scratch), input_output_aliases=aliases or {}, compiler_params=pltpu.CompilerParams(**params))


def _sigmoid(x):
    return 1.0 / (1.0 + jnp.exp(-x))


def rms_fwd(x, g, token):
    S = x.shape[0]
    tm = min(512, S)

    def body(x_ref, g_ref, token_ref, o_ref):
        xv = x_ref[...]
        r = lax.rsqrt(jnp.mean(xv * xv, axis=-1, keepdims=True) + EPS)
        o_ref[...] = (xv * r * g_ref[...]).astype(BF16)

    return _call("rms_fwd", body, grid=(S // tm,),
                 in_specs=[pl.BlockSpec((tm, D_MODEL), lambda i: (i, 0)), pl.BlockSpec((1, D_MODEL), lambda i: (0, 0)),
                           pl.BlockSpec((8, LANES), lambda i: (0, 0))],
                 out_specs=pl.BlockSpec((tm, D_MODEL), lambda i: (i, 0)),
                 out_shape=jax.ShapeDtypeStruct((S, D_MODEL), BF16), sem=("parallel",))(x, g, token)


def _rms_bwd_math(xv, gv, dh):
    r = lax.rsqrt(jnp.mean(xv * xv, axis=-1, keepdims=True) + EPS)
    n = xv * r
    dn = dh * gv
    dx = r * (dn - n * jnp.mean(dn * n, axis=-1, keepdims=True))
    dg = jnp.sum(dh * n, axis=0, keepdims=True)
    return dx, dg


def rms_bwd(x, g, dh, dres):
    S = x.shape[0]
    tm = min(512, S)

    def body(x_ref, g_ref, dh_ref, dres_ref, dx_ref, dxb_ref, dg_ref):
        dx, dg = _rms_bwd_math(x_ref[...], g_ref[...], dh_ref[...].astype(F32))
        dx = dx + dres_ref[...]
        dx_ref[...] = dx
        dxb_ref[...] = dx.astype(BF16)

        @pl.when(pl.program_id(0) == 0)
        def _():
            dg_ref[...] = dg

        @pl.when(pl.program_id(0) > 0)
        def _():
            dg_ref[...] += dg

    row = pl.BlockSpec((tm, D_MODEL), lambda i: (i, 0))
    vec = pl.BlockSpec((1, D_MODEL), lambda i: (0, 0))
    return _call("rms_bwd", body, grid=(S // tm,), in_specs=[row, vec, row, row], out_specs=[row, row, vec],
                 out_shape=[jax.ShapeDtypeStruct((S, D_MODEL), F32), jax.ShapeDtypeStruct((S, D_MODEL), BF16),
                            jax.ShapeDtypeStruct((1, D_MODEL), F32)], sem=("arbitrary",))(x, g, dh, dres)


def final_loss(x, g, target):
    S = x.shape[0]
    tm = min(256, S)

    def body(x_ref, g_ref, t_ref, loss_ref, dx_ref, dxb_ref, dg_ref):
        xv = x_ref[...]
        gv = g_ref[...]
        r = lax.rsqrt(jnp.mean(xv * xv, axis=-1, keepdims=True) + EPS)
        err = xv * r * gv - t_ref[...]
        part = 0.5 * jnp.sum(jnp.mean(err * err, axis=-1, keepdims=True), axis=0, keepdims=True)
        dx, dg = _rms_bwd_math(xv, gv, err * (1.0 / D_MODEL))
        dx_ref[...] = dx
        dxb_ref[...] = dx.astype(BF16)

        @pl.when(pl.program_id(0) == 0)
        def _():
            dg_ref[...] = dg
            loss_ref[...] = part

        @pl.when(pl.program_id(0) > 0)
        def _():
            dg_ref[...] += dg
            loss_ref[...] += part

    row = pl.BlockSpec((tm, D_MODEL), lambda i: (i, 0))
    vec = pl.BlockSpec((1, D_MODEL), lambda i: (0, 0))
    one = pl.BlockSpec((1, 1), lambda i: (0, 0))
    return _call("final_loss", body, grid=(S // tm,), in_specs=[row, vec, row], out_specs=[one, row, row, vec],
                 out_shape=[jax.ShapeDtypeStruct((1, 1), F32), jax.ShapeDtypeStruct((S, D_MODEL), F32),
                            jax.ShapeDtypeStruct((S, D_MODEL), BF16), jax.ShapeDtypeStruct((1, D_MODEL), F32)],
                 sem=("arbitrary",))(x, g, target)


EPILOGUE_ROWS = 256
NN = (((1,), (0,)), ((), ()))
NT = (((1,), (1,)), ((), ()))
TN = (((0,), (0,)), ((), ()))


def _matmul(name, operands, in_specs, out_shape, out_specs, grid, pairs, dims, acc_shape, epilogue):
    n_in, n_out, nk = len(operands), len(out_shape), grid[-1]

    def body(*refs):
        ins, outs = refs[:n_in], refs[n_in:n_in + n_out]
        part = None
        for ia, ib in pairs:
            d = lax.dot_general(ins[ia][...], ins[ib][...], dims, preferred_element_type=F32)
            part = d if part is None else part + d
        if nk == 1:
            epilogue(part, ins, outs)
        else:
            acc = refs[-1]
            k = pl.program_id(len(grid) - 1)

            @pl.when(k == 0)
            def _():
                acc[...] = part

            @pl.when(k > 0)
            def _():
                acc[...] += part

            @pl.when(k == nk - 1)
            def _():
                epilogue(acc[...], ins, outs)

    scratch = [pltpu.VMEM(acc_shape, F32)] if nk > 1 else []
    sem = ("parallel",) * (len(grid) - 1) + ("arbitrary",)
    return _call(name, body, grid=grid, in_specs=in_specs, out_specs=out_specs, out_shape=out_shape,
                 scratch=scratch, sem=sem)(*operands)


def _store(dtype):
    def epilogue(acc, ins, outs):
        outs[0][...] = acc.astype(dtype)
    return epilogue


def mm_nn_cols(a, w, token):
    S, K = a.shape
    J, _, Ns = w.shape
    tm = min(512, S)
    return _matmul("mm_nn_cols", (a, w, token),
                   [pl.BlockSpec((tm, K), lambda j, i, k: (i, 0)), pl.BlockSpec((None, K, Ns), lambda j, i, k: (j, 0, 0)),
                    pl.BlockSpec((8, LANES), lambda j, i, k: (0, 0))],
                   [jax.ShapeDtypeStruct((S, J * Ns), BF16)], [pl.BlockSpec((tm, Ns), lambda j, i, k: (i, j))],
                   (J, S // tm, 1), [(0, 1)], NN, None, _store(BF16))[0]


def ffn_up(h, wg, wu, token):
    S, K = h.shape
    J, _, Ns = wg.shape
    tm = min(512, S)

    sub = min(EPILOGUE_ROWS, tm)

    def body(h_ref, wg_ref, wu_ref, token_ref, g_ref, u_ref, a_ref):
        for r in range(tm // sub):
            rows = slice(r * sub, (r + 1) * sub)
            hv = h_ref[rows, :]
            gv = jnp.dot(hv, wg_ref[...], preferred_element_type=F32)
            uv = jnp.dot(hv, wu_ref[...], preferred_element_type=F32)
            g_ref[rows, :] = gv.astype(BF16)
            u_ref[rows, :] = uv.astype(BF16)
            a_ref[rows, :] = (gv * _sigmoid(gv) * uv).astype(BF16)

    wspec = pl.BlockSpec((None, K, Ns), lambda j, i: (j, 0, 0))
    ospec = pl.BlockSpec((tm, Ns), lambda j, i: (i, j))
    oshape = jax.ShapeDtypeStruct((S, J * Ns), BF16)
    return _call("ffn_up", body, grid=(J, S // tm),
                 in_specs=[pl.BlockSpec((tm, K), lambda j, i: (i, 0)), wspec, wspec, pl.BlockSpec((8, LANES), lambda j, i: (0, 0))],
                 out_specs=[ospec, ospec, ospec], out_shape=[oshape, oshape, oshape],
                 sem=("parallel", "parallel"))(h, wg, wu, token)


def mm_nn_rows_res(a, w, res, gain):
    S, K = a.shape
    N = w.shape[1]
    tm = min(512, S)
    tk = K if K <= 2048 else K // 4

    def body(a_ref, w_ref, res_ref, g_ref, x_ref, h_ref, *acc):
        def finish(total):
            xv = total + res_ref[...]
            x_ref[...] = xv
            r = lax.rsqrt(jnp.mean(xv * xv, axis=-1, keepdims=True) + EPS)
            h_ref[...] = (xv * r * g_ref[...]).astype(BF16)

        part = jnp.dot(a_ref[...], w_ref[...], preferred_element_type=F32)
        if K == tk:
            finish(part)
        else:
            k = pl.program_id(1)

            @pl.when(k == 0)
            def _():
                acc[0][...] = part

            @pl.when(k > 0)
            def _():
                acc[0][...] += part

            @pl.when(k == K // tk - 1)
            def _():
                finish(acc[0][...])

    row = pl.BlockSpec((tm, N), lambda i, k: (i, 0))
    return _call("mm_nn_rows_res", body, grid=(S // tm, K // tk),
                 in_specs=[pl.BlockSpec((tm, tk), lambda i, k: (i, k)), pl.BlockSpec((tk, N), lambda i, k: (k, 0)), row,
                           pl.BlockSpec((1, N), lambda i, k: (0, 0))],
                 out_specs=[row, row], out_shape=[jax.ShapeDtypeStruct((S, N), F32), jax.ShapeDtypeStruct((S, N), BF16)],
                 scratch=[pltpu.VMEM((tm, N), F32)] if K != tk else [], sem=("parallel", "arbitrary"))(a, w, res, gain)


def mm_nt_cols(pairs_in, out_dtype, shards_per_step):
    dz0, w0 = pairs_in[0]
    S = dz0.shape[0]
    J, K, Ns = w0.shape
    tm = min(512, S)
    sps = shards_per_step
    operands, specs, pairs = [], [], []
    for dz, w in pairs_in:
        for s in range(sps):
            pairs.append((len(operands), len(operands) + 1))
            operands += [dz, w]
            specs += [pl.BlockSpec((tm, Ns), lambda i, j, s=s: (i, j * sps + s)),
                      pl.BlockSpec((None, K, Ns), lambda i, j, s=s: (j * sps + s, 0, 0))]
    return _matmul("mm_nt_cols%d" % len(pairs_in), tuple(operands), specs,
                   [jax.ShapeDtypeStruct((S, K), out_dtype)], [pl.BlockSpec((tm, K), lambda i, j: (i, 0))],
                   (S // tm, J // sps), pairs, NT, (tm, K), _store(out_dtype))[0]


def mm_nt_rows(dy, w, token):
    S, N = dy.shape
    K = w.shape[0]
    tm, tko = min(1024, S), 512
    return _matmul("mm_nt_rows", (dy, w, token),
                   [pl.BlockSpec((tm, N), lambda i, kk, z: (i, 0)), pl.BlockSpec((tko, N), lambda i, kk, z: (kk, 0)),
                    pl.BlockSpec((8, LANES), lambda i, kk, z: (0, 0))],
                   [jax.ShapeDtypeStruct((S, K), BF16)], [pl.BlockSpec((tm, tko), lambda i, kk, z: (i, kk))],
                   (S // tm, K // tko, 1), [(0, 1)], NT, None, _store(BF16))[0]


def ffn_down_bwd(dy, w, gate, up, token):
    S, N = dy.shape
    K = w.shape[0]
    tm, tko = min(1024, S), 512
    sub = min(EPILOGUE_ROWS, tm)

    def body(dy_ref, w_ref, g_ref, u_ref, token_ref, dg_ref, du_ref):
        for r in range(tm // sub):
            rows = slice(r * sub, (r + 1) * sub)
            dact = lax.dot_general(dy_ref[rows, :], w_ref[...], NT, preferred_element_type=F32)
            gv = g_ref[rows, :].astype(F32)
            uv = u_ref[rows, :].astype(F32)
            sg = _sigmoid(gv)
            dg_ref[rows, :] = (dact * uv * sg * (1.0 + gv * (1.0 - sg))).astype(BF16)
            du_ref[rows, :] = (dact * gv * sg).astype(BF16)

    tile = pl.BlockSpec((tm, tko), lambda i, kk: (i, kk))
    oshape = jax.ShapeDtypeStruct((S, K), BF16)
    return _call("ffn_down_bwd", body, grid=(S // tm, K // tko),
                 in_specs=[pl.BlockSpec((tm, N), lambda i, kk: (i, 0)), pl.BlockSpec((tko, N), lambda i, kk: (kk, 0)), tile, tile,
                           pl.BlockSpec((8, LANES), lambda i, kk: (0, 0))],
                 out_specs=[tile, tile], out_shape=[oshape, oshape], sem=("parallel", "parallel"))(dy, w, gate, up, token)


def mm_tn_cols(a, dz, J):
    S, M = a.shape
    Ns = dz.shape[1] // J
    tm, tk = 512, S
    return _matmul("mm_tn_cols", (a, dz),
                   [pl.BlockSpec((tk, tm), lambda j, m, k: (k, m)), pl.BlockSpec((tk, Ns), lambda j, m, k: (k, j))],
                   [jax.ShapeDtypeStruct((J, M, Ns), BF16)], [pl.BlockSpec((None, tm, Ns), lambda j, m, k: (j, m, 0))],
                   (J, M // tm, S // tk), [(0, 1)], TN, (tm, Ns), _store(BF16))[0]


def mm_tn_rows(a, dy):
    S, K = a.shape
    N = dy.shape[1]
    tm, tk = 512, min(2048, S)
    return _matmul("mm_tn_rows", (a, dy),
                   [pl.BlockSpec((tk, tm), lambda m, k: (k, m)), pl.BlockSpec((tk, N), lambda m, k: (k, 0))],
                   [jax.ShapeDtypeStruct((K, N), BF16)], [pl.BlockSpec((tm, N), lambda m, k: (m, 0))],
                   (K // tm, S // tk), [(0, 1)], TN, (tm, N), _store(BF16))[0]


def rope_tables(S):
    half = ROT_DIM // 2
    pos = jnp.arange(S, dtype=F32)
    inv = ROPE_THETA ** (-jnp.arange(0, ROT_DIM, 2, dtype=F32) / ROT_DIM)
    ang = pos[:, None] * inv[None, :]
    cos, sin = jnp.cos(ang), jnp.sin(ang)
    zeros = jnp.zeros((S, HEAD_DIM - ROT_DIM), F32)
    c = jnp.concatenate([cos, cos, jnp.ones((S, HEAD_DIM - ROT_DIM), F32)], axis=1)
    s_lo = jnp.concatenate([-sin, jnp.zeros((S, half), F32), zeros], axis=1)
    s_hi = jnp.concatenate([jnp.zeros((S, half), F32), sin, zeros], axis=1)
    return c, s_lo, s_hi


ROPE_ROWS = 512


def _rope(t, c, s_lo, s_hi):
    half = ROT_DIM // 2
    return t * c + pltpu.roll(t, HEAD_DIM - half, 1) * s_lo + pltpu.roll(t, half, 1) * s_hi


def _unrope(d, c, s_lo, s_hi):
    half = ROT_DIM // 2
    return d * c + pltpu.roll(d * s_lo, half, 1) + pltpu.roll(d * s_hi, HEAD_DIM - half, 1)


def rope_fwd(z, tabs):
    S = z.shape[0]
    T = min(ROPE_ROWS, S)

    def body(q_ref, kv_ref, c_ref, sl_ref, sh_ref, qr_ref, kp_ref, vp_ref):
        i = pl.program_id(0)

        @pl.when(i == 0)
        def _():
            zero = jnp.zeros((CHUNK, KV_WIDTH), BF16)
            kp_ref[0:CHUNK, :] = zero
            vp_ref[0:CHUNK, :] = zero
            kp_ref[S + CHUNK:S + 2 * CHUNK, :] = zero
            vp_ref[S + CHUNK:S + 2 * CHUNK, :] = zero

        c, sl, sh = c_ref[...], sl_ref[...], sh_ref[...]
        for h in range(N_Q_HEADS):
            cols = slice(h * HEAD_DIM, (h + 1) * HEAD_DIM)
            qr_ref[:, cols] = _rope(q_ref[:, cols].astype(F32), c, sl, sh).astype(BF16)
        rows = pl.ds(pl.multiple_of(CHUNK + i * T, CHUNK), T)
        for g in range(N_KV_HEADS):
            cols = slice(g * HEAD_DIM, (g + 1) * HEAD_DIM)
            kp_ref[rows, cols] = _rope(kv_ref[:, cols].astype(F32), c, sl, sh).astype(BF16)
        vp_ref[rows, :] = kv_ref[:, KV_WIDTH:2 * KV_WIDTH]

    tab = pl.BlockSpec((T, HEAD_DIM), lambda i: (i, 0))
    pad = pl.BlockSpec((S + 2 * CHUNK, KV_WIDTH), lambda i: (0, 0))
    return _call("rope_fwd", body, grid=(S // T,),
                 in_specs=[pl.BlockSpec((T, ATTN_WIDTH), lambda i: (i, 0)),
                           pl.BlockSpec((T, 2 * KV_WIDTH), lambda i: (i, OFF_K // (2 * KV_WIDTH))), tab, tab, tab],
                 out_specs=[pl.BlockSpec((T, ATTN_WIDTH), lambda i: (i, 0)), pad, pad],
                 out_shape=[jax.ShapeDtypeStruct((S, ATTN_WIDTH), BF16), jax.ShapeDtypeStruct((S + 2 * CHUNK, KV_WIDTH), BF16),
                            jax.ShapeDtypeStruct((S + 2 * CHUNK, KV_WIDTH), BF16)], sem=("arbitrary",))(z, z, *tabs)


def rope_bwd(dq, dkp, dvp, tabs):
    S = dq.shape[0]
    T = min(ROPE_ROWS, S)

    def body(dq_ref, dk_ref, dv_ref, c_ref, sl_ref, sh_ref, o_ref):
        c, sl, sh = c_ref[...], sl_ref[...], sh_ref[...]
        for h in range(N_Q_HEADS):
            cols = slice(h * HEAD_DIM, (h + 1) * HEAD_DIM)
            o_ref[:, cols] = _unrope(dq_ref[:, cols], c, sl, sh).astype(BF16)
        rows = pl.ds(pl.multiple_of(CHUNK + pl.program_id(0) * T, CHUNK), T)
        for g in range(N_KV_HEADS):
            cols = slice(g * HEAD_DIM, (g + 1) * HEAD_DIM)
            o_ref[:, OFF_K + g * HEAD_DIM:OFF_K + (g + 1) * HEAD_DIM] = _unrope(dk_ref[rows, cols], c, sl, sh).astype(BF16)
        o_ref[:, OFF_V:OFF_V + KV_WIDTH] = dv_ref[rows, :].astype(BF16)

    tab = pl.BlockSpec((T, HEAD_DIM), lambda i: (i, 0))
    pad = pl.BlockSpec((S + 2 * CHUNK, KV_WIDTH), lambda i: (0, 0))
    return _call("rope_bwd", body, grid=(S // T,),
                 in_specs=[pl.BlockSpec((T, ATTN_WIDTH), lambda i: (i, 0)), pad, pad, tab, tab, tab],
                 out_specs=pl.BlockSpec((T, OFF_CA), lambda i: (i, 0)),
                 out_shape=jax.ShapeDtypeStruct((S, IN_WIDTH), BF16), sem=("parallel",))(dq, dkp, dvp, *tabs)


STACK = Q_PER_KV * CHUNK


def _stack_heads(ref, rows):
    return jnp.concatenate([ref[rows, r * HEAD_DIM:(r + 1) * HEAD_DIM] for r in range(Q_PER_KV)], axis=0)


def _stack_sinks(s_ref):
    return jnp.concatenate([jnp.broadcast_to(s_ref[r:r + 1, 0:1], (CHUNK, 1)) for r in range(Q_PER_KV)], axis=0)


MASKED = -1e30


def _scores(q, kb):
    return lax.dot_general(q, kb, NT, preferred_element_type=F32) * (1.0 / math.sqrt(HEAD_DIM))


def _band_bias():
    row = lax.broadcasted_iota(jnp.int32, (STACK, 3 * CHUNK), 0) & (CHUNK - 1)
    col = lax.broadcasted_iota(jnp.int32, (STACK, 3 * CHUNK), 1)
    return jnp.where(jnp.abs(col - CHUNK - row) <= WINDOW, 0.0, MASKED).astype(F32)


def _edge_bias(n, S):
    kpos = (n - 1) * CHUNK + lax.broadcasted_iota(jnp.int32, (1, 3 * CHUNK), 1)
    return jnp.where((kpos >= 0) & (kpos < S), 0.0, MASKED).astype(F32)


def _softmax_sink(s, sk, bias):
    s = s + bias
    m = jnp.maximum(jnp.max(s, axis=1, keepdims=True), sk)
    e = jnp.exp(s - m)
    es = jnp.exp(sk - m)
    inv = 1.0 / (jnp.sum(e, axis=1, keepdims=True) + es)
    return e * inv, es * inv


def _block_views(i, nblk):
    ns = [i * nblk + b for b in range(nblk)]
    wins = [pl.ds(pl.multiple_of(n * CHUNK, CHUNK), 3 * CHUNK) for n in ns]
    rows = [slice(b * CHUNK, (b + 1) * CHUNK) for b in range(nblk)]
    return ns, wins, rows


def attn_fwd(qr, kp, vp, sink3):
    S = qr.shape[0]
    tq = min(2048, S)
    gw = Q_PER_KV * HEAD_DIM
    nblk = tq // CHUNK

    def body(q_ref, k_ref, v_ref, s_ref, o_ref):
        ns, wins, rows = _block_views(pl.program_id(1), nblk)
        sk = _stack_sinks(s_ref)
        band = _band_bias()
        scores = [_scores(_stack_heads(q_ref, rows[b]), k_ref[wins[b], :]) for b in range(nblk)]
        probs = [_softmax_sink(scores[b], sk, band + _edge_bias(ns[b], S))[0].astype(BF16) for b in range(nblk)]
        outs = [jnp.dot(probs[b], v_ref[wins[b], :], preferred_element_type=F32).astype(BF16) for b in range(nblk)]
        for b in range(nblk):
            for r in range(Q_PER_KV):
                o_ref[rows[b], r * HEAD_DIM:(r + 1) * HEAD_DIM] = outs[b][r * CHUNK:(r + 1) * CHUNK]

    kv = pl.BlockSpec((S + 2 * CHUNK, HEAD_DIM), lambda g, i: (0, g))
    return _call("attn_fwd", body, grid=(N_KV_HEADS, S // tq),
                 in_specs=[pl.BlockSpec((tq, gw), lambda g, i: (i, g)), kv, kv,
                           pl.BlockSpec((None, Q_PER_KV, LANES), lambda g, i: (g, 0, 0))],
                 out_specs=pl.BlockSpec((tq, gw), lambda g, i: (i, g)),
                 out_shape=jax.ShapeDtypeStruct((S, D_MODEL), BF16), sem=("parallel", "arbitrary"))(qr, kp, vp, sink3)


def attn_bwd(qr, kp, vp, sink3, dmix):
    S = qr.shape[0]
    tq = min(1024, S)
    gw = Q_PER_KV * HEAD_DIM
    scale = 1.0 / math.sqrt(HEAD_DIM)
    nblk = tq // CHUNK

    def body(q_ref, k_ref, v_ref, s_ref, do_ref, dq_ref, dk_ref, dv_ref, ds_ref):
        i = pl.program_id(1)

        @pl.when(i == 0)
        def _():
            dk_ref[...] = jnp.zeros_like(dk_ref)
            dv_ref[...] = jnp.zeros_like(dv_ref)
            ds_ref[...] = jnp.zeros_like(ds_ref)

        blocks = range(nblk)
        ns, wins, rows = _block_views(i, nblk)
        sk = _stack_sinks(s_ref)
        band = _band_bias()
        qs = [_stack_heads(q_ref, rows[b]) for b in blocks]
        dos = [_stack_heads(do_ref, rows[b]) for b in blocks]
        scores = [_scores(qs[b], k_ref[wins[b], :]) for b in blocks]
        dps = [lax.dot_general(dos[b], v_ref[wins[b], :], NT, preferred_element_type=F32) for b in blocks]
        probs = [_softmax_sink(scores[b], sk, band + _edge_bias(ns[b], S)) for b in blocks]
        deltas = [jnp.sum(probs[b][0] * dps[b], axis=1, keepdims=True) for b in blocks]
        dscs = [(probs[b][0] * (dps[b] - deltas[b]) * scale).astype(BF16) for b in blocks]
        dqs = [jnp.dot(dscs[b], k_ref[wins[b], :], preferred_element_type=F32) for b in blocks]
        dks = [lax.dot_general(dscs[b], qs[b], TN, preferred_element_type=F32) for b in blocks]
        dvs = [lax.dot_general(probs[b][0].astype(BF16), dos[b], TN, preferred_element_type=F32) for b in blocks]
        for b in blocks:
            for r in range(Q_PER_KV):
                dq_ref[rows[b], r * HEAD_DIM:(r + 1) * HEAD_DIM] = dqs[b][r * CHUNK:(r + 1) * CHUNK]
        for m in range(nblk + 2):
            parts = [(b, m - b) for b in blocks if 0 <= m - b <= 2]
            krows = pl.ds(pl.multiple_of(i * tq + m * CHUNK, CHUNK), CHUNK)
            dk_ref[krows, :] += sum(dks[b][o * CHUNK:(o + 1) * CHUNK] for b, o in parts)
            dv_ref[krows, :] += sum(dvs[b][o * CHUNK:(o + 1) * CHUNK] for b, o in parts)
        for r in range(Q_PER_KV):
            head = slice(r * CHUNK, (r + 1) * CHUNK)
            dsink = sum(jnp.sum(-probs[b][1][head] * deltas[b][head], axis=0, keepdims=True) for b in blocks)
            ds_ref[r:r + 1, :] += jnp.broadcast_to(dsink, (1, LANES))

    kv = pl.BlockSpec((S + 2 * CHUNK, HEAD_DIM), lambda g, i: (0, g))
    qspec = pl.BlockSpec((tq, gw), lambda g, i: (i, g))
    sspec = pl.BlockSpec((None, Q_PER_KV, LANES), lambda g, i: (g, 0, 0))
    padshape = jax.ShapeDtypeStruct((S + 2 * CHUNK, KV_WIDTH), F32)
    return _call("attn_bwd", body, grid=(N_KV_HEADS, S // tq),
                 in_specs=[qspec, kv, kv, sspec, qspec],
                 out_specs=[qspec, kv, kv, sspec],
                 out_shape=[jax.ShapeDtypeStruct((S, ATTN_WIDTH), F32), padshape, padshape,
                            jax.ShapeDtypeStruct((N_KV_HEADS, Q_PER_KV, LANES), F32)],
                 sem=("parallel", "arbitrary"))(qr, kp, vp, sink3, dmix)


CONV_TILE = 256


def _fill_padded(dst_ref, value, S):
    zero = jnp.zeros((CONV_PAD, LANES), F32)
    dst_ref[0:CONV_PAD, :] = zero
    dst_ref[CONV_PAD + S:2 * CONV_PAD + S, :] = zero
    dst_ref[CONV_PAD:CONV_PAD + S, :] = value


def conv_dw_fwd(z, w32, b):
    S = z.shape[0]
    T = min(CONV_TILE, S)
    lo = CONV_PAD - (CONV_KERNEL - 1) // 2

    def body(a_ref, g_ref, w_ref, b_ref, o_ref, c0_ref):
        _fill_padded(c0_ref, a_ref[...].astype(F32) * _sigmoid(g_ref[...].astype(F32)), S)

        def tile(t, carry):
            base = pl.multiple_of(t * T, T)
            acc = jnp.broadcast_to(b_ref[...], (T, LANES))
            for j in range(CONV_KERNEL):
                acc = acc + w_ref[j:j + 1, :] * c0_ref[pl.ds(base + lo + j, T), :]
            o_ref[pl.ds(base, T), :] = acc
            return carry

        lax.fori_loop(0, S // T, tile, 0)

    nca, ncg = OFF_CA // LANES, OFF_CG // LANES
    return _call("conv_dw_fwd", body, grid=(CONV_WIDTH // LANES,),
                 in_specs=[pl.BlockSpec((S, LANES), lambda cb: (0, nca + cb)), pl.BlockSpec((S, LANES), lambda cb: (0, ncg + cb)),
                           pl.BlockSpec((32, LANES), lambda cb: (0, cb)), pl.BlockSpec((1, LANES), lambda cb: (0, cb))],
                 out_specs=pl.BlockSpec((S, LANES), lambda cb: (0, cb)),
                 out_shape=jax.ShapeDtypeStruct((S, CONV_WIDTH), F32),
                 scratch=[pltpu.VMEM((S + 2 * CONV_PAD, LANES), F32)], sem=("parallel",))(z, z, w32, b)


def _ln_stats(x):
    mu = jnp.mean(x, axis=-1, keepdims=True)
    xc = x - mu
    rs = lax.rsqrt(jnp.mean(xc * xc, axis=-1, keepdims=True) + EPS)
    return xc * rs, rs


def _ln_bwd(dy, xh, rs, g):
    dxh = dy * g
    return rs * (dxh - jnp.mean(dxh, axis=-1, keepdims=True) - xh * jnp.mean(dxh * xh, axis=-1, keepdims=True))


def conv_ln_fwd(c1, g, b, mix):
    S = c1.shape[0]
    T = min(512, S)

    def body(x_ref, g_ref, b_ref, mix_ref, o_ref):
        xh, _ = _ln_stats(x_ref[...])
        y = xh * g_ref[...] + b_ref[...]
        o_ref[...] = (y * _sigmoid(y)).astype(BF16)

    row = pl.BlockSpec((T, CONV_WIDTH), lambda i: (i, 0))
    vec = pl.BlockSpec((1, CONV_WIDTH), lambda i: (0, 0))
    return _call("conv_ln_fwd", body, grid=(S // T,), in_specs=[row, vec, vec, pl.BlockSpec(memory_space=pl.ANY)],
                 out_specs=pl.BlockSpec((T, CONV_WIDTH), lambda i: (i, ATTN_WIDTH // CONV_WIDTH)),
                 out_shape=jax.ShapeDtypeStruct(mix.shape, BF16), sem=("parallel",), aliases={3: 0})(c1, g, b, mix)


def _acc_out(ref, value, step=None):
    step = pl.program_id(0) if step is None else step

    @pl.when(step == 0)
    def _():
        ref[...] = value

    @pl.when(step > 0)
    def _():
        ref[...] += value


def conv_ln_bwd(dmix, c1, g, b):
    S = c1.shape[0]
    T = min(512, S)

    def body(d_ref, x_ref, g_ref, b_ref, dx_ref, dg_ref, db_ref):
        xh, rs = _ln_stats(x_ref[...])
        gv = g_ref[...]
        y = xh * gv + b_ref[...]
        sg = _sigmoid(y)
        dy = d_ref[...].astype(F32) * sg * (1.0 + y * (1.0 - sg))
        dx_ref[...] = _ln_bwd(dy, xh, rs, gv)
        _acc_out(dg_ref, jnp.sum(dy * xh, axis=0, keepdims=True))
        _acc_out(db_ref, jnp.sum(dy, axis=0, keepdims=True))

    row = pl.BlockSpec((T, CONV_WIDTH), lambda i: (i, 0))
    vec = pl.BlockSpec((1, CONV_WIDTH), lambda i: (0, 0))
    vshape = jax.ShapeDtypeStruct((1, CONV_WIDTH), F32)
    return _call("conv_ln_bwd", body, grid=(S // T,),
                 in_specs=[pl.BlockSpec((T, CONV_WIDTH), lambda i: (i, ATTN_WIDTH // CONV_WIDTH)), row, vec, vec],
                 out_specs=[row, vec, vec], out_shape=[jax.ShapeDtypeStruct((S, CONV_WIDTH), F32), vshape, vshape],
                 sem=("arbitrary",))(dmix, c1, g, b)


def conv_dw_bwd(dc1, z, w32, dz):
    S = z.shape[0]
    T = min(CONV_TILE, S)
    half = (CONV_KERNEL - 1) // 2
    lo = CONV_PAD - half
    n_cb = CONV_WIDTH // LANES

    def body(d_ref, a_ref, g_ref, w_ref, dz_ref, o_ref, dw_ref, db_ref, c0_ref, d1_ref, wacc_ref, dg_ref):
        @pl.when(pl.program_id(1) == 0)
        def _():
            av = a_ref[...].astype(F32)
            sg = _sigmoid(g_ref[...].astype(F32))
            _fill_padded(c0_ref, av * sg, S)
            _fill_padded(d1_ref, d_ref[...], S)
            wacc_ref[...] = jnp.zeros_like(wacc_ref)

            def tile(t, carry):
                base = pl.multiple_of(t * T, T)
                d1 = d_ref[pl.ds(base, T), :]
                acc = jnp.zeros((T, LANES), F32)
                for j in range(CONV_KERNEL):
                    acc = acc + w_ref[j:j + 1, :] * d1_ref[pl.ds(base + CONV_PAD + half - j, T), :]
                    prod = d1 * c0_ref[pl.ds(base + lo + j, T), :]
                    wacc_ref[j] += jnp.sum(prod.reshape(T // 8, 8, LANES), axis=0)
                rows = pl.ds(base, T)
                a_t = a_ref[rows, :].astype(F32)
                s_t = _sigmoid(g_ref[rows, :].astype(F32))
                o_ref[rows, :] = (acc * s_t).astype(BF16)
                dg_ref[rows, :] = (acc * a_t * s_t * (1.0 - s_t)).astype(BF16)
                return carry

            lax.fori_loop(0, S // T, tile, 0)
            dw_ref[...] = jnp.sum(wacc_ref[...], axis=1)
            db_ref[...] = jnp.sum(d_ref[...], axis=0, keepdims=True)

        @pl.when(pl.program_id(1) == 1)
        def _():
            o_ref[...] = dg_ref[...]

    nca, ncg = OFF_CA // LANES, OFF_CG // LANES
    return _call("conv_dw_bwd", body, grid=(n_cb, 2),
                 in_specs=[pl.BlockSpec((S, LANES), lambda cb, j: (0, cb)), pl.BlockSpec((S, LANES), lambda cb, j: (0, nca + cb)),
                           pl.BlockSpec((S, LANES), lambda cb, j: (0, ncg + cb)), pl.BlockSpec((32, LANES), lambda cb, j: (0, cb)),
                           pl.BlockSpec(memory_space=pl.ANY)],
                 out_specs=[pl.BlockSpec((S, LANES), lambda cb, j: (0, nca + cb + n_cb * j)),
                            pl.BlockSpec((32, LANES), lambda cb, j: (0, cb)), pl.BlockSpec((1, LANES), lambda cb, j: (0, cb))],
                 out_shape=[jax.ShapeDtypeStruct(dz.shape, BF16), jax.ShapeDtypeStruct((32, CONV_WIDTH), F32),
                            jax.ShapeDtypeStruct((1, CONV_WIDTH), F32)],
                 scratch=[pltpu.VMEM((S + 2 * CONV_PAD, LANES), F32), pltpu.VMEM((S + 2 * CONV_PAD, LANES), F32),
                          pltpu.VMEM((32, 8, LANES), F32), pltpu.VMEM((S, LANES), BF16)],
                 sem=("parallel", "arbitrary"), aliases={4: 0})(dc1, z, z, w32, dz)


_INV_SQRT2 = 1.0 / math.sqrt(2.0)
_INV_SQRT2PI = 1.0 / math.sqrt(2.0 * math.pi)


def _gelu(x):
    return 0.5 * x * (1.0 + lax.erf(x * _INV_SQRT2))


def _gelu_grad(x):
    return 0.5 * (1.0 + lax.erf(x * _INV_SQRT2)) + x * jnp.exp(-0.5 * x * x) * _INV_SQRT2PI


def sgu_fwd(z, g, b, ws, bs, mix):
    S = z.shape[0]
    T = min(512, S)

    def body(u_ref, v_ref, g_ref, b_ref, ws_ref, bs_ref, mix_ref, o_ref):
        xh, _ = _ln_stats(_gelu(v_ref[...].astype(F32)))
        vn = (xh * g_ref[...] + b_ref[...]).astype(BF16)
        for ch in range(T // CHUNK):
            rows = slice(ch * CHUNK, (ch + 1) * CHUNK)
            for h in range(SGU_HEADS):
                cols = slice(h * HEAD_DIM, (h + 1) * HEAD_DIM)
                sp = jnp.dot(ws_ref[h], vn[rows, cols], preferred_element_type=F32) + bs_ref[h]
                o_ref[rows, cols] = (_gelu(u_ref[rows, cols].astype(F32)) * sp).astype(BF16)

    vec = pl.BlockSpec((1, SGU_WIDTH), lambda i: (0, 0))
    full = pl.BlockSpec((SGU_HEADS, CHUNK, CHUNK), lambda i: (0, 0, 0))
    return _call("sgu_fwd", body, grid=(S // T,),
                 in_specs=[pl.BlockSpec((T, SGU_WIDTH), lambda i: (i, OFF_U // SGU_WIDTH)),
                           pl.BlockSpec((T, SGU_WIDTH), lambda i: (i, OFF_VV // SGU_WIDTH)), vec, vec, full, full,
                           pl.BlockSpec(memory_space=pl.ANY)],
                 out_specs=pl.BlockSpec((T, SGU_WIDTH), lambda i: (i, (ATTN_WIDTH + CONV_WIDTH) // SGU_WIDTH)),
                 out_shape=jax.ShapeDtypeStruct(mix.shape, BF16), sem=("parallel",), aliases={6: 0})(z, z, g, b, ws, bs, mix)


def sgu_bwd(z, dmix, g, b, ws, bs, dz):
    S = z.shape[0]
    T = min(512, S)

    def body(u_ref, v_ref, d_ref, g_ref, b_ref, ws_ref, bs_ref, dz_ref, o_ref, dws_ref, dbs_ref, dg_ref, db_ref, dvn_ref, dv_ref):
        tile = pl.program_id(0)
        first = pl.program_id(1) == 0

        @pl.when(first & (tile == 0))
        def _():
            dws_ref[...] = jnp.zeros_like(dws_ref)
            dbs_ref[...] = jnp.zeros_like(dbs_ref)

        @pl.when(first)
        def _():
            vraw = v_ref[...].astype(F32)
            xh, rs = _ln_stats(_gelu(vraw))
            gv = g_ref[...]
            vn = (xh * gv + b_ref[...]).astype(BF16)
            for ch in range(T // CHUNK):
                rows = slice(ch * CHUNK, (ch + 1) * CHUNK)
                for h in range(SGU_HEADS):
                    cols = slice(h * HEAD_DIM, (h + 1) * HEAD_DIM)
                    w = ws_ref[h]
                    vb = vn[rows, cols]
                    sp = jnp.dot(w, vb, preferred_element_type=F32) + bs_ref[h]
                    uraw = u_ref[rows, cols].astype(F32)
                    dout = d_ref[rows, cols].astype(F32)
                    o_ref[rows, cols] = (dout * sp * _gelu_grad(uraw)).astype(BF16)
                    dsp = dout * _gelu(uraw)
                    dspb = dsp.astype(BF16)
                    dvn_ref[rows, cols] = lax.dot_general(w, dspb, TN, preferred_element_type=F32)
                    dws_ref[h] += lax.dot_general(dspb, vb, NT, preferred_element_type=F32)
                    dbs_ref[h] += jnp.sum(dsp, axis=1, keepdims=True)
            dvn = dvn_ref[...]
            dv_ref[...] = (_ln_bwd(dvn, xh, rs, gv) * _gelu_grad(vraw)).astype(BF16)
            _acc_out(dg_ref, jnp.sum(dvn * xh, axis=0, keepdims=True), tile)
            _acc_out(db_ref, jnp.sum(dvn, axis=0, keepdims=True), tile)

        @pl.when(pl.program_id(1) == 1)
        def _():
            o_ref[...] = dv_ref[...]

    vec = pl.BlockSpec((1, SGU_WIDTH), lambda i, j: (0, 0))
    full = pl.BlockSpec((SGU_HEADS, CHUNK, CHUNK), lambda i, j: (0, 0, 0))
    vshape = jax.ShapeDtypeStruct((1, SGU_WIDTH), F32)
    return _call("sgu_bwd", body, grid=(S // T, 2),
                 in_specs=[pl.BlockSpec((T, SGU_WIDTH), lambda i, j: (i, OFF_U // SGU_WIDTH)),
                           pl.BlockSpec((T, SGU_WIDTH), lambda i, j: (i, OFF_VV // SGU_WIDTH)),
                           pl.BlockSpec((T, SGU_WIDTH), lambda i, j: (i, (ATTN_WIDTH + CONV_WIDTH) // SGU_WIDTH)), vec, vec, full, full,
                           pl.BlockSpec(memory_space=pl.ANY)],
                 out_specs=[pl.BlockSpec((T, SGU_WIDTH), lambda i, j: (i, OFF_U // SGU_WIDTH + j)), full,
                            pl.BlockSpec((SGU_HEADS, CHUNK, 1), lambda i, j: (0, 0, 0)), vec, vec],
                 out_shape=[jax.ShapeDtypeStruct(dz.shape, BF16), jax.ShapeDtypeStruct((SGU_HEADS, CHUNK, CHUNK), F32),
                            jax.ShapeDtypeStruct((SGU_HEADS, CHUNK, 1), F32), vshape, vshape],
                 scratch=[pltpu.VMEM((T, SGU_WIDTH), F32), pltpu.VMEM((T, SGU_WIDTH), BF16)],
                 sem=("arbitrary", "arbitrary"), aliases={7: 0})(z, z, dmix, g, b, ws, bs, dz)


def _row_tile(rows, cols, n_arrays, budget_mib=24):
    budget = (budget_mib * 1024 * 1024) // (n_arrays * 2 * 4 * cols)
    t = min(rows, max(16, budget // 16 * 16))
    while rows % t:
        t -= 16
    return t


def add_sibling_half(grad, recv, c_idx):
    J, R, C = grad.shape
    hr = R // 2
    tr = _row_tile(hr, C, 3)
    nb = hr // tr

    def body(c_ref, g_ref, r_ref, o_ref):
        o_ref[...] = (g_ref[...].astype(F32) + r_ref[...].astype(F32)).astype(BF16)

    grid_spec = pltpu.PrefetchScalarGridSpec(
        num_scalar_prefetch=1, grid=(J, nb),
        in_specs=[pl.BlockSpec((None, tr, C), lambda j, i, c: (j, c[0] * nb + i, 0)),
                  pl.BlockSpec((None, tr, C), lambda j, i, c: (j, i, 0))],
        out_specs=pl.BlockSpec((None, tr, C), lambda j, i, c: (j, i, 0)))
    return pl.pallas_call(body, name="add_sibling_half", grid_spec=grid_spec,
                          out_shape=jax.ShapeDtypeStruct((J, hr, C), BF16),
                          compiler_params=pltpu.CompilerParams(vmem_limit_bytes=VMEM_LIMIT,
                                                               dimension_semantics=("parallel", "parallel")))(c_idx, grad, recv)


def sum_chips(own, others, stack, x_idx, y_idx, layer):
    R, C = own.shape[1:]
    tr = _row_tile(R, C, 4)

    def body(x_ref, y_ref, own_ref, oth_ref, stack_ref, o_ref):
        acc = own_ref[...].astype(F32)
        for j in range(3):
            acc = acc + oth_ref[j].astype(F32)
        o_ref[...] = acc

    grid_spec = pltpu.PrefetchScalarGridSpec(
        num_scalar_prefetch=2, grid=(R // tr,),
        in_specs=[pl.BlockSpec((None, tr, C), lambda i, xr, yr: (2 * xr[0] + yr[0], i, 0)),
                  pl.BlockSpec((3, tr, C), lambda i, xr, yr: (0, i, 0)),
                  pl.BlockSpec(memory_space=pl.ANY)],
        out_specs=pl.BlockSpec((None, tr, C), lambda i, xr, yr: (layer, i, 0)))
    return pl.pallas_call(body, name="sum_chips", grid_spec=grid_spec,
                          out_shape=jax.ShapeDtypeStruct(stack.shape, F32), input_output_aliases={4: 0},
                          compiler_params=pltpu.CompilerParams(vmem_limit_bytes=VMEM_LIMIT,
                                                               dimension_semantics=("parallel",)))(x_idx, y_idx, own, others, stack)


def adamw_halves(w, mine, theirs, m, v, c_idx):
    L, R, C = w.shape
    hr = R // 2
    tr = _row_tile(hr, C, 9, budget_mib=40)
    nb = hr // tr

    def body(c_ref, w_ref, a_ref, b_ref, m_ref, v_ref, g_ref, d_ref, nm_ref, nv_ref):
        gv = jnp.where(pl.program_id(1) == c_ref[0], a_ref[...], b_ref[...])
        g_ref[...] = gv
        nm = ADAM_B1 * m_ref[...] + (1.0 - ADAM_B1) * gv
        nv = ADAM_B2 * v_ref[...] + (1.0 - ADAM_B2) * (gv * gv)
        m_hat = nm / (1.0 - ADAM_B1 ** ADAM_STEP)
        v_hat = nv / (1.0 - ADAM_B2 ** ADAM_STEP)
        d_ref[...] = -ADAM_LR * (m_hat / (jnp.sqrt(v_hat) + ADAM_EPS) + ADAM_WD * w_ref[...])
        nm_ref[...] = nm
        nv_ref[...] = nv

    full = pl.BlockSpec((None, tr, C), lambda l, h, i, c: (l, h * nb + i, 0))
    a_spec = pl.BlockSpec((None, tr, C), lambda l, h, i, c: (l, jnp.where(h == c[0], i, 0), 0))
    b_spec = pl.BlockSpec((None, tr, C), lambda l, h, i, c: (l, jnp.where(h == c[0], 0, i), 0))
    grid_spec = pltpu.PrefetchScalarGridSpec(num_scalar_prefetch=1, grid=(L, 2, nb),
                                             in_specs=[full, a_spec, b_spec, full, full], out_specs=[full] * 4)
    shape = jax.ShapeDtypeStruct((L, R, C), F32)
    return pl.pallas_call(body, name="adamw_halves", grid_spec=grid_spec, out_shape=[shape] * 4,
                          compiler_params=pltpu.CompilerParams(vmem_limit_bytes=VMEM_LIMIT,
                                                               dimension_semantics=("parallel", "arbitrary", "arbitrary")))(
        c_idx, w, mine, theirs, m, v)


def adamw(w, g, m, v):
    R, C = w.shape
    tr = _row_tile(R, C, 7)

    def body(w_ref, g_ref, m_ref, v_ref, d_ref, nm_ref, nv_ref):
        gv = g_ref[...]
        nm = ADAM_B1 * m_ref[...] + (1.0 - ADAM_B1) * gv
        nv = ADAM_B2 * v_ref[...] + (1.0 - ADAM_B2) * (gv * gv)
        m_hat = nm / (1.0 - ADAM_B1 ** ADAM_STEP)
        v_hat = nv / (1.0 - ADAM_B2 ** ADAM_STEP)
        d_ref[...] = -ADAM_LR * (m_hat / (jnp.sqrt(v_hat) + ADAM_EPS) + ADAM_WD * w_ref[...])
        nm_ref[...] = nm
        nv_ref[...] = nv

    spec = pl.BlockSpec((tr, C), lambda i: (i, 0))
    shape = jax.ShapeDtypeStruct((R, C), F32)
    return _call("adamw", body, grid=(R // tr,), in_specs=[spec] * 4, out_specs=[spec] * 3, out_shape=[shape] * 3,
                 sem=("parallel",))(w, g, m, v)


def _place():
    x, y, c = lax.axis_index("x"), lax.axis_index("y"), lax.axis_index("c")
    chips = [(1 - x, y), (x, 1 - y), (1 - x, 1 - y)]
    return x, y, c, chips


def _remote(src, dst, send_sem, recv_sem, dev):
    return pltpu.make_async_remote_copy(src_ref=src, dst_ref=dst, send_sem=send_sem, recv_sem=recv_sem,
                                        device_id=dev, device_id_type=MESH)


EFFECT = pltpu.SideEffectType.DATAFLOW_SIDE_EFFECTING
SEM = pl.BlockSpec(memory_space=pltpu.SEMAPHORE)
ANY = pl.BlockSpec(memory_space=pl.ANY)
TOKEN = jax.ShapeDtypeStruct((8, LANES), F32)


def _in_hbm(a):
    return pltpu.with_memory_space_constraint(a, pltpu.HBM)


def _gather_copies(shards, lands, send_sems, recv_sems):
    x, y, c, chips = _place()
    me = 2 * x + y
    copies = []
    for k in range(len(shards)):
        hr = shards[k].shape[0] // 2
        mine = pl.ds(pl.multiple_of(c * hr, 8), hr)
        for t, (px, py) in enumerate(chips):
            copies.append(_remote(shards[k].at[mine, :], lands[k].at[me, mine, :], send_sems.at[4 * k + t], recv_sems.at[4 * k + t],
                                  (px, py, c)))
        copies.append(_remote(shards[k], lands[k].at[me], send_sems.at[4 * k + 3], recv_sems.at[4 * k + 3], (x, y, 1 - c)))
    return copies


def _gather_landings(lands, send_sems, recv_sems):
    x, y, c, chips = _place()
    me = 2 * x + y
    landings = []
    for k in range(len(lands)):
        hr = lands[k].shape[1] // 2
        mine = pl.ds(pl.multiple_of(c * hr, 8), hr)
        for t, (px, py) in enumerate(chips):
            dst = lands[k].at[2 * px + py, mine, :]
            landings.append(_remote(dst, dst, send_sems.at[4 * k + t], recv_sems.at[4 * k + t], (px, py, c)))
        dst = lands[k].at[me]
        landings.append(_remote(dst, dst, send_sems.at[4 * k + 3], recv_sems.at[4 * k + 3], (x, y, 1 - c)))
    return landings


def gather_start(shards, after):
    n = len(shards)

    def body(*refs):
        srcs, lands_in = refs[:n], refs[n:2 * n]
        send_sems, recv_sems = refs[2 * n + 1], refs[2 * n + 2]
        token = refs[-1]
        for cp in _gather_copies(srcs, lands_in, send_sems, recv_sems):
            cp.start()
        token[...] = jnp.zeros_like(token)

    lands = [lax.empty((N_CHIPS,) + s.shape, s.dtype) for s in shards]
    outs = pl.pallas_call(
        body, name="gather_start", in_specs=[HBM] * (2 * n) + [ANY],
        out_specs=[SEM, SEM] + [HBM] * (2 * n) + [VMEM_SPEC],
        out_shape=[pltpu.SemaphoreType.DMA((4 * n,)), pltpu.SemaphoreType.DMA((4 * n,))]
        + [pltpu.HBM(s.shape, s.dtype) for s in shards] + [pltpu.HBM(l.shape, l.dtype) for l in lands] + [TOKEN],
        input_output_aliases={i: 2 + i for i in range(2 * n)},
        compiler_params=pltpu.CompilerParams(has_side_effects=EFFECT),
    )(*[_in_hbm(s) for s in shards], *[_in_hbm(l) for l in lands], after)
    return outs[0], outs[1], outs[2:2 + n], outs[2 + n:2 + 2 * n], outs[-1]


def gather_wait(send_sems, recv_sems, shards, lands, after):
    n = len(shards)

    def body(*refs):
        srcs, lands_in = refs[:n], refs[n:2 * n]
        send, recv = refs[2 * n], refs[2 * n + 1]
        for cp in _gather_copies(srcs, lands_in, send, recv):
            cp.wait_send()
        for cp in _gather_landings(lands_in, send, recv):
            cp.wait_recv()

    outs = pl.pallas_call(
        body, name="gather_wait", in_specs=[HBM] * (2 * n) + [SEM, SEM, ANY], out_specs=[HBM] * (2 * n),
        out_shape=[pltpu.HBM(s.shape, s.dtype) for s in shards] + [pltpu.HBM(l.shape, l.dtype) for l in lands],
        input_output_aliases={i: i for i in range(2 * n)},
        compiler_params=pltpu.CompilerParams(has_side_effects=EFFECT),
    )(*shards, *lands, send_sems, recv_sems, after)
    return outs[n:]


def _forward_copies(lands, send_sems, recv_sems, received):
    x, y, c, chips = _place()
    copies = []
    for k in range(len(lands)):
        hr = lands[k].shape[1] // 2
        half = (1 - c) if received else c
        rows = pl.ds(pl.multiple_of(half * hr, 8), hr)
        for t, (px, py) in enumerate(chips):
            block = lands[k].at[2 * px + py, rows, :]
            copies.append(_remote(block, block, send_sems.at[3 * k + t], recv_sems.at[3 * k + t], (x, y, 1 - c)))
    return copies


def forward_start(lands):
    n = len(lands)

    def body(*refs):
        for cp in _forward_copies(refs[:n], refs[n], refs[n + 1], received=False):
            cp.start()
        refs[-1][...] = jnp.zeros_like(refs[-1])

    outs = pl.pallas_call(
        body, name="forward_start", in_specs=[HBM] * n, out_specs=[SEM, SEM] + [HBM] * n + [VMEM_SPEC],
        out_shape=[pltpu.SemaphoreType.DMA((3 * n,)), pltpu.SemaphoreType.DMA((3 * n,))]
        + [pltpu.HBM(l.shape, l.dtype) for l in lands] + [TOKEN],
        input_output_aliases={i: 2 + i for i in range(n)},
        compiler_params=pltpu.CompilerParams(has_side_effects=EFFECT),
    )(*[_in_hbm(l) for l in lands])
    return outs[0], outs[1], outs[2:2 + n], outs[-1]


def forward_wait(send_sems, recv_sems, lands, after):
    n = len(lands)

    def body(*refs):
        for cp in _forward_copies(refs[:n], refs[n], refs[n + 1], received=False):
            cp.wait_send()
        for cp in _forward_copies(refs[:n], refs[n], refs[n + 1], received=True):
            cp.wait_recv()

    return pl.pallas_call(
        body, name="forward_wait", in_specs=[HBM] * n + [SEM, SEM, ANY], out_specs=[HBM] * n,
        out_shape=[pltpu.HBM(l.shape, l.dtype) for l in lands],
        input_output_aliases={i: i for i in range(n)},
        compiler_params=pltpu.CompilerParams(has_side_effects=EFFECT),
    )(*lands, send_sems, recv_sems, after)


def forward_halves(lands):
    n = len(lands)

    def body(*refs):
        ins, outs = refs[:n], refs[n:2 * n]
        send_sems, recv_sems = refs[2 * n:]
        x, y, c, chips = _place()
        sibling = (x, y, 1 - c)
        sends = []
        for k in range(n):
            hr = ins[k].shape[1] // 2
            mine = pl.ds(pl.multiple_of(c * hr, 8), hr)
            for t, (px, py) in enumerate(chips):
                cp = _remote(ins[k].at[2 * px + py, mine, :], outs[k].at[2 * px + py, mine, :],
                             send_sems.at[k, t], recv_sems.at[k, t], sibling)
                cp.start()
                sends.append(cp)
        for k in range(n):
            hr = ins[k].shape[1] // 2
            other = pl.ds(pl.multiple_of((1 - c) * hr, 8), hr)
            for t, (px, py) in enumerate(chips):
                dst = outs[k].at[2 * px + py, other, :]
                _remote(dst, dst, send_sems.at[k, t], recv_sems.at[k, t], sibling).wait_recv()
        for cp in sends:
            cp.wait_send()

    return pl.pallas_call(
        body, name="forward_halves", in_specs=[HBM] * n, out_specs=[HBM] * n,
        out_shape=[jax.ShapeDtypeStruct(l.shape, l.dtype) for l in lands],
        input_output_aliases={i: i for i in range(n)},
        scratch_shapes=[pltpu.SemaphoreType.DMA((n, 3)), pltpu.SemaphoreType.DMA((n, 3))],
    )(*lands)


def gather_small(block):
    def body(in_ref, out_ref, send_sems, recv_sems):
        x, y, c, chips = _place()
        me = 2 * x + y
        out_ref[me] = in_ref[...]
        sends = []
        for t, (px, py) in enumerate(chips):
            cp = _remote(in_ref, out_ref.at[me], send_sems.at[t], recv_sems.at[t], (px, py, c))
            cp.start()
            sends.append(cp)
        for t, (px, py) in enumerate(chips):
            landed = out_ref.at[2 * px + py]
            _remote(landed, landed, send_sems.at[t], recv_sems.at[t], (px, py, c)).wait_recv()
        for cp in sends:
            cp.wait_send()

    return pl.pallas_call(
        body, name="gather_small", in_specs=[VMEM_SPEC], out_specs=VMEM_SPEC,
        out_shape=jax.ShapeDtypeStruct((N_CHIPS,) + block.shape, block.dtype),
        scratch_shapes=[pltpu.SemaphoreType.DMA((3,)), pltpu.SemaphoreType.DMA((3,))],
    )(block)


def exchange_sibling_halves(grads):
    n = len(grads)

    def body(*refs):
        ins, outs = refs[:n], refs[n:2 * n]
        send_sems, recv_sems = refs[2 * n:]
        x, y, c, _ = _place()
        copies = []
        for k in range(n):
            hr = ins[k].shape[1] // 2
            theirs = pl.ds(pl.multiple_of((1 - c) * hr, 8), hr)
            cp = _remote(ins[k].at[:, theirs, :], outs[k], send_sems.at[k], recv_sems.at[k], (x, y, 1 - c))
            cp.start()
            copies.append(cp)
        for cp in copies:
            cp.wait()

    return pl.pallas_call(
        body, name="exchange_sibling_halves", in_specs=[HBM] * n, out_specs=[HBM] * n,
        out_shape=[jax.ShapeDtypeStruct((g.shape[0], g.shape[1] // 2, g.shape[2]), g.dtype) for g in grads],
        scratch_shapes=[pltpu.SemaphoreType.DMA((n,)), pltpu.SemaphoreType.DMA((n,))],
    )(*grads)


def _sibling_half_copies(grads, lands, send_sems, recv_sems):
    x, y, c, _ = _place()
    copies = []
    for k in range(len(grads)):
        hr = grads[k].shape[1] // 2
        theirs = pl.ds(pl.multiple_of((1 - c) * hr, 8), hr)
        copies.append(_remote(grads[k].at[:, theirs, :], lands[k], send_sems.at[k], recv_sems.at[k], (x, y, 1 - c)))
    return copies


def _sibling_whole_copies(srcs, lands, send_sems, recv_sems):
    x, y, c, _ = _place()
    return [_remote(srcs[k], lands[k], send_sems.at[k], recv_sems.at[k], (x, y, 1 - c)) for k in range(len(srcs))]


def pair_start(name, make_copies, srcs, land_shapes):
    n = len(srcs)

    def body(*refs):
        src_refs, land_refs = refs[:n], refs[n:2 * n]
        send_sems, recv_sems = refs[2 * n], refs[2 * n + 1]
        token = refs[-1]
        for cp in make_copies(src_refs, land_refs, send_sems, recv_sems):
            cp.start()
        token[...] = jnp.zeros_like(token)

    lands = [lax.empty(shape, s.dtype) for shape, s in zip(land_shapes, srcs)]
    outs = pl.pallas_call(
        body, name=name, in_specs=[HBM] * (2 * n), out_specs=[SEM, SEM] + [HBM] * (2 * n) + [VMEM_SPEC],
        out_shape=[pltpu.SemaphoreType.DMA((n,)), pltpu.SemaphoreType.DMA((n,))]
        + [pltpu.HBM(s.shape, s.dtype) for s in srcs] + [pltpu.HBM(l.shape, l.dtype) for l in lands] + [TOKEN],
        input_output_aliases={i: 2 + i for i in range(2 * n)},
        compiler_params=pltpu.CompilerParams(has_side_effects=EFFECT),
    )(*[_in_hbm(s) for s in srcs], *[_in_hbm(l) for l in lands])
    return outs[0], outs[1], outs[2:2 + n], outs[2 + n:2 + 2 * n], outs[-1]


def pair_wait_one(name, send_sems, recv_sems, src, land, after, index):
    def body(src_ref, land_ref, send, recv, after_ref, src_out, land_out):
        x, y, c, _ = _place()
        cp = _remote(src_ref, land_ref, send.at[index], recv.at[index], (x, y, 1 - c))
        cp.wait_send()
        cp.wait_recv()

    return pl.pallas_call(
        body, name=name, in_specs=[HBM, HBM, SEM, SEM, ANY], out_specs=[HBM, HBM],
        out_shape=[pltpu.HBM(src.shape, src.dtype), pltpu.HBM(land.shape, land.dtype)],
        input_output_aliases={0: 0, 1: 1},
        compiler_params=pltpu.CompilerParams(has_side_effects=EFFECT),
    )(src, land, send_sems, recv_sems, after)


def pair_wait(name, make_copies, send_sems, recv_sems, srcs, lands, after):
    n = len(srcs)

    def body(*refs):
        src_refs, land_refs = refs[:n], refs[n:2 * n]
        for cp in make_copies(src_refs, land_refs, refs[2 * n], refs[2 * n + 1]):
            cp.wait_send()
            cp.wait_recv()

    outs = pl.pallas_call(
        body, name=name, in_specs=[HBM] * (2 * n) + [SEM, SEM, ANY], out_specs=[HBM] * (2 * n),
        out_shape=[pltpu.HBM(s.shape, s.dtype) for s in srcs] + [pltpu.HBM(l.shape, l.dtype) for l in lands],
        input_output_aliases={i: i for i in range(2 * n)},
        compiler_params=pltpu.CompilerParams(has_side_effects=EFFECT),
    )(*srcs, *lands, send_sems, recv_sems, after)
    return outs[:n], outs[n:]


def _chip_copies(parts, lands, send_sems, recv_sems):
    x, y, c, chips = _place()
    return [_remote(parts[k].at[2 * px + py], lands[k].at[t], send_sems.at[3 * k + t], recv_sems.at[3 * k + t], (px, py, c))
            for k in range(len(parts)) for t, (px, py) in enumerate(chips)]


def chip_parts_start(parts):
    n = len(parts)

    def body(*refs):
        srcs, lands_in = refs[:n], refs[n:2 * n]
        send_sems, recv_sems = refs[2 * n], refs[2 * n + 1]
        token = refs[-1]
        for cp in _chip_copies(srcs, lands_in, send_sems, recv_sems):
            cp.start()
        token[...] = jnp.zeros_like(token)

    lands = [lax.empty((3,) + p.shape[1:], p.dtype) for p in parts]
    outs = pl.pallas_call(
        body, name="chip_parts_start", in_specs=[HBM] * (2 * n), out_specs=[SEM, SEM] + [HBM] * (2 * n) + [VMEM_SPEC],
        out_shape=[pltpu.SemaphoreType.DMA((3 * n,)), pltpu.SemaphoreType.DMA((3 * n,))]
        + [pltpu.HBM(p.shape, p.dtype) for p in parts] + [pltpu.HBM(l.shape, l.dtype) for l in lands] + [TOKEN],
        input_output_aliases={i: 2 + i for i in range(2 * n)},
        compiler_params=pltpu.CompilerParams(has_side_effects=EFFECT),
    )(*[_in_hbm(p) for p in parts], *[_in_hbm(l) for l in lands])
    return outs[0], outs[1], outs[2:2 + n], outs[2 + n:2 + 2 * n], outs[-1]


def chip_parts_wait(send_sems, recv_sems, parts, lands, after):
    n = len(parts)

    def body(*refs):
        srcs, lands_in = refs[:n], refs[n:2 * n]
        send, recv = refs[2 * n], refs[2 * n + 1]
        for cp in _chip_copies(srcs, lands_in, send, recv):
            cp.wait_send()
            cp.wait_recv()

    outs = pl.pallas_call(
        body, name="chip_parts_wait", in_specs=[HBM] * (2 * n) + [SEM, SEM, ANY], out_specs=[HBM] * (2 * n),
        out_shape=[pltpu.HBM(p.shape, p.dtype) for p in parts] + [pltpu.HBM(l.shape, l.dtype) for l in lands],
        input_output_aliases={i: i for i in range(2 * n)},
        compiler_params=pltpu.CompilerParams(has_side_effects=EFFECT),
    )(*parts, *lands, send_sems, recv_sems, after)
    return outs[:n], outs[n:]


def allreduce_small(packed):
    R = packed.shape[0]

    def body(x_ref, sum_ref, all_ref, send_sems, recv_sems):
        x, y, c, chips = _place()
        me, sibling = (x, y, c), (x, y, 1 - c)

        def rows(px, py, pc):
            return all_ref.at[4 * px + 2 * py + pc]

        def copy(k, block, to, src=None):
            return _remote(rows(*block) if src is None else src, rows(*block), send_sems.at[k], recv_sems.at[k], to)

        all_ref[4 * x + 2 * y + c] = x_ref[...]
        first = [copy(0, me, sibling, src=x_ref)]
        first += [copy(1 + j, me, (*chip, c), src=x_ref) for j, chip in enumerate(chips)]
        for cp in first:
            cp.start()
        passed = [copy(4 + j, (*chip, c), sibling) for j, chip in enumerate(chips)]
        for j, chip in enumerate(chips):
            copy(1 + j, (*chip, c), me).wait_recv()
            passed[j].start()
        copy(0, sibling, me).wait_recv()
        for j, chip in enumerate(chips):
            copy(4 + j, (*chip, 1 - c), me).wait_recv()
        for cp in first + passed:
            cp.wait_send()

        def chunk(i, carry):
            rws = pl.ds(pl.multiple_of(i * PACK_ROWS, PACK_ROWS), PACK_ROWS)
            acc = all_ref[0, rws, :]
            for d in range(1, N_DEV):
                acc = acc + all_ref[d, rws, :]
            sum_ref[rws, :] = acc
            return carry

        lax.fori_loop(0, R // PACK_ROWS, chunk, 0)

    return pl.pallas_call(
        body, name="allreduce_small", in_specs=[VMEM_SPEC], out_specs=VMEM_SPEC,
        out_shape=jax.ShapeDtypeStruct((R, LANES), F32),
        scratch_shapes=[pltpu.VMEM((N_DEV, R, LANES), F32), pltpu.SemaphoreType.DMA((7,)), pltpu.SemaphoreType.DMA((7,))],
        compiler_params=pltpu.CompilerParams(vmem_limit_bytes=VMEM_LIMIT),
    )(packed)


def _mixer_fwd(x, h, p, tabs, token, late_weights=None):
    z = mm_nn_cols(h, p["w_in"], token)
    qr, kp, vp = rope_fwd(z, tabs)
    mix = attn_fwd(qr, kp, vp, p["sink3"])
    c1 = conv_dw_fwd(z, p["conv_w32"], p["conv_dw_b"])
    mix = conv_ln_fwd(c1, p["conv_ln_g"], p["conv_ln_b"], mix)
    mix = sgu_fwd(z, p["sgu_ln_g"], p["sgu_ln_b"], p["sgu_w16"], p["sgu_b3"], mix)
    if late_weights is not None:
        p.update(late_weights(mix))
    x_mid, h2 = mm_nn_rows_res(mix, p["w_out"], x, p["ffn_norm_g"])
    return x_mid, h2, dict(x=x, h=h, z=z, qr=qr, kp=kp, vp=vp, c1=c1, mix=mix, x_mid=x_mid)


def _ffn_fwd(x_mid, h2, p, next_gain, token):
    gate, up, act = ffn_up(h2, p["w_gate"], p["w_up"], token)
    x_out, h_next = mm_nn_rows_res(act, p["w_down"], x_mid, next_gain)
    return x_out, h_next, dict(h2=h2, gate=gate, up=up, act=act)


def _layer_fwd(x, h, p, next_gain, tabs, token):
    x_mid, h2, s_mix = _mixer_fwd(x, h, p, tabs, token)
    x_out, h_next, s_ffn = _ffn_fwd(x_mid, h2, p, next_gain, token)
    return x_out, h_next, {**s_mix, **s_ffn}


def _ffn_bwd(dx, dxb, p, s, token):
    dgate, dup = ffn_down_bwd(dxb, p["w_down"], s["gate"], s["up"], token)
    g_down = mm_tn_rows(s["act"], dxb)
    dh2 = mm_nt_cols([(dgate, p["w_gate"]), (dup, p["w_up"])], BF16, 1)
    g_gate = mm_tn_cols(s["h2"], dgate, N_CHIPS)
    g_up = mm_tn_cols(s["h2"], dup, N_CHIPS)
    dmid, dmidb, g_ffn_norm = rms_bwd(s["x_mid"], p["ffn_norm_g"], dh2, dx)
    return dmid, dmidb, [g_gate, g_up, g_down.reshape(N_CHIPS, -1, D_MODEL)], g_ffn_norm


def _mixer_bwd(dmid, dmidb, p, s, tabs, token):
    dmix = mm_nt_rows(dmidb, p["w_out"], token)
    g_out = mm_tn_rows(s["mix"], dmidb)
    dq, dkp, dvp, dsink = attn_bwd(s["qr"], s["kp"], s["vp"], p["sink3"], dmix)
    dz = rope_bwd(dq, dkp, dvp, tabs)
    dc1, g_cln_g, g_cln_b = conv_ln_bwd(dmix, s["c1"], p["conv_ln_g"], p["conv_ln_b"])
    dz, g_cw, g_cb = conv_dw_bwd(dc1, s["z"], p["conv_w32"], dz)
    dz, g_sw, g_sb, g_sln_g, g_sln_b = sgu_bwd(s["z"], dmix, p["sgu_ln_g"], p["sgu_ln_b"], p["sgu_w16"], p["sgu_b3"], dz)
    dh = mm_nt_cols([(dz, p["w_in"])], BF16, N_CHIPS)
    g_in = mm_tn_cols(s["h"], dz, N_CHIPS)
    dx_in, dxb_in, g_mix_norm = rms_bwd(s["x"], p["mix_norm_g"], dh, dmid)
    small = dict(mix_norm_g=g_mix_norm, sink=dsink[:, :, 0].reshape(1, N_Q_HEADS), conv_dw_w=g_cw[:CONV_KERNEL],
                 conv_dw_b=g_cb, conv_ln_g=g_cln_g, conv_ln_b=g_cln_b, sgu_ln_g=g_sln_g, sgu_ln_b=g_sln_b,
                 sgu_w=g_sw, sgu_b=g_sb[:, :, 0])
    return dx_in, dxb_in, [g_in, g_out.reshape(N_CHIPS, -1, D_MODEL)], small


def _layer_bwd(dx, dxb, p, s, tabs, token):
    dmid, dmidb, ffn_big, g_ffn_norm = _ffn_bwd(dx, dxb, p, s, token)
    dx_in, dxb_in, mix_big, small = _mixer_bwd(dmid, dmidb, p, s, tabs, token)
    return dx_in, dxb_in, mix_big + ffn_big, dict(small, ffn_norm_g=g_ffn_norm)


def _mixer_weights(gathered):
    w_in, w_out = gathered
    return dict(w_in=w_in, w_out=w_out.reshape(-1, D_MODEL))


def _ffn_weights(gathered):
    w_gate, w_up, w_down = gathered
    return dict(w_gate=w_gate, w_up=w_up, w_down=w_down.reshape(-1, D_MODEL))


def _small_params(l, conv_w_full, mix_norm_g, sink, conv_dw_b, conv_ln_g, conv_ln_b, sgu_ln_g, sgu_ln_b, sgu_w, sgu_b,
                  ffn_norm_g):
    return dict(
        mix_norm_g=mix_norm_g[l:l + 1], ffn_norm_g=ffn_norm_g[l:l + 1],
        sink3=jnp.broadcast_to(sink[l].reshape(N_KV_HEADS, Q_PER_KV, 1), (N_KV_HEADS, Q_PER_KV, LANES)),
        conv_w32=jnp.pad(conv_w_full[l], ((0, 32 - CONV_KERNEL), (0, 0))),
        conv_dw_b=conv_dw_b[l:l + 1], conv_ln_g=conv_ln_g[l:l + 1], conv_ln_b=conv_ln_b[l:l + 1],
        sgu_ln_g=sgu_ln_g[l:l + 1], sgu_ln_b=sgu_ln_b[l:l + 1], sgu_w16=sgu_w[l].astype(BF16),
        sgu_b3=jnp.broadcast_to(sgu_b[l][:, :, None], (SGU_HEADS, CHUNK, CHUNK)))


_SMALL = ["mix_norm_g", "sink", "conv_dw_b", "conv_ln_g", "conv_ln_b", "sgu_ln_g", "sgu_ln_b", "sgu_w", "sgu_b", "ffn_norm_g",
          "final_norm_g"]


def _pack_rows(arrays):
    rows, counts = [], []
    for a in arrays:
        flat = a.reshape(-1)
        n = -(-flat.shape[0] // LANES)
        rows.append(jnp.pad(flat, (0, n * LANES - flat.shape[0])).reshape(n, LANES))
        counts.append(n)
    packed = jnp.concatenate(rows, axis=0)
    pad = -packed.shape[0] % PACK_ROWS
    return jnp.pad(packed, ((0, pad), (0, 0))), counts


def _unpack_rows(packed, counts, shapes):
    out, r = [], 0
    for n, shape in zip(counts, shapes):
        size = math.prod(shape)
        out.append(packed[r:r + n].reshape(-1)[:size].reshape(shape))
        r += n
    return out


def kernel(x, mix_norm_g, w_in, sink, conv_dw_w, conv_dw_b, conv_ln_g, conv_ln_b, sgu_ln_g, sgu_ln_b, sgu_w, sgu_b, w_out, ffn_norm_g, w_gate, w_up, w_down, final_norm_g, loss_target, m_mix_norm_g, m_w_in, m_sink, m_conv_dw_w, m_conv_dw_b, m_conv_ln_g, m_conv_ln_b, m_sgu_ln_g, m_sgu_ln_b, m_sgu_w, m_sgu_b, m_w_out, m_ffn_norm_g, m_w_gate, m_w_up, m_w_down, m_final_norm_g, v_mix_norm_g, v_w_in, v_sink, v_conv_dw_w, v_conv_dw_b, v_conv_ln_g, v_conv_ln_b, v_sgu_ln_g, v_sgu_ln_b, v_sgu_w, v_sgu_b, v_w_out, v_ffn_norm_g, v_w_gate, v_w_up, v_w_down, v_final_norm_g):
    S = x.shape[1]
    my_chip = 2 * lax.axis_index("x") + lax.axis_index("y")
    c_idx = lax.axis_index("c").astype(jnp.int32).reshape(1)
    big_w = [w_in, w_out, w_gate, w_up, w_down]
    big_m = [m_w_in, m_w_out, m_w_gate, m_w_up, m_w_down]
    big_v = [v_w_in, v_w_out, v_w_gate, v_w_up, v_w_down]
    n_kinds = len(big_w)

    x_idx = lax.axis_index("x").astype(jnp.int32).reshape(1)
    y_idx = lax.axis_index("y").astype(jnp.int32).reshape(1)
    conv_w_all = gather_small(conv_dw_w)
    conv_w_full = jnp.transpose(conv_w_all, (1, 2, 0, 3)).reshape(DEPTH, CONV_KERNEL, CONV_WIDTH)
    tabs = rope_tables(S)
    no_token = jnp.zeros(TOKEN.shape, TOKEN.dtype)

    mixer_kinds, ffn_kinds = [0, 1], [2, 3, 4]
    shards = [[w[l].astype(BF16) for w in big_w] for l in range(DEPTH)]

    def fetch(pending, after):
        send_sems, recv_sems, srcs, lands, _ = pending
        return forward_halves(gather_wait(send_sems, recv_sems, srcs, lands, after))

    first_mixer = gather_start([shards[0][k] for k in mixer_kinds], conv_w_all)
    first_ffn = gather_start([shards[0][k] for k in ffn_kinds], first_mixer[4])
    pending = gather_start(shards[1], first_ffn[4])
    act = x[0]
    h = rms_fwd(act, mix_norm_g[0:1], no_token)
    saved, params = [], []
    for l in range(DEPTH):
        p = _small_params(l, conv_w_full, mix_norm_g, sink, conv_dw_b, conv_ln_g, conv_ln_b, sgu_ln_g, sgu_ln_b, sgu_w, sgu_b,
                          ffn_norm_g)
        next_gain = mix_norm_g[l + 1:l + 2] if l + 1 < DEPTH else final_norm_g.reshape(1, D_MODEL)
        if l == 0:
            p.update(_mixer_weights(fetch(first_mixer, act)))
            x_mid, h2, s_mix = _mixer_fwd(act, h, p, tabs, pending[4])
            p.update(_ffn_weights(fetch(first_ffn, x_mid)))
            late, token = None, no_token
        else:
            send_sems, recv_sems, srcs, lands, _ = pending
            lands = gather_wait(send_sems, recv_sems, srcs, lands, act)
            w_in_full = forward_halves(lands[:1])[0]
            p.update(w_in=w_in_full)
            fwd_send, fwd_recv, rest, token = forward_start(lands[1:])

            def late(mix, fwd_send=fwd_send, fwd_recv=fwd_recv, rest=rest):
                w_out_full, *ffn_full = forward_wait(fwd_send, fwd_recv, rest, mix)
                return dict(_ffn_weights(ffn_full), w_out=w_out_full.reshape(-1, D_MODEL))

            if l + 1 < DEPTH:
                pending = gather_start(shards[l + 1], w_in_full)
                token = token + pending[4]
        if l > 0:
            x_mid, h2, s_mix = _mixer_fwd(act, h, p, tabs, token, late)
        act, h, s_ffn = _ffn_fwd(x_mid, h2, p, next_gain, token)
        params.append(p)
        saved.append({**s_mix, **s_ffn})
    loss_part, dx, dxb, g_final = final_loss(act, final_norm_g.reshape(1, D_MODEL), loss_target[0])
    loss = lax.psum(loss_part[0, 0], ("x", "y", "c"))

    halves = [jnp.zeros((DEPTH, w.shape[1] // 2, w.shape[2]), F32) for w in big_w]
    small_grads = [None] * DEPTH

    def chip_start(layer, kinds, grads, recv):
        chip_sum = [add_sibling_half(g, r, c_idx) for g, r in zip(grads, recv)]
        send_sems, recv_sems, parts, lands, token = chip_parts_start(chip_sum)
        return (layer, kinds, send_sems, recv_sems, parts, lands), token

    def reduce_start(layer, kinds, grads):
        return chip_start(layer, kinds, grads, exchange_sibling_halves(grads))

    def reduce_finish(pending, halves, after):
        layer, kinds, send_sems, recv_sems, parts, lands = pending
        parts, others = chip_parts_wait(send_sems, recv_sems, parts, lands, after)
        halves = list(halves)
        for i, k in enumerate(kinds):
            halves[k] = sum_chips(parts[i], others[i], halves[k], x_idx, y_idx, layer)
        return halves

    pending, token = None, no_token
    for l in reversed(range(DEPTH)):
        dmid, dmidb, ffn_big, g_ffn_norm = _ffn_bwd(dx, dxb, params[l], saved[l], token)
        if l == 0:
            last_ffn, mixer_token = reduce_start(l, ffn_kinds, ffn_big)
        else:
            half_shapes = [(g.shape[0], g.shape[1] // 2, g.shape[2]) for g in ffn_big]
            sib_send, sib_recv, ffn_big, ffn_lands, mixer_token = pair_start("sibling_start", _sibling_half_copies, ffn_big, half_shapes)
        dx, dxb, mix_big, small = _mixer_bwd(dmid, dmidb, params[l], saved[l], tabs, mixer_token)
        small_grads[l] = dict(small, ffn_norm_g=g_ffn_norm)
        if pending is not None:
            halves = reduce_finish(pending, halves, dx)
        if l == 0:
            last_mixer, token = reduce_start(l, mixer_kinds, mix_big)
            halves = reduce_finish(last_ffn, halves, token)
        else:
            ffn_big, ffn_recv = pair_wait("sibling_wait", _sibling_half_copies, sib_send, sib_recv, ffn_big, ffn_lands, dx)
            mix_recv = exchange_sibling_halves(mix_big)
            pending, token = chip_start(l, mixer_kinds + ffn_kinds, list(mix_big) + list(ffn_big), list(mix_recv) + list(ffn_recv))

    def final_start(kinds):
        send_sems, recv_sems, mine, lands, _ = pair_start("final_start", _sibling_whole_copies, [halves[k] for k in kinds],
                                                          [halves[k].shape for k in kinds])
        return send_sems, recv_sems, mine, lands

    ffn_final = final_start(ffn_kinds)

    stacked = {n: jnp.stack([small_grads[l][n] for l in range(DEPTH)]) for n in small_grads[0]}
    stacked["final_norm_g"] = g_final
    packed, counts = _pack_rows([stacked[n] for n in _SMALL] + [stacked["conv_dw_w"]])
    reduced = allreduce_small(packed)
    small_w = dict(mix_norm_g=mix_norm_g, sink=sink, conv_dw_b=conv_dw_b, conv_ln_g=conv_ln_g, conv_ln_b=conv_ln_b,
                   sgu_ln_g=sgu_ln_g, sgu_ln_b=sgu_ln_b, sgu_w=sgu_w, sgu_b=sgu_b, ffn_norm_g=ffn_norm_g,
                   final_norm_g=final_norm_g)
    small_m = dict(mix_norm_g=m_mix_norm_g, sink=m_sink, conv_dw_b=m_conv_dw_b, conv_ln_g=m_conv_ln_g,
                   conv_ln_b=m_conv_ln_b, sgu_ln_g=m_sgu_ln_g, sgu_ln_b=m_sgu_ln_b, sgu_w=m_sgu_w, sgu_b=m_sgu_b,
                   ffn_norm_g=m_ffn_norm_g, final_norm_g=m_final_norm_g)
    small_v = dict(mix_norm_g=v_mix_norm_g, sink=v_sink, conv_dw_b=v_conv_dw_b, conv_ln_g=v_conv_ln_g,
                   conv_ln_b=v_conv_ln_b, sgu_ln_g=v_sgu_ln_g, sgu_ln_b=v_sgu_ln_b, sgu_w=v_sgu_w, sgu_b=v_sgu_b,
                   ffn_norm_g=v_ffn_norm_g, final_norm_g=v_final_norm_g)
    shapes = [small_w[n].shape for n in _SMALL] + [(DEPTH, CONV_KERNEL, CONV_WIDTH)]
    red = _unpack_rows(reduced, counts, shapes)
    g_small = dict(zip(_SMALL, red[:-1]))
    g_small["conv_dw_w"] = lax.dynamic_slice_in_dim(red[-1], my_chip * LANES, LANES, axis=2)
    small_w["conv_dw_w"], small_m["conv_dw_w"], small_v["conv_dw_w"] = conv_dw_w, m_conv_dw_w, v_conv_dw_w
    names = _SMALL + ["conv_dw_w"]
    pw, cnt = _pack_rows([small_w[n] for n in names])
    pg, _ = _pack_rows([g_small[n] for n in names])
    pm, _ = _pack_rows([small_m[n] for n in names])
    pv, _ = _pack_rows([small_v[n] for n in names])
    sd, sm, sv = adamw(pw, pg, pm, pv)
    shp = [small_w[n].shape for n in names]
    d_small = dict(zip(names, _unpack_rows(sd, cnt, shp)))
    m_small = dict(zip(names, _unpack_rows(sm, cnt, shp)))
    v_small = dict(zip(names, _unpack_rows(sv, cnt, shp)))

    big_names = ["w_in", "w_out", "w_gate", "w_up", "w_down"]
    g_big, d_big, m_big, v_big = {}, {}, {}, {}
    after = sd
    for kinds, final in ((ffn_kinds, ffn_final), (mixer_kinds, None)):
        if final is None:
            halves = reduce_finish(last_mixer, halves, after)
            final = final_start(kinds)
        send_sems, recv_sems, sent, lands = final
        for i, k in enumerate(kinds):
            n = big_names[k]
            mine, theirs = pair_wait_one("final_wait", send_sems, recv_sems, sent[i], lands[i], after, i)
            g_big[n], d_big[n], m_big[n], v_big[n] = adamw_halves(big_w[k], mine, theirs, big_m[k], big_v[k], c_idx)
            after = d_big[n]

    order = ["mix_norm_g", "w_in", "sink", "conv_dw_w", "conv_dw_b", "conv_ln_g", "conv_ln_b", "sgu_ln_g", "sgu_ln_b",
             "sgu_w", "sgu_b", "w_out", "ffn_norm_g", "w_gate", "w_up", "w_down", "final_norm_g"]
    grads = {**g_small, **g_big}
    deltas = {**d_small, **d_big}
    new_m = {**m_small, **m_big}
    new_v = {**v_small, **v_big}
    return (loss, dx[None], *[grads[n] for n in order], *[deltas[n] for n in order],
            *[new_m[n] for n in order], *[new_v[n] for n in order])
```

```python
import functools
import math

import jax
import jax.numpy as jnp
from jax import lax
from jax.experimental import pallas as pl
from jax.experimental.pallas import tpu as pltpu

F32, BF16 = jnp.float32, jnp.bfloat16

D_MODEL = 2048
DEPTH = 4
HEAD_DIM = 128
N_Q_HEADS = 8
N_KV_HEADS = 2
Q_PER_KV = N_Q_HEADS // N_KV_HEADS
ATTN_WIDTH = N_Q_HEADS * HEAD_DIM
KV_WIDTH = N_KV_HEADS * HEAD_DIM
CONV_WIDTH = 512
CONV_KERNEL = 31
CONV_PAD = 16
SGU_WIDTH = 512
SGU_HEADS = 4
CHUNK = 128
IN_WIDTH = 3584
D_FF = 5632
WINDOW = 128
ROT_DIM = 32
ROPE_THETA = 500000.0
EPS = 1e-6
N_CHIPS = 4
N_DEV = 8
LANES = 128
PACK_ROWS = 64
OFF_K = ATTN_WIDTH
OFF_V = OFF_K + KV_WIDTH
OFF_CA = OFF_V + KV_WIDTH
OFF_CG = OFF_CA + CONV_WIDTH
OFF_U = OFF_CG + CONV_WIDTH
OFF_VV = OFF_U + SGU_WIDTH

ADAM_LR, ADAM_B1, ADAM_B2, ADAM_EPS, ADAM_WD, ADAM_STEP = 0.001, 0.9, 0.999, 1e-08, 0.01, 10

VMEM_LIMIT = 56 * 1024 * 1024
MESH = pl.DeviceIdType.MESH
HBM = pl.BlockSpec(memory_space=pltpu.HBM)
VMEM_SPEC = pl.BlockSpec(memory_space=pltpu.VMEM)


def _call(name, body, *, grid, in_specs, out_specs, out_shape, scratch=(), sem=None, aliases=None):
    params = dict(vmem_limit_bytes=VMEM_LIMIT)
    if sem is not None:
        params["dimension_semantics"] = sem
    return pl.pallas_call(
        body, name=name, grid=grid, in_specs=in_specs, out_specs=out_specs, out_shape=out_shape,
        scratch_shapes=list(scratch), input_output_aliases=aliases or {}, compiler_params=pltpu.CompilerParams(**params))


def _sigmoid(x):
    return 1.0 / (1.0 + jnp.exp(-x))


def rms_fwd(x, g, token):
    S = x.shape[0]
    tm = min(512, S)

    def body(x_ref, g_ref, token_ref, o_ref):
        xv = x_ref[...]
        r = lax.rsqrt(jnp.mean(xv * xv, axis=-1, keepdims=True) + EPS)
        o_ref[...] = (xv * r * g_ref[...]).astype(BF16)

    return _call("rms_fwd", body, grid=(S // tm,),
                 in_specs=[pl.BlockSpec((tm, D_MODEL), lambda i: (i, 0)), pl.BlockSpec((1, D_MODEL), lambda i: (0, 0)),
                           pl.BlockSpec((8, LANES), lambda i: (0, 0))],
                 out_specs=pl.BlockSpec((tm, D_MODEL), lambda i: (i, 0)),
                 out_shape=jax.ShapeDtypeStruct((S, D_MODEL), BF16), sem=("parallel",))(x, g, token)


def _rms_bwd_math(xv, gv, dh):
    r = lax.rsqrt(jnp.mean(xv * xv, axis=-1, keepdims=True) + EPS)
    n = xv * r
    dn = dh * gv
    dx = r * (dn - n * jnp.mean(dn * n, axis=-1, keepdims=True))
    dg = jnp.sum(dh * n, axis=0, keepdims=True)
    return dx, dg


def rms_bwd(x, g, dh, dres):
    S = x.shape[0]
    tm = min(512, S)

    def body(x_ref, g_ref, dh_ref, dres_ref, dx_ref, dxb_ref, dg_ref):
        dx, dg = _rms_bwd_math(x_ref[...], g_ref[...], dh_ref[...].astype(F32))
        dx = dx + dres_ref[...]
        dx_ref[...] = dx
        dxb_ref[...] = dx.astype(BF16)

        @pl.when(pl.program_id(0) == 0)
        def _():
            dg_ref[...] = dg

        @pl.when(pl.program_id(0) > 0)
        def _():
            dg_ref[...] += dg

    row = pl.BlockSpec((tm, D_MODEL), lambda i: (i, 0))
    vec = pl.BlockSpec((1, D_MODEL), lambda i: (0, 0))
    return _call("rms_bwd", body, grid=(S // tm,), in_specs=[row, vec, row, row], out_specs=[row, row, vec],
                 out_shape=[jax.ShapeDtypeStruct((S, D_MODEL), F32), jax.ShapeDtypeStruct((S, D_MODEL), BF16),
                            jax.ShapeDtypeStruct((1, D_MODEL), F32)], sem=("arbitrary",))(x, g, dh, dres)


def final_loss(x, g, target):
    S = x.shape[0]
    tm = min(256, S)

    def body(x_ref, g_ref, t_ref, loss_ref, dx_ref, dxb_ref, dg_ref):
        xv = x_ref[...]
        gv = g_ref[...]
        r = lax.rsqrt(jnp.mean(xv * xv, axis=-1, keepdims=True) + EPS)
        err = xv * r * gv - t_ref[...]
        part = 0.5 * jnp.sum(jnp.mean(err * err, axis=-1, keepdims=True), axis=0, keepdims=True)
        dx, dg = _rms_bwd_math(xv, gv, err * (1.0 / D_MODEL))
        dx_ref[...] = dx
        dxb_ref[...] = dx.astype(BF16)

        @pl.when(pl.program_id(0) == 0)
        def _():
            dg_ref[...] = dg
            loss_ref[...] = part

        @pl.when(pl.program_id(0) > 0)
        def _():
            dg_ref[...] += dg
            loss_ref[...] += part

    row = pl.BlockSpec((tm, D_MODEL), lambda i: (i, 0))
    vec = pl.BlockSpec((1, D_MODEL), lambda i: (0, 0))
    one = pl.BlockSpec((1, 1), lambda i: (0, 0))
    return _call("final_loss", body, grid=(S // tm,), in_specs=[row, vec, row], out_specs=[one, row, row, vec],
                 out_shape=[jax.ShapeDtypeStruct((1, 1), F32), jax.ShapeDtypeStruct((S, D_MODEL), F32),
                            jax.ShapeDtypeStruct((S, D_MODEL), BF16), jax.ShapeDtypeStruct((1, D_MODEL), F32)],
                 sem=("arbitrary",))(x, g, target)


EPILOGUE_ROWS = 256
NN = (((1,), (0,)), ((), ()))
NT = (((1,), (1,)), ((), ()))
TN = (((0,), (0,)), ((), ()))


def _matmul(name, operands, in_specs, out_shape, out_specs, grid, pairs, dims, acc_shape, epilogue):
    n_in, n_out, nk = len(operands), len(out_shape), grid[-1]

    def body(*refs):
        ins, outs = refs[:n_in], refs[n_in:n_in + n_out]
        part = None
        for ia, ib in pairs:
            d = lax.dot_general(ins[ia][...], ins[ib][...], dims, preferred_element_type=F32)
            part = d if part is None else part + d
        if nk == 1:
            epilogue(part, ins, outs)
        else:
            acc = refs[-1]
            k = pl.program_id(len(grid) - 1)

            @pl.when(k == 0)
            def _():
                acc[...] = part

            @pl.when(k > 0)
            def _():
                acc[...] += part

            @pl.when(k == nk - 1)
            def _():
                epilogue(acc[...], ins, outs)

    scratch = [pltpu.VMEM(acc_shape, F32)] if nk > 1 else []
    sem = ("parallel",) * (len(grid) - 1) + ("arbitrary",)
    return _call(name, body, grid=grid, in_specs=in_specs, out_specs=out_specs, out_shape=out_shape,
                 scratch=scratch, sem=sem)(*operands)


def _store(dtype):
    def epilogue(acc, ins, outs):
        outs[0][...] = acc.astype(dtype)
    return epilogue


def mm_nn_cols(a, w, token):
    S, K = a.shape
    J, _, Ns = w.shape
    tm = min(512, S)
    return _matmul("mm_nn_cols", (a, w, token),
                   [pl.BlockSpec((tm, K), lambda j, i, k: (i, 0)), pl.BlockSpec((None, K, Ns), lambda j, i, k: (j, 0, 0)),
                    pl.BlockSpec((8, LANES), lambda j, i, k: (0, 0))],
                   [jax.ShapeDtypeStruct((S, J * Ns), BF16)], [pl.BlockSpec((tm, Ns), lambda j, i, k: (i, j))],
                   (J, S // tm, 1), [(0, 1)], NN, None, _store(BF16))[0]


def ffn_up(h, wg, wu, token):
    S, K = h.shape
    J, _, Ns = wg.shape
    tm = min(512, S)

    sub = min(EPILOGUE_ROWS, tm)

    def body(h_ref, wg_ref, wu_ref, token_ref, g_ref, u_ref, a_ref):
        for r in range(tm // sub):
            rows = slice(r * sub, (r + 1) * sub)
            hv = h_ref[rows, :]
            gv = jnp.dot(hv, wg_ref[...], preferred_element_type=F32)
            uv = jnp.dot(hv, wu_ref[...], preferred_element_type=F32)
            g_ref[rows, :] = gv.astype(BF16)
            u_ref[rows, :] = uv.astype(BF16)
            a_ref[rows, :] = (gv * _sigmoid(gv) * uv).astype(BF16)

    wspec = pl.BlockSpec((None, K, Ns), lambda j, i: (j, 0, 0))
    ospec = pl.BlockSpec((tm, Ns), lambda j, i: (i, j))
    oshape = jax.ShapeDtypeStruct((S, J * Ns), BF16)
    return _call("ffn_up", body, grid=(J, S // tm),
                 in_specs=[pl.BlockSpec((tm, K), lambda j, i: (i, 0)), wspec, wspec, pl.BlockSpec((8, LANES), lambda j, i: (0, 0))],
                 out_specs=[ospec, ospec, ospec], out_shape=[oshape, oshape, oshape],
                 sem=("parallel", "parallel"))(h, wg, wu, token)


def mm_nn_rows_res(a, w, res, gain):
    S, K = a.shape
    N = w.shape[1]
    tm = min(512, S)
    tk, tn = (K, N) if K <= 2048 else (K // 2, N // 2)
    n_n, n_k = N // tn, K // tk

    def body(a_ref, w_ref, res_ref, g_ref, x_ref, h_ref, *acc):
        n, k = pl.program_id(1), pl.program_id(2)

        def normed(xv):
            r = lax.rsqrt(jnp.mean(xv * xv, axis=-1, keepdims=True) + EPS)
            h_ref[...] = (xv * r * g_ref[...]).astype(BF16)

        def store_columns(total):
            if n_n == 1:
                xv = total + res_ref[...]
                x_ref[...] = xv
                normed(xv)
                return
            for c in range(n_n):
                @pl.when(n == c)
                def _(c=c):
                    cols = slice(c * tn, (c + 1) * tn)
                    x_ref[:, cols] = total + res_ref[:, cols]

            @pl.when(n == n_n - 1)
            def _():
                normed(x_ref[...])

        part = jnp.dot(a_ref[...], w_ref[...], preferred_element_type=F32)
        if n_k == 1:
            store_columns(part)
        else:
            @pl.when(k == 0)
            def _():
                acc[0][...] = part

            @pl.when(k > 0)
            def _():
                acc[0][...] += part

            @pl.when(k == n_k - 1)
            def _():
                store_columns(acc[0][...])

    row = pl.BlockSpec((tm, N), lambda i, n, k: (i, 0))
    return _call("mm_nn_rows_res", body, grid=(S // tm, n_n, n_k),
                 in_specs=[pl.BlockSpec((tm, tk), lambda i, n, k: (i, k)), pl.BlockSpec((tk, tn), lambda i, n, k: (k, n)), row,
                           pl.BlockSpec((1, N), lambda i, n, k: (0, 0))],
                 out_specs=[row, row], out_shape=[jax.ShapeDtypeStruct((S, N), F32), jax.ShapeDtypeStruct((S, N), BF16)],
                 scratch=[pltpu.VMEM((tm, tn), F32)] if n_k > 1 else [],
                 sem=("parallel", "arbitrary", "arbitrary"))(a, w, res, gain)


def mm_nt_cols(pairs_in, out_dtype, shards_per_step):
    dz0, w0 = pairs_in[0]
    S = dz0.shape[0]
    J, K, Ns = w0.shape
    tm = min(512, S)
    sps = shards_per_step
    operands, specs, pairs = [], [], []
    for dz, w in pairs_in:
        for s in range(sps):
            pairs.append((len(operands), len(operands) + 1))
            operands += [dz, w]
            specs += [pl.BlockSpec((tm, Ns), lambda i, j, s=s: (i, j * sps + s)),
                      pl.BlockSpec((None, K, Ns), lambda i, j, s=s: (j * sps + s, 0, 0))]
    return _matmul("mm_nt_cols%d" % len(pairs_in), tuple(operands), specs,
                   [jax.ShapeDtypeStruct((S, K), out_dtype)], [pl.BlockSpec((tm, K), lambda i, j: (i, 0))],
                   (S // tm, J // sps), pairs, NT, (tm, K), _store(out_dtype))[0]


def mm_nt_rows(dy, w, token):
    S, N = dy.shape
    K = w.shape[0]
    tm, tko = min(1024, S), 512
    return _matmul("mm_nt_rows", (dy, w, token),
                   [pl.BlockSpec((tm, N), lambda i, kk, z: (i, 0)), pl.BlockSpec((tko, N), lambda i, kk, z: (kk, 0)),
                    pl.BlockSpec((8, LANES), lambda i, kk, z: (0, 0))],
                   [jax.ShapeDtypeStruct((S, K), BF16)], [pl.BlockSpec((tm, tko), lambda i, kk, z: (i, kk))],
                   (S // tm, K // tko, 1), [(0, 1)], NT, None, _store(BF16))[0]


def ffn_down_bwd(dy, w, gate, up, token):
    S, N = dy.shape
    K = w.shape[0]
    tm, tko = min(1024, S), 512
    sub = min(EPILOGUE_ROWS, tm)

    def body(dy_ref, w_ref, g_ref, u_ref, token_ref, dg_ref, du_ref):
        for r in range(tm // sub):
            rows = slice(r * sub, (r + 1) * sub)
            dact = lax.dot_general(dy_ref[rows, :], w_ref[...], NT, preferred_element_type=F32)
            gv = g_ref[rows, :].astype(F32)
            uv = u_ref[rows, :].astype(F32)
            sg = _sigmoid(gv)
            dg_ref[rows, :] = (dact * uv * sg * (1.0 + gv * (1.0 - sg))).astype(BF16)
            du_ref[rows, :] = (dact * gv * sg).astype(BF16)

    tile = pl.BlockSpec((tm, tko), lambda i, kk: (i, kk))
    oshape = jax.ShapeDtypeStruct((S, K), BF16)
    return _call("ffn_down_bwd", body, grid=(S // tm, K // tko),
                 in_specs=[pl.BlockSpec((tm, N), lambda i, kk: (i, 0)), pl.BlockSpec((tko, N), lambda i, kk: (kk, 0)), tile, tile,
                           pl.BlockSpec((8, LANES), lambda i, kk: (0, 0))],
                 out_specs=[tile, tile], out_shape=[oshape, oshape], sem=("parallel", "parallel"))(dy, w, gate, up, token)


def mm_tn_cols(a, dz, J):
    S, M = a.shape
    Ns = dz.shape[1] // J
    tm, tk = 512, S
    return _matmul("mm_tn_cols", (a, dz),
                   [pl.BlockSpec((tk, tm), lambda j, m, k: (k, m)), pl.BlockSpec((tk, Ns), lambda j, m, k: (k, j))],
                   [jax.ShapeDtypeStruct((J, M, Ns), BF16)], [pl.BlockSpec((None, tm, Ns), lambda j, m, k: (j, m, 0))],
                   (J, M // tm, S // tk), [(0, 1)], TN, (tm, Ns), _store(BF16))[0]


def mm_tn_rows(a, dy):
    S, K = a.shape
    N = dy.shape[1]
    tm, tk = 512, min(2048, S)
    return _matmul("mm_tn_rows", (a, dy),
                   [pl.BlockSpec((tk, tm), lambda m, k: (k, m)), pl.BlockSpec((tk, N), lambda m, k: (k, 0))],
                   [jax.ShapeDtypeStruct((K, N), BF16)], [pl.BlockSpec((tm, N), lambda m, k: (m, 0))],
                   (K // tm, S // tk), [(0, 1)], TN, (tm, N), _store(BF16))[0]


def rope_tables(S):
    half = ROT_DIM // 2
    pos = jnp.arange(S, dtype=F32)
    inv = ROPE_THETA ** (-jnp.arange(0, ROT_DIM, 2, dtype=F32) / ROT_DIM)
    ang = pos[:, None] * inv[None, :]
    cos, sin = jnp.cos(ang), jnp.sin(ang)
    zeros = jnp.zeros((S, HEAD_DIM - ROT_DIM), F32)
    c = jnp.concatenate([cos, cos, jnp.ones((S, HEAD_DIM - ROT_DIM), F32)], axis=1)
    s_lo = jnp.concatenate([-sin, jnp.zeros((S, half), F32), zeros], axis=1)
    s_hi = jnp.concatenate([jnp.zeros((S, half), F32), sin, zeros], axis=1)
    return c, s_lo, s_hi


ROPE_ROWS = 512


def _rope(t, c, s_lo, s_hi):
    half = ROT_DIM // 2
    return t * c + pltpu.roll(t, HEAD_DIM - half, 1) * s_lo + pltpu.roll(t, half, 1) * s_hi


def _unrope(d, c, s_lo, s_hi):
    half = ROT_DIM // 2
    return d * c + pltpu.roll(d * s_lo, half, 1) + pltpu.roll(d * s_hi, HEAD_DIM - half, 1)


def rope_fwd(z, tabs):
    S = z.shape[0]
    T = min(ROPE_ROWS, S)

    def body(q_ref, kv_ref, c_ref, sl_ref, sh_ref, qr_ref, kp_ref, vp_ref):
        i = pl.program_id(0)

        @pl.when(i == 0)
        def _():
            zero = jnp.zeros((CHUNK, KV_WIDTH), BF16)
            kp_ref[0:CHUNK, :] = zero
            vp_ref[0:CHUNK, :] = zero
            kp_ref[S + CHUNK:S + 2 * CHUNK, :] = zero
            vp_ref[S + CHUNK:S + 2 * CHUNK, :] = zero

        c, sl, sh = c_ref[...], sl_ref[...], sh_ref[...]
        for h in range(N_Q_HEADS):
            cols = slice(h * HEAD_DIM, (h + 1) * HEAD_DIM)
            qr_ref[:, cols] = _rope(q_ref[:, cols].astype(F32), c, sl, sh).astype(BF16)
        rows = pl.ds(pl.multiple_of(CHUNK + i * T, CHUNK), T)
        for g in range(N_KV_HEADS):
            cols = slice(g * HEAD_DIM, (g + 1) * HEAD_DIM)
            kp_ref[rows, cols] = _rope(kv_ref[:, cols].astype(F32), c, sl, sh).astype(BF16)
        vp_ref[rows, :] = kv_ref[:, KV_WIDTH:2 * KV_WIDTH]

    tab = pl.BlockSpec((T, HEAD_DIM), lambda i: (i, 0))
    pad = pl.BlockSpec((S + 2 * CHUNK, KV_WIDTH), lambda i: (0, 0))
    return _call("rope_fwd", body, grid=(S // T,),
                 in_specs=[pl.BlockSpec((T, ATTN_WIDTH), lambda i: (i, 0)),
                           pl.BlockSpec((T, 2 * KV_WIDTH), lambda i: (i, OFF_K // (2 * KV_WIDTH))), tab, tab, tab],
                 out_specs=[pl.BlockSpec((T, ATTN_WIDTH), lambda i: (i, 0)), pad, pad],
                 out_shape=[jax.ShapeDtypeStruct((S, ATTN_WIDTH), BF16), jax.ShapeDtypeStruct((S + 2 * CHUNK, KV_WIDTH), BF16),
                            jax.ShapeDtypeStruct((S + 2 * CHUNK, KV_WIDTH), BF16)], sem=("arbitrary",))(z, z, *tabs)


def rope_bwd(dq, dkp, dvp, tabs):
    S = dq.shape[0]
    T = min(ROPE_ROWS, S)

    def body(dq_ref, dk_ref, dv_ref, c_ref, sl_ref, sh_ref, o_ref):
        c, sl, sh = c_ref[...], sl_ref[...], sh_ref[...]
        for h in range(N_Q_HEADS):
            cols = slice(h * HEAD_DIM, (h + 1) * HEAD_DIM)
            o_ref[:, cols] = _unrope(dq_ref[:, cols], c, sl, sh).astype(BF16)
        rows = pl.ds(pl.multiple_of(CHUNK + pl.program_id(0) * T, CHUNK), T)
        for g in range(N_KV_HEADS):
            cols = slice(g * HEAD_DIM, (g + 1) * HEAD_DIM)
            o_ref[:, OFF_K + g * HEAD_DIM:OFF_K + (g + 1) * HEAD_DIM] = _unrope(dk_ref[rows, cols], c, sl, sh).astype(BF16)
        o_ref[:, OFF_V:OFF_V + KV_WIDTH] = dv_ref[rows, :].astype(BF16)

    tab = pl.BlockSpec((T, HEAD_DIM), lambda i: (i, 0))
    pad = pl.BlockSpec((S + 2 * CHUNK, KV_WIDTH), lambda i: (0, 0))
    return _call("rope_bwd", body, grid=(S // T,),
                 in_specs=[pl.BlockSpec((T, ATTN_WIDTH), lambda i: (i, 0)), pad, pad, tab, tab, tab],
                 out_specs=pl.BlockSpec((T, OFF_CA), lambda i: (i, 0)),
                 out_shape=jax.ShapeDtypeStruct((S, IN_WIDTH), BF16), sem=("parallel",))(dq, dkp, dvp, *tabs)


STACK = Q_PER_KV * CHUNK


def _stack_heads(ref, rows):
    return jnp.concatenate([ref[rows, r * HEAD_DIM:(r + 1) * HEAD_DIM] for r in range(Q_PER_KV)], axis=0)


def _stack_sinks(s_ref):
    return jnp.concatenate([jnp.broadcast_to(s_ref[r:r + 1, 0:1], (CHUNK, 1)) for r in range(Q_PER_KV)], axis=0)


MASKED = -1e30


def _scores(q, kb):
    return lax.dot_general(q, kb, NT, preferred_element_type=F32) * (1.0 / math.sqrt(HEAD_DIM))


def _band_bias():
    row = lax.broadcasted_iota(jnp.int32, (STACK, 3 * CHUNK), 0) & (CHUNK - 1)
    col = lax.broadcasted_iota(jnp.int32, (STACK, 3 * CHUNK), 1)
    return jnp.where(jnp.abs(col - CHUNK - row) <= WINDOW, 0.0, MASKED).astype(F32)


def _edge_bias(n, S):
    kpos = (n - 1) * CHUNK + lax.broadcasted_iota(jnp.int32, (1, 3 * CHUNK), 1)
    return jnp.where((kpos >= 0) & (kpos < S), 0.0, MASKED).astype(F32)


def _softmax_sink(s, sk, bias):
    s = s + bias
    m = jnp.maximum(jnp.max(s, axis=1, keepdims=True), sk)
    e = jnp.exp(s - m)
    es = jnp.exp(sk - m)
    inv = 1.0 / (jnp.sum(e, axis=1, keepdims=True) + es)
    return e * inv, es * inv


def _block_views(i, nblk):
    ns = [i * nblk + b for b in range(nblk)]
    wins = [pl.ds(pl.multiple_of(n * CHUNK, CHUNK), 3 * CHUNK) for n in ns]
    rows = [slice(b * CHUNK, (b + 1) * CHUNK) for b in range(nblk)]
    return ns, wins, rows


def attn_fwd(qr, kp, vp, sink3):
    S = qr.shape[0]
    tq = min(2048, S)
    gw = Q_PER_KV * HEAD_DIM
    nblk = tq // CHUNK

    def body(q_ref, k_ref, v_ref, s_ref, o_ref):
        ns, wins, rows = _block_views(pl.program_id(1), nblk)
        sk = _stack_sinks(s_ref)
        band = _band_bias()
        scores = [_scores(_stack_heads(q_ref, rows[b]), k_ref[wins[b], :]) for b in range(nblk)]
        probs = [_softmax_sink(scores[b], sk, band + _edge_bias(ns[b], S))[0].astype(BF16) for b in range(nblk)]
        outs = [jnp.dot(probs[b], v_ref[wins[b], :], preferred_element_type=F32).astype(BF16) for b in range(nblk)]
        for b in range(nblk):
            for r in range(Q_PER_KV):
                o_ref[rows[b], r * HEAD_DIM:(r + 1) * HEAD_DIM] = outs[b][r * CHUNK:(r + 1) * CHUNK]

    kv = pl.BlockSpec((S + 2 * CHUNK, HEAD_DIM), lambda g, i: (0, g))
    return _call("attn_fwd", body, grid=(N_KV_HEADS, S // tq),
                 in_specs=[pl.BlockSpec((tq, gw), lambda g, i: (i, g)), kv, kv,
                           pl.BlockSpec((None, Q_PER_KV, LANES), lambda g, i: (g, 0, 0))],
                 out_specs=pl.BlockSpec((tq, gw), lambda g, i: (i, g)),
                 out_shape=jax.ShapeDtypeStruct((S, D_MODEL), BF16), sem=("parallel", "arbitrary"))(qr, kp, vp, sink3)


def attn_bwd(qr, kp, vp, sink3, dmix):
    S = qr.shape[0]
    tq = min(1024, S)
    gw = Q_PER_KV * HEAD_DIM
    scale = 1.0 / math.sqrt(HEAD_DIM)
    nblk = tq // CHUNK

    def body(q_ref, k_ref, v_ref, s_ref, do_ref, dq_ref, dk_ref, dv_ref, ds_ref):
        i = pl.program_id(1)

        @pl.when(i == 0)
        def _():
            dk_ref[...] = jnp.zeros_like(dk_ref)
            dv_ref[...] = jnp.zeros_like(dv_ref)
            ds_ref[...] = jnp.zeros_like(ds_ref)

        blocks = range(nblk)
        ns, wins, rows = _block_views(i, nblk)
        sk = _stack_sinks(s_ref)
        band = _band_bias()
        qs = [_stack_heads(q_ref, rows[b]) for b in blocks]
        dos = [_stack_heads(do_ref, rows[b]) for b in blocks]
        scores = [_scores(qs[b], k_ref[wins[b], :]) for b in blocks]
        dps = [lax.dot_general(dos[b], v_ref[wins[b], :], NT, preferred_element_type=F32) for b in blocks]
        probs = [_softmax_sink(scores[b], sk, band + _edge_bias(ns[b], S)) for b in blocks]
        deltas = [jnp.sum(probs[b][0] * dps[b], axis=1, keepdims=True) for b in blocks]
        dscs = [(probs[b][0] * (dps[b] - deltas[b]) * scale).astype(BF16) for b in blocks]
        dqs = [jnp.dot(dscs[b], k_ref[wins[b], :], preferred_element_type=F32) for b in blocks]
        dks = [lax.dot_general(dscs[b], qs[b], TN, preferred_element_type=F32) for b in blocks]
        dvs = [lax.dot_general(probs[b][0].astype(BF16), dos[b], TN, preferred_element_type=F32) for b in blocks]
        for b in blocks:
            for r in range(Q_PER_KV):
                dq_ref[rows[b], r * HEAD_DIM:(r + 1) * HEAD_DIM] = dqs[b][r * CHUNK:(r + 1) * CHUNK]
        for m in range(nblk + 2):
            parts = [(b, m - b) for b in blocks if 0 <= m - b <= 2]
            krows = pl.ds(pl.multiple_of(i * tq + m * CHUNK, CHUNK), CHUNK)
            dk_ref[krows, :] += sum(dks[b][o * CHUNK:(o + 1) * CHUNK] for b, o in parts)
            dv_ref[krows, :] += sum(dvs[b][o * CHUNK:(o + 1) * CHUNK] for b, o in parts)
        for r in range(Q_PER_KV):
            head = slice(r * CHUNK, (r + 1) * CHUNK)
            dsink = sum(jnp.sum(-probs[b][1][head] * deltas[b][head], axis=0, keepdims=True) for b in blocks)
            ds_ref[r:r + 1, :] += jnp.broadcast_to(dsink, (1, LANES))

    kv = pl.BlockSpec((S + 2 * CHUNK, HEAD_DIM), lambda g, i: (0, g))
    qspec = pl.BlockSpec((tq, gw), lambda g, i: (i, g))
    sspec = pl.BlockSpec((None, Q_PER_KV, LANES), lambda g, i: (g, 0, 0))
    padshape = jax.ShapeDtypeStruct((S + 2 * CHUNK, KV_WIDTH), F32)
    return _call("attn_bwd", body, grid=(N_KV_HEADS, S // tq),
                 in_specs=[qspec, kv, kv, sspec, qspec],
                 out_specs=[qspec, kv, kv, sspec],
                 out_shape=[jax.ShapeDtypeStruct((S, ATTN_WIDTH), F32), padshape, padshape,
                            jax.ShapeDtypeStruct((N_KV_HEADS, Q_PER_KV, LANES), F32)],
                 sem=("parallel", "arbitrary"))(qr, kp, vp, sink3, dmix)


CONV_TILE = 256


def _fill_padded(dst_ref, value, S):
    zero = jnp.zeros((CONV_PAD, LANES), F32)
    dst_ref[0:CONV_PAD, :] = zero
    dst_ref[CONV_PAD + S:2 * CONV_PAD + S, :] = zero
    dst_ref[CONV_PAD:CONV_PAD + S, :] = value


def conv_dw_fwd(z, w32, b):
    S = z.shape[0]
    T = min(CONV_TILE, S)
    lo = CONV_PAD - (CONV_KERNEL - 1) // 2

    def body(a_ref, g_ref, w_ref, b_ref, o_ref, c0_ref):
        _fill_padded(c0_ref, a_ref[...].astype(F32) * _sigmoid(g_ref[...].astype(F32)), S)

        def tile(t, carry):
            base = pl.multiple_of(t * T, T)
            acc = jnp.broadcast_to(b_ref[...], (T, LANES))
            for j in range(CONV_KERNEL):
                acc = acc + w_ref[j:j + 1, :] * c0_ref[pl.ds(base + lo + j, T), :]
            o_ref[pl.ds(base, T), :] = acc
            return carry

        lax.fori_loop(0, S // T, tile, 0)

    nca, ncg = OFF_CA // LANES, OFF_CG // LANES
    return _call("conv_dw_fwd", body, grid=(CONV_WIDTH // LANES,),
                 in_specs=[pl.BlockSpec((S, LANES), lambda cb: (0, nca + cb)), pl.BlockSpec((S, LANES), lambda cb: (0, ncg + cb)),
                           pl.BlockSpec((32, LANES), lambda cb: (0, cb)), pl.BlockSpec((1, LANES), lambda cb: (0, cb))],
                 out_specs=pl.BlockSpec((S, LANES), lambda cb: (0, cb)),
                 out_shape=jax.ShapeDtypeStruct((S, CONV_WIDTH), F32),
                 scratch=[pltpu.VMEM((S + 2 * CONV_PAD, LANES), F32)], sem=("parallel",))(z, z, w32, b)


def _ln_stats(x):
    mu = jnp.mean(x, axis=-1, keepdims=True)
    xc = x - mu
    rs = lax.rsqrt(jnp.mean(xc * xc, axis=-1, keepdims=True) + EPS)
    return xc * rs, rs


def _ln_bwd(dy, xh, rs, g):
    dxh = dy * g
    return rs * (dxh - jnp.mean(dxh, axis=-1, keepdims=True) - xh * jnp.mean(dxh * xh, axis=-1, keepdims=True))


def conv_ln_fwd(c1, g, b, mix):
    S = c1.shape[0]
    T = min(512, S)

    def body(x_ref, g_ref, b_ref, mix_ref, o_ref):
        xh, _ = _ln_stats(x_ref[...])
        y = xh * g_ref[...] + b_ref[...]
        o_ref[...] = (y * _sigmoid(y)).astype(BF16)

    row = pl.BlockSpec((T, CONV_WIDTH), lambda i: (i, 0))
    vec = pl.BlockSpec((1, CONV_WIDTH), lambda i: (0, 0))
    return _call("conv_ln_fwd", body, grid=(S // T,), in_specs=[row, vec, vec, pl.BlockSpec(memory_space=pl.ANY)],
                 out_specs=pl.BlockSpec((T, CONV_WIDTH), lambda i: (i, ATTN_WIDTH // CONV_WIDTH)),
                 out_shape=jax.ShapeDtypeStruct(mix.shape, BF16), sem=("parallel",), aliases={3: 0})(c1, g, b, mix)


def _acc_out(ref, value, step=None):
    step = pl.program_id(0) if step is None else step

    @pl.when(step == 0)
    def _():
        ref[...] = value

    @pl.when(step > 0)
    def _():
        ref[...] += value


def conv_ln_bwd(dmix, c1, g, b):
    S = c1.shape[0]
    T = min(512, S)

    def body(d_ref, x_ref, g_ref, b_ref, dx_ref, dg_ref, db_ref):
        xh, rs = _ln_stats(x_ref[...])
        gv = g_ref[...]
        y = xh * gv + b_ref[...]
        sg = _sigmoid(y)
        dy = d_ref[...].astype(F32) * sg * (1.0 + y * (1.0 - sg))
        dx_ref[...] = _ln_bwd(dy, xh, rs, gv)
        _acc_out(dg_ref, jnp.sum(dy * xh, axis=0, keepdims=True))
        _acc_out(db_ref, jnp.sum(dy, axis=0, keepdims=True))

    row = pl.BlockSpec((T, CONV_WIDTH), lambda i: (i, 0))
    vec = pl.BlockSpec((1, CONV_WIDTH), lambda i: (0, 0))
    vshape = jax.ShapeDtypeStruct((1, CONV_WIDTH), F32)
    return _call("conv_ln_bwd", body, grid=(S // T,),
                 in_specs=[pl.BlockSpec((T, CONV_WIDTH), lambda i: (i, ATTN_WIDTH // CONV_WIDTH)), row, vec, vec],
                 out_specs=[row, vec, vec], out_shape=[jax.ShapeDtypeStruct((S, CONV_WIDTH), F32), vshape, vshape],
                 sem=("arbitrary",))(dmix, c1, g, b)


def conv_dw_bwd(dc1, z, w32, dz):
    S = z.shape[0]
    T = min(CONV_TILE, S)
    half = (CONV_KERNEL - 1) // 2
    lo = CONV_PAD - half
    n_cb = CONV_WIDTH // LANES

    def body(d_ref, a_ref, g_ref, w_ref, dz_ref, o_ref, dw_ref, db_ref, c0_ref, d1_ref, wacc_ref, dg_ref):
        @pl.when(pl.program_id(1) == 0)
        def _():
            av = a_ref[...].astype(F32)
            sg = _sigmoid(g_ref[...].astype(F32))
            _fill_padded(c0_ref, av * sg, S)
            _fill_padded(d1_ref, d_ref[...], S)
            wacc_ref[...] = jnp.zeros_like(wacc_ref)

            def tile(t, carry):
                base = pl.multiple_of(t * T, T)
                d1 = d_ref[pl.ds(base, T), :]
                acc = jnp.zeros((T, LANES), F32)
                for j in range(CONV_KERNEL):
                    acc = acc + w_ref[j:j + 1, :] * d1_ref[pl.ds(base + CONV_PAD + half - j, T), :]
                    prod = d1 * c0_ref[pl.ds(base + lo + j, T), :]
                    wacc_ref[j] += jnp.sum(prod.reshape(T // 8, 8, LANES), axis=0)
                rows = pl.ds(base, T)
                a_t = a_ref[rows, :].astype(F32)
                s_t = _sigmoid(g_ref[rows, :].astype(F32))
                o_ref[rows, :] = (acc * s_t).astype(BF16)
                dg_ref[rows, :] = (acc * a_t * s_t * (1.0 - s_t)).astype(BF16)
                return carry

            lax.fori_loop(0, S // T, tile, 0)
            dw_ref[...] = jnp.sum(wacc_ref[...], axis=1)
            db_ref[...] = jnp.sum(d_ref[...], axis=0, keepdims=True)

        @pl.when(pl.program_id(1) == 1)
        def _():
            o_ref[...] = dg_ref[...]

    nca, ncg = OFF_CA // LANES, OFF_CG // LANES
    return _call("conv_dw_bwd", body, grid=(n_cb, 2),
                 in_specs=[pl.BlockSpec((S, LANES), lambda cb, j: (0, cb)), pl.BlockSpec((S, LANES), lambda cb, j: (0, nca + cb)),
                           pl.BlockSpec((S, LANES), lambda cb, j: (0, ncg + cb)), pl.BlockSpec((32, LANES), lambda cb, j: (0, cb)),
                           pl.BlockSpec(memory_space=pl.ANY)],
                 out_specs=[pl.BlockSpec((S, LANES), lambda cb, j: (0, nca + cb + n_cb * j)),
                            pl.BlockSpec((32, LANES), lambda cb, j: (0, cb)), pl.BlockSpec((1, LANES), lambda cb, j: (0, cb))],
                 out_shape=[jax.ShapeDtypeStruct(dz.shape, BF16), jax.ShapeDtypeStruct((32, CONV_WIDTH), F32),
                            jax.ShapeDtypeStruct((1, CONV_WIDTH), F32)],
                 scratch=[pltpu.VMEM((S + 2 * CONV_PAD, LANES), F32), pltpu.VMEM((S + 2 * CONV_PAD, LANES), F32),
                          pltpu.VMEM((32, 8, LANES), F32), pltpu.VMEM((S, LANES), BF16)],
                 sem=("parallel", "arbitrary"), aliases={4: 0})(dc1, z, z, w32, dz)


_INV_SQRT2 = 1.0 / math.sqrt(2.0)
_INV_SQRT2PI = 1.0 / math.sqrt(2.0 * math.pi)


def _gelu(x):
    return 0.5 * x * (1.0 + lax.erf(x * _INV_SQRT2))


def _gelu_grad(x):
    return 0.5 * (1.0 + lax.erf(x * _INV_SQRT2)) + x * jnp.exp(-0.5 * x * x) * _INV_SQRT2PI


def sgu_fwd(z, g, b, ws, bs, mix):
    S = z.shape[0]
    T = min(512, S)

    def body(u_ref, v_ref, g_ref, b_ref, ws_ref, bs_ref, mix_ref, o_ref):
        xh, _ = _ln_stats(_gelu(v_ref[...].astype(F32)))
        vn = (xh * g_ref[...] + b_ref[...]).astype(BF16)
        for ch in range(T // CHUNK):
            rows = slice(ch * CHUNK, (ch + 1) * CHUNK)
            for h in range(SGU_HEADS):
                cols = slice(h * HEAD_DIM, (h + 1) * HEAD_DIM)
                sp = jnp.dot(ws_ref[h], vn[rows, cols], preferred_element_type=F32) + bs_ref[h]
                o_ref[rows, cols] = (_gelu(u_ref[rows, cols].astype(F32)) * sp).astype(BF16)

    vec = pl.BlockSpec((1, SGU_WIDTH), lambda i: (0, 0))
    full = pl.BlockSpec((SGU_HEADS, CHUNK, CHUNK), lambda i: (0, 0, 0))
    return _call("sgu_fwd", body, grid=(S // T,),
                 in_specs=[pl.BlockSpec((T, SGU_WIDTH), lambda i: (i, OFF_U // SGU_WIDTH)),
                           pl.BlockSpec((T, SGU_WIDTH), lambda i: (i, OFF_VV // SGU_WIDTH)), vec, vec, full, full,
                           pl.BlockSpec(memory_space=pl.ANY)],
                 out_specs=pl.BlockSpec((T, SGU_WIDTH), lambda i: (i, (ATTN_WIDTH + CONV_WIDTH) // SGU_WIDTH)),
                 out_shape=jax.ShapeDtypeStruct(mix.shape, BF16), sem=("parallel",), aliases={6: 0})(z, z, g, b, ws, bs, mix)


def sgu_bwd(z, dmix, g, b, ws, bs, dz):
    S = z.shape[0]
    T = min(512, S)

    def body(u_ref, v_ref, d_ref, g_ref, b_ref, ws_ref, bs_ref, dz_ref, o_ref, dws_ref, dbs_ref, dg_ref, db_ref, dvn_ref, dv_ref):
        tile = pl.program_id(0)
        first = pl.program_id(1) == 0

        @pl.when(first & (tile == 0))
        def _():
            dws_ref[...] = jnp.zeros_like(dws_ref)
            dbs_ref[...] = jnp.zeros_like(dbs_ref)

        @pl.when(first)
        def _():
            vraw = v_ref[...].astype(F32)
            xh, rs = _ln_stats(_gelu(vraw))
            gv = g_ref[...]
            vn = (xh * gv + b_ref[...]).astype(BF16)
            for ch in range(T // CHUNK):
                rows = slice(ch * CHUNK, (ch + 1) * CHUNK)
                for h in range(SGU_HEADS):
                    cols = slice(h * HEAD_DIM, (h + 1) * HEAD_DIM)
                    w = ws_ref[h]
                    vb = vn[rows, cols]
                    sp = jnp.dot(w, vb, preferred_element_type=F32) + bs_ref[h]
                    uraw = u_ref[rows, cols].astype(F32)
                    dout = d_ref[rows, cols].astype(F32)
                    o_ref[rows, cols] = (dout * sp * _gelu_grad(uraw)).astype(BF16)
                    dsp = dout * _gelu(uraw)
                    dspb = dsp.astype(BF16)
                    dvn_ref[rows, cols] = lax.dot_general(w, dspb, TN, preferred_element_type=F32)
                    dws_ref[h] += lax.dot_general(dspb, vb, NT, preferred_element_type=F32)
                    dbs_ref[h] += jnp.sum(dsp, axis=1, keepdims=True)
            dvn = dvn_ref[...]
            dv_ref[...] = (_ln_bwd(dvn, xh, rs, gv) * _gelu_grad(vraw)).astype(BF16)
            _acc_out(dg_ref, jnp.sum(dvn * xh, axis=0, keepdims=True), tile)
            _acc_out(db_ref, jnp.sum(dvn, axis=0, keepdims=True), tile)

        @pl.when(pl.program_id(1) == 1)
        def _():
            o_ref[...] = dv_ref[...]

    vec = pl.BlockSpec((1, SGU_WIDTH), lambda i, j: (0, 0))
    full = pl.BlockSpec((SGU_HEADS, CHUNK, CHUNK), lambda i, j: (0, 0, 0))
    vshape = jax.ShapeDtypeStruct((1, SGU_WIDTH), F32)
    return _call("sgu_bwd", body, grid=(S // T, 2),
                 in_specs=[pl.BlockSpec((T, SGU_WIDTH), lambda i, j: (i, OFF_U // SGU_WIDTH)),
                           pl.BlockSpec((T, SGU_WIDTH), lambda i, j: (i, OFF_VV // SGU_WIDTH)),
                           pl.BlockSpec((T, SGU_WIDTH), lambda i, j: (i, (ATTN_WIDTH + CONV_WIDTH) // SGU_WIDTH)), vec, vec, full, full,
                           pl.BlockSpec(memory_space=pl.ANY)],
                 out_specs=[pl.BlockSpec((T, SGU_WIDTH), lambda i, j: (i, OFF_U // SGU_WIDTH + j)), full,
                            pl.BlockSpec((SGU_HEADS, CHUNK, 1), lambda i, j: (0, 0, 0)), vec, vec],
                 out_shape=[jax.ShapeDtypeStruct(dz.shape, BF16), jax.ShapeDtypeStruct((SGU_HEADS, CHUNK, CHUNK), F32),
                            jax.ShapeDtypeStruct((SGU_HEADS, CHUNK, 1), F32), vshape, vshape],
                 scratch=[pltpu.VMEM((T, SGU_WIDTH), F32), pltpu.VMEM((T, SGU_WIDTH), BF16)],
                 sem=("arbitrary", "arbitrary"), aliases={7: 0})(z, z, dmix, g, b, ws, bs, dz)


def _row_tile(rows, cols, n_arrays, budget_mib=24):
    budget = (budget_mib * 1024 * 1024) // (n_arrays * 2 * 4 * cols)
    t = min(rows, max(16, budget // 16 * 16))
    while rows % t:
        t -= 16
    return t


def add_sibling_half(grad, recv, c_idx):
    J, R, C = grad.shape
    hr = R // 2
    tr = _row_tile(hr, C, 3)
    nb = hr // tr

    def body(c_ref, g_ref, r_ref, o_ref):
        o_ref[...] = (g_ref[...].astype(F32) + r_ref[...].astype(F32)).astype(BF16)

    grid_spec = pltpu.PrefetchScalarGridSpec(
        num_scalar_prefetch=1, grid=(J, nb),
        in_specs=[pl.BlockSpec((None, tr, C), lambda j, i, c: (j, c[0] * nb + i, 0)),
                  pl.BlockSpec((None, tr, C), lambda j, i, c: (j, i, 0))],
        out_specs=pl.BlockSpec((None, tr, C), lambda j, i, c: (j, i, 0)))
    return pl.pallas_call(body, name="add_sibling_half", grid_spec=grid_spec,
                          out_shape=jax.ShapeDtypeStruct((J, hr, C), BF16),
                          compiler_params=pltpu.CompilerParams(vmem_limit_bytes=VMEM_LIMIT,
                                                               dimension_semantics=("parallel", "parallel")))(c_idx, grad, recv)


def sum_chips(own, others, stack, x_idx, y_idx, layer):
    R, C = own.shape[1:]
    tr = _row_tile(R, C, 4)

    def body(x_ref, y_ref, own_ref, oth_ref, stack_ref, o_ref):
        acc = own_ref[...].astype(F32)
        for j in range(3):
            acc = acc + oth_ref[j].astype(F32)
        o_ref[...] = acc

    grid_spec = pltpu.PrefetchScalarGridSpec(
        num_scalar_prefetch=2, grid=(R // tr,),
        in_specs=[pl.BlockSpec((None, tr, C), lambda i, xr, yr: (2 * xr[0] + yr[0], i, 0)),
                  pl.BlockSpec((3, tr, C), lambda i, xr, yr: (0, i, 0)),
                  pl.BlockSpec(memory_space=pl.ANY)],
        out_specs=pl.BlockSpec((None, tr, C), lambda i, xr, yr: (layer, i, 0)))
    return pl.pallas_call(body, name="sum_chips", grid_spec=grid_spec,
                          out_shape=jax.ShapeDtypeStruct(stack.shape, F32), input_output_aliases={4: 0},
                          compiler_params=pltpu.CompilerParams(vmem_limit_bytes=VMEM_LIMIT,
                                                               dimension_semantics=("parallel",)))(x_idx, y_idx, own, others, stack)


def adamw_halves(w, mine, theirs, m, v, c_idx):
    L, R, C = w.shape
    hr = R // 2
    tr = _row_tile(hr, C, 9, budget_mib=40)
    nb = hr // tr

    def body(c_ref, w_ref, a_ref, b_ref, m_ref, v_ref, g_ref, d_ref, nm_ref, nv_ref):
        gv = jnp.where(pl.program_id(1) == c_ref[0], a_ref[...], b_ref[...])
        g_ref[...] = gv
        nm = ADAM_B1 * m_ref[...] + (1.0 - ADAM_B1) * gv
        nv = ADAM_B2 * v_ref[...] + (1.0 - ADAM_B2) * (gv * gv)
        m_hat = nm / (1.0 - ADAM_B1 ** ADAM_STEP)
        v_hat = nv / (1.0 - ADAM_B2 ** ADAM_STEP)
        d_ref[...] = -ADAM_LR * (m_hat / (jnp.sqrt(v_hat) + ADAM_EPS) + ADAM_WD * w_ref[...])
        nm_ref[...] = nm
        nv_ref[...] = nv

    full = pl.BlockSpec((None, tr, C), lambda l, h, i, c: (l, h * nb + i, 0))
    a_spec = pl.BlockSpec((None, tr, C), lambda l, h, i, c: (l, jnp.where(h == c[0], i, 0), 0))
    b_spec = pl.BlockSpec((None, tr, C), lambda l, h, i, c: (l, jnp.where(h == c[0], 0, i), 0))
    grid_spec = pltpu.PrefetchScalarGridSpec(num_scalar_prefetch=1, grid=(L, 2, nb),
                                             in_specs=[full, a_spec, b_spec, full, full], out_specs=[full] * 4)
    shape = jax.ShapeDtypeStruct((L, R, C), F32)
    return pl.pallas_call(body, name="adamw_halves", grid_spec=grid_spec, out_shape=[shape] * 4,
                          compiler_params=pltpu.CompilerParams(vmem_limit_bytes=VMEM_LIMIT,
                                                               dimension_semantics=("parallel", "arbitrary", "arbitrary")))(
        c_idx, w, mine, theirs, m, v)


def adamw(w, g, m, v):
    R, C = w.shape
    tr = _row_tile(R, C, 7)

    def body(w_ref, g_ref, m_ref, v_ref, d_ref, nm_ref, nv_ref):
        gv = g_ref[...]
        nm = ADAM_B1 * m_ref[...] + (1.0 - ADAM_B1) * gv
        nv = ADAM_B2 * v_ref[...] + (1.0 - ADAM_B2) * (gv * gv)
        m_hat = nm / (1.0 - ADAM_B1 ** ADAM_STEP)
        v_hat = nv / (1.0 - ADAM_B2 ** ADAM_STEP)
        d_ref[...] = -ADAM_LR * (m_hat / (jnp.sqrt(v_hat) + ADAM_EPS) + ADAM_WD * w_ref[...])
        nm_ref[...] = nm
        nv_ref[...] = nv

    spec = pl.BlockSpec((tr, C), lambda i: (i, 0))
    shape = jax.ShapeDtypeStruct((R, C), F32)
    return _call("adamw", body, grid=(R // tr,), in_specs=[spec] * 4, out_specs=[spec] * 3, out_shape=[shape] * 3,
                 sem=("parallel",))(w, g, m, v)


def _place():
    x, y, c = lax.axis_index("x"), lax.axis_index("y"), lax.axis_index("c")
    chips = [(1 - x, y), (x, 1 - y), (1 - x, 1 - y)]
    return x, y, c, chips


def _remote(src, dst, send_sem, recv_sem, dev):
    return pltpu.make_async_remote_copy(src_ref=src, dst_ref=dst, send_sem=send_sem, recv_sem=recv_sem,
                                        device_id=dev, device_id_type=MESH)


EFFECT = pltpu.SideEffectType.DATAFLOW_SIDE_EFFECTING
SEM = pl.BlockSpec(memory_space=pltpu.SEMAPHORE)
ANY = pl.BlockSpec(memory_space=pl.ANY)
TOKEN = jax.ShapeDtypeStruct((8, LANES), F32)


def _in_hbm(a):
    return pltpu.with_memory_space_constraint(a, pltpu.HBM)


def _gather_copies(shards, lands, send_sems, recv_sems):
    x, y, c, chips = _place()
    me = 2 * x + y
    copies = []
    for k in range(len(shards)):
        hr = shards[k].shape[0] // 2
        mine = pl.ds(pl.multiple_of(c * hr, 8), hr)
        for t, (px, py) in enumerate(chips):
            copies.append(_remote(shards[k].at[mine, :], lands[k].at[me, mine, :], send_sems.at[4 * k + t], recv_sems.at[4 * k + t],
                                  (px, py, c)))
        copies.append(_remote(shards[k], lands[k].at[me], send_sems.at[4 * k + 3], recv_sems.at[4 * k + 3], (x, y, 1 - c)))
    return copies


def _gather_landings(lands, send_sems, recv_sems):
    x, y, c, chips = _place()
    me = 2 * x + y
    landings = []
    for k in range(len(lands)):
        hr = lands[k].shape[1] // 2
        mine = pl.ds(pl.multiple_of(c * hr, 8), hr)
        for t, (px, py) in enumerate(chips):
            dst = lands[k].at[2 * px + py, mine, :]
            landings.append(_remote(dst, dst, send_sems.at[4 * k + t], recv_sems.at[4 * k + t], (px, py, c)))
        dst = lands[k].at[me]
        landings.append(_remote(dst, dst, send_sems.at[4 * k + 3], recv_sems.at[4 * k + 3], (x, y, 1 - c)))
    return landings


def gather_start(shards, after):
    n = len(shards)

    def body(*refs):
        srcs, lands_in = refs[:n], refs[n:2 * n]
        send_sems, recv_sems = refs[2 * n + 1], refs[2 * n + 2]
        token = refs[-1]
        for cp in _gather_copies(srcs, lands_in, send_sems, recv_sems):
            cp.start()
        token[...] = jnp.zeros_like(token)

    lands = [lax.empty((N_CHIPS,) + s.shape, s.dtype) for s in shards]
    outs = pl.pallas_call(
        body, name="gather_start", in_specs=[HBM] * (2 * n) + [ANY],
        out_specs=[SEM, SEM] + [HBM] * (2 * n) + [VMEM_SPEC],
        out_shape=[pltpu.SemaphoreType.DMA((4 * n,)), pltpu.SemaphoreType.DMA((4 * n,))]
        + [pltpu.HBM(s.shape, s.dtype) for s in shards] + [pltpu.HBM(l.shape, l.dtype) for l in lands] + [TOKEN],
        input_output_aliases={i: 2 + i for i in range(2 * n)},
        compiler_params=pltpu.CompilerParams(has_side_effects=EFFECT),
    )(*[_in_hbm(s) for s in shards], *[_in_hbm(l) for l in lands], after)
    return outs[0], outs[1], outs[2:2 + n], outs[2 + n:2 + 2 * n], outs[-1]


def gather_wait(send_sems, recv_sems, shards, lands, after):
    n = len(shards)

    def body(*refs):
        srcs, lands_in = refs[:n], refs[n:2 * n]
        send, recv = refs[2 * n], refs[2 * n + 1]
        for cp in _gather_copies(srcs, lands_in, send, recv):
            cp.wait_send()
        for cp in _gather_landings(lands_in, send, recv):
            cp.wait_recv()

    outs = pl.pallas_call(
        body, name="gather_wait", in_specs=[HBM] * (2 * n) + [SEM, SEM, ANY], out_specs=[HBM] * (2 * n),
        out_shape=[pltpu.HBM(s.shape, s.dtype) for s in shards] + [pltpu.HBM(l.shape, l.dtype) for l in lands],
        input_output_aliases={i: i for i in range(2 * n)},
        compiler_params=pltpu.CompilerParams(has_side_effects=EFFECT),
    )(*shards, *lands, send_sems, recv_sems, after)
    return outs[n:]


def _forward_copies(lands, send_sems, recv_sems, received):
    x, y, c, chips = _place()
    copies = []
    for k in range(len(lands)):
        hr = lands[k].shape[1] // 2
        half = (1 - c) if received else c
        rows = pl.ds(pl.multiple_of(half * hr, 8), hr)
        for t, (px, py) in enumerate(chips):
            block = lands[k].at[2 * px + py, rows, :]
            copies.append(_remote(block, block, send_sems.at[3 * k + t], recv_sems.at[3 * k + t], (x, y, 1 - c)))
    return copies


def forward_start(lands):
    n = len(lands)

    def body(*refs):
        for cp in _forward_copies(refs[:n], refs[n], refs[n + 1], received=False):
            cp.start()
        refs[-1][...] = jnp.zeros_like(refs[-1])

    outs = pl.pallas_call(
        body, name="forward_start", in_specs=[HBM] * n, out_specs=[SEM, SEM] + [HBM] * n + [VMEM_SPEC],
        out_shape=[pltpu.SemaphoreType.DMA((3 * n,)), pltpu.SemaphoreType.DMA((3 * n,))]
        + [pltpu.HBM(l.shape, l.dtype) for l in lands] + [TOKEN],
        input_output_aliases={i: 2 + i for i in range(n)},
        compiler_params=pltpu.CompilerParams(has_side_effects=EFFECT),
    )(*[_in_hbm(l) for l in lands])
    return outs[0], outs[1], outs[2:2 + n], outs[-1]


def forward_wait(send_sems, recv_sems, lands, after):
    n = len(lands)

    def body(*refs):
        for cp in _forward_copies(refs[:n], refs[n], refs[n + 1], received=False):
            cp.wait_send()
        for cp in _forward_copies(refs[:n], refs[n], refs[n + 1], received=True):
            cp.wait_recv()

    return pl.pallas_call(
        body, name="forward_wait", in_specs=[HBM] * n + [SEM, SEM, ANY], out_specs=[HBM] * n,
        out_shape=[pltpu.HBM(l.shape, l.dtype) for l in lands],
        input_output_aliases={i: i for i in range(n)},
        compiler_params=pltpu.CompilerParams(has_side_effects=EFFECT),
    )(*lands, send_sems, recv_sems, after)


def forward_halves(lands):
    n = len(lands)

    def body(*refs):
        ins, outs = refs[:n], refs[n:2 * n]
        send_sems, recv_sems = refs[2 * n:]
        x, y, c, chips = _place()
        sibling = (x, y, 1 - c)
        sends = []
        for k in range(n):
            hr = ins[k].shape[1] // 2
            mine = pl.ds(pl.multiple_of(c * hr, 8), hr)
            for t, (px, py) in enumerate(chips):
                cp = _remote(ins[k].at[2 * px + py, mine, :], outs[k].at[2 * px + py, mine, :],
                             send_sems.at[k, t], recv_sems.at[k, t], sibling)
                cp.start()
                sends.append(cp)
        for k in range(n):
            hr = ins[k].shape[1] // 2
            other = pl.ds(pl.multiple_of((1 - c) * hr, 8), hr)
            for t, (px, py) in enumerate(chips):
                dst = outs[k].at[2 * px + py, other, :]
                _remote(dst, dst, send_sems.at[k, t], recv_sems.at[k, t], sibling).wait_recv()
        for cp in sends:
            cp.wait_send()

    return pl.pallas_call(
        body, name="forward_halves", in_specs=[HBM] * n, out_specs=[HBM] * n,
        out_shape=[jax.ShapeDtypeStruct(l.shape, l.dtype) for l in lands],
        input_output_aliases={i: i for i in range(n)},
        scratch_shapes=[pltpu.SemaphoreType.DMA((n, 3)), pltpu.SemaphoreType.DMA((n, 3))],
    )(*lands)


def gather_small(block):
    def body(in_ref, out_ref, send_sems, recv_sems):
        x, y, c, chips = _place()
        me = 2 * x + y
        out_ref[me] = in_ref[...]
        sends = []
        for t, (px, py) in enumerate(chips):
            cp = _remote(in_ref, out_ref.at[me], send_sems.at[t], recv_sems.at[t], (px, py, c))
            cp.start()
            sends.append(cp)
        for t, (px, py) in enumerate(chips):
            landed = out_ref.at[2 * px + py]
            _remote(landed, landed, send_sems.at[t], recv_sems.at[t], (px, py, c)).wait_recv()
        for cp in sends:
            cp.wait_send()

    return pl.pallas_call(
        body, name="gather_small", in_specs=[VMEM_SPEC], out_specs=VMEM_SPEC,
        out_shape=jax.ShapeDtypeStruct((N_CHIPS,) + block.shape, block.dtype),
        scratch_shapes=[pltpu.SemaphoreType.DMA((3,)), pltpu.SemaphoreType.DMA((3,))],
    )(block)


def exchange_sibling_halves(grads):
    n = len(grads)

    def body(*refs):
        ins, outs = refs[:n], refs[n:2 * n]
        send_sems, recv_sems = refs[2 * n:]
        x, y, c, _ = _place()
        copies = []
        for k in range(n):
            hr = ins[k].shape[1] // 2
            theirs = pl.ds(pl.multiple_of((1 - c) * hr, 8), hr)
            cp = _remote(ins[k].at[:, theirs, :], outs[k], send_sems.at[k], recv_sems.at[k], (x, y, 1 - c))
            cp.start()
            copies.append(cp)
        for cp in copies:
            cp.wait()

    return pl.pallas_call(
        body, name="exchange_sibling_halves", in_specs=[HBM] * n, out_specs=[HBM] * n,
        out_shape=[jax.ShapeDtypeStruct((g.shape[0], g.shape[1] // 2, g.shape[2]), g.dtype) for g in grads],
        scratch_shapes=[pltpu.SemaphoreType.DMA((n,)), pltpu.SemaphoreType.DMA((n,))],
    )(*grads)


def _sibling_half_copies(grads, lands, send_sems, recv_sems):
    x, y, c, _ = _place()
    copies = []
    for k in range(len(grads)):
        hr = grads[k].shape[1] // 2
        theirs = pl.ds(pl.multiple_of((1 - c) * hr, 8), hr)
        copies.append(_remote(grads[k].at[:, theirs, :], lands[k], send_sems.at[k], recv_sems.at[k], (x, y, 1 - c)))
    return copies


def _sibling_whole_copies(srcs, lands, send_sems, recv_sems):
    x, y, c, _ = _place()
    return [_remote(srcs[k], lands[k], send_sems.at[k], recv_sems.at[k], (x, y, 1 - c)) for k in range(len(srcs))]


def pair_start(name, make_copies, srcs, land_shapes):
    n = len(srcs)

    def body(*refs):
        src_refs, land_refs = refs[:n], refs[n:2 * n]
        send_sems, recv_sems = refs[2 * n], refs[2 * n + 1]
        token = refs[-1]
        for cp in make_copies(src_refs, land_refs, send_sems, recv_sems):
            cp.start()
        token[...] = jnp.zeros_like(token)

    lands = [lax.empty(shape, s.dtype) for shape, s in zip(land_shapes, srcs)]
    outs = pl.pallas_call(
        body, name=name, in_specs=[HBM] * (2 * n), out_specs=[SEM, SEM] + [HBM] * (2 * n) + [VMEM_SPEC],
        out_shape=[pltpu.SemaphoreType.DMA((n,)), pltpu.SemaphoreType.DMA((n,))]
        + [pltpu.HBM(s.shape, s.dtype) for s in srcs] + [pltpu.HBM(l.shape, l.dtype) for l in lands] + [TOKEN],
        input_output_aliases={i: 2 + i for i in range(2 * n)},
        compiler_params=pltpu.CompilerParams(has_side_effects=EFFECT),
    )(*[_in_hbm(s) for s in srcs], *[_in_hbm(l) for l in lands])
    return outs[0], outs[1], outs[2:2 + n], outs[2 + n:2 + 2 * n], outs[-1]


def pair_wait_one(name, send_sems, recv_sems, src, land, after, index):
    def body(src_ref, land_ref, send, recv, after_ref, src_out, land_out):
        x, y, c, _ = _place()
        cp = _remote(src_ref, land_ref, send.at[index], recv.at[index], (x, y, 1 - c))
        cp.wait_send()
        cp.wait_recv()

    return pl.pallas_call(
        body, name=name, in_specs=[HBM, HBM, SEM, SEM, ANY], out_specs=[HBM, HBM],
        out_shape=[pltpu.HBM(src.shape, src.dtype), pltpu.HBM(land.shape, land.dtype)],
        input_output_aliases={0: 0, 1: 1},
        compiler_params=pltpu.CompilerParams(has_side_effects=EFFECT),
    )(src, land, send_sems, recv_sems, after)


def pair_wait(name, make_copies, send_sems, recv_sems, srcs, lands, after):
    n = len(srcs)

    def body(*refs):
        src_refs, land_refs = refs[:n], refs[n:2 * n]
        for cp in make_copies(src_refs, land_refs, refs[2 * n], refs[2 * n + 1]):
            cp.wait_send()
            cp.wait_recv()

    outs = pl.pallas_call(
        body, name=name, in_specs=[HBM] * (2 * n) + [SEM, SEM, ANY], out_specs=[HBM] * (2 * n),
        out_shape=[pltpu.HBM(s.shape, s.dtype) for s in srcs] + [pltpu.HBM(l.shape, l.dtype) for l in lands],
        input_output_aliases={i: i for i in range(2 * n)},
        compiler_params=pltpu.CompilerParams(has_side_effects=EFFECT),
    )(*srcs, *lands, send_sems, recv_sems, after)
    return outs[:n], outs[n:]


def _chip_copies(parts, lands, send_sems, recv_sems):
    x, y, c, chips = _place()
    return [_remote(parts[k].at[2 * px + py], lands[k].at[t], send_sems.at[3 * k + t], recv_sems.at[3 * k + t], (px, py, c))
            for k in range(len(parts)) for t, (px, py) in enumerate(chips)]


def chip_parts_start(parts):
    n = len(parts)

    def body(*refs):
        srcs, lands_in = refs[:n], refs[n:2 * n]
        send_sems, recv_sems = refs[2 * n], refs[2 * n + 1]
        token = refs[-1]
        for cp in _chip_copies(srcs, lands_in, send_sems, recv_sems):
            cp.start()
        token[...] = jnp.zeros_like(token)

    lands = [lax.empty((3,) + p.shape[1:], p.dtype) for p in parts]
    outs = pl.pallas_call(
        body, name="chip_parts_start", in_specs=[HBM] * (2 * n), out_specs=[SEM, SEM] + [HBM] * (2 * n) + [VMEM_SPEC],
        out_shape=[pltpu.SemaphoreType.DMA((3 * n,)), pltpu.SemaphoreType.DMA((3 * n,))]
        + [pltpu.HBM(p.shape, p.dtype) for p in parts] + [pltpu.HBM(l.shape, l.dtype) for l in lands] + [TOKEN],
        input_output_aliases={i: 2 + i for i in range(2 * n)},
        compiler_params=pltpu.CompilerParams(has_side_effects=EFFECT),
    )(*[_in_hbm(p) for p in parts], *[_in_hbm(l) for l in lands])
    return outs[0], outs[1], outs[2:2 + n], outs[2 + n:2 + 2 * n], outs[-1]


def chip_parts_wait(send_sems, recv_sems, parts, lands, after):
    n = len(parts)

    def body(*refs):
        srcs, lands_in = refs[:n], refs[n:2 * n]
        send, recv = refs[2 * n], refs[2 * n + 1]
        for cp in _chip_copies(srcs, lands_in, send, recv):
            cp.wait_send()
            cp.wait_recv()

    outs = pl.pallas_call(
        body, name="chip_parts_wait", in_specs=[HBM] * (2 * n) + [SEM, SEM, ANY], out_specs=[HBM] * (2 * n),
        out_shape=[pltpu.HBM(p.shape, p.dtype) for p in parts] + [pltpu.HBM(l.shape, l.dtype) for l in lands],
        input_output_aliases={i: i for i in range(2 * n)},
        compiler_params=pltpu.CompilerParams(has_side_effects=EFFECT),
    )(*parts, *lands, send_sems, recv_sems, after)
    return outs[:n], outs[n:]


def allreduce_small(packed):
    R = packed.shape[0]

    def body(x_ref, sum_ref, all_ref, send_sems, recv_sems):
        x, y, c, chips = _place()
        me, sibling = (x, y, c), (x, y, 1 - c)

        def rows(px, py, pc):
            return all_ref.at[4 * px + 2 * py + pc]

        def copy(k, block, to, src=None):
            return _remote(rows(*block) if src is None else src, rows(*block), send_sems.at[k], recv_sems.at[k], to)

        all_ref[4 * x + 2 * y + c] = x_ref[...]
        first = [copy(0, me, sibling, src=x_ref)]
        first += [copy(1 + j, me, (*chip, c), src=x_ref) for j, chip in enumerate(chips)]
        for cp in first:
            cp.start()
        passed = [copy(4 + j, (*chip, c), sibling) for j, chip in enumerate(chips)]
        for j, chip in enumerate(chips):
            copy(1 + j, (*chip, c), me).wait_recv()
            passed[j].start()
        copy(0, sibling, me).wait_recv()
        for j, chip in enumerate(chips):
            copy(4 + j, (*chip, 1 - c), me).wait_recv()
        for cp in first + passed:
            cp.wait_send()

        def chunk(i, carry):
            rws = pl.ds(pl.multiple_of(i * PACK_ROWS, PACK_ROWS), PACK_ROWS)
            acc = all_ref[0, rws, :]
            for d in range(1, N_DEV):
                acc = acc + all_ref[d, rws, :]
            sum_ref[rws, :] = acc
            return carry

        lax.fori_loop(0, R // PACK_ROWS, chunk, 0)

    return pl.pallas_call(
        body, name="allreduce_small", in_specs=[VMEM_SPEC], out_specs=VMEM_SPEC,
        out_shape=jax.ShapeDtypeStruct((R, LANES), F32),
        scratch_shapes=[pltpu.VMEM((N_DEV, R, LANES), F32), pltpu.SemaphoreType.DMA((7,)), pltpu.SemaphoreType.DMA((7,))],
        compiler_params=pltpu.CompilerParams(vmem_limit_bytes=VMEM_LIMIT),
    )(packed)


def _mixer_fwd(x, h, p, tabs, token, late_weights=None):
    z = mm_nn_cols(h, p["w_in"], token)
    qr, kp, vp = rope_fwd(z, tabs)
    mix = attn_fwd(qr, kp, vp, p["sink3"])
    c1 = conv_dw_fwd(z, p["conv_w32"], p["conv_dw_b"])
    mix = conv_ln_fwd(c1, p["conv_ln_g"], p["conv_ln_b"], mix)
    mix = sgu_fwd(z, p["sgu_ln_g"], p["sgu_ln_b"], p["sgu_w16"], p["sgu_b3"], mix)
    if late_weights is not None:
        p.update(late_weights(mix))
    x_mid, h2 = mm_nn_rows_res(mix, p["w_out"], x, p["ffn_norm_g"])
    return x_mid, h2, dict(x=x, h=h, z=z, qr=qr, kp=kp, vp=vp, c1=c1, mix=mix, x_mid=x_mid)


def _ffn_fwd(x_mid, h2, p, next_gain, token):
    gate, up, act = ffn_up(h2, p["w_gate"], p["w_up"], token)
    x_out, h_next = mm_nn_rows_res(act, p["w_down"], x_mid, next_gain)
    return x_out, h_next, dict(h2=h2, gate=gate, up=up, act=act)


def _layer_fwd(x, h, p, next_gain, tabs, token):
    x_mid, h2, s_mix = _mixer_fwd(x, h, p, tabs, token)
    x_out, h_next, s_ffn = _ffn_fwd(x_mid, h2, p, next_gain, token)
    return x_out, h_next, {**s_mix, **s_ffn}


def _ffn_bwd(dx, dxb, p, s, token):
    dgate, dup = ffn_down_bwd(dxb, p["w_down"], s["gate"], s["up"], token)
    g_down = mm_tn_rows(s["act"], dxb)
    dh2 = mm_nt_cols([(dgate, p["w_gate"]), (dup, p["w_up"])], BF16, 1)
    g_gate = mm_tn_cols(s["h2"], dgate, N_CHIPS)
    g_up = mm_tn_cols(s["h2"], dup, N_CHIPS)
    dmid, dmidb, g_ffn_norm = rms_bwd(s["x_mid"], p["ffn_norm_g"], dh2, dx)
    return dmid, dmidb, [g_gate, g_up, g_down.reshape(N_CHIPS, -1, D_MODEL)], g_ffn_norm


def _mixer_bwd(dmid, dmidb, p, s, tabs, token):
    dmix = mm_nt_rows(dmidb, p["w_out"], token)
    g_out = mm_tn_rows(s["mix"], dmidb)
    dq, dkp, dvp, dsink = attn_bwd(s["qr"], s["kp"], s["vp"], p["sink3"], dmix)
    dz = rope_bwd(dq, dkp, dvp, tabs)
    dc1, g_cln_g, g_cln_b = conv_ln_bwd(dmix, s["c1"], p["conv_ln_g"], p["conv_ln_b"])
    dz, g_cw, g_cb = conv_dw_bwd(dc1, s["z"], p["conv_w32"], dz)
    dz, g_sw, g_sb, g_sln_g, g_sln_b = sgu_bwd(s["z"], dmix, p["sgu_ln_g"], p["sgu_ln_b"], p["sgu_w16"], p["sgu_b3"], dz)
    dh = mm_nt_cols([(dz, p["w_in"])], BF16, N_CHIPS)
    g_in = mm_tn_cols(s["h"], dz, N_CHIPS)
    dx_in, dxb_in, g_mix_norm = rms_bwd(s["x"], p["mix_norm_g"], dh, dmid)
    small = dict(mix_norm_g=g_mix_norm, sink=dsink[:, :, 0].reshape(1, N_Q_HEADS), conv_dw_w=g_cw[:CONV_KERNEL],
                 conv_dw_b=g_cb, conv_ln_g=g_cln_g, conv_ln_b=g_cln_b, sgu_ln_g=g_sln_g, sgu_ln_b=g_sln_b,
                 sgu_w=g_sw, sgu_b=g_sb[:, :, 0])
    return dx_in, dxb_in, [g_in, g_out.reshape(N_CHIPS, -1, D_MODEL)], small


def _layer_bwd(dx, dxb, p, s, tabs, token):
    dmid, dmidb, ffn_big, g_ffn_norm = _ffn_bwd(dx, dxb, p, s, token)
    dx_in, dxb_in, mix_big, small = _mixer_bwd(dmid, dmidb, p, s, tabs, token)
    return dx_in, dxb_in, mix_big + ffn_big, dict(small, ffn_norm_g=g_ffn_norm)


def _mixer_weights(gathered):
    w_in, w_out = gathered
    return dict(w_in=w_in, w_out=w_out.reshape(-1, D_MODEL))


def _ffn_weights(gathered):
    w_gate, w_up, w_down = gathered
    return dict(w_gate=w_gate, w_up=w_up, w_down=w_down.reshape(-1, D_MODEL))


def _small_params(l, conv_w_full, mix_norm_g, sink, conv_dw_b, conv_ln_g, conv_ln_b, sgu_ln_g, sgu_ln_b, sgu_w, sgu_b,
                  ffn_norm_g):
    return dict(
        mix_norm_g=mix_norm_g[l:l + 1], ffn_norm_g=ffn_norm_g[l:l + 1],
        sink3=jnp.broadcast_to(sink[l].reshape(N_KV_HEADS, Q_PER_KV, 1), (N_KV_HEADS, Q_PER_KV, LANES)),
        conv_w32=jnp.pad(conv_w_full[l], ((0, 32 - CONV_KERNEL), (0, 0))),
        conv_dw_b=conv_dw_b[l:l + 1], conv_ln_g=conv_ln_g[l:l + 1], conv_ln_b=conv_ln_b[l:l + 1],
        sgu_ln_g=sgu_ln_g[l:l + 1], sgu_ln_b=sgu_ln_b[l:l + 1], sgu_w16=sgu_w[l].astype(BF16),
        sgu_b3=jnp.broadcast_to(sgu_b[l][:, :, None], (SGU_HEADS, CHUNK, CHUNK)))


_SMALL = ["mix_norm_g", "sink", "conv_dw_b", "conv_ln_g", "conv_ln_b", "sgu_ln_g", "sgu_ln_b", "sgu_w", "sgu_b", "ffn_norm_g",
          "final_norm_g"]


def _pack_rows(arrays):
    rows, counts = [], []
    for a in arrays:
        flat = a.reshape(-1)
        n = -(-flat.shape[0] // LANES)
        rows.append(jnp.pad(flat, (0, n * LANES - flat.shape[0])).reshape(n, LANES))
        counts.append(n)
    packed = jnp.concatenate(rows, axis=0)
    pad = -packed.shape[0] % PACK_ROWS
    return jnp.pad(packed, ((0, pad), (0, 0))), counts


def _unpack_rows(packed, counts, shapes):
    out, r = [], 0
    for n, shape in zip(counts, shapes):
        size = math.prod(shape)
        out.append(packed[r:r + n].reshape(-1)[:size].reshape(shape))
        r += n
    return out


def kernel(x, mix_norm_g, w_in, sink, conv_dw_w, conv_dw_b, conv_ln_g, conv_ln_b, sgu_ln_g, sgu_ln_b, sgu_w, sgu_b, w_out, ffn_norm_g, w_gate, w_up, w_down, final_norm_g, loss_target, m_mix_norm_g, m_w_in, m_sink, m_conv_dw_w, m_conv_dw_b, m_conv_ln_g, m_conv_ln_b, m_sgu_ln_g, m_sgu_ln_b, m_sgu_w, m_sgu_b, m_w_out, m_ffn_norm_g, m_w_gate, m_w_up, m_w_down, m_final_norm_g, v_mix_norm_g, v_w_in, v_sink, v_conv_dw_w, v_conv_dw_b, v_conv_ln_g, v_conv_ln_b, v_sgu_ln_g, v_sgu_ln_b, v_sgu_w, v_sgu_b, v_w_out, v_ffn_norm_g, v_w_gate, v_w_up, v_w_down, v_final_norm_g):
    S = x.shape[1]
    my_chip = 2 * lax.axis_index("x") + lax.axis_index("y")
    c_idx = lax.axis_index("c").astype(jnp.int32).reshape(1)
    big_w = [w_in, w_out, w_gate, w_up, w_down]
    big_m = [m_w_in, m_w_out, m_w_gate, m_w_up, m_w_down]
    big_v = [v_w_in, v_w_out, v_w_gate, v_w_up, v_w_down]
    n_kinds = len(big_w)

    x_idx = lax.axis_index("x").astype(jnp.int32).reshape(1)
    y_idx = lax.axis_index("y").astype(jnp.int32).reshape(1)
    conv_w_all = gather_small(conv_dw_w)
    conv_w_full = jnp.transpose(conv_w_all, (1, 2, 0, 3)).reshape(DEPTH, CONV_KERNEL, CONV_WIDTH)
    tabs = rope_tables(S)
    no_token = jnp.zeros(TOKEN.shape, TOKEN.dtype)

    mixer_kinds, ffn_kinds = [0, 1], [2, 3, 4]
    shards = [[w[l].astype(BF16) for w in big_w] for l in range(DEPTH)]

    def fetch(pending, after):
        send_sems, recv_sems, srcs, lands, _ = pending
        return forward_halves(gather_wait(send_sems, recv_sems, srcs, lands, after))

    first_mixer = gather_start([shards[0][k] for k in mixer_kinds], conv_w_all)
    first_ffn = gather_start([shards[0][k] for k in ffn_kinds], first_mixer[4])
    pending = gather_start(shards[1], first_ffn[4])
    act = x[0]
    h = rms_fwd(act, mix_norm_g[0:1], no_token)
    saved, params = [], []
    for l in range(DEPTH):
        p = _small_params(l, conv_w_full, mix_norm_g, sink, conv_dw_b, conv_ln_g, conv_ln_b, sgu_ln_g, sgu_ln_b, sgu_w, sgu_b,
                          ffn_norm_g)
        next_gain = mix_norm_g[l + 1:l + 2] if l + 1 < DEPTH else final_norm_g.reshape(1, D_MODEL)
        if l == 0:
            p.update(_mixer_weights(fetch(first_mixer, act)))
            x_mid, h2, s_mix = _mixer_fwd(act, h, p, tabs, pending[4])
            p.update(_ffn_weights(fetch(first_ffn, x_mid)))
            late, token = None, no_token
        else:
            send_sems, recv_sems, srcs, lands, _ = pending
            lands = gather_wait(send_sems, recv_sems, srcs, lands, act)
            w_in_full = forward_halves(lands[:1])[0]
            p.update(w_in=w_in_full)
            fwd_send, fwd_recv, rest, token = forward_start(lands[1:])

            def late(mix, fwd_send=fwd_send, fwd_recv=fwd_recv, rest=rest):
                w_out_full, *ffn_full = forward_wait(fwd_send, fwd_recv, rest, mix)
                return dict(_ffn_weights(ffn_full), w_out=w_out_full.reshape(-1, D_MODEL))

            if l + 1 < DEPTH:
                pending = gather_start(shards[l + 1], w_in_full)
                token = token + pending[4]
        if l > 0:
            x_mid, h2, s_mix = _mixer_fwd(act, h, p, tabs, token, late)
        act, h, s_ffn = _ffn_fwd(x_mid, h2, p, next_gain, token)
        params.append(p)
        saved.append({**s_mix, **s_ffn})
    loss_part, dx, dxb, g_final = final_loss(act, final_norm_g.reshape(1, D_MODEL), loss_target[0])
    loss = lax.psum(loss_part[0, 0], ("x", "y", "c"))

    halves = [lax.empty((DEPTH, w.shape[1] // 2, w.shape[2]), F32) for w in big_w]
    small_grads = [None] * DEPTH

    def chip_start(layer, kinds, grads, recv):
        chip_sum = [add_sibling_half(g, r, c_idx) for g, r in zip(grads, recv)]
        send_sems, recv_sems, parts, lands, token = chip_parts_start(chip_sum)
        return (layer, kinds, send_sems, recv_sems, parts, lands), token

    def reduce_start(layer, kinds, grads):
        return chip_start(layer, kinds, grads, exchange_sibling_halves(grads))

    def reduce_finish(pending, halves, after):
        layer, kinds, send_sems, recv_sems, parts, lands = pending
        parts, others = chip_parts_wait(send_sems, recv_sems, parts, lands, after)
        halves = list(halves)
        for i, k in enumerate(kinds):
            halves[k] = sum_chips(parts[i], others[i], halves[k], x_idx, y_idx, layer)
        return halves

    pending, token = None, no_token
    for l in reversed(range(DEPTH)):
        dmid, dmidb, ffn_big, g_ffn_norm = _ffn_bwd(dx, dxb, params[l], saved[l], token)
        if l == 0:
            last_ffn, mixer_token = reduce_start(l, ffn_kinds, ffn_big)
        else:
            half_shapes = [(g.shape[0], g.shape[1] // 2, g.shape[2]) for g in ffn_big]
            sib_send, sib_recv, ffn_big, ffn_lands, mixer_token = pair_start("sibling_start", _sibling_half_copies, ffn_big, half_shapes)
        dx, dxb, mix_big, small = _mixer_bwd(dmid, dmidb, params[l], saved[l], tabs, mixer_token)
        small_grads[l] = dict(small, ffn_norm_g=g_ffn_norm)
        if pending is not None:
            halves = reduce_finish(pending, halves, dx)
        if l == 0:
            last_mixer, token = reduce_start(l, mixer_kinds, mix_big)
            halves = reduce_finish(last_ffn, halves, token)
        else:
            ffn_big, ffn_recv = pair_wait("sibling_wait", _sibling_half_copies, sib_send, sib_recv, ffn_big, ffn_lands, dx)
            mix_recv = exchange_sibling_halves(mix_big)
            pending, token = chip_start(l, mixer_kinds + ffn_kinds, list(mix_big) + list(ffn_big), list(mix_recv) + list(ffn_recv))

    def final_start(kinds):
        send_sems, recv_sems, mine, lands, _ = pair_start("final_start", _sibling_whole_copies, [halves[k] for k in kinds],
                                                          [halves[k].shape for k in kinds])
        return send_sems, recv_sems, mine, lands

    ffn_final = final_start(ffn_kinds)

    stacked = {n: jnp.stack([small_grads[l][n] for l in range(DEPTH)]) for n in small_grads[0]}
    stacked["final_norm_g"] = g_final
    packed, counts = _pack_rows([stacked[n] for n in _SMALL] + [stacked["conv_dw_w"]])
    reduced = allreduce_small(packed)
    small_w = dict(mix_norm_g=mix_norm_g, sink=sink, conv_dw_b=conv_dw_b, conv_ln_g=conv_ln_g, conv_ln_b=conv_ln_b,
                   sgu_ln_g=sgu_ln_g, sgu_ln_b=sgu_ln_b, sgu_w=sgu_w, sgu_b=sgu_b, ffn_norm_g=ffn_norm_g,
                   final_norm_g=final_norm_g)
    small_m = dict(mix_norm_g=m_mix_norm_g, sink=m_sink, conv_dw_b=m_conv_dw_b, conv_ln_g=m_conv_ln_g,
                   conv_ln_b=m_conv_ln_b, sgu_ln_g=m_sgu_ln_g, sgu_ln_b=m_sgu_ln_b, sgu_w=m_sgu_w, sgu_b=m_sgu_b,
                   ffn_norm_g=m_ffn_norm_g, final_norm_g=m_final_norm_g)
    small_v = dict(mix_norm_g=v_mix_norm_g, sink=v_sink, conv_dw_b=v_conv_dw_b, conv_ln_g=v_conv_ln_g,
                   conv_ln_b=v_conv_ln_b, sgu_ln_g=v_sgu_ln_g, sgu_ln_b=v_sgu_ln_b, sgu_w=v_sgu_w, sgu_b=v_sgu_b,
                   ffn_norm_g=v_ffn_norm_g, final_norm_g=v_final_norm_g)
    shapes = [small_w[n].shape for n in _SMALL] + [(DEPTH, CONV_KERNEL, CONV_WIDTH)]
    red = _unpack_rows(reduced, counts, shapes)
    g_small = dict(zip(_SMALL, red[:-1]))
    g_small["conv_dw_w"] = lax.dynamic_slice_in_dim(red[-1], my_chip * LANES, LANES, axis=2)
    small_w["conv_dw_w"], small_m["conv_dw_w"], small_v["conv_dw_w"] = conv_dw_w, m_conv_dw_w, v_conv_dw_w
    names = _SMALL + ["conv_dw_w"]
    pw, cnt = _pack_rows([small_w[n] for n in names])
    pg, _ = _pack_rows([g_small[n] for n in names])
    pm, _ = _pack_rows([small_m[n] for n in names])
    pv, _ = _pack_rows([small_v[n] for n in names])
    sd, sm, sv = adamw(pw, pg, pm, pv)
    shp = [small_w[n].shape for n in names]
    d_small = dict(zip(names, _unpack_rows(sd, cnt, shp)))
    m_small = dict(zip(names, _unpack_rows(sm, cnt, shp)))
    v_small = dict(zip(names, _unpack_rows(sv, cnt, shp)))

    big_names = ["w_in", "w_out", "w_gate", "w_up", "w_down"]
    g_big, d_big, m_big, v_big = {}, {}, {}, {}
    after = sd
    for kinds, final in ((ffn_kinds, ffn_final), (mixer_kinds, None)):
        if final is None:
            halves = reduce_finish(last_mixer, halves, after)
            final = final_start(kinds)
        send_sems, recv_sems, sent, lands = final
        for i, k in enumerate(kinds):
            n = big_names[k]
            mine, theirs = pair_wait_one("final_wait", send_sems, recv_sems, sent[i], lands[i], after, i)
            g_big[n], d_big[n], m_big[n], v_big[n] = adamw_halves(big_w[k], mine, theirs, big_m[k], big_v[k], c_idx)
            after = d_big[n]

    order = ["mix_norm_g", "w_in", "sink", "conv_dw_w", "conv_dw_b", "conv_ln_g", "conv_ln_b", "sgu_ln_g", "sgu_ln_b",
             "sgu_w", "sgu_b", "w_out", "ffn_norm_g", "w_gate", "w_up", "w_down", "final_norm_g"]
    grads = {**g_small, **g_big}
    deltas = {**d_small, **d_big}
    new_m = {**m_small, **m_big}
    new_v = {**v_small, **v_big}
    return (loss, dx[None], *[grads[n] for n in order], *[deltas[n] for n in order],
            *[new_m[n] for n in order], *[new_v[n] for n in order])
```

```python
import functools
import math

import jax
import jax.numpy as jnp
from jax import lax
from jax.experimental import pallas as pl
from jax.experimental.pallas import tpu as pltpu

F32, BF16 = jnp.float32, jnp.bfloat16

D_MODEL = 2048
DEPTH = 4
HEAD_DIM = 128
N_Q_HEADS = 8
N_KV_HEADS = 2
Q_PER_KV = N_Q_HEADS // N_KV_HEADS
ATTN_WIDTH = N_Q_HEADS * HEAD_DIM
KV_WIDTH = N_KV_HEADS * HEAD_DIM
CONV_WIDTH = 512
CONV_KERNEL = 31
CONV_PAD = 16
SGU_WIDTH = 512
SGU_HEADS = 4
CHUNK = 128
IN_WIDTH = 3584
D_FF = 5632
WINDOW = 128
ROT_DIM = 32
ROPE_THETA = 500000.0
EPS = 1e-6
N_CHIPS = 4
N_DEV = 8
LANES = 128
PACK_ROWS = 64
OFF_K = ATTN_WIDTH
OFF_V = OFF_K + KV_WIDTH
OFF_CA = OFF_V + KV_WIDTH
OFF_CG = OFF_CA + CONV_WIDTH
OFF_U = OFF_CG + CONV_WIDTH
OFF_VV = OFF_U + SGU_WIDTH

ADAM_LR, ADAM_B1, ADAM_B2, ADAM_EPS, ADAM_WD, ADAM_STEP = 0.001, 0.9, 0.999, 1e-08, 0.01, 10

VMEM_LIMIT = 56 * 1024 * 1024
MESH = pl.DeviceIdType.MESH
HBM = pl.BlockSpec(memory_space=pltpu.HBM)
VMEM_SPEC = pl.BlockSpec(memory_space=pltpu.VMEM)


def _call(name, body, *, grid, in_specs, out_specs, out_shape, scratch=(), sem=None, aliases=None):
    params = dict(vmem_limit_bytes=VMEM_LIMIT)
    if sem is not None:
        params["dimension_semantics"] = sem
    return pl.pallas_call(
        body, name=name, grid=grid, in_specs=in_specs, out_specs=out_specs, out_shape=out_shape,
        scratch_shapes=list(scratch), input_output_aliases=aliases or {}, compiler_params=pltpu.CompilerParams(**params))


def _sigmoid(x):
    return 1.0 / (1.0 + jnp.exp(-x))


def rms_fwd(x, g, token):
    S = x.shape[0]
    tm = min(512, S)

    def body(x_ref, g_ref, token_ref, o_ref):
        xv = x_ref[...]
        r = lax.rsqrt(jnp.mean(xv * xv, axis=-1, keepdims=True) + EPS)
        o_ref[...] = (xv * r * g_ref[...]).astype(BF16)

    return _call("rms_fwd", body, grid=(S // tm,),
                 in_specs=[pl.BlockSpec((tm, D_MODEL), lambda i: (i, 0)), pl.BlockSpec((1, D_MODEL), lambda i: (0, 0)),
                           pl.BlockSpec((8, LANES), lambda i: (0, 0))],
                 out_specs=pl.BlockSpec((tm, D_MODEL), lambda i: (i, 0)),
                 out_shape=jax.ShapeDtypeStruct((S, D_MODEL), BF16), sem=("parallel",))(x, g, token)


def _rms_bwd_math(xv, gv, dh):
    r = lax.rsqrt(jnp.mean(xv * xv, axis=-1, keepdims=True) + EPS)
    n = xv * r
    dn = dh * gv
    dx = r * (dn - n * jnp.mean(dn * n, axis=-1, keepdims=True))
    dg = jnp.sum(dh * n, axis=0, keepdims=True)
    return dx, dg


def rms_bwd(x, g, dh, dres, out_dtype):
    S = x.shape[0]
    tm = min(512, S)

    def body(x_ref, g_ref, dh_ref, dres_ref, dx_ref, dg_ref):
        dx, dg = _rms_bwd_math(x_ref[...], g_ref[...], dh_ref[...].astype(F32))
        dx_ref[...] = (dx + dres_ref[...].astype(F32)).astype(out_dtype)

        @pl.when(pl.program_id(0) == 0)
        def _():
            dg_ref[...] = dg

        @pl.when(pl.program_id(0) > 0)
        def _():
            dg_ref[...] += dg

    row = pl.BlockSpec((tm, D_MODEL), lambda i: (i, 0))
    vec = pl.BlockSpec((1, D_MODEL), lambda i: (0, 0))
    return _call("rms_bwd", body, grid=(S // tm,), in_specs=[row, vec, row, row], out_specs=[row, vec],
                 out_shape=[jax.ShapeDtypeStruct((S, D_MODEL), out_dtype), jax.ShapeDtypeStruct((1, D_MODEL), F32)],
                 sem=("arbitrary",))(x, g, dh, dres)


def final_loss(x, g, target):
    S = x.shape[0]
    tm = min(256, S)

    def body(x_ref, g_ref, t_ref, loss_ref, dxb_ref, dg_ref):
        xv = x_ref[...]
        gv = g_ref[...]
        r = lax.rsqrt(jnp.mean(xv * xv, axis=-1, keepdims=True) + EPS)
        err = xv * r * gv - t_ref[...]
        part = 0.5 * jnp.sum(jnp.mean(err * err, axis=-1, keepdims=True), axis=0, keepdims=True)
        dx, dg = _rms_bwd_math(xv, gv, err * (1.0 / D_MODEL))
        dxb_ref[...] = dx.astype(BF16)

        @pl.when(pl.program_id(0) == 0)
        def _():
            dg_ref[...] = dg
            loss_ref[...] = part

        @pl.when(pl.program_id(0) > 0)
        def _():
            dg_ref[...] += dg
            loss_ref[...] += part

    row = pl.BlockSpec((tm, D_MODEL), lambda i: (i, 0))
    vec = pl.BlockSpec((1, D_MODEL), lambda i: (0, 0))
    one = pl.BlockSpec((1, 1), lambda i: (0, 0))
    return _call("final_loss", body, grid=(S // tm,), in_specs=[row, vec, row], out_specs=[one, row, vec],
                 out_shape=[jax.ShapeDtypeStruct((1, 1), F32), jax.ShapeDtypeStruct((S, D_MODEL), BF16),
                            jax.ShapeDtypeStruct((1, D_MODEL), F32)],
                 sem=("arbitrary",))(x, g, target)


EPILOGUE_ROWS = 256
NN = (((1,), (0,)), ((), ()))
NT = (((1,), (1,)), ((), ()))
TN = (((0,), (0,)), ((), ()))


def _matmul(name, operands, in_specs, out_shape, out_specs, grid, pairs, dims, acc_shape, epilogue):
    n_in, n_out, nk = len(operands), len(out_shape), grid[-1]

    def body(*refs):
        ins, outs = refs[:n_in], refs[n_in:n_in + n_out]
        part = None
        for ia, ib in pairs:
            d = lax.dot_general(ins[ia][...], ins[ib][...], dims, preferred_element_type=F32)
            part = d if part is None else part + d
        if nk == 1:
            epilogue(part, ins, outs)
        else:
            acc = refs[-1]
            k = pl.program_id(len(grid) - 1)

            @pl.when(k == 0)
            def _():
                acc[...] = part

            @pl.when(k > 0)
            def _():
                acc[...] += part

            @pl.when(k == nk - 1)
            def _():
                epilogue(acc[...], ins, outs)

    scratch = [pltpu.VMEM(acc_shape, F32)] if nk > 1 else []
    sem = ("parallel",) * (len(grid) - 1) + ("arbitrary",)
    return _call(name, body, grid=grid, in_specs=in_specs, out_specs=out_specs, out_shape=out_shape,
                 scratch=scratch, sem=sem)(*operands)


def _store(dtype):
    def epilogue(acc, ins, outs):
        outs[0][...] = acc.astype(dtype)
    return epilogue


def mm_nn_cols(a, w, token):
    S, K = a.shape
    J, _, Ns = w.shape
    tm = min(512, S)
    return _matmul("mm_nn_cols", (a, w, token),
                   [pl.BlockSpec((tm, K), lambda j, i, k: (i, 0)), pl.BlockSpec((None, K, Ns), lambda j, i, k: (j, 0, 0)),
                    pl.BlockSpec((8, LANES), lambda j, i, k: (0, 0))],
                   [jax.ShapeDtypeStruct((S, J * Ns), BF16)], [pl.BlockSpec((tm, Ns), lambda j, i, k: (i, j))],
                   (J, S // tm, 1), [(0, 1)], NN, None, _store(BF16))[0]


def ffn_up(h, wg, wu, token):
    S, K = h.shape
    J, _, Ns = wg.shape
    tm = min(512, S)

    sub = min(EPILOGUE_ROWS, tm)

    def body(h_ref, wg_ref, wu_ref, token_ref, g_ref, u_ref, a_ref):
        for r in range(tm // sub):
            rows = slice(r * sub, (r + 1) * sub)
            hv = h_ref[rows, :]
            gv = jnp.dot(hv, wg_ref[...], preferred_element_type=F32)
            uv = jnp.dot(hv, wu_ref[...], preferred_element_type=F32)
            g_ref[rows, :] = gv.astype(BF16)
            u_ref[rows, :] = uv.astype(BF16)
            a_ref[rows, :] = (gv * _sigmoid(gv) * uv).astype(BF16)

    wspec = pl.BlockSpec((None, K, Ns), lambda j, i: (j, 0, 0))
    ospec = pl.BlockSpec((tm, Ns), lambda j, i: (i, j))
    oshape = jax.ShapeDtypeStruct((S, J * Ns), BF16)
    return _call("ffn_up", body, grid=(J, S // tm),
                 in_specs=[pl.BlockSpec((tm, K), lambda j, i: (i, 0)), wspec, wspec, pl.BlockSpec((8, LANES), lambda j, i: (0, 0))],
                 out_specs=[ospec, ospec, ospec], out_shape=[oshape, oshape, oshape],
                 sem=("parallel", "parallel"))(h, wg, wu, token)


def mm_nn_rows_res(a, w, res, gain):
    S, K = a.shape
    N = w.shape[1]
    tm = min(512, S)
    tk, tn = (K, N) if K <= 2048 else (K // 2, N // 2)
    n_n, n_k = N // tn, K // tk

    def body(a_ref, w_ref, res_ref, g_ref, x_ref, h_ref, *acc):
        n, k = pl.program_id(1), pl.program_id(2)

        def normed(xv):
            r = lax.rsqrt(jnp.mean(xv * xv, axis=-1, keepdims=True) + EPS)
            h_ref[...] = (xv * r * g_ref[...]).astype(BF16)

        def store_columns(total):
            if n_n == 1:
                xv = total + res_ref[...]
                x_ref[...] = xv
                normed(xv)
                return
            for c in range(n_n):
                @pl.when(n == c)
                def _(c=c):
                    cols = slice(c * tn, (c + 1) * tn)
                    x_ref[:, cols] = total + res_ref[:, cols]

            @pl.when(n == n_n - 1)
            def _():
                normed(x_ref[...])

        part = jnp.dot(a_ref[...], w_ref[...], preferred_element_type=F32)
        if n_k == 1:
            store_columns(part)
        else:
            @pl.when(k == 0)
            def _():
                acc[0][...] = part

            @pl.when(k > 0)
            def _():
                acc[0][...] += part

            @pl.when(k == n_k - 1)
            def _():
                store_columns(acc[0][...])

    row = pl.BlockSpec((tm, N), lambda i, n, k: (i, 0))
    return _call("mm_nn_rows_res", body, grid=(S // tm, n_n, n_k),
                 in_specs=[pl.BlockSpec((tm, tk), lambda i, n, k: (i, k)), pl.BlockSpec((tk, tn), lambda i, n, k: (k, n)), row,
                           pl.BlockSpec((1, N), lambda i, n, k: (0, 0))],
                 out_specs=[row, row], out_shape=[jax.ShapeDtypeStruct((S, N), F32), jax.ShapeDtypeStruct((S, N), BF16)],
                 scratch=[pltpu.VMEM((tm, tn), F32)] if n_k > 1 else [],
                 sem=("parallel", "arbitrary", "arbitrary"))(a, w, res, gain)


def mm_nt_cols(pairs_in, out_dtype, shards_per_step):
    dz0, w0 = pairs_in[0]
    S = dz0.shape[0]
    J, K, Ns = w0.shape
    tm = min(512, S)
    sps = shards_per_step
    operands, specs, pairs = [], [], []
    for dz, w in pairs_in:
        for s in range(sps):
            pairs.append((len(operands), len(operands) + 1))
            operands += [dz, w]
            specs += [pl.BlockSpec((tm, Ns), lambda i, j, s=s: (i, j * sps + s)),
                      pl.BlockSpec((None, K, Ns), lambda i, j, s=s: (j * sps + s, 0, 0))]
    return _matmul("mm_nt_cols%d" % len(pairs_in), tuple(operands), specs,
                   [jax.ShapeDtypeStruct((S, K), out_dtype)], [pl.BlockSpec((tm, K), lambda i, j: (i, 0))],
                   (S // tm, J // sps), pairs, NT, (tm, K), _store(out_dtype))[0]


def mm_nt_rows(dy, w, token):
    S, N = dy.shape
    K = w.shape[0]
    tm, tko = min(1024, S), 512
    return _matmul("mm_nt_rows", (dy, w, token),
                   [pl.BlockSpec((tm, N), lambda i, kk, z: (i, 0)), pl.BlockSpec((tko, N), lambda i, kk, z: (kk, 0)),
                    pl.BlockSpec((8, LANES), lambda i, kk, z: (0, 0))],
                   [jax.ShapeDtypeStruct((S, K), BF16)], [pl.BlockSpec((tm, tko), lambda i, kk, z: (i, kk))],
                   (S // tm, K // tko, 1), [(0, 1)], NT, None, _store(BF16))[0]


def ffn_down_bwd(dy, w, gate, up, token):
    S, N = dy.shape
    K = w.shape[0]
    tm, tko = min(1024, S), 512
    sub = min(EPILOGUE_ROWS, tm)

    def body(dy_ref, w_ref, g_ref, u_ref, token_ref, dg_ref, du_ref):
        for r in range(tm // sub):
            rows = slice(r * sub, (r + 1) * sub)
            dact = lax.dot_general(dy_ref[rows, :], w_ref[...], NT, preferred_element_type=F32)
            gv = g_ref[rows, :].astype(F32)
            uv = u_ref[rows, :].astype(F32)
            sg = _sigmoid(gv)
            dg_ref[rows, :] = (dact * uv * sg * (1.0 + gv * (1.0 - sg))).astype(BF16)
            du_ref[rows, :] = (dact * gv * sg).astype(BF16)

    tile = pl.BlockSpec((tm, tko), lambda i, kk: (i, kk))
    oshape = jax.ShapeDtypeStruct((S, K), BF16)
    return _call("ffn_down_bwd", body, grid=(S // tm, K // tko),
                 in_specs=[pl.BlockSpec((tm, N), lambda i, kk: (i, 0)), pl.BlockSpec((tko, N), lambda i, kk: (kk, 0)), tile, tile,
                           pl.BlockSpec((8, LANES), lambda i, kk: (0, 0))],
                 out_specs=[tile, tile], out_shape=[oshape, oshape], sem=("parallel", "parallel"))(dy, w, gate, up, token)


def mm_tn_cols(a, dz, J):
    S, M = a.shape
    Ns = dz.shape[1] // J
    tm, tk = 512, S
    return _matmul("mm_tn_cols", (a, dz),
                   [pl.BlockSpec((tk, tm), lambda j, m, k: (k, m)), pl.BlockSpec((tk, Ns), lambda j, m, k: (k, j))],
                   [jax.ShapeDtypeStruct((J, M, Ns), BF16)], [pl.BlockSpec((None, tm, Ns), lambda j, m, k: (j, m, 0))],
                   (J, M // tm, S // tk), [(0, 1)], TN, (tm, Ns), _store(BF16))[0]


def mm_tn_rows(a, dy):
    S, K = a.shape
    N = dy.shape[1]
    tm, tk = 512, min(2048, S)
    return _matmul("mm_tn_rows", (a, dy),
                   [pl.BlockSpec((tk, tm), lambda m, k: (k, m)), pl.BlockSpec((tk, N), lambda m, k: (k, 0))],
                   [jax.ShapeDtypeStruct((K, N), BF16)], [pl.BlockSpec((tm, N), lambda m, k: (m, 0))],
                   (K // tm, S // tk), [(0, 1)], TN, (tm, N), _store(BF16))[0]


def rope_tables(S):
    half = ROT_DIM // 2
    pos = jnp.arange(S, dtype=F32)
    inv = ROPE_THETA ** (-jnp.arange(0, ROT_DIM, 2, dtype=F32) / ROT_DIM)
    ang = pos[:, None] * inv[None, :]
    cos, sin = jnp.cos(ang), jnp.sin(ang)
    zeros = jnp.zeros((S, HEAD_DIM - ROT_DIM), F32)
    c = jnp.concatenate([cos, cos, jnp.ones((S, HEAD_DIM - ROT_DIM), F32)], axis=1)
    s_lo = jnp.concatenate([-sin, jnp.zeros((S, half), F32), zeros], axis=1)
    s_hi = jnp.concatenate([jnp.zeros((S, half), F32), sin, zeros], axis=1)
    return c, s_lo, s_hi


ROPE_ROWS = 512


def _rope(t, c, s_lo, s_hi):
    half = ROT_DIM // 2
    return t * c + pltpu.roll(t, HEAD_DIM - half, 1) * s_lo + pltpu.roll(t, half, 1) * s_hi


def _unrope(d, c, s_lo, s_hi):
    half = ROT_DIM // 2
    return d * c + pltpu.roll(d * s_lo, half, 1) + pltpu.roll(d * s_hi, HEAD_DIM - half, 1)


def rope_fwd(z, tabs):
    S = z.shape[0]
    T = min(ROPE_ROWS, S)

    def body(q_ref, kv_ref, c_ref, sl_ref, sh_ref, qr_ref, kp_ref, vp_ref):
        i = pl.program_id(0)

        @pl.when(i == 0)
        def _():
            zero = jnp.zeros((CHUNK, KV_WIDTH), BF16)
            kp_ref[0:CHUNK, :] = zero
            vp_ref[0:CHUNK, :] = zero
            kp_ref[S + CHUNK:S + 2 * CHUNK, :] = zero
            vp_ref[S + CHUNK:S + 2 * CHUNK, :] = zero

        c, sl, sh = c_ref[...], sl_ref[...], sh_ref[...]
        for h in range(N_Q_HEADS):
            cols = slice(h * HEAD_DIM, (h + 1) * HEAD_DIM)
            qr_ref[:, cols] = _rope(q_ref[:, cols].astype(F32), c, sl, sh).astype(BF16)
        rows = pl.ds(pl.multiple_of(CHUNK + i * T, CHUNK), T)
        for g in range(N_KV_HEADS):
            cols = slice(g * HEAD_DIM, (g + 1) * HEAD_DIM)
            kp_ref[rows, cols] = _rope(kv_ref[:, cols].astype(F32), c, sl, sh).astype(BF16)
        vp_ref[rows, :] = kv_ref[:, KV_WIDTH:2 * KV_WIDTH]

    tab = pl.BlockSpec((T, HEAD_DIM), lambda i: (i, 0))
    pad = pl.BlockSpec((S + 2 * CHUNK, KV_WIDTH), lambda i: (0, 0))
    return _call("rope_fwd", body, grid=(S // T,),
                 in_specs=[pl.BlockSpec((T, ATTN_WIDTH), lambda i: (i, 0)),
                           pl.BlockSpec((T, 2 * KV_WIDTH), lambda i: (i, OFF_K // (2 * KV_WIDTH))), tab, tab, tab],
                 out_specs=[pl.BlockSpec((T, ATTN_WIDTH), lambda i: (i, 0)), pad, pad],
                 out_shape=[jax.ShapeDtypeStruct((S, ATTN_WIDTH), BF16), jax.ShapeDtypeStruct((S + 2 * CHUNK, KV_WIDTH), BF16),
                            jax.ShapeDtypeStruct((S + 2 * CHUNK, KV_WIDTH), BF16)], sem=("arbitrary",))(z, z, *tabs)


def rope_bwd(dq, dkp, dvp, tabs):
    S = dq.shape[0]
    T = min(ROPE_ROWS, S)

    def body(dq_ref, dk_ref, dv_ref, c_ref, sl_ref, sh_ref, o_ref):
        c, sl, sh = c_ref[...], sl_ref[...], sh_ref[...]
        for h in range(N_Q_HEADS):
            cols = slice(h * HEAD_DIM, (h + 1) * HEAD_DIM)
            o_ref[:, cols] = _unrope(dq_ref[:, cols], c, sl, sh).astype(BF16)
        rows = pl.ds(pl.multiple_of(CHUNK + pl.program_id(0) * T, CHUNK), T)
        for g in range(N_KV_HEADS):
            cols = slice(g * HEAD_DIM, (g + 1) * HEAD_DIM)
            o_ref[:, OFF_K + g * HEAD_DIM:OFF_K + (g + 1) * HEAD_DIM] = _unrope(dk_ref[rows, cols], c, sl, sh).astype(BF16)
        o_ref[:, OFF_V:OFF_V + KV_WIDTH] = dv_ref[rows, :].astype(BF16)

    tab = pl.BlockSpec((T, HEAD_DIM), lambda i: (i, 0))
    pad = pl.BlockSpec((S + 2 * CHUNK, KV_WIDTH), lambda i: (0, 0))
    return _call("rope_bwd", body, grid=(S // T,),
                 in_specs=[pl.BlockSpec((T, ATTN_WIDTH), lambda i: (i, 0)), pad, pad, tab, tab, tab],
                 out_specs=pl.BlockSpec((T, OFF_CA), lambda i: (i, 0)),
                 out_shape=jax.ShapeDtypeStruct((S, IN_WIDTH), BF16), sem=("parallel",))(dq, dkp, dvp, *tabs)


STACK = Q_PER_KV * CHUNK


def _stack_heads(ref, rows):
    return jnp.concatenate([ref[rows, r * HEAD_DIM:(r + 1) * HEAD_DIM] for r in range(Q_PER_KV)], axis=0)


def _stack_sinks(s_ref):
    return jnp.concatenate([jnp.broadcast_to(s_ref[r:r + 1, 0:1], (CHUNK, 1)) for r in range(Q_PER_KV)], axis=0)


MASKED = -1e30


def _scores(q, kb):
    return lax.dot_general(q, kb, NT, preferred_element_type=F32) * (1.0 / math.sqrt(HEAD_DIM))


def _band_bias():
    row = lax.broadcasted_iota(jnp.int32, (STACK, 3 * CHUNK), 0) & (CHUNK - 1)
    col = lax.broadcasted_iota(jnp.int32, (STACK, 3 * CHUNK), 1)
    return jnp.where(jnp.abs(col - CHUNK - row) <= WINDOW, 0.0, MASKED).astype(F32)


def _edge_bias(n, S):
    kpos = (n - 1) * CHUNK + lax.broadcasted_iota(jnp.int32, (1, 3 * CHUNK), 1)
    return jnp.where((kpos >= 0) & (kpos < S), 0.0, MASKED).astype(F32)


def _softmax_sink(s, sk, bias):
    s = s + bias
    m = jnp.maximum(jnp.max(s, axis=1, keepdims=True), sk)
    e = jnp.exp(s - m)
    es = jnp.exp(sk - m)
    inv = 1.0 / (jnp.sum(e, axis=1, keepdims=True) + es)
    return e * inv, es * inv


def _block_views(i, nblk):
    ns = [i * nblk + b for b in range(nblk)]
    wins = [pl.ds(pl.multiple_of(n * CHUNK, CHUNK), 3 * CHUNK) for n in ns]
    rows = [slice(b * CHUNK, (b + 1) * CHUNK) for b in range(nblk)]
    return ns, wins, rows


def attn_fwd(qr, kp, vp, sink3):
    S = qr.shape[0]
    tq = min(2048, S)
    gw = Q_PER_KV * HEAD_DIM
    nblk = tq // CHUNK

    def body(q_ref, k_ref, v_ref, s_ref, o_ref):
        ns, wins, rows = _block_views(pl.program_id(1), nblk)
        sk = _stack_sinks(s_ref)
        band = _band_bias()
        scores = [_scores(_stack_heads(q_ref, rows[b]), k_ref[wins[b], :]) for b in range(nblk)]
        probs = [_softmax_sink(scores[b], sk, band + _edge_bias(ns[b], S))[0].astype(BF16) for b in range(nblk)]
        outs = [jnp.dot(probs[b], v_ref[wins[b], :], preferred_element_type=F32).astype(BF16) for b in range(nblk)]
        for b in range(nblk):
            for r in range(Q_PER_KV):
                o_ref[rows[b], r * HEAD_DIM:(r + 1) * HEAD_DIM] = outs[b][r * CHUNK:(r + 1) * CHUNK]

    kv = pl.BlockSpec((S + 2 * CHUNK, HEAD_DIM), lambda g, i: (0, g))
    return _call("attn_fwd", body, grid=(N_KV_HEADS, S // tq),
                 in_specs=[pl.BlockSpec((tq, gw), lambda g, i: (i, g)), kv, kv,
                           pl.BlockSpec((None, Q_PER_KV, LANES), lambda g, i: (g, 0, 0))],
                 out_specs=pl.BlockSpec((tq, gw), lambda g, i: (i, g)),
                 out_shape=jax.ShapeDtypeStruct((S, D_MODEL), BF16), sem=("parallel", "arbitrary"))(qr, kp, vp, sink3)


def attn_bwd(qr, kp, vp, sink3, dmix):
    S = qr.shape[0]
    tq = min(1024, S)
    gw = Q_PER_KV * HEAD_DIM
    scale = 1.0 / math.sqrt(HEAD_DIM)
    nblk = tq // CHUNK

    def body(q_ref, k_ref, v_ref, s_ref, do_ref, dq_ref, dk_ref, dv_ref, ds_ref):
        i = pl.program_id(1)

        @pl.when(i == 0)
        def _():
            dk_ref[...] = jnp.zeros_like(dk_ref)
            dv_ref[...] = jnp.zeros_like(dv_ref)
            ds_ref[...] = jnp.zeros_like(ds_ref)

        blocks = range(nblk)
        ns, wins, rows = _block_views(i, nblk)
        sk = _stack_sinks(s_ref)
        band = _band_bias()
        qs = [_stack_heads(q_ref, rows[b]) for b in blocks]
        dos = [_stack_heads(do_ref, rows[b]) for b in blocks]
        scores = [_scores(qs[b], k_ref[wins[b], :]) for b in blocks]
        dps = [lax.dot_general(dos[b], v_ref[wins[b], :], NT, preferred_element_type=F32) for b in blocks]
        probs = [_softmax_sink(scores[b], sk, band + _edge_bias(ns[b], S)) for b in blocks]
        deltas = [jnp.sum(probs[b][0] * dps[b], axis=1, keepdims=True) for b in blocks]
        dscs = [(probs[b][0] * (dps[b] - deltas[b]) * scale).astype(BF16) for b in blocks]
        dqs = [jnp.dot(dscs[b], k_ref[wins[b], :], preferred_element_type=F32) for b in blocks]
        dks = [lax.dot_general(dscs[b], qs[b], TN, preferred_element_type=F32) for b in blocks]
        dvs = [lax.dot_general(probs[b][0].astype(BF16), dos[b], TN, preferred_element_type=F32) for b in blocks]
        for b in blocks:
            for r in range(Q_PER_KV):
                dq_ref[rows[b], r * HEAD_DIM:(r + 1) * HEAD_DIM] = dqs[b][r * CHUNK:(r + 1) * CHUNK]
        for m in range(nblk + 2):
            parts = [(b, m - b) for b in blocks if 0 <= m - b <= 2]
            krows = pl.ds(pl.multiple_of(i * tq + m * CHUNK, CHUNK), CHUNK)
            dk_ref[krows, :] += sum(dks[b][o * CHUNK:(o + 1) * CHUNK] for b, o in parts)
            dv_ref[krows, :] += sum(dvs[b][o * CHUNK:(o + 1) * CHUNK] for b, o in parts)
        for r in range(Q_PER_KV):
            head = slice(r * CHUNK, (r + 1) * CHUNK)
            dsink = sum(jnp.sum(-probs[b][1][head] * deltas[b][head], axis=0, keepdims=True) for b in blocks)
            ds_ref[r:r + 1, :] += jnp.broadcast_to(dsink, (1, LANES))

    kv = pl.BlockSpec((S + 2 * CHUNK, HEAD_DIM), lambda g, i: (0, g))
    qspec = pl.BlockSpec((tq, gw), lambda g, i: (i, g))
    sspec = pl.BlockSpec((None, Q_PER_KV, LANES), lambda g, i: (g, 0, 0))
    padshape = jax.ShapeDtypeStruct((S + 2 * CHUNK, KV_WIDTH), F32)
    return _call("attn_bwd", body, grid=(N_KV_HEADS, S // tq),
                 in_specs=[qspec, kv, kv, sspec, qspec],
                 out_specs=[qspec, kv, kv, sspec],
                 out_shape=[jax.ShapeDtypeStruct((S, ATTN_WIDTH), F32), padshape, padshape,
                            jax.ShapeDtypeStruct((N_KV_HEADS, Q_PER_KV, LANES), F32)],
                 sem=("parallel", "arbitrary"))(qr, kp, vp, sink3, dmix)


CONV_TILE = 256


def _fill_padded(dst_ref, value, S):
    zero = jnp.zeros((CONV_PAD, LANES), F32)
    dst_ref[0:CONV_PAD, :] = zero
    dst_ref[CONV_PAD + S:2 * CONV_PAD + S, :] = zero
    dst_ref[CONV_PAD:CONV_PAD + S, :] = value


def conv_dw_fwd(z, w32, b):
    S = z.shape[0]
    T = min(CONV_TILE, S)
    lo = CONV_PAD - (CONV_KERNEL - 1) // 2

    def body(a_ref, g_ref, w_ref, b_ref, o_ref, c0_ref):
        _fill_padded(c0_ref, a_ref[...].astype(F32) * _sigmoid(g_ref[...].astype(F32)), S)

        def tile(t, carry):
            base = pl.multiple_of(t * T, T)
            acc = jnp.broadcast_to(b_ref[...], (T, LANES))
            for j in range(CONV_KERNEL):
                acc = acc + w_ref[j:j + 1, :] * c0_ref[pl.ds(base + lo + j, T), :]
            o_ref[pl.ds(base, T), :] = acc
            return carry

        lax.fori_loop(0, S // T, tile, 0)

    nca, ncg = OFF_CA // LANES, OFF_CG // LANES
    return _call("conv_dw_fwd", body, grid=(CONV_WIDTH // LANES,),
                 in_specs=[pl.BlockSpec((S, LANES), lambda cb: (0, nca + cb)), pl.BlockSpec((S, LANES), lambda cb: (0, ncg + cb)),
                           pl.BlockSpec((32, LANES), lambda cb: (0, cb)), pl.BlockSpec((1, LANES), lambda cb: (0, cb))],
                 out_specs=pl.BlockSpec((S, LANES), lambda cb: (0, cb)),
                 out_shape=jax.ShapeDtypeStruct((S, CONV_WIDTH), F32),
                 scratch=[pltpu.VMEM((S + 2 * CONV_PAD, LANES), F32)], sem=("parallel",))(z, z, w32, b)


def _ln_stats(x):
    mu = jnp.mean(x, axis=-1, keepdims=True)
    xc = x - mu
    rs = lax.rsqrt(jnp.mean(xc * xc, axis=-1, keepdims=True) + EPS)
    return xc * rs, rs


def _ln_bwd(dy, xh, rs, g):
    dxh = dy * g
    return rs * (dxh - jnp.mean(dxh, axis=-1, keepdims=True) - xh * jnp.mean(dxh * xh, axis=-1, keepdims=True))


def conv_ln_fwd(c1, g, b, mix):
    S = c1.shape[0]
    T = min(512, S)

    def body(x_ref, g_ref, b_ref, mix_ref, o_ref):
        xh, _ = _ln_stats(x_ref[...])
        y = xh * g_ref[...] + b_ref[...]
        o_ref[...] = (y * _sigmoid(y)).astype(BF16)

    row = pl.BlockSpec((T, CONV_WIDTH), lambda i: (i, 0))
    vec = pl.BlockSpec((1, CONV_WIDTH), lambda i: (0, 0))
    return _call("conv_ln_fwd", body, grid=(S // T,), in_specs=[row, vec, vec, pl.BlockSpec(memory_space=pl.ANY)],
                 out_specs=pl.BlockSpec((T, CONV_WIDTH), lambda i: (i, ATTN_WIDTH // CONV_WIDTH)),
                 out_shape=jax.ShapeDtypeStruct(mix.shape, BF16), sem=("parallel",), aliases={3: 0})(c1, g, b, mix)


def _acc_out(ref, value, step=None):
    step = pl.program_id(0) if step is None else step

    @pl.when(step == 0)
    def _():
        ref[...] = value

    @pl.when(step > 0)
    def _():
        ref[...] += value


def conv_ln_bwd(dmix, c1, g, b):
    S = c1.shape[0]
    T = min(512, S)

    def body(d_ref, x_ref, g_ref, b_ref, dx_ref, dg_ref, db_ref):
        xh, rs = _ln_stats(x_ref[...])
        gv = g_ref[...]
        y = xh * gv + b_ref[...]
        sg = _sigmoid(y)
        dy = d_ref[...].astype(F32) * sg * (1.0 + y * (1.0 - sg))
        dx_ref[...] = _ln_bwd(dy, xh, rs, gv)
        _acc_out(dg_ref, jnp.sum(dy * xh, axis=0, keepdims=True))
        _acc_out(db_ref, jnp.sum(dy, axis=0, keepdims=True))

    row = pl.BlockSpec((T, CONV_WIDTH), lambda i: (i, 0))
    vec = pl.BlockSpec((1, CONV_WIDTH), lambda i: (0, 0))
    vshape = jax.ShapeDtypeStruct((1, CONV_WIDTH), F32)
    return _call("conv_ln_bwd", body, grid=(S // T,),
                 in_specs=[pl.BlockSpec((T, CONV_WIDTH), lambda i: (i, ATTN_WIDTH // CONV_WIDTH)), row, vec, vec],
                 out_specs=[row, vec, vec], out_shape=[jax.ShapeDtypeStruct((S, CONV_WIDTH), F32), vshape, vshape],
                 sem=("arbitrary",))(dmix, c1, g, b)


def conv_dw_bwd(dc1, z, w32, dz):
    S = z.shape[0]
    T = min(CONV_TILE, S)
    half = (CONV_KERNEL - 1) // 2
    lo = CONV_PAD - half
    n_cb = CONV_WIDTH // LANES

    def body(d_ref, a_ref, g_ref, w_ref, dz_ref, o_ref, dw_ref, db_ref, c0_ref, d1_ref, wacc_ref, dg_ref):
        @pl.when(pl.program_id(1) == 0)
        def _():
            av = a_ref[...].astype(F32)
            sg = _sigmoid(g_ref[...].astype(F32))
            _fill_padded(c0_ref, av * sg, S)
            _fill_padded(d1_ref, d_ref[...], S)
            wacc_ref[...] = jnp.zeros_like(wacc_ref)

            def tile(t, carry):
                base = pl.multiple_of(t * T, T)
                d1 = d_ref[pl.ds(base, T), :]
                acc = jnp.zeros((T, LANES), F32)
                for j in range(CONV_KERNEL):
                    acc = acc + w_ref[j:j + 1, :] * d1_ref[pl.ds(base + CONV_PAD + half - j, T), :]
                    prod = d1 * c0_ref[pl.ds(base + lo + j, T), :]
                    wacc_ref[j] += jnp.sum(prod.reshape(T // 8, 8, LANES), axis=0)
                rows = pl.ds(base, T)
                a_t = a_ref[rows, :].astype(F32)
                s_t = _sigmoid(g_ref[rows, :].astype(F32))
                o_ref[rows, :] = (acc * s_t).astype(BF16)
                dg_ref[rows, :] = (acc * a_t * s_t * (1.0 - s_t)).astype(BF16)
                return carry

            lax.fori_loop(0, S // T, tile, 0)
            dw_ref[...] = jnp.sum(wacc_ref[...], axis=1)
            db_ref[...] = jnp.sum(d_ref[...], axis=0, keepdims=True)

        @pl.when(pl.program_id(1) == 1)
        def _():
            o_ref[...] = dg_ref[...]

    nca, ncg = OFF_CA // LANES, OFF_CG // LANES
    return _call("conv_dw_bwd", body, grid=(n_cb, 2),
                 in_specs=[pl.BlockSpec((S, LANES), lambda cb, j: (0, cb)), pl.BlockSpec((S, LANES), lambda cb, j: (0, nca + cb)),
                           pl.BlockSpec((S, LANES), lambda cb, j: (0, ncg + cb)), pl.BlockSpec((32, LANES), lambda cb, j: (0, cb)),
                           pl.BlockSpec(memory_space=pl.ANY)],
                 out_specs=[pl.BlockSpec((S, LANES), lambda cb, j: (0, nca + cb + n_cb * j)),
                            pl.BlockSpec((32, LANES), lambda cb, j: (0, cb)), pl.BlockSpec((1, LANES), lambda cb, j: (0, cb))],
                 out_shape=[jax.ShapeDtypeStruct(dz.shape, BF16), jax.ShapeDtypeStruct((32, CONV_WIDTH), F32),
                            jax.ShapeDtypeStruct((1, CONV_WIDTH), F32)],
                 scratch=[pltpu.VMEM((S + 2 * CONV_PAD, LANES), F32), pltpu.VMEM((S + 2 * CONV_PAD, LANES), F32),
                          pltpu.VMEM((32, 8, LANES), F32), pltpu.VMEM((S, LANES), BF16)],
                 sem=("parallel", "arbitrary"), aliases={4: 0})(dc1, z, z, w32, dz)


_INV_SQRT2 = 1.0 / math.sqrt(2.0)
_INV_SQRT2PI = 1.0 / math.sqrt(2.0 * math.pi)


def _gelu(x):
    return 0.5 * x * (1.0 + lax.erf(x * _INV_SQRT2))


def _gelu_grad(x):
    return 0.5 * (1.0 + lax.erf(x * _INV_SQRT2)) + x * jnp.exp(-0.5 * x * x) * _INV_SQRT2PI


def sgu_fwd(z, g, b, ws, bs, mix):
    S = z.shape[0]
    T = min(512, S)

    def body(u_ref, v_ref, g_ref, b_ref, ws_ref, bs_ref, mix_ref, o_ref):
        xh, _ = _ln_stats(_gelu(v_ref[...].astype(F32)))
        vn = (xh * g_ref[...] + b_ref[...]).astype(BF16)
        for ch in range(T // CHUNK):
            rows = slice(ch * CHUNK, (ch + 1) * CHUNK)
            for h in range(SGU_HEADS):
                cols = slice(h * HEAD_DIM, (h + 1) * HEAD_DIM)
                sp = jnp.dot(ws_ref[h], vn[rows, cols], preferred_element_type=F32) + bs_ref[h]
                o_ref[rows, cols] = (_gelu(u_ref[rows, cols].astype(F32)) * sp).astype(BF16)

    vec = pl.BlockSpec((1, SGU_WIDTH), lambda i: (0, 0))
    full = pl.BlockSpec((SGU_HEADS, CHUNK, CHUNK), lambda i: (0, 0, 0))
    return _call("sgu_fwd", body, grid=(S // T,),
                 in_specs=[pl.BlockSpec((T, SGU_WIDTH), lambda i: (i, OFF_U // SGU_WIDTH)),
                           pl.BlockSpec((T, SGU_WIDTH), lambda i: (i, OFF_VV // SGU_WIDTH)), vec, vec, full, full,
                           pl.BlockSpec(memory_space=pl.ANY)],
                 out_specs=pl.BlockSpec((T, SGU_WIDTH), lambda i: (i, (ATTN_WIDTH + CONV_WIDTH) // SGU_WIDTH)),
                 out_shape=jax.ShapeDtypeStruct(mix.shape, BF16), sem=("parallel",), aliases={6: 0})(z, z, g, b, ws, bs, mix)


def sgu_bwd(z, dmix, g, b, ws, bs, dz):
    S = z.shape[0]
    T = min(512, S)

    def body(u_ref, v_ref, d_ref, g_ref, b_ref, ws_ref, bs_ref, dz_ref, o_ref, dws_ref, dbs_ref, dg_ref, db_ref, dvn_ref, dv_ref):
        tile = pl.program_id(0)
        first = pl.program_id(1) == 0

        @pl.when(first & (tile == 0))
        def _():
            dws_ref[...] = jnp.zeros_like(dws_ref)
            dbs_ref[...] = jnp.zeros_like(dbs_ref)

        @pl.when(first)
        def _():
            vraw = v_ref[...].astype(F32)
            xh, rs = _ln_stats(_gelu(vraw))
            gv = g_ref[...]
            vn = (xh * gv + b_ref[...]).astype(BF16)
            for ch in range(T // CHUNK):
                rows = slice(ch * CHUNK, (ch + 1) * CHUNK)
                for h in range(SGU_HEADS):
                    cols = slice(h * HEAD_DIM, (h + 1) * HEAD_DIM)
                    w = ws_ref[h]
                    vb = vn[rows, cols]
                    sp = jnp.dot(w, vb, preferred_element_type=F32) + bs_ref[h]
                    uraw = u_ref[rows, cols].astype(F32)
                    dout = d_ref[rows, cols].astype(F32)
                    o_ref[rows, cols] = (dout * sp * _gelu_grad(uraw)).astype(BF16)
                    dsp = dout * _gelu(uraw)
                    dspb = dsp.astype(BF16)
                    dvn_ref[rows, cols] = lax.dot_general(w, dspb, TN, preferred_element_type=F32)
                    dws_ref[h] += lax.dot_general(dspb, vb, NT, preferred_element_type=F32)
                    dbs_ref[h] += jnp.sum(dsp, axis=1, keepdims=True)
            dvn = dvn_ref[...]
            dv_ref[...] = (_ln_bwd(dvn, xh, rs, gv) * _gelu_grad(vraw)).astype(BF16)
            _acc_out(dg_ref, jnp.sum(dvn * xh, axis=0, keepdims=True), tile)
            _acc_out(db_ref, jnp.sum(dvn, axis=0, keepdims=True), tile)

        @pl.when(pl.program_id(1) == 1)
        def _():
            o_ref[...] = dv_ref[...]

    vec = pl.BlockSpec((1, SGU_WIDTH), lambda i, j: (0, 0))
    full = pl.BlockSpec((SGU_HEADS, CHUNK, CHUNK), lambda i, j: (0, 0, 0))
    vshape = jax.ShapeDtypeStruct((1, SGU_WIDTH), F32)
    return _call("sgu_bwd", body, grid=(S // T, 2),
                 in_specs=[pl.BlockSpec((T, SGU_WIDTH), lambda i, j: (i, OFF_U // SGU_WIDTH)),
                           pl.BlockSpec((T, SGU_WIDTH), lambda i, j: (i, OFF_VV // SGU_WIDTH)),
                           pl.BlockSpec((T, SGU_WIDTH), lambda i, j: (i, (ATTN_WIDTH + CONV_WIDTH) // SGU_WIDTH)), vec, vec, full, full,
                           pl.BlockSpec(memory_space=pl.ANY)],
                 out_specs=[pl.BlockSpec((T, SGU_WIDTH), lambda i, j: (i, OFF_U // SGU_WIDTH + j)), full,
                            pl.BlockSpec((SGU_HEADS, CHUNK, 1), lambda i, j: (0, 0, 0)), vec, vec],
                 out_shape=[jax.ShapeDtypeStruct(dz.shape, BF16), jax.ShapeDtypeStruct((SGU_HEADS, CHUNK, CHUNK), F32),
                            jax.ShapeDtypeStruct((SGU_HEADS, CHUNK, 1), F32), vshape, vshape],
                 scratch=[pltpu.VMEM((T, SGU_WIDTH), F32), pltpu.VMEM((T, SGU_WIDTH), BF16)],
                 sem=("arbitrary", "arbitrary"), aliases={7: 0})(z, z, dmix, g, b, ws, bs, dz)


def _row_tile(rows, cols, n_arrays, budget_mib=24):
    budget = (budget_mib * 1024 * 1024) // (n_arrays * 2 * 4 * cols)
    t = min(rows, max(16, budget // 16 * 16))
    while rows % t:
        t -= 16
    return t


def add_sibling_half(grad, recv, c_idx):
    J, R, C = grad.shape
    hr = R // 2
    tr = _row_tile(hr, C, 3)
    nb = hr // tr

    def body(c_ref, g_ref, r_ref, o_ref):
        o_ref[...] = (g_ref[...].astype(F32) + r_ref[...].astype(F32)).astype(BF16)

    grid_spec = pltpu.PrefetchScalarGridSpec(
        num_scalar_prefetch=1, grid=(J, nb),
        in_specs=[pl.BlockSpec((None, tr, C), lambda j, i, c: (j, c[0] * nb + i, 0)),
                  pl.BlockSpec((None, tr, C), lambda j, i, c: (j, i, 0))],
        out_specs=pl.BlockSpec((None, tr, C), lambda j, i, c: (j, i, 0)))
    return pl.pallas_call(body, name="add_sibling_half", grid_spec=grid_spec,
                          out_shape=jax.ShapeDtypeStruct((J, hr, C), BF16),
                          compiler_params=pltpu.CompilerParams(vmem_limit_bytes=VMEM_LIMIT,
                                                               dimension_semantics=("parallel", "parallel")))(c_idx, grad, recv)


def sum_chips(own, others, stack, x_idx, y_idx, layer):
    R, C = own.shape[1:]
    tr = _row_tile(R, C, 4)

    def body(x_ref, y_ref, own_ref, oth_ref, stack_ref, o_ref):
        acc = own_ref[...].astype(F32)
        for j in range(3):
            acc = acc + oth_ref[j].astype(F32)
        o_ref[...] = acc

    grid_spec = pltpu.PrefetchScalarGridSpec(
        num_scalar_prefetch=2, grid=(R // tr,),
        in_specs=[pl.BlockSpec((None, tr, C), lambda i, xr, yr: (2 * xr[0] + yr[0], i, 0)),
                  pl.BlockSpec((3, tr, C), lambda i, xr, yr: (0, i, 0)),
                  pl.BlockSpec(memory_space=pl.ANY)],
        out_specs=pl.BlockSpec((None, tr, C), lambda i, xr, yr: (layer, i, 0)))
    return pl.pallas_call(body, name="sum_chips", grid_spec=grid_spec,
                          out_shape=jax.ShapeDtypeStruct(stack.shape, F32), input_output_aliases={4: 0},
                          compiler_params=pltpu.CompilerParams(vmem_limit_bytes=VMEM_LIMIT,
                                                               dimension_semantics=("parallel",)))(x_idx, y_idx, own, others, stack)


def adamw_halves(w, mine, theirs, m, v, c_idx):
    L, R, C = w.shape
    hr = R // 2
    tr = _row_tile(hr, C, 9, budget_mib=40)
    nb = hr // tr

    def body(c_ref, w_ref, a_ref, b_ref, m_ref, v_ref, g_ref, d_ref, nm_ref, nv_ref):
        gv = jnp.where(pl.program_id(1) == c_ref[0], a_ref[...], b_ref[...])
        g_ref[...] = gv
        nm = ADAM_B1 * m_ref[...] + (1.0 - ADAM_B1) * gv
        nv = ADAM_B2 * v_ref[...] + (1.0 - ADAM_B2) * (gv * gv)
        m_hat = nm / (1.0 - ADAM_B1 ** ADAM_STEP)
        v_hat = nv / (1.0 - ADAM_B2 ** ADAM_STEP)
        d_ref[...] = -ADAM_LR * (m_hat / (jnp.sqrt(v_hat) + ADAM_EPS) + ADAM_WD * w_ref[...])
        nm_ref[...] = nm
        nv_ref[...] = nv

    full = pl.BlockSpec((None, tr, C), lambda l, h, i, c: (l, h * nb + i, 0))
    a_spec = pl.BlockSpec((None, tr, C), lambda l, h, i, c: (l, jnp.where(h == c[0], i, 0), 0))
    b_spec = pl.BlockSpec((None, tr, C), lambda l, h, i, c: (l, jnp.where(h == c[0], 0, i), 0))
    grid_spec = pltpu.PrefetchScalarGridSpec(num_scalar_prefetch=1, grid=(L, 2, nb),
                                             in_specs=[full, a_spec, b_spec, full, full], out_specs=[full] * 4)
    shape = jax.ShapeDtypeStruct((L, R, C), F32)
    return pl.pallas_call(body, name="adamw_halves", grid_spec=grid_spec, out_shape=[shape] * 4,
                          compiler_params=pltpu.CompilerParams(vmem_limit_bytes=VMEM_LIMIT,
                                                               dimension_semantics=("parallel", "arbitrary", "arbitrary")))(
        c_idx, w, mine, theirs, m, v)


def adamw(w, g, m, v):
    R, C = w.shape
    tr = _row_tile(R, C, 7)

    def body(w_ref, g_ref, m_ref, v_ref, d_ref, nm_ref, nv_ref):
        gv = g_ref[...]
        nm = ADAM_B1 * m_ref[...] + (1.0 - ADAM_B1) * gv
        nv = ADAM_B2 * v_ref[...] + (1.0 - ADAM_B2) * (gv * gv)
        m_hat = nm / (1.0 - ADAM_B1 ** ADAM_STEP)
        v_hat = nv / (1.0 - ADAM_B2 ** ADAM_STEP)
        d_ref[...] = -ADAM_LR * (m_hat / (jnp.sqrt(v_hat) + ADAM_EPS) + ADAM_WD * w_ref[...])
        nm_ref[...] = nm
        nv_ref[...] = nv

    spec = pl.BlockSpec((tr, C), lambda i: (i, 0))
    shape = jax.ShapeDtypeStruct((R, C), F32)
    return _call("adamw", body, grid=(R // tr,), in_specs=[spec] * 4, out_specs=[spec] * 3, out_shape=[shape] * 3,
                 sem=("parallel",))(w, g, m, v)


def _place():
    x, y, c = lax.axis_index("x"), lax.axis_index("y"), lax.axis_index("c")
    chips = [(1 - x, y), (x, 1 - y), (1 - x, 1 - y)]
    return x, y, c, chips


def _remote(src, dst, send_sem, recv_sem, dev):
    return pltpu.make_async_remote_copy(src_ref=src, dst_ref=dst, send_sem=send_sem, recv_sem=recv_sem,
                                        device_id=dev, device_id_type=MESH)


EFFECT = pltpu.SideEffectType.DATAFLOW_SIDE_EFFECTING
SEM = pl.BlockSpec(memory_space=pltpu.SEMAPHORE)
ANY = pl.BlockSpec(memory_space=pl.ANY)
TOKEN = jax.ShapeDtypeStruct((8, LANES), F32)


def _in_hbm(a):
    return pltpu.with_memory_space_constraint(a, pltpu.HBM)


def _gather_copies(shards, lands, send_sems, recv_sems):
    x, y, c, chips = _place()
    me = 2 * x + y
    copies = []
    for k in range(len(shards)):
        hr = shards[k].shape[0] // 2
        mine = pl.ds(pl.multiple_of(c * hr, 8), hr)
        for t, (px, py) in enumerate(chips):
            copies.append(_remote(shards[k].at[mine, :], lands[k].at[me, mine, :], send_sems.at[4 * k + t], recv_sems.at[4 * k + t],
                                  (px, py, c)))
        copies.append(_remote(shards[k], lands[k].at[me], send_sems.at[4 * k + 3], recv_sems.at[4 * k + 3], (x, y, 1 - c)))
    return copies


def _gather_landings(lands, send_sems, recv_sems):
    x, y, c, chips = _place()
    me = 2 * x + y
    landings = []
    for k in range(len(lands)):
        hr = lands[k].shape[1] // 2
        mine = pl.ds(pl.multiple_of(c * hr, 8), hr)
        for t, (px, py) in enumerate(chips):
            dst = lands[k].at[2 * px + py, mine, :]
            landings.append(_remote(dst, dst, send_sems.at[4 * k + t], recv_sems.at[4 * k + t], (px, py, c)))
        dst = lands[k].at[me]
        landings.append(_remote(dst, dst, send_sems.at[4 * k + 3], recv_sems.at[4 * k + 3], (x, y, 1 - c)))
    return landings


def gather_start(shards, after):
    n = len(shards)

    def body(*refs):
        srcs, lands_in = refs[:n], refs[n:2 * n]
        send_sems, recv_sems = refs[2 * n + 1], refs[2 * n + 2]
        token = refs[-1]
        for cp in _gather_copies(srcs, lands_in, send_sems, recv_sems):
            cp.start()
        token[...] = jnp.zeros_like(token)

    lands = [lax.empty((N_CHIPS,) + s.shape, s.dtype) for s in shards]
    outs = pl.pallas_call(
        body, name="gather_start", in_specs=[HBM] * (2 * n) + [ANY],
        out_specs=[SEM, SEM] + [HBM] * (2 * n) + [VMEM_SPEC],
        out_shape=[pltpu.SemaphoreType.DMA((4 * n,)), pltpu.SemaphoreType.DMA((4 * n,))]
        + [pltpu.HBM(s.shape, s.dtype) for s in shards] + [pltpu.HBM(l.shape, l.dtype) for l in lands] + [TOKEN],
        input_output_aliases={i: 2 + i for i in range(2 * n)},
        compiler_params=pltpu.CompilerParams(has_side_effects=EFFECT),
    )(*[_in_hbm(s) for s in shards], *[_in_hbm(l) for l in lands], after)
    return outs[0], outs[1], outs[2:2 + n], outs[2 + n:2 + 2 * n], outs[-1]


def gather_wait(send_sems, recv_sems, shards, lands, after):
    n = len(shards)

    def body(*refs):
        srcs, lands_in = refs[:n], refs[n:2 * n]
        send, recv = refs[2 * n], refs[2 * n + 1]
        for cp in _gather_copies(srcs, lands_in, send, recv):
            cp.wait_send()
        for cp in _gather_landings(lands_in, send, recv):
            cp.wait_recv()

    outs = pl.pallas_call(
        body, name="gather_wait", in_specs=[HBM] * (2 * n) + [SEM, SEM, ANY], out_specs=[HBM] * (2 * n),
        out_shape=[pltpu.HBM(s.shape, s.dtype) for s in shards] + [pltpu.HBM(l.shape, l.dtype) for l in lands],
        input_output_aliases={i: i for i in range(2 * n)},
        compiler_params=pltpu.CompilerParams(has_side_effects=EFFECT),
    )(*shards, *lands, send_sems, recv_sems, after)
    return outs[n:]


def _forward_copies(lands, send_sems, recv_sems, received):
    x, y, c, chips = _place()
    copies = []
    for k in range(len(lands)):
        hr = lands[k].shape[1] // 2
        half = (1 - c) if received else c
        rows = pl.ds(pl.multiple_of(half * hr, 8), hr)
        for t, (px, py) in enumerate(chips):
            block = lands[k].at[2 * px + py, rows, :]
            copies.append(_remote(block, block, send_sems.at[3 * k + t], recv_sems.at[3 * k + t], (x, y, 1 - c)))
    return copies


def forward_start(lands):
    n = len(lands)

    def body(*refs):
        for cp in _forward_copies(refs[:n], refs[n], refs[n + 1], received=False):
            cp.start()
        refs[-1][...] = jnp.zeros_like(refs[-1])

    outs = pl.pallas_call(
        body, name="forward_start", in_specs=[HBM] * n, out_specs=[SEM, SEM] + [HBM] * n + [VMEM_SPEC],
        out_shape=[pltpu.SemaphoreType.DMA((3 * n,)), pltpu.SemaphoreType.DMA((3 * n,))]
        + [pltpu.HBM(l.shape, l.dtype) for l in lands] + [TOKEN],
        input_output_aliases={i: 2 + i for i in range(n)},
        compiler_params=pltpu.CompilerParams(has_side_effects=EFFECT),
    )(*[_in_hbm(l) for l in lands])
    return outs[0], outs[1], outs[2:2 + n], outs[-1]


def forward_wait(send_sems, recv_sems, lands, after):
    n = len(lands)

    def body(*refs):
        for cp in _forward_copies(refs[:n], refs[n], refs[n + 1], received=False):
            cp.wait_send()
        for cp in _forward_copies(refs[:n], refs[n], refs[n + 1], received=True):
            cp.wait_recv()

    return pl.pallas_call(
        body, name="forward_wait", in_specs=[HBM] * n + [SEM, SEM, ANY], out_specs=[HBM] * n,
        out_shape=[pltpu.HBM(l.shape, l.dtype) for l in lands],
        input_output_aliases={i: i for i in range(n)},
        compiler_params=pltpu.CompilerParams(has_side_effects=EFFECT),
    )(*lands, send_sems, recv_sems, after)


def forward_halves(lands):
    n = len(lands)

    def body(*refs):
        ins, outs = refs[:n], refs[n:2 * n]
        send_sems, recv_sems = refs[2 * n:]
        x, y, c, chips = _place()
        sibling = (x, y, 1 - c)
        sends = []
        for k in range(n):
            hr = ins[k].shape[1] // 2
            mine = pl.ds(pl.multiple_of(c * hr, 8), hr)
            for t, (px, py) in enumerate(chips):
                cp = _remote(ins[k].at[2 * px + py, mine, :], outs[k].at[2 * px + py, mine, :],
                             send_sems.at[k, t], recv_sems.at[k, t], sibling)
                cp.start()
                sends.append(cp)
        for k in range(n):
            hr = ins[k].shape[1] // 2
            other = pl.ds(pl.multiple_of((1 - c) * hr, 8), hr)
            for t, (px, py) in enumerate(chips):
                dst = outs[k].at[2 * px + py, other, :]
                _remote(dst, dst, send_sems.at[k, t], recv_sems.at[k, t], sibling).wait_recv()
        for cp in sends:
            cp.wait_send()

    return pl.pallas_call(
        body, name="forward_halves", in_specs=[HBM] * n, out_specs=[HBM] * n,
        out_shape=[jax.ShapeDtypeStruct(l.shape, l.dtype) for l in lands],
        input_output_aliases={i: i for i in range(n)},
        scratch_shapes=[pltpu.SemaphoreType.DMA((n, 3)), pltpu.SemaphoreType.DMA((n, 3))],
    )(*lands)


def gather_small(block):
    def body(in_ref, out_ref, send_sems, recv_sems):
        x, y, c, chips = _place()
        me = 2 * x + y
        out_ref[me] = in_ref[...]
        sends = []
        for t, (px, py) in enumerate(chips):
            cp = _remote(in_ref, out_ref.at[me], send_sems.at[t], recv_sems.at[t], (px, py, c))
            cp.start()
            sends.append(cp)
        for t, (px, py) in enumerate(chips):
            landed = out_ref.at[2 * px + py]
            _remote(landed, landed, send_sems.at[t], recv_sems.at[t], (px, py, c)).wait_recv()
        for cp in sends:
            cp.wait_send()

    return pl.pallas_call(
        body, name="gather_small", in_specs=[VMEM_SPEC], out_specs=VMEM_SPEC,
        out_shape=jax.ShapeDtypeStruct((N_CHIPS,) + block.shape, block.dtype),
        scratch_shapes=[pltpu.SemaphoreType.DMA((3,)), pltpu.SemaphoreType.DMA((3,))],
    )(block)


def exchange_sibling_halves(grads):
    n = len(grads)

    def body(*refs):
        ins, outs = refs[:n], refs[n:2 * n]
        send_sems, recv_sems = refs[2 * n:]
        x, y, c, _ = _place()
        copies = []
        for k in range(n):
            hr = ins[k].shape[1] // 2
            theirs = pl.ds(pl.multiple_of((1 - c) * hr, 8), hr)
            cp = _remote(ins[k].at[:, theirs, :], outs[k], send_sems.at[k], recv_sems.at[k], (x, y, 1 - c))
            cp.start()
            copies.append(cp)
        for cp in copies:
            cp.wait()

    return pl.pallas_call(
        body, name="exchange_sibling_halves", in_specs=[HBM] * n, out_specs=[HBM] * n,
        out_shape=[jax.ShapeDtypeStruct((g.shape[0], g.shape[1] // 2, g.shape[2]), g.dtype) for g in grads],
        scratch_shapes=[pltpu.SemaphoreType.DMA((n,)), pltpu.SemaphoreType.DMA((n,))],
    )(*grads)


def _sibling_half_copies(grads, lands, send_sems, recv_sems):
    x, y, c, _ = _place()
    copies = []
    for k in range(len(grads)):
        hr = grads[k].shape[1] // 2
        theirs = pl.ds(pl.multiple_of((1 - c) * hr, 8), hr)
        copies.append(_remote(grads[k].at[:, theirs, :], lands[k], send_sems.at[k], recv_sems.at[k], (x, y, 1 - c)))
    return copies


def _sibling_whole_copies(srcs, lands, send_sems, recv_sems):
    x, y, c, _ = _place()
    return [_remote(srcs[k], lands[k], send_sems.at[k], recv_sems.at[k], (x, y, 1 - c)) for k in range(len(srcs))]


def pair_start(name, make_copies, srcs, land_shapes):
    n = len(srcs)

    def body(*refs):
        src_refs, land_refs = refs[:n], refs[n:2 * n]
        send_sems, recv_sems = refs[2 * n], refs[2 * n + 1]
        token = refs[-1]
        for cp in make_copies(src_refs, land_refs, send_sems, recv_sems):
            cp.start()
        token[...] = jnp.zeros_like(token)

    lands = [lax.empty(shape, s.dtype) for shape, s in zip(land_shapes, srcs)]
    outs = pl.pallas_call(
        body, name=name, in_specs=[HBM] * (2 * n), out_specs=[SEM, SEM] + [HBM] * (2 * n) + [VMEM_SPEC],
        out_shape=[pltpu.SemaphoreType.DMA((n,)), pltpu.SemaphoreType.DMA((n,))]
        + [pltpu.HBM(s.shape, s.dtype) for s in srcs] + [pltpu.HBM(l.shape, l.dtype) for l in lands] + [TOKEN],
        input_output_aliases={i: 2 + i for i in range(2 * n)},
        compiler_params=pltpu.CompilerParams(has_side_effects=EFFECT),
    )(*[_in_hbm(s) for s in srcs], *[_in_hbm(l) for l in lands])
    return outs[0], outs[1], outs[2:2 + n], outs[2 + n:2 + 2 * n], outs[-1]


def pair_wait_one(name, send_sems, recv_sems, src, land, after, index):
    def body(src_ref, land_ref, send, recv, after_ref, src_out, land_out):
        x, y, c, _ = _place()
        cp = _remote(src_ref, land_ref, send.at[index], recv.at[index], (x, y, 1 - c))
        cp.wait_send()
        cp.wait_recv()

    return pl.pallas_call(
        body, name=name, in_specs=[HBM, HBM, SEM, SEM, ANY], out_specs=[HBM, HBM],
        out_shape=[pltpu.HBM(src.shape, src.dtype), pltpu.HBM(land.shape, land.dtype)],
        input_output_aliases={0: 0, 1: 1},
        compiler_params=pltpu.CompilerParams(has_side_effects=EFFECT),
    )(src, land, send_sems, recv_sems, after)


def pair_wait(name, make_copies, send_sems, recv_sems, srcs, lands, after):
    n = len(srcs)

    def body(*refs):
        src_refs, land_refs = refs[:n], refs[n:2 * n]
        for cp in make_copies(src_refs, land_refs, refs[2 * n], refs[2 * n + 1]):
            cp.wait_send()
            cp.wait_recv()

    outs = pl.pallas_call(
        body, name=name, in_specs=[HBM] * (2 * n) + [SEM, SEM, ANY], out_specs=[HBM] * (2 * n),
        out_shape=[pltpu.HBM(s.shape, s.dtype) for s in srcs] + [pltpu.HBM(l.shape, l.dtype) for l in lands],
        input_output_aliases={i: i for i in range(2 * n)},
        compiler_params=pltpu.CompilerParams(has_side_effects=EFFECT),
    )(*srcs, *lands, send_sems, recv_sems, after)
    return outs[:n], outs[n:]


def _chip_copies(parts, lands, send_sems, recv_sems):
    x, y, c, chips = _place()
    return [_remote(parts[k].at[2 * px + py], lands[k].at[t], send_sems.at[3 * k + t], recv_sems.at[3 * k + t], (px, py, c))
            for k in range(len(parts)) for t, (px, py) in enumerate(chips)]


def chip_parts_start(parts):
    n = len(parts)

    def body(*refs):
        srcs, lands_in = refs[:n], refs[n:2 * n]
        send_sems, recv_sems = refs[2 * n], refs[2 * n + 1]
        token = refs[-1]
        for cp in _chip_copies(srcs, lands_in, send_sems, recv_sems):
            cp.start()
        token[...] = jnp.zeros_like(token)

    lands = [lax.empty((3,) + p.shape[1:], p.dtype) for p in parts]
    outs = pl.pallas_call(
        body, name="chip_parts_start", in_specs=[HBM] * (2 * n), out_specs=[SEM, SEM] + [HBM] * (2 * n) + [VMEM_SPEC],
        out_shape=[pltpu.SemaphoreType.DMA((3 * n,)), pltpu.SemaphoreType.DMA((3 * n,))]
        + [pltpu.HBM(p.shape, p.dtype) for p in parts] + [pltpu.HBM(l.shape, l.dtype) for l in lands] + [TOKEN],
        input_output_aliases={i: 2 + i for i in range(2 * n)},
        compiler_params=pltpu.CompilerParams(has_side_effects=EFFECT),
    )(*[_in_hbm(p) for p in parts], *[_in_hbm(l) for l in lands])
    return outs[0], outs[1], outs[2:2 + n], outs[2 + n:2 + 2 * n], outs[-1]


def chip_parts_wait(send_sems, recv_sems, parts, lands, after):
    n = len(parts)

    def body(*refs):
        srcs, lands_in = refs[:n], refs[n:2 * n]
        send, recv = refs[2 * n], refs[2 * n + 1]
        for cp in _chip_copies(srcs, lands_in, send, recv):
            cp.wait_send()
            cp.wait_recv()

    outs = pl.pallas_call(
        body, name="chip_parts_wait", in_specs=[HBM] * (2 * n) + [SEM, SEM, ANY], out_specs=[HBM] * (2 * n),
        out_shape=[pltpu.HBM(p.shape, p.dtype) for p in parts] + [pltpu.HBM(l.shape, l.dtype) for l in lands],
        input_output_aliases={i: i for i in range(2 * n)},
        compiler_params=pltpu.CompilerParams(has_side_effects=EFFECT),
    )(*parts, *lands, send_sems, recv_sems, after)
    return outs[:n], outs[n:]


def allreduce_small(packed):
    R = packed.shape[0]

    def body(x_ref, sum_ref, all_ref, send_sems, recv_sems):
        x, y, c, chips = _place()
        me, sibling = (x, y, c), (x, y, 1 - c)

        def rows(px, py, pc):
            return all_ref.at[4 * px + 2 * py + pc]

        def copy(k, block, to, src=None):
            return _remote(rows(*block) if src is None else src, rows(*block), send_sems.at[k], recv_sems.at[k], to)

        all_ref[4 * x + 2 * y + c] = x_ref[...]
        first = [copy(0, me, sibling, src=x_ref)]
        first += [copy(1 + j, me, (*chip, c), src=x_ref) for j, chip in enumerate(chips)]
        for cp in first:
            cp.start()
        passed = [copy(4 + j, (*chip, c), sibling) for j, chip in enumerate(chips)]
        for j, chip in enumerate(chips):
            copy(1 + j, (*chip, c), me).wait_recv()
            passed[j].start()
        copy(0, sibling, me).wait_recv()
        for j, chip in enumerate(chips):
            copy(4 + j, (*chip, 1 - c), me).wait_recv()
        for cp in first + passed:
            cp.wait_send()

        def chunk(i, carry):
            rws = pl.ds(pl.multiple_of(i * PACK_ROWS, PACK_ROWS), PACK_ROWS)
            acc = all_ref[0, rws, :]
            for d in range(1, N_DEV):
                acc = acc + all_ref[d, rws, :]
            sum_ref[rws, :] = acc
            return carry

        lax.fori_loop(0, R // PACK_ROWS, chunk, 0)

    return pl.pallas_call(
        body, name="allreduce_small", in_specs=[VMEM_SPEC], out_specs=VMEM_SPEC,
        out_shape=jax.ShapeDtypeStruct((R, LANES), F32),
        scratch_shapes=[pltpu.VMEM((N_DEV, R, LANES), F32), pltpu.SemaphoreType.DMA((7,)), pltpu.SemaphoreType.DMA((7,))],
        compiler_params=pltpu.CompilerParams(vmem_limit_bytes=VMEM_LIMIT),
    )(packed)


def _mixer_fwd(x, h, p, tabs, token, late_weights=None):
    z = mm_nn_cols(h, p["w_in"], token)
    qr, kp, vp = rope_fwd(z, tabs)
    mix = attn_fwd(qr, kp, vp, p["sink3"])
    c1 = conv_dw_fwd(z, p["conv_w32"], p["conv_dw_b"])
    mix = conv_ln_fwd(c1, p["conv_ln_g"], p["conv_ln_b"], mix)
    mix = sgu_fwd(z, p["sgu_ln_g"], p["sgu_ln_b"], p["sgu_w16"], p["sgu_b3"], mix)
    if late_weights is not None:
        p.update(late_weights(mix))
    x_mid, h2 = mm_nn_rows_res(mix, p["w_out"], x, p["ffn_norm_g"])
    return x_mid, h2, dict(x=x, h=h, z=z, qr=qr, kp=kp, vp=vp, c1=c1, mix=mix, x_mid=x_mid)


def _ffn_fwd(x_mid, h2, p, next_gain, token):
    gate, up, act = ffn_up(h2, p["w_gate"], p["w_up"], token)
    x_out, h_next = mm_nn_rows_res(act, p["w_down"], x_mid, next_gain)
    return x_out, h_next, dict(h2=h2, gate=gate, up=up, act=act)


def _layer_fwd(x, h, p, next_gain, tabs, token):
    x_mid, h2, s_mix = _mixer_fwd(x, h, p, tabs, token)
    x_out, h_next, s_ffn = _ffn_fwd(x_mid, h2, p, next_gain, token)
    return x_out, h_next, {**s_mix, **s_ffn}


def _ffn_bwd(dxb, p, s, token):
    dgate, dup = ffn_down_bwd(dxb, p["w_down"], s["gate"], s["up"], token)
    g_down = mm_tn_rows(s["act"], dxb)
    dh2 = mm_nt_cols([(dgate, p["w_gate"]), (dup, p["w_up"])], BF16, 1)
    g_gate = mm_tn_cols(s["h2"], dgate, N_CHIPS)
    g_up = mm_tn_cols(s["h2"], dup, N_CHIPS)
    dmidb, g_ffn_norm = rms_bwd(s["x_mid"], p["ffn_norm_g"], dh2, dxb, BF16)
    return dmidb, [g_gate, g_up, g_down.reshape(N_CHIPS, -1, D_MODEL)], g_ffn_norm


def _mixer_bwd(dmidb, p, s, tabs, token, out_dtype):
    dmix = mm_nt_rows(dmidb, p["w_out"], token)
    g_out = mm_tn_rows(s["mix"], dmidb)
    dq, dkp, dvp, dsink = attn_bwd(s["qr"], s["kp"], s["vp"], p["sink3"], dmix)
    dz = rope_bwd(dq, dkp, dvp, tabs)
    dc1, g_cln_g, g_cln_b = conv_ln_bwd(dmix, s["c1"], p["conv_ln_g"], p["conv_ln_b"])
    dz, g_cw, g_cb = conv_dw_bwd(dc1, s["z"], p["conv_w32"], dz)
    dz, g_sw, g_sb, g_sln_g, g_sln_b = sgu_bwd(s["z"], dmix, p["sgu_ln_g"], p["sgu_ln_b"], p["sgu_w16"], p["sgu_b3"], dz)
    dh = mm_nt_cols([(dz, p["w_in"])], BF16, N_CHIPS)
    g_in = mm_tn_cols(s["h"], dz, N_CHIPS)
    dx_in, g_mix_norm = rms_bwd(s["x"], p["mix_norm_g"], dh, dmidb, out_dtype)
    small = dict(mix_norm_g=g_mix_norm, sink=dsink[:, :, 0].reshape(1, N_Q_HEADS), conv_dw_w=g_cw[:CONV_KERNEL],
                 conv_dw_b=g_cb, conv_ln_g=g_cln_g, conv_ln_b=g_cln_b, sgu_ln_g=g_sln_g, sgu_ln_b=g_sln_b,
                 sgu_w=g_sw, sgu_b=g_sb[:, :, 0])
    return dx_in, [g_in, g_out.reshape(N_CHIPS, -1, D_MODEL)], small


def _layer_bwd(dxb, p, s, tabs, token, out_dtype):
    dmidb, ffn_big, g_ffn_norm = _ffn_bwd(dxb, p, s, token)
    dx_in, mix_big, small = _mixer_bwd(dmidb, p, s, tabs, token, out_dtype)
    return dx_in, mix_big + ffn_big, dict(small, ffn_norm_g=g_ffn_norm)


def _mixer_weights(gathered):
    w_in, w_out = gathered
    return dict(w_in=w_in, w_out=w_out.reshape(-1, D_MODEL))


def _ffn_weights(gathered):
    w_gate, w_up, w_down = gathered
    return dict(w_gate=w_gate, w_up=w_up, w_down=w_down.reshape(-1, D_MODEL))


def _small_params(l, conv_w_full, mix_norm_g, sink, conv_dw_b, conv_ln_g, conv_ln_b, sgu_ln_g, sgu_ln_b, sgu_w, sgu_b,
                  ffn_norm_g):
    return dict(
        mix_norm_g=mix_norm_g[l:l + 1], ffn_norm_g=ffn_norm_g[l:l + 1],
        sink3=jnp.broadcast_to(sink[l].reshape(N_KV_HEADS, Q_PER_KV, 1), (N_KV_HEADS, Q_PER_KV, LANES)),
        conv_w32=jnp.pad(conv_w_full[l], ((0, 32 - CONV_KERNEL), (0, 0))),
        conv_dw_b=conv_dw_b[l:l + 1], conv_ln_g=conv_ln_g[l:l + 1], conv_ln_b=conv_ln_b[l:l + 1],
        sgu_ln_g=sgu_ln_g[l:l + 1], sgu_ln_b=sgu_ln_b[l:l + 1], sgu_w16=sgu_w[l].astype(BF16),
        sgu_b3=jnp.broadcast_to(sgu_b[l][:, :, None], (SGU_HEADS, CHUNK, CHUNK)))


_SMALL = ["mix_norm_g", "sink", "conv_dw_b", "conv_ln_g", "conv_ln_b", "sgu_ln_g", "sgu_ln_b", "sgu_w", "sgu_b", "ffn_norm_g",
          "final_norm_g"]


def _pack_rows(arrays):
    rows, counts = [], []
    for a in arrays:
        flat = a.reshape(-1)
        n = -(-flat.shape[0] // LANES)
        rows.append(jnp.pad(flat, (0, n * LANES - flat.shape[0])).reshape(n, LANES))
        counts.append(n)
    packed = jnp.concatenate(rows, axis=0)
    pad = -packed.shape[0] % PACK_ROWS
    return jnp.pad(packed, ((0, pad), (0, 0))), counts


def _unpack_rows(packed, counts, shapes):
    out, r = [], 0
    for n, shape in zip(counts, shapes):
        size = math.prod(shape)
        out.append(packed[r:r + n].reshape(-1)[:size].reshape(shape))
        r += n
    return out


def kernel(x, mix_norm_g, w_in, sink, conv_dw_w, conv_dw_b, conv_ln_g, conv_ln_b, sgu_ln_g, sgu_ln_b, sgu_w, sgu_b, w_out, ffn_norm_g, w_gate, w_up, w_down, final_norm_g, loss_target, m_mix_norm_g, m_w_in, m_sink, m_conv_dw_w, m_conv_dw_b, m_conv_ln_g, m_conv_ln_b, m_sgu_ln_g, m_sgu_ln_b, m_sgu_w, m_sgu_b, m_w_out, m_ffn_norm_g, m_w_gate, m_w_up, m_w_down, m_final_norm_g, v_mix_norm_g, v_w_in, v_sink, v_conv_dw_w, v_conv_dw_b, v_conv_ln_g, v_conv_ln_b, v_sgu_ln_g, v_sgu_ln_b, v_sgu_w, v_sgu_b, v_w_out, v_ffn_norm_g, v_w_gate, v_w_up, v_w_down, v_final_norm_g):
    S = x.shape[1]
    my_chip = 2 * lax.axis_index("x") + lax.axis_index("y")
    c_idx = lax.axis_index("c").astype(jnp.int32).reshape(1)
    big_w = [w_in, w_out, w_gate, w_up, w_down]
    big_m = [m_w_in, m_w_out, m_w_gate, m_w_up, m_w_down]
    big_v = [v_w_in, v_w_out, v_w_gate, v_w_up, v_w_down]
    n_kinds = len(big_w)

    x_idx = lax.axis_index("x").astype(jnp.int32).reshape(1)
    y_idx = lax.axis_index("y").astype(jnp.int32).reshape(1)
    conv_w_all = gather_small(conv_dw_w)
    conv_w_full = jnp.transpose(conv_w_all, (1, 2, 0, 3)).reshape(DEPTH, CONV_KERNEL, CONV_WIDTH)
    tabs = rope_tables(S)
    no_token = jnp.zeros(TOKEN.shape, TOKEN.dtype)

    mixer_kinds, ffn_kinds = [0, 1], [2, 3, 4]
    shards = [[w[l].astype(BF16) for w in big_w] for l in range(DEPTH)]

    def fetch(pending, after):
        send_sems, recv_sems, srcs, lands, _ = pending
        return forward_halves(gather_wait(send_sems, recv_sems, srcs, lands, after))

    first_mixer = gather_start([shards[0][k] for k in mixer_kinds], conv_w_all)
    first_ffn = gather_start([shards[0][k] for k in ffn_kinds], first_mixer[4])
    pending = gather_start(shards[1], first_ffn[4])
    act = x[0]
    h = rms_fwd(act, mix_norm_g[0:1], no_token)
    saved, params = [], []
    for l in range(DEPTH):
        p = _small_params(l, conv_w_full, mix_norm_g, sink, conv_dw_b, conv_ln_g, conv_ln_b, sgu_ln_g, sgu_ln_b, sgu_w, sgu_b,
                          ffn_norm_g)
        next_gain = mix_norm_g[l + 1:l + 2] if l + 1 < DEPTH else final_norm_g.reshape(1, D_MODEL)
        if l == 0:
            p.update(_mixer_weights(fetch(first_mixer, act)))
            x_mid, h2, s_mix = _mixer_fwd(act, h, p, tabs, pending[4])
            p.update(_ffn_weights(fetch(first_ffn, x_mid)))
            late, token = None, no_token
        else:
            send_sems, recv_sems, srcs, lands, _ = pending
            lands = gather_wait(send_sems, recv_sems, srcs, lands, act)
            w_in_full = forward_halves(lands[:1])[0]
            p.update(w_in=w_in_full)
            fwd_send, fwd_recv, rest, token = forward_start(lands[1:])

            def late(mix, fwd_send=fwd_send, fwd_recv=fwd_recv, rest=rest):
                w_out_full, *ffn_full = forward_wait(fwd_send, fwd_recv, rest, mix)
                return dict(_ffn_weights(ffn_full), w_out=w_out_full.reshape(-1, D_MODEL))

            if l + 1 < DEPTH:
                pending = gather_start(shards[l + 1], w_in_full)
                token = token + pending[4]
        if l > 0:
            x_mid, h2, s_mix = _mixer_fwd(act, h, p, tabs, token, late)
        act, h, s_ffn = _ffn_fwd(x_mid, h2, p, next_gain, token)
        params.append(p)
        saved.append({**s_mix, **s_ffn})
    loss_part, dxb, g_final = final_loss(act, final_norm_g.reshape(1, D_MODEL), loss_target[0])
    loss = lax.psum(loss_part[0, 0], ("x", "y", "c"))

    halves = [lax.empty((DEPTH, w.shape[1] // 2, w.shape[2]), F32) for w in big_w]
    small_grads = [None] * DEPTH

    def chip_start(layer, kinds, grads, recv):
        chip_sum = [add_sibling_half(g, r, c_idx) for g, r in zip(grads, recv)]
        send_sems, recv_sems, parts, lands, token = chip_parts_start(chip_sum)
        return (layer, kinds, send_sems, recv_sems, parts, lands), token

    def reduce_start(layer, kinds, grads):
        return chip_start(layer, kinds, grads, exchange_sibling_halves(grads))

    def reduce_finish(pending, halves, after):
        layer, kinds, send_sems, recv_sems, parts, lands = pending
        parts, others = chip_parts_wait(send_sems, recv_sems, parts, lands, after)
        halves = list(halves)
        for i, k in enumerate(kinds):
            halves[k] = sum_chips(parts[i], others[i], halves[k], x_idx, y_idx, layer)
        return halves

    pending, token = None, no_token
    for l in reversed(range(DEPTH)):
        dmidb, ffn_big, g_ffn_norm = _ffn_bwd(dxb, params[l], saved[l], token)
        if l == 0:
            last_ffn, mixer_token = reduce_start(l, ffn_kinds, ffn_big)
        else:
            half_shapes = [(g.shape[0], g.shape[1] // 2, g.shape[2]) for g in ffn_big]
            sib_send, sib_recv, ffn_big, ffn_lands, mixer_token = pair_start("sibling_start", _sibling_half_copies, ffn_big, half_shapes)
        dxb, mix_big, small = _mixer_bwd(dmidb, params[l], saved[l], tabs, mixer_token, F32 if l == 0 else BF16)
        small_grads[l] = dict(small, ffn_norm_g=g_ffn_norm)
        if pending is not None:
            halves = reduce_finish(pending, halves, dxb)
        if l == 0:
            last_mixer, token = reduce_start(l, mixer_kinds, mix_big)
            halves = reduce_finish(last_ffn, halves, token)
        else:
            ffn_big, ffn_recv = pair_wait("sibling_wait", _sibling_half_copies, sib_send, sib_recv, ffn_big, ffn_lands, dxb)
            mix_recv = exchange_sibling_halves(mix_big)
            pending, token = chip_start(l, mixer_kinds + ffn_kinds, list(mix_big) + list(ffn_big), list(mix_recv) + list(ffn_recv))

    def final_start(kinds):
        send_sems, recv_sems, mine, lands, _ = pair_start("final_start", _sibling_whole_copies, [halves[k] for k in kinds],
                                                          [halves[k].shape for k in kinds])
        return send_sems, recv_sems, mine, lands

    ffn_final = final_start(ffn_kinds)

    stacked = {n: jnp.stack([small_grads[l][n] for l in range(DEPTH)]) for n in small_grads[0]}
    stacked["final_norm_g"] = g_final
    packed, counts = _pack_rows([stacked[n] for n in _SMALL] + [stacked["conv_dw_w"]])
    reduced = allreduce_small(packed)
    small_w = dict(mix_norm_g=mix_norm_g, sink=sink, conv_dw_b=conv_dw_b, conv_ln_g=conv_ln_g, conv_ln_b=conv_ln_b,
                   sgu_ln_g=sgu_ln_g, sgu_ln_b=sgu_ln_b, sgu_w=sgu_w, sgu_b=sgu_b, ffn_norm_g=ffn_norm_g,
                   final_norm_g=final_norm_g)
    small_m = dict(mix_norm_g=m_mix_norm_g, sink=m_sink, conv_dw_b=m_conv_dw_b, conv_ln_g=m_conv_ln_g,
                   conv_ln_b=m_conv_ln_b, sgu_ln_g=m_sgu_ln_g, sgu_ln_b=m_sgu_ln_b, sgu_w=m_sgu_w, sgu_b=m_sgu_b,
                   ffn_norm_g=m_ffn_norm_g, final_norm_g=m_final_norm_g)
    small_v = dict(mix_norm_g=v_mix_norm_g, sink=v_sink, conv_dw_b=v_conv_dw_b, conv_ln_g=v_conv_ln_g,
                   conv_ln_b=v_conv_ln_b, sgu_ln_g=v_sgu_ln_g, sgu_ln_b=v_sgu_ln_b, sgu_w=v_sgu_w, sgu_b=v_sgu_b,
                   ffn_norm_g=v_ffn_norm_g, final_norm_g=v_final_norm_g)
    shapes = [small_w[n].shape for n in _SMALL] + [(DEPTH, CONV_KERNEL, CONV_WIDTH)]
    red = _unpack_rows(reduced, counts, shapes)
    g_small = dict(zip(_SMALL, red[:-1]))
    g_small["conv_dw_w"] = lax.dynamic_slice_in_dim(red[-1], my_chip * LANES, LANES, axis=2)
    small_w["conv_dw_w"], small_m["conv_dw_w"], small_v["conv_dw_w"] = conv_dw_w, m_conv_dw_w, v_conv_dw_w
    names = _SMALL + ["conv_dw_w"]
    pw, cnt = _pack_rows([small_w[n] for n in names])
    pg, _ = _pack_rows([g_small[n] for n in names])
    pm, _ = _pack_rows([small_m[n] for n in names])
    pv, _ = _pack_rows([small_v[n] for n in names])
    sd, sm, sv = adamw(pw, pg, pm, pv)
    shp = [small_w[n].shape for n in names]
    d_small = dict(zip(names, _unpack_rows(sd, cnt, shp)))
    m_small = dict(zip(names, _unpack_rows(sm, cnt, shp)))
    v_small = dict(zip(names, _unpack_rows(sv, cnt, shp)))

    big_names = ["w_in", "w_out", "w_gate", "w_up", "w_down"]
    g_big, d_big, m_big, v_big = {}, {}, {}, {}
    after = sd
    for kinds, final in ((ffn_kinds, ffn_final), (mixer_kinds, None)):
        if final is None:
            halves = reduce_finish(last_mixer, halves, after)
            final = final_start(kinds)
        send_sems, recv_sems, sent, lands = final
        for i, k in enumerate(kinds):
            n = big_names[k]
            mine, theirs = pair_wait_one("final_wait", send_sems, recv_sems, sent[i], lands[i], after, i)
            g_big[n], d_big[n], m_big[n], v_big[n] = adamw_halves(big_w[k], mine, theirs, big_m[k], big_v[k], c_idx)
            after = d_big[n]

    order = ["mix_norm_g", "w_in", "sink", "conv_dw_w", "conv_dw_b", "conv_ln_g", "conv_ln_b", "sgu_ln_g", "sgu_ln_b",
             "sgu_w", "sgu_b", "w_out", "ffn_norm_g", "w_gate", "w_up", "w_down", "final_norm_g"]
    grads = {**g_small, **g_big}
    deltas = {**d_small, **d_big}
    new_m = {**m_small, **m_big}
    new_v = {**v_small, **v_big}
    return (loss, dxb[None], *[grads[n] for n in order], *[deltas[n] for n in order],
            *[new_m[n] for n in order], *[new_v[n] for n in order])
```

```python
import functools
import math

import jax
import jax.numpy as jnp
from jax import lax
from jax.experimental import pallas as pl
from jax.experimental.pallas import tpu as pltpu

F32, BF16 = jnp.float32, jnp.bfloat16

D_MODEL = 2048
DEPTH = 4
HEAD_DIM = 128
N_Q_HEADS = 8
N_KV_HEADS = 2
Q_PER_KV = N_Q_HEADS // N_KV_HEADS
ATTN_WIDTH = N_Q_HEADS * HEAD_DIM
KV_WIDTH = N_KV_HEADS * HEAD_DIM
CONV_WIDTH = 512
CONV_KERNEL = 31
CONV_PAD = 16
SGU_WIDTH = 512
SGU_HEADS = 4
CHUNK = 128
IN_WIDTH = 3584
D_FF = 5632
WINDOW = 128
ROT_DIM = 32
ROPE_THETA = 500000.0
EPS = 1e-6
N_CHIPS = 4
N_DEV = 8
LANES = 128
PACK_ROWS = 64
OFF_K = ATTN_WIDTH
OFF_V = OFF_K + KV_WIDTH
OFF_CA = OFF_V + KV_WIDTH
OFF_CG = OFF_CA + CONV_WIDTH
OFF_U = OFF_CG + CONV_WIDTH
OFF_VV = OFF_U + SGU_WIDTH

ADAM_LR, ADAM_B1, ADAM_B2, ADAM_EPS, ADAM_WD, ADAM_STEP = 0.001, 0.9, 0.999, 1e-08, 0.01, 10

VMEM_LIMIT = 56 * 1024 * 1024
MESH = pl.DeviceIdType.MESH
HBM = pl.BlockSpec(memory_space=pltpu.HBM)
VMEM_SPEC = pl.BlockSpec(memory_space=pltpu.VMEM)


def _call(name, body, *, grid, in_specs, out_specs, out_shape, scratch=(), sem=None, aliases=None):
    params = dict(vmem_limit_bytes=VMEM_LIMIT)
    if sem is not None:
        params["dimension_semantics"] = sem
    return pl.pallas_call(
        body, name=name, grid=grid, in_specs=in_specs, out_specs=out_specs, out_shape=out_shape,
        scratch_shapes=list(scratch), input_output_aliases=aliases or {}, compiler_params=pltpu.CompilerParams(**params))


def _sigmoid(x):
    return 1.0 / (1.0 + jnp.exp(-x))


def rms_fwd(x, g, token):
    S = x.shape[0]
    tm = min(512, S)

    def body(x_ref, g_ref, token_ref, o_ref):
        xv = x_ref[...]
        r = lax.rsqrt(jnp.mean(xv * xv, axis=-1, keepdims=True) + EPS)
        o_ref[...] = (xv * r * g_ref[...]).astype(BF16)

    return _call("rms_fwd", body, grid=(S // tm,),
                 in_specs=[pl.BlockSpec((tm, D_MODEL), lambda i: (i, 0)), pl.BlockSpec((1, D_MODEL), lambda i: (0, 0)),
                           pl.BlockSpec((8, LANES), lambda i: (0, 0))],
                 out_specs=pl.BlockSpec((tm, D_MODEL), lambda i: (i, 0)),
                 out_shape=jax.ShapeDtypeStruct((S, D_MODEL), BF16), sem=("parallel",))(x, g, token)


def _rms_bwd_math(xv, gv, dh):
    r = lax.rsqrt(jnp.mean(xv * xv, axis=-1, keepdims=True) + EPS)
    n = xv * r
    dn = dh * gv
    dx = r * (dn - n * jnp.mean(dn * n, axis=-1, keepdims=True))
    dg = jnp.sum(dh * n, axis=0, keepdims=True)
    return dx, dg


def rms_bwd(x, g, dh, dres, out_dtype):
    S = x.shape[0]
    tm = min(512, S)

    def body(x_ref, g_ref, dh_ref, dres_ref, dx_ref, dg_ref):
        dx, dg = _rms_bwd_math(x_ref[...], g_ref[...], dh_ref[...].astype(F32))
        dx_ref[...] = (dx + dres_ref[...].astype(F32)).astype(out_dtype)

        @pl.when(pl.program_id(0) == 0)
        def _():
            dg_ref[...] = dg

        @pl.when(pl.program_id(0) > 0)
        def _():
            dg_ref[...] += dg

    row = pl.BlockSpec((tm, D_MODEL), lambda i: (i, 0))
    vec = pl.BlockSpec((1, D_MODEL), lambda i: (0, 0))
    return _call("rms_bwd", body, grid=(S // tm,), in_specs=[row, vec, row, row], out_specs=[row, vec],
                 out_shape=[jax.ShapeDtypeStruct((S, D_MODEL), out_dtype), jax.ShapeDtypeStruct((1, D_MODEL), F32)],
                 sem=("arbitrary",))(x, g, dh, dres)


def final_loss(x, g, target):
    S = x.shape[0]
    tm = min(256, S)

    def body(x_ref, g_ref, t_ref, loss_ref, dxb_ref, dg_ref):
        xv = x_ref[...]
        gv = g_ref[...]
        r = lax.rsqrt(jnp.mean(xv * xv, axis=-1, keepdims=True) + EPS)
        err = xv * r * gv - t_ref[...]
        part = 0.5 * jnp.sum(jnp.mean(err * err, axis=-1, keepdims=True), axis=0, keepdims=True)
        dx, dg = _rms_bwd_math(xv, gv, err * (1.0 / D_MODEL))
        dxb_ref[...] = dx.astype(BF16)

        @pl.when(pl.program_id(0) == 0)
        def _():
            dg_ref[...] = dg
            loss_ref[...] = part

        @pl.when(pl.program_id(0) > 0)
        def _():
            dg_ref[...] += dg
            loss_ref[...] += part

    row = pl.BlockSpec((tm, D_MODEL), lambda i: (i, 0))
    vec = pl.BlockSpec((1, D_MODEL), lambda i: (0, 0))
    one = pl.BlockSpec((1, 1), lambda i: (0, 0))
    return _call("final_loss", body, grid=(S // tm,), in_specs=[row, vec, row], out_specs=[one, row, vec],
                 out_shape=[jax.ShapeDtypeStruct((1, 1), F32), jax.ShapeDtypeStruct((S, D_MODEL), BF16),
                            jax.ShapeDtypeStruct((1, D_MODEL), F32)],
                 sem=("arbitrary",))(x, g, target)


EPILOGUE_ROWS = 256
NN = (((1,), (0,)), ((), ()))
NT = (((1,), (1,)), ((), ()))
TN = (((0,), (0,)), ((), ()))


def _matmul(name, operands, in_specs, out_shape, out_specs, grid, pairs, dims, acc_shape, epilogue):
    n_in, n_out, nk = len(operands), len(out_shape), grid[-1]

    def body(*refs):
        ins, outs = refs[:n_in], refs[n_in:n_in + n_out]
        part = None
        for ia, ib in pairs:
            d = lax.dot_general(ins[ia][...], ins[ib][...], dims, preferred_element_type=F32)
            part = d if part is None else part + d
        if nk == 1:
            epilogue(part, ins, outs)
        else:
            acc = refs[-1]
            k = pl.program_id(len(grid) - 1)

            @pl.when(k == 0)
            def _():
                acc[...] = part

            @pl.when(k > 0)
            def _():
                acc[...] += part

            @pl.when(k == nk - 1)
            def _():
                epilogue(acc[...], ins, outs)

    scratch = [pltpu.VMEM(acc_shape, F32)] if nk > 1 else []
    sem = ("parallel",) * (len(grid) - 1) + ("arbitrary",)
    return _call(name, body, grid=grid, in_specs=in_specs, out_specs=out_specs, out_shape=out_shape,
                 scratch=scratch, sem=sem)(*operands)


def _store(dtype):
    def epilogue(acc, ins, outs):
        outs[0][...] = acc.astype(dtype)
    return epilogue


def mm_nn_cols(a, w, token):
    S, K = a.shape
    J, _, Ns = w.shape
    tm = min(512, S)
    return _matmul("mm_nn_cols", (a, w, token),
                   [pl.BlockSpec((tm, K), lambda j, i, k: (i, 0)), pl.BlockSpec((None, K, Ns), lambda j, i, k: (j, 0, 0)),
                    pl.BlockSpec((8, LANES), lambda j, i, k: (0, 0))],
                   [jax.ShapeDtypeStruct((S, J * Ns), BF16)], [pl.BlockSpec((tm, Ns), lambda j, i, k: (i, j))],
                   (J, S // tm, 1), [(0, 1)], NN, None, _store(BF16))[0]


def ffn_up(h, wg, wu, token):
    S, K = h.shape
    J, _, Ns = wg.shape
    tm = min(512, S)

    sub = min(EPILOGUE_ROWS, tm)

    def body(h_ref, wg_ref, wu_ref, token_ref, g_ref, u_ref, a_ref):
        for r in range(tm // sub):
            rows = slice(r * sub, (r + 1) * sub)
            hv = h_ref[rows, :]
            gv = jnp.dot(hv, wg_ref[...], preferred_element_type=F32)
            uv = jnp.dot(hv, wu_ref[...], preferred_element_type=F32)
            g_ref[rows, :] = gv.astype(BF16)
            u_ref[rows, :] = uv.astype(BF16)
            a_ref[rows, :] = (gv * _sigmoid(gv) * uv).astype(BF16)

    wspec = pl.BlockSpec((None, K, Ns), lambda j, i: (j, 0, 0))
    ospec = pl.BlockSpec((tm, Ns), lambda j, i: (i, j))
    oshape = jax.ShapeDtypeStruct((S, J * Ns), BF16)
    return _call("ffn_up", body, grid=(J, S // tm),
                 in_specs=[pl.BlockSpec((tm, K), lambda j, i: (i, 0)), wspec, wspec, pl.BlockSpec((8, LANES), lambda j, i: (0, 0))],
                 out_specs=[ospec, ospec, ospec], out_shape=[oshape, oshape, oshape],
                 sem=("parallel", "parallel"))(h, wg, wu, token)


def mm_nn_rows_res(a, w, res, gain):
    S, K = a.shape
    N = w.shape[1]
    tm = min(512, S)
    tk, tn = (K, N) if K <= 2048 else (K // 2, N // 2)
    n_n, n_k = N // tn, K // tk

    def body(a_ref, w_ref, res_ref, g_ref, x_ref, h_ref, *acc):
        n, k = pl.program_id(1), pl.program_id(2)

        def normed(xv):
            r = lax.rsqrt(jnp.mean(xv * xv, axis=-1, keepdims=True) + EPS)
            h_ref[...] = (xv * r * g_ref[...]).astype(BF16)

        def store_columns(total):
            if n_n == 1:
                xv = total + res_ref[...]
                x_ref[...] = xv
                normed(xv)
                return
            for c in range(n_n):
                @pl.when(n == c)
                def _(c=c):
                    cols = slice(c * tn, (c + 1) * tn)
                    x_ref[:, cols] = total + res_ref[:, cols]

            @pl.when(n == n_n - 1)
            def _():
                normed(x_ref[...])

        part = jnp.dot(a_ref[...], w_ref[...], preferred_element_type=F32)
        if n_k == 1:
            store_columns(part)
        else:
            @pl.when(k == 0)
            def _():
                acc[0][...] = part

            @pl.when(k > 0)
            def _():
                acc[0][...] += part

            @pl.when(k == n_k - 1)
            def _():
                store_columns(acc[0][...])

    row = pl.BlockSpec((tm, N), lambda i, n, k: (i, 0))
    return _call("mm_nn_rows_res", body, grid=(S // tm, n_n, n_k),
                 in_specs=[pl.BlockSpec((tm, tk), lambda i, n, k: (i, k)), pl.BlockSpec((tk, tn), lambda i, n, k: (k, n)), row,
                           pl.BlockSpec((1, N), lambda i, n, k: (0, 0))],
                 out_specs=[row, row], out_shape=[jax.ShapeDtypeStruct((S, N), F32), jax.ShapeDtypeStruct((S, N), BF16)],
                 scratch=[pltpu.VMEM((tm, tn), F32)] if n_k > 1 else [],
                 sem=("parallel", "arbitrary", "arbitrary"))(a, w, res, gain)


def mm_nt_cols(pairs_in, out_dtype, shards_per_step):
    dz0, w0 = pairs_in[0]
    S = dz0.shape[0]
    J, K, Ns = w0.shape
    tm = min(512, S)
    sps = shards_per_step
    operands, specs, pairs = [], [], []
    for dz, w in pairs_in:
        for s in range(sps):
            pairs.append((len(operands), len(operands) + 1))
            operands += [dz, w]
            specs += [pl.BlockSpec((tm, Ns), lambda i, j, s=s: (i, j * sps + s)),
                      pl.BlockSpec((None, K, Ns), lambda i, j, s=s: (j * sps + s, 0, 0))]
    return _matmul("mm_nt_cols%d" % len(pairs_in), tuple(operands), specs,
                   [jax.ShapeDtypeStruct((S, K), out_dtype)], [pl.BlockSpec((tm, K), lambda i, j: (i, 0))],
                   (S // tm, J // sps), pairs, NT, (tm, K), _store(out_dtype))[0]


def mm_nt_rows(dy, w, token):
    S, N = dy.shape
    K = w.shape[0]
    tm, tko = min(1024, S), 512
    return _matmul("mm_nt_rows", (dy, w, token),
                   [pl.BlockSpec((tm, N), lambda i, kk, z: (i, 0)), pl.BlockSpec((tko, N), lambda i, kk, z: (kk, 0)),
                    pl.BlockSpec((8, LANES), lambda i, kk, z: (0, 0))],
                   [jax.ShapeDtypeStruct((S, K), BF16)], [pl.BlockSpec((tm, tko), lambda i, kk, z: (i, kk))],
                   (S // tm, K // tko, 1), [(0, 1)], NT, None, _store(BF16))[0]


def ffn_down_bwd(dy, w, gate, up, token):
    S, N = dy.shape
    K = w.shape[0]
    tm, tko = min(1024, S), 512
    sub = min(EPILOGUE_ROWS, tm)

    def body(dy_ref, w_ref, g_ref, u_ref, token_ref, dg_ref, du_ref):
        for r in range(tm // sub):
            rows = slice(r * sub, (r + 1) * sub)
            dact = lax.dot_general(dy_ref[rows, :], w_ref[...], NT, preferred_element_type=F32)
            gv = g_ref[rows, :].astype(F32)
            uv = u_ref[rows, :].astype(F32)
            sg = _sigmoid(gv)
            dg_ref[rows, :] = (dact * uv * sg * (1.0 + gv * (1.0 - sg))).astype(BF16)
            du_ref[rows, :] = (dact * gv * sg).astype(BF16)

    tile = pl.BlockSpec((tm, tko), lambda i, kk: (i, kk))
    oshape = jax.ShapeDtypeStruct((S, K), BF16)
    return _call("ffn_down_bwd", body, grid=(S // tm, K // tko),
                 in_specs=[pl.BlockSpec((tm, N), lambda i, kk: (i, 0)), pl.BlockSpec((tko, N), lambda i, kk: (kk, 0)), tile, tile,
                           pl.BlockSpec((8, LANES), lambda i, kk: (0, 0))],
                 out_specs=[tile, tile], out_shape=[oshape, oshape], sem=("parallel", "parallel"))(dy, w, gate, up, token)


def mm_tn_cols(a, dz, J):
    S, M = a.shape
    Ns = dz.shape[1] // J
    tm, tk = 512, S
    return _matmul("mm_tn_cols", (a, dz),
                   [pl.BlockSpec((tk, tm), lambda j, m, k: (k, m)), pl.BlockSpec((tk, Ns), lambda j, m, k: (k, j))],
                   [jax.ShapeDtypeStruct((J, M, Ns), BF16)], [pl.BlockSpec((None, tm, Ns), lambda j, m, k: (j, m, 0))],
                   (J, M // tm, S // tk), [(0, 1)], TN, (tm, Ns), _store(BF16))[0]


def mm_tn_rows(a, dy):
    S, K = a.shape
    N = dy.shape[1]
    tm, tk = 512, S
    return _matmul("mm_tn_rows", (a, dy),
                   [pl.BlockSpec((tk, tm), lambda m, k: (k, m)), pl.BlockSpec((tk, N), lambda m, k: (k, 0))],
                   [jax.ShapeDtypeStruct((K, N), BF16)], [pl.BlockSpec((tm, N), lambda m, k: (m, 0))],
                   (K // tm, S // tk), [(0, 1)], TN, (tm, N), _store(BF16))[0]


def rope_tables(S):
    half = ROT_DIM // 2
    pos = jnp.arange(S, dtype=F32)
    inv = ROPE_THETA ** (-jnp.arange(0, ROT_DIM, 2, dtype=F32) / ROT_DIM)
    ang = pos[:, None] * inv[None, :]
    cos, sin = jnp.cos(ang), jnp.sin(ang)
    zeros = jnp.zeros((S, HEAD_DIM - ROT_DIM), F32)
    c = jnp.concatenate([cos, cos, jnp.ones((S, HEAD_DIM - ROT_DIM), F32)], axis=1)
    s_lo = jnp.concatenate([-sin, jnp.zeros((S, half), F32), zeros], axis=1)
    s_hi = jnp.concatenate([jnp.zeros((S, half), F32), sin, zeros], axis=1)
    return c, s_lo, s_hi


ROPE_ROWS = 512


def _rope(t, c, s_lo, s_hi):
    half = ROT_DIM // 2
    return t * c + pltpu.roll(t, HEAD_DIM - half, 1) * s_lo + pltpu.roll(t, half, 1) * s_hi


def _unrope(d, c, s_lo, s_hi):
    half = ROT_DIM // 2
    return d * c + pltpu.roll(d * s_lo, half, 1) + pltpu.roll(d * s_hi, HEAD_DIM - half, 1)


def rope_fwd(z, tabs):
    S = z.shape[0]
    T = min(ROPE_ROWS, S)

    def body(q_ref, kv_ref, c_ref, sl_ref, sh_ref, qr_ref, kp_ref, vp_ref):
        i = pl.program_id(0)

        @pl.when(i == 0)
        def _():
            zero = jnp.zeros((CHUNK, KV_WIDTH), BF16)
            kp_ref[0:CHUNK, :] = zero
            vp_ref[0:CHUNK, :] = zero
            kp_ref[S + CHUNK:S + 2 * CHUNK, :] = zero
            vp_ref[S + CHUNK:S + 2 * CHUNK, :] = zero

        c, sl, sh = c_ref[...], sl_ref[...], sh_ref[...]
        for h in range(N_Q_HEADS):
            cols = slice(h * HEAD_DIM, (h + 1) * HEAD_DIM)
            qr_ref[:, cols] = _rope(q_ref[:, cols].astype(F32), c, sl, sh).astype(BF16)
        rows = pl.ds(pl.multiple_of(CHUNK + i * T, CHUNK), T)
        for g in range(N_KV_HEADS):
            cols = slice(g * HEAD_DIM, (g + 1) * HEAD_DIM)
            kp_ref[rows, cols] = _rope(kv_ref[:, cols].astype(F32), c, sl, sh).astype(BF16)
        vp_ref[rows, :] = kv_ref[:, KV_WIDTH:2 * KV_WIDTH]

    tab = pl.BlockSpec((T, HEAD_DIM), lambda i: (i, 0))
    pad = pl.BlockSpec((S + 2 * CHUNK, KV_WIDTH), lambda i: (0, 0))
    return _call("rope_fwd", body, grid=(S // T,),
                 in_specs=[pl.BlockSpec((T, ATTN_WIDTH), lambda i: (i, 0)),
                           pl.BlockSpec((T, 2 * KV_WIDTH), lambda i: (i, OFF_K // (2 * KV_WIDTH))), tab, tab, tab],
                 out_specs=[pl.BlockSpec((T, ATTN_WIDTH), lambda i: (i, 0)), pad, pad],
                 out_shape=[jax.ShapeDtypeStruct((S, ATTN_WIDTH), BF16), jax.ShapeDtypeStruct((S + 2 * CHUNK, KV_WIDTH), BF16),
                            jax.ShapeDtypeStruct((S + 2 * CHUNK, KV_WIDTH), BF16)], sem=("arbitrary",))(z, z, *tabs)


def rope_bwd(dq, dkp, dvp, tabs):
    S = dq.shape[0]
    T = min(ROPE_ROWS, S)

    def body(dq_ref, dk_ref, dv_ref, c_ref, sl_ref, sh_ref, o_ref):
        c, sl, sh = c_ref[...], sl_ref[...], sh_ref[...]
        for h in range(N_Q_HEADS):
            cols = slice(h * HEAD_DIM, (h + 1) * HEAD_DIM)
            o_ref[:, cols] = _unrope(dq_ref[:, cols], c, sl, sh).astype(BF16)
        rows = pl.ds(pl.multiple_of(CHUNK + pl.program_id(0) * T, CHUNK), T)
        for g in range(N_KV_HEADS):
            cols = slice(g * HEAD_DIM, (g + 1) * HEAD_DIM)
            o_ref[:, OFF_K + g * HEAD_DIM:OFF_K + (g + 1) * HEAD_DIM] = _unrope(dk_ref[rows, cols], c, sl, sh).astype(BF16)
        o_ref[:, OFF_V:OFF_V + KV_WIDTH] = dv_ref[rows, :].astype(BF16)

    tab = pl.BlockSpec((T, HEAD_DIM), lambda i: (i, 0))
    pad = pl.BlockSpec((S + 2 * CHUNK, KV_WIDTH), lambda i: (0, 0))
    return _call("rope_bwd", body, grid=(S // T,),
                 in_specs=[pl.BlockSpec((T, ATTN_WIDTH), lambda i: (i, 0)), pad, pad, tab, tab, tab],
                 out_specs=pl.BlockSpec((T, OFF_CA), lambda i: (i, 0)),
                 out_shape=jax.ShapeDtypeStruct((S, IN_WIDTH), BF16), sem=("parallel",))(dq, dkp, dvp, *tabs)


STACK = Q_PER_KV * CHUNK


def _stack_heads(ref, rows):
    return jnp.concatenate([ref[rows, r * HEAD_DIM:(r + 1) * HEAD_DIM] for r in range(Q_PER_KV)], axis=0)


def _stack_sinks(s_ref):
    return jnp.concatenate([jnp.broadcast_to(s_ref[r:r + 1, 0:1], (CHUNK, 1)) for r in range(Q_PER_KV)], axis=0)


MASKED = -1e30


def _scores(q, kb):
    return lax.dot_general(q, kb, NT, preferred_element_type=F32) * (1.0 / math.sqrt(HEAD_DIM))


def _band_bias():
    row = lax.broadcasted_iota(jnp.int32, (STACK, 3 * CHUNK), 0) & (CHUNK - 1)
    col = lax.broadcasted_iota(jnp.int32, (STACK, 3 * CHUNK), 1)
    return jnp.where(jnp.abs(col - CHUNK - row) <= WINDOW, 0.0, MASKED).astype(F32)


def _edge_bias(n, S):
    kpos = (n - 1) * CHUNK + lax.broadcasted_iota(jnp.int32, (1, 3 * CHUNK), 1)
    return jnp.where((kpos >= 0) & (kpos < S), 0.0, MASKED).astype(F32)


def _softmax_sink(s, sk, bias):
    s = s + bias
    m = jnp.maximum(jnp.max(s, axis=1, keepdims=True), sk)
    e = jnp.exp(s - m)
    es = jnp.exp(sk - m)
    inv = 1.0 / (jnp.sum(e, axis=1, keepdims=True) + es)
    return e * inv, es * inv


def _block_views(i, nblk):
    ns = [i * nblk + b for b in range(nblk)]
    wins = [pl.ds(pl.multiple_of(n * CHUNK, CHUNK), 3 * CHUNK) for n in ns]
    rows = [slice(b * CHUNK, (b + 1) * CHUNK) for b in range(nblk)]
    return ns, wins, rows


def attn_fwd(qr, kp, vp, sink3):
    S = qr.shape[0]
    tq = min(2048, S)
    gw = Q_PER_KV * HEAD_DIM
    nblk = tq // CHUNK

    def body(q_ref, k_ref, v_ref, s_ref, o_ref):
        ns, wins, rows = _block_views(pl.program_id(1), nblk)
        sk = _stack_sinks(s_ref)
        band = _band_bias()
        scores = [_scores(_stack_heads(q_ref, rows[b]), k_ref[wins[b], :]) for b in range(nblk)]
        probs = [_softmax_sink(scores[b], sk, band + _edge_bias(ns[b], S))[0].astype(BF16) for b in range(nblk)]
        outs = [jnp.dot(probs[b], v_ref[wins[b], :], preferred_element_type=F32).astype(BF16) for b in range(nblk)]
        for b in range(nblk):
            for r in range(Q_PER_KV):
                o_ref[rows[b], r * HEAD_DIM:(r + 1) * HEAD_DIM] = outs[b][r * CHUNK:(r + 1) * CHUNK]

    kv = pl.BlockSpec((S + 2 * CHUNK, HEAD_DIM), lambda g, i: (0, g))
    return _call("attn_fwd", body, grid=(N_KV_HEADS, S // tq),
                 in_specs=[pl.BlockSpec((tq, gw), lambda g, i: (i, g)), kv, kv,
                           pl.BlockSpec((None, Q_PER_KV, LANES), lambda g, i: (g, 0, 0))],
                 out_specs=pl.BlockSpec((tq, gw), lambda g, i: (i, g)),
                 out_shape=jax.ShapeDtypeStruct((S, D_MODEL), BF16), sem=("parallel", "arbitrary"))(qr, kp, vp, sink3)


def attn_bwd(qr, kp, vp, sink3, dmix):
    S = qr.shape[0]
    tq = min(1024, S)
    gw = Q_PER_KV * HEAD_DIM
    scale = 1.0 / math.sqrt(HEAD_DIM)
    nblk = tq // CHUNK

    def body(q_ref, k_ref, v_ref, s_ref, do_ref, dq_ref, dk_ref, dv_ref, ds_ref):
        i = pl.program_id(1)

        @pl.when(i == 0)
        def _():
            dk_ref[...] = jnp.zeros_like(dk_ref)
            dv_ref[...] = jnp.zeros_like(dv_ref)
            ds_ref[...] = jnp.zeros_like(ds_ref)

        blocks = range(nblk)
        ns, wins, rows = _block_views(i, nblk)
        sk = _stack_sinks(s_ref)
        band = _band_bias()
        qs = [_stack_heads(q_ref, rows[b]) for b in blocks]
        dos = [_stack_heads(do_ref, rows[b]) for b in blocks]
        scores = [_scores(qs[b], k_ref[wins[b], :]) for b in blocks]
        dps = [lax.dot_general(dos[b], v_ref[wins[b], :], NT, preferred_element_type=F32) for b in blocks]
        probs = [_softmax_sink(scores[b], sk, band + _edge_bias(ns[b], S)) for b in blocks]
        deltas = [jnp.sum(probs[b][0] * dps[b], axis=1, keepdims=True) for b in blocks]
        dscs = [(probs[b][0] * (dps[b] - deltas[b]) * scale).astype(BF16) for b in blocks]
        dqs = [jnp.dot(dscs[b], k_ref[wins[b], :], preferred_element_type=F32) for b in blocks]
        dks = [lax.dot_general(dscs[b], qs[b], TN, preferred_element_type=F32) for b in blocks]
        dvs = [lax.dot_general(probs[b][0].astype(BF16), dos[b], TN, preferred_element_type=F32) for b in blocks]
        for b in blocks:
            for r in range(Q_PER_KV):
                dq_ref[rows[b], r * HEAD_DIM:(r + 1) * HEAD_DIM] = dqs[b][r * CHUNK:(r + 1) * CHUNK]
        for m in range(nblk + 2):
            parts = [(b, m - b) for b in blocks if 0 <= m - b <= 2]
            krows = pl.ds(pl.multiple_of(i * tq + m * CHUNK, CHUNK), CHUNK)
            dk_ref[krows, :] += sum(dks[b][o * CHUNK:(o + 1) * CHUNK] for b, o in parts)
            dv_ref[krows, :] += sum(dvs[b][o * CHUNK:(o + 1) * CHUNK] for b, o in parts)
        for r in range(Q_PER_KV):
            head = slice(r * CHUNK, (r + 1) * CHUNK)
            dsink = sum(jnp.sum(-probs[b][1][head] * deltas[b][head], axis=0, keepdims=True) for b in blocks)
            ds_ref[r:r + 1, :] += jnp.broadcast_to(dsink, (1, LANES))

    kv = pl.BlockSpec((S + 2 * CHUNK, HEAD_DIM), lambda g, i: (0, g))
    qspec = pl.BlockSpec((tq, gw), lambda g, i: (i, g))
    sspec = pl.BlockSpec((None, Q_PER_KV, LANES), lambda g, i: (g, 0, 0))
    padshape = jax.ShapeDtypeStruct((S + 2 * CHUNK, KV_WIDTH), F32)
    return _call("attn_bwd", body, grid=(N_KV_HEADS, S // tq),
                 in_specs=[qspec, kv, kv, sspec, qspec],
                 out_specs=[qspec, kv, kv, sspec],
                 out_shape=[jax.ShapeDtypeStruct((S, ATTN_WIDTH), F32), padshape, padshape,
                            jax.ShapeDtypeStruct((N_KV_HEADS, Q_PER_KV, LANES), F32)],
                 sem=("parallel", "arbitrary"))(qr, kp, vp, sink3, dmix)


CONV_TILE = 256


def _fill_padded(dst_ref, value, S):
    zero = jnp.zeros((CONV_PAD, LANES), F32)
    dst_ref[0:CONV_PAD, :] = zero
    dst_ref[CONV_PAD + S:2 * CONV_PAD + S, :] = zero
    dst_ref[CONV_PAD:CONV_PAD + S, :] = value


def conv_dw_fwd(z, w32, b):
    S = z.shape[0]
    T = min(CONV_TILE, S)
    lo = CONV_PAD - (CONV_KERNEL - 1) // 2

    def body(a_ref, g_ref, w_ref, b_ref, o_ref, c0_ref):
        _fill_padded(c0_ref, a_ref[...].astype(F32) * _sigmoid(g_ref[...].astype(F32)), S)

        def tile(t, carry):
            base = pl.multiple_of(t * T, T)
            acc = jnp.broadcast_to(b_ref[...], (T, LANES))
            for j in range(CONV_KERNEL):
                acc = acc + w_ref[j:j + 1, :] * c0_ref[pl.ds(base + lo + j, T), :]
            o_ref[pl.ds(base, T), :] = acc
            return carry

        lax.fori_loop(0, S // T, tile, 0)

    nca, ncg = OFF_CA // LANES, OFF_CG // LANES
    return _call("conv_dw_fwd", body, grid=(CONV_WIDTH // LANES,),
                 in_specs=[pl.BlockSpec((S, LANES), lambda cb: (0, nca + cb)), pl.BlockSpec((S, LANES), lambda cb: (0, ncg + cb)),
                           pl.BlockSpec((32, LANES), lambda cb: (0, cb)), pl.BlockSpec((1, LANES), lambda cb: (0, cb))],
                 out_specs=pl.BlockSpec((S, LANES), lambda cb: (0, cb)),
                 out_shape=jax.ShapeDtypeStruct((S, CONV_WIDTH), F32),
                 scratch=[pltpu.VMEM((S + 2 * CONV_PAD, LANES), F32)], sem=("parallel",))(z, z, w32, b)


def _ln_stats(x):
    mu = jnp.mean(x, axis=-1, keepdims=True)
    xc = x - mu
    rs = lax.rsqrt(jnp.mean(xc * xc, axis=-1, keepdims=True) + EPS)
    return xc * rs, rs


def _ln_bwd(dy, xh, rs, g):
    dxh = dy * g
    return rs * (dxh - jnp.mean(dxh, axis=-1, keepdims=True) - xh * jnp.mean(dxh * xh, axis=-1, keepdims=True))


def conv_ln_fwd(c1, g, b, mix):
    S = c1.shape[0]
    T = min(512, S)

    def body(x_ref, g_ref, b_ref, mix_ref, o_ref):
        xh, _ = _ln_stats(x_ref[...])
        y = xh * g_ref[...] + b_ref[...]
        o_ref[...] = (y * _sigmoid(y)).astype(BF16)

    row = pl.BlockSpec((T, CONV_WIDTH), lambda i: (i, 0))
    vec = pl.BlockSpec((1, CONV_WIDTH), lambda i: (0, 0))
    return _call("conv_ln_fwd", body, grid=(S // T,), in_specs=[row, vec, vec, pl.BlockSpec(memory_space=pl.ANY)],
                 out_specs=pl.BlockSpec((T, CONV_WIDTH), lambda i: (i, ATTN_WIDTH // CONV_WIDTH)),
                 out_shape=jax.ShapeDtypeStruct(mix.shape, BF16), sem=("parallel",), aliases={3: 0})(c1, g, b, mix)


def _acc_out(ref, value, step=None):
    step = pl.program_id(0) if step is None else step

    @pl.when(step == 0)
    def _():
        ref[...] = value

    @pl.when(step > 0)
    def _():
        ref[...] += value


def conv_ln_bwd(dmix, c1, g, b):
    S = c1.shape[0]
    T = min(512, S)

    def body(d_ref, x_ref, g_ref, b_ref, dx_ref, dg_ref, db_ref):
        xh, rs = _ln_stats(x_ref[...])
        gv = g_ref[...]
        y = xh * gv + b_ref[...]
        sg = _sigmoid(y)
        dy = d_ref[...].astype(F32) * sg * (1.0 + y * (1.0 - sg))
        dx_ref[...] = _ln_bwd(dy, xh, rs, gv)
        _acc_out(dg_ref, jnp.sum(dy * xh, axis=0, keepdims=True))
        _acc_out(db_ref, jnp.sum(dy, axis=0, keepdims=True))

    row = pl.BlockSpec((T, CONV_WIDTH), lambda i: (i, 0))
    vec = pl.BlockSpec((1, CONV_WIDTH), lambda i: (0, 0))
    vshape = jax.ShapeDtypeStruct((1, CONV_WIDTH), F32)
    return _call("conv_ln_bwd", body, grid=(S // T,),
                 in_specs=[pl.BlockSpec((T, CONV_WIDTH), lambda i: (i, ATTN_WIDTH // CONV_WIDTH)), row, vec, vec],
                 out_specs=[row, vec, vec], out_shape=[jax.ShapeDtypeStruct((S, CONV_WIDTH), F32), vshape, vshape],
                 sem=("arbitrary",))(dmix, c1, g, b)


def conv_dw_bwd(dc1, z, w32, dz):
    S = z.shape[0]
    T = min(CONV_TILE, S)
    half = (CONV_KERNEL - 1) // 2
    lo = CONV_PAD - half
    n_cb = CONV_WIDTH // LANES

    def body(d_ref, a_ref, g_ref, w_ref, dz_ref, o_ref, dw_ref, db_ref, c0_ref, d1_ref, wacc_ref, dg_ref):
        @pl.when(pl.program_id(1) == 0)
        def _():
            av = a_ref[...].astype(F32)
            sg = _sigmoid(g_ref[...].astype(F32))
            _fill_padded(c0_ref, av * sg, S)
            _fill_padded(d1_ref, d_ref[...], S)
            wacc_ref[...] = jnp.zeros_like(wacc_ref)

            def tile(t, carry):
                base = pl.multiple_of(t * T, T)
                d1 = d_ref[pl.ds(base, T), :]
                acc = jnp.zeros((T, LANES), F32)
                for j in range(CONV_KERNEL):
                    acc = acc + w_ref[j:j + 1, :] * d1_ref[pl.ds(base + CONV_PAD + half - j, T), :]
                    prod = d1 * c0_ref[pl.ds(base + lo + j, T), :]
                    wacc_ref[j] += jnp.sum(prod.reshape(T // 8, 8, LANES), axis=0)
                rows = pl.ds(base, T)
                a_t = a_ref[rows, :].astype(F32)
                s_t = _sigmoid(g_ref[rows, :].astype(F32))
                o_ref[rows, :] = (acc * s_t).astype(BF16)
                dg_ref[rows, :] = (acc * a_t * s_t * (1.0 - s_t)).astype(BF16)
                return carry

            lax.fori_loop(0, S // T, tile, 0)
            dw_ref[...] = jnp.sum(wacc_ref[...], axis=1)
            db_ref[...] = jnp.sum(d_ref[...], axis=0, keepdims=True)

        @pl.when(pl.program_id(1) == 1)
        def _():
            o_ref[...] = dg_ref[...]

    nca, ncg = OFF_CA // LANES, OFF_CG // LANES
    return _call("conv_dw_bwd", body, grid=(n_cb, 2),
                 in_specs=[pl.BlockSpec((S, LANES), lambda cb, j: (0, cb)), pl.BlockSpec((S, LANES), lambda cb, j: (0, nca + cb)),
                           pl.BlockSpec((S, LANES), lambda cb, j: (0, ncg + cb)), pl.BlockSpec((32, LANES), lambda cb, j: (0, cb)),
                           pl.BlockSpec(memory_space=pl.ANY)],
                 out_specs=[pl.BlockSpec((S, LANES), lambda cb, j: (0, nca + cb + n_cb * j)),
                            pl.BlockSpec((32, LANES), lambda cb, j: (0, cb)), pl.BlockSpec((1, LANES), lambda cb, j: (0, cb))],
                 out_shape=[jax.ShapeDtypeStruct(dz.shape, BF16), jax.ShapeDtypeStruct((32, CONV_WIDTH), F32),
                            jax.ShapeDtypeStruct((1, CONV_WIDTH), F32)],
                 scratch=[pltpu.VMEM((S + 2 * CONV_PAD, LANES), F32), pltpu.VMEM((S + 2 * CONV_PAD, LANES), F32),
                          pltpu.VMEM((32, 8, LANES), F32), pltpu.VMEM((S, LANES), BF16)],
                 sem=("parallel", "arbitrary"), aliases={4: 0})(dc1, z, z, w32, dz)


_INV_SQRT2 = 1.0 / math.sqrt(2.0)
_INV_SQRT2PI = 1.0 / math.sqrt(2.0 * math.pi)


def _gelu(x):
    return 0.5 * x * (1.0 + lax.erf(x * _INV_SQRT2))


def _gelu_grad(x):
    return 0.5 * (1.0 + lax.erf(x * _INV_SQRT2)) + x * jnp.exp(-0.5 * x * x) * _INV_SQRT2PI


def sgu_fwd(z, g, b, ws, bs, mix):
    S = z.shape[0]
    T = min(512, S)

    def body(u_ref, v_ref, g_ref, b_ref, ws_ref, bs_ref, mix_ref, o_ref):
        xh, _ = _ln_stats(_gelu(v_ref[...].astype(F32)))
        vn = (xh * g_ref[...] + b_ref[...]).astype(BF16)
        for ch in range(T // CHUNK):
            rows = slice(ch * CHUNK, (ch + 1) * CHUNK)
            for h in range(SGU_HEADS):
                cols = slice(h * HEAD_DIM, (h + 1) * HEAD_DIM)
                sp = jnp.dot(ws_ref[h], vn[rows, cols], preferred_element_type=F32) + bs_ref[h]
                o_ref[rows, cols] = (_gelu(u_ref[rows, cols].astype(F32)) * sp).astype(BF16)

    vec = pl.BlockSpec((1, SGU_WIDTH), lambda i: (0, 0))
    full = pl.BlockSpec((SGU_HEADS, CHUNK, CHUNK), lambda i: (0, 0, 0))
    return _call("sgu_fwd", body, grid=(S // T,),
                 in_specs=[pl.BlockSpec((T, SGU_WIDTH), lambda i: (i, OFF_U // SGU_WIDTH)),
                           pl.BlockSpec((T, SGU_WIDTH), lambda i: (i, OFF_VV // SGU_WIDTH)), vec, vec, full, full,
                           pl.BlockSpec(memory_space=pl.ANY)],
                 out_specs=pl.BlockSpec((T, SGU_WIDTH), lambda i: (i, (ATTN_WIDTH + CONV_WIDTH) // SGU_WIDTH)),
                 out_shape=jax.ShapeDtypeStruct(mix.shape, BF16), sem=("parallel",), aliases={6: 0})(z, z, g, b, ws, bs, mix)


def sgu_bwd(z, dmix, g, b, ws, bs, dz):
    S = z.shape[0]
    T = min(512, S)

    def body(u_ref, v_ref, d_ref, g_ref, b_ref, ws_ref, bs_ref, dz_ref, o_ref, dws_ref, dbs_ref, dg_ref, db_ref, dvn_ref, dv_ref):
        tile = pl.program_id(0)
        first = pl.program_id(1) == 0

        @pl.when(first & (tile == 0))
        def _():
            dws_ref[...] = jnp.zeros_like(dws_ref)
            dbs_ref[...] = jnp.zeros_like(dbs_ref)

        @pl.when(first)
        def _():
            vraw = v_ref[...].astype(F32)
            xh, rs = _ln_stats(_gelu(vraw))
            gv = g_ref[...]
            vn = (xh * gv + b_ref[...]).astype(BF16)
            for ch in range(T // CHUNK):
                rows = slice(ch * CHUNK, (ch + 1) * CHUNK)
                for h in range(SGU_HEADS):
                    cols = slice(h * HEAD_DIM, (h + 1) * HEAD_DIM)
                    w = ws_ref[h]
                    vb = vn[rows, cols]
                    sp = jnp.dot(w, vb, preferred_element_type=F32) + bs_ref[h]
                    uraw = u_ref[rows, cols].astype(F32)
                    dout = d_ref[rows, cols].astype(F32)
                    o_ref[rows, cols] = (dout * sp * _gelu_grad(uraw)).astype(BF16)
                    dsp = dout * _gelu(uraw)
                    dspb = dsp.astype(BF16)
                    dvn_ref[rows, cols] = lax.dot_general(w, dspb, TN, preferred_element_type=F32)
                    dws_ref[h] += lax.dot_general(dspb, vb, NT, preferred_element_type=F32)
                    dbs_ref[h] += jnp.sum(dsp, axis=1, keepdims=True)
            dvn = dvn_ref[...]
            dv_ref[...] = (_ln_bwd(dvn, xh, rs, gv) * _gelu_grad(vraw)).astype(BF16)
            _acc_out(dg_ref, jnp.sum(dvn * xh, axis=0, keepdims=True), tile)
            _acc_out(db_ref, jnp.sum(dvn, axis=0, keepdims=True), tile)

        @pl.when(pl.program_id(1) == 1)
        def _():
            o_ref[...] = dv_ref[...]

    vec = pl.BlockSpec((1, SGU_WIDTH), lambda i, j: (0, 0))
    full = pl.BlockSpec((SGU_HEADS, CHUNK, CHUNK), lambda i, j: (0, 0, 0))
    vshape = jax.ShapeDtypeStruct((1, SGU_WIDTH), F32)
    return _call("sgu_bwd", body, grid=(S // T, 2),
                 in_specs=[pl.BlockSpec((T, SGU_WIDTH), lambda i, j: (i, OFF_U // SGU_WIDTH)),
                           pl.BlockSpec((T, SGU_WIDTH), lambda i, j: (i, OFF_VV // SGU_WIDTH)),
                           pl.BlockSpec((T, SGU_WIDTH), lambda i, j: (i, (ATTN_WIDTH + CONV_WIDTH) // SGU_WIDTH)), vec, vec, full, full,
                           pl.BlockSpec(memory_space=pl.ANY)],
                 out_specs=[pl.BlockSpec((T, SGU_WIDTH), lambda i, j: (i, OFF_U // SGU_WIDTH + j)), full,
                            pl.BlockSpec((SGU_HEADS, CHUNK, 1), lambda i, j: (0, 0, 0)), vec, vec],
                 out_shape=[jax.ShapeDtypeStruct(dz.shape, BF16), jax.ShapeDtypeStruct((SGU_HEADS, CHUNK, CHUNK), F32),
                            jax.ShapeDtypeStruct((SGU_HEADS, CHUNK, 1), F32), vshape, vshape],
                 scratch=[pltpu.VMEM((T, SGU_WIDTH), F32), pltpu.VMEM((T, SGU_WIDTH), BF16)],
                 sem=("arbitrary", "arbitrary"), aliases={7: 0})(z, z, dmix, g, b, ws, bs, dz)


def _row_tile(rows, cols, n_arrays, budget_mib=24):
    budget = (budget_mib * 1024 * 1024) // (n_arrays * 2 * 4 * cols)
    t = min(rows, max(16, budget // 16 * 16))
    while rows % t:
        t -= 16
    return t


def add_sibling_half(grad, recv, c_idx):
    J, R, C = grad.shape
    hr = R // 2
    tr = _row_tile(hr, C, 3)
    nb = hr // tr

    def body(c_ref, g_ref, r_ref, o_ref):
        o_ref[...] = (g_ref[...].astype(F32) + r_ref[...].astype(F32)).astype(BF16)

    grid_spec = pltpu.PrefetchScalarGridSpec(
        num_scalar_prefetch=1, grid=(J, nb),
        in_specs=[pl.BlockSpec((None, tr, C), lambda j, i, c: (j, c[0] * nb + i, 0)),
                  pl.BlockSpec((None, tr, C), lambda j, i, c: (j, i, 0))],
        out_specs=pl.BlockSpec((None, tr, C), lambda j, i, c: (j, i, 0)))
    return pl.pallas_call(body, name="add_sibling_half", grid_spec=grid_spec,
                          out_shape=jax.ShapeDtypeStruct((J, hr, C), BF16),
                          compiler_params=pltpu.CompilerParams(vmem_limit_bytes=VMEM_LIMIT,
                                                               dimension_semantics=("parallel", "parallel")))(c_idx, grad, recv)


def sum_chips(own, others, stack, x_idx, y_idx, layer):
    R, C = own.shape[1:]
    tr = _row_tile(R, C, 4)

    def body(x_ref, y_ref, own_ref, oth_ref, stack_ref, o_ref):
        acc = own_ref[...].astype(F32)
        for j in range(3):
            acc = acc + oth_ref[j].astype(F32)
        o_ref[...] = acc

    grid_spec = pltpu.PrefetchScalarGridSpec(
        num_scalar_prefetch=2, grid=(R // tr,),
        in_specs=[pl.BlockSpec((None, tr, C), lambda i, xr, yr: (2 * xr[0] + yr[0], i, 0)),
                  pl.BlockSpec((3, tr, C), lambda i, xr, yr: (0, i, 0)),
                  pl.BlockSpec(memory_space=pl.ANY)],
        out_specs=pl.BlockSpec((None, tr, C), lambda i, xr, yr: (layer, i, 0)))
    return pl.pallas_call(body, name="sum_chips", grid_spec=grid_spec,
                          out_shape=jax.ShapeDtypeStruct(stack.shape, F32), input_output_aliases={4: 0},
                          compiler_params=pltpu.CompilerParams(vmem_limit_bytes=VMEM_LIMIT,
                                                               dimension_semantics=("parallel",)))(x_idx, y_idx, own, others, stack)


def adamw_halves(w, mine, theirs, m, v, c_idx):
    L, R, C = w.shape
    hr = R // 2
    tr = _row_tile(hr, C, 9, budget_mib=40)
    nb = hr // tr

    def body(c_ref, w_ref, a_ref, b_ref, m_ref, v_ref, g_ref, d_ref, nm_ref, nv_ref):
        gv = jnp.where(pl.program_id(1) == c_ref[0], a_ref[...], b_ref[...])
        g_ref[...] = gv
        nm = ADAM_B1 * m_ref[...] + (1.0 - ADAM_B1) * gv
        nv = ADAM_B2 * v_ref[...] + (1.0 - ADAM_B2) * (gv * gv)
        m_hat = nm / (1.0 - ADAM_B1 ** ADAM_STEP)
        v_hat = nv / (1.0 - ADAM_B2 ** ADAM_STEP)
        d_ref[...] = -ADAM_LR * (m_hat / (jnp.sqrt(v_hat) + ADAM_EPS) + ADAM_WD * w_ref[...])
        nm_ref[...] = nm
        nv_ref[...] = nv

    full = pl.BlockSpec((None, tr, C), lambda l, h, i, c: (l, h * nb + i, 0))
    a_spec = pl.BlockSpec((None, tr, C), lambda l, h, i, c: (l, jnp.where(h == c[0], i, 0), 0))
    b_spec = pl.BlockSpec((None, tr, C), lambda l, h, i, c: (l, jnp.where(h == c[0], 0, i), 0))
    grid_spec = pltpu.PrefetchScalarGridSpec(num_scalar_prefetch=1, grid=(L, 2, nb),
                                             in_specs=[full, a_spec, b_spec, full, full], out_specs=[full] * 4)
    shape = jax.ShapeDtypeStruct((L, R, C), F32)
    return pl.pallas_call(body, name="adamw_halves", grid_spec=grid_spec, out_shape=[shape] * 4,
                          compiler_params=pltpu.CompilerParams(vmem_limit_bytes=VMEM_LIMIT,
                                                               dimension_semantics=("parallel", "arbitrary", "arbitrary")))(
        c_idx, w, mine, theirs, m, v)


def adamw(w, g, m, v):
    R, C = w.shape
    tr = _row_tile(R, C, 7)

    def body(w_ref, g_ref, m_ref, v_ref, d_ref, nm_ref, nv_ref):
        gv = g_ref[...]
        nm = ADAM_B1 * m_ref[...] + (1.0 - ADAM_B1) * gv
        nv = ADAM_B2 * v_ref[...] + (1.0 - ADAM_B2) * (gv * gv)
        m_hat = nm / (1.0 - ADAM_B1 ** ADAM_STEP)
        v_hat = nv / (1.0 - ADAM_B2 ** ADAM_STEP)
        d_ref[...] = -ADAM_LR * (m_hat / (jnp.sqrt(v_hat) + ADAM_EPS) + ADAM_WD * w_ref[...])
        nm_ref[...] = nm
        nv_ref[...] = nv

    spec = pl.BlockSpec((tr, C), lambda i: (i, 0))
    shape = jax.ShapeDtypeStruct((R, C), F32)
    return _call("adamw", body, grid=(R // tr,), in_specs=[spec] * 4, out_specs=[spec] * 3, out_shape=[shape] * 3,
                 sem=("parallel",))(w, g, m, v)


def _place():
    x, y, c = lax.axis_index("x"), lax.axis_index("y"), lax.axis_index("c")
    chips = [(1 - x, y), (x, 1 - y), (1 - x, 1 - y)]
    return x, y, c, chips


def _remote(src, dst, send_sem, recv_sem, dev):
    return pltpu.make_async_remote_copy(src_ref=src, dst_ref=dst, send_sem=send_sem, recv_sem=recv_sem,
                                        device_id=dev, device_id_type=MESH)


EFFECT = pltpu.SideEffectType.DATAFLOW_SIDE_EFFECTING
SEM = pl.BlockSpec(memory_space=pltpu.SEMAPHORE)
ANY = pl.BlockSpec(memory_space=pl.ANY)
TOKEN = jax.ShapeDtypeStruct((8, LANES), F32)


def _in_hbm(a):
    return pltpu.with_memory_space_constraint(a, pltpu.HBM)


def _gather_copies(shards, lands, send_sems, recv_sems):
    x, y, c, chips = _place()
    me = 2 * x + y
    copies = []
    for k in range(len(shards)):
        hr = shards[k].shape[0] // 2
        mine = pl.ds(pl.multiple_of(c * hr, 8), hr)
        for t, (px, py) in enumerate(chips):
            copies.append(_remote(shards[k].at[mine, :], lands[k].at[me, mine, :], send_sems.at[4 * k + t], recv_sems.at[4 * k + t],
                                  (px, py, c)))
        copies.append(_remote(shards[k], lands[k].at[me], send_sems.at[4 * k + 3], recv_sems.at[4 * k + 3], (x, y, 1 - c)))
    return copies


def _gather_landings(lands, send_sems, recv_sems):
    x, y, c, chips = _place()
    me = 2 * x + y
    landings = []
    for k in range(len(lands)):
        hr = lands[k].shape[1] // 2
        mine = pl.ds(pl.multiple_of(c * hr, 8), hr)
        for t, (px, py) in enumerate(chips):
            dst = lands[k].at[2 * px + py, mine, :]
            landings.append(_remote(dst, dst, send_sems.at[4 * k + t], recv_sems.at[4 * k + t], (px, py, c)))
        dst = lands[k].at[me]
        landings.append(_remote(dst, dst, send_sems.at[4 * k + 3], recv_sems.at[4 * k + 3], (x, y, 1 - c)))
    return landings


def gather_start(shards, after):
    n = len(shards)

    def body(*refs):
        srcs, lands_in = refs[:n], refs[n:2 * n]
        send_sems, recv_sems = refs[2 * n + 1], refs[2 * n + 2]
        token = refs[-1]
        for cp in _gather_copies(srcs, lands_in, send_sems, recv_sems):
            cp.start()
        token[...] = jnp.zeros_like(token)

    lands = [lax.empty((N_CHIPS,) + s.shape, s.dtype) for s in shards]
    outs = pl.pallas_call(
        body, name="gather_start", in_specs=[HBM] * (2 * n) + [ANY],
        out_specs=[SEM, SEM] + [HBM] * (2 * n) + [VMEM_SPEC],
        out_shape=[pltpu.SemaphoreType.DMA((4 * n,)), pltpu.SemaphoreType.DMA((4 * n,))]
        + [pltpu.HBM(s.shape, s.dtype) for s in shards] + [pltpu.HBM(l.shape, l.dtype) for l in lands] + [TOKEN],
        input_output_aliases={i: 2 + i for i in range(2 * n)},
        compiler_params=pltpu.CompilerParams(has_side_effects=EFFECT),
    )(*[_in_hbm(s) for s in shards], *[_in_hbm(l) for l in lands], after)
    return outs[0], outs[1], outs[2:2 + n], outs[2 + n:2 + 2 * n], outs[-1]


def gather_wait(send_sems, recv_sems, shards, lands, after):
    n = len(shards)

    def body(*refs):
        srcs, lands_in = refs[:n], refs[n:2 * n]
        send, recv = refs[2 * n], refs[2 * n + 1]
        for cp in _gather_copies(srcs, lands_in, send, recv):
            cp.wait_send()
        for cp in _gather_landings(lands_in, send, recv):
            cp.wait_recv()

    outs = pl.pallas_call(
        body, name="gather_wait", in_specs=[HBM] * (2 * n) + [SEM, SEM, ANY], out_specs=[HBM] * (2 * n),
        out_shape=[pltpu.HBM(s.shape, s.dtype) for s in shards] + [pltpu.HBM(l.shape, l.dtype) for l in lands],
        input_output_aliases={i: i for i in range(2 * n)},
        compiler_params=pltpu.CompilerParams(has_side_effects=EFFECT),
    )(*shards, *lands, send_sems, recv_sems, after)
    return outs[n:]


def _forward_copies(lands, send_sems, recv_sems, received):
    x, y, c, chips = _place()
    copies = []
    for k in range(len(lands)):
        hr = lands[k].shape[1] // 2
        half = (1 - c) if received else c
        rows = pl.ds(pl.multiple_of(half * hr, 8), hr)
        for t, (px, py) in enumerate(chips):
            block = lands[k].at[2 * px + py, rows, :]
            copies.append(_remote(block, block, send_sems.at[3 * k + t], recv_sems.at[3 * k + t], (x, y, 1 - c)))
    return copies


def forward_start(lands):
    n = len(lands)

    def body(*refs):
        for cp in _forward_copies(refs[:n], refs[n], refs[n + 1], received=False):
            cp.start()
        refs[-1][...] = jnp.zeros_like(refs[-1])

    outs = pl.pallas_call(
        body, name="forward_start", in_specs=[HBM] * n, out_specs=[SEM, SEM] + [HBM] * n + [VMEM_SPEC],
        out_shape=[pltpu.SemaphoreType.DMA((3 * n,)), pltpu.SemaphoreType.DMA((3 * n,))]
        + [pltpu.HBM(l.shape, l.dtype) for l in lands] + [TOKEN],
        input_output_aliases={i: 2 + i for i in range(n)},
        compiler_params=pltpu.CompilerParams(has_side_effects=EFFECT),
    )(*[_in_hbm(l) for l in lands])
    return outs[0], outs[1], outs[2:2 + n], outs[-1]


def forward_wait(send_sems, recv_sems, lands, after):
    n = len(lands)

    def body(*refs):
        for cp in _forward_copies(refs[:n], refs[n], refs[n + 1], received=False):
            cp.wait_send()
        for cp in _forward_copies(refs[:n], refs[n], refs[n + 1], received=True):
            cp.wait_recv()

    return pl.pallas_call(
        body, name="forward_wait", in_specs=[HBM] * n + [SEM, SEM, ANY], out_specs=[HBM] * n,
        out_shape=[pltpu.HBM(l.shape, l.dtype) for l in lands],
        input_output_aliases={i: i for i in range(n)},
        compiler_params=pltpu.CompilerParams(has_side_effects=EFFECT),
    )(*lands, send_sems, recv_sems, after)


def forward_halves(lands):
    n = len(lands)

    def body(*refs):
        ins, outs = refs[:n], refs[n:2 * n]
        send_sems, recv_sems = refs[2 * n:]
        x, y, c, chips = _place()
        sibling = (x, y, 1 - c)
        sends = []
        for k in range(n):
            hr = ins[k].shape[1] // 2
            mine = pl.ds(pl.multiple_of(c * hr, 8), hr)
            for t, (px, py) in enumerate(chips):
                cp = _remote(ins[k].at[2 * px + py, mine, :], outs[k].at[2 * px + py, mine, :],
                             send_sems.at[k, t], recv_sems.at[k, t], sibling)
                cp.start()
                sends.append(cp)
        for k in range(n):
            hr = ins[k].shape[1] // 2
            other = pl.ds(pl.multiple_of((1 - c) * hr, 8), hr)
            for t, (px, py) in enumerate(chips):
                dst = outs[k].at[2 * px + py, other, :]
                _remote(dst, dst, send_sems.at[k, t], recv_sems.at[k, t], sibling).wait_recv()
        for cp in sends:
            cp.wait_send()

    return pl.pallas_call(
        body, name="forward_halves", in_specs=[HBM] * n, out_specs=[HBM] * n,
        out_shape=[jax.ShapeDtypeStruct(l.shape, l.dtype) for l in lands],
        input_output_aliases={i: i for i in range(n)},
        scratch_shapes=[pltpu.SemaphoreType.DMA((n, 3)), pltpu.SemaphoreType.DMA((n, 3))],
    )(*lands)


def gather_small(block):
    def body(in_ref, out_ref, send_sems, recv_sems):
        x, y, c, chips = _place()
        me = 2 * x + y
        out_ref[me] = in_ref[...]
        sends = []
        for t, (px, py) in enumerate(chips):
            cp = _remote(in_ref, out_ref.at[me], send_sems.at[t], recv_sems.at[t], (px, py, c))
            cp.start()
            sends.append(cp)
        for t, (px, py) in enumerate(chips):
            landed = out_ref.at[2 * px + py]
            _remote(landed, landed, send_sems.at[t], recv_sems.at[t], (px, py, c)).wait_recv()
        for cp in sends:
            cp.wait_send()

    return pl.pallas_call(
        body, name="gather_small", in_specs=[VMEM_SPEC], out_specs=VMEM_SPEC,
        out_shape=jax.ShapeDtypeStruct((N_CHIPS,) + block.shape, block.dtype),
        scratch_shapes=[pltpu.SemaphoreType.DMA((3,)), pltpu.SemaphoreType.DMA((3,))],
    )(block)


def exchange_sibling_halves(grads):
    n = len(grads)

    def body(*refs):
        ins, outs = refs[:n], refs[n:2 * n]
        send_sems, recv_sems = refs[2 * n:]
        x, y, c, _ = _place()
        copies = []
        for k in range(n):
            hr = ins[k].shape[1] // 2
            theirs = pl.ds(pl.multiple_of((1 - c) * hr, 8), hr)
            cp = _remote(ins[k].at[:, theirs, :], outs[k], send_sems.at[k], recv_sems.at[k], (x, y, 1 - c))
            cp.start()
            copies.append(cp)
        for cp in copies:
            cp.wait()

    return pl.pallas_call(
        body, name="exchange_sibling_halves", in_specs=[HBM] * n, out_specs=[HBM] * n,
        out_shape=[jax.ShapeDtypeStruct((g.shape[0], g.shape[1] // 2, g.shape[2]), g.dtype) for g in grads],
        scratch_shapes=[pltpu.SemaphoreType.DMA((n,)), pltpu.SemaphoreType.DMA((n,))],
    )(*grads)


def _sibling_half_copies(grads, lands, send_sems, recv_sems):
    x, y, c, _ = _place()
    copies = []
    for k in range(len(grads)):
        hr = grads[k].shape[1] // 2
        theirs = pl.ds(pl.multiple_of((1 - c) * hr, 8), hr)
        copies.append(_remote(grads[k].at[:, theirs, :], lands[k], send_sems.at[k], recv_sems.at[k], (x, y, 1 - c)))
    return copies


def _sibling_whole_copies(srcs, lands, send_sems, recv_sems):
    x, y, c, _ = _place()
    return [_remote(srcs[k], lands[k], send_sems.at[k], recv_sems.at[k], (x, y, 1 - c)) for k in range(len(srcs))]


def pair_start(name, make_copies, srcs, land_shapes):
    n = len(srcs)

    def body(*refs):
        src_refs, land_refs = refs[:n], refs[n:2 * n]
        send_sems, recv_sems = refs[2 * n], refs[2 * n + 1]
        token = refs[-1]
        for cp in make_copies(src_refs, land_refs, send_sems, recv_sems):
            cp.start()
        token[...] = jnp.zeros_like(token)

    lands = [lax.empty(shape, s.dtype) for shape, s in zip(land_shapes, srcs)]
    outs = pl.pallas_call(
        body, name=name, in_specs=[HBM] * (2 * n), out_specs=[SEM, SEM] + [HBM] * (2 * n) + [VMEM_SPEC],
        out_shape=[pltpu.SemaphoreType.DMA((n,)), pltpu.SemaphoreType.DMA((n,))]
        + [pltpu.HBM(s.shape, s.dtype) for s in srcs] + [pltpu.HBM(l.shape, l.dtype) for l in lands] + [TOKEN],
        input_output_aliases={i: 2 + i for i in range(2 * n)},
        compiler_params=pltpu.CompilerParams(has_side_effects=EFFECT),
    )(*[_in_hbm(s) for s in srcs], *[_in_hbm(l) for l in lands])
    return outs[0], outs[1], outs[2:2 + n], outs[2 + n:2 + 2 * n], outs[-1]


def pair_wait_one(name, send_sems, recv_sems, src, land, after, index):
    def body(src_ref, land_ref, send, recv, after_ref, src_out, land_out):
        x, y, c, _ = _place()
        cp = _remote(src_ref, land_ref, send.at[index], recv.at[index], (x, y, 1 - c))
        cp.wait_send()
        cp.wait_recv()

    return pl.pallas_call(
        body, name=name, in_specs=[HBM, HBM, SEM, SEM, ANY], out_specs=[HBM, HBM],
        out_shape=[pltpu.HBM(src.shape, src.dtype), pltpu.HBM(land.shape, land.dtype)],
        input_output_aliases={0: 0, 1: 1},
        compiler_params=pltpu.CompilerParams(has_side_effects=EFFECT),
    )(src, land, send_sems, recv_sems, after)


def pair_wait(name, make_copies, send_sems, recv_sems, srcs, lands, after):
    n = len(srcs)

    def body(*refs):
        src_refs, land_refs = refs[:n], refs[n:2 * n]
        for cp in make_copies(src_refs, land_refs, refs[2 * n], refs[2 * n + 1]):
            cp.wait_send()
            cp.wait_recv()

    outs = pl.pallas_call(
        body, name=name, in_specs=[HBM] * (2 * n) + [SEM, SEM, ANY], out_specs=[HBM] * (2 * n),
        out_shape=[pltpu.HBM(s.shape, s.dtype) for s in srcs] + [pltpu.HBM(l.shape, l.dtype) for l in lands],
        input_output_aliases={i: i for i in range(2 * n)},
        compiler_params=pltpu.CompilerParams(has_side_effects=EFFECT),
    )(*srcs, *lands, send_sems, recv_sems, after)
    return outs[:n], outs[n:]


def _chip_copies(parts, lands, send_sems, recv_sems):
    x, y, c, chips = _place()
    return [_remote(parts[k].at[2 * px + py], lands[k].at[t], send_sems.at[3 * k + t], recv_sems.at[3 * k + t], (px, py, c))
            for k in range(len(parts)) for t, (px, py) in enumerate(chips)]


def chip_parts_start(parts):
    n = len(parts)

    def body(*refs):
        srcs, lands_in = refs[:n], refs[n:2 * n]
        send_sems, recv_sems = refs[2 * n], refs[2 * n + 1]
        token = refs[-1]
        for cp in _chip_copies(srcs, lands_in, send_sems, recv_sems):
            cp.start()
        token[...] = jnp.zeros_like(token)

    lands = [lax.empty((3,) + p.shape[1:], p.dtype) for p in parts]
    outs = pl.pallas_call(
        body, name="chip_parts_start", in_specs=[HBM] * (2 * n), out_specs=[SEM, SEM] + [HBM] * (2 * n) + [VMEM_SPEC],
        out_shape=[pltpu.SemaphoreType.DMA((3 * n,)), pltpu.SemaphoreType.DMA((3 * n,))]
        + [pltpu.HBM(p.shape, p.dtype) for p in parts] + [pltpu.HBM(l.shape, l.dtype) for l in lands] + [TOKEN],
        input_output_aliases={i: 2 + i for i in range(2 * n)},
        compiler_params=pltpu.CompilerParams(has_side_effects=EFFECT),
    )(*[_in_hbm(p) for p in parts], *[_in_hbm(l) for l in lands])
    return outs[0], outs[1], outs[2:2 + n], outs[2 + n:2 + 2 * n], outs[-1]


def chip_parts_wait(send_sems, recv_sems, parts, lands, after):
    n = len(parts)

    def body(*refs):
        srcs, lands_in = refs[:n], refs[n:2 * n]
        send, recv = refs[2 * n], refs[2 * n + 1]
        for cp in _chip_copies(srcs, lands_in, send, recv):
            cp.wait_send()
            cp.wait_recv()

    outs = pl.pallas_call(
        body, name="chip_parts_wait", in_specs=[HBM] * (2 * n) + [SEM, SEM, ANY], out_specs=[HBM] * (2 * n),
        out_shape=[pltpu.HBM(p.shape, p.dtype) for p in parts] + [pltpu.HBM(l.shape, l.dtype) for l in lands],
        input_output_aliases={i: i for i in range(2 * n)},
        compiler_params=pltpu.CompilerParams(has_side_effects=EFFECT),
    )(*parts, *lands, send_sems, recv_sems, after)
    return outs[:n], outs[n:]


def allreduce_small(packed):
    R = packed.shape[0]

    def body(x_ref, sum_ref, all_ref, send_sems, recv_sems):
        x, y, c, chips = _place()
        me, sibling = (x, y, c), (x, y, 1 - c)

        def rows(px, py, pc):
            return all_ref.at[4 * px + 2 * py + pc]

        def copy(k, block, to, src=None):
            return _remote(rows(*block) if src is None else src, rows(*block), send_sems.at[k], recv_sems.at[k], to)

        all_ref[4 * x + 2 * y + c] = x_ref[...]
        first = [copy(0, me, sibling, src=x_ref)]
        first += [copy(1 + j, me, (*chip, c), src=x_ref) for j, chip in enumerate(chips)]
        for cp in first:
            cp.start()
        passed = [copy(4 + j, (*chip, c), sibling) for j, chip in enumerate(chips)]
        for j, chip in enumerate(chips):
            copy(1 + j, (*chip, c), me).wait_recv()
            passed[j].start()
        copy(0, sibling, me).wait_recv()
        for j, chip in enumerate(chips):
            copy(4 + j, (*chip, 1 - c), me).wait_recv()
        for cp in first + passed:
            cp.wait_send()

        def chunk(i, carry):
            rws = pl.ds(pl.multiple_of(i * PACK_ROWS, PACK_ROWS), PACK_ROWS)
            acc = all_ref[0, rws, :]
            for d in range(1, N_DEV):
                acc = acc + all_ref[d, rws, :]
            sum_ref[rws, :] = acc
            return carry

        lax.fori_loop(0, R // PACK_ROWS, chunk, 0)

    return pl.pallas_call(
        body, name="allreduce_small", in_specs=[VMEM_SPEC], out_specs=VMEM_SPEC,
        out_shape=jax.ShapeDtypeStruct((R, LANES), F32),
        scratch_shapes=[pltpu.VMEM((N_DEV, R, LANES), F32), pltpu.SemaphoreType.DMA((7,)), pltpu.SemaphoreType.DMA((7,))],
        compiler_params=pltpu.CompilerParams(vmem_limit_bytes=VMEM_LIMIT),
    )(packed)


def _mixer_fwd(x, h, p, tabs, token, late_weights=None):
    z = mm_nn_cols(h, p["w_in"], token)
    qr, kp, vp = rope_fwd(z, tabs)
    mix = attn_fwd(qr, kp, vp, p["sink3"])
    c1 = conv_dw_fwd(z, p["conv_w32"], p["conv_dw_b"])
    mix = conv_ln_fwd(c1, p["conv_ln_g"], p["conv_ln_b"], mix)
    mix = sgu_fwd(z, p["sgu_ln_g"], p["sgu_ln_b"], p["sgu_w16"], p["sgu_b3"], mix)
    if late_weights is not None:
        p.update(late_weights(mix))
    x_mid, h2 = mm_nn_rows_res(mix, p["w_out"], x, p["ffn_norm_g"])
    return x_mid, h2, dict(x=x, h=h, z=z, qr=qr, kp=kp, vp=vp, c1=c1, mix=mix, x_mid=x_mid)


def _ffn_fwd(x_mid, h2, p, next_gain, token):
    gate, up, act = ffn_up(h2, p["w_gate"], p["w_up"], token)
    x_out, h_next = mm_nn_rows_res(act, p["w_down"], x_mid, next_gain)
    return x_out, h_next, dict(h2=h2, gate=gate, up=up, act=act)


def _layer_fwd(x, h, p, next_gain, tabs, token):
    x_mid, h2, s_mix = _mixer_fwd(x, h, p, tabs, token)
    x_out, h_next, s_ffn = _ffn_fwd(x_mid, h2, p, next_gain, token)
    return x_out, h_next, {**s_mix, **s_ffn}


def _ffn_bwd(dxb, p, s, token):
    dgate, dup = ffn_down_bwd(dxb, p["w_down"], s["gate"], s["up"], token)
    g_down = mm_tn_rows(s["act"], dxb)
    dh2 = mm_nt_cols([(dgate, p["w_gate"]), (dup, p["w_up"])], BF16, 1)
    g_gate = mm_tn_cols(s["h2"], dgate, N_CHIPS)
    g_up = mm_tn_cols(s["h2"], dup, N_CHIPS)
    dmidb, g_ffn_norm = rms_bwd(s["x_mid"], p["ffn_norm_g"], dh2, dxb, BF16)
    return dmidb, [g_gate, g_up, g_down.reshape(N_CHIPS, -1, D_MODEL)], g_ffn_norm


def _mixer_bwd(dmidb, p, s, tabs, token, out_dtype):
    dmix = mm_nt_rows(dmidb, p["w_out"], token)
    g_out = mm_tn_rows(s["mix"], dmidb)
    dq, dkp, dvp, dsink = attn_bwd(s["qr"], s["kp"], s["vp"], p["sink3"], dmix)
    dz = rope_bwd(dq, dkp, dvp, tabs)
    dc1, g_cln_g, g_cln_b = conv_ln_bwd(dmix, s["c1"], p["conv_ln_g"], p["conv_ln_b"])
    dz, g_cw, g_cb = conv_dw_bwd(dc1, s["z"], p["conv_w32"], dz)
    dz, g_sw, g_sb, g_sln_g, g_sln_b = sgu_bwd(s["z"], dmix, p["sgu_ln_g"], p["sgu_ln_b"], p["sgu_w16"], p["sgu_b3"], dz)
    dh = mm_nt_cols([(dz, p["w_in"])], BF16, N_CHIPS)
    g_in = mm_tn_cols(s["h"], dz, N_CHIPS)
    dx_in, g_mix_norm = rms_bwd(s["x"], p["mix_norm_g"], dh, dmidb, out_dtype)
    small = dict(mix_norm_g=g_mix_norm, sink=dsink[:, :, 0].reshape(1, N_Q_HEADS), conv_dw_w=g_cw[:CONV_KERNEL],
                 conv_dw_b=g_cb, conv_ln_g=g_cln_g, conv_ln_b=g_cln_b, sgu_ln_g=g_sln_g, sgu_ln_b=g_sln_b,
                 sgu_w=g_sw, sgu_b=g_sb[:, :, 0])
    return dx_in, [g_in, g_out.reshape(N_CHIPS, -1, D_MODEL)], small


def _layer_bwd(dxb, p, s, tabs, token, out_dtype):
    dmidb, ffn_big, g_ffn_norm = _ffn_bwd(dxb, p, s, token)
    dx_in, mix_big, small = _mixer_bwd(dmidb, p, s, tabs, token, out_dtype)
    return dx_in, mix_big + ffn_big, dict(small, ffn_norm_g=g_ffn_norm)


def _mixer_weights(gathered):
    w_in, w_out = gathered
    return dict(w_in=w_in, w_out=w_out.reshape(-1, D_MODEL))


def _ffn_weights(gathered):
    w_gate, w_up, w_down = gathered
    return dict(w_gate=w_gate, w_up=w_up, w_down=w_down.reshape(-1, D_MODEL))


def _small_params(l, conv_w_full, mix_norm_g, sink, conv_dw_b, conv_ln_g, conv_ln_b, sgu_ln_g, sgu_ln_b, sgu_w, sgu_b,
                  ffn_norm_g):
    return dict(
        mix_norm_g=mix_norm_g[l:l + 1], ffn_norm_g=ffn_norm_g[l:l + 1],
        sink3=jnp.broadcast_to(sink[l].reshape(N_KV_HEADS, Q_PER_KV, 1), (N_KV_HEADS, Q_PER_KV, LANES)),
        conv_w32=jnp.pad(conv_w_full[l], ((0, 32 - CONV_KERNEL), (0, 0))),
        conv_dw_b=conv_dw_b[l:l + 1], conv_ln_g=conv_ln_g[l:l + 1], conv_ln_b=conv_ln_b[l:l + 1],
        sgu_ln_g=sgu_ln_g[l:l + 1], sgu_ln_b=sgu_ln_b[l:l + 1], sgu_w16=sgu_w[l].astype(BF16),
        sgu_b3=jnp.broadcast_to(sgu_b[l][:, :, None], (SGU_HEADS, CHUNK, CHUNK)))


_SMALL = ["mix_norm_g", "sink", "conv_dw_b", "conv_ln_g", "conv_ln_b", "sgu_ln_g", "sgu_ln_b", "sgu_w", "sgu_b", "ffn_norm_g",
          "final_norm_g"]


def _pack_rows(arrays):
    rows, counts = [], []
    for a in arrays:
        flat = a.reshape(-1)
        n = -(-flat.shape[0] // LANES)
        rows.append(jnp.pad(flat, (0, n * LANES - flat.shape[0])).reshape(n, LANES))
        counts.append(n)
    packed = jnp.concatenate(rows, axis=0)
    pad = -packed.shape[0] % PACK_ROWS
    return jnp.pad(packed, ((0, pad), (0, 0))), counts


def _unpack_rows(packed, counts, shapes):
    out, r = [], 0
    for n, shape in zip(counts, shapes):
        size = math.prod(shape)
        out.append(packed[r:r + n].reshape(-1)[:size].reshape(shape))
        r += n
    return out


def kernel(x, mix_norm_g, w_in, sink, conv_dw_w, conv_dw_b, conv_ln_g, conv_ln_b, sgu_ln_g, sgu_ln_b, sgu_w, sgu_b, w_out, ffn_norm_g, w_gate, w_up, w_down, final_norm_g, loss_target, m_mix_norm_g, m_w_in, m_sink, m_conv_dw_w, m_conv_dw_b, m_conv_ln_g, m_conv_ln_b, m_sgu_ln_g, m_sgu_ln_b, m_sgu_w, m_sgu_b, m_w_out, m_ffn_norm_g, m_w_gate, m_w_up, m_w_down, m_final_norm_g, v_mix_norm_g, v_w_in, v_sink, v_conv_dw_w, v_conv_dw_b, v_conv_ln_g, v_conv_ln_b, v_sgu_ln_g, v_sgu_ln_b, v_sgu_w, v_sgu_b, v_w_out, v_ffn_norm_g, v_w_gate, v_w_up, v_w_down, v_final_norm_g):
    S = x.shape[1]
    my_chip = 2 * lax.axis_index("x") + lax.axis_index("y")
    c_idx = lax.axis_index("c").astype(jnp.int32).reshape(1)
    big_w = [w_in, w_out, w_gate, w_up, w_down]
    big_m = [m_w_in, m_w_out, m_w_gate, m_w_up, m_w_down]
    big_v = [v_w_in, v_w_out, v_w_gate, v_w_up, v_w_down]
    n_kinds = len(big_w)

    x_idx = lax.axis_index("x").astype(jnp.int32).reshape(1)
    y_idx = lax.axis_index("y").astype(jnp.int32).reshape(1)
    conv_w_all = gather_small(conv_dw_w)
    conv_w_full = jnp.transpose(conv_w_all, (1, 2, 0, 3)).reshape(DEPTH, CONV_KERNEL, CONV_WIDTH)
    tabs = rope_tables(S)
    no_token = jnp.zeros(TOKEN.shape, TOKEN.dtype)

    mixer_kinds, ffn_kinds = [0, 1], [2, 3, 4]
    shards = [[w[l].astype(BF16) for w in big_w] for l in range(DEPTH)]

    def fetch(pending, after):
        send_sems, recv_sems, srcs, lands, _ = pending
        return forward_halves(gather_wait(send_sems, recv_sems, srcs, lands, after))

    first_mixer = gather_start([shards[0][k] for k in mixer_kinds], conv_w_all)
    first_ffn = gather_start([shards[0][k] for k in ffn_kinds], first_mixer[4])
    pending = gather_start(shards[1], first_ffn[4])
    act = x[0]
    h = rms_fwd(act, mix_norm_g[0:1], no_token)
    saved, params = [], []
    for l in range(DEPTH):
        p = _small_params(l, conv_w_full, mix_norm_g, sink, conv_dw_b, conv_ln_g, conv_ln_b, sgu_ln_g, sgu_ln_b, sgu_w, sgu_b,
                          ffn_norm_g)
        next_gain = mix_norm_g[l + 1:l + 2] if l + 1 < DEPTH else final_norm_g.reshape(1, D_MODEL)
        if l == 0:
            p.update(_mixer_weights(fetch(first_mixer, act)))
            x_mid, h2, s_mix = _mixer_fwd(act, h, p, tabs, pending[4])
            p.update(_ffn_weights(fetch(first_ffn, x_mid)))
            late, token = None, no_token
        else:
            send_sems, recv_sems, srcs, lands, _ = pending
            lands = gather_wait(send_sems, recv_sems, srcs, lands, act)
            w_in_full = forward_halves(lands[:1])[0]
            p.update(w_in=w_in_full)
            fwd_send, fwd_recv, rest, token = forward_start(lands[1:])

            def late(mix, fwd_send=fwd_send, fwd_recv=fwd_recv, rest=rest):
                w_out_full, *ffn_full = forward_wait(fwd_send, fwd_recv, rest, mix)
                return dict(_ffn_weights(ffn_full), w_out=w_out_full.reshape(-1, D_MODEL))

            if l + 1 < DEPTH:
                pending = gather_start(shards[l + 1], w_in_full)
                token = token + pending[4]
        if l > 0:
            x_mid, h2, s_mix = _mixer_fwd(act, h, p, tabs, token, late)
        act, h, s_ffn = _ffn_fwd(x_mid, h2, p, next_gain, token)
        params.append(p)
        saved.append({**s_mix, **s_ffn})
    loss_part, dxb, g_final = final_loss(act, final_norm_g.reshape(1, D_MODEL), loss_target[0])
    loss = lax.psum(loss_part[0, 0], ("x", "y", "c"))

    halves = [lax.empty((DEPTH, w.shape[1] // 2, w.shape[2]), F32) for w in big_w]
    small_grads = [None] * DEPTH

    def chip_start(layer, kinds, grads, recv):
        chip_sum = [add_sibling_half(g, r, c_idx) for g, r in zip(grads, recv)]
        send_sems, recv_sems, parts, lands, token = chip_parts_start(chip_sum)
        return (layer, kinds, send_sems, recv_sems, parts, lands), token

    def reduce_start(layer, kinds, grads):
        return chip_start(layer, kinds, grads, exchange_sibling_halves(grads))

    def reduce_finish(pending, halves, after):
        layer, kinds, send_sems, recv_sems, parts, lands = pending
        parts, others = chip_parts_wait(send_sems, recv_sems, parts, lands, after)
        halves = list(halves)
        for i, k in enumerate(kinds):
            halves[k] = sum_chips(parts[i], others[i], halves[k], x_idx, y_idx, layer)
        return halves

    pending, token = None, no_token
    for l in reversed(range(DEPTH)):
        dmidb, ffn_big, g_ffn_norm = _ffn_bwd(dxb, params[l], saved[l], token)
        if l == 0:
            last_ffn, mixer_token = reduce_start(l, ffn_kinds, ffn_big)
        else:
            half_shapes = [(g.shape[0], g.shape[1] // 2, g.shape[2]) for g in ffn_big]
            sib_send, sib_recv, ffn_big, ffn_lands, mixer_token = pair_start("sibling_start", _sibling_half_copies, ffn_big, half_shapes)
        dxb, mix_big, small = _mixer_bwd(dmidb, params[l], saved[l], tabs, mixer_token, F32 if l == 0 else BF16)
        small_grads[l] = dict(small, ffn_norm_g=g_ffn_norm)
        if pending is not None:
            halves = reduce_finish(pending, halves, dxb)
        if l == 0:
            last_mixer, token = reduce_start(l, mixer_kinds, mix_big)
            halves = reduce_finish(last_ffn, halves, token)
        else:
            ffn_big, ffn_recv = pair_wait("sibling_wait", _sibling_half_copies, sib_send, sib_recv, ffn_big, ffn_lands, dxb)
            mix_recv = exchange_sibling_halves(mix_big)
            pending, token = chip_start(l, mixer_kinds + ffn_kinds, list(mix_big) + list(ffn_big), list(mix_recv) + list(ffn_recv))

    def final_start(kinds):
        send_sems, recv_sems, mine, lands, _ = pair_start("final_start", _sibling_whole_copies, [halves[k] for k in kinds],
                                                          [halves[k].shape for k in kinds])
        return send_sems, recv_sems, mine, lands

    ffn_final = final_start(ffn_kinds)

    stacked = {n: jnp.stack([small_grads[l][n] for l in range(DEPTH)]) for n in small_grads[0]}
    stacked["final_norm_g"] = g_final
    packed, counts = _pack_rows([stacked[n] for n in _SMALL] + [stacked["conv_dw_w"]])
    reduced = allreduce_small(packed)
    small_w = dict(mix_norm_g=mix_norm_g, sink=sink, conv_dw_b=conv_dw_b, conv_ln_g=conv_ln_g, conv_ln_b=conv_ln_b,
                   sgu_ln_g=sgu_ln_g, sgu_ln_b=sgu_ln_b, sgu_w=sgu_w, sgu_b=sgu_b, ffn_norm_g=ffn_norm_g,
                   final_norm_g=final_norm_g)
    small_m = dict(mix_norm_g=m_mix_norm_g, sink=m_sink, conv_dw_b=m_conv_dw_b, conv_ln_g=m_conv_ln_g,
                   conv_ln_b=m_conv_ln_b, sgu_ln_g=m_sgu_ln_g, sgu_ln_b=m_sgu_ln_b, sgu_w=m_sgu_w, sgu_b=m_sgu_b,
                   ffn_norm_g=m_ffn_norm_g, final_norm_g=m_final_norm_g)
    small_v = dict(mix_norm_g=v_mix_norm_g, sink=v_sink, conv_dw_b=v_conv_dw_b, conv_ln_g=v_conv_ln_g,
                   conv_ln_b=v_conv_ln_b, sgu_ln_g=v_sgu_ln_g, sgu_ln_b=v_sgu_ln_b, sgu_w=v_sgu_w, sgu_b=v_sgu_b,
                   ffn_norm_g=v_ffn_norm_g, final_norm_g=v_final_norm_g)
    shapes = [small_w[n].shape for n in _SMALL] + [(DEPTH, CONV_KERNEL, CONV_WIDTH)]
    red = _unpack_rows(reduced, counts, shapes)
    g_small = dict(zip(_SMALL, red[:-1]))
    g_small["conv_dw_w"] = lax.dynamic_slice_in_dim(red[-1], my_chip * LANES, LANES, axis=2)
    small_w["conv_dw_w"], small_m["conv_dw_w"], small_v["conv_dw_w"] = conv_dw_w, m_conv_dw_w, v_conv_dw_w
    names = _SMALL + ["conv_dw_w"]
    pw, cnt = _pack_rows([small_w[n] for n in names])
    pg, _ = _pack_rows([g_small[n] for n in names])
    pm, _ = _pack_rows([small_m[n] for n in names])
    pv, _ = _pack_rows([small_v[n] for n in names])
    sd, sm, sv = adamw(pw, pg, pm, pv)
    shp = [small_w[n].shape for n in names]
    d_small = dict(zip(names, _unpack_rows(sd, cnt, shp)))
    m_small = dict(zip(names, _unpack_rows(sm, cnt, shp)))
    v_small = dict(zip(names, _unpack_rows(sv, cnt, shp)))

    big_names = ["w_in", "w_out", "w_gate", "w_up", "w_down"]
    g_big, d_big, m_big, v_big = {}, {}, {}, {}
    after = sd
    for kinds, final in ((ffn_kinds, ffn_final), (mixer_kinds, None)):
        if final is None:
            halves = reduce_finish(last_mixer, halves, after)
            final = final_start(kinds)
        send_sems, recv_sems, sent, lands = final
        for i, k in enumerate(kinds):
            n = big_names[k]
            mine, theirs = pair_wait_one("final_wait", send_sems, recv_sems, sent[i], lands[i], after, i)
            g_big[n], d_big[n], m_big[n], v_big[n] = adamw_halves(big_w[k], mine, theirs, big_m[k], big_v[k], c_idx)
            after = d_big[n]

    order = ["mix_norm_g", "w_in", "sink", "conv_dw_w", "conv_dw_b", "conv_ln_g", "conv_ln_b", "sgu_ln_g", "sgu_ln_b",
             "sgu_w", "sgu_b", "w_out", "ffn_norm_g", "w_gate", "w_up", "w_down", "final_norm_g"]
    grads = {**g_small, **g_big}
    deltas = {**d_small, **d_big}
    new_m = {**m_small, **m_big}
    new_v = {**v_small, **v_big}
    return (loss, dxb[None], *[grads[n] for n in order], *[deltas[n] for n in order],
            *[new_m[n] for n in order], *[new_v[n] for n in order])
```

```python
import math

import jax
import jax.numpy as jnp
from jax import lax
from jax.experimental import pallas as pl
from jax.experimental.pallas import tpu as pltpu

F32, BF16 = jnp.float32, jnp.bfloat16

D_MODEL = 2048
DEPTH = 4
HEAD_DIM = 128
N_Q_HEADS = 8
N_KV_HEADS = 2
Q_PER_KV = N_Q_HEADS // N_KV_HEADS
ATTN_WIDTH = N_Q_HEADS * HEAD_DIM
KV_WIDTH = N_KV_HEADS * HEAD_DIM
CONV_WIDTH = 512
CONV_KERNEL = 31
CONV_PAD = 16
SGU_WIDTH = 512
SGU_HEADS = 4
CHUNK = 128
IN_WIDTH = 3584
D_FF = 5632
WINDOW = 128
ROT_DIM = 32
ROPE_THETA = 500000.0
EPS = 1e-6
N_CHIPS = 4
N_DEV = 8
LANES = 128
PACK_ROWS = 64
OFF_K = ATTN_WIDTH
OFF_V = OFF_K + KV_WIDTH
OFF_CA = OFF_V + KV_WIDTH
OFF_CG = OFF_CA + CONV_WIDTH
OFF_U = OFF_CG + CONV_WIDTH
OFF_VV = OFF_U + SGU_WIDTH

ADAM_LR, ADAM_B1, ADAM_B2, ADAM_EPS, ADAM_WD, ADAM_STEP = 0.001, 0.9, 0.999, 1e-08, 0.01, 10

VMEM_LIMIT = 56 * 1024 * 1024
MESH = pl.DeviceIdType.MESH
HBM = pl.BlockSpec(memory_space=pltpu.HBM)
VMEM_SPEC = pl.BlockSpec(memory_space=pltpu.VMEM)


def _call(name, body, *, grid, in_specs, out_specs, out_shape, scratch=(), sem=None, aliases=None):
    params = dict(vmem_limit_bytes=VMEM_LIMIT)
    if sem is not None:
        params["dimension_semantics"] = sem
    return pl.pallas_call(
        body, name=name, grid=grid, in_specs=in_specs, out_specs=out_specs, out_shape=out_shape,
        scratch_shapes=list(scratch), input_output_aliases=aliases or {}, compiler_params=pltpu.CompilerParams(**params))


def _sigmoid(x):
    return 1.0 / (1.0 + jnp.exp(-x))


def rms_fwd(x, g, token):
    S = x.shape[0]
    tm = min(512, S)

    def body(x_ref, g_ref, token_ref, o_ref):
        xv = x_ref[...]
        r = lax.rsqrt(jnp.mean(xv * xv, axis=-1, keepdims=True) + EPS)
        o_ref[...] = (xv * r * g_ref[...]).astype(BF16)

    return _call("rms_fwd", body, grid=(S // tm,),
                 in_specs=[pl.BlockSpec((tm, D_MODEL), lambda i: (i, 0)), pl.BlockSpec((1, D_MODEL), lambda i: (0, 0)),
                           pl.BlockSpec((8, LANES), lambda i: (0, 0))],
                 out_specs=pl.BlockSpec((tm, D_MODEL), lambda i: (i, 0)),
                 out_shape=jax.ShapeDtypeStruct((S, D_MODEL), BF16), sem=("parallel",))(x, g, token)


def _rms_bwd_math(xv, gv, dh):
    r = lax.rsqrt(jnp.mean(xv * xv, axis=-1, keepdims=True) + EPS)
    n = xv * r
    dn = dh * gv
    dx = r * (dn - n * jnp.mean(dn * n, axis=-1, keepdims=True))
    dg = jnp.sum(dh * n, axis=0, keepdims=True)
    return dx, dg


def rms_bwd(x, g, dh, dres, out_dtype):
    S = x.shape[0]
    tm = min(512, S)

    def body(x_ref, g_ref, dh_ref, dres_ref, dx_ref, dg_ref):
        dx, dg = _rms_bwd_math(x_ref[...], g_ref[...], dh_ref[...].astype(F32))
        dx_ref[...] = (dx + dres_ref[...].astype(F32)).astype(out_dtype)

        @pl.when(pl.program_id(0) == 0)
        def _():
            dg_ref[...] = dg

        @pl.when(pl.program_id(0) > 0)
        def _():
            dg_ref[...] += dg

    row = pl.BlockSpec((tm, D_MODEL), lambda i: (i, 0))
    vec = pl.BlockSpec((1, D_MODEL), lambda i: (0, 0))
    return _call("rms_bwd", body, grid=(S // tm,), in_specs=[row, vec, row, row], out_specs=[row, vec],
                 out_shape=[jax.ShapeDtypeStruct((S, D_MODEL), out_dtype), jax.ShapeDtypeStruct((1, D_MODEL), F32)],
                 sem=("arbitrary",))(x, g, dh, dres)


def final_loss(x, g, target):
    S = x.shape[0]
    tm = min(256, S)

    def body(x_ref, g_ref, t_ref, loss_ref, dxb_ref, dg_ref):
        xv = x_ref[...]
        gv = g_ref[...]
        r = lax.rsqrt(jnp.mean(xv * xv, axis=-1, keepdims=True) + EPS)
        err = xv * r * gv - t_ref[...]
        part = 0.5 * jnp.sum(jnp.mean(err * err, axis=-1, keepdims=True), axis=0, keepdims=True)
        dx, dg = _rms_bwd_math(xv, gv, err * (1.0 / D_MODEL))
        dxb_ref[...] = dx.astype(BF16)

        @pl.when(pl.program_id(0) == 0)
        def _():
            dg_ref[...] = dg
            loss_ref[...] = part

        @pl.when(pl.program_id(0) > 0)
        def _():
            dg_ref[...] += dg
            loss_ref[...] += part

    row = pl.BlockSpec((tm, D_MODEL), lambda i: (i, 0))
    vec = pl.BlockSpec((1, D_MODEL), lambda i: (0, 0))
    one = pl.BlockSpec((1, 1), lambda i: (0, 0))
    return _call("final_loss", body, grid=(S // tm,), in_specs=[row, vec, row], out_specs=[one, row, vec],
                 out_shape=[jax.ShapeDtypeStruct((1, 1), F32), jax.ShapeDtypeStruct((S, D_MODEL), BF16),
                            jax.ShapeDtypeStruct((1, D_MODEL), F32)],
                 sem=("arbitrary",))(x, g, target)


EPILOGUE_ROWS = 256
NN = (((1,), (0,)), ((), ()))
NT = (((1,), (1,)), ((), ()))
TN = (((0,), (0,)), ((), ()))


def _matmul(name, operands, in_specs, out_shape, out_specs, grid, pairs, dims, acc_shape, epilogue):
    n_in, n_out, nk = len(operands), len(out_shape), grid[-1]

    def body(*refs):
        ins, outs = refs[:n_in], refs[n_in:n_in + n_out]
        part = None
        for ia, ib in pairs:
            d = lax.dot_general(ins[ia][...], ins[ib][...], dims, preferred_element_type=F32)
            part = d if part is None else part + d
        if nk == 1:
            epilogue(part, ins, outs)
        else:
            acc = refs[-1]
            k = pl.program_id(len(grid) - 1)

            @pl.when(k == 0)
            def _():
                acc[...] = part

            @pl.when(k > 0)
            def _():
                acc[...] += part

            @pl.when(k == nk - 1)
            def _():
                epilogue(acc[...], ins, outs)

    scratch = [pltpu.VMEM(acc_shape, F32)] if nk > 1 else []
    sem = ("parallel",) * (len(grid) - 1) + ("arbitrary",)
    return _call(name, body, grid=grid, in_specs=in_specs, out_specs=out_specs, out_shape=out_shape,
                 scratch=scratch, sem=sem)(*operands)


def _store(dtype):
    def epilogue(acc, ins, outs):
        outs[0][...] = acc.astype(dtype)
    return epilogue


def mm_nn_cols(a, w, token):
    S, K = a.shape
    J, _, Ns = w.shape
    tm = min(512, S)
    return _matmul("mm_nn_cols", (a, w, token),
                   [pl.BlockSpec((tm, K), lambda j, i, k: (i, 0)), pl.BlockSpec((None, K, Ns), lambda j, i, k: (j, 0, 0)),
                    pl.BlockSpec((8, LANES), lambda j, i, k: (0, 0))],
                   [jax.ShapeDtypeStruct((S, J * Ns), BF16)], [pl.BlockSpec((tm, Ns), lambda j, i, k: (i, j))],
                   (J, S // tm, 1), [(0, 1)], NN, None, _store(BF16))[0]


def ffn_up(h, wg, wu, token):
    S, K = h.shape
    J, _, Ns = wg.shape
    tm = min(512, S)

    sub = min(EPILOGUE_ROWS, tm)

    def body(h_ref, wg_ref, wu_ref, token_ref, g_ref, u_ref, a_ref):
        for r in range(tm // sub):
            rows = slice(r * sub, (r + 1) * sub)
            hv = h_ref[rows, :]
            gv = jnp.dot(hv, wg_ref[...], preferred_element_type=F32)
            uv = jnp.dot(hv, wu_ref[...], preferred_element_type=F32)
            g_ref[rows, :] = gv.astype(BF16)
            u_ref[rows, :] = uv.astype(BF16)
            a_ref[rows, :] = (gv * _sigmoid(gv) * uv).astype(BF16)

    wspec = pl.BlockSpec((None, K, Ns), lambda j, i: (j, 0, 0))
    ospec = pl.BlockSpec((tm, Ns), lambda j, i: (i, j))
    oshape = jax.ShapeDtypeStruct((S, J * Ns), BF16)
    return _call("ffn_up", body, grid=(J, S // tm),
                 in_specs=[pl.BlockSpec((tm, K), lambda j, i: (i, 0)), wspec, wspec, pl.BlockSpec((8, LANES), lambda j, i: (0, 0))],
                 out_specs=[ospec, ospec, ospec], out_shape=[oshape, oshape, oshape],
                 sem=("parallel", "parallel"))(h, wg, wu, token)


def mm_nn_rows_res(a, w, res, gain):
    S, K = a.shape
    N = w.shape[1]
    tm = min(512, S)
    tk, tn = (K, N) if K <= 2048 else (K // 2, N // 2)
    n_n, n_k = N // tn, K // tk

    def body(a_ref, w_ref, res_ref, g_ref, x_ref, h_ref, *acc):
        n, k = pl.program_id(1), pl.program_id(2)

        def normed(xv):
            r = lax.rsqrt(jnp.mean(xv * xv, axis=-1, keepdims=True) + EPS)
            h_ref[...] = (xv * r * g_ref[...]).astype(BF16)

        def store_columns(total):
            if n_n == 1:
                xv = total + res_ref[...]
                x_ref[...] = xv
                normed(xv)
                return
            for c in range(n_n):
                @pl.when(n == c)
                def _(c=c):
                    cols = slice(c * tn, (c + 1) * tn)
                    x_ref[:, cols] = total + res_ref[:, cols]

            @pl.when(n == n_n - 1)
            def _():
                normed(x_ref[...])

        part = jnp.dot(a_ref[...], w_ref[...], preferred_element_type=F32)
        if n_k == 1:
            store_columns(part)
        else:
            @pl.when(k == 0)
            def _():
                acc[0][...] = part

            @pl.when(k > 0)
            def _():
                acc[0][...] += part

            @pl.when(k == n_k - 1)
            def _():
                store_columns(acc[0][...])

    row = pl.BlockSpec((tm, N), lambda i, n, k: (i, 0))
    return _call("mm_nn_rows_res", body, grid=(S // tm, n_n, n_k),
                 in_specs=[pl.BlockSpec((tm, tk), lambda i, n, k: (i, k)), pl.BlockSpec((tk, tn), lambda i, n, k: (k, n)), row,
                           pl.BlockSpec((1, N), lambda i, n, k: (0, 0))],
                 out_specs=[row, row], out_shape=[jax.ShapeDtypeStruct((S, N), F32), jax.ShapeDtypeStruct((S, N), BF16)],
                 scratch=[pltpu.VMEM((tm, tn), F32)] if n_k > 1 else [],
                 sem=("parallel", "arbitrary", "arbitrary"))(a, w, res, gain)


def mm_nt_cols(pairs_in, out_dtype, shards_per_step):
    dz0, w0 = pairs_in[0]
    S = dz0.shape[0]
    J, K, Ns = w0.shape
    tm = min(512, S)
    sps = shards_per_step
    operands, specs, pairs = [], [], []
    for dz, w in pairs_in:
        for s in range(sps):
            pairs.append((len(operands), len(operands) + 1))
            operands += [dz, w]
            specs += [pl.BlockSpec((tm, Ns), lambda i, j, s=s: (i, j * sps + s)),
                      pl.BlockSpec((None, K, Ns), lambda i, j, s=s: (j * sps + s, 0, 0))]
    return _matmul("mm_nt_cols%d" % len(pairs_in), tuple(operands), specs,
                   [jax.ShapeDtypeStruct((S, K), out_dtype)], [pl.BlockSpec((tm, K), lambda i, j: (i, 0))],
                   (S // tm, J // sps), pairs, NT, (tm, K), _store(out_dtype))[0]


def mm_nt_rows(dy, w, token):
    S, N = dy.shape
    K = w.shape[0]
    tm, tko = min(1024, S), 512
    return _matmul("mm_nt_rows", (dy, w, token),
                   [pl.BlockSpec((tm, N), lambda i, kk, z: (i, 0)), pl.BlockSpec((tko, N), lambda i, kk, z: (kk, 0)),
                    pl.BlockSpec((8, LANES), lambda i, kk, z: (0, 0))],
                   [jax.ShapeDtypeStruct((S, K), BF16)], [pl.BlockSpec((tm, tko), lambda i, kk, z: (i, kk))],
                   (S // tm, K // tko, 1), [(0, 1)], NT, None, _store(BF16))[0]


def ffn_down_bwd(dy, w, gate, up, token):
    S, N = dy.shape
    K = w.shape[0]
    tm, tko = min(1024, S), 512
    sub = min(EPILOGUE_ROWS, tm)

    def body(dy_ref, w_ref, g_ref, u_ref, token_ref, dg_ref, du_ref):
        for r in range(tm // sub):
            rows = slice(r * sub, (r + 1) * sub)
            dact = lax.dot_general(dy_ref[rows, :], w_ref[...], NT, preferred_element_type=F32)
            gv = g_ref[rows, :].astype(F32)
            uv = u_ref[rows, :].astype(F32)
            sg = _sigmoid(gv)
            dg_ref[rows, :] = (dact * uv * sg * (1.0 + gv * (1.0 - sg))).astype(BF16)
            du_ref[rows, :] = (dact * gv * sg).astype(BF16)

    tile = pl.BlockSpec((tm, tko), lambda i, kk: (i, kk))
    oshape = jax.ShapeDtypeStruct((S, K), BF16)
    return _call("ffn_down_bwd", body, grid=(S // tm, K // tko),
                 in_specs=[pl.BlockSpec((tm, N), lambda i, kk: (i, 0)), pl.BlockSpec((tko, N), lambda i, kk: (kk, 0)), tile, tile,
                           pl.BlockSpec((8, LANES), lambda i, kk: (0, 0))],
                 out_specs=[tile, tile], out_shape=[oshape, oshape], sem=("parallel", "parallel"))(dy, w, gate, up, token)


def mm_tn_cols(a, dz, J):
    S, M = a.shape
    Ns = dz.shape[1] // J
    tm, tk = 512, S
    return _matmul("mm_tn_cols", (a, dz),
                   [pl.BlockSpec((tk, tm), lambda j, m, k: (k, m)), pl.BlockSpec((tk, Ns), lambda j, m, k: (k, j))],
                   [jax.ShapeDtypeStruct((J, M, Ns), BF16)], [pl.BlockSpec((None, tm, Ns), lambda j, m, k: (j, m, 0))],
                   (J, M // tm, S // tk), [(0, 1)], TN, (tm, Ns), _store(BF16))[0]


def mm_tn_rows(a, dy):
    S, K = a.shape
    N = dy.shape[1]
    tm, tk = 512, S
    return _matmul("mm_tn_rows", (a, dy),
                   [pl.BlockSpec((tk, tm), lambda m, k: (k, m)), pl.BlockSpec((tk, N), lambda m, k: (k, 0))],
                   [jax.ShapeDtypeStruct((K, N), BF16)], [pl.BlockSpec((tm, N), lambda m, k: (m, 0))],
                   (K // tm, S // tk), [(0, 1)], TN, (tm, N), _store(BF16))[0]


def rope_tables(S):
    half = ROT_DIM // 2
    pos = jnp.arange(S, dtype=F32)
    inv = ROPE_THETA ** (-jnp.arange(0, ROT_DIM, 2, dtype=F32) / ROT_DIM)
    ang = pos[:, None] * inv[None, :]
    cos, sin = jnp.cos(ang), jnp.sin(ang)
    zeros = jnp.zeros((S, HEAD_DIM - ROT_DIM), F32)
    c = jnp.concatenate([cos, cos, jnp.ones((S, HEAD_DIM - ROT_DIM), F32)], axis=1)
    s_lo = jnp.concatenate([-sin, jnp.zeros((S, half), F32), zeros], axis=1)
    s_hi = jnp.concatenate([jnp.zeros((S, half), F32), sin, zeros], axis=1)
    return c, s_lo, s_hi


ROPE_ROWS = 512


def _rope(t, c, s_lo, s_hi):
    half = ROT_DIM // 2
    return t * c + pltpu.roll(t, HEAD_DIM - half, 1) * s_lo + pltpu.roll(t, half, 1) * s_hi


def _unrope(d, c, s_lo, s_hi):
    half = ROT_DIM // 2
    return d * c + pltpu.roll(d * s_lo, half, 1) + pltpu.roll(d * s_hi, HEAD_DIM - half, 1)


def rope_fwd(z, tabs):
    S = z.shape[0]
    T = min(ROPE_ROWS, S)

    def body(q_ref, kv_ref, c_ref, sl_ref, sh_ref, qr_ref, kp_ref, vp_ref):
        i = pl.program_id(0)

        @pl.when(i == 0)
        def _():
            zero = jnp.zeros((CHUNK, KV_WIDTH), BF16)
            kp_ref[0:CHUNK, :] = zero
            vp_ref[0:CHUNK, :] = zero
            kp_ref[S + CHUNK:S + 2 * CHUNK, :] = zero
            vp_ref[S + CHUNK:S + 2 * CHUNK, :] = zero

        c, sl, sh = c_ref[...], sl_ref[...], sh_ref[...]
        for h in range(N_Q_HEADS):
            cols = slice(h * HEAD_DIM, (h + 1) * HEAD_DIM)
            qr_ref[:, cols] = _rope(q_ref[:, cols].astype(F32), c, sl, sh).astype(BF16)
        rows = pl.ds(pl.multiple_of(CHUNK + i * T, CHUNK), T)
        for g in range(N_KV_HEADS):
            cols = slice(g * HEAD_DIM, (g + 1) * HEAD_DIM)
            kp_ref[rows, cols] = _rope(kv_ref[:, cols].astype(F32), c, sl, sh).astype(BF16)
        vp_ref[rows, :] = kv_ref[:, KV_WIDTH:2 * KV_WIDTH]

    tab = pl.BlockSpec((T, HEAD_DIM), lambda i: (i, 0))
    pad = pl.BlockSpec((S + 2 * CHUNK, KV_WIDTH), lambda i: (0, 0))
    return _call("rope_fwd", body, grid=(S // T,),
                 in_specs=[pl.BlockSpec((T, ATTN_WIDTH), lambda i: (i, 0)),
                           pl.BlockSpec((T, 2 * KV_WIDTH), lambda i: (i, OFF_K // (2 * KV_WIDTH))), tab, tab, tab],
                 out_specs=[pl.BlockSpec((T, ATTN_WIDTH), lambda i: (i, 0)), pad, pad],
                 out_shape=[jax.ShapeDtypeStruct((S, ATTN_WIDTH), BF16), jax.ShapeDtypeStruct((S + 2 * CHUNK, KV_WIDTH), BF16),
                            jax.ShapeDtypeStruct((S + 2 * CHUNK, KV_WIDTH), BF16)], sem=("arbitrary",))(z, z, *tabs)


def rope_bwd(dq, dkp, dvp, tabs):
    S = dq.shape[0]
    T = min(ROPE_ROWS, S)

    def body(dq_ref, dk_ref, dv_ref, c_ref, sl_ref, sh_ref, o_ref):
        c, sl, sh = c_ref[...], sl_ref[...], sh_ref[...]
        for h in range(N_Q_HEADS):
            cols = slice(h * HEAD_DIM, (h + 1) * HEAD_DIM)
            o_ref[:, cols] = _unrope(dq_ref[:, cols], c, sl, sh).astype(BF16)
        rows = pl.ds(pl.multiple_of(CHUNK + pl.program_id(0) * T, CHUNK), T)
        for g in range(N_KV_HEADS):
            cols = slice(g * HEAD_DIM, (g + 1) * HEAD_DIM)
            o_ref[:, OFF_K + g * HEAD_DIM:OFF_K + (g + 1) * HEAD_DIM] = _unrope(dk_ref[rows, cols], c, sl, sh).astype(BF16)
        o_ref[:, OFF_V:OFF_V + KV_WIDTH] = dv_ref[rows, :].astype(BF16)

    tab = pl.BlockSpec((T, HEAD_DIM), lambda i: (i, 0))
    pad = pl.BlockSpec((S + 2 * CHUNK, KV_WIDTH), lambda i: (0, 0))
    return _call("rope_bwd", body, grid=(S // T,),
                 in_specs=[pl.BlockSpec((T, ATTN_WIDTH), lambda i: (i, 0)), pad, pad, tab, tab, tab],
                 out_specs=pl.BlockSpec((T, OFF_CA), lambda i: (i, 0)),
                 out_shape=jax.ShapeDtypeStruct((S, IN_WIDTH), BF16), sem=("parallel",))(dq, dkp, dvp, *tabs)


STACK = Q_PER_KV * CHUNK


def _stack_heads(ref, rows):
    return jnp.concatenate([ref[rows, r * HEAD_DIM:(r + 1) * HEAD_DIM] for r in range(Q_PER_KV)], axis=0)


def _stack_sinks(s_ref):
    return jnp.concatenate([jnp.broadcast_to(s_ref[r:r + 1, 0:1], (CHUNK, 1)) for r in range(Q_PER_KV)], axis=0)


MASKED = -1e30


def _scores(q, kb):
    return lax.dot_general(q, kb, NT, preferred_element_type=F32) * (1.0 / math.sqrt(HEAD_DIM))


def _band_bias():
    row = lax.broadcasted_iota(jnp.int32, (STACK, 3 * CHUNK), 0) & (CHUNK - 1)
    col = lax.broadcasted_iota(jnp.int32, (STACK, 3 * CHUNK), 1)
    return jnp.where(jnp.abs(col - CHUNK - row) <= WINDOW, 0.0, MASKED).astype(F32)


def _edge_bias(n, S):
    kpos = (n - 1) * CHUNK + lax.broadcasted_iota(jnp.int32, (1, 3 * CHUNK), 1)
    return jnp.where((kpos >= 0) & (kpos < S), 0.0, MASKED).astype(F32)


def _softmax_sink(s, sk, bias):
    s = s + bias
    m = jnp.maximum(jnp.max(s, axis=1, keepdims=True), sk)
    e = jnp.exp(s - m)
    es = jnp.exp(sk - m)
    inv = 1.0 / (jnp.sum(e, axis=1, keepdims=True) + es)
    return e * inv, es * inv


def _block_views(i, nblk):
    ns = [i * nblk + b for b in range(nblk)]
    wins = [pl.ds(pl.multiple_of(n * CHUNK, CHUNK), 3 * CHUNK) for n in ns]
    rows = [slice(b * CHUNK, (b + 1) * CHUNK) for b in range(nblk)]
    return ns, wins, rows


def attn_fwd(qr, kp, vp, sink3):
    S = qr.shape[0]
    tq = min(2048, S)
    gw = Q_PER_KV * HEAD_DIM
    nblk = tq // CHUNK

    def body(q_ref, k_ref, v_ref, s_ref, o_ref):
        ns, wins, rows = _block_views(pl.program_id(1), nblk)
        sk = _stack_sinks(s_ref)
        band = _band_bias()
        scores = [_scores(_stack_heads(q_ref, rows[b]), k_ref[wins[b], :]) for b in range(nblk)]
        probs = [_softmax_sink(scores[b], sk, band + _edge_bias(ns[b], S))[0].astype(BF16) for b in range(nblk)]
        outs = [jnp.dot(probs[b], v_ref[wins[b], :], preferred_element_type=F32).astype(BF16) for b in range(nblk)]
        for b in range(nblk):
            for r in range(Q_PER_KV):
                o_ref[rows[b], r * HEAD_DIM:(r + 1) * HEAD_DIM] = outs[b][r * CHUNK:(r + 1) * CHUNK]

    kv = pl.BlockSpec((S + 2 * CHUNK, HEAD_DIM), lambda g, i: (0, g))
    return _call("attn_fwd", body, grid=(N_KV_HEADS, S // tq),
                 in_specs=[pl.BlockSpec((tq, gw), lambda g, i: (i, g)), kv, kv,
                           pl.BlockSpec((None, Q_PER_KV, LANES), lambda g, i: (g, 0, 0))],
                 out_specs=pl.BlockSpec((tq, gw), lambda g, i: (i, g)),
                 out_shape=jax.ShapeDtypeStruct((S, D_MODEL), BF16), sem=("parallel", "arbitrary"))(qr, kp, vp, sink3)


def attn_bwd(qr, kp, vp, sink3, dmix):
    S = qr.shape[0]
    tq = min(1024, S)
    gw = Q_PER_KV * HEAD_DIM
    scale = 1.0 / math.sqrt(HEAD_DIM)
    nblk = tq // CHUNK

    def body(q_ref, k_ref, v_ref, s_ref, do_ref, dq_ref, dk_ref, dv_ref, ds_ref):
        i = pl.program_id(1)

        @pl.when(i == 0)
        def _():
            dk_ref[...] = jnp.zeros_like(dk_ref)
            dv_ref[...] = jnp.zeros_like(dv_ref)
            ds_ref[...] = jnp.zeros_like(ds_ref)

        blocks = range(nblk)
        ns, wins, rows = _block_views(i, nblk)
        sk = _stack_sinks(s_ref)
        band = _band_bias()
        qs = [_stack_heads(q_ref, rows[b]) for b in blocks]
        dos = [_stack_heads(do_ref, rows[b]) for b in blocks]
        scores = [_scores(qs[b], k_ref[wins[b], :]) for b in blocks]
        dps = [lax.dot_general(dos[b], v_ref[wins[b], :], NT, preferred_element_type=F32) for b in blocks]
        probs = [_softmax_sink(scores[b], sk, band + _edge_bias(ns[b], S)) for b in blocks]
        deltas = [jnp.sum(probs[b][0] * dps[b], axis=1, keepdims=True) for b in blocks]
        dscs = [(probs[b][0] * (dps[b] - deltas[b]) * scale).astype(BF16) for b in blocks]
        dqs = [jnp.dot(dscs[b], k_ref[wins[b], :], preferred_element_type=F32) for b in blocks]
        dks = [lax.dot_general(dscs[b], qs[b], TN, preferred_element_type=F32) for b in blocks]
        dvs = [lax.dot_general(probs[b][0].astype(BF16), dos[b], TN, preferred_element_type=F32) for b in blocks]
        for b in blocks:
            for r in range(Q_PER_KV):
                dq_ref[rows[b], r * HEAD_DIM:(r + 1) * HEAD_DIM] = dqs[b][r * CHUNK:(r + 1) * CHUNK]
        for m in range(nblk + 2):
            parts = [(b, m - b) for b in blocks if 0 <= m - b <= 2]
            krows = pl.ds(pl.multiple_of(i * tq + m * CHUNK, CHUNK), CHUNK)
            dk_ref[krows, :] += sum(dks[b][o * CHUNK:(o + 1) * CHUNK] for b, o in parts)
            dv_ref[krows, :] += sum(dvs[b][o * CHUNK:(o + 1) * CHUNK] for b, o in parts)
        for r in range(Q_PER_KV):
            head = slice(r * CHUNK, (r + 1) * CHUNK)
            dsink = sum(jnp.sum(-probs[b][1][head] * deltas[b][head], axis=0, keepdims=True) for b in blocks)
            ds_ref[r:r + 1, :] += jnp.broadcast_to(dsink, (1, LANES))

    kv = pl.BlockSpec((S + 2 * CHUNK, HEAD_DIM), lambda g, i: (0, g))
    qspec = pl.BlockSpec((tq, gw), lambda g, i: (i, g))
    sspec = pl.BlockSpec((None, Q_PER_KV, LANES), lambda g, i: (g, 0, 0))
    padshape = jax.ShapeDtypeStruct((S + 2 * CHUNK, KV_WIDTH), F32)
    return _call("attn_bwd", body, grid=(N_KV_HEADS, S // tq),
                 in_specs=[qspec, kv, kv, sspec, qspec],
                 out_specs=[qspec, kv, kv, sspec],
                 out_shape=[jax.ShapeDtypeStruct((S, ATTN_WIDTH), F32), padshape, padshape,
                            jax.ShapeDtypeStruct((N_KV_HEADS, Q_PER_KV, LANES), F32)],
                 sem=("parallel", "arbitrary"))(qr, kp, vp, sink3, dmix)


CONV_TILE = 256


def _fill_padded(dst_ref, value, S):
    zero = jnp.zeros((CONV_PAD, LANES), F32)
    dst_ref[0:CONV_PAD, :] = zero
    dst_ref[CONV_PAD + S:2 * CONV_PAD + S, :] = zero
    dst_ref[CONV_PAD:CONV_PAD + S, :] = value


def conv_dw_fwd(z, w32, b):
    S = z.shape[0]
    T = min(CONV_TILE, S)
    lo = CONV_PAD - (CONV_KERNEL - 1) // 2

    def body(a_ref, g_ref, w_ref, b_ref, o_ref, c0_ref):
        _fill_padded(c0_ref, a_ref[...].astype(F32) * _sigmoid(g_ref[...].astype(F32)), S)

        def tile(t, carry):
            base = pl.multiple_of(t * T, T)
            acc = jnp.broadcast_to(b_ref[...], (T, LANES))
            for j in range(CONV_KERNEL):
                acc = acc + w_ref[j:j + 1, :] * c0_ref[pl.ds(base + lo + j, T), :]
            o_ref[pl.ds(base, T), :] = acc
            return carry

        lax.fori_loop(0, S // T, tile, 0)

    nca, ncg = OFF_CA // LANES, OFF_CG // LANES
    return _call("conv_dw_fwd", body, grid=(CONV_WIDTH // LANES,),
                 in_specs=[pl.BlockSpec((S, LANES), lambda cb: (0, nca + cb)), pl.BlockSpec((S, LANES), lambda cb: (0, ncg + cb)),
                           pl.BlockSpec((32, LANES), lambda cb: (0, cb)), pl.BlockSpec((1, LANES), lambda cb: (0, cb))],
                 out_specs=pl.BlockSpec((S, LANES), lambda cb: (0, cb)),
                 out_shape=jax.ShapeDtypeStruct((S, CONV_WIDTH), F32),
                 scratch=[pltpu.VMEM((S + 2 * CONV_PAD, LANES), F32)], sem=("parallel",))(z, z, w32, b)


def _ln_stats(x):
    mu = jnp.mean(x, axis=-1, keepdims=True)
    xc = x - mu
    rs = lax.rsqrt(jnp.mean(xc * xc, axis=-1, keepdims=True) + EPS)
    return xc * rs, rs


def _ln_bwd(dy, xh, rs, g):
    dxh = dy * g
    return rs * (dxh - jnp.mean(dxh, axis=-1, keepdims=True) - xh * jnp.mean(dxh * xh, axis=-1, keepdims=True))


def conv_ln_fwd(c1, g, b, mix):
    S = c1.shape[0]
    T = min(512, S)

    def body(x_ref, g_ref, b_ref, mix_ref, o_ref):
        xh, _ = _ln_stats(x_ref[...])
        y = xh * g_ref[...] + b_ref[...]
        o_ref[...] = (y * _sigmoid(y)).astype(BF16)

    row = pl.BlockSpec((T, CONV_WIDTH), lambda i: (i, 0))
    vec = pl.BlockSpec((1, CONV_WIDTH), lambda i: (0, 0))
    return _call("conv_ln_fwd", body, grid=(S // T,), in_specs=[row, vec, vec, pl.BlockSpec(memory_space=pl.ANY)],
                 out_specs=pl.BlockSpec((T, CONV_WIDTH), lambda i: (i, ATTN_WIDTH // CONV_WIDTH)),
                 out_shape=jax.ShapeDtypeStruct(mix.shape, BF16), sem=("parallel",), aliases={3: 0})(c1, g, b, mix)


def _acc_out(ref, value, step=None):
    step = pl.program_id(0) if step is None else step

    @pl.when(step == 0)
    def _():
        ref[...] = value

    @pl.when(step > 0)
    def _():
        ref[...] += value


def conv_ln_bwd(dmix, c1, g, b):
    S = c1.shape[0]
    T = min(512, S)

    def body(d_ref, x_ref, g_ref, b_ref, dx_ref, dg_ref, db_ref):
        xh, rs = _ln_stats(x_ref[...])
        gv = g_ref[...]
        y = xh * gv + b_ref[...]
        sg = _sigmoid(y)
        dy = d_ref[...].astype(F32) * sg * (1.0 + y * (1.0 - sg))
        dx_ref[...] = _ln_bwd(dy, xh, rs, gv)
        _acc_out(dg_ref, jnp.sum(dy * xh, axis=0, keepdims=True))
        _acc_out(db_ref, jnp.sum(dy, axis=0, keepdims=True))

    row = pl.BlockSpec((T, CONV_WIDTH), lambda i: (i, 0))
    vec = pl.BlockSpec((1, CONV_WIDTH), lambda i: (0, 0))
    vshape = jax.ShapeDtypeStruct((1, CONV_WIDTH), F32)
    return _call("conv_ln_bwd", body, grid=(S // T,),
                 in_specs=[pl.BlockSpec((T, CONV_WIDTH), lambda i: (i, ATTN_WIDTH // CONV_WIDTH)), row, vec, vec],
                 out_specs=[row, vec, vec], out_shape=[jax.ShapeDtypeStruct((S, CONV_WIDTH), F32), vshape, vshape],
                 sem=("arbitrary",))(dmix, c1, g, b)


def conv_dw_bwd(dc1, z, w32, dz):
    S = z.shape[0]
    T = min(CONV_TILE, S)
    half = (CONV_KERNEL - 1) // 2
    lo = CONV_PAD - half
    n_cb = CONV_WIDTH // LANES

    def body(d_ref, a_ref, g_ref, w_ref, dz_ref, o_ref, dw_ref, db_ref, c0_ref, d1_ref, wacc_ref, dg_ref):
        @pl.when(pl.program_id(1) == 0)
        def _():
            av = a_ref[...].astype(F32)
            sg = _sigmoid(g_ref[...].astype(F32))
            _fill_padded(c0_ref, av * sg, S)
            _fill_padded(d1_ref, d_ref[...], S)
            wacc_ref[...] = jnp.zeros_like(wacc_ref)

            def tile(t, carry):
                base = pl.multiple_of(t * T, T)
                d1 = d_ref[pl.ds(base, T), :]
                acc = jnp.zeros((T, LANES), F32)
                for j in range(CONV_KERNEL):
                    acc = acc + w_ref[j:j + 1, :] * d1_ref[pl.ds(base + CONV_PAD + half - j, T), :]
                    prod = d1 * c0_ref[pl.ds(base + lo + j, T), :]
                    wacc_ref[j] += jnp.sum(prod.reshape(T // 8, 8, LANES), axis=0)
                rows = pl.ds(base, T)
                a_t = a_ref[rows, :].astype(F32)
                s_t = _sigmoid(g_ref[rows, :].astype(F32))
                o_ref[rows, :] = (acc * s_t).astype(BF16)
                dg_ref[rows, :] = (acc * a_t * s_t * (1.0 - s_t)).astype(BF16)
                return carry

            lax.fori_loop(0, S // T, tile, 0)
            dw_ref[...] = jnp.sum(wacc_ref[...], axis=1)
            db_ref[...] = jnp.sum(d_ref[...], axis=0, keepdims=True)

        @pl.when(pl.program_id(1) == 1)
        def _():
            o_ref[...] = dg_ref[...]

    nca, ncg = OFF_CA // LANES, OFF_CG // LANES
    return _call("conv_dw_bwd", body, grid=(n_cb, 2),
                 in_specs=[pl.BlockSpec((S, LANES), lambda cb, j: (0, cb)), pl.BlockSpec((S, LANES), lambda cb, j: (0, nca + cb)),
                           pl.BlockSpec((S, LANES), lambda cb, j: (0, ncg + cb)), pl.BlockSpec((32, LANES), lambda cb, j: (0, cb)),
                           pl.BlockSpec(memory_space=pl.ANY)],
                 out_specs=[pl.BlockSpec((S, LANES), lambda cb, j: (0, nca + cb + n_cb * j)),
                            pl.BlockSpec((32, LANES), lambda cb, j: (0, cb)), pl.BlockSpec((1, LANES), lambda cb, j: (0, cb))],
                 out_shape=[jax.ShapeDtypeStruct(dz.shape, BF16), jax.ShapeDtypeStruct((32, CONV_WIDTH), F32),
                            jax.ShapeDtypeStruct((1, CONV_WIDTH), F32)],
                 scratch=[pltpu.VMEM((S + 2 * CONV_PAD, LANES), F32), pltpu.VMEM((S + 2 * CONV_PAD, LANES), F32),
                          pltpu.VMEM((32, 8, LANES), F32), pltpu.VMEM((S, LANES), BF16)],
                 sem=("parallel", "arbitrary"), aliases={4: 0})(dc1, z, z, w32, dz)


_INV_SQRT2 = 1.0 / math.sqrt(2.0)
_INV_SQRT2PI = 1.0 / math.sqrt(2.0 * math.pi)


def _gelu(x):
    return 0.5 * x * (1.0 + lax.erf(x * _INV_SQRT2))


def _gelu_grad(x):
    return 0.5 * (1.0 + lax.erf(x * _INV_SQRT2)) + x * jnp.exp(-0.5 * x * x) * _INV_SQRT2PI


def sgu_fwd(z, g, b, ws, bs, mix):
    S = z.shape[0]
    T = min(512, S)

    def body(u_ref, v_ref, g_ref, b_ref, ws_ref, bs_ref, mix_ref, o_ref):
        xh, _ = _ln_stats(_gelu(v_ref[...].astype(F32)))
        vn = (xh * g_ref[...] + b_ref[...]).astype(BF16)
        for ch in range(T // CHUNK):
            rows = slice(ch * CHUNK, (ch + 1) * CHUNK)
            for h in range(SGU_HEADS):
                cols = slice(h * HEAD_DIM, (h + 1) * HEAD_DIM)
                sp = jnp.dot(ws_ref[h], vn[rows, cols], preferred_element_type=F32) + bs_ref[h]
                o_ref[rows, cols] = (_gelu(u_ref[rows, cols].astype(F32)) * sp).astype(BF16)

    vec = pl.BlockSpec((1, SGU_WIDTH), lambda i: (0, 0))
    full = pl.BlockSpec((SGU_HEADS, CHUNK, CHUNK), lambda i: (0, 0, 0))
    return _call("sgu_fwd", body, grid=(S // T,),
                 in_specs=[pl.BlockSpec((T, SGU_WIDTH), lambda i: (i, OFF_U // SGU_WIDTH)),
                           pl.BlockSpec((T, SGU_WIDTH), lambda i: (i, OFF_VV // SGU_WIDTH)), vec, vec, full, full,
                           pl.BlockSpec(memory_space=pl.ANY)],
                 out_specs=pl.BlockSpec((T, SGU_WIDTH), lambda i: (i, (ATTN_WIDTH + CONV_WIDTH) // SGU_WIDTH)),
                 out_shape=jax.ShapeDtypeStruct(mix.shape, BF16), sem=("parallel",), aliases={6: 0})(z, z, g, b, ws, bs, mix)


def sgu_bwd(z, dmix, g, b, ws, bs, dz):
    S = z.shape[0]
    T = min(512, S)

    def body(u_ref, v_ref, d_ref, g_ref, b_ref, ws_ref, bs_ref, dz_ref, o_ref, dws_ref, dbs_ref, dg_ref, db_ref, dvn_ref, dv_ref):
        tile = pl.program_id(0)
        first = pl.program_id(1) == 0

        @pl.when(first & (tile == 0))
        def _():
            dws_ref[...] = jnp.zeros_like(dws_ref)
            dbs_ref[...] = jnp.zeros_like(dbs_ref)

        @pl.when(first)
        def _():
            vraw = v_ref[...].astype(F32)
            xh, rs = _ln_stats(_gelu(vraw))
            gv = g_ref[...]
            vn = (xh * gv + b_ref[...]).astype(BF16)
            for ch in range(T // CHUNK):
                rows = slice(ch * CHUNK, (ch + 1) * CHUNK)
                for h in range(SGU_HEADS):
                    cols = slice(h * HEAD_DIM, (h + 1) * HEAD_DIM)
                    w = ws_ref[h]
                    vb = vn[rows, cols]
                    sp = jnp.dot(w, vb, preferred_element_type=F32) + bs_ref[h]
                    uraw = u_ref[rows, cols].astype(F32)
                    dout = d_ref[rows, cols].astype(F32)
                    o_ref[rows, cols] = (dout * sp * _gelu_grad(uraw)).astype(BF16)
                    dsp = dout * _gelu(uraw)
                    dspb = dsp.astype(BF16)
                    dvn_ref[rows, cols] = lax.dot_general(w, dspb, TN, preferred_element_type=F32)
                    dws_ref[h] += lax.dot_general(dspb, vb, NT, preferred_element_type=F32)
                    dbs_ref[h] += jnp.sum(dsp, axis=1, keepdims=True)
            dvn = dvn_ref[...]
            dv_ref[...] = (_ln_bwd(dvn, xh, rs, gv) * _gelu_grad(vraw)).astype(BF16)
            _acc_out(dg_ref, jnp.sum(dvn * xh, axis=0, keepdims=True), tile)
            _acc_out(db_ref, jnp.sum(dvn, axis=0, keepdims=True), tile)

        @pl.when(pl.program_id(1) == 1)
        def _():
            o_ref[...] = dv_ref[...]

    vec = pl.BlockSpec((1, SGU_WIDTH), lambda i, j: (0, 0))
    full = pl.BlockSpec((SGU_HEADS, CHUNK, CHUNK), lambda i, j: (0, 0, 0))
    vshape = jax.ShapeDtypeStruct((1, SGU_WIDTH), F32)
    return _call("sgu_bwd", body, grid=(S // T, 2),
                 in_specs=[pl.BlockSpec((T, SGU_WIDTH), lambda i, j: (i, OFF_U // SGU_WIDTH)),
                           pl.BlockSpec((T, SGU_WIDTH), lambda i, j: (i, OFF_VV // SGU_WIDTH)),
                           pl.BlockSpec((T, SGU_WIDTH), lambda i, j: (i, (ATTN_WIDTH + CONV_WIDTH) // SGU_WIDTH)), vec, vec, full, full,
                           pl.BlockSpec(memory_space=pl.ANY)],
                 out_specs=[pl.BlockSpec((T, SGU_WIDTH), lambda i, j: (i, OFF_U // SGU_WIDTH + j)), full,
                            pl.BlockSpec((SGU_HEADS, CHUNK, 1), lambda i, j: (0, 0, 0)), vec, vec],
                 out_shape=[jax.ShapeDtypeStruct(dz.shape, BF16), jax.ShapeDtypeStruct((SGU_HEADS, CHUNK, CHUNK), F32),
                            jax.ShapeDtypeStruct((SGU_HEADS, CHUNK, 1), F32), vshape, vshape],
                 scratch=[pltpu.VMEM((T, SGU_WIDTH), F32), pltpu.VMEM((T, SGU_WIDTH), BF16)],
                 sem=("arbitrary", "arbitrary"), aliases={7: 0})(z, z, dmix, g, b, ws, bs, dz)


def _row_tile(rows, cols, n_arrays, budget_mib=24):
    budget = (budget_mib * 1024 * 1024) // (n_arrays * 2 * 4 * cols)
    t = min(rows, max(16, budget // 16 * 16))
    while rows % t:
        t -= 16
    return t


def cast_layer(w, layer, token):
    _, R, C = w.shape
    tr = _row_tile(R, C, 2)

    def body(w_ref, token_ref, o_ref):
        o_ref[...] = w_ref[...].astype(BF16)

    return _call("cast_layer", body, grid=(R // tr,),
                 in_specs=[pl.BlockSpec((None, tr, C), lambda i: (layer, i, 0)), pl.BlockSpec((8, LANES), lambda i: (0, 0))],
                 out_specs=pl.BlockSpec((tr, C), lambda i: (i, 0)), out_shape=jax.ShapeDtypeStruct((R, C), BF16),
                 sem=("parallel",))(w, token)


def add_sibling_half(grad, recv, c_idx):
    J, R, C = grad.shape
    hr = R // 2
    tr = _row_tile(hr, C, 3)
    nb = hr // tr

    def body(c_ref, g_ref, r_ref, o_ref):
        o_ref[...] = (g_ref[...].astype(F32) + r_ref[...].astype(F32)).astype(BF16)

    grid_spec = pltpu.PrefetchScalarGridSpec(
        num_scalar_prefetch=1, grid=(J, nb),
        in_specs=[pl.BlockSpec((None, tr, C), lambda j, i, c: (j, c[0] * nb + i, 0)),
                  pl.BlockSpec((None, tr, C), lambda j, i, c: (j, i, 0))],
        out_specs=pl.BlockSpec((None, tr, C), lambda j, i, c: (j, i, 0)))
    return pl.pallas_call(body, name="add_sibling_half", grid_spec=grid_spec,
                          out_shape=jax.ShapeDtypeStruct((J, hr, C), BF16),
                          compiler_params=pltpu.CompilerParams(vmem_limit_bytes=VMEM_LIMIT,
                                                               dimension_semantics=("parallel", "parallel")))(c_idx, grad, recv)


def sum_chips(own, others, stack, x_idx, y_idx, layer):
    R, C = own.shape[1:]
    tr = _row_tile(R, C, 4)

    def body(x_ref, y_ref, own_ref, oth_ref, stack_ref, o_ref):
        acc = own_ref[...].astype(F32)
        for j in range(3):
            acc = acc + oth_ref[j].astype(F32)
        o_ref[...] = acc

    grid_spec = pltpu.PrefetchScalarGridSpec(
        num_scalar_prefetch=2, grid=(R // tr,),
        in_specs=[pl.BlockSpec((None, tr, C), lambda i, xr, yr: (2 * xr[0] + yr[0], i, 0)),
                  pl.BlockSpec((3, tr, C), lambda i, xr, yr: (0, i, 0)),
                  pl.BlockSpec(memory_space=pl.ANY)],
        out_specs=pl.BlockSpec((None, tr, C), lambda i, xr, yr: (layer, i, 0)))
    return pl.pallas_call(body, name="sum_chips", grid_spec=grid_spec,
                          out_shape=jax.ShapeDtypeStruct(stack.shape, F32), input_output_aliases={4: 0},
                          compiler_params=pltpu.CompilerParams(vmem_limit_bytes=VMEM_LIMIT,
                                                               dimension_semantics=("parallel",)))(x_idx, y_idx, own, others, stack)


def adamw_halves(w, mine, theirs, m, v, c_idx):
    L, R, C = w.shape
    hr = R // 2
    tr = _row_tile(hr, C, 9, budget_mib=40)
    nb = hr // tr

    def body(c_ref, w_ref, a_ref, b_ref, m_ref, v_ref, g_ref, d_ref, nm_ref, nv_ref):
        gv = jnp.where(pl.program_id(1) == c_ref[0], a_ref[...], b_ref[...])
        g_ref[...] = gv
        nm = ADAM_B1 * m_ref[...] + (1.0 - ADAM_B1) * gv
        nv = ADAM_B2 * v_ref[...] + (1.0 - ADAM_B2) * (gv * gv)
        m_hat = nm / (1.0 - ADAM_B1 ** ADAM_STEP)
        v_hat = nv / (1.0 - ADAM_B2 ** ADAM_STEP)
        d_ref[...] = -ADAM_LR * (m_hat / (jnp.sqrt(v_hat) + ADAM_EPS) + ADAM_WD * w_ref[...])
        nm_ref[...] = nm
        nv_ref[...] = nv

    full = pl.BlockSpec((None, tr, C), lambda l, h, i, c: (l, h * nb + i, 0))
    a_spec = pl.BlockSpec((None, tr, C), lambda l, h, i, c: (l, jnp.where(h == c[0], i, 0), 0))
    b_spec = pl.BlockSpec((None, tr, C), lambda l, h, i, c: (l, jnp.where(h == c[0], 0, i), 0))
    grid_spec = pltpu.PrefetchScalarGridSpec(num_scalar_prefetch=1, grid=(L, 2, nb),
                                             in_specs=[full, a_spec, b_spec, full, full], out_specs=[full] * 4)
    shape = jax.ShapeDtypeStruct((L, R, C), F32)
    return pl.pallas_call(body, name="adamw_halves", grid_spec=grid_spec, out_shape=[shape] * 4,
                          compiler_params=pltpu.CompilerParams(vmem_limit_bytes=VMEM_LIMIT,
                                                               dimension_semantics=("parallel", "arbitrary", "arbitrary")))(
        c_idx, w, mine, theirs, m, v)


def adamw(w, g, m, v):
    R, C = w.shape
    tr = _row_tile(R, C, 7)

    def body(w_ref, g_ref, m_ref, v_ref, d_ref, nm_ref, nv_ref):
        gv = g_ref[...]
        nm = ADAM_B1 * m_ref[...] + (1.0 - ADAM_B1) * gv
        nv = ADAM_B2 * v_ref[...] + (1.0 - ADAM_B2) * (gv * gv)
        m_hat = nm / (1.0 - ADAM_B1 ** ADAM_STEP)
        v_hat = nv / (1.0 - ADAM_B2 ** ADAM_STEP)
        d_ref[...] = -ADAM_LR * (m_hat / (jnp.sqrt(v_hat) + ADAM_EPS) + ADAM_WD * w_ref[...])
        nm_ref[...] = nm
        nv_ref[...] = nv

    spec = pl.BlockSpec((tr, C), lambda i: (i, 0))
    shape = jax.ShapeDtypeStruct((R, C), F32)
    return _call("adamw", body, grid=(R // tr,), in_specs=[spec] * 4, out_specs=[spec] * 3, out_shape=[shape] * 3,
                 sem=("parallel",))(w, g, m, v)


def _place():
    x, y, c = lax.axis_index("x"), lax.axis_index("y"), lax.axis_index("c")
    chips = [(1 - x, y), (x, 1 - y), (1 - x, 1 - y)]
    return x, y, c, chips


def _remote(src, dst, send_sem, recv_sem, dev):
    return pltpu.make_async_remote_copy(src_ref=src, dst_ref=dst, send_sem=send_sem, recv_sem=recv_sem,
                                        device_id=dev, device_id_type=MESH)


EFFECT = pltpu.SideEffectType.DATAFLOW_SIDE_EFFECTING
SEM = pl.BlockSpec(memory_space=pltpu.SEMAPHORE)
ANY = pl.BlockSpec(memory_space=pl.ANY)
TOKEN = jax.ShapeDtypeStruct((8, LANES), F32)


def _in_hbm(a):
    return pltpu.with_memory_space_constraint(a, pltpu.HBM)


def _gather_copies(shards, lands, send_sems, recv_sems):
    x, y, c, chips = _place()
    me = 2 * x + y
    copies = []
    for k in range(len(shards)):
        hr = shards[k].shape[0] // 2
        mine = pl.ds(pl.multiple_of(c * hr, 8), hr)
        for t, (px, py) in enumerate(chips):
            copies.append(_remote(shards[k].at[mine, :], lands[k].at[me, mine, :], send_sems.at[4 * k + t], recv_sems.at[4 * k + t],
                                  (px, py, c)))
        copies.append(_remote(shards[k], lands[k].at[me], send_sems.at[4 * k + 3], recv_sems.at[4 * k + 3], (x, y, 1 - c)))
    return copies


def _gather_landings(lands, send_sems, recv_sems):
    x, y, c, chips = _place()
    me = 2 * x + y
    landings = []
    for k in range(len(lands)):
        hr = lands[k].shape[1] // 2
        mine = pl.ds(pl.multiple_of(c * hr, 8), hr)
        for t, (px, py) in enumerate(chips):
            dst = lands[k].at[2 * px + py, mine, :]
            landings.append(_remote(dst, dst, send_sems.at[4 * k + t], recv_sems.at[4 * k + t], (px, py, c)))
        dst = lands[k].at[me]
        landings.append(_remote(dst, dst, send_sems.at[4 * k + 3], recv_sems.at[4 * k + 3], (x, y, 1 - c)))
    return landings


def gather_start(shards, after):
    n = len(shards)

    def body(*refs):
        srcs, lands_in = refs[:n], refs[n:2 * n]
        send_sems, recv_sems = refs[2 * n + 1], refs[2 * n + 2]
        token = refs[-1]
        for cp in _gather_copies(srcs, lands_in, send_sems, recv_sems):
            cp.start()
        token[...] = jnp.zeros_like(token)

    lands = [lax.empty((N_CHIPS,) + s.shape, s.dtype) for s in shards]
    outs = pl.pallas_call(
        body, name="gather_start", in_specs=[HBM] * (2 * n) + [ANY],
        out_specs=[SEM, SEM] + [HBM] * (2 * n) + [VMEM_SPEC],
        out_shape=[pltpu.SemaphoreType.DMA((4 * n,)), pltpu.SemaphoreType.DMA((4 * n,))]
        + [pltpu.HBM(s.shape, s.dtype) for s in shards] + [pltpu.HBM(l.shape, l.dtype) for l in lands] + [TOKEN],
        input_output_aliases={i: 2 + i for i in range(2 * n)},
        compiler_params=pltpu.CompilerParams(has_side_effects=EFFECT),
    )(*[_in_hbm(s) for s in shards], *[_in_hbm(l) for l in lands], after)
    return outs[0], outs[1], outs[2:2 + n], outs[2 + n:2 + 2 * n], outs[-1]


def gather_wait(send_sems, recv_sems, shards, lands, after):
    n = len(shards)

    def body(*refs):
        srcs, lands_in = refs[:n], refs[n:2 * n]
        send, recv = refs[2 * n], refs[2 * n + 1]
        for cp in _gather_copies(srcs, lands_in, send, recv):
            cp.wait_send()
        for cp in _gather_landings(lands_in, send, recv):
            cp.wait_recv()

    outs = pl.pallas_call(
        body, name="gather_wait", in_specs=[HBM] * (2 * n) + [SEM, SEM, ANY], out_specs=[HBM] * (2 * n),
        out_shape=[pltpu.HBM(s.shape, s.dtype) for s in shards] + [pltpu.HBM(l.shape, l.dtype) for l in lands],
        input_output_aliases={i: i for i in range(2 * n)},
        compiler_params=pltpu.CompilerParams(has_side_effects=EFFECT),
    )(*shards, *lands, send_sems, recv_sems, after)
    return outs[n:]


def _forward_copies(lands, send_sems, recv_sems, received):
    x, y, c, chips = _place()
    copies = []
    for k in range(len(lands)):
        hr = lands[k].shape[1] // 2
        half = (1 - c) if received else c
        rows = pl.ds(pl.multiple_of(half * hr, 8), hr)
        for t, (px, py) in enumerate(chips):
            block = lands[k].at[2 * px + py, rows, :]
            copies.append(_remote(block, block, send_sems.at[3 * k + t], recv_sems.at[3 * k + t], (x, y, 1 - c)))
    return copies


def forward_start(lands):
    n = len(lands)

    def body(*refs):
        for cp in _forward_copies(refs[:n], refs[n], refs[n + 1], received=False):
            cp.start()
        refs[-1][...] = jnp.zeros_like(refs[-1])

    outs = pl.pallas_call(
        body, name="forward_start", in_specs=[HBM] * n, out_specs=[SEM, SEM] + [HBM] * n + [VMEM_SPEC],
        out_shape=[pltpu.SemaphoreType.DMA((3 * n,)), pltpu.SemaphoreType.DMA((3 * n,))]
        + [pltpu.HBM(l.shape, l.dtype) for l in lands] + [TOKEN],
        input_output_aliases={i: 2 + i for i in range(n)},
        compiler_params=pltpu.CompilerParams(has_side_effects=EFFECT),
    )(*[_in_hbm(l) for l in lands])
    return outs[0], outs[1], outs[2:2 + n], outs[-1]


def forward_wait(send_sems, recv_sems, lands, after):
    n = len(lands)

    def body(*refs):
        for cp in _forward_copies(refs[:n], refs[n], refs[n + 1], received=False):
            cp.wait_send()
        for cp in _forward_copies(refs[:n], refs[n], refs[n + 1], received=True):
            cp.wait_recv()

    return pl.pallas_call(
        body, name="forward_wait", in_specs=[HBM] * n + [SEM, SEM, ANY], out_specs=[HBM] * n,
        out_shape=[pltpu.HBM(l.shape, l.dtype) for l in lands],
        input_output_aliases={i: i for i in range(n)},
        compiler_params=pltpu.CompilerParams(has_side_effects=EFFECT),
    )(*lands, send_sems, recv_sems, after)


def forward_halves(lands):
    n = len(lands)

    def body(*refs):
        ins, outs = refs[:n], refs[n:2 * n]
        send_sems, recv_sems = refs[2 * n:]
        x, y, c, chips = _place()
        sibling = (x, y, 1 - c)
        sends = []
        for k in range(n):
            hr = ins[k].shape[1] // 2
            mine = pl.ds(pl.multiple_of(c * hr, 8), hr)
            for t, (px, py) in enumerate(chips):
                cp = _remote(ins[k].at[2 * px + py, mine, :], outs[k].at[2 * px + py, mine, :],
                             send_sems.at[k, t], recv_sems.at[k, t], sibling)
                cp.start()
                sends.append(cp)
        for k in range(n):
            hr = ins[k].shape[1] // 2
            other = pl.ds(pl.multiple_of((1 - c) * hr, 8), hr)
            for t, (px, py) in enumerate(chips):
                dst = outs[k].at[2 * px + py, other, :]
                _remote(dst, dst, send_sems.at[k, t], recv_sems.at[k, t], sibling).wait_recv()
        for cp in sends:
            cp.wait_send()

    return pl.pallas_call(
        body, name="forward_halves", in_specs=[HBM] * n, out_specs=[HBM] * n,
        out_shape=[jax.ShapeDtypeStruct(l.shape, l.dtype) for l in lands],
        input_output_aliases={i: i for i in range(n)},
        scratch_shapes=[pltpu.SemaphoreType.DMA((n, 3)), pltpu.SemaphoreType.DMA((n, 3))],
    )(*lands)


def gather_small(block):
    def body(in_ref, out_ref, send_sems, recv_sems):
        x, y, c, chips = _place()
        me = 2 * x + y
        out_ref[me] = in_ref[...]
        sends = []
        for t, (px, py) in enumerate(chips):
            cp = _remote(in_ref, out_ref.at[me], send_sems.at[t], recv_sems.at[t], (px, py, c))
            cp.start()
            sends.append(cp)
        for t, (px, py) in enumerate(chips):
            landed = out_ref.at[2 * px + py]
            _remote(landed, landed, send_sems.at[t], recv_sems.at[t], (px, py, c)).wait_recv()
        for cp in sends:
            cp.wait_send()

    return pl.pallas_call(
        body, name="gather_small", in_specs=[VMEM_SPEC], out_specs=VMEM_SPEC,
        out_shape=jax.ShapeDtypeStruct((N_CHIPS,) + block.shape, block.dtype),
        scratch_shapes=[pltpu.SemaphoreType.DMA((3,)), pltpu.SemaphoreType.DMA((3,))],
    )(block)


def exchange_sibling_halves(grads):
    n = len(grads)

    def body(*refs):
        ins, outs = refs[:n], refs[n:2 * n]
        send_sems, recv_sems = refs[2 * n:]
        x, y, c, _ = _place()
        copies = []
        for k in range(n):
            hr = ins[k].shape[1] // 2
            theirs = pl.ds(pl.multiple_of((1 - c) * hr, 8), hr)
            cp = _remote(ins[k].at[:, theirs, :], outs[k], send_sems.at[k], recv_sems.at[k], (x, y, 1 - c))
            cp.start()
            copies.append(cp)
        for cp in copies:
            cp.wait()

    return pl.pallas_call(
        body, name="exchange_sibling_halves", in_specs=[HBM] * n, out_specs=[HBM] * n,
        out_shape=[jax.ShapeDtypeStruct((g.shape[0], g.shape[1] // 2, g.shape[2]), g.dtype) for g in grads],
        scratch_shapes=[pltpu.SemaphoreType.DMA((n,)), pltpu.SemaphoreType.DMA((n,))],
    )(*grads)


def _sibling_half_copies(grads, lands, send_sems, recv_sems):
    x, y, c, _ = _place()
    copies = []
    for k in range(len(grads)):
        hr = grads[k].shape[1] // 2
        theirs = pl.ds(pl.multiple_of((1 - c) * hr, 8), hr)
        copies.append(_remote(grads[k].at[:, theirs, :], lands[k], send_sems.at[k], recv_sems.at[k], (x, y, 1 - c)))
    return copies


def _sibling_whole_copies(srcs, lands, send_sems, recv_sems):
    x, y, c, _ = _place()
    return [_remote(srcs[k], lands[k], send_sems.at[k], recv_sems.at[k], (x, y, 1 - c)) for k in range(len(srcs))]


def pair_start(name, make_copies, srcs, land_shapes):
    n = len(srcs)

    def body(*refs):
        src_refs, land_refs = refs[:n], refs[n:2 * n]
        send_sems, recv_sems = refs[2 * n], refs[2 * n + 1]
        token = refs[-1]
        for cp in make_copies(src_refs, land_refs, send_sems, recv_sems):
            cp.start()
        token[...] = jnp.zeros_like(token)

    lands = [lax.empty(shape, s.dtype) for shape, s in zip(land_shapes, srcs)]
    outs = pl.pallas_call(
        body, name=name, in_specs=[HBM] * (2 * n), out_specs=[SEM, SEM] + [HBM] * (2 * n) + [VMEM_SPEC],
        out_shape=[pltpu.SemaphoreType.DMA((n,)), pltpu.SemaphoreType.DMA((n,))]
        + [pltpu.HBM(s.shape, s.dtype) for s in srcs] + [pltpu.HBM(l.shape, l.dtype) for l in lands] + [TOKEN],
        input_output_aliases={i: 2 + i for i in range(2 * n)},
        compiler_params=pltpu.CompilerParams(has_side_effects=EFFECT),
    )(*[_in_hbm(s) for s in srcs], *[_in_hbm(l) for l in lands])
    return outs[0], outs[1], outs[2:2 + n], outs[2 + n:2 + 2 * n], outs[-1]


def pair_wait_one(name, send_sems, recv_sems, src, land, after, index):
    def body(src_ref, land_ref, send, recv, after_ref, src_out, land_out):
        x, y, c, _ = _place()
        cp = _remote(src_ref, land_ref, send.at[index], recv.at[index], (x, y, 1 - c))
        cp.wait_send()
        cp.wait_recv()

    return pl.pallas_call(
        body, name=name, in_specs=[HBM, HBM, SEM, SEM, ANY], out_specs=[HBM, HBM],
        out_shape=[pltpu.HBM(src.shape, src.dtype), pltpu.HBM(land.shape, land.dtype)],
        input_output_aliases={0: 0, 1: 1},
        compiler_params=pltpu.CompilerParams(has_side_effects=EFFECT),
    )(src, land, send_sems, recv_sems, after)


def pair_wait(name, make_copies, send_sems, recv_sems, srcs, lands, after):
    n = len(srcs)

    def body(*refs):
        src_refs, land_refs = refs[:n], refs[n:2 * n]
        for cp in make_copies(src_refs, land_refs, refs[2 * n], refs[2 * n + 1]):
            cp.wait_send()
            cp.wait_recv()

    outs = pl.pallas_call(
        body, name=name, in_specs=[HBM] * (2 * n) + [SEM, SEM, ANY], out_specs=[HBM] * (2 * n),
        out_shape=[pltpu.HBM(s.shape, s.dtype) for s in srcs] + [pltpu.HBM(l.shape, l.dtype) for l in lands],
        input_output_aliases={i: i for i in range(2 * n)},
        compiler_params=pltpu.CompilerParams(has_side_effects=EFFECT),
    )(*srcs, *lands, send_sems, recv_sems, after)
    return outs[:n], outs[n:]


def _chip_copies(parts, lands, send_sems, recv_sems):
    x, y, c, chips = _place()
    return [_remote(parts[k].at[2 * px + py], lands[k].at[t], send_sems.at[3 * k + t], recv_sems.at[3 * k + t], (px, py, c))
            for k in range(len(parts)) for t, (px, py) in enumerate(chips)]


def chip_parts_start(parts):
    n = len(parts)

    def body(*refs):
        srcs, lands_in = refs[:n], refs[n:2 * n]
        send_sems, recv_sems = refs[2 * n], refs[2 * n + 1]
        token = refs[-1]
        for cp in _chip_copies(srcs, lands_in, send_sems, recv_sems):
            cp.start()
        token[...] = jnp.zeros_like(token)

    lands = [lax.empty((3,) + p.shape[1:], p.dtype) for p in parts]
    outs = pl.pallas_call(
        body, name="chip_parts_start", in_specs=[HBM] * (2 * n), out_specs=[SEM, SEM] + [HBM] * (2 * n) + [VMEM_SPEC],
        out_shape=[pltpu.SemaphoreType.DMA((3 * n,)), pltpu.SemaphoreType.DMA((3 * n,))]
        + [pltpu.HBM(p.shape, p.dtype) for p in parts] + [pltpu.HBM(l.shape, l.dtype) for l in lands] + [TOKEN],
        input_output_aliases={i: 2 + i for i in range(2 * n)},
        compiler_params=pltpu.CompilerParams(has_side_effects=EFFECT),
    )(*[_in_hbm(p) for p in parts], *[_in_hbm(l) for l in lands])
    return outs[0], outs[1], outs[2:2 + n], outs[2 + n:2 + 2 * n], outs[-1]


def chip_parts_wait(send_sems, recv_sems, parts, lands, after):
    n = len(parts)

    def body(*refs):
        srcs, lands_in = refs[:n], refs[n:2 * n]
        send, recv = refs[2 * n], refs[2 * n + 1]
        for cp in _chip_copies(srcs, lands_in, send, recv):
            cp.wait_send()
            cp.wait_recv()

    outs = pl.pallas_call(
        body, name="chip_parts_wait", in_specs=[HBM] * (2 * n) + [SEM, SEM, ANY], out_specs=[HBM] * (2 * n),
        out_shape=[pltpu.HBM(p.shape, p.dtype) for p in parts] + [pltpu.HBM(l.shape, l.dtype) for l in lands],
        input_output_aliases={i: i for i in range(2 * n)},
        compiler_params=pltpu.CompilerParams(has_side_effects=EFFECT),
    )(*parts, *lands, send_sems, recv_sems, after)
    return outs[:n], outs[n:]


def allreduce_small(packed):
    R = packed.shape[0]

    def body(x_ref, sum_ref, all_ref, send_sems, recv_sems):
        x, y, c, chips = _place()
        me, sibling = (x, y, c), (x, y, 1 - c)

        def rows(px, py, pc):
            return all_ref.at[4 * px + 2 * py + pc]

        def copy(k, block, to, src=None):
            return _remote(rows(*block) if src is None else src, rows(*block), send_sems.at[k], recv_sems.at[k], to)

        all_ref[4 * x + 2 * y + c] = x_ref[...]
        first = [copy(0, me, sibling, src=x_ref)]
        first += [copy(1 + j, me, (*chip, c), src=x_ref) for j, chip in enumerate(chips)]
        for cp in first:
            cp.start()
        passed = [copy(4 + j, (*chip, c), sibling) for j, chip in enumerate(chips)]
        for j, chip in enumerate(chips):
            copy(1 + j, (*chip, c), me).wait_recv()
            passed[j].start()
        copy(0, sibling, me).wait_recv()
        for j, chip in enumerate(chips):
            copy(4 + j, (*chip, 1 - c), me).wait_recv()
        for cp in first + passed:
            cp.wait_send()

        def chunk(i, carry):
            rws = pl.ds(pl.multiple_of(i * PACK_ROWS, PACK_ROWS), PACK_ROWS)
            acc = all_ref[0, rws, :]
            for d in range(1, N_DEV):
                acc = acc + all_ref[d, rws, :]
            sum_ref[rws, :] = acc
            return carry

        lax.fori_loop(0, R // PACK_ROWS, chunk, 0)

    return pl.pallas_call(
        body, name="allreduce_small", in_specs=[VMEM_SPEC], out_specs=VMEM_SPEC,
        out_shape=jax.ShapeDtypeStruct((R, LANES), F32),
        scratch_shapes=[pltpu.VMEM((N_DEV, R, LANES), F32), pltpu.SemaphoreType.DMA((7,)), pltpu.SemaphoreType.DMA((7,))],
        compiler_params=pltpu.CompilerParams(vmem_limit_bytes=VMEM_LIMIT),
    )(packed)


def _mixer_fwd(x, h, p, tabs, token, late_weights=None):
    z = mm_nn_cols(h, p["w_in"], token)
    qr, kp, vp = rope_fwd(z, tabs)
    mix = attn_fwd(qr, kp, vp, p["sink3"])
    c1 = conv_dw_fwd(z, p["conv_w32"], p["conv_dw_b"])
    mix = conv_ln_fwd(c1, p["conv_ln_g"], p["conv_ln_b"], mix)
    mix = sgu_fwd(z, p["sgu_ln_g"], p["sgu_ln_b"], p["sgu_w16"], p["sgu_b3"], mix)
    if late_weights is not None:
        p.update(late_weights(mix))
    x_mid, h2 = mm_nn_rows_res(mix, p["w_out"], x, p["ffn_norm_g"])
    return x_mid, h2, dict(x=x, h=h, z=z, qr=qr, kp=kp, vp=vp, c1=c1, mix=mix, x_mid=x_mid)


def _ffn_fwd(x_mid, h2, p, next_gain, token):
    gate, up, act = ffn_up(h2, p["w_gate"], p["w_up"], token)
    x_out, h_next = mm_nn_rows_res(act, p["w_down"], x_mid, next_gain)
    return x_out, h_next, dict(h2=h2, gate=gate, up=up, act=act)


def _layer_fwd(x, h, p, next_gain, tabs, token):
    x_mid, h2, s_mix = _mixer_fwd(x, h, p, tabs, token)
    x_out, h_next, s_ffn = _ffn_fwd(x_mid, h2, p, next_gain, token)
    return x_out, h_next, {**s_mix, **s_ffn}


def _ffn_bwd(dxb, p, s, token):
    dgate, dup = ffn_down_bwd(dxb, p["w_down"], s["gate"], s["up"], token)
    g_down = mm_tn_rows(s["act"], dxb)
    dh2 = mm_nt_cols([(dgate, p["w_gate"]), (dup, p["w_up"])], BF16, 1)
    g_gate = mm_tn_cols(s["h2"], dgate, N_CHIPS)
    g_up = mm_tn_cols(s["h2"], dup, N_CHIPS)
    dmidb, g_ffn_norm = rms_bwd(s["x_mid"], p["ffn_norm_g"], dh2, dxb, BF16)
    return dmidb, [g_gate, g_up, g_down.reshape(N_CHIPS, -1, D_MODEL)], g_ffn_norm


def _mixer_bwd(dmidb, p, s, tabs, token, out_dtype):
    dmix = mm_nt_rows(dmidb, p["w_out"], token)
    g_out = mm_tn_rows(s["mix"], dmidb)
    dq, dkp, dvp, dsink = attn_bwd(s["qr"], s["kp"], s["vp"], p["sink3"], dmix)
    dz = rope_bwd(dq, dkp, dvp, tabs)
    dc1, g_cln_g, g_cln_b = conv_ln_bwd(dmix, s["c1"], p["conv_ln_g"], p["conv_ln_b"])
    dz, g_cw, g_cb = conv_dw_bwd(dc1, s["z"], p["conv_w32"], dz)
    dz, g_sw, g_sb, g_sln_g, g_sln_b = sgu_bwd(s["z"], dmix, p["sgu_ln_g"], p["sgu_ln_b"], p["sgu_w16"], p["sgu_b3"], dz)
    dh = mm_nt_cols([(dz, p["w_in"])], BF16, N_CHIPS)
    g_in = mm_tn_cols(s["h"], dz, N_CHIPS)
    dx_in, g_mix_norm = rms_bwd(s["x"], p["mix_norm_g"], dh, dmidb, out_dtype)
    small = dict(mix_norm_g=g_mix_norm, sink=dsink[:, :, 0].reshape(1, N_Q_HEADS), conv_dw_w=g_cw[:CONV_KERNEL],
                 conv_dw_b=g_cb, conv_ln_g=g_cln_g, conv_ln_b=g_cln_b, sgu_ln_g=g_sln_g, sgu_ln_b=g_sln_b,
                 sgu_w=g_sw, sgu_b=g_sb[:, :, 0])
    return dx_in, [g_in, g_out.reshape(N_CHIPS, -1, D_MODEL)], small


def _layer_bwd(dxb, p, s, tabs, token, out_dtype):
    dmidb, ffn_big, g_ffn_norm = _ffn_bwd(dxb, p, s, token)
    dx_in, mix_big, small = _mixer_bwd(dmidb, p, s, tabs, token, out_dtype)
    return dx_in, mix_big + ffn_big, dict(small, ffn_norm_g=g_ffn_norm)


def _mixer_weights(gathered):
    w_in, w_out = gathered
    return dict(w_in=w_in, w_out=w_out.reshape(-1, D_MODEL))


def _ffn_weights(gathered):
    w_gate, w_up, w_down = gathered
    return dict(w_gate=w_gate, w_up=w_up, w_down=w_down.reshape(-1, D_MODEL))


def _small_params(l, conv_w_full, mix_norm_g, sink, conv_dw_b, conv_ln_g, conv_ln_b, sgu_ln_g, sgu_ln_b, sgu_w, sgu_b,
                  ffn_norm_g):
    return dict(
        mix_norm_g=mix_norm_g[l:l + 1], ffn_norm_g=ffn_norm_g[l:l + 1],
        sink3=jnp.broadcast_to(sink[l].reshape(N_KV_HEADS, Q_PER_KV, 1), (N_KV_HEADS, Q_PER_KV, LANES)),
        conv_w32=jnp.pad(conv_w_full[l], ((0, 32 - CONV_KERNEL), (0, 0))),
        conv_dw_b=conv_dw_b[l:l + 1], conv_ln_g=conv_ln_g[l:l + 1], conv_ln_b=conv_ln_b[l:l + 1],
        sgu_ln_g=sgu_ln_g[l:l + 1], sgu_ln_b=sgu_ln_b[l:l + 1], sgu_w16=sgu_w[l].astype(BF16),
        sgu_b3=jnp.broadcast_to(sgu_b[l][:, :, None], (SGU_HEADS, CHUNK, CHUNK)))


_SMALL = ["mix_norm_g", "sink", "conv_dw_b", "conv_ln_g", "conv_ln_b", "sgu_ln_g", "sgu_ln_b", "sgu_w", "sgu_b", "ffn_norm_g",
          "final_norm_g"]


def _pack_rows(arrays):
    rows, counts = [], []
    for a in arrays:
        flat = a.reshape(-1)
        n = -(-flat.shape[0] // LANES)
        rows.append(jnp.pad(flat, (0, n * LANES - flat.shape[0])).reshape(n, LANES))
        counts.append(n)
    packed = jnp.concatenate(rows, axis=0)
    pad = -packed.shape[0] % PACK_ROWS
    return jnp.pad(packed, ((0, pad), (0, 0))), counts


def _unpack_rows(packed, counts, shapes):
    out, r = [], 0
    for n, shape in zip(counts, shapes):
        size = math.prod(shape)
        out.append(packed[r:r + n].reshape(-1)[:size].reshape(shape))
        r += n
    return out


def kernel(x, mix_norm_g, w_in, sink, conv_dw_w, conv_dw_b, conv_ln_g, conv_ln_b, sgu_ln_g, sgu_ln_b, sgu_w, sgu_b, w_out, ffn_norm_g, w_gate, w_up, w_down, final_norm_g, loss_target, m_mix_norm_g, m_w_in, m_sink, m_conv_dw_w, m_conv_dw_b, m_conv_ln_g, m_conv_ln_b, m_sgu_ln_g, m_sgu_ln_b, m_sgu_w, m_sgu_b, m_w_out, m_ffn_norm_g, m_w_gate, m_w_up, m_w_down, m_final_norm_g, v_mix_norm_g, v_w_in, v_sink, v_conv_dw_w, v_conv_dw_b, v_conv_ln_g, v_conv_ln_b, v_sgu_ln_g, v_sgu_ln_b, v_sgu_w, v_sgu_b, v_w_out, v_ffn_norm_g, v_w_gate, v_w_up, v_w_down, v_final_norm_g):
    S = x.shape[1]
    my_chip = 2 * lax.axis_index("x") + lax.axis_index("y")
    c_idx = lax.axis_index("c").astype(jnp.int32).reshape(1)
    big_w = [w_in, w_out, w_gate, w_up, w_down]
    big_m = [m_w_in, m_w_out, m_w_gate, m_w_up, m_w_down]
    big_v = [v_w_in, v_w_out, v_w_gate, v_w_up, v_w_down]
    n_kinds = len(big_w)

    x_idx = lax.axis_index("x").astype(jnp.int32).reshape(1)
    y_idx = lax.axis_index("y").astype(jnp.int32).reshape(1)
    conv_w_all = gather_small(conv_dw_w)
    conv_w_full = jnp.transpose(conv_w_all, (1, 2, 0, 3)).reshape(DEPTH, CONV_KERNEL, CONV_WIDTH)
    tabs = rope_tables(S)
    no_token = jnp.zeros(TOKEN.shape, TOKEN.dtype)

    mixer_kinds, ffn_kinds = [0, 1], [2, 3, 4]
    def shards(layer, kinds, token):
        return [cast_layer(big_w[k], layer, token) for k in kinds]

    def fetch(pending, after):
        send_sems, recv_sems, srcs, lands, _ = pending
        return forward_halves(gather_wait(send_sems, recv_sems, srcs, lands, after))

    all_kinds = mixer_kinds + ffn_kinds
    first_mixer = gather_start(shards(0, mixer_kinds, no_token), conv_w_all)
    first_ffn = gather_start(shards(0, ffn_kinds, first_mixer[4]), first_mixer[4])
    pending = gather_start(shards(1, all_kinds, first_ffn[4]), first_ffn[4])
    act = x[0]
    h = rms_fwd(act, mix_norm_g[0:1], no_token)
    saved, params = [], []
    for l in range(DEPTH):
        p = _small_params(l, conv_w_full, mix_norm_g, sink, conv_dw_b, conv_ln_g, conv_ln_b, sgu_ln_g, sgu_ln_b, sgu_w, sgu_b,
                          ffn_norm_g)
        next_gain = mix_norm_g[l + 1:l + 2] if l + 1 < DEPTH else final_norm_g.reshape(1, D_MODEL)
        if l == 0:
            p.update(_mixer_weights(fetch(first_mixer, act)))
            x_mid, h2, s_mix = _mixer_fwd(act, h, p, tabs, pending[4])
            p.update(_ffn_weights(fetch(first_ffn, x_mid)))
            late, token = None, no_token
        else:
            send_sems, recv_sems, srcs, lands, _ = pending
            lands = gather_wait(send_sems, recv_sems, srcs, lands, act)
            w_in_full = forward_halves(lands[:1])[0]
            p.update(w_in=w_in_full)
            fwd_send, fwd_recv, rest, token = forward_start(lands[1:])

            def late(mix, fwd_send=fwd_send, fwd_recv=fwd_recv, rest=rest):
                w_out_full, *ffn_full = forward_wait(fwd_send, fwd_recv, rest, mix)
                return dict(_ffn_weights(ffn_full), w_out=w_out_full.reshape(-1, D_MODEL))

            if l + 1 < DEPTH:
                pending = gather_start(shards(l + 1, all_kinds, token), w_in_full)
                token = token + pending[4]
        if l > 0:
            x_mid, h2, s_mix = _mixer_fwd(act, h, p, tabs, token, late)
        act, h, s_ffn = _ffn_fwd(x_mid, h2, p, next_gain, token)
        params.append(p)
        saved.append({**s_mix, **s_ffn})
    loss_part, dxb, g_final = final_loss(act, final_norm_g.reshape(1, D_MODEL), loss_target[0])
    loss = lax.psum(loss_part[0, 0], ("x", "y", "c"))

    halves = [lax.empty((DEPTH, w.shape[1] // 2, w.shape[2]), F32) for w in big_w]
    small_grads = [None] * DEPTH

    def chip_start(layer, kinds, grads, recv):
        chip_sum = [add_sibling_half(g, r, c_idx) for g, r in zip(grads, recv)]
        send_sems, recv_sems, parts, lands, token = chip_parts_start(chip_sum)
        return (layer, kinds, send_sems, recv_sems, parts, lands), token

    def reduce_start(layer, kinds, grads):
        return chip_start(layer, kinds, grads, exchange_sibling_halves(grads))

    def reduce_finish(pending, halves, after):
        layer, kinds, send_sems, recv_sems, parts, lands = pending
        parts, others = chip_parts_wait(send_sems, recv_sems, parts, lands, after)
        halves = list(halves)
        for i, k in enumerate(kinds):
            halves[k] = sum_chips(parts[i], others[i], halves[k], x_idx, y_idx, layer)
        return halves

    pending, token = None, no_token
    for l in reversed(range(DEPTH)):
        dmidb, ffn_big, g_ffn_norm = _ffn_bwd(dxb, params[l], saved[l], token)
        if l == 0:
            last_ffn, mixer_token = reduce_start(l, ffn_kinds, ffn_big)
        else:
            half_shapes = [(g.shape[0], g.shape[1] // 2, g.shape[2]) for g in ffn_big]
            sib_send, sib_recv, ffn_big, ffn_lands, mixer_token = pair_start("sibling_start", _sibling_half_copies, ffn_big, half_shapes)
        dxb, mix_big, small = _mixer_bwd(dmidb, params[l], saved[l], tabs, mixer_token, F32 if l == 0 else BF16)
        small_grads[l] = dict(small, ffn_norm_g=g_ffn_norm)
        if pending is not None:
            halves = reduce_finish(pending, halves, dxb)
        if l == 0:
            last_mixer, token = reduce_start(l, mixer_kinds, mix_big)
            halves = reduce_finish(last_ffn, halves, token)
        else:
            ffn_big, ffn_recv = pair_wait("sibling_wait", _sibling_half_copies, sib_send, sib_recv, ffn_big, ffn_lands, dxb)
            mix_recv = exchange_sibling_halves(mix_big)
            pending, token = chip_start(l, mixer_kinds + ffn_kinds, list(mix_big) + list(ffn_big), list(mix_recv) + list(ffn_recv))

    def final_start(kinds):
        send_sems, recv_sems, mine, lands, _ = pair_start("final_start", _sibling_whole_copies, [halves[k] for k in kinds],
                                                          [halves[k].shape for k in kinds])
        return send_sems, recv_sems, mine, lands

    ffn_final = final_start(ffn_kinds)

    stacked = {n: jnp.stack([small_grads[l][n] for l in range(DEPTH)]) for n in small_grads[0]}
    stacked["final_norm_g"] = g_final
    packed, counts = _pack_rows([stacked[n] for n in _SMALL] + [stacked["conv_dw_w"]])
    reduced = allreduce_small(packed)
    small_w = dict(mix_norm_g=mix_norm_g, sink=sink, conv_dw_b=conv_dw_b, conv_ln_g=conv_ln_g, conv_ln_b=conv_ln_b,
                   sgu_ln_g=sgu_ln_g, sgu_ln_b=sgu_ln_b, sgu_w=sgu_w, sgu_b=sgu_b, ffn_norm_g=ffn_norm_g,
                   final_norm_g=final_norm_g)
    small_m = dict(mix_norm_g=m_mix_norm_g, sink=m_sink, conv_dw_b=m_conv_dw_b, conv_ln_g=m_conv_ln_g,
                   conv_ln_b=m_conv_ln_b, sgu_ln_g=m_sgu_ln_g, sgu_ln_b=m_sgu_ln_b, sgu_w=m_sgu_w, sgu_b=m_sgu_b,
                   ffn_norm_g=m_ffn_norm_g, final_norm_g=m_final_norm_g)
    small_v = dict(mix_norm_g=v_mix_norm_g, sink=v_sink, conv_dw_b=v_conv_dw_b, conv_ln_g=v_conv_ln_g,
                   conv_ln_b=v_conv_ln_b, sgu_ln_g=v_sgu_ln_g, sgu_ln_b=v_sgu_ln_b, sgu_w=v_sgu_w, sgu_b=v_sgu_b,
                   ffn_norm_g=v_ffn_norm_g, final_norm_g=v_final_norm_g)
    shapes = [small_w[n].shape for n in _SMALL] + [(DEPTH, CONV_KERNEL, CONV_WIDTH)]
    red = _unpack_rows(reduced, counts, shapes)
    g_small = dict(zip(_SMALL, red[:-1]))
    g_small["conv_dw_w"] = lax.dynamic_slice_in_dim(red[-1], my_chip * LANES, LANES, axis=2)
    small_w["conv_dw_w"], small_m["conv_dw_w"], small_v["conv_dw_w"] = conv_dw_w, m_conv_dw_w, v_conv_dw_w
    names = _SMALL + ["conv_dw_w"]
    pw, cnt = _pack_rows([small_w[n] for n in names])
    pg, _ = _pack_rows([g_small[n] for n in names])
    pm, _ = _pack_rows([small_m[n] for n in names])
    pv, _ = _pack_rows([small_v[n] for n in names])
    sd, sm, sv = adamw(pw, pg, pm, pv)
    shp = [small_w[n].shape for n in names]
    d_small = dict(zip(names, _unpack_rows(sd, cnt, shp)))
    m_small = dict(zip(names, _unpack_rows(sm, cnt, shp)))
    v_small = dict(zip(names, _unpack_rows(sv, cnt, shp)))

    big_names = ["w_in", "w_out", "w_gate", "w_up", "w_down"]
    g_big, d_big, m_big, v_big = {}, {}, {}, {}
    after = sd
    for kinds, final in ((ffn_kinds, ffn_final), (mixer_kinds, None)):
        if final is None:
            halves = reduce_finish(last_mixer, halves, after)
            final = final_start(kinds)
        send_sems, recv_sems, sent, lands = final
        for i, k in enumerate(kinds):
            n = big_names[k]
            mine, theirs = pair_wait_one("final_wait", send_sems, recv_sems, sent[i], lands[i], after, i)
            g_big[n], d_big[n], m_big[n], v_big[n] = adamw_halves(big_w[k], mine, theirs, big_m[k], big_v[k], c_idx)
            after = d_big[n]

    order = ["mix_norm_g", "w_in", "sink", "conv_dw_w", "conv_dw_b", "conv_ln_g", "conv_ln_b", "sgu_ln_g", "sgu_ln_b",
             "sgu_w", "sgu_b", "w_out", "ffn_norm_g", "w_gate", "w_up", "w_down", "final_norm_g"]
    grads = {**g_small, **g_big}
    deltas = {**d_small, **d_big}
    new_m = {**m_small, **m_big}
    new_v = {**v_small, **v_big}
    return (loss, dxb[None], *[grads[n] for n in order], *[deltas[n] for n in order],
            *[new_m[n] for n in order], *[new_v[n] for n in order])
```

```python
import math

import jax
import jax.numpy as jnp
from jax import lax
from jax.experimental import pallas as pl
from jax.experimental.pallas import tpu as pltpu

F32, BF16 = jnp.float32, jnp.bfloat16

D_MODEL = 2048
DEPTH = 4
HEAD_DIM = 128
N_Q_HEADS = 8
N_KV_HEADS = 2
Q_PER_KV = N_Q_HEADS // N_KV_HEADS
ATTN_WIDTH = N_Q_HEADS * HEAD_DIM
KV_WIDTH = N_KV_HEADS * HEAD_DIM
CONV_WIDTH = 512
CONV_KERNEL = 31
CONV_PAD = 16
SGU_WIDTH = 512
SGU_HEADS = 4
CHUNK = 128
IN_WIDTH = 3584
D_FF = 5632
WINDOW = 128
ROT_DIM = 32
ROPE_THETA = 500000.0
EPS = 1e-6
N_CHIPS = 4
N_DEV = 8
LANES = 128
PACK_ROWS = 64
OFF_K = ATTN_WIDTH
OFF_V = OFF_K + KV_WIDTH
OFF_CA = OFF_V + KV_WIDTH
OFF_CG = OFF_CA + CONV_WIDTH
OFF_U = OFF_CG + CONV_WIDTH
OFF_VV = OFF_U + SGU_WIDTH

ADAM_LR, ADAM_B1, ADAM_B2, ADAM_EPS, ADAM_WD, ADAM_STEP = 0.001, 0.9, 0.999, 1e-08, 0.01, 10

VMEM_LIMIT = 56 * 1024 * 1024
MESH = pl.DeviceIdType.MESH
HBM = pl.BlockSpec(memory_space=pltpu.HBM)
VMEM_SPEC = pl.BlockSpec(memory_space=pltpu.VMEM)


def _call(name, body, *, grid, in_specs, out_specs, out_shape, scratch=(), sem=None, aliases=None):
    params = dict(vmem_limit_bytes=VMEM_LIMIT)
    if sem is not None:
        params["dimension_semantics"] = sem
    return pl.pallas_call(
        body, name=name, grid=grid, in_specs=in_specs, out_specs=out_specs, out_shape=out_shape,
        scratch_shapes=list(scratch), input_output_aliases=aliases or {}, compiler_params=pltpu.CompilerParams(**params))


def _sigmoid(x):
    return 1.0 / (1.0 + jnp.exp(-x))


def rms_fwd(x, g, token):
    S = x.shape[0]
    tm = min(512, S)

    def body(x_ref, g_ref, token_ref, o_ref):
        xv = x_ref[...]
        r = lax.rsqrt(jnp.mean(xv * xv, axis=-1, keepdims=True) + EPS)
        o_ref[...] = (xv * r * g_ref[...]).astype(BF16)

    return _call("rms_fwd", body, grid=(S // tm,),
                 in_specs=[pl.BlockSpec((tm, D_MODEL), lambda i: (i, 0)), pl.BlockSpec((1, D_MODEL), lambda i: (0, 0)),
                           pl.BlockSpec((8, LANES), lambda i: (0, 0))],
                 out_specs=pl.BlockSpec((tm, D_MODEL), lambda i: (i, 0)),
                 out_shape=jax.ShapeDtypeStruct((S, D_MODEL), BF16), sem=("parallel",))(x, g, token)


def _rms_bwd_math(xv, gv, dh):
    r = lax.rsqrt(jnp.mean(xv * xv, axis=-1, keepdims=True) + EPS)
    n = xv * r
    dn = dh * gv
    dx = r * (dn - n * jnp.mean(dn * n, axis=-1, keepdims=True))
    dg = jnp.sum(dh * n, axis=0, keepdims=True)
    return dx, dg


def rms_bwd(x, g, dh, dres, out_dtype):
    S = x.shape[0]
    tm = min(512, S)

    def body(x_ref, g_ref, dh_ref, dres_ref, dx_ref, dg_ref):
        dx, dg = _rms_bwd_math(x_ref[...], g_ref[...], dh_ref[...].astype(F32))
        dx_ref[...] = (dx + dres_ref[...].astype(F32)).astype(out_dtype)

        @pl.when(pl.program_id(0) == 0)
        def _():
            dg_ref[...] = dg

        @pl.when(pl.program_id(0) > 0)
        def _():
            dg_ref[...] += dg

    row = pl.BlockSpec((tm, D_MODEL), lambda i: (i, 0))
    vec = pl.BlockSpec((1, D_MODEL), lambda i: (0, 0))
    return _call("rms_bwd", body, grid=(S // tm,), in_specs=[row, vec, row, row], out_specs=[row, vec],
                 out_shape=[jax.ShapeDtypeStruct((S, D_MODEL), out_dtype), jax.ShapeDtypeStruct((1, D_MODEL), F32)],
                 sem=("arbitrary",))(x, g, dh, dres)


def final_loss(x, g, target):
    S = x.shape[0]
    tm = min(256, S)

    def body(x_ref, g_ref, t_ref, loss_ref, dxb_ref, dg_ref):
        xv = x_ref[...]
        gv = g_ref[...]
        r = lax.rsqrt(jnp.mean(xv * xv, axis=-1, keepdims=True) + EPS)
        err = xv * r * gv - t_ref[...]
        part = 0.5 * jnp.sum(jnp.mean(err * err, axis=-1, keepdims=True), axis=0, keepdims=True)
        dx, dg = _rms_bwd_math(xv, gv, err * (1.0 / D_MODEL))
        dxb_ref[...] = dx.astype(BF16)

        @pl.when(pl.program_id(0) == 0)
        def _():
            dg_ref[...] = dg
            loss_ref[...] = part

        @pl.when(pl.program_id(0) > 0)
        def _():
            dg_ref[...] += dg
            loss_ref[...] += part

    row = pl.BlockSpec((tm, D_MODEL), lambda i: (i, 0))
    vec = pl.BlockSpec((1, D_MODEL), lambda i: (0, 0))
    one = pl.BlockSpec((1, 1), lambda i: (0, 0))
    return _call("final_loss", body, grid=(S // tm,), in_specs=[row, vec, row], out_specs=[one, row, vec],
                 out_shape=[jax.ShapeDtypeStruct((1, 1), F32), jax.ShapeDtypeStruct((S, D_MODEL), BF16),
                            jax.ShapeDtypeStruct((1, D_MODEL), F32)],
                 sem=("arbitrary",))(x, g, target)


EPILOGUE_ROWS = 256
NN = (((1,), (0,)), ((), ()))
NT = (((1,), (1,)), ((), ()))
TN = (((0,), (0,)), ((), ()))


def _matmul(name, operands, in_specs, out_shape, out_specs, grid, pairs, dims, acc_shape, epilogue):
    n_in, n_out, nk = len(operands), len(out_shape), grid[-1]

    def body(*refs):
        ins, outs = refs[:n_in], refs[n_in:n_in + n_out]
        part = None
        for ia, ib in pairs:
            d = lax.dot_general(ins[ia][...], ins[ib][...], dims, preferred_element_type=F32)
            part = d if part is None else part + d
        if nk == 1:
            epilogue(part, ins, outs)
        else:
            acc = refs[-1]
            k = pl.program_id(len(grid) - 1)

            @pl.when(k == 0)
            def _():
                acc[...] = part

            @pl.when(k > 0)
            def _():
                acc[...] += part

            @pl.when(k == nk - 1)
            def _():
                epilogue(acc[...], ins, outs)

    scratch = [pltpu.VMEM(acc_shape, F32)] if nk > 1 else []
    sem = ("parallel",) * (len(grid) - 1) + ("arbitrary",)
    return _call(name, body, grid=grid, in_specs=in_specs, out_specs=out_specs, out_shape=out_shape,
                 scratch=scratch, sem=sem)(*operands)


def _store(dtype):
    def epilogue(acc, ins, outs):
        outs[0][...] = acc.astype(dtype)
    return epilogue


def mm_nn_cols(a, w, token):
    S, K = a.shape
    J, _, Ns = w.shape
    tm = min(512, S)
    return _matmul("mm_nn_cols", (a, w, token),
                   [pl.BlockSpec((tm, K), lambda j, i, k: (i, 0)), pl.BlockSpec((None, K, Ns), lambda j, i, k: (j, 0, 0)),
                    pl.BlockSpec((8, LANES), lambda j, i, k: (0, 0))],
                   [jax.ShapeDtypeStruct((S, J * Ns), BF16)], [pl.BlockSpec((tm, Ns), lambda j, i, k: (i, j))],
                   (J, S // tm, 1), [(0, 1)], NN, None, _store(BF16))[0]


def ffn_up(h, wg, wu, token):
    S, K = h.shape
    J, _, Ns = wg.shape
    tm = min(512, S)

    sub = min(EPILOGUE_ROWS, tm)

    def body(h_ref, wg_ref, wu_ref, token_ref, g_ref, u_ref, a_ref):
        for r in range(tm // sub):
            rows = slice(r * sub, (r + 1) * sub)
            hv = h_ref[rows, :]
            gv = jnp.dot(hv, wg_ref[...], preferred_element_type=F32)
            uv = jnp.dot(hv, wu_ref[...], preferred_element_type=F32)
            g_ref[rows, :] = gv.astype(BF16)
            u_ref[rows, :] = uv.astype(BF16)
            a_ref[rows, :] = (gv * _sigmoid(gv) * uv).astype(BF16)

    wspec = pl.BlockSpec((None, K, Ns), lambda j, i: (j, 0, 0))
    ospec = pl.BlockSpec((tm, Ns), lambda j, i: (i, j))
    oshape = jax.ShapeDtypeStruct((S, J * Ns), BF16)
    return _call("ffn_up", body, grid=(J, S // tm),
                 in_specs=[pl.BlockSpec((tm, K), lambda j, i: (i, 0)), wspec, wspec, pl.BlockSpec((8, LANES), lambda j, i: (0, 0))],
                 out_specs=[ospec, ospec, ospec], out_shape=[oshape, oshape, oshape],
                 sem=("parallel", "parallel"))(h, wg, wu, token)


def mm_nn_rows_res(a, w, res, gain):
    S, K = a.shape
    N = w.shape[1]
    tm = min(512, S)
    tk, tn = (K, N) if K <= 2048 else (K // 2, N // 2)
    n_n, n_k = N // tn, K // tk

    def body(a_ref, w_ref, res_ref, g_ref, x_ref, h_ref, *acc):
        n, k = pl.program_id(1), pl.program_id(2)

        def normed(xv):
            r = lax.rsqrt(jnp.mean(xv * xv, axis=-1, keepdims=True) + EPS)
            h_ref[...] = (xv * r * g_ref[...]).astype(BF16)

        def store_columns(total):
            if n_n == 1:
                xv = total + res_ref[...]
                x_ref[...] = xv
                normed(xv)
                return
            for c in range(n_n):
                @pl.when(n == c)
                def _(c=c):
                    cols = slice(c * tn, (c + 1) * tn)
                    x_ref[:, cols] = total + res_ref[:, cols]

            @pl.when(n == n_n - 1)
            def _():
                normed(x_ref[...])

        part = jnp.dot(a_ref[...], w_ref[...], preferred_element_type=F32)
        if n_k == 1:
            store_columns(part)
        else:
            @pl.when(k == 0)
            def _():
                acc[0][...] = part

            @pl.when(k > 0)
            def _():
                acc[0][...] += part

            @pl.when(k == n_k - 1)
            def _():
                store_columns(acc[0][...])

    row = pl.BlockSpec((tm, N), lambda i, n, k: (i, 0))
    return _call("mm_nn_rows_res", body, grid=(S // tm, n_n, n_k),
                 in_specs=[pl.BlockSpec((tm, tk), lambda i, n, k: (i, k)), pl.BlockSpec((tk, tn), lambda i, n, k: (k, n)), row,
                           pl.BlockSpec((1, N), lambda i, n, k: (0, 0))],
                 out_specs=[row, row], out_shape=[jax.ShapeDtypeStruct((S, N), F32), jax.ShapeDtypeStruct((S, N), BF16)],
                 scratch=[pltpu.VMEM((tm, tn), F32)] if n_k > 1 else [],
                 sem=("parallel", "arbitrary", "arbitrary"))(a, w, res, gain)


def mm_nt_cols(pairs_in, out_dtype, shards_per_step):
    dz0, w0 = pairs_in[0]
    S = dz0.shape[0]
    J, K, Ns = w0.shape
    tm = min(512, S)
    sps = shards_per_step
    operands, specs, pairs = [], [], []
    for dz, w in pairs_in:
        for s in range(sps):
            pairs.append((len(operands), len(operands) + 1))
            operands += [dz, w]
            specs += [pl.BlockSpec((tm, Ns), lambda i, j, s=s: (i, j * sps + s)),
                      pl.BlockSpec((None, K, Ns), lambda i, j, s=s: (j * sps + s, 0, 0))]
    return _matmul("mm_nt_cols%d" % len(pairs_in), tuple(operands), specs,
                   [jax.ShapeDtypeStruct((S, K), out_dtype)], [pl.BlockSpec((tm, K), lambda i, j: (i, 0))],
                   (S // tm, J // sps), pairs, NT, (tm, K), _store(out_dtype))[0]


def mm_nt_rows(dy, w, token):
    S, N = dy.shape
    K = w.shape[0]
    tm, tko = min(1024, S), 512
    return _matmul("mm_nt_rows", (dy, w, token),
                   [pl.BlockSpec((tm, N), lambda i, kk, z: (i, 0)), pl.BlockSpec((tko, N), lambda i, kk, z: (kk, 0)),
                    pl.BlockSpec((8, LANES), lambda i, kk, z: (0, 0))],
                   [jax.ShapeDtypeStruct((S, K), BF16)], [pl.BlockSpec((tm, tko), lambda i, kk, z: (i, kk))],
                   (S // tm, K // tko, 1), [(0, 1)], NT, None, _store(BF16))[0]


def ffn_down_bwd(dy, w, gate, up, token):
    S, N = dy.shape
    K = w.shape[0]
    tm, tko = min(1024, S), 512
    sub = min(EPILOGUE_ROWS, tm)

    def body(dy_ref, w_ref, g_ref, u_ref, token_ref, dg_ref, du_ref):
        for r in range(tm // sub):
            rows = slice(r * sub, (r + 1) * sub)
            dact = lax.dot_general(dy_ref[rows, :], w_ref[...], NT, preferred_element_type=F32)
            gv = g_ref[rows, :].astype(F32)
            uv = u_ref[rows, :].astype(F32)
            sg = _sigmoid(gv)
            dg_ref[rows, :] = (dact * uv * sg * (1.0 + gv * (1.0 - sg))).astype(BF16)
            du_ref[rows, :] = (dact * gv * sg).astype(BF16)

    tile = pl.BlockSpec((tm, tko), lambda i, kk: (i, kk))
    oshape = jax.ShapeDtypeStruct((S, K), BF16)
    return _call("ffn_down_bwd", body, grid=(S // tm, K // tko),
                 in_specs=[pl.BlockSpec((tm, N), lambda i, kk: (i, 0)), pl.BlockSpec((tko, N), lambda i, kk: (kk, 0)), tile, tile,
                           pl.BlockSpec((8, LANES), lambda i, kk: (0, 0))],
                 out_specs=[tile, tile], out_shape=[oshape, oshape], sem=("parallel", "parallel"))(dy, w, gate, up, token)


def mm_tn_cols(a, dz, J):
    S, M = a.shape
    Ns = dz.shape[1] // J
    tm, tk = 512, S
    return _matmul("mm_tn_cols", (a, dz),
                   [pl.BlockSpec((tk, tm), lambda j, m, k: (k, m)), pl.BlockSpec((tk, Ns), lambda j, m, k: (k, j))],
                   [jax.ShapeDtypeStruct((J, M, Ns), BF16)], [pl.BlockSpec((None, tm, Ns), lambda j, m, k: (j, m, 0))],
                   (J, M // tm, S // tk), [(0, 1)], TN, (tm, Ns), _store(BF16))[0]


def mm_tn_rows(a, dy):
    S, K = a.shape
    N = dy.shape[1]
    tm, tk = 512, S
    return _matmul("mm_tn_rows", (a, dy),
                   [pl.BlockSpec((tk, tm), lambda m, k: (k, m)), pl.BlockSpec((tk, N), lambda m, k: (k, 0))],
                   [jax.ShapeDtypeStruct((K, N), BF16)], [pl.BlockSpec((tm, N), lambda m, k: (m, 0))],
                   (K // tm, S // tk), [(0, 1)], TN, (tm, N), _store(BF16))[0]


def rope_tables(S):
    half = ROT_DIM // 2
    pos = jnp.arange(S, dtype=F32)
    inv = ROPE_THETA ** (-jnp.arange(0, ROT_DIM, 2, dtype=F32) / ROT_DIM)
    ang = pos[:, None] * inv[None, :]
    cos, sin = jnp.cos(ang), jnp.sin(ang)
    zeros = jnp.zeros((S, HEAD_DIM - ROT_DIM), F32)
    c = jnp.concatenate([cos, cos, jnp.ones((S, HEAD_DIM - ROT_DIM), F32)], axis=1)
    s_lo = jnp.concatenate([-sin, jnp.zeros((S, half), F32), zeros], axis=1)
    s_hi = jnp.concatenate([jnp.zeros((S, half), F32), sin, zeros], axis=1)
    return c, s_lo, s_hi


ROPE_ROWS = 512


def _rope(t, c, s_lo, s_hi):
    half = ROT_DIM // 2
    return t * c + pltpu.roll(t, HEAD_DIM - half, 1) * s_lo + pltpu.roll(t, half, 1) * s_hi


def _unrope(d, c, s_lo, s_hi):
    half = ROT_DIM // 2
    return d * c + pltpu.roll(d * s_lo, half, 1) + pltpu.roll(d * s_hi, HEAD_DIM - half, 1)


def rope_fwd(z, tabs):
    S = z.shape[0]
    T = min(ROPE_ROWS, S)

    def body(q_ref, kv_ref, c_ref, sl_ref, sh_ref, qr_ref, kp_ref, vp_ref):
        i = pl.program_id(0)

        @pl.when(i == 0)
        def _():
            zero = jnp.zeros((CHUNK, KV_WIDTH), BF16)
            kp_ref[0:CHUNK, :] = zero
            vp_ref[0:CHUNK, :] = zero
            kp_ref[S + CHUNK:S + 2 * CHUNK, :] = zero
            vp_ref[S + CHUNK:S + 2 * CHUNK, :] = zero

        c, sl, sh = c_ref[...], sl_ref[...], sh_ref[...]
        for h in range(N_Q_HEADS):
            cols = slice(h * HEAD_DIM, (h + 1) * HEAD_DIM)
            qr_ref[:, cols] = _rope(q_ref[:, cols].astype(F32), c, sl, sh).astype(BF16)
        rows = pl.ds(pl.multiple_of(CHUNK + i * T, CHUNK), T)
        for g in range(N_KV_HEADS):
            cols = slice(g * HEAD_DIM, (g + 1) * HEAD_DIM)
            kp_ref[rows, cols] = _rope(kv_ref[:, cols].astype(F32), c, sl, sh).astype(BF16)
        vp_ref[rows, :] = kv_ref[:, KV_WIDTH:2 * KV_WIDTH]

    tab = pl.BlockSpec((T, HEAD_DIM), lambda i: (i, 0))
    pad = pl.BlockSpec((S + 2 * CHUNK, KV_WIDTH), lambda i: (0, 0))
    return _call("rope_fwd", body, grid=(S // T,),
                 in_specs=[pl.BlockSpec((T, ATTN_WIDTH), lambda i: (i, 0)),
                           pl.BlockSpec((T, 2 * KV_WIDTH), lambda i: (i, OFF_K // (2 * KV_WIDTH))), tab, tab, tab],
                 out_specs=[pl.BlockSpec((T, ATTN_WIDTH), lambda i: (i, 0)), pad, pad],
                 out_shape=[jax.ShapeDtypeStruct((S, ATTN_WIDTH), BF16), jax.ShapeDtypeStruct((S + 2 * CHUNK, KV_WIDTH), BF16),
                            jax.ShapeDtypeStruct((S + 2 * CHUNK, KV_WIDTH), BF16)], sem=("arbitrary",))(z, z, *tabs)


def rope_bwd(dq, dkp, dvp, tabs):
    S = dq.shape[0]
    T = min(ROPE_ROWS, S)

    def body(dq_ref, dk_ref, dv_ref, c_ref, sl_ref, sh_ref, o_ref):
        c, sl, sh = c_ref[...], sl_ref[...], sh_ref[...]
        for h in range(N_Q_HEADS):
            cols = slice(h * HEAD_DIM, (h + 1) * HEAD_DIM)
            o_ref[:, cols] = _unrope(dq_ref[:, cols].astype(F32), c, sl, sh).astype(BF16)
        rows = pl.ds(pl.multiple_of(CHUNK + pl.program_id(0) * T, CHUNK), T)
        for g in range(N_KV_HEADS):
            cols = slice(g * HEAD_DIM, (g + 1) * HEAD_DIM)
            o_ref[:, OFF_K + g * HEAD_DIM:OFF_K + (g + 1) * HEAD_DIM] = _unrope(dk_ref[rows, cols], c, sl, sh).astype(BF16)
        o_ref[:, OFF_V:OFF_V + KV_WIDTH] = dv_ref[rows, :].astype(BF16)

    tab = pl.BlockSpec((T, HEAD_DIM), lambda i: (i, 0))
    pad = pl.BlockSpec((S + 2 * CHUNK, KV_WIDTH), lambda i: (0, 0))
    return _call("rope_bwd", body, grid=(S // T,),
                 in_specs=[pl.BlockSpec((T, ATTN_WIDTH), lambda i: (i, 0)), pad, pad, tab, tab, tab],
                 out_specs=pl.BlockSpec((T, OFF_CA), lambda i: (i, 0)),
                 out_shape=jax.ShapeDtypeStruct((S, IN_WIDTH), BF16), sem=("parallel",))(dq, dkp, dvp, *tabs)


STACK = Q_PER_KV * CHUNK


def _stack_heads(ref, rows):
    return jnp.concatenate([ref[rows, r * HEAD_DIM:(r + 1) * HEAD_DIM] for r in range(Q_PER_KV)], axis=0)


def _stack_sinks(s_ref):
    return jnp.concatenate([jnp.broadcast_to(s_ref[r:r + 1, 0:1], (CHUNK, 1)) for r in range(Q_PER_KV)], axis=0)


MASKED = -1e30


def _scores(q, kb):
    return lax.dot_general(q, kb, NT, preferred_element_type=F32) * (1.0 / math.sqrt(HEAD_DIM))


def _band_bias():
    row = lax.broadcasted_iota(jnp.int32, (STACK, 3 * CHUNK), 0) & (CHUNK - 1)
    col = lax.broadcasted_iota(jnp.int32, (STACK, 3 * CHUNK), 1)
    return jnp.where(jnp.abs(col - CHUNK - row) <= WINDOW, 0.0, MASKED).astype(F32)


def _edge_bias(n, S):
    kpos = (n - 1) * CHUNK + lax.broadcasted_iota(jnp.int32, (1, 3 * CHUNK), 1)
    return jnp.where((kpos >= 0) & (kpos < S), 0.0, MASKED).astype(F32)


def _softmax_sink(s, sk, bias):
    s = s + bias
    m = jnp.maximum(jnp.max(s, axis=1, keepdims=True), sk)
    e = jnp.exp(s - m)
    es = jnp.exp(sk - m)
    inv = 1.0 / (jnp.sum(e, axis=1, keepdims=True) + es)
    return e * inv, es * inv


def _block_views(i, nblk):
    ns = [i * nblk + b for b in range(nblk)]
    wins = [pl.ds(pl.multiple_of(n * CHUNK, CHUNK), 3 * CHUNK) for n in ns]
    rows = [slice(b * CHUNK, (b + 1) * CHUNK) for b in range(nblk)]
    return ns, wins, rows


def attn_fwd(qr, kp, vp, sink3):
    S = qr.shape[0]
    tq = min(2048, S)
    gw = Q_PER_KV * HEAD_DIM
    nblk = tq // CHUNK

    def body(q_ref, k_ref, v_ref, s_ref, o_ref):
        ns, wins, rows = _block_views(pl.program_id(1), nblk)
        sk = _stack_sinks(s_ref)
        band = _band_bias()
        scores = [_scores(_stack_heads(q_ref, rows[b]), k_ref[wins[b], :]) for b in range(nblk)]
        probs = [_softmax_sink(scores[b], sk, band + _edge_bias(ns[b], S))[0].astype(BF16) for b in range(nblk)]
        outs = [jnp.dot(probs[b], v_ref[wins[b], :], preferred_element_type=F32).astype(BF16) for b in range(nblk)]
        for b in range(nblk):
            for r in range(Q_PER_KV):
                o_ref[rows[b], r * HEAD_DIM:(r + 1) * HEAD_DIM] = outs[b][r * CHUNK:(r + 1) * CHUNK]

    kv = pl.BlockSpec((S + 2 * CHUNK, HEAD_DIM), lambda g, i: (0, g))
    return _call("attn_fwd", body, grid=(N_KV_HEADS, S // tq),
                 in_specs=[pl.BlockSpec((tq, gw), lambda g, i: (i, g)), kv, kv,
                           pl.BlockSpec((None, Q_PER_KV, LANES), lambda g, i: (g, 0, 0))],
                 out_specs=pl.BlockSpec((tq, gw), lambda g, i: (i, g)),
                 out_shape=jax.ShapeDtypeStruct((S, D_MODEL), BF16), sem=("parallel", "arbitrary"))(qr, kp, vp, sink3)


def attn_bwd(qr, kp, vp, sink3, dmix):
    S = qr.shape[0]
    tq = min(1024, S)
    gw = Q_PER_KV * HEAD_DIM
    scale = 1.0 / math.sqrt(HEAD_DIM)
    nblk = tq // CHUNK

    def body(q_ref, k_ref, v_ref, s_ref, do_ref, dq_ref, dk_ref, dv_ref, ds_ref):
        i = pl.program_id(1)

        @pl.when(i == 0)
        def _():
            dk_ref[...] = jnp.zeros_like(dk_ref)
            dv_ref[...] = jnp.zeros_like(dv_ref)
            ds_ref[...] = jnp.zeros_like(ds_ref)

        blocks = range(nblk)
        ns, wins, rows = _block_views(i, nblk)
        sk = _stack_sinks(s_ref)
        band = _band_bias()
        qs = [_stack_heads(q_ref, rows[b]) for b in blocks]
        dos = [_stack_heads(do_ref, rows[b]) for b in blocks]
        scores = [_scores(qs[b], k_ref[wins[b], :]) for b in blocks]
        dps = [lax.dot_general(dos[b], v_ref[wins[b], :], NT, preferred_element_type=F32) for b in blocks]
        probs = [_softmax_sink(scores[b], sk, band + _edge_bias(ns[b], S)) for b in blocks]
        deltas = [jnp.sum(probs[b][0] * dps[b], axis=1, keepdims=True) for b in blocks]
        dscs = [(probs[b][0] * (dps[b] - deltas[b]) * scale).astype(BF16) for b in blocks]
        dqs = [jnp.dot(dscs[b], k_ref[wins[b], :], preferred_element_type=F32) for b in blocks]
        dks = [lax.dot_general(dscs[b], qs[b], TN, preferred_element_type=F32) for b in blocks]
        dvs = [lax.dot_general(probs[b][0].astype(BF16), dos[b], TN, preferred_element_type=F32) for b in blocks]
        for b in blocks:
            for r in range(Q_PER_KV):
                dq_ref[rows[b], r * HEAD_DIM:(r + 1) * HEAD_DIM] = dqs[b][r * CHUNK:(r + 1) * CHUNK].astype(BF16)
        for m in range(nblk + 2):
            parts = [(b, m - b) for b in blocks if 0 <= m - b <= 2]
            krows = pl.ds(pl.multiple_of(i * tq + m * CHUNK, CHUNK), CHUNK)
            dk_ref[krows, :] += sum(dks[b][o * CHUNK:(o + 1) * CHUNK] for b, o in parts)
            dv_ref[krows, :] += sum(dvs[b][o * CHUNK:(o + 1) * CHUNK] for b, o in parts)
        for r in range(Q_PER_KV):
            head = slice(r * CHUNK, (r + 1) * CHUNK)
            dsink = sum(jnp.sum(-probs[b][1][head] * deltas[b][head], axis=0, keepdims=True) for b in blocks)
            ds_ref[r:r + 1, :] += jnp.broadcast_to(dsink, (1, LANES))

    kv = pl.BlockSpec((S + 2 * CHUNK, HEAD_DIM), lambda g, i: (0, g))
    qspec = pl.BlockSpec((tq, gw), lambda g, i: (i, g))
    sspec = pl.BlockSpec((None, Q_PER_KV, LANES), lambda g, i: (g, 0, 0))
    padshape = jax.ShapeDtypeStruct((S + 2 * CHUNK, KV_WIDTH), F32)
    return _call("attn_bwd", body, grid=(N_KV_HEADS, S // tq),
                 in_specs=[qspec, kv, kv, sspec, qspec],
                 out_specs=[qspec, kv, kv, sspec],
                 out_shape=[jax.ShapeDtypeStruct((S, ATTN_WIDTH), BF16), padshape, padshape,
                            jax.ShapeDtypeStruct((N_KV_HEADS, Q_PER_KV, LANES), F32)],
                 sem=("parallel", "arbitrary"))(qr, kp, vp, sink3, dmix)


CONV_TILE = 256


def _fill_padded(dst_ref, value, S):
    zero = jnp.zeros((CONV_PAD, LANES), F32)
    dst_ref[0:CONV_PAD, :] = zero
    dst_ref[CONV_PAD + S:2 * CONV_PAD + S, :] = zero
    dst_ref[CONV_PAD:CONV_PAD + S, :] = value


def conv_dw_fwd(z, w32, b):
    S = z.shape[0]
    T = min(CONV_TILE, S)
    lo = CONV_PAD - (CONV_KERNEL - 1) // 2

    def body(a_ref, g_ref, w_ref, b_ref, o_ref, c0_ref):
        _fill_padded(c0_ref, a_ref[...].astype(F32) * _sigmoid(g_ref[...].astype(F32)), S)

        def tile(t, carry):
            base = pl.multiple_of(t * T, T)
            acc = jnp.broadcast_to(b_ref[...], (T, LANES))
            for j in range(CONV_KERNEL):
                acc = acc + w_ref[j:j + 1, :] * c0_ref[pl.ds(base + lo + j, T), :]
            o_ref[pl.ds(base, T), :] = acc
            return carry

        lax.fori_loop(0, S // T, tile, 0)

    nca, ncg = OFF_CA // LANES, OFF_CG // LANES
    return _call("conv_dw_fwd", body, grid=(CONV_WIDTH // LANES,),
                 in_specs=[pl.BlockSpec((S, LANES), lambda cb: (0, nca + cb)), pl.BlockSpec((S, LANES), lambda cb: (0, ncg + cb)),
                           pl.BlockSpec((32, LANES), lambda cb: (0, cb)), pl.BlockSpec((1, LANES), lambda cb: (0, cb))],
                 out_specs=pl.BlockSpec((S, LANES), lambda cb: (0, cb)),
                 out_shape=jax.ShapeDtypeStruct((S, CONV_WIDTH), F32),
                 scratch=[pltpu.VMEM((S + 2 * CONV_PAD, LANES), F32)], sem=("parallel",))(z, z, w32, b)


def _ln_stats(x):
    mu = jnp.mean(x, axis=-1, keepdims=True)
    xc = x - mu
    rs = lax.rsqrt(jnp.mean(xc * xc, axis=-1, keepdims=True) + EPS)
    return xc * rs, rs


def _ln_bwd(dy, xh, rs, g):
    dxh = dy * g
    return rs * (dxh - jnp.mean(dxh, axis=-1, keepdims=True) - xh * jnp.mean(dxh * xh, axis=-1, keepdims=True))


def conv_ln_fwd(c1, g, b, mix):
    S = c1.shape[0]
    T = min(512, S)

    def body(x_ref, g_ref, b_ref, mix_ref, o_ref):
        xh, _ = _ln_stats(x_ref[...])
        y = xh * g_ref[...] + b_ref[...]
        o_ref[...] = (y * _sigmoid(y)).astype(BF16)

    row = pl.BlockSpec((T, CONV_WIDTH), lambda i: (i, 0))
    vec = pl.BlockSpec((1, CONV_WIDTH), lambda i: (0, 0))
    return _call("conv_ln_fwd", body, grid=(S // T,), in_specs=[row, vec, vec, pl.BlockSpec(memory_space=pl.ANY)],
                 out_specs=pl.BlockSpec((T, CONV_WIDTH), lambda i: (i, ATTN_WIDTH // CONV_WIDTH)),
                 out_shape=jax.ShapeDtypeStruct(mix.shape, BF16), sem=("parallel",), aliases={3: 0})(c1, g, b, mix)


def _acc_out(ref, value, step=None):
    step = pl.program_id(0) if step is None else step

    @pl.when(step == 0)
    def _():
        ref[...] = value

    @pl.when(step > 0)
    def _():
        ref[...] += value


def conv_ln_bwd(dmix, c1, g, b):
    S = c1.shape[0]
    T = min(512, S)

    def body(d_ref, x_ref, g_ref, b_ref, dx_ref, dg_ref, db_ref):
        xh, rs = _ln_stats(x_ref[...])
        gv = g_ref[...]
        y = xh * gv + b_ref[...]
        sg = _sigmoid(y)
        dy = d_ref[...].astype(F32) * sg * (1.0 + y * (1.0 - sg))
        dx_ref[...] = _ln_bwd(dy, xh, rs, gv)
        _acc_out(dg_ref, jnp.sum(dy * xh, axis=0, keepdims=True))
        _acc_out(db_ref, jnp.sum(dy, axis=0, keepdims=True))

    row = pl.BlockSpec((T, CONV_WIDTH), lambda i: (i, 0))
    vec = pl.BlockSpec((1, CONV_WIDTH), lambda i: (0, 0))
    vshape = jax.ShapeDtypeStruct((1, CONV_WIDTH), F32)
    return _call("conv_ln_bwd", body, grid=(S // T,),
                 in_specs=[pl.BlockSpec((T, CONV_WIDTH), lambda i: (i, ATTN_WIDTH // CONV_WIDTH)), row, vec, vec],
                 out_specs=[row, vec, vec], out_shape=[jax.ShapeDtypeStruct((S, CONV_WIDTH), F32), vshape, vshape],
                 sem=("arbitrary",))(dmix, c1, g, b)


def conv_dw_bwd(dc1, z, w32, dz):
    S = z.shape[0]
    T = min(CONV_TILE, S)
    half = (CONV_KERNEL - 1) // 2
    lo = CONV_PAD - half
    n_cb = CONV_WIDTH // LANES

    def body(d_ref, a_ref, g_ref, w_ref, dz_ref, o_ref, dw_ref, db_ref, c0_ref, d1_ref, wacc_ref, dg_ref):
        @pl.when(pl.program_id(1) == 0)
        def _():
            av = a_ref[...].astype(F32)
            sg = _sigmoid(g_ref[...].astype(F32))
            _fill_padded(c0_ref, av * sg, S)
            _fill_padded(d1_ref, d_ref[...], S)
            wacc_ref[...] = jnp.zeros_like(wacc_ref)

            def tile(t, carry):
                base = pl.multiple_of(t * T, T)
                d1 = d_ref[pl.ds(base, T), :]
                acc = jnp.zeros((T, LANES), F32)
                for j in range(CONV_KERNEL):
                    acc = acc + w_ref[j:j + 1, :] * d1_ref[pl.ds(base + CONV_PAD + half - j, T), :]
                    prod = d1 * c0_ref[pl.ds(base + lo + j, T), :]
                    wacc_ref[j] += jnp.sum(prod.reshape(T // 8, 8, LANES), axis=0)
                rows = pl.ds(base, T)
                a_t = a_ref[rows, :].astype(F32)
                s_t = _sigmoid(g_ref[rows, :].astype(F32))
                o_ref[rows, :] = (acc * s_t).astype(BF16)
                dg_ref[rows, :] = (acc * a_t * s_t * (1.0 - s_t)).astype(BF16)
                return carry

            lax.fori_loop(0, S // T, tile, 0)
            dw_ref[...] = jnp.sum(wacc_ref[...], axis=1)
            db_ref[...] = jnp.sum(d_ref[...], axis=0, keepdims=True)

        @pl.when(pl.program_id(1) == 1)
        def _():
            o_ref[...] = dg_ref[...]

    nca, ncg = OFF_CA // LANES, OFF_CG // LANES
    return _call("conv_dw_bwd", body, grid=(n_cb, 2),
                 in_specs=[pl.BlockSpec((S, LANES), lambda cb, j: (0, cb)), pl.BlockSpec((S, LANES), lambda cb, j: (0, nca + cb)),
                           pl.BlockSpec((S, LANES), lambda cb, j: (0, ncg + cb)), pl.BlockSpec((32, LANES), lambda cb, j: (0, cb)),
                           pl.BlockSpec(memory_space=pl.ANY)],
                 out_specs=[pl.BlockSpec((S, LANES), lambda cb, j: (0, nca + cb + n_cb * j)),
                            pl.BlockSpec((32, LANES), lambda cb, j: (0, cb)), pl.BlockSpec((1, LANES), lambda cb, j: (0, cb))],
                 out_shape=[jax.ShapeDtypeStruct(dz.shape, BF16), jax.ShapeDtypeStruct((32, CONV_WIDTH), F32),
                            jax.ShapeDtypeStruct((1, CONV_WIDTH), F32)],
                 scratch=[pltpu.VMEM((S + 2 * CONV_PAD, LANES), F32), pltpu.VMEM((S + 2 * CONV_PAD, LANES), F32),
                          pltpu.VMEM((32, 8, LANES), F32), pltpu.VMEM((S, LANES), BF16)],
                 sem=("parallel", "arbitrary"), aliases={4: 0})(dc1, z, z, w32, dz)


_INV_SQRT2 = 1.0 / math.sqrt(2.0)
_INV_SQRT2PI = 1.0 / math.sqrt(2.0 * math.pi)


def _gelu(x):
    return 0.5 * x * (1.0 + lax.erf(x * _INV_SQRT2))


def _gelu_grad(x):
    return 0.5 * (1.0 + lax.erf(x * _INV_SQRT2)) + x * jnp.exp(-0.5 * x * x) * _INV_SQRT2PI


def sgu_fwd(z, g, b, ws, bs, mix):
    S = z.shape[0]
    T = min(512, S)

    def body(u_ref, v_ref, g_ref, b_ref, ws_ref, bs_ref, mix_ref, o_ref):
        xh, _ = _ln_stats(_gelu(v_ref[...].astype(F32)))
        vn = (xh * g_ref[...] + b_ref[...]).astype(BF16)
        for ch in range(T // CHUNK):
            rows = slice(ch * CHUNK, (ch + 1) * CHUNK)
            for h in range(SGU_HEADS):
                cols = slice(h * HEAD_DIM, (h + 1) * HEAD_DIM)
                sp = jnp.dot(ws_ref[h], vn[rows, cols], preferred_element_type=F32) + bs_ref[h]
                o_ref[rows, cols] = (_gelu(u_ref[rows, cols].astype(F32)) * sp).astype(BF16)

    vec = pl.BlockSpec((1, SGU_WIDTH), lambda i: (0, 0))
    full = pl.BlockSpec((SGU_HEADS, CHUNK, CHUNK), lambda i: (0, 0, 0))
    return _call("sgu_fwd", body, grid=(S // T,),
                 in_specs=[pl.BlockSpec((T, SGU_WIDTH), lambda i: (i, OFF_U // SGU_WIDTH)),
                           pl.BlockSpec((T, SGU_WIDTH), lambda i: (i, OFF_VV // SGU_WIDTH)), vec, vec, full, full,
                           pl.BlockSpec(memory_space=pl.ANY)],
                 out_specs=pl.BlockSpec((T, SGU_WIDTH), lambda i: (i, (ATTN_WIDTH + CONV_WIDTH) // SGU_WIDTH)),
                 out_shape=jax.ShapeDtypeStruct(mix.shape, BF16), sem=("parallel",), aliases={6: 0})(z, z, g, b, ws, bs, mix)


def sgu_bwd(z, dmix, g, b, ws, bs, dz):
    S = z.shape[0]
    T = min(512, S)

    def body(u_ref, v_ref, d_ref, g_ref, b_ref, ws_ref, bs_ref, dz_ref, o_ref, dws_ref, dbs_ref, dg_ref, db_ref, dvn_ref, dv_ref):
        tile = pl.program_id(0)
        first = pl.program_id(1) == 0

        @pl.when(first & (tile == 0))
        def _():
            dws_ref[...] = jnp.zeros_like(dws_ref)
            dbs_ref[...] = jnp.zeros_like(dbs_ref)

        @pl.when(first)
        def _():
            vraw = v_ref[...].astype(F32)
            xh, rs = _ln_stats(_gelu(vraw))
            gv = g_ref[...]
            vn = (xh * gv + b_ref[...]).astype(BF16)
            for ch in range(T // CHUNK):
                rows = slice(ch * CHUNK, (ch + 1) * CHUNK)
                for h in range(SGU_HEADS):
                    cols = slice(h * HEAD_DIM, (h + 1) * HEAD_DIM)
                    w = ws_ref[h]
                    vb = vn[rows, cols]
                    sp = jnp.dot(w, vb, preferred_element_type=F32) + bs_ref[h]
                    uraw = u_ref[rows, cols].astype(F32)
                    dout = d_ref[rows, cols].astype(F32)
                    o_ref[rows, cols] = (dout * sp * _gelu_grad(uraw)).astype(BF16)
                    dsp = dout * _gelu(uraw)
                    dspb = dsp.astype(BF16)
                    dvn_ref[rows, cols] = lax.dot_general(w, dspb, TN, preferred_element_type=F32)
                    dws_ref[h] += lax.dot_general(dspb, vb, NT, preferred_element_type=F32)
                    dbs_ref[h] += jnp.sum(dsp, axis=1, keepdims=True)
            dvn = dvn_ref[...]
            dv_ref[...] = (_ln_bwd(dvn, xh, rs, gv) * _gelu_grad(vraw)).astype(BF16)
            _acc_out(dg_ref, jnp.sum(dvn * xh, axis=0, keepdims=True), tile)
            _acc_out(db_ref, jnp.sum(dvn, axis=0, keepdims=True), tile)

        @pl.when(pl.program_id(1) == 1)
        def _():
            o_ref[...] = dv_ref[...]

    vec = pl.BlockSpec((1, SGU_WIDTH), lambda i, j: (0, 0))
    full = pl.BlockSpec((SGU_HEADS, CHUNK, CHUNK), lambda i, j: (0, 0, 0))
    vshape = jax.ShapeDtypeStruct((1, SGU_WIDTH), F32)
    return _call("sgu_bwd", body, grid=(S // T, 2),
                 in_specs=[pl.BlockSpec((T, SGU_WIDTH), lambda i, j: (i, OFF_U // SGU_WIDTH)),
                           pl.BlockSpec((T, SGU_WIDTH), lambda i, j: (i, OFF_VV // SGU_WIDTH)),
                           pl.BlockSpec((T, SGU_WIDTH), lambda i, j: (i, (ATTN_WIDTH + CONV_WIDTH) // SGU_WIDTH)), vec, vec, full, full,
                           pl.BlockSpec(memory_space=pl.ANY)],
                 out_specs=[pl.BlockSpec((T, SGU_WIDTH), lambda i, j: (i, OFF_U // SGU_WIDTH + j)), full,
                            pl.BlockSpec((SGU_HEADS, CHUNK, 1), lambda i, j: (0, 0, 0)), vec, vec],
                 out_shape=[jax.ShapeDtypeStruct(dz.shape, BF16), jax.ShapeDtypeStruct((SGU_HEADS, CHUNK, CHUNK), F32),
                            jax.ShapeDtypeStruct((SGU_HEADS, CHUNK, 1), F32), vshape, vshape],
                 scratch=[pltpu.VMEM((T, SGU_WIDTH), F32), pltpu.VMEM((T, SGU_WIDTH), BF16)],
                 sem=("arbitrary", "arbitrary"), aliases={7: 0})(z, z, dmix, g, b, ws, bs, dz)


def _row_tile(rows, cols, n_arrays, budget_mib=24):
    budget = (budget_mib * 1024 * 1024) // (n_arrays * 2 * 4 * cols)
    t = min(rows, max(16, budget // 16 * 16))
    while rows % t:
        t -= 16
    return t


def cast_layer(w, layer, token):
    _, R, C = w.shape
    tr = _row_tile(R, C, 2)

    def body(w_ref, token_ref, o_ref):
        o_ref[...] = w_ref[...].astype(BF16)

    return _call("cast_layer", body, grid=(R // tr,),
                 in_specs=[pl.BlockSpec((None, tr, C), lambda i: (layer, i, 0)), pl.BlockSpec((8, LANES), lambda i: (0, 0))],
                 out_specs=pl.BlockSpec((tr, C), lambda i: (i, 0)), out_shape=jax.ShapeDtypeStruct((R, C), BF16),
                 sem=("parallel",))(w, token)


def add_sibling_half(grad, recv, c_idx):
    J, R, C = grad.shape
    hr = R // 2
    tr = _row_tile(hr, C, 3)
    nb = hr // tr

    def body(c_ref, g_ref, r_ref, o_ref):
        o_ref[...] = (g_ref[...].astype(F32) + r_ref[...].astype(F32)).astype(BF16)

    grid_spec = pltpu.PrefetchScalarGridSpec(
        num_scalar_prefetch=1, grid=(J, nb),
        in_specs=[pl.BlockSpec((None, tr, C), lambda j, i, c: (j, c[0] * nb + i, 0)),
                  pl.BlockSpec((None, tr, C), lambda j, i, c: (j, i, 0))],
        out_specs=pl.BlockSpec((None, tr, C), lambda j, i, c: (j, i, 0)))
    return pl.pallas_call(body, name="add_sibling_half", grid_spec=grid_spec,
                          out_shape=jax.ShapeDtypeStruct((J, hr, C), BF16),
                          compiler_params=pltpu.CompilerParams(vmem_limit_bytes=VMEM_LIMIT,
                                                               dimension_semantics=("parallel", "parallel")))(c_idx, grad, recv)


def sum_chips(own, others, stack, x_idx, y_idx, layer):
    R, C = own.shape[1:]
    tr = _row_tile(R, C, 4)

    def body(x_ref, y_ref, own_ref, oth_ref, stack_ref, o_ref):
        acc = own_ref[...].astype(F32)
        for j in range(3):
            acc = acc + oth_ref[j].astype(F32)
        o_ref[...] = acc

    grid_spec = pltpu.PrefetchScalarGridSpec(
        num_scalar_prefetch=2, grid=(R // tr,),
        in_specs=[pl.BlockSpec((None, tr, C), lambda i, xr, yr: (2 * xr[0] + yr[0], i, 0)),
                  pl.BlockSpec((3, tr, C), lambda i, xr, yr: (0, i, 0)),
                  pl.BlockSpec(memory_space=pl.ANY)],
        out_specs=pl.BlockSpec((None, tr, C), lambda i, xr, yr: (layer, i, 0)))
    return pl.pallas_call(body, name="sum_chips", grid_spec=grid_spec,
                          out_shape=jax.ShapeDtypeStruct(stack.shape, F32), input_output_aliases={4: 0},
                          compiler_params=pltpu.CompilerParams(vmem_limit_bytes=VMEM_LIMIT,
                                                               dimension_semantics=("parallel",)))(x_idx, y_idx, own, others, stack)


def adamw_halves(w, mine, theirs, m, v, c_idx):
    L, R, C = w.shape
    hr = R // 2
    tr = _row_tile(hr, C, 9, budget_mib=40)
    nb = hr // tr

    def body(c_ref, w_ref, a_ref, b_ref, m_ref, v_ref, g_ref, d_ref, nm_ref, nv_ref):
        gv = jnp.where(pl.program_id(1) == c_ref[0], a_ref[...], b_ref[...])
        g_ref[...] = gv
        nm = ADAM_B1 * m_ref[...] + (1.0 - ADAM_B1) * gv
        nv = ADAM_B2 * v_ref[...] + (1.0 - ADAM_B2) * (gv * gv)
        m_hat = nm / (1.0 - ADAM_B1 ** ADAM_STEP)
        v_hat = nv / (1.0 - ADAM_B2 ** ADAM_STEP)
        d_ref[...] = -ADAM_LR * (m_hat / (jnp.sqrt(v_hat) + ADAM_EPS) + ADAM_WD * w_ref[...])
        nm_ref[...] = nm
        nv_ref[...] = nv

    full = pl.BlockSpec((None, tr, C), lambda l, h, i, c: (l, h * nb + i, 0))
    a_spec = pl.BlockSpec((None, tr, C), lambda l, h, i, c: (l, jnp.where(h == c[0], i, 0), 0))
    b_spec = pl.BlockSpec((None, tr, C), lambda l, h, i, c: (l, jnp.where(h == c[0], 0, i), 0))
    grid_spec = pltpu.PrefetchScalarGridSpec(num_scalar_prefetch=1, grid=(L, 2, nb),
                                             in_specs=[full, a_spec, b_spec, full, full], out_specs=[full] * 4)
    shape = jax.ShapeDtypeStruct((L, R, C), F32)
    return pl.pallas_call(body, name="adamw_halves", grid_spec=grid_spec, out_shape=[shape] * 4,
                          compiler_params=pltpu.CompilerParams(vmem_limit_bytes=VMEM_LIMIT,
                                                               dimension_semantics=("parallel", "arbitrary", "arbitrary")))(
        c_idx, w, mine, theirs, m, v)


def adamw(w, g, m, v):
    R, C = w.shape
    tr = _row_tile(R, C, 7)

    def body(w_ref, g_ref, m_ref, v_ref, d_ref, nm_ref, nv_ref):
        gv = g_ref[...]
        nm = ADAM_B1 * m_ref[...] + (1.0 - ADAM_B1) * gv
        nv = ADAM_B2 * v_ref[...] + (1.0 - ADAM_B2) * (gv * gv)
        m_hat = nm / (1.0 - ADAM_B1 ** ADAM_STEP)
        v_hat = nv / (1.0 - ADAM_B2 ** ADAM_STEP)
        d_ref[...] = -ADAM_LR * (m_hat / (jnp.sqrt(v_hat) + ADAM_EPS) + ADAM_WD * w_ref[...])
        nm_ref[...] = nm
        nv_ref[...] = nv

    spec = pl.BlockSpec((tr, C), lambda i: (i, 0))
    shape = jax.ShapeDtypeStruct((R, C), F32)
    return _call("adamw", body, grid=(R // tr,), in_specs=[spec] * 4, out_specs=[spec] * 3, out_shape=[shape] * 3,
                 sem=("parallel",))(w, g, m, v)


def _place():
    x, y, c = lax.axis_index("x"), lax.axis_index("y"), lax.axis_index("c")
    chips = [(1 - x, y), (x, 1 - y), (1 - x, 1 - y)]
    return x, y, c, chips


def _remote(src, dst, send_sem, recv_sem, dev):
    return pltpu.make_async_remote_copy(src_ref=src, dst_ref=dst, send_sem=send_sem, recv_sem=recv_sem,
                                        device_id=dev, device_id_type=MESH)


EFFECT = pltpu.SideEffectType.DATAFLOW_SIDE_EFFECTING
SEM = pl.BlockSpec(memory_space=pltpu.SEMAPHORE)
ANY = pl.BlockSpec(memory_space=pl.ANY)
TOKEN = jax.ShapeDtypeStruct((8, LANES), F32)


def _in_hbm(a):
    return pltpu.with_memory_space_constraint(a, pltpu.HBM)


def _gather_copies(shards, lands, send_sems, recv_sems):
    x, y, c, chips = _place()
    me = 2 * x + y
    copies = []
    for k in range(len(shards)):
        hr = shards[k].shape[0] // 2
        mine = pl.ds(pl.multiple_of(c * hr, 8), hr)
        for t, (px, py) in enumerate(chips):
            copies.append(_remote(shards[k].at[mine, :], lands[k].at[me, mine, :], send_sems.at[4 * k + t], recv_sems.at[4 * k + t],
                                  (px, py, c)))
        copies.append(_remote(shards[k], lands[k].at[me], send_sems.at[4 * k + 3], recv_sems.at[4 * k + 3], (x, y, 1 - c)))
    return copies


def _gather_landings(lands, send_sems, recv_sems):
    x, y, c, chips = _place()
    me = 2 * x + y
    landings = []
    for k in range(len(lands)):
        hr = lands[k].shape[1] // 2
        mine = pl.ds(pl.multiple_of(c * hr, 8), hr)
        for t, (px, py) in enumerate(chips):
            dst = lands[k].at[2 * px + py, mine, :]
            landings.append(_remote(dst, dst, send_sems.at[4 * k + t], recv_sems.at[4 * k + t], (px, py, c)))
        dst = lands[k].at[me]
        landings.append(_remote(dst, dst, send_sems.at[4 * k + 3], recv_sems.at[4 * k + 3], (x, y, 1 - c)))
    return landings


def gather_start(shards, after):
    n = len(shards)

    def body(*refs):
        srcs, lands_in = refs[:n], refs[n:2 * n]
        send_sems, recv_sems = refs[2 * n + 1], refs[2 * n + 2]
        token = refs[-1]
        for cp in _gather_copies(srcs, lands_in, send_sems, recv_sems):
            cp.start()
        token[...] = jnp.zeros_like(token)

    lands = [lax.empty((N_CHIPS,) + s.shape, s.dtype) for s in shards]
    outs = pl.pallas_call(
        body, name="gather_start", in_specs=[HBM] * (2 * n) + [ANY],
        out_specs=[SEM, SEM] + [HBM] * (2 * n) + [VMEM_SPEC],
        out_shape=[pltpu.SemaphoreType.DMA((4 * n,)), pltpu.SemaphoreType.DMA((4 * n,))]
        + [pltpu.HBM(s.shape, s.dtype) for s in shards] + [pltpu.HBM(l.shape, l.dtype) for l in lands] + [TOKEN],
        input_output_aliases={i: 2 + i for i in range(2 * n)},
        compiler_params=pltpu.CompilerParams(has_side_effects=EFFECT),
    )(*[_in_hbm(s) for s in shards], *[_in_hbm(l) for l in lands], after)
    return outs[0], outs[1], outs[2:2 + n], outs[2 + n:2 + 2 * n], outs[-1]


def gather_wait(send_sems, recv_sems, shards, lands, after):
    n = len(shards)

    def body(*refs):
        srcs, lands_in = refs[:n], refs[n:2 * n]
        send, recv = refs[2 * n], refs[2 * n + 1]
        for cp in _gather_copies(srcs, lands_in, send, recv):
            cp.wait_send()
        for cp in _gather_landings(lands_in, send, recv):
            cp.wait_recv()

    outs = pl.pallas_call(
        body, name="gather_wait", in_specs=[HBM] * (2 * n) + [SEM, SEM, ANY], out_specs=[HBM] * (2 * n),
        out_shape=[pltpu.HBM(s.shape, s.dtype) for s in shards] + [pltpu.HBM(l.shape, l.dtype) for l in lands],
        input_output_aliases={i: i for i in range(2 * n)},
        compiler_params=pltpu.CompilerParams(has_side_effects=EFFECT),
    )(*shards, *lands, send_sems, recv_sems, after)
    return outs[n:]


def _forward_copies(lands, send_sems, recv_sems, received):
    x, y, c, chips = _place()
    copies = []
    for k in range(len(lands)):
        hr = lands[k].shape[1] // 2
        half = (1 - c) if received else c
        rows = pl.ds(pl.multiple_of(half * hr, 8), hr)
        for t, (px, py) in enumerate(chips):
            block = lands[k].at[2 * px + py, rows, :]
            copies.append(_remote(block, block, send_sems.at[3 * k + t], recv_sems.at[3 * k + t], (x, y, 1 - c)))
    return copies


def forward_start(lands):
    n = len(lands)

    def body(*refs):
        for cp in _forward_copies(refs[:n], refs[n], refs[n + 1], received=False):
            cp.start()
        refs[-1][...] = jnp.zeros_like(refs[-1])

    outs = pl.pallas_call(
        body, name="forward_start", in_specs=[HBM] * n, out_specs=[SEM, SEM] + [HBM] * n + [VMEM_SPEC],
        out_shape=[pltpu.SemaphoreType.DMA((3 * n,)), pltpu.SemaphoreType.DMA((3 * n,))]
        + [pltpu.HBM(l.shape, l.dtype) for l in lands] + [TOKEN],
        input_output_aliases={i: 2 + i for i in range(n)},
        compiler_params=pltpu.CompilerParams(has_side_effects=EFFECT),
    )(*[_in_hbm(l) for l in lands])
    return outs[0], outs[1], outs[2:2 + n], outs[-1]


def forward_wait(send_sems, recv_sems, lands, after):
    n = len(lands)

    def body(*refs):
        for cp in _forward_copies(refs[:n], refs[n], refs[n + 1], received=False):
            cp.wait_send()
        for cp in _forward_copies(refs[:n], refs[n], refs[n + 1], received=True):
            cp.wait_recv()

    return pl.pallas_call(
        body, name="forward_wait", in_specs=[HBM] * n + [SEM, SEM, ANY], out_specs=[HBM] * n,
        out_shape=[pltpu.HBM(l.shape, l.dtype) for l in lands],
        input_output_aliases={i: i for i in range(n)},
        compiler_params=pltpu.CompilerParams(has_side_effects=EFFECT),
    )(*lands, send_sems, recv_sems, after)


def forward_halves(lands):
    n = len(lands)

    def body(*refs):
        ins, outs = refs[:n], refs[n:2 * n]
        send_sems, recv_sems = refs[2 * n:]
        x, y, c, chips = _place()
        sibling = (x, y, 1 - c)
        sends = []
        for k in range(n):
            hr = ins[k].shape[1] // 2
            mine = pl.ds(pl.multiple_of(c * hr, 8), hr)
            for t, (px, py) in enumerate(chips):
                cp = _remote(ins[k].at[2 * px + py, mine, :], outs[k].at[2 * px + py, mine, :],
                             send_sems.at[k, t], recv_sems.at[k, t], sibling)
                cp.start()
                sends.append(cp)
        for k in range(n):
            hr = ins[k].shape[1] // 2
            other = pl.ds(pl.multiple_of((1 - c) * hr, 8), hr)
            for t, (px, py) in enumerate(chips):
                dst = outs[k].at[2 * px + py, other, :]
                _remote(dst, dst, send_sems.at[k, t], recv_sems.at[k, t], sibling).wait_recv()
        for cp in sends:
            cp.wait_send()

    return pl.pallas_call(
        body, name="forward_halves", in_specs=[HBM] * n, out_specs=[HBM] * n,
        out_shape=[jax.ShapeDtypeStruct(l.shape, l.dtype) for l in lands],
        input_output_aliases={i: i for i in range(n)},
        scratch_shapes=[pltpu.SemaphoreType.DMA((n, 3)), pltpu.SemaphoreType.DMA((n, 3))],
    )(*lands)


def gather_small(block):
    def body(in_ref, out_ref, send_sems, recv_sems):
        x, y, c, chips = _place()
        me = 2 * x + y
        out_ref[me] = in_ref[...]
        sends = []
        for t, (px, py) in enumerate(chips):
            cp = _remote(in_ref, out_ref.at[me], send_sems.at[t], recv_sems.at[t], (px, py, c))
            cp.start()
            sends.append(cp)
        for t, (px, py) in enumerate(chips):
            landed = out_ref.at[2 * px + py]
            _remote(landed, landed, send_sems.at[t], recv_sems.at[t], (px, py, c)).wait_recv()
        for cp in sends:
            cp.wait_send()

    return pl.pallas_call(
        body, name="gather_small", in_specs=[VMEM_SPEC], out_specs=VMEM_SPEC,
        out_shape=jax.ShapeDtypeStruct((N_CHIPS,) + block.shape, block.dtype),
        scratch_shapes=[pltpu.SemaphoreType.DMA((3,)), pltpu.SemaphoreType.DMA((3,))],
    )(block)


def exchange_sibling_halves(grads):
    n = len(grads)

    def body(*refs):
        ins, outs = refs[:n], refs[n:2 * n]
        send_sems, recv_sems = refs[2 * n:]
        x, y, c, _ = _place()
        copies = []
        for k in range(n):
            hr = ins[k].shape[1] // 2
            theirs = pl.ds(pl.multiple_of((1 - c) * hr, 8), hr)
            cp = _remote(ins[k].at[:, theirs, :], outs[k], send_sems.at[k], recv_sems.at[k], (x, y, 1 - c))
            cp.start()
            copies.append(cp)
        for cp in copies:
            cp.wait()

    return pl.pallas_call(
        body, name="exchange_sibling_halves", in_specs=[HBM] * n, out_specs=[HBM] * n,
        out_shape=[jax.ShapeDtypeStruct((g.shape[0], g.shape[1] // 2, g.shape[2]), g.dtype) for g in grads],
        scratch_shapes=[pltpu.SemaphoreType.DMA((n,)), pltpu.SemaphoreType.DMA((n,))],
    )(*grads)


def _sibling_half_copies(grads, lands, send_sems, recv_sems):
    x, y, c, _ = _place()
    copies = []
    for k in range(len(grads)):
        hr = grads[k].shape[1] // 2
        theirs = pl.ds(pl.multiple_of((1 - c) * hr, 8), hr)
        copies.append(_remote(grads[k].at[:, theirs, :], lands[k], send_sems.at[k], recv_sems.at[k], (x, y, 1 - c)))
    return copies


def _sibling_whole_copies(srcs, lands, send_sems, recv_sems):
    x, y, c, _ = _place()
    return [_remote(srcs[k], lands[k], send_sems.at[k], recv_sems.at[k], (x, y, 1 - c)) for k in range(len(srcs))]


def pair_start(name, make_copies, srcs, land_shapes):
    n = len(srcs)

    def body(*refs):
        src_refs, land_refs = refs[:n], refs[n:2 * n]
        send_sems, recv_sems = refs[2 * n], refs[2 * n + 1]
        token = refs[-1]
        for cp in make_copies(src_refs, land_refs, send_sems, recv_sems):
            cp.start()
        token[...] = jnp.zeros_like(token)

    lands = [lax.empty(shape, s.dtype) for shape, s in zip(land_shapes, srcs)]
    outs = pl.pallas_call(
        body, name=name, in_specs=[HBM] * (2 * n), out_specs=[SEM, SEM] + [HBM] * (2 * n) + [VMEM_SPEC],
        out_shape=[pltpu.SemaphoreType.DMA((n,)), pltpu.SemaphoreType.DMA((n,))]
        + [pltpu.HBM(s.shape, s.dtype) for s in srcs] + [pltpu.HBM(l.shape, l.dtype) for l in lands] + [TOKEN],
        input_output_aliases={i: 2 + i for i in range(2 * n)},
        compiler_params=pltpu.CompilerParams(has_side_effects=EFFECT),
    )(*[_in_hbm(s) for s in srcs], *[_in_hbm(l) for l in lands])
    return outs[0], outs[1], outs[2:2 + n], outs[2 + n:2 + 2 * n], outs[-1]


def pair_wait_one(name, send_sems, recv_sems, src, land, after, index):
    def body(src_ref, land_ref, send, recv, after_ref, src_out, land_out):
        x, y, c, _ = _place()
        cp = _remote(src_ref, land_ref, send.at[index], recv.at[index], (x, y, 1 - c))
        cp.wait_send()
        cp.wait_recv()

    return pl.pallas_call(
        body, name=name, in_specs=[HBM, HBM, SEM, SEM, ANY], out_specs=[HBM, HBM],
        out_shape=[pltpu.HBM(src.shape, src.dtype), pltpu.HBM(land.shape, land.dtype)],
        input_output_aliases={0: 0, 1: 1},
        compiler_params=pltpu.CompilerParams(has_side_effects=EFFECT),
    )(src, land, send_sems, recv_sems, after)


def pair_wait(name, make_copies, send_sems, recv_sems, srcs, lands, after):
    n = len(srcs)

    def body(*refs):
        src_refs, land_refs = refs[:n], refs[n:2 * n]
        for cp in make_copies(src_refs, land_refs, refs[2 * n], refs[2 * n + 1]):
            cp.wait_send()
            cp.wait_recv()

    outs = pl.pallas_call(
        body, name=name, in_specs=[HBM] * (2 * n) + [SEM, SEM, ANY], out_specs=[HBM] * (2 * n),
        out_shape=[pltpu.HBM(s.shape, s.dtype) for s in srcs] + [pltpu.HBM(l.shape, l.dtype) for l in lands],
        input_output_aliases={i: i for i in range(2 * n)},
        compiler_params=pltpu.CompilerParams(has_side_effects=EFFECT),
    )(*srcs, *lands, send_sems, recv_sems, after)
    return outs[:n], outs[n:]


def _chip_copies(parts, lands, send_sems, recv_sems):
    x, y, c, chips = _place()
    return [_remote(parts[k].at[2 * px + py], lands[k].at[t], send_sems.at[3 * k + t], recv_sems.at[3 * k + t], (px, py, c))
            for k in range(len(parts)) for t, (px, py) in enumerate(chips)]


def chip_parts_start(parts):
    n = len(parts)

    def body(*refs):
        srcs, lands_in = refs[:n], refs[n:2 * n]
        send_sems, recv_sems = refs[2 * n], refs[2 * n + 1]
        token = refs[-1]
        for cp in _chip_copies(srcs, lands_in, send_sems, recv_sems):
            cp.start()
        token[...] = jnp.zeros_like(token)

    lands = [lax.empty((3,) + p.shape[1:], p.dtype) for p in parts]
    outs = pl.pallas_call(
        body, name="chip_parts_start", in_specs=[HBM] * (2 * n), out_specs=[SEM, SEM] + [HBM] * (2 * n) + [VMEM_SPEC],
        out_shape=[pltpu.SemaphoreType.DMA((3 * n,)), pltpu.SemaphoreType.DMA((3 * n,))]
        + [pltpu.HBM(p.shape, p.dtype) for p in parts] + [pltpu.HBM(l.shape, l.dtype) for l in lands] + [TOKEN],
        input_output_aliases={i: 2 + i for i in range(2 * n)},
        compiler_params=pltpu.CompilerParams(has_side_effects=EFFECT),
    )(*[_in_hbm(p) for p in parts], *[_in_hbm(l) for l in lands])
    return outs[0], outs[1], outs[2:2 + n], outs[2 + n:2 + 2 * n], outs[-1]


def chip_parts_wait(send_sems, recv_sems, parts, lands, after):
    n = len(parts)

    def body(*refs):
        srcs, lands_in = refs[:n], refs[n:2 * n]
        send, recv = refs[2 * n], refs[2 * n + 1]
        for cp in _chip_copies(srcs, lands_in, send, recv):
            cp.wait_send()
            cp.wait_recv()

    outs = pl.pallas_call(
        body, name="chip_parts_wait", in_specs=[HBM] * (2 * n) + [SEM, SEM, ANY], out_specs=[HBM] * (2 * n),
        out_shape=[pltpu.HBM(p.shape, p.dtype) for p in parts] + [pltpu.HBM(l.shape, l.dtype) for l in lands],
        input_output_aliases={i: i for i in range(2 * n)},
        compiler_params=pltpu.CompilerParams(has_side_effects=EFFECT),
    )(*parts, *lands, send_sems, recv_sems, after)
    return outs[:n], outs[n:]


def allreduce_small(packed):
    R = packed.shape[0]

    def body(x_ref, sum_ref, all_ref, send_sems, recv_sems):
        x, y, c, chips = _place()
        me, sibling = (x, y, c), (x, y, 1 - c)

        def rows(px, py, pc):
            return all_ref.at[4 * px + 2 * py + pc]

        def copy(k, block, to, src=None):
            return _remote(rows(*block) if src is None else src, rows(*block), send_sems.at[k], recv_sems.at[k], to)

        all_ref[4 * x + 2 * y + c] = x_ref[...]
        first = [copy(0, me, sibling, src=x_ref)]
        first += [copy(1 + j, me, (*chip, c), src=x_ref) for j, chip in enumerate(chips)]
        for cp in first:
            cp.start()
        passed = [copy(4 + j, (*chip, c), sibling) for j, chip in enumerate(chips)]
        for j, chip in enumerate(chips):
            copy(1 + j, (*chip, c), me).wait_recv()
            passed[j].start()
        copy(0, sibling, me).wait_recv()
        for j, chip in enumerate(chips):
            copy(4 + j, (*chip, 1 - c), me).wait_recv()
        for cp in first + passed:
            cp.wait_send()

        def chunk(i, carry):
            rws = pl.ds(pl.multiple_of(i * PACK_ROWS, PACK_ROWS), PACK_ROWS)
            acc = all_ref[0, rws, :]
            for d in range(1, N_DEV):
                acc = acc + all_ref[d, rws, :]
            sum_ref[rws, :] = acc
            return carry

        lax.fori_loop(0, R // PACK_ROWS, chunk, 0)

    return pl.pallas_call(
        body, name="allreduce_small", in_specs=[VMEM_SPEC], out_specs=VMEM_SPEC,
        out_shape=jax.ShapeDtypeStruct((R, LANES), F32),
        scratch_shapes=[pltpu.VMEM((N_DEV, R, LANES), F32), pltpu.SemaphoreType.DMA((7,)), pltpu.SemaphoreType.DMA((7,))],
        compiler_params=pltpu.CompilerParams(vmem_limit_bytes=VMEM_LIMIT),
    )(packed)


def _mixer_fwd(x, h, p, tabs, token, late_weights=None):
    z = mm_nn_cols(h, p["w_in"], token)
    qr, kp, vp = rope_fwd(z, tabs)
    mix = attn_fwd(qr, kp, vp, p["sink3"])
    c1 = conv_dw_fwd(z, p["conv_w32"], p["conv_dw_b"])
    mix = conv_ln_fwd(c1, p["conv_ln_g"], p["conv_ln_b"], mix)
    mix = sgu_fwd(z, p["sgu_ln_g"], p["sgu_ln_b"], p["sgu_w16"], p["sgu_b3"], mix)
    if late_weights is not None:
        p.update(late_weights(mix))
    x_mid, h2 = mm_nn_rows_res(mix, p["w_out"], x, p["ffn_norm_g"])
    return x_mid, h2, dict(x=x, h=h, z=z, qr=qr, kp=kp, vp=vp, c1=c1, mix=mix, x_mid=x_mid)


def _ffn_fwd(x_mid, h2, p, next_gain, token):
    gate, up, act = ffn_up(h2, p["w_gate"], p["w_up"], token)
    x_out, h_next = mm_nn_rows_res(act, p["w_down"], x_mid, next_gain)
    return x_out, h_next, dict(h2=h2, gate=gate, up=up, act=act)


def _layer_fwd(x, h, p, next_gain, tabs, token):
    x_mid, h2, s_mix = _mixer_fwd(x, h, p, tabs, token)
    x_out, h_next, s_ffn = _ffn_fwd(x_mid, h2, p, next_gain, token)
    return x_out, h_next, {**s_mix, **s_ffn}


def _ffn_bwd(dxb, p, s, token):
    dgate, dup = ffn_down_bwd(dxb, p["w_down"], s["gate"], s["up"], token)
    g_down = mm_tn_rows(s["act"], dxb)
    dh2 = mm_nt_cols([(dgate, p["w_gate"]), (dup, p["w_up"])], BF16, 1)
    g_gate = mm_tn_cols(s["h2"], dgate, N_CHIPS)
    g_up = mm_tn_cols(s["h2"], dup, N_CHIPS)
    dmidb, g_ffn_norm = rms_bwd(s["x_mid"], p["ffn_norm_g"], dh2, dxb, BF16)
    return dmidb, [g_gate, g_up, g_down.reshape(N_CHIPS, -1, D_MODEL)], g_ffn_norm


def _mixer_bwd(dmidb, p, s, tabs, token, out_dtype):
    dmix = mm_nt_rows(dmidb, p["w_out"], token)
    g_out = mm_tn_rows(s["mix"], dmidb)
    dq, dkp, dvp, dsink = attn_bwd(s["qr"], s["kp"], s["vp"], p["sink3"], dmix)
    dz = rope_bwd(dq, dkp, dvp, tabs)
    dc1, g_cln_g, g_cln_b = conv_ln_bwd(dmix, s["c1"], p["conv_ln_g"], p["conv_ln_b"])
    dz, g_cw, g_cb = conv_dw_bwd(dc1, s["z"], p["conv_w32"], dz)
    dz, g_sw, g_sb, g_sln_g, g_sln_b = sgu_bwd(s["z"], dmix, p["sgu_ln_g"], p["sgu_ln_b"], p["sgu_w16"], p["sgu_b3"], dz)
    dh = mm_nt_cols([(dz, p["w_in"])], BF16, N_CHIPS)
    g_in = mm_tn_cols(s["h"], dz, N_CHIPS)
    dx_in, g_mix_norm = rms_bwd(s["x"], p["mix_norm_g"], dh, dmidb, out_dtype)
    small = dict(mix_norm_g=g_mix_norm, sink=dsink[:, :, 0].reshape(1, N_Q_HEADS), conv_dw_w=g_cw[:CONV_KERNEL],
                 conv_dw_b=g_cb, conv_ln_g=g_cln_g, conv_ln_b=g_cln_b, sgu_ln_g=g_sln_g, sgu_ln_b=g_sln_b,
                 sgu_w=g_sw, sgu_b=g_sb[:, :, 0])
    return dx_in, [g_in, g_out.reshape(N_CHIPS, -1, D_MODEL)], small


def _layer_bwd(dxb, p, s, tabs, token, out_dtype):
    dmidb, ffn_big, g_ffn_norm = _ffn_bwd(dxb, p, s, token)
    dx_in, mix_big, small = _mixer_bwd(dmidb, p, s, tabs, token, out_dtype)
    return dx_in, mix_big + ffn_big, dict(small, ffn_norm_g=g_ffn_norm)


def _mixer_weights(gathered):
    w_in, w_out = gathered
    return dict(w_in=w_in, w_out=w_out.reshape(-1, D_MODEL))


def _ffn_weights(gathered):
    w_gate, w_up, w_down = gathered
    return dict(w_gate=w_gate, w_up=w_up, w_down=w_down.reshape(-1, D_MODEL))


def _small_params(l, conv_w_full, mix_norm_g, sink, conv_dw_b, conv_ln_g, conv_ln_b, sgu_ln_g, sgu_ln_b, sgu_w, sgu_b,
                  ffn_norm_g):
    return dict(
        mix_norm_g=mix_norm_g[l:l + 1], ffn_norm_g=ffn_norm_g[l:l + 1],
        sink3=jnp.broadcast_to(sink[l].reshape(N_KV_HEADS, Q_PER_KV, 1), (N_KV_HEADS, Q_PER_KV, LANES)),
        conv_w32=jnp.pad(conv_w_full[l], ((0, 32 - CONV_KERNEL), (0, 0))),
        conv_dw_b=conv_dw_b[l:l + 1], conv_ln_g=conv_ln_g[l:l + 1], conv_ln_b=conv_ln_b[l:l + 1],
        sgu_ln_g=sgu_ln_g[l:l + 1], sgu_ln_b=sgu_ln_b[l:l + 1], sgu_w16=sgu_w[l].astype(BF16),
        sgu_b3=jnp.broadcast_to(sgu_b[l][:, :, None], (SGU_HEADS, CHUNK, CHUNK)))


_SMALL = ["mix_norm_g", "sink", "conv_dw_b", "conv_ln_g", "conv_ln_b", "sgu_ln_g", "sgu_ln_b", "sgu_w", "sgu_b", "ffn_norm_g",
          "final_norm_g"]


def _pack_rows(arrays):
    rows, counts = [], []
    for a in arrays:
        flat = a.reshape(-1)
        n = -(-flat.shape[0] // LANES)
        rows.append(jnp.pad(flat, (0, n * LANES - flat.shape[0])).reshape(n, LANES))
        counts.append(n)
    packed = jnp.concatenate(rows, axis=0)
    pad = -packed.shape[0] % PACK_ROWS
    return jnp.pad(packed, ((0, pad), (0, 0))), counts


def _unpack_rows(packed, counts, shapes):
    out, r = [], 0
    for n, shape in zip(counts, shapes):
        size = math.prod(shape)
        out.append(packed[r:r + n].reshape(-1)[:size].reshape(shape))
        r += n
    return out


def kernel(x, mix_norm_g, w_in, sink, conv_dw_w, conv_dw_b, conv_ln_g, conv_ln_b, sgu_ln_g, sgu_ln_b, sgu_w, sgu_b, w_out, ffn_norm_g, w_gate, w_up, w_down, final_norm_g, loss_target, m_mix_norm_g, m_w_in, m_sink, m_conv_dw_w, m_conv_dw_b, m_conv_ln_g, m_conv_ln_b, m_sgu_ln_g, m_sgu_ln_b, m_sgu_w, m_sgu_b, m_w_out, m_ffn_norm_g, m_w_gate, m_w_up, m_w_down, m_final_norm_g, v_mix_norm_g, v_w_in, v_sink, v_conv_dw_w, v_conv_dw_b, v_conv_ln_g, v_conv_ln_b, v_sgu_ln_g, v_sgu_ln_b, v_sgu_w, v_sgu_b, v_w_out, v_ffn_norm_g, v_w_gate, v_w_up, v_w_down, v_final_norm_g):
    S = x.shape[1]
    my_chip = 2 * lax.axis_index("x") + lax.axis_index("y")
    c_idx = lax.axis_index("c").astype(jnp.int32).reshape(1)
    big_w = [w_in, w_out, w_gate, w_up, w_down]
    big_m = [m_w_in, m_w_out, m_w_gate, m_w_up, m_w_down]
    big_v = [v_w_in, v_w_out, v_w_gate, v_w_up, v_w_down]
    n_kinds = len(big_w)

    x_idx = lax.axis_index("x").astype(jnp.int32).reshape(1)
    y_idx = lax.axis_index("y").astype(jnp.int32).reshape(1)
    conv_w_all = gather_small(conv_dw_w)
    conv_w_full = jnp.transpose(conv_w_all, (1, 2, 0, 3)).reshape(DEPTH, CONV_KERNEL, CONV_WIDTH)
    tabs = rope_tables(S)
    no_token = jnp.zeros(TOKEN.shape, TOKEN.dtype)

    mixer_kinds, ffn_kinds = [0, 1], [2, 3, 4]
    def shards(layer, kinds, token):
        return [cast_layer(big_w[k], layer, token) for k in kinds]

    def fetch(pending, after):
        send_sems, recv_sems, srcs, lands, _ = pending
        return forward_halves(gather_wait(send_sems, recv_sems, srcs, lands, after))

    all_kinds = mixer_kinds + ffn_kinds
    first_mixer = gather_start(shards(0, mixer_kinds, no_token), conv_w_all)
    first_ffn = gather_start(shards(0, ffn_kinds, first_mixer[4]), first_mixer[4])
    pending = gather_start(shards(1, all_kinds, first_ffn[4]), first_ffn[4])
    act = x[0]
    h = rms_fwd(act, mix_norm_g[0:1], no_token)
    saved, params = [], []
    for l in range(DEPTH):
        p = _small_params(l, conv_w_full, mix_norm_g, sink, conv_dw_b, conv_ln_g, conv_ln_b, sgu_ln_g, sgu_ln_b, sgu_w, sgu_b,
                          ffn_norm_g)
        next_gain = mix_norm_g[l + 1:l + 2] if l + 1 < DEPTH else final_norm_g.reshape(1, D_MODEL)
        if l == 0:
            p.update(_mixer_weights(fetch(first_mixer, act)))
            x_mid, h2, s_mix = _mixer_fwd(act, h, p, tabs, pending[4])
            p.update(_ffn_weights(fetch(first_ffn, x_mid)))
            late, token = None, no_token
        else:
            send_sems, recv_sems, srcs, lands, _ = pending
            lands = gather_wait(send_sems, recv_sems, srcs, lands, act)
            w_in_full = forward_halves(lands[:1])[0]
            p.update(w_in=w_in_full)
            fwd_send, fwd_recv, rest, token = forward_start(lands[1:])

            def late(mix, fwd_send=fwd_send, fwd_recv=fwd_recv, rest=rest):
                w_out_full, *ffn_full = forward_wait(fwd_send, fwd_recv, rest, mix)
                return dict(_ffn_weights(ffn_full), w_out=w_out_full.reshape(-1, D_MODEL))

            if l + 1 < DEPTH:
                pending = gather_start(shards(l + 1, all_kinds, token), w_in_full)
                token = token + pending[4]
        if l > 0:
            x_mid, h2, s_mix = _mixer_fwd(act, h, p, tabs, token, late)
        act, h, s_ffn = _ffn_fwd(x_mid, h2, p, next_gain, token)
        params.append(p)
        saved.append({**s_mix, **s_ffn})
    loss_part, dxb, g_final = final_loss(act, final_norm_g.reshape(1, D_MODEL), loss_target[0])
    loss = lax.psum(loss_part[0, 0], ("x", "y", "c"))

    halves = [lax.empty((DEPTH, w.shape[1] // 2, w.shape[2]), F32) for w in big_w]
    small_grads = [None] * DEPTH

    def chip_start(layer, kinds, grads, recv):
        chip_sum = [add_sibling_half(g, r, c_idx) for g, r in zip(grads, recv)]
        send_sems, recv_sems, parts, lands, token = chip_parts_start(chip_sum)
        return (layer, kinds, send_sems, recv_sems, parts, lands), token

    def reduce_start(layer, kinds, grads):
        return chip_start(layer, kinds, grads, exchange_sibling_halves(grads))

    def reduce_finish(pending, halves, after):
        layer, kinds, send_sems, recv_sems, parts, lands = pending
        parts, others = chip_parts_wait(send_sems, recv_sems, parts, lands, after)
        halves = list(halves)
        for i, k in enumerate(kinds):
            halves[k] = sum_chips(parts[i], others[i], halves[k], x_idx, y_idx, layer)
        return halves

    pending, token = None, no_token
    for l in reversed(range(DEPTH)):
        dmidb, ffn_big, g_ffn_norm = _ffn_bwd(dxb, params[l], saved[l], token)
        if l == 0:
            last_ffn, mixer_token = reduce_start(l, ffn_kinds, ffn_big)
        else:
            half_shapes = [(g.shape[0], g.shape[1] // 2, g.shape[2]) for g in ffn_big]
            sib_send, sib_recv, ffn_big, ffn_lands, mixer_token = pair_start("sibling_start", _sibling_half_copies, ffn_big, half_shapes)
        dxb, mix_big, small = _mixer_bwd(dmidb, params[l], saved[l], tabs, mixer_token, F32 if l == 0 else BF16)
        small_grads[l] = dict(small, ffn_norm_g=g_ffn_norm)
        if pending is not None:
            halves = reduce_finish(pending, halves, dxb)
        if l == 0:
            last_mixer, token = reduce_start(l, mixer_kinds, mix_big)
            halves = reduce_finish(last_ffn, halves, token)
        else:
            ffn_big, ffn_recv = pair_wait("sibling_wait", _sibling_half_copies, sib_send, sib_recv, ffn_big, ffn_lands, dxb)
            mix_recv = exchange_sibling_halves(mix_big)
            pending, token = chip_start(l, mixer_kinds + ffn_kinds, list(mix_big) + list(ffn_big), list(mix_recv) + list(ffn_recv))

    def final_start(kinds):
        send_sems, recv_sems, mine, lands, _ = pair_start("final_start", _sibling_whole_copies, [halves[k] for k in kinds],
                                                          [halves[k].shape for k in kinds])
        return send_sems, recv_sems, mine, lands

    ffn_final = final_start(ffn_kinds)

    stacked = {n: jnp.stack([small_grads[l][n] for l in range(DEPTH)]) for n in small_grads[0]}
    stacked["final_norm_g"] = g_final
    packed, counts = _pack_rows([stacked[n] for n in _SMALL] + [stacked["conv_dw_w"]])
    reduced = allreduce_small(packed)
    small_w = dict(mix_norm_g=mix_norm_g, sink=sink, conv_dw_b=conv_dw_b, conv_ln_g=conv_ln_g, conv_ln_b=conv_ln_b,
                   sgu_ln_g=sgu_ln_g, sgu_ln_b=sgu_ln_b, sgu_w=sgu_w, sgu_b=sgu_b, ffn_norm_g=ffn_norm_g,
                   final_norm_g=final_norm_g)
    small_m = dict(mix_norm_g=m_mix_norm_g, sink=m_sink, conv_dw_b=m_conv_dw_b, conv_ln_g=m_conv_ln_g,
                   conv_ln_b=m_conv_ln_b, sgu_ln_g=m_sgu_ln_g, sgu_ln_b=m_sgu_ln_b, sgu_w=m_sgu_w, sgu_b=m_sgu_b,
                   ffn_norm_g=m_ffn_norm_g, final_norm_g=m_final_norm_g)
    small_v = dict(mix_norm_g=v_mix_norm_g, sink=v_sink, conv_dw_b=v_conv_dw_b, conv_ln_g=v_conv_ln_g,
                   conv_ln_b=v_conv_ln_b, sgu_ln_g=v_sgu_ln_g, sgu_ln_b=v_sgu_ln_b, sgu_w=v_sgu_w, sgu_b=v_sgu_b,
                   ffn_norm_g=v_ffn_norm_g, final_norm_g=v_final_norm_g)
    shapes = [small_w[n].shape for n in _SMALL] + [(DEPTH, CONV_KERNEL, CONV_WIDTH)]
    red = _unpack_rows(reduced, counts, shapes)
    g_small = dict(zip(_SMALL, red[:-1]))
    g_small["conv_dw_w"] = lax.dynamic_slice_in_dim(red[-1], my_chip * LANES, LANES, axis=2)
    small_w["conv_dw_w"], small_m["conv_dw_w"], small_v["conv_dw_w"] = conv_dw_w, m_conv_dw_w, v_conv_dw_w
    names = _SMALL + ["conv_dw_w"]
    pw, cnt = _pack_rows([small_w[n] for n in names])
    pg, _ = _pack_rows([g_small[n] for n in names])
    pm, _ = _pack_rows([small_m[n] for n in names])
    pv, _ = _pack_rows([small_v[n] for n in names])
    sd, sm, sv = adamw(pw, pg, pm, pv)
    shp = [small_w[n].shape for n in names]
    d_small = dict(zip(names, _unpack_rows(sd, cnt, shp)))
    m_small = dict(zip(names, _unpack_rows(sm, cnt, shp)))
    v_small = dict(zip(names, _unpack_rows(sv, cnt, shp)))

    big_names = ["w_in", "w_out", "w_gate", "w_up", "w_down"]
    g_big, d_big, m_big, v_big = {}, {}, {}, {}
    after = sd
    for kinds, final in ((ffn_kinds, ffn_final), (mixer_kinds, None)):
        if final is None:
            halves = reduce_finish(last_mixer, halves, after)
            final = final_start(kinds)
        send_sems, recv_sems, sent, lands = final
        for i, k in enumerate(kinds):
            n = big_names[k]
            mine, theirs = pair_wait_one("final_wait", send_sems, recv_sems, sent[i], lands[i], after, i)
            g_big[n], d_big[n], m_big[n], v_big[n] = adamw_halves(big_w[k], mine, theirs, big_m[k], big_v[k], c_idx)
            after = d_big[n]

    order = ["mix_norm_g", "w_in", "sink", "conv_dw_w", "conv_dw_b", "conv_ln_g", "conv_ln_b", "sgu_ln_g", "sgu_ln_b",
             "sgu_w", "sgu_b", "w_out", "ffn_norm_g", "w_gate", "w_up", "w_down", "final_norm_g"]
    grads = {**g_small, **g_big}
    deltas = {**d_small, **d_big}
    new_m = {**m_small, **m_big}
    new_v = {**v_small, **v_big}
    return (loss, dxb[None], *[grads[n] for n in order], *[deltas[n] for n in order],
            *[new_m[n] for n in order], *[new_v[n] for n in order])
```

```python
import math

import jax
import jax.numpy as jnp
from jax import lax
from jax.experimental import pallas as pl
from jax.experimental.pallas import tpu as pltpu

F32, BF16 = jnp.float32, jnp.bfloat16

D_MODEL = 2048
DEPTH = 4
HEAD_DIM = 128
N_Q_HEADS = 8
N_KV_HEADS = 2
Q_PER_KV = N_Q_HEADS // N_KV_HEADS
ATTN_WIDTH = N_Q_HEADS * HEAD_DIM
KV_WIDTH = N_KV_HEADS * HEAD_DIM
CONV_WIDTH = 512
CONV_KERNEL = 31
CONV_PAD = 16
SGU_WIDTH = 512
SGU_HEADS = 4
CHUNK = 128
IN_WIDTH = 3584
D_FF = 5632
WINDOW = 128
ROT_DIM = 32
ROPE_THETA = 500000.0
EPS = 1e-6
N_CHIPS = 4
N_DEV = 8
LANES = 128
PACK_ROWS = 64
OFF_K = ATTN_WIDTH
OFF_V = OFF_K + KV_WIDTH
OFF_CA = OFF_V + KV_WIDTH
OFF_CG = OFF_CA + CONV_WIDTH
OFF_U = OFF_CG + CONV_WIDTH
OFF_VV = OFF_U + SGU_WIDTH

ADAM_LR, ADAM_B1, ADAM_B2, ADAM_EPS, ADAM_WD, ADAM_STEP = 0.001, 0.9, 0.999, 1e-08, 0.01, 10

VMEM_LIMIT = 56 * 1024 * 1024
MESH = pl.DeviceIdType.MESH
HBM = pl.BlockSpec(memory_space=pltpu.HBM)
VMEM_SPEC = pl.BlockSpec(memory_space=pltpu.VMEM)


def _call(name, body, *, grid, in_specs, out_specs, out_shape, scratch=(), sem=None, aliases=None):
    params = dict(vmem_limit_bytes=VMEM_LIMIT)
    if sem is not None:
        params["dimension_semantics"] = sem
    return pl.pallas_call(
        body, name=name, grid=grid, in_specs=in_specs, out_specs=out_specs, out_shape=out_shape,
        scratch_shapes=list(scratch), input_output_aliases=aliases or {}, compiler_params=pltpu.CompilerParams(**params))


def _sigmoid(x):
    return 1.0 / (1.0 + jnp.exp(-x))


def rms_fwd(x, g, token):
    S = x.shape[0]
    tm = min(512, S)

    def body(x_ref, g_ref, token_ref, o_ref):
        xv = x_ref[...]
        r = lax.rsqrt(jnp.mean(xv * xv, axis=-1, keepdims=True) + EPS)
        o_ref[...] = (xv * r * g_ref[...]).astype(BF16)

    return _call("rms_fwd", body, grid=(S // tm,),
                 in_specs=[pl.BlockSpec((tm, D_MODEL), lambda i: (i, 0)), pl.BlockSpec((1, D_MODEL), lambda i: (0, 0)),
                           pl.BlockSpec((8, LANES), lambda i: (0, 0))],
                 out_specs=pl.BlockSpec((tm, D_MODEL), lambda i: (i, 0)),
                 out_shape=jax.ShapeDtypeStruct((S, D_MODEL), BF16), sem=("parallel",))(x, g, token)


def _rms_bwd_math(xv, gv, dh):
    r = lax.rsqrt(jnp.mean(xv * xv, axis=-1, keepdims=True) + EPS)
    n = xv * r
    dn = dh * gv
    dx = r * (dn - n * jnp.mean(dn * n, axis=-1, keepdims=True))
    dg = jnp.sum(dh * n, axis=0, keepdims=True)
    return dx, dg


def rms_bwd(x, g, dh, dres, out_dtype):
    S = x.shape[0]
    tm = min(512, S)

    def body(x_ref, g_ref, dh_ref, dres_ref, dx_ref, dg_ref):
        dx, dg = _rms_bwd_math(x_ref[...], g_ref[...], dh_ref[...].astype(F32))
        dx_ref[...] = (dx + dres_ref[...].astype(F32)).astype(out_dtype)

        @pl.when(pl.program_id(0) == 0)
        def _():
            dg_ref[...] = dg

        @pl.when(pl.program_id(0) > 0)
        def _():
            dg_ref[...] += dg

    row = pl.BlockSpec((tm, D_MODEL), lambda i: (i, 0))
    vec = pl.BlockSpec((1, D_MODEL), lambda i: (0, 0))
    return _call("rms_bwd", body, grid=(S // tm,), in_specs=[row, vec, row, row], out_specs=[row, vec],
                 out_shape=[jax.ShapeDtypeStruct((S, D_MODEL), out_dtype), jax.ShapeDtypeStruct((1, D_MODEL), F32)],
                 sem=("arbitrary",))(x, g, dh, dres)


def final_loss(x, g, target):
    S = x.shape[0]
    tm = min(256, S)

    def body(x_ref, g_ref, t_ref, loss_ref, dxb_ref, dg_ref):
        xv = x_ref[...]
        gv = g_ref[...]
        r = lax.rsqrt(jnp.mean(xv * xv, axis=-1, keepdims=True) + EPS)
        err = xv * r * gv - t_ref[...]
        part = 0.5 * jnp.sum(jnp.mean(err * err, axis=-1, keepdims=True), axis=0, keepdims=True)
        dx, dg = _rms_bwd_math(xv, gv, err * (1.0 / D_MODEL))
        dxb_ref[...] = dx.astype(BF16)

        @pl.when(pl.program_id(0) == 0)
        def _():
            dg_ref[...] = dg
            loss_ref[...] = part

        @pl.when(pl.program_id(0) > 0)
        def _():
            dg_ref[...] += dg
            loss_ref[...] += part

    row = pl.BlockSpec((tm, D_MODEL), lambda i: (i, 0))
    vec = pl.BlockSpec((1, D_MODEL), lambda i: (0, 0))
    one = pl.BlockSpec((1, 1), lambda i: (0, 0))
    return _call("final_loss", body, grid=(S // tm,), in_specs=[row, vec, row], out_specs=[one, row, vec],
                 out_shape=[jax.ShapeDtypeStruct((1, 1), F32), jax.ShapeDtypeStruct((S, D_MODEL), BF16),
                            jax.ShapeDtypeStruct((1, D_MODEL), F32)],
                 sem=("arbitrary",))(x, g, target)


EPILOGUE_ROWS = 256
NN = (((1,), (0,)), ((), ()))
NT = (((1,), (1,)), ((), ()))
TN = (((0,), (0,)), ((), ()))


def _matmul(name, operands, in_specs, out_shape, out_specs, grid, pairs, dims, acc_shape, epilogue):
    n_in, n_out, nk = len(operands), len(out_shape), grid[-1]

    def body(*refs):
        ins, outs = refs[:n_in], refs[n_in:n_in + n_out]
        part = None
        for ia, ib in pairs:
            d = lax.dot_general(ins[ia][...], ins[ib][...], dims, preferred_element_type=F32)
            part = d if part is None else part + d
        if nk == 1:
            epilogue(part, ins, outs)
        else:
            acc = refs[-1]
            k = pl.program_id(len(grid) - 1)

            @pl.when(k == 0)
            def _():
                acc[...] = part

            @pl.when(k > 0)
            def _():
                acc[...] += part

            @pl.when(k == nk - 1)
            def _():
                epilogue(acc[...], ins, outs)

    scratch = [pltpu.VMEM(acc_shape, F32)] if nk > 1 else []
    sem = ("parallel",) * (len(grid) - 1) + ("arbitrary",)
    return _call(name, body, grid=grid, in_specs=in_specs, out_specs=out_specs, out_shape=out_shape,
                 scratch=scratch, sem=sem)(*operands)


def _store(dtype):
    def epilogue(acc, ins, outs):
        outs[0][...] = acc.astype(dtype)
    return epilogue


def mm_nn_cols(a, w, token):
    S, K = a.shape
    J, _, Ns = w.shape
    tm = min(512, S)
    return _matmul("mm_nn_cols", (a, w, token),
                   [pl.BlockSpec((tm, K), lambda j, i, k: (i, 0)), pl.BlockSpec((None, K, Ns), lambda j, i, k: (j, 0, 0)),
                    pl.BlockSpec((8, LANES), lambda j, i, k: (0, 0))],
                   [jax.ShapeDtypeStruct((S, J * Ns), BF16)], [pl.BlockSpec((tm, Ns), lambda j, i, k: (i, j))],
                   (J, S // tm, 1), [(0, 1)], NN, None, _store(BF16))[0]


def ffn_up(h, wg, wu, token):
    S, K = h.shape
    J, _, Ns = wg.shape
    tm = min(512, S)

    sub = min(EPILOGUE_ROWS, tm)

    def body(h_ref, wg_ref, wu_ref, token_ref, g_ref, u_ref, a_ref):
        for r in range(tm // sub):
            rows = slice(r * sub, (r + 1) * sub)
            hv = h_ref[rows, :]
            gv = jnp.dot(hv, wg_ref[...], preferred_element_type=F32)
            uv = jnp.dot(hv, wu_ref[...], preferred_element_type=F32)
            g_ref[rows, :] = gv.astype(BF16)
            u_ref[rows, :] = uv.astype(BF16)
            a_ref[rows, :] = (gv * _sigmoid(gv) * uv).astype(BF16)

    wspec = pl.BlockSpec((None, K, Ns), lambda j, i: (j, 0, 0))
    ospec = pl.BlockSpec((tm, Ns), lambda j, i: (i, j))
    oshape = jax.ShapeDtypeStruct((S, J * Ns), BF16)
    return _call("ffn_up", body, grid=(J, S // tm),
                 in_specs=[pl.BlockSpec((tm, K), lambda j, i: (i, 0)), wspec, wspec, pl.BlockSpec((8, LANES), lambda j, i: (0, 0))],
                 out_specs=[ospec, ospec, ospec], out_shape=[oshape, oshape, oshape],
                 sem=("parallel", "parallel"))(h, wg, wu, token)


def mm_nn_rows_res(a, w, res, gain):
    S, K = a.shape
    N = w.shape[1]
    tm = min(512, S)
    tk, tn = (K, N) if K <= 2048 else (K // 2, N // 2)
    n_n, n_k = N // tn, K // tk

    def body(a_ref, w_ref, res_ref, g_ref, x_ref, h_ref, *acc):
        n, k = pl.program_id(1), pl.program_id(2)

        def normed(xv):
            r = lax.rsqrt(jnp.mean(xv * xv, axis=-1, keepdims=True) + EPS)
            h_ref[...] = (xv * r * g_ref[...]).astype(BF16)

        def store_columns(total):
            if n_n == 1:
                xv = total + res_ref[...]
                x_ref[...] = xv
                normed(xv)
                return
            for c in range(n_n):
                @pl.when(n == c)
                def _(c=c):
                    cols = slice(c * tn, (c + 1) * tn)
                    x_ref[:, cols] = total + res_ref[:, cols]

            @pl.when(n == n_n - 1)
            def _():
                normed(x_ref[...])

        part = jnp.dot(a_ref[...], w_ref[...], preferred_element_type=F32)
        if n_k == 1:
            store_columns(part)
        else:
            @pl.when(k == 0)
            def _():
                acc[0][...] = part

            @pl.when(k > 0)
            def _():
                acc[0][...] += part

            @pl.when(k == n_k - 1)
            def _():
                store_columns(acc[0][...])

    row = pl.BlockSpec((tm, N), lambda i, n, k: (i, 0))
    return _call("mm_nn_rows_res", body, grid=(S // tm, n_n, n_k),
                 in_specs=[pl.BlockSpec((tm, tk), lambda i, n, k: (i, k)), pl.BlockSpec((tk, tn), lambda i, n, k: (k, n)), row,
                           pl.BlockSpec((1, N), lambda i, n, k: (0, 0))],
                 out_specs=[row, row], out_shape=[jax.ShapeDtypeStruct((S, N), F32), jax.ShapeDtypeStruct((S, N), BF16)],
                 scratch=[pltpu.VMEM((tm, tn), F32)] if n_k > 1 else [],
                 sem=("parallel", "arbitrary", "arbitrary"))(a, w, res, gain)


def mm_nt_cols(pairs_in, out_dtype, shards_per_step):
    dz0, w0 = pairs_in[0]
    S = dz0.shape[0]
    J, K, Ns = w0.shape
    tm = min(512, S)
    sps = shards_per_step
    operands, specs, pairs = [], [], []
    for dz, w in pairs_in:
        for s in range(sps):
            pairs.append((len(operands), len(operands) + 1))
            operands += [dz, w]
            specs += [pl.BlockSpec((tm, Ns), lambda i, j, s=s: (i, j * sps + s)),
                      pl.BlockSpec((None, K, Ns), lambda i, j, s=s: (j * sps + s, 0, 0))]
    return _matmul("mm_nt_cols%d" % len(pairs_in), tuple(operands), specs,
                   [jax.ShapeDtypeStruct((S, K), out_dtype)], [pl.BlockSpec((tm, K), lambda i, j: (i, 0))],
                   (S // tm, J // sps), pairs, NT, (tm, K), _store(out_dtype))[0]


def mm_nt_rows(dy, w, token):
    S, N = dy.shape
    K = w.shape[0]
    tm, tko = min(1024, S), 512
    return _matmul("mm_nt_rows", (dy, w, token),
                   [pl.BlockSpec((tm, N), lambda i, kk, z: (i, 0)), pl.BlockSpec((tko, N), lambda i, kk, z: (kk, 0)),
                    pl.BlockSpec((8, LANES), lambda i, kk, z: (0, 0))],
                   [jax.ShapeDtypeStruct((S, K), BF16)], [pl.BlockSpec((tm, tko), lambda i, kk, z: (i, kk))],
                   (S // tm, K // tko, 1), [(0, 1)], NT, None, _store(BF16))[0]


def ffn_down_bwd(dy, w, gate, up, token):
    S, N = dy.shape
    K = w.shape[0]
    tm, tko = min(1024, S), 512
    sub = min(EPILOGUE_ROWS, tm)

    def body(dy_ref, w_ref, g_ref, u_ref, token_ref, dg_ref, du_ref):
        for r in range(tm // sub):
            rows = slice(r * sub, (r + 1) * sub)
            dact = lax.dot_general(dy_ref[rows, :], w_ref[...], NT, preferred_element_type=F32)
            gv = g_ref[rows, :].astype(F32)
            uv = u_ref[rows, :].astype(F32)
            sg = _sigmoid(gv)
            dg_ref[rows, :] = (dact * uv * sg * (1.0 + gv * (1.0 - sg))).astype(BF16)
            du_ref[rows, :] = (dact * gv * sg).astype(BF16)

    tile = pl.BlockSpec((tm, tko), lambda i, kk: (i, kk))
    oshape = jax.ShapeDtypeStruct((S, K), BF16)
    return _call("ffn_down_bwd", body, grid=(S // tm, K // tko),
                 in_specs=[pl.BlockSpec((tm, N), lambda i, kk: (i, 0)), pl.BlockSpec((tko, N), lambda i, kk: (kk, 0)), tile, tile,
                           pl.BlockSpec((8, LANES), lambda i, kk: (0, 0))],
                 out_specs=[tile, tile], out_shape=[oshape, oshape], sem=("parallel", "parallel"))(dy, w, gate, up, token)


def mm_tn_cols(a, dz, J):
    S, M = a.shape
    Ns = dz.shape[1] // J
    tm, tk = 512, S
    return _matmul("mm_tn_cols", (a, dz),
                   [pl.BlockSpec((tk, tm), lambda j, m, k: (k, m)), pl.BlockSpec((tk, Ns), lambda j, m, k: (k, j))],
                   [jax.ShapeDtypeStruct((J, M, Ns), BF16)], [pl.BlockSpec((None, tm, Ns), lambda j, m, k: (j, m, 0))],
                   (J, M // tm, S // tk), [(0, 1)], TN, (tm, Ns), _store(BF16))[0]


def mm_tn_rows(a, dy):
    S, K = a.shape
    N = dy.shape[1]
    tm, tk = 512, S
    return _matmul("mm_tn_rows", (a, dy),
                   [pl.BlockSpec((tk, tm), lambda m, k: (k, m)), pl.BlockSpec((tk, N), lambda m, k: (k, 0))],
                   [jax.ShapeDtypeStruct((K, N), BF16)], [pl.BlockSpec((tm, N), lambda m, k: (m, 0))],
                   (K // tm, S // tk), [(0, 1)], TN, (tm, N), _store(BF16))[0]


def rope_tables(S):
    half = ROT_DIM // 2
    pos = jnp.arange(S, dtype=F32)
    inv = ROPE_THETA ** (-jnp.arange(0, ROT_DIM, 2, dtype=F32) / ROT_DIM)
    ang = pos[:, None] * inv[None, :]
    cos, sin = jnp.cos(ang), jnp.sin(ang)
    zeros = jnp.zeros((S, HEAD_DIM - ROT_DIM), F32)
    c = jnp.concatenate([cos, cos, jnp.ones((S, HEAD_DIM - ROT_DIM), F32)], axis=1)
    s_lo = jnp.concatenate([-sin, jnp.zeros((S, half), F32), zeros], axis=1)
    s_hi = jnp.concatenate([jnp.zeros((S, half), F32), sin, zeros], axis=1)
    return c, s_lo, s_hi


ROPE_ROWS = 512


def _rope(t, c, s_lo, s_hi):
    half = ROT_DIM // 2
    return t * c + pltpu.roll(t, HEAD_DIM - half, 1) * s_lo + pltpu.roll(t, half, 1) * s_hi


def _unrope(d, c, s_lo, s_hi):
    half = ROT_DIM // 2
    return d * c + pltpu.roll(d * s_lo, half, 1) + pltpu.roll(d * s_hi, HEAD_DIM - half, 1)


def rope_fwd(z, tabs):
    S = z.shape[0]
    T = min(ROPE_ROWS, S)

    def body(q_ref, kv_ref, c_ref, sl_ref, sh_ref, qr_ref, kp_ref, vp_ref):
        i = pl.program_id(0)

        @pl.when(i == 0)
        def _():
            zero = jnp.zeros((CHUNK, KV_WIDTH), BF16)
            kp_ref[0:CHUNK, :] = zero
            vp_ref[0:CHUNK, :] = zero
            kp_ref[S + CHUNK:S + 2 * CHUNK, :] = zero
            vp_ref[S + CHUNK:S + 2 * CHUNK, :] = zero

        c, sl, sh = c_ref[...], sl_ref[...], sh_ref[...]
        for h in range(N_Q_HEADS):
            cols = slice(h * HEAD_DIM, (h + 1) * HEAD_DIM)
            qr_ref[:, cols] = _rope(q_ref[:, cols].astype(F32), c, sl, sh).astype(BF16)
        rows = pl.ds(pl.multiple_of(CHUNK + i * T, CHUNK), T)
        for g in range(N_KV_HEADS):
            cols = slice(g * HEAD_DIM, (g + 1) * HEAD_DIM)
            kp_ref[rows, cols] = _rope(kv_ref[:, cols].astype(F32), c, sl, sh).astype(BF16)
        vp_ref[rows, :] = kv_ref[:, KV_WIDTH:2 * KV_WIDTH]

    tab = pl.BlockSpec((T, HEAD_DIM), lambda i: (i, 0))
    pad = pl.BlockSpec((S + 2 * CHUNK, KV_WIDTH), lambda i: (0, 0))
    return _call("rope_fwd", body, grid=(S // T,),
                 in_specs=[pl.BlockSpec((T, ATTN_WIDTH), lambda i: (i, 0)),
                           pl.BlockSpec((T, 2 * KV_WIDTH), lambda i: (i, OFF_K // (2 * KV_WIDTH))), tab, tab, tab],
                 out_specs=[pl.BlockSpec((T, ATTN_WIDTH), lambda i: (i, 0)), pad, pad],
                 out_shape=[jax.ShapeDtypeStruct((S, ATTN_WIDTH), BF16), jax.ShapeDtypeStruct((S + 2 * CHUNK, KV_WIDTH), BF16),
                            jax.ShapeDtypeStruct((S + 2 * CHUNK, KV_WIDTH), BF16)], sem=("arbitrary",))(z, z, *tabs)


def rope_bwd(dq, dkp, dvp, tabs):
    S = dq.shape[0]
    T = min(ROPE_ROWS, S)

    def body(dq_ref, dk_ref, dv_ref, c_ref, sl_ref, sh_ref, o_ref):
        c, sl, sh = c_ref[...], sl_ref[...], sh_ref[...]
        for h in range(N_Q_HEADS):
            cols = slice(h * HEAD_DIM, (h + 1) * HEAD_DIM)
            o_ref[:, cols] = _unrope(dq_ref[:, cols], c, sl, sh).astype(BF16)
        rows = pl.ds(pl.multiple_of(CHUNK + pl.program_id(0) * T, CHUNK), T)
        for g in range(N_KV_HEADS):
            cols = slice(g * HEAD_DIM, (g + 1) * HEAD_DIM)
            o_ref[:, OFF_K + g * HEAD_DIM:OFF_K + (g + 1) * HEAD_DIM] = _unrope(dk_ref[rows, cols], c, sl, sh).astype(BF16)
        o_ref[:, OFF_V:OFF_V + KV_WIDTH] = dv_ref[rows, :].astype(BF16)

    tab = pl.BlockSpec((T, HEAD_DIM), lambda i: (i, 0))
    pad = pl.BlockSpec((S + 2 * CHUNK, KV_WIDTH), lambda i: (0, 0))
    return _call("rope_bwd", body, grid=(S // T,),
                 in_specs=[pl.BlockSpec((T, ATTN_WIDTH), lambda i: (i, 0)), pad, pad, tab, tab, tab],
                 out_specs=pl.BlockSpec((T, OFF_CA), lambda i: (i, 0)),
                 out_shape=jax.ShapeDtypeStruct((S, IN_WIDTH), BF16), sem=("parallel",))(dq, dkp, dvp, *tabs)


STACK = Q_PER_KV * CHUNK


def _stack_heads(ref, rows):
    return jnp.concatenate([ref[rows, r * HEAD_DIM:(r + 1) * HEAD_DIM] for r in range(Q_PER_KV)], axis=0)


def _stack_sinks(s_ref):
    return jnp.concatenate([jnp.broadcast_to(s_ref[r:r + 1, 0:1], (CHUNK, 1)) for r in range(Q_PER_KV)], axis=0)


MASKED = -1e30


def _scores(q, kb):
    return lax.dot_general(q, kb, NT, preferred_element_type=F32) * (1.0 / math.sqrt(HEAD_DIM))


def _band_bias():
    row = lax.broadcasted_iota(jnp.int32, (STACK, 3 * CHUNK), 0) & (CHUNK - 1)
    col = lax.broadcasted_iota(jnp.int32, (STACK, 3 * CHUNK), 1)
    return jnp.where(jnp.abs(col - CHUNK - row) <= WINDOW, 0.0, MASKED).astype(F32)


def _edge_bias(n, S):
    kpos = (n - 1) * CHUNK + lax.broadcasted_iota(jnp.int32, (1, 3 * CHUNK), 1)
    return jnp.where((kpos >= 0) & (kpos < S), 0.0, MASKED).astype(F32)


def _softmax_sink(s, sk, bias):
    s = s + bias
    m = jnp.maximum(jnp.max(s, axis=1, keepdims=True), sk)
    e = jnp.exp(s - m)
    es = jnp.exp(sk - m)
    inv = 1.0 / (jnp.sum(e, axis=1, keepdims=True) + es)
    return e * inv, es * inv


def _block_views(i, nblk):
    ns = [i * nblk + b for b in range(nblk)]
    wins = [pl.ds(pl.multiple_of(n * CHUNK, CHUNK), 3 * CHUNK) for n in ns]
    rows = [slice(b * CHUNK, (b + 1) * CHUNK) for b in range(nblk)]
    return ns, wins, rows


def attn_fwd(qr, kp, vp, sink3):
    S = qr.shape[0]
    tq = min(2048, S)
    gw = Q_PER_KV * HEAD_DIM
    nblk = tq // CHUNK

    def body(q_ref, k_ref, v_ref, s_ref, o_ref):
        ns, wins, rows = _block_views(pl.program_id(1), nblk)
        sk = _stack_sinks(s_ref)
        band = _band_bias()
        scores = [_scores(_stack_heads(q_ref, rows[b]), k_ref[wins[b], :]) for b in range(nblk)]
        probs = [_softmax_sink(scores[b], sk, band + _edge_bias(ns[b], S))[0].astype(BF16) for b in range(nblk)]
        outs = [jnp.dot(probs[b], v_ref[wins[b], :], preferred_element_type=F32).astype(BF16) for b in range(nblk)]
        for b in range(nblk):
            for r in range(Q_PER_KV):
                o_ref[rows[b], r * HEAD_DIM:(r + 1) * HEAD_DIM] = outs[b][r * CHUNK:(r + 1) * CHUNK]

    kv = pl.BlockSpec((S + 2 * CHUNK, HEAD_DIM), lambda g, i: (0, g))
    return _call("attn_fwd", body, grid=(N_KV_HEADS, S // tq),
                 in_specs=[pl.BlockSpec((tq, gw), lambda g, i: (i, g)), kv, kv,
                           pl.BlockSpec((None, Q_PER_KV, LANES), lambda g, i: (g, 0, 0))],
                 out_specs=pl.BlockSpec((tq, gw), lambda g, i: (i, g)),
                 out_shape=jax.ShapeDtypeStruct((S, D_MODEL), BF16), sem=("parallel", "arbitrary"))(qr, kp, vp, sink3)


def attn_bwd(qr, kp, vp, sink3, dmix):
    S = qr.shape[0]
    tq = min(1024, S)
    gw = Q_PER_KV * HEAD_DIM
    scale = 1.0 / math.sqrt(HEAD_DIM)
    nblk = tq // CHUNK

    def body(q_ref, k_ref, v_ref, s_ref, do_ref, dq_ref, dk_ref, dv_ref, ds_ref):
        i = pl.program_id(1)

        @pl.when(i == 0)
        def _():
            dk_ref[...] = jnp.zeros_like(dk_ref)
            dv_ref[...] = jnp.zeros_like(dv_ref)
            ds_ref[...] = jnp.zeros_like(ds_ref)

        blocks = range(nblk)
        ns, wins, rows = _block_views(i, nblk)
        sk = _stack_sinks(s_ref)
        band = _band_bias()
        qs = [_stack_heads(q_ref, rows[b]) for b in blocks]
        dos = [_stack_heads(do_ref, rows[b]) for b in blocks]
        scores = [_scores(qs[b], k_ref[wins[b], :]) for b in blocks]
        dps = [lax.dot_general(dos[b], v_ref[wins[b], :], NT, preferred_element_type=F32) for b in blocks]
        probs = [_softmax_sink(scores[b], sk, band + _edge_bias(ns[b], S)) for b in blocks]
        deltas = [jnp.sum(probs[b][0] * dps[b], axis=1, keepdims=True) for b in blocks]
        dscs = [(probs[b][0] * (dps[b] - deltas[b]) * scale).astype(BF16) for b in blocks]
        dqs = [jnp.dot(dscs[b], k_ref[wins[b], :], preferred_element_type=F32) for b in blocks]
        dks = [lax.dot_general(dscs[b], qs[b], TN, preferred_element_type=F32) for b in blocks]
        dvs = [lax.dot_general(probs[b][0].astype(BF16), dos[b], TN, preferred_element_type=F32) for b in blocks]
        for b in blocks:
            for r in range(Q_PER_KV):
                dq_ref[rows[b], r * HEAD_DIM:(r + 1) * HEAD_DIM] = dqs[b][r * CHUNK:(r + 1) * CHUNK]
        for m in range(nblk + 2):
            parts = [(b, m - b) for b in blocks if 0 <= m - b <= 2]
            krows = pl.ds(pl.multiple_of(i * tq + m * CHUNK, CHUNK), CHUNK)
            dk_ref[krows, :] += sum(dks[b][o * CHUNK:(o + 1) * CHUNK] for b, o in parts)
            dv_ref[krows, :] += sum(dvs[b][o * CHUNK:(o + 1) * CHUNK] for b, o in parts)
        for r in range(Q_PER_KV):
            head = slice(r * CHUNK, (r + 1) * CHUNK)
            dsink = sum(jnp.sum(-probs[b][1][head] * deltas[b][head], axis=0, keepdims=True) for b in blocks)
            ds_ref[r:r + 1, :] += jnp.broadcast_to(dsink, (1, LANES))

    kv = pl.BlockSpec((S + 2 * CHUNK, HEAD_DIM), lambda g, i: (0, g))
    qspec = pl.BlockSpec((tq, gw), lambda g, i: (i, g))
    sspec = pl.BlockSpec((None, Q_PER_KV, LANES), lambda g, i: (g, 0, 0))
    padshape = jax.ShapeDtypeStruct((S + 2 * CHUNK, KV_WIDTH), F32)
    return _call("attn_bwd", body, grid=(N_KV_HEADS, S // tq),
                 in_specs=[qspec, kv, kv, sspec, qspec],
                 out_specs=[qspec, kv, kv, sspec],
                 out_shape=[jax.ShapeDtypeStruct((S, ATTN_WIDTH), F32), padshape, padshape,
                            jax.ShapeDtypeStruct((N_KV_HEADS, Q_PER_KV, LANES), F32)],
                 sem=("parallel", "arbitrary"))(qr, kp, vp, sink3, dmix)


CONV_TILE = 256


def _fill_padded(dst_ref, value, S):
    zero = jnp.zeros((CONV_PAD, LANES), F32)
    dst_ref[0:CONV_PAD, :] = zero
    dst_ref[CONV_PAD + S:2 * CONV_PAD + S, :] = zero
    dst_ref[CONV_PAD:CONV_PAD + S, :] = value


def conv_dw_fwd(z, w32, b):
    S = z.shape[0]
    T = min(CONV_TILE, S)
    lo = CONV_PAD - (CONV_KERNEL - 1) // 2

    def body(a_ref, g_ref, w_ref, b_ref, o_ref, c0_ref):
        _fill_padded(c0_ref, a_ref[...].astype(F32) * _sigmoid(g_ref[...].astype(F32)), S)

        def tile(t, carry):
            base = pl.multiple_of(t * T, T)
            acc = jnp.broadcast_to(b_ref[...], (T, LANES))
            for j in range(CONV_KERNEL):
                acc = acc + w_ref[j:j + 1, :] * c0_ref[pl.ds(base + lo + j, T), :]
            o_ref[pl.ds(base, T), :] = acc
            return carry

        lax.fori_loop(0, S // T, tile, 0)

    nca, ncg = OFF_CA // LANES, OFF_CG // LANES
    return _call("conv_dw_fwd", body, grid=(CONV_WIDTH // LANES,),
                 in_specs=[pl.BlockSpec((S, LANES), lambda cb: (0, nca + cb)), pl.BlockSpec((S, LANES), lambda cb: (0, ncg + cb)),
                           pl.BlockSpec((32, LANES), lambda cb: (0, cb)), pl.BlockSpec((1, LANES), lambda cb: (0, cb))],
                 out_specs=pl.BlockSpec((S, LANES), lambda cb: (0, cb)),
                 out_shape=jax.ShapeDtypeStruct((S, CONV_WIDTH), F32),
                 scratch=[pltpu.VMEM((S + 2 * CONV_PAD, LANES), F32)], sem=("parallel",))(z, z, w32, b)


def _ln_stats(x):
    mu = jnp.mean(x, axis=-1, keepdims=True)
    xc = x - mu
    rs = lax.rsqrt(jnp.mean(xc * xc, axis=-1, keepdims=True) + EPS)
    return xc * rs, rs


def _ln_bwd(dy, xh, rs, g):
    dxh = dy * g
    return rs * (dxh - jnp.mean(dxh, axis=-1, keepdims=True) - xh * jnp.mean(dxh * xh, axis=-1, keepdims=True))


def conv_ln_fwd(c1, g, b, mix):
    S = c1.shape[0]
    T = min(512, S)

    def body(x_ref, g_ref, b_ref, mix_ref, o_ref):
        xh, _ = _ln_stats(x_ref[...])
        y = xh * g_ref[...] + b_ref[...]
        o_ref[...] = (y * _sigmoid(y)).astype(BF16)

    row = pl.BlockSpec((T, CONV_WIDTH), lambda i: (i, 0))
    vec = pl.BlockSpec((1, CONV_WIDTH), lambda i: (0, 0))
    return _call("conv_ln_fwd", body, grid=(S // T,), in_specs=[row, vec, vec, pl.BlockSpec(memory_space=pl.ANY)],
                 out_specs=pl.BlockSpec((T, CONV_WIDTH), lambda i: (i, ATTN_WIDTH // CONV_WIDTH)),
                 out_shape=jax.ShapeDtypeStruct(mix.shape, BF16), sem=("parallel",), aliases={3: 0})(c1, g, b, mix)


def _acc_out(ref, value, step=None):
    step = pl.program_id(0) if step is None else step

    @pl.when(step == 0)
    def _():
        ref[...] = value

    @pl.when(step > 0)
    def _():
        ref[...] += value


def conv_ln_bwd(dmix, c1, g, b):
    S = c1.shape[0]
    T = min(512, S)

    def body(d_ref, x_ref, g_ref, b_ref, dx_ref, dg_ref, db_ref):
        xh, rs = _ln_stats(x_ref[...])
        gv = g_ref[...]
        y = xh * gv + b_ref[...]
        sg = _sigmoid(y)
        dy = d_ref[...].astype(F32) * sg * (1.0 + y * (1.0 - sg))
        dx_ref[...] = _ln_bwd(dy, xh, rs, gv)
        _acc_out(dg_ref, jnp.sum(dy * xh, axis=0, keepdims=True))
        _acc_out(db_ref, jnp.sum(dy, axis=0, keepdims=True))

    row = pl.BlockSpec((T, CONV_WIDTH), lambda i: (i, 0))
    vec = pl.BlockSpec((1, CONV_WIDTH), lambda i: (0, 0))
    vshape = jax.ShapeDtypeStruct((1, CONV_WIDTH), F32)
    return _call("conv_ln_bwd", body, grid=(S // T,),
                 in_specs=[pl.BlockSpec((T, CONV_WIDTH), lambda i: (i, ATTN_WIDTH // CONV_WIDTH)), row, vec, vec],
                 out_specs=[row, vec, vec], out_shape=[jax.ShapeDtypeStruct((S, CONV_WIDTH), F32), vshape, vshape],
                 sem=("arbitrary",))(dmix, c1, g, b)


def conv_dw_bwd(dc1, z, w32, dz):
    S = z.shape[0]
    T = min(CONV_TILE, S)
    half = (CONV_KERNEL - 1) // 2
    lo = CONV_PAD - half
    n_cb = CONV_WIDTH // LANES

    def body(d_ref, a_ref, g_ref, w_ref, dz_ref, o_ref, dw_ref, db_ref, c0_ref, d1_ref, wacc_ref, dg_ref):
        @pl.when(pl.program_id(1) == 0)
        def _():
            av = a_ref[...].astype(F32)
            sg = _sigmoid(g_ref[...].astype(F32))
            _fill_padded(c0_ref, av * sg, S)
            _fill_padded(d1_ref, d_ref[...], S)
            wacc_ref[...] = jnp.zeros_like(wacc_ref)

            def tile(t, carry):
                base = pl.multiple_of(t * T, T)
                d1 = d_ref[pl.ds(base, T), :]
                acc = jnp.zeros((T, LANES), F32)
                for j in range(CONV_KERNEL):
                    acc = acc + w_ref[j:j + 1, :] * d1_ref[pl.ds(base + CONV_PAD + half - j, T), :]
                    prod = d1 * c0_ref[pl.ds(base + lo + j, T), :]
                    wacc_ref[j] += jnp.sum(prod.reshape(T // 8, 8, LANES), axis=0)
                rows = pl.ds(base, T)
                a_t = a_ref[rows, :].astype(F32)
                s_t = _sigmoid(g_ref[rows, :].astype(F32))
                o_ref[rows, :] = (acc * s_t).astype(BF16)
                dg_ref[rows, :] = (acc * a_t * s_t * (1.0 - s_t)).astype(BF16)
                return carry

            lax.fori_loop(0, S // T, tile, 0)
            dw_ref[...] = jnp.sum(wacc_ref[...], axis=1)
            db_ref[...] = jnp.sum(d_ref[...], axis=0, keepdims=True)

        @pl.when(pl.program_id(1) == 1)
        def _():
            o_ref[...] = dg_ref[...]

    nca, ncg = OFF_CA // LANES, OFF_CG // LANES
    return _call("conv_dw_bwd", body, grid=(n_cb, 2),
                 in_specs=[pl.BlockSpec((S, LANES), lambda cb, j: (0, cb)), pl.BlockSpec((S, LANES), lambda cb, j: (0, nca + cb)),
                           pl.BlockSpec((S, LANES), lambda cb, j: (0, ncg + cb)), pl.BlockSpec((32, LANES), lambda cb, j: (0, cb)),
                           pl.BlockSpec(memory_space=pl.ANY)],
                 out_specs=[pl.BlockSpec((S, LANES), lambda cb, j: (0, nca + cb + n_cb * j)),
                            pl.BlockSpec((32, LANES), lambda cb, j: (0, cb)), pl.BlockSpec((1, LANES), lambda cb, j: (0, cb))],
                 out_shape=[jax.ShapeDtypeStruct(dz.shape, BF16), jax.ShapeDtypeStruct((32, CONV_WIDTH), F32),
                            jax.ShapeDtypeStruct((1, CONV_WIDTH), F32)],
                 scratch=[pltpu.VMEM((S + 2 * CONV_PAD, LANES), F32), pltpu.VMEM((S + 2 * CONV_PAD, LANES), F32),
                          pltpu.VMEM((32, 8, LANES), F32), pltpu.VMEM((S, LANES), BF16)],
                 sem=("parallel", "arbitrary"), aliases={4: 0})(dc1, z, z, w32, dz)


_INV_SQRT2 = 1.0 / math.sqrt(2.0)
_INV_SQRT2PI = 1.0 / math.sqrt(2.0 * math.pi)


def _gelu(x):
    return 0.5 * x * (1.0 + lax.erf(x * _INV_SQRT2))


def _gelu_grad(x):
    return 0.5 * (1.0 + lax.erf(x * _INV_SQRT2)) + x * jnp.exp(-0.5 * x * x) * _INV_SQRT2PI


def sgu_fwd(z, g, b, ws, bs, mix):
    S = z.shape[0]
    T = min(512, S)

    def body(u_ref, v_ref, g_ref, b_ref, ws_ref, bs_ref, mix_ref, o_ref):
        xh, _ = _ln_stats(_gelu(v_ref[...].astype(F32)))
        vn = (xh * g_ref[...] + b_ref[...]).astype(BF16)
        for ch in range(T // CHUNK):
            rows = slice(ch * CHUNK, (ch + 1) * CHUNK)
            for h in range(SGU_HEADS):
                cols = slice(h * HEAD_DIM, (h + 1) * HEAD_DIM)
                sp = jnp.dot(ws_ref[h], vn[rows, cols], preferred_element_type=F32) + bs_ref[h]
                o_ref[rows, cols] = (_gelu(u_ref[rows, cols].astype(F32)) * sp).astype(BF16)

    vec = pl.BlockSpec((1, SGU_WIDTH), lambda i: (0, 0))
    full = pl.BlockSpec((SGU_HEADS, CHUNK, CHUNK), lambda i: (0, 0, 0))
    return _call("sgu_fwd", body, grid=(S // T,),
                 in_specs=[pl.BlockSpec((T, SGU_WIDTH), lambda i: (i, OFF_U // SGU_WIDTH)),
                           pl.BlockSpec((T, SGU_WIDTH), lambda i: (i, OFF_VV // SGU_WIDTH)), vec, vec, full, full,
                           pl.BlockSpec(memory_space=pl.ANY)],
                 out_specs=pl.BlockSpec((T, SGU_WIDTH), lambda i: (i, (ATTN_WIDTH + CONV_WIDTH) // SGU_WIDTH)),
                 out_shape=jax.ShapeDtypeStruct(mix.shape, BF16), sem=("parallel",), aliases={6: 0})(z, z, g, b, ws, bs, mix)


def sgu_bwd(z, dmix, g, b, ws, bs, dz):
    S = z.shape[0]
    T = min(512, S)

    def body(u_ref, v_ref, d_ref, g_ref, b_ref, ws_ref, bs_ref, dz_ref, o_ref, dws_ref, dbs_ref, dg_ref, db_ref, dvn_ref, dv_ref):
        tile = pl.program_id(0)
        first = pl.program_id(1) == 0

        @pl.when(first & (tile == 0))
        def _():
            dws_ref[...] = jnp.zeros_like(dws_ref)
            dbs_ref[...] = jnp.zeros_like(dbs_ref)

        @pl.when(first)
        def _():
            vraw = v_ref[...].astype(F32)
            xh, rs = _ln_stats(_gelu(vraw))
            gv = g_ref[...]
            vn = (xh * gv + b_ref[...]).astype(BF16)
            for ch in range(T // CHUNK):
                rows = slice(ch * CHUNK, (ch + 1) * CHUNK)
                for h in range(SGU_HEADS):
                    cols = slice(h * HEAD_DIM, (h + 1) * HEAD_DIM)
                    w = ws_ref[h]
                    vb = vn[rows, cols]
                    sp = jnp.dot(w, vb, preferred_element_type=F32) + bs_ref[h]
                    uraw = u_ref[rows, cols].astype(F32)
                    dout = d_ref[rows, cols].astype(F32)
                    o_ref[rows, cols] = (dout * sp * _gelu_grad(uraw)).astype(BF16)
                    dsp = dout * _gelu(uraw)
                    dspb = dsp.astype(BF16)
                    dvn_ref[rows, cols] = lax.dot_general(w, dspb, TN, preferred_element_type=F32)
                    dws_ref[h] += lax.dot_general(dspb, vb, NT, preferred_element_type=F32)
                    dbs_ref[h] += jnp.sum(dsp, axis=1, keepdims=True)
            dvn = dvn_ref[...]
            dv_ref[...] = (_ln_bwd(dvn, xh, rs, gv) * _gelu_grad(vraw)).astype(BF16)
            _acc_out(dg_ref, jnp.sum(dvn * xh, axis=0, keepdims=True), tile)
            _acc_out(db_ref, jnp.sum(dvn, axis=0, keepdims=True), tile)

        @pl.when(pl.program_id(1) == 1)
        def _():
            o_ref[...] = dv_ref[...]

    vec = pl.BlockSpec((1, SGU_WIDTH), lambda i, j: (0, 0))
    full = pl.BlockSpec((SGU_HEADS, CHUNK, CHUNK), lambda i, j: (0, 0, 0))
    vshape = jax.ShapeDtypeStruct((1, SGU_WIDTH), F32)
    return _call("sgu_bwd", body, grid=(S // T, 2),
                 in_specs=[pl.BlockSpec((T, SGU_WIDTH), lambda i, j: (i, OFF_U // SGU_WIDTH)),
                           pl.BlockSpec((T, SGU_WIDTH), lambda i, j: (i, OFF_VV // SGU_WIDTH)),
                           pl.BlockSpec((T, SGU_WIDTH), lambda i, j: (i, (ATTN_WIDTH + CONV_WIDTH) // SGU_WIDTH)), vec, vec, full, full,
                           pl.BlockSpec(memory_space=pl.ANY)],
                 out_specs=[pl.BlockSpec((T, SGU_WIDTH), lambda i, j: (i, OFF_U // SGU_WIDTH + j)), full,
                            pl.BlockSpec((SGU_HEADS, CHUNK, 1), lambda i, j: (0, 0, 0)), vec, vec],
                 out_shape=[jax.ShapeDtypeStruct(dz.shape, BF16), jax.ShapeDtypeStruct((SGU_HEADS, CHUNK, CHUNK), F32),
                            jax.ShapeDtypeStruct((SGU_HEADS, CHUNK, 1), F32), vshape, vshape],
                 scratch=[pltpu.VMEM((T, SGU_WIDTH), F32), pltpu.VMEM((T, SGU_WIDTH), BF16)],
                 sem=("arbitrary", "arbitrary"), aliases={7: 0})(z, z, dmix, g, b, ws, bs, dz)


def _row_tile(rows, cols, n_arrays, budget_mib=24):
    budget = (budget_mib * 1024 * 1024) // (n_arrays * 2 * 4 * cols)
    t = min(rows, max(16, budget // 16 * 16))
    while rows % t:
        t -= 16
    return t


def cast_layer(w, layer, token):
    _, R, C = w.shape
    tr = _row_tile(R, C, 2)

    def body(w_ref, token_ref, o_ref):
        o_ref[...] = w_ref[...].astype(BF16)

    return _call("cast_layer", body, grid=(R // tr,),
                 in_specs=[pl.BlockSpec((None, tr, C), lambda i: (layer, i, 0)), pl.BlockSpec((8, LANES), lambda i: (0, 0))],
                 out_specs=pl.BlockSpec((tr, C), lambda i: (i, 0)), out_shape=jax.ShapeDtypeStruct((R, C), BF16),
                 sem=("parallel",))(w, token)


def add_sibling_half(grad, recv, c_idx):
    J, R, C = grad.shape
    hr = R // 2
    tr = _row_tile(hr, C, 3)
    nb = hr // tr

    def body(c_ref, g_ref, r_ref, o_ref):
        o_ref[...] = (g_ref[...].astype(F32) + r_ref[...].astype(F32)).astype(BF16)

    grid_spec = pltpu.PrefetchScalarGridSpec(
        num_scalar_prefetch=1, grid=(J, nb),
        in_specs=[pl.BlockSpec((None, tr, C), lambda j, i, c: (j, c[0] * nb + i, 0)),
                  pl.BlockSpec((None, tr, C), lambda j, i, c: (j, i, 0))],
        out_specs=pl.BlockSpec((None, tr, C), lambda j, i, c: (j, i, 0)))
    return pl.pallas_call(body, name="add_sibling_half", grid_spec=grid_spec,
                          out_shape=jax.ShapeDtypeStruct((J, hr, C), BF16),
                          compiler_params=pltpu.CompilerParams(vmem_limit_bytes=VMEM_LIMIT,
                                                               dimension_semantics=("parallel", "parallel")))(c_idx, grad, recv)


def sum_chips(own, others, stack, x_idx, y_idx, layer):
    R, C = own.shape[1:]
    tr = _row_tile(R, C, 4)

    def body(x_ref, y_ref, own_ref, oth_ref, stack_ref, o_ref):
        acc = own_ref[...].astype(F32)
        for j in range(3):
            acc = acc + oth_ref[j].astype(F32)
        o_ref[...] = acc

    grid_spec = pltpu.PrefetchScalarGridSpec(
        num_scalar_prefetch=2, grid=(R // tr,),
        in_specs=[pl.BlockSpec((None, tr, C), lambda i, xr, yr: (2 * xr[0] + yr[0], i, 0)),
                  pl.BlockSpec((3, tr, C), lambda i, xr, yr: (0, i, 0)),
                  pl.BlockSpec(memory_space=pl.ANY)],
        out_specs=pl.BlockSpec((None, tr, C), lambda i, xr, yr: (layer, i, 0)))
    return pl.pallas_call(body, name="sum_chips", grid_spec=grid_spec,
                          out_shape=jax.ShapeDtypeStruct(stack.shape, F32), input_output_aliases={4: 0},
                          compiler_params=pltpu.CompilerParams(vmem_limit_bytes=VMEM_LIMIT,
                                                               dimension_semantics=("parallel",)))(x_idx, y_idx, own, others, stack)


def adamw_halves(w, mine, theirs, m, v, c_idx):
    L, R, C = w.shape
    hr = R // 2
    tr = _row_tile(hr, C, 9, budget_mib=40)
    nb = hr // tr

    def body(c_ref, w_ref, a_ref, b_ref, m_ref, v_ref, g_ref, d_ref, nm_ref, nv_ref):
        gv = jnp.where(pl.program_id(1) == c_ref[0], a_ref[...], b_ref[...])
        g_ref[...] = gv
        nm = ADAM_B1 * m_ref[...] + (1.0 - ADAM_B1) * gv
        nv = ADAM_B2 * v_ref[...] + (1.0 - ADAM_B2) * (gv * gv)
        m_hat = nm / (1.0 - ADAM_B1 ** ADAM_STEP)
        v_hat = nv / (1.0 - ADAM_B2 ** ADAM_STEP)
        d_ref[...] = -ADAM_LR * (m_hat / (jnp.sqrt(v_hat) + ADAM_EPS) + ADAM_WD * w_ref[...])
        nm_ref[...] = nm
        nv_ref[...] = nv

    full = pl.BlockSpec((None, tr, C), lambda l, h, i, c: (l, h * nb + i, 0))
    a_spec = pl.BlockSpec((None, tr, C), lambda l, h, i, c: (l, jnp.where(h == c[0], i, 0), 0))
    b_spec = pl.BlockSpec((None, tr, C), lambda l, h, i, c: (l, jnp.where(h == c[0], 0, i), 0))
    grid_spec = pltpu.PrefetchScalarGridSpec(num_scalar_prefetch=1, grid=(L, 2, nb),
                                             in_specs=[full, a_spec, b_spec, full, full], out_specs=[full] * 4)
    shape = jax.ShapeDtypeStruct((L, R, C), F32)
    return pl.pallas_call(body, name="adamw_halves", grid_spec=grid_spec, out_shape=[shape] * 4,
                          compiler_params=pltpu.CompilerParams(vmem_limit_bytes=VMEM_LIMIT,
                                                               dimension_semantics=("parallel", "arbitrary", "arbitrary")))(
        c_idx, w, mine, theirs, m, v)


def adamw(w, g, m, v):
    R, C = w.shape
    tr = _row_tile(R, C, 7)

    def body(w_ref, g_ref, m_ref, v_ref, d_ref, nm_ref, nv_ref):
        gv = g_ref[...]
        nm = ADAM_B1 * m_ref[...] + (1.0 - ADAM_B1) * gv
        nv = ADAM_B2 * v_ref[...] + (1.0 - ADAM_B2) * (gv * gv)
        m_hat = nm / (1.0 - ADAM_B1 ** ADAM_STEP)
        v_hat = nv / (1.0 - ADAM_B2 ** ADAM_STEP)
        d_ref[...] = -ADAM_LR * (m_hat / (jnp.sqrt(v_hat) + ADAM_EPS) + ADAM_WD * w_ref[...])
        nm_ref[...] = nm
        nv_ref[...] = nv

    spec = pl.BlockSpec((tr, C), lambda i: (i, 0))
    shape = jax.ShapeDtypeStruct((R, C), F32)
    return _call("adamw", body, grid=(R // tr,), in_specs=[spec] * 4, out_specs=[spec] * 3, out_shape=[shape] * 3,
                 sem=("parallel",))(w, g, m, v)


def _place():
    x, y, c = lax.axis_index("x"), lax.axis_index("y"), lax.axis_index("c")
    chips = [(1 - x, y), (x, 1 - y), (1 - x, 1 - y)]
    return x, y, c, chips


def _remote(src, dst, send_sem, recv_sem, dev):
    return pltpu.make_async_remote_copy(src_ref=src, dst_ref=dst, send_sem=send_sem, recv_sem=recv_sem,
                                        device_id=dev, device_id_type=MESH)


EFFECT = pltpu.SideEffectType.DATAFLOW_SIDE_EFFECTING
SEM = pl.BlockSpec(memory_space=pltpu.SEMAPHORE)
ANY = pl.BlockSpec(memory_space=pl.ANY)
TOKEN = jax.ShapeDtypeStruct((8, LANES), F32)


def _in_hbm(a):
    return pltpu.with_memory_space_constraint(a, pltpu.HBM)


def _gather_copies(shards, lands, send_sems, recv_sems):
    x, y, c, chips = _place()
    me = 2 * x + y
    copies = []
    for k in range(len(shards)):
        hr = shards[k].shape[0] // 2
        mine = pl.ds(pl.multiple_of(c * hr, 8), hr)
        for t, (px, py) in enumerate(chips):
            copies.append(_remote(shards[k].at[mine, :], lands[k].at[me, mine, :], send_sems.at[4 * k + t], recv_sems.at[4 * k + t],
                                  (px, py, c)))
        copies.append(_remote(shards[k], lands[k].at[me], send_sems.at[4 * k + 3], recv_sems.at[4 * k + 3], (x, y, 1 - c)))
    return copies


def _gather_landings(lands, send_sems, recv_sems):
    x, y, c, chips = _place()
    me = 2 * x + y
    landings = []
    for k in range(len(lands)):
        hr = lands[k].shape[1] // 2
        mine = pl.ds(pl.multiple_of(c * hr, 8), hr)
        for t, (px, py) in enumerate(chips):
            dst = lands[k].at[2 * px + py, mine, :]
            landings.append(_remote(dst, dst, send_sems.at[4 * k + t], recv_sems.at[4 * k + t], (px, py, c)))
        dst = lands[k].at[me]
        landings.append(_remote(dst, dst, send_sems.at[4 * k + 3], recv_sems.at[4 * k + 3], (x, y, 1 - c)))
    return landings


def gather_start(shards, after):
    n = len(shards)

    def body(*refs):
        srcs, lands_in = refs[:n], refs[n:2 * n]
        send_sems, recv_sems = refs[2 * n + 1], refs[2 * n + 2]
        token = refs[-1]
        for cp in _gather_copies(srcs, lands_in, send_sems, recv_sems):
            cp.start()
        token[...] = jnp.zeros_like(token)

    lands = [lax.empty((N_CHIPS,) + s.shape, s.dtype) for s in shards]
    outs = pl.pallas_call(
        body, name="gather_start", in_specs=[HBM] * (2 * n) + [ANY],
        out_specs=[SEM, SEM] + [HBM] * (2 * n) + [VMEM_SPEC],
        out_shape=[pltpu.SemaphoreType.DMA((4 * n,)), pltpu.SemaphoreType.DMA((4 * n,))]
        + [pltpu.HBM(s.shape, s.dtype) for s in shards] + [pltpu.HBM(l.shape, l.dtype) for l in lands] + [TOKEN],
        input_output_aliases={i: 2 + i for i in range(2 * n)},
        compiler_params=pltpu.CompilerParams(has_side_effects=EFFECT),
    )(*[_in_hbm(s) for s in shards], *[_in_hbm(l) for l in lands], after)
    return outs[0], outs[1], outs[2:2 + n], outs[2 + n:2 + 2 * n], outs[-1]


def gather_wait(send_sems, recv_sems, shards, lands, after):
    n = len(shards)

    def body(*refs):
        srcs, lands_in = refs[:n], refs[n:2 * n]
        send, recv = refs[2 * n], refs[2 * n + 1]
        for cp in _gather_copies(srcs, lands_in, send, recv):
            cp.wait_send()
        for cp in _gather_landings(lands_in, send, recv):
            cp.wait_recv()

    outs = pl.pallas_call(
        body, name="gather_wait", in_specs=[HBM] * (2 * n) + [SEM, SEM, ANY], out_specs=[HBM] * (2 * n),
        out_shape=[pltpu.HBM(s.shape, s.dtype) for s in shards] + [pltpu.HBM(l.shape, l.dtype) for l in lands],
        input_output_aliases={i: i for i in range(2 * n)},
        compiler_params=pltpu.CompilerParams(has_side_effects=EFFECT),
    )(*shards, *lands, send_sems, recv_sems, after)
    return outs[n:]


def _forward_copies(lands, send_sems, recv_sems, received):
    x, y, c, chips = _place()
    copies = []
    for k in range(len(lands)):
        hr = lands[k].shape[1] // 2
        half = (1 - c) if received else c
        rows = pl.ds(pl.multiple_of(half * hr, 8), hr)
        for t, (px, py) in enumerate(chips):
            block = lands[k].at[2 * px + py, rows, :]
            copies.append(_remote(block, block, send_sems.at[3 * k + t], recv_sems.at[3 * k + t], (x, y, 1 - c)))
    return copies


def forward_start(lands):
    n = len(lands)

    def body(*refs):
        for cp in _forward_copies(refs[:n], refs[n], refs[n + 1], received=False):
            cp.start()
        refs[-1][...] = jnp.zeros_like(refs[-1])

    outs = pl.pallas_call(
        body, name="forward_start", in_specs=[HBM] * n, out_specs=[SEM, SEM] + [HBM] * n + [VMEM_SPEC],
        out_shape=[pltpu.SemaphoreType.DMA((3 * n,)), pltpu.SemaphoreType.DMA((3 * n,))]
        + [pltpu.HBM(l.shape, l.dtype) for l in lands] + [TOKEN],
        input_output_aliases={i: 2 + i for i in range(n)},
        compiler_params=pltpu.CompilerParams(has_side_effects=EFFECT),
    )(*[_in_hbm(l) for l in lands])
    return outs[0], outs[1], outs[2:2 + n], outs[-1]


def forward_wait(send_sems, recv_sems, lands, after):
    n = len(lands)

    def body(*refs):
        for cp in _forward_copies(refs[:n], refs[n], refs[n + 1], received=False):
            cp.wait_send()
        for cp in _forward_copies(refs[:n], refs[n], refs[n + 1], received=True):
            cp.wait_recv()

    return pl.pallas_call(
        body, name="forward_wait", in_specs=[HBM] * n + [SEM, SEM, ANY], out_specs=[HBM] * n,
        out_shape=[pltpu.HBM(l.shape, l.dtype) for l in lands],
        input_output_aliases={i: i for i in range(n)},
        compiler_params=pltpu.CompilerParams(has_side_effects=EFFECT),
    )(*lands, send_sems, recv_sems, after)


def forward_halves(lands):
    n = len(lands)

    def body(*refs):
        ins, outs = refs[:n], refs[n:2 * n]
        send_sems, recv_sems = refs[2 * n:]
        x, y, c, chips = _place()
        sibling = (x, y, 1 - c)
        sends = []
        for k in range(n):
            hr = ins[k].shape[1] // 2
            mine = pl.ds(pl.multiple_of(c * hr, 8), hr)
            for t, (px, py) in enumerate(chips):
                cp = _remote(ins[k].at[2 * px + py, mine, :], outs[k].at[2 * px + py, mine, :],
                             send_sems.at[k, t], recv_sems.at[k, t], sibling)
                cp.start()
                sends.append(cp)
        for k in range(n):
            hr = ins[k].shape[1] // 2
            other = pl.ds(pl.multiple_of((1 - c) * hr, 8), hr)
            for t, (px, py) in enumerate(chips):
                dst = outs[k].at[2 * px + py, other, :]
                _remote(dst, dst, send_sems.at[k, t], recv_sems.at[k, t], sibling).wait_recv()
        for cp in sends:
            cp.wait_send()

    return pl.pallas_call(
        body, name="forward_halves", in_specs=[HBM] * n, out_specs=[HBM] * n,
        out_shape=[jax.ShapeDtypeStruct(l.shape, l.dtype) for l in lands],
        input_output_aliases={i: i for i in range(n)},
        scratch_shapes=[pltpu.SemaphoreType.DMA((n, 3)), pltpu.SemaphoreType.DMA((n, 3))],
    )(*lands)


def gather_small(block):
    def body(in_ref, out_ref, send_sems, recv_sems):
        x, y, c, chips = _place()
        me = 2 * x + y
        out_ref[me] = in_ref[...]
        sends = []
        for t, (px, py) in enumerate(chips):
            cp = _remote(in_ref, out_ref.at[me], send_sems.at[t], recv_sems.at[t], (px, py, c))
            cp.start()
            sends.append(cp)
        for t, (px, py) in enumerate(chips):
            landed = out_ref.at[2 * px + py]
            _remote(landed, landed, send_sems.at[t], recv_sems.at[t], (px, py, c)).wait_recv()
        for cp in sends:
            cp.wait_send()

    return pl.pallas_call(
        body, name="gather_small", in_specs=[VMEM_SPEC], out_specs=VMEM_SPEC,
        out_shape=jax.ShapeDtypeStruct((N_CHIPS,) + block.shape, block.dtype),
        scratch_shapes=[pltpu.SemaphoreType.DMA((3,)), pltpu.SemaphoreType.DMA((3,))],
    )(block)


def exchange_sibling_halves(grads):
    n = len(grads)

    def body(*refs):
        ins, outs = refs[:n], refs[n:2 * n]
        send_sems, recv_sems = refs[2 * n:]
        x, y, c, _ = _place()
        copies = []
        for k in range(n):
            hr = ins[k].shape[1] // 2
            theirs = pl.ds(pl.multiple_of((1 - c) * hr, 8), hr)
            cp = _remote(ins[k].at[:, theirs, :], outs[k], send_sems.at[k], recv_sems.at[k], (x, y, 1 - c))
            cp.start()
            copies.append(cp)
        for cp in copies:
            cp.wait()

    return pl.pallas_call(
        body, name="exchange_sibling_halves", in_specs=[HBM] * n, out_specs=[HBM] * n,
        out_shape=[jax.ShapeDtypeStruct((g.shape[0], g.shape[1] // 2, g.shape[2]), g.dtype) for g in grads],
        scratch_shapes=[pltpu.SemaphoreType.DMA((n,)), pltpu.SemaphoreType.DMA((n,))],
    )(*grads)


def _sibling_half_copies(grads, lands, send_sems, recv_sems):
    x, y, c, _ = _place()
    copies = []
    for k in range(len(grads)):
        hr = grads[k].shape[1] // 2
        theirs = pl.ds(pl.multiple_of((1 - c) * hr, 8), hr)
        copies.append(_remote(grads[k].at[:, theirs, :], lands[k], send_sems.at[k], recv_sems.at[k], (x, y, 1 - c)))
    return copies


def _sibling_whole_copies(srcs, lands, send_sems, recv_sems):
    x, y, c, _ = _place()
    return [_remote(srcs[k], lands[k], send_sems.at[k], recv_sems.at[k], (x, y, 1 - c)) for k in range(len(srcs))]


def pair_start(name, make_copies, srcs, land_shapes):
    n = len(srcs)

    def body(*refs):
        src_refs, land_refs = refs[:n], refs[n:2 * n]
        send_sems, recv_sems = refs[2 * n], refs[2 * n + 1]
        token = refs[-1]
        for cp in make_copies(src_refs, land_refs, send_sems, recv_sems):
            cp.start()
        token[...] = jnp.zeros_like(token)

    lands = [lax.empty(shape, s.dtype) for shape, s in zip(land_shapes, srcs)]
    outs = pl.pallas_call(
        body, name=name, in_specs=[HBM] * (2 * n), out_specs=[SEM, SEM] + [HBM] * (2 * n) + [VMEM_SPEC],
        out_shape=[pltpu.SemaphoreType.DMA((n,)), pltpu.SemaphoreType.DMA((n,))]
        + [pltpu.HBM(s.shape, s.dtype) for s in srcs] + [pltpu.HBM(l.shape, l.dtype) for l in lands] + [TOKEN],
        input_output_aliases={i: 2 + i for i in range(2 * n)},
        compiler_params=pltpu.CompilerParams(has_side_effects=EFFECT),
    )(*[_in_hbm(s) for s in srcs], *[_in_hbm(l) for l in lands])
    return outs[0], outs[1], outs[2:2 + n], outs[2 + n:2 + 2 * n], outs[-1]


def pair_wait_one(name, send_sems, recv_sems, src, land, after, index):
    def body(src_ref, land_ref, send, recv, after_ref, src_out, land_out):
        x, y, c, _ = _place()
        cp = _remote(src_ref, land_ref, send.at[index], recv.at[index], (x, y, 1 - c))
        cp.wait_send()
        cp.wait_recv()

    return pl.pallas_call(
        body, name=name, in_specs=[HBM, HBM, SEM, SEM, ANY], out_specs=[HBM, HBM],
        out_shape=[pltpu.HBM(src.shape, src.dtype), pltpu.HBM(land.shape, land.dtype)],
        input_output_aliases={0: 0, 1: 1},
        compiler_params=pltpu.CompilerParams(has_side_effects=EFFECT),
    )(src, land, send_sems, recv_sems, after)


def pair_wait(name, make_copies, send_sems, recv_sems, srcs, lands, after):
    n = len(srcs)

    def body(*refs):
        src_refs, land_refs = refs[:n], refs[n:2 * n]
        for cp in make_copies(src_refs, land_refs, refs[2 * n], refs[2 * n + 1]):
            cp.wait_send()
            cp.wait_recv()

    outs = pl.pallas_call(
        body, name=name, in_specs=[HBM] * (2 * n) + [SEM, SEM, ANY], out_specs=[HBM] * (2 * n),
        out_shape=[pltpu.HBM(s.shape, s.dtype) for s in srcs] + [pltpu.HBM(l.shape, l.dtype) for l in lands],
        input_output_aliases={i: i for i in range(2 * n)},
        compiler_params=pltpu.CompilerParams(has_side_effects=EFFECT),
    )(*srcs, *lands, send_sems, recv_sems, after)
    return outs[:n], outs[n:]


def _chip_copies(parts, lands, send_sems, recv_sems):
    x, y, c, chips = _place()
    return [_remote(parts[k].at[2 * px + py], lands[k].at[t], send_sems.at[3 * k + t], recv_sems.at[3 * k + t], (px, py, c))
            for k in range(len(parts)) for t, (px, py) in enumerate(chips)]


def chip_parts_start(parts):
    n = len(parts)

    def body(*refs):
        srcs, lands_in = refs[:n], refs[n:2 * n]
        send_sems, recv_sems = refs[2 * n], refs[2 * n + 1]
        token = refs[-1]
        for cp in _chip_copies(srcs, lands_in, send_sems, recv_sems):
            cp.start()
        token[...] = jnp.zeros_like(token)

    lands = [lax.empty((3,) + p.shape[1:], p.dtype) for p in parts]
    outs = pl.pallas_call(
        body, name="chip_parts_start", in_specs=[HBM] * (2 * n), out_specs=[SEM, SEM] + [HBM] * (2 * n) + [VMEM_SPEC],
        out_shape=[pltpu.SemaphoreType.DMA((3 * n,)), pltpu.SemaphoreType.DMA((3 * n,))]
        + [pltpu.HBM(p.shape, p.dtype) for p in parts] + [pltpu.HBM(l.shape, l.dtype) for l in lands] + [TOKEN],
        input_output_aliases={i: 2 + i for i in range(2 * n)},
        compiler_params=pltpu.CompilerParams(has_side_effects=EFFECT),
    )(*[_in_hbm(p) for p in parts], *[_in_hbm(l) for l in lands])
    return outs[0], outs[1], outs[2:2 + n], outs[2 + n:2 + 2 * n], outs[-1]


def chip_parts_wait(send_sems, recv_sems, parts, lands, after):
    n = len(parts)

    def body(*refs):
        srcs, lands_in = refs[:n], refs[n:2 * n]
        send, recv = refs[2 * n], refs[2 * n + 1]
        for cp in _chip_copies(srcs, lands_in, send, recv):
            cp.wait_send()
            cp.wait_recv()

    outs = pl.pallas_call(
        body, name="chip_parts_wait", in_specs=[HBM] * (2 * n) + [SEM, SEM, ANY], out_specs=[HBM] * (2 * n),
        out_shape=[pltpu.HBM(p.shape, p.dtype) for p in parts] + [pltpu.HBM(l.shape, l.dtype) for l in lands],
        input_output_aliases={i: i for i in range(2 * n)},
        compiler_params=pltpu.CompilerParams(has_side_effects=EFFECT),
    )(*parts, *lands, send_sems, recv_sems, after)
    return outs[:n], outs[n:]


def allreduce_small(packed):
    R = packed.shape[0]

    def body(x_ref, sum_ref, all_ref, send_sems, recv_sems):
        x, y, c, chips = _place()
        me, sibling = (x, y, c), (x, y, 1 - c)

        def rows(px, py, pc):
            return all_ref.at[4 * px + 2 * py + pc]

        def copy(k, block, to, src=None):
            return _remote(rows(*block) if src is None else src, rows(*block), send_sems.at[k], recv_sems.at[k], to)

        all_ref[4 * x + 2 * y + c] = x_ref[...]
        first = [copy(0, me, sibling, src=x_ref)]
        first += [copy(1 + j, me, (*chip, c), src=x_ref) for j, chip in enumerate(chips)]
        for cp in first:
            cp.start()
        passed = [copy(4 + j, (*chip, c), sibling) for j, chip in enumerate(chips)]
        for j, chip in enumerate(chips):
            copy(1 + j, (*chip, c), me).wait_recv()
            passed[j].start()
        copy(0, sibling, me).wait_recv()
        for j, chip in enumerate(chips):
            copy(4 + j, (*chip, 1 - c), me).wait_recv()
        for cp in first + passed:
            cp.wait_send()

        def chunk(i, carry):
            rws = pl.ds(pl.multiple_of(i * PACK_ROWS, PACK_ROWS), PACK_ROWS)
            acc = all_ref[0, rws, :]
            for d in range(1, N_DEV):
                acc = acc + all_ref[d, rws, :]
            sum_ref[rws, :] = acc
            return carry

        lax.fori_loop(0, R // PACK_ROWS, chunk, 0)

    return pl.pallas_call(
        body, name="allreduce_small", in_specs=[VMEM_SPEC], out_specs=VMEM_SPEC,
        out_shape=jax.ShapeDtypeStruct((R, LANES), F32),
        scratch_shapes=[pltpu.VMEM((N_DEV, R, LANES), F32), pltpu.SemaphoreType.DMA((7,)), pltpu.SemaphoreType.DMA((7,))],
        compiler_params=pltpu.CompilerParams(vmem_limit_bytes=VMEM_LIMIT),
    )(packed)


def _mixer_fwd(x, h, p, tabs, token, late_weights=None):
    z = mm_nn_cols(h, p["w_in"], token)
    qr, kp, vp = rope_fwd(z, tabs)
    mix = attn_fwd(qr, kp, vp, p["sink3"])
    c1 = conv_dw_fwd(z, p["conv_w32"], p["conv_dw_b"])
    mix = conv_ln_fwd(c1, p["conv_ln_g"], p["conv_ln_b"], mix)
    mix = sgu_fwd(z, p["sgu_ln_g"], p["sgu_ln_b"], p["sgu_w16"], p["sgu_b3"], mix)
    if late_weights is not None:
        p.update(late_weights(mix))
    x_mid, h2 = mm_nn_rows_res(mix, p["w_out"], x, p["ffn_norm_g"])
    return x_mid, h2, dict(x=x, h=h, z=z, qr=qr, kp=kp, vp=vp, c1=c1, mix=mix, x_mid=x_mid)


def _ffn_fwd(x_mid, h2, p, next_gain, token):
    gate, up, act = ffn_up(h2, p["w_gate"], p["w_up"], token)
    x_out, h_next = mm_nn_rows_res(act, p["w_down"], x_mid, next_gain)
    return x_out, h_next, dict(h2=h2, gate=gate, up=up, act=act)


def _layer_fwd(x, h, p, next_gain, tabs, token):
    x_mid, h2, s_mix = _mixer_fwd(x, h, p, tabs, token)
    x_out, h_next, s_ffn = _ffn_fwd(x_mid, h2, p, next_gain, token)
    return x_out, h_next, {**s_mix, **s_ffn}


def _ffn_bwd(dxb, p, s, token):
    dgate, dup = ffn_down_bwd(dxb, p["w_down"], s["gate"], s["up"], token)
    g_down = mm_tn_rows(s["act"], dxb)
    dh2 = mm_nt_cols([(dgate, p["w_gate"]), (dup, p["w_up"])], BF16, 1)
    g_gate = mm_tn_cols(s["h2"], dgate, N_CHIPS)
    g_up = mm_tn_cols(s["h2"], dup, N_CHIPS)
    dmidb, g_ffn_norm = rms_bwd(s["x_mid"], p["ffn_norm_g"], dh2, dxb, BF16)
    return dmidb, [g_gate, g_up, g_down.reshape(N_CHIPS, -1, D_MODEL)], g_ffn_norm


def _mixer_bwd(dmidb, p, s, tabs, token, out_dtype):
    dmix = mm_nt_rows(dmidb, p["w_out"], token)
    g_out = mm_tn_rows(s["mix"], dmidb)
    dq, dkp, dvp, dsink = attn_bwd(s["qr"], s["kp"], s["vp"], p["sink3"], dmix)
    dz = rope_bwd(dq, dkp, dvp, tabs)
    dc1, g_cln_g, g_cln_b = conv_ln_bwd(dmix, s["c1"], p["conv_ln_g"], p["conv_ln_b"])
    dz, g_cw, g_cb = conv_dw_bwd(dc1, s["z"], p["conv_w32"], dz)
    dz, g_sw, g_sb, g_sln_g, g_sln_b = sgu_bwd(s["z"], dmix, p["sgu_ln_g"], p["sgu_ln_b"], p["sgu_w16"], p["sgu_b3"], dz)
    dh = mm_nt_cols([(dz, p["w_in"])], BF16, N_CHIPS)
    g_in = mm_tn_cols(s["h"], dz, N_CHIPS)
    dx_in, g_mix_norm = rms_bwd(s["x"], p["mix_norm_g"], dh, dmidb, out_dtype)
    small = dict(mix_norm_g=g_mix_norm, sink=dsink[:, :, 0].reshape(1, N_Q_HEADS), conv_dw_w=g_cw[:CONV_KERNEL],
                 conv_dw_b=g_cb, conv_ln_g=g_cln_g, conv_ln_b=g_cln_b, sgu_ln_g=g_sln_g, sgu_ln_b=g_sln_b,
                 sgu_w=g_sw, sgu_b=g_sb[:, :, 0])
    return dx_in, [g_in, g_out.reshape(N_CHIPS, -1, D_MODEL)], small


def _layer_bwd(dxb, p, s, tabs, token, out_dtype):
    dmidb, ffn_big, g_ffn_norm = _ffn_bwd(dxb, p, s, token)
    dx_in, mix_big, small = _mixer_bwd(dmidb, p, s, tabs, token, out_dtype)
    return dx_in, mix_big + ffn_big, dict(small, ffn_norm_g=g_ffn_norm)


def _mixer_weights(gathered):
    w_in, w_out = gathered
    return dict(w_in=w_in, w_out=w_out.reshape(-1, D_MODEL))


def _ffn_weights(gathered):
    w_gate, w_up, w_down = gathered
    return dict(w_gate=w_gate, w_up=w_up, w_down=w_down.reshape(-1, D_MODEL))


def _small_params(l, conv_w_full, mix_norm_g, sink, conv_dw_b, conv_ln_g, conv_ln_b, sgu_ln_g, sgu_ln_b, sgu_w, sgu_b,
                  ffn_norm_g):
    return dict(
        mix_norm_g=mix_norm_g[l:l + 1], ffn_norm_g=ffn_norm_g[l:l + 1],
        sink3=jnp.broadcast_to(sink[l].reshape(N_KV_HEADS, Q_PER_KV, 1), (N_KV_HEADS, Q_PER_KV, LANES)),
        conv_w32=jnp.pad(conv_w_full[l], ((0, 32 - CONV_KERNEL), (0, 0))),
        conv_dw_b=conv_dw_b[l:l + 1], conv_ln_g=conv_ln_g[l:l + 1], conv_ln_b=conv_ln_b[l:l + 1],
        sgu_ln_g=sgu_ln_g[l:l + 1], sgu_ln_b=sgu_ln_b[l:l + 1], sgu_w16=sgu_w[l].astype(BF16),
        sgu_b3=jnp.broadcast_to(sgu_b[l][:, :, None], (SGU_HEADS, CHUNK, CHUNK)))


_SMALL = ["mix_norm_g", "sink", "conv_dw_b", "conv_ln_g", "conv_ln_b", "sgu_ln_g", "sgu_ln_b", "sgu_w", "sgu_b", "ffn_norm_g",
          "final_norm_g"]


def _pack_rows(arrays):
    rows, counts = [], []
    for a in arrays:
        flat = a.reshape(-1)
        n = -(-flat.shape[0] // LANES)
        rows.append(jnp.pad(flat, (0, n * LANES - flat.shape[0])).reshape(n, LANES))
        counts.append(n)
    packed = jnp.concatenate(rows, axis=0)
    pad = -packed.shape[0] % PACK_ROWS
    return jnp.pad(packed, ((0, pad), (0, 0))), counts


def _unpack_rows(packed, counts, shapes):
    out, r = [], 0
    for n, shape in zip(counts, shapes):
        size = math.prod(shape)
        out.append(packed[r:r + n].reshape(-1)[:size].reshape(shape))
        r += n
    return out


def kernel(x, mix_norm_g, w_in, sink, conv_dw_w, conv_dw_b, conv_ln_g, conv_ln_b, sgu_ln_g, sgu_ln_b, sgu_w, sgu_b, w_out, ffn_norm_g, w_gate, w_up, w_down, final_norm_g, loss_target, m_mix_norm_g, m_w_in, m_sink, m_conv_dw_w, m_conv_dw_b, m_conv_ln_g, m_conv_ln_b, m_sgu_ln_g, m_sgu_ln_b, m_sgu_w, m_sgu_b, m_w_out, m_ffn_norm_g, m_w_gate, m_w_up, m_w_down, m_final_norm_g, v_mix_norm_g, v_w_in, v_sink, v_conv_dw_w, v_conv_dw_b, v_conv_ln_g, v_conv_ln_b, v_sgu_ln_g, v_sgu_ln_b, v_sgu_w, v_sgu_b, v_w_out, v_ffn_norm_g, v_w_gate, v_w_up, v_w_down, v_final_norm_g):
    S = x.shape[1]
    my_chip = 2 * lax.axis_index("x") + lax.axis_index("y")
    c_idx = lax.axis_index("c").astype(jnp.int32).reshape(1)
    big_w = [w_in, w_out, w_gate, w_up, w_down]
    big_m = [m_w_in, m_w_out, m_w_gate, m_w_up, m_w_down]
    big_v = [v_w_in, v_w_out, v_w_gate, v_w_up, v_w_down]
    n_kinds = len(big_w)

    x_idx = lax.axis_index("x").astype(jnp.int32).reshape(1)
    y_idx = lax.axis_index("y").astype(jnp.int32).reshape(1)
    conv_w_all = gather_small(conv_dw_w)
    conv_w_full = jnp.transpose(conv_w_all, (1, 2, 0, 3)).reshape(DEPTH, CONV_KERNEL, CONV_WIDTH)
    tabs = rope_tables(S)
    no_token = jnp.zeros(TOKEN.shape, TOKEN.dtype)

    mixer_kinds, ffn_kinds = [0, 1], [2, 3, 4]
    def shards(layer, kinds, token):
        return [cast_layer(big_w[k], layer, token) for k in kinds]

    def fetch(pending, after):
        send_sems, recv_sems, srcs, lands, _ = pending
        return forward_halves(gather_wait(send_sems, recv_sems, srcs, lands, after))

    all_kinds = mixer_kinds + ffn_kinds
    first_mixer = gather_start(shards(0, mixer_kinds, no_token), conv_w_all)
    first_ffn = gather_start(shards(0, ffn_kinds, first_mixer[4]), first_mixer[4])
    pending = gather_start(shards(1, all_kinds, first_ffn[4]), first_ffn[4])
    later_shards = {l: shards(l, all_kinds, pending[4]) for l in range(2, DEPTH)}
    first_token = pending[4]
    for cast in later_shards.values():
        for a in cast:
            first_token = first_token + 0.0 * a[:TOKEN.shape[0], :LANES].astype(F32)
    act = x[0]
    h = rms_fwd(act, mix_norm_g[0:1], no_token)
    saved, params = [], []
    for l in range(DEPTH):
        p = _small_params(l, conv_w_full, mix_norm_g, sink, conv_dw_b, conv_ln_g, conv_ln_b, sgu_ln_g, sgu_ln_b, sgu_w, sgu_b,
                          ffn_norm_g)
        next_gain = mix_norm_g[l + 1:l + 2] if l + 1 < DEPTH else final_norm_g.reshape(1, D_MODEL)
        if l == 0:
            p.update(_mixer_weights(fetch(first_mixer, act)))
            x_mid, h2, s_mix = _mixer_fwd(act, h, p, tabs, first_token)
            p.update(_ffn_weights(fetch(first_ffn, x_mid)))
            late, token = None, no_token
        else:
            send_sems, recv_sems, srcs, lands, _ = pending
            lands = gather_wait(send_sems, recv_sems, srcs, lands, act)
            w_in_full = forward_halves(lands[:1])[0]
            p.update(w_in=w_in_full)
            fwd_send, fwd_recv, rest, token = forward_start(lands[1:])

            def late(mix, fwd_send=fwd_send, fwd_recv=fwd_recv, rest=rest):
                w_out_full, *ffn_full = forward_wait(fwd_send, fwd_recv, rest, mix)
                return dict(_ffn_weights(ffn_full), w_out=w_out_full.reshape(-1, D_MODEL))

            if l + 1 < DEPTH:
                pending = gather_start(later_shards[l + 1], w_in_full)
                token = token + pending[4]
        if l > 0:
            x_mid, h2, s_mix = _mixer_fwd(act, h, p, tabs, token, late)
        act, h, s_ffn = _ffn_fwd(x_mid, h2, p, next_gain, token)
        params.append(p)
        saved.append({**s_mix, **s_ffn})
    loss_part, dxb, g_final = final_loss(act, final_norm_g.reshape(1, D_MODEL), loss_target[0])
    loss = lax.psum(loss_part[0, 0], ("x", "y", "c"))

    halves = [lax.empty((DEPTH, w.shape[1] // 2, w.shape[2]), F32) for w in big_w]
    small_grads = [None] * DEPTH

    def chip_start(layer, kinds, grads, recv):
        chip_sum = [add_sibling_half(g, r, c_idx) for g, r in zip(grads, recv)]
        send_sems, recv_sems, parts, lands, token = chip_parts_start(chip_sum)
        return (layer, kinds, send_sems, recv_sems, parts, lands), token

    def reduce_start(layer, kinds, grads):
        return chip_start(layer, kinds, grads, exchange_sibling_halves(grads))

    def reduce_finish(pending, halves, after):
        layer, kinds, send_sems, recv_sems, parts, lands = pending
        parts, others = chip_parts_wait(send_sems, recv_sems, parts, lands, after)
        halves = list(halves)
        for i, k in enumerate(kinds):
            halves[k] = sum_chips(parts[i], others[i], halves[k], x_idx, y_idx, layer)
        return halves

    pending, token = None, no_token
    for l in reversed(range(DEPTH)):
        dmidb, ffn_big, g_ffn_norm = _ffn_bwd(dxb, params[l], saved[l], token)
        if l == 0:
            last_ffn, mixer_token = reduce_start(l, ffn_kinds, ffn_big)
        else:
            half_shapes = [(g.shape[0], g.shape[1] // 2, g.shape[2]) for g in ffn_big]
            sib_send, sib_recv, ffn_big, ffn_lands, mixer_token = pair_start("sibling_start", _sibling_half_copies, ffn_big, half_shapes)
        dxb, mix_big, small = _mixer_bwd(dmidb, params[l], saved[l], tabs, mixer_token, F32 if l == 0 else BF16)
        small_grads[l] = dict(small, ffn_norm_g=g_ffn_norm)
        if pending is not None:
            halves = reduce_finish(pending, halves, dxb)
        if l == 0:
            last_mixer, token = reduce_start(l, mixer_kinds, mix_big)
            halves = reduce_finish(last_ffn, halves, token)
        else:
            ffn_big, ffn_recv = pair_wait("sibling_wait", _sibling_half_copies, sib_send, sib_recv, ffn_big, ffn_lands, dxb)
            mix_recv = exchange_sibling_halves(mix_big)
            pending, token = chip_start(l, mixer_kinds + ffn_kinds, list(mix_big) + list(ffn_big), list(mix_recv) + list(ffn_recv))

    def final_start(kinds):
        send_sems, recv_sems, mine, lands, _ = pair_start("final_start", _sibling_whole_copies, [halves[k] for k in kinds],
                                                          [halves[k].shape for k in kinds])
        return send_sems, recv_sems, mine, lands

    ffn_final = final_start(ffn_kinds)

    stacked = {n: jnp.stack([small_grads[l][n] for l in range(DEPTH)]) for n in small_grads[0]}
    stacked["final_norm_g"] = g_final
    packed, counts = _pack_rows([stacked[n] for n in _SMALL] + [stacked["conv_dw_w"]])
    reduced = allreduce_small(packed)
    small_w = dict(mix_norm_g=mix_norm_g, sink=sink, conv_dw_b=conv_dw_b, conv_ln_g=conv_ln_g, conv_ln_b=conv_ln_b,
                   sgu_ln_g=sgu_ln_g, sgu_ln_b=sgu_ln_b, sgu_w=sgu_w, sgu_b=sgu_b, ffn_norm_g=ffn_norm_g,
                   final_norm_g=final_norm_g)
    small_m = dict(mix_norm_g=m_mix_norm_g, sink=m_sink, conv_dw_b=m_conv_dw_b, conv_ln_g=m_conv_ln_g,
                   conv_ln_b=m_conv_ln_b, sgu_ln_g=m_sgu_ln_g, sgu_ln_b=m_sgu_ln_b, sgu_w=m_sgu_w, sgu_b=m_sgu_b,
                   ffn_norm_g=m_ffn_norm_g, final_norm_g=m_final_norm_g)
    small_v = dict(mix_norm_g=v_mix_norm_g, sink=v_sink, conv_dw_b=v_conv_dw_b, conv_ln_g=v_conv_ln_g,
                   conv_ln_b=v_conv_ln_b, sgu_ln_g=v_sgu_ln_g, sgu_ln_b=v_sgu_ln_b, sgu_w=v_sgu_w, sgu_b=v_sgu_b,
                   ffn_norm_g=v_ffn_norm_g, final_norm_g=v_final_norm_g)
    shapes = [small_w[n].shape for n in _SMALL] + [(DEPTH, CONV_KERNEL, CONV_WIDTH)]
    red = _unpack_rows(reduced, counts, shapes)
    g_small = dict(zip(_SMALL, red[:-1]))
    g_small["conv_dw_w"] = lax.dynamic_slice_in_dim(red[-1], my_chip * LANES, LANES, axis=2)
    small_w["conv_dw_w"], small_m["conv_dw_w"], small_v["conv_dw_w"] = conv_dw_w, m_conv_dw_w, v_conv_dw_w
    names = _SMALL + ["conv_dw_w"]
    pw, cnt = _pack_rows([small_w[n] for n in names])
    pg, _ = _pack_rows([g_small[n] for n in names])
    pm, _ = _pack_rows([small_m[n] for n in names])
    pv, _ = _pack_rows([small_v[n] for n in names])
    sd, sm, sv = adamw(pw, pg, pm, pv)
    shp = [small_w[n].shape for n in names]
    d_small = dict(zip(names, _unpack_rows(sd, cnt, shp)))
    m_small = dict(zip(names, _unpack_rows(sm, cnt, shp)))
    v_small = dict(zip(names, _unpack_rows(sv, cnt, shp)))

    big_names = ["w_in", "w_out", "w_gate", "w_up", "w_down"]
    g_big, d_big, m_big, v_big = {}, {}, {}, {}
    after = sd
    for kinds, final in ((ffn_kinds, ffn_final), (mixer_kinds, None)):
        if final is None:
            halves = reduce_finish(last_mixer, halves, after)
            final = final_start(kinds)
        send_sems, recv_sems, sent, lands = final
        for i, k in enumerate(kinds):
            n = big_names[k]
            mine, theirs = pair_wait_one("final_wait", send_sems, recv_sems, sent[i], lands[i], after, i)
            g_big[n], d_big[n], m_big[n], v_big[n] = adamw_halves(big_w[k], mine, theirs, big_m[k], big_v[k], c_idx)
            after = d_big[n]

    order = ["mix_norm_g", "w_in", "sink", "conv_dw_w", "conv_dw_b", "conv_ln_g", "conv_ln_b", "sgu_ln_g", "sgu_ln_b",
             "sgu_w", "sgu_b", "w_out", "ffn_norm_g", "w_gate", "w_up", "w_down", "final_norm_g"]
    grads = {**g_small, **g_big}
    deltas = {**d_small, **d_big}
    new_m = {**m_small, **m_big}
    new_v = {**v_small, **v_big}
    return (loss, dxb[None], *[grads[n] for n in order], *[deltas[n] for n in order],
            *[new_m[n] for n in order], *[new_v[n] for n in order])
```

```python
import math

import jax
import jax.numpy as jnp
from jax import lax
from jax.experimental import pallas as pl
from jax.experimental.pallas import tpu as pltpu

F32, BF16 = jnp.float32, jnp.bfloat16

D_MODEL = 2048
DEPTH = 4
HEAD_DIM = 128
N_Q_HEADS = 8
N_KV_HEADS = 2
Q_PER_KV = N_Q_HEADS // N_KV_HEADS
ATTN_WIDTH = N_Q_HEADS * HEAD_DIM
KV_WIDTH = N_KV_HEADS * HEAD_DIM
CONV_WIDTH = 512
CONV_KERNEL = 31
CONV_PAD = 16
SGU_WIDTH = 512
SGU_HEADS = 4
CHUNK = 128
IN_WIDTH = 3584
D_FF = 5632
WINDOW = 128
ROT_DIM = 32
ROPE_THETA = 500000.0
EPS = 1e-6
N_CHIPS = 4
N_DEV = 8
LANES = 128
PACK_ROWS = 64
OFF_K = ATTN_WIDTH
OFF_V = OFF_K + KV_WIDTH
OFF_CA = OFF_V + KV_WIDTH
OFF_CG = OFF_CA + CONV_WIDTH
OFF_U = OFF_CG + CONV_WIDTH
OFF_VV = OFF_U + SGU_WIDTH

ADAM_LR, ADAM_B1, ADAM_B2, ADAM_EPS, ADAM_WD, ADAM_STEP = 0.001, 0.9, 0.999, 1e-08, 0.01, 10

VMEM_LIMIT = 56 * 1024 * 1024
MESH = pl.DeviceIdType.MESH
HBM = pl.BlockSpec(memory_space=pltpu.HBM)
VMEM_SPEC = pl.BlockSpec(memory_space=pltpu.VMEM)


def _call(name, body, *, grid, in_specs, out_specs, out_shape, scratch=(), sem=None, aliases=None):
    params = dict(vmem_limit_bytes=VMEM_LIMIT)
    if sem is not None:
        params["dimension_semantics"] = sem
    return pl.pallas_call(
        body, name=name, grid=grid, in_specs=in_specs, out_specs=out_specs, out_shape=out_shape,
        scratch_shapes=list(scratch), input_output_aliases=aliases or {}, compiler_params=pltpu.CompilerParams(**params))


def _sigmoid(x):
    return 1.0 / (1.0 + jnp.exp(-x))


def rms_fwd(x, g, token):
    S = x.shape[0]
    tm = min(512, S)

    def body(x_ref, g_ref, token_ref, o_ref):
        xv = x_ref[...]
        r = lax.rsqrt(jnp.mean(xv * xv, axis=-1, keepdims=True) + EPS)
        o_ref[...] = (xv * r * g_ref[...]).astype(BF16)

    return _call("rms_fwd", body, grid=(S // tm,),
                 in_specs=[pl.BlockSpec((tm, D_MODEL), lambda i: (i, 0)), pl.BlockSpec((1, D_MODEL), lambda i: (0, 0)),
                           pl.BlockSpec((8, LANES), lambda i: (0, 0))],
                 out_specs=pl.BlockSpec((tm, D_MODEL), lambda i: (i, 0)),
                 out_shape=jax.ShapeDtypeStruct((S, D_MODEL), BF16), sem=("parallel",))(x, g, token)


def _rms_bwd_math(xv, gv, dh):
    r = lax.rsqrt(jnp.mean(xv * xv, axis=-1, keepdims=True) + EPS)
    n = xv * r
    dn = dh * gv
    dx = r * (dn - n * jnp.mean(dn * n, axis=-1, keepdims=True))
    dg = jnp.sum(dh * n, axis=0, keepdims=True)
    return dx, dg


def rms_bwd(x, g, dh, dres, out_dtype):
    S = x.shape[0]
    tm = min(512, S)

    def body(x_ref, g_ref, dh_ref, dres_ref, dx_ref, dg_ref):
        dx, dg = _rms_bwd_math(x_ref[...], g_ref[...], dh_ref[...].astype(F32))
        dx_ref[...] = (dx + dres_ref[...].astype(F32)).astype(out_dtype)

        @pl.when(pl.program_id(0) == 0)
        def _():
            dg_ref[...] = dg

        @pl.when(pl.program_id(0) > 0)
        def _():
            dg_ref[...] += dg

    row = pl.BlockSpec((tm, D_MODEL), lambda i: (i, 0))
    vec = pl.BlockSpec((1, D_MODEL), lambda i: (0, 0))
    return _call("rms_bwd", body, grid=(S // tm,), in_specs=[row, vec, row, row], out_specs=[row, vec],
                 out_shape=[jax.ShapeDtypeStruct((S, D_MODEL), out_dtype), jax.ShapeDtypeStruct((1, D_MODEL), F32)],
                 sem=("arbitrary",))(x, g, dh, dres)


def final_loss(x, g, target):
    S = x.shape[0]
    tm = min(256, S)

    def body(x_ref, g_ref, t_ref, loss_ref, dxb_ref, dg_ref):
        xv = x_ref[...]
        gv = g_ref[...]
        r = lax.rsqrt(jnp.mean(xv * xv, axis=-1, keepdims=True) + EPS)
        err = xv * r * gv - t_ref[...]
        part = 0.5 * jnp.sum(jnp.mean(err * err, axis=-1, keepdims=True), axis=0, keepdims=True)
        dx, dg = _rms_bwd_math(xv, gv, err * (1.0 / D_MODEL))
        dxb_ref[...] = dx.astype(BF16)

        @pl.when(pl.program_id(0) == 0)
        def _():
            dg_ref[...] = dg
            loss_ref[...] = part

        @pl.when(pl.program_id(0) > 0)
        def _():
            dg_ref[...] += dg
            loss_ref[...] += part

    row = pl.BlockSpec((tm, D_MODEL), lambda i: (i, 0))
    vec = pl.BlockSpec((1, D_MODEL), lambda i: (0, 0))
    one = pl.BlockSpec((1, 1), lambda i: (0, 0))
    return _call("final_loss", body, grid=(S // tm,), in_specs=[row, vec, row], out_specs=[one, row, vec],
                 out_shape=[jax.ShapeDtypeStruct((1, 1), F32), jax.ShapeDtypeStruct((S, D_MODEL), BF16),
                            jax.ShapeDtypeStruct((1, D_MODEL), F32)],
                 sem=("arbitrary",))(x, g, target)


EPILOGUE_ROWS = 256
NN = (((1,), (0,)), ((), ()))
NT = (((1,), (1,)), ((), ()))
TN = (((0,), (0,)), ((), ()))


def _matmul(name, operands, in_specs, out_shape, out_specs, grid, pairs, dims, acc_shape, epilogue):
    n_in, n_out, nk = len(operands), len(out_shape), grid[-1]

    def body(*refs):
        ins, outs = refs[:n_in], refs[n_in:n_in + n_out]
        part = None
        for ia, ib in pairs:
            d = lax.dot_general(ins[ia][...], ins[ib][...], dims, preferred_element_type=F32)
            part = d if part is None else part + d
        if nk == 1:
            epilogue(part, ins, outs)
        else:
            acc = refs[-1]
            k = pl.program_id(len(grid) - 1)

            @pl.when(k == 0)
            def _():
                acc[...] = part

            @pl.when(k > 0)
            def _():
                acc[...] += part

            @pl.when(k == nk - 1)
            def _():
                epilogue(acc[...], ins, outs)

    scratch = [pltpu.VMEM(acc_shape, F32)] if nk > 1 else []
    sem = ("parallel",) * (len(grid) - 1) + ("arbitrary",)
    return _call(name, body, grid=grid, in_specs=in_specs, out_specs=out_specs, out_shape=out_shape,
                 scratch=scratch, sem=sem)(*operands)


def _store(dtype):
    def epilogue(acc, ins, outs):
        outs[0][...] = acc.astype(dtype)
    return epilogue


def mm_nn_cols(a, w, token):
    S, K = a.shape
    J, _, Ns = w.shape
    tm = min(512, S)
    return _matmul("mm_nn_cols", (a, w, token),
                   [pl.BlockSpec((tm, K), lambda j, i, k: (i, 0)), pl.BlockSpec((None, K, Ns), lambda j, i, k: (j, 0, 0)),
                    pl.BlockSpec((8, LANES), lambda j, i, k: (0, 0))],
                   [jax.ShapeDtypeStruct((S, J * Ns), BF16)], [pl.BlockSpec((tm, Ns), lambda j, i, k: (i, j))],
                   (J, S // tm, 1), [(0, 1)], NN, None, _store(BF16))[0]


def ffn_up(h, wg, wu, token):
    S, K = h.shape
    J, _, Ns = wg.shape
    tm = min(512, S)

    sub = min(EPILOGUE_ROWS, tm)

    def body(h_ref, wg_ref, wu_ref, token_ref, g_ref, u_ref, a_ref):
        for r in range(tm // sub):
            rows = slice(r * sub, (r + 1) * sub)
            hv = h_ref[rows, :]
            gv = jnp.dot(hv, wg_ref[...], preferred_element_type=F32)
            uv = jnp.dot(hv, wu_ref[...], preferred_element_type=F32)
            g_ref[rows, :] = gv.astype(BF16)
            u_ref[rows, :] = uv.astype(BF16)
            a_ref[rows, :] = (gv * _sigmoid(gv) * uv).astype(BF16)

    wspec = pl.BlockSpec((None, K, Ns), lambda j, i: (j, 0, 0))
    ospec = pl.BlockSpec((tm, Ns), lambda j, i: (i, j))
    oshape = jax.ShapeDtypeStruct((S, J * Ns), BF16)
    return _call("ffn_up", body, grid=(J, S // tm),
                 in_specs=[pl.BlockSpec((tm, K), lambda j, i: (i, 0)), wspec, wspec, pl.BlockSpec((8, LANES), lambda j, i: (0, 0))],
                 out_specs=[ospec, ospec, ospec], out_shape=[oshape, oshape, oshape],
                 sem=("parallel", "parallel"))(h, wg, wu, token)


def mm_nn_rows_res(a, w, res, gain):
    S, K = a.shape
    N = w.shape[1]
    tm = min(512, S)
    tk, tn = (K, N) if K <= 2048 else (K // 2, N // 2)
    n_n, n_k = N // tn, K // tk

    def body(a_ref, w_ref, res_ref, g_ref, x_ref, h_ref, *acc):
        n, k = pl.program_id(1), pl.program_id(2)

        def normed(xv):
            r = lax.rsqrt(jnp.mean(xv * xv, axis=-1, keepdims=True) + EPS)
            h_ref[...] = (xv * r * g_ref[...]).astype(BF16)

        def store_columns(total):
            if n_n == 1:
                xv = total + res_ref[...]
                x_ref[...] = xv
                normed(xv)
                return
            for c in range(n_n):
                @pl.when(n == c)
                def _(c=c):
                    cols = slice(c * tn, (c + 1) * tn)
                    x_ref[:, cols] = total + res_ref[:, cols]

            @pl.when(n == n_n - 1)
            def _():
                normed(x_ref[...])

        part = jnp.dot(a_ref[...], w_ref[...], preferred_element_type=F32)
        if n_k == 1:
            store_columns(part)
        else:
            @pl.when(k == 0)
            def _():
                acc[0][...] = part

            @pl.when(k > 0)
            def _():
                acc[0][...] += part

            @pl.when(k == n_k - 1)
            def _():
                store_columns(acc[0][...])

    row = pl.BlockSpec((tm, N), lambda i, n, k: (i, 0))
    return _call("mm_nn_rows_res", body, grid=(S // tm, n_n, n_k),
                 in_specs=[pl.BlockSpec((tm, tk), lambda i, n, k: (i, k)), pl.BlockSpec((tk, tn), lambda i, n, k: (k, n)), row,
                           pl.BlockSpec((1, N), lambda i, n, k: (0, 0))],
                 out_specs=[row, row], out_shape=[jax.ShapeDtypeStruct((S, N), F32), jax.ShapeDtypeStruct((S, N), BF16)],
                 scratch=[pltpu.VMEM((tm, tn), F32)] if n_k > 1 else [],
                 sem=("parallel", "arbitrary", "arbitrary"))(a, w, res, gain)


def mm_nt_cols(pairs_in, out_dtype, shards_per_step):
    dz0, w0 = pairs_in[0]
    S = dz0.shape[0]
    J, K, Ns = w0.shape
    tm = min(512, S)
    sps = shards_per_step
    operands, specs, pairs = [], [], []
    for dz, w in pairs_in:
        for s in range(sps):
            pairs.append((len(operands), len(operands) + 1))
            operands += [dz, w]
            specs += [pl.BlockSpec((tm, Ns), lambda i, j, s=s: (i, j * sps + s)),
                      pl.BlockSpec((None, K, Ns), lambda i, j, s=s: (j * sps + s, 0, 0))]
    return _matmul("mm_nt_cols%d" % len(pairs_in), tuple(operands), specs,
                   [jax.ShapeDtypeStruct((S, K), out_dtype)], [pl.BlockSpec((tm, K), lambda i, j: (i, 0))],
                   (S // tm, J // sps), pairs, NT, (tm, K), _store(out_dtype))[0]


def mm_nt_rows(dy, w, token):
    S, N = dy.shape
    K = w.shape[0]
    tm, tko = min(1024, S), 512
    return _matmul("mm_nt_rows", (dy, w, token),
                   [pl.BlockSpec((tm, N), lambda i, kk, z: (i, 0)), pl.BlockSpec((tko, N), lambda i, kk, z: (kk, 0)),
                    pl.BlockSpec((8, LANES), lambda i, kk, z: (0, 0))],
                   [jax.ShapeDtypeStruct((S, K), BF16)], [pl.BlockSpec((tm, tko), lambda i, kk, z: (i, kk))],
                   (S // tm, K // tko, 1), [(0, 1)], NT, None, _store(BF16))[0]


def ffn_down_bwd(dy, w, gate, up, token):
    S, N = dy.shape
    K = w.shape[0]
    tm, tko = min(1024, S), 512
    sub = min(EPILOGUE_ROWS, tm)

    def body(dy_ref, w_ref, g_ref, u_ref, token_ref, dg_ref, du_ref):
        for r in range(tm // sub):
            rows = slice(r * sub, (r + 1) * sub)
            dact = lax.dot_general(dy_ref[rows, :], w_ref[...], NT, preferred_element_type=F32)
            gv = g_ref[rows, :].astype(F32)
            uv = u_ref[rows, :].astype(F32)
            sg = _sigmoid(gv)
            dg_ref[rows, :] = (dact * uv * sg * (1.0 + gv * (1.0 - sg))).astype(BF16)
            du_ref[rows, :] = (dact * gv * sg).astype(BF16)

    tile = pl.BlockSpec((tm, tko), lambda i, kk: (i, kk))
    oshape = jax.ShapeDtypeStruct((S, K), BF16)
    return _call("ffn_down_bwd", body, grid=(S // tm, K // tko),
                 in_specs=[pl.BlockSpec((tm, N), lambda i, kk: (i, 0)), pl.BlockSpec((tko, N), lambda i, kk: (kk, 0)), tile, tile,
                           pl.BlockSpec((8, LANES), lambda i, kk: (0, 0))],
                 out_specs=[tile, tile], out_shape=[oshape, oshape], sem=("parallel", "parallel"))(dy, w, gate, up, token)


def mm_tn_cols(a, dz, J):
    S, M = a.shape
    Ns = dz.shape[1] // J
    tm, tk = 512, S
    return _matmul("mm_tn_cols", (a, dz),
                   [pl.BlockSpec((tk, tm), lambda j, m, k: (k, m)), pl.BlockSpec((tk, Ns), lambda j, m, k: (k, j))],
                   [jax.ShapeDtypeStruct((J, M, Ns), BF16)], [pl.BlockSpec((None, tm, Ns), lambda j, m, k: (j, m, 0))],
                   (J, M // tm, S // tk), [(0, 1)], TN, (tm, Ns), _store(BF16))[0]


def mm_tn_rows(a, dy):
    S, K = a.shape
    N = dy.shape[1]
    tm, tk = 512, S
    return _matmul("mm_tn_rows", (a, dy),
                   [pl.BlockSpec((tk, tm), lambda m, k: (k, m)), pl.BlockSpec((tk, N), lambda m, k: (k, 0))],
                   [jax.ShapeDtypeStruct((K, N), BF16)], [pl.BlockSpec((tm, N), lambda m, k: (m, 0))],
                   (K // tm, S // tk), [(0, 1)], TN, (tm, N), _store(BF16))[0]


def rope_tables(S):
    half = ROT_DIM // 2
    pos = jnp.arange(S, dtype=F32)
    inv = ROPE_THETA ** (-jnp.arange(0, ROT_DIM, 2, dtype=F32) / ROT_DIM)
    ang = pos[:, None] * inv[None, :]
    cos, sin = jnp.cos(ang), jnp.sin(ang)
    zeros = jnp.zeros((S, HEAD_DIM - ROT_DIM), F32)
    c = jnp.concatenate([cos, cos, jnp.ones((S, HEAD_DIM - ROT_DIM), F32)], axis=1)
    s_lo = jnp.concatenate([-sin, jnp.zeros((S, half), F32), zeros], axis=1)
    s_hi = jnp.concatenate([jnp.zeros((S, half), F32), sin, zeros], axis=1)
    return c, s_lo, s_hi


ROPE_ROWS = 512


def _rope(t, c, s_lo, s_hi):
    half = ROT_DIM // 2
    return t * c + pltpu.roll(t, HEAD_DIM - half, 1) * s_lo + pltpu.roll(t, half, 1) * s_hi


def _unrope(d, c, s_lo, s_hi):
    half = ROT_DIM // 2
    return d * c + pltpu.roll(d * s_lo, half, 1) + pltpu.roll(d * s_hi, HEAD_DIM - half, 1)


def rope_fwd(z, tabs):
    S = z.shape[0]
    T = min(ROPE_ROWS, S)

    def body(q_ref, kv_ref, c_ref, sl_ref, sh_ref, qr_ref, kp_ref, vp_ref):
        i = pl.program_id(0)

        @pl.when(i == 0)
        def _():
            zero = jnp.zeros((CHUNK, KV_WIDTH), BF16)
            kp_ref[0:CHUNK, :] = zero
            vp_ref[0:CHUNK, :] = zero
            kp_ref[S + CHUNK:S + 2 * CHUNK, :] = zero
            vp_ref[S + CHUNK:S + 2 * CHUNK, :] = zero

        c, sl, sh = c_ref[...], sl_ref[...], sh_ref[...]
        for h in range(N_Q_HEADS):
            cols = slice(h * HEAD_DIM, (h + 1) * HEAD_DIM)
            qr_ref[:, cols] = _rope(q_ref[:, cols].astype(F32), c, sl, sh).astype(BF16)
        rows = pl.ds(pl.multiple_of(CHUNK + i * T, CHUNK), T)
        for g in range(N_KV_HEADS):
            cols = slice(g * HEAD_DIM, (g + 1) * HEAD_DIM)
            kp_ref[rows, cols] = _rope(kv_ref[:, cols].astype(F32), c, sl, sh).astype(BF16)
        vp_ref[rows, :] = kv_ref[:, KV_WIDTH:2 * KV_WIDTH]

    tab = pl.BlockSpec((T, HEAD_DIM), lambda i: (i, 0))
    pad = pl.BlockSpec((S + 2 * CHUNK, KV_WIDTH), lambda i: (0, 0))
    return _call("rope_fwd", body, grid=(S // T,),
                 in_specs=[pl.BlockSpec((T, ATTN_WIDTH), lambda i: (i, 0)),
                           pl.BlockSpec((T, 2 * KV_WIDTH), lambda i: (i, OFF_K // (2 * KV_WIDTH))), tab, tab, tab],
                 out_specs=[pl.BlockSpec((T, ATTN_WIDTH), lambda i: (i, 0)), pad, pad],
                 out_shape=[jax.ShapeDtypeStruct((S, ATTN_WIDTH), BF16), jax.ShapeDtypeStruct((S + 2 * CHUNK, KV_WIDTH), BF16),
                            jax.ShapeDtypeStruct((S + 2 * CHUNK, KV_WIDTH), BF16)], sem=("arbitrary",))(z, z, *tabs)


def rope_bwd(dq, dkp, dvp, tabs):
    S = dq.shape[0]
    T = min(ROPE_ROWS, S)

    def body(dq_ref, dk_ref, dv_ref, c_ref, sl_ref, sh_ref, o_ref):
        c, sl, sh = c_ref[...], sl_ref[...], sh_ref[...]
        for h in range(N_Q_HEADS):
            cols = slice(h * HEAD_DIM, (h + 1) * HEAD_DIM)
            o_ref[:, cols] = _unrope(dq_ref[:, cols], c, sl, sh).astype(BF16)
        rows = pl.ds(pl.multiple_of(CHUNK + pl.program_id(0) * T, CHUNK), T)
        for g in range(N_KV_HEADS):
            cols = slice(g * HEAD_DIM, (g + 1) * HEAD_DIM)
            o_ref[:, OFF_K + g * HEAD_DIM:OFF_K + (g + 1) * HEAD_DIM] = _unrope(dk_ref[rows, cols], c, sl, sh).astype(BF16)
        o_ref[:, OFF_V:OFF_V + KV_WIDTH] = dv_ref[rows, :].astype(BF16)

    tab = pl.BlockSpec((T, HEAD_DIM), lambda i: (i, 0))
    pad = pl.BlockSpec((S + 2 * CHUNK, KV_WIDTH), lambda i: (0, 0))
    return _call("rope_bwd", body, grid=(S // T,),
                 in_specs=[pl.BlockSpec((T, ATTN_WIDTH), lambda i: (i, 0)), pad, pad, tab, tab, tab],
                 out_specs=pl.BlockSpec((T, OFF_CA), lambda i: (i, 0)),
                 out_shape=jax.ShapeDtypeStruct((S, IN_WIDTH), BF16), sem=("parallel",))(dq, dkp, dvp, *tabs)


STACK = Q_PER_KV * CHUNK


def _stack_heads(ref, rows):
    return jnp.concatenate([ref[rows, r * HEAD_DIM:(r + 1) * HEAD_DIM] for r in range(Q_PER_KV)], axis=0)


def _stack_sinks(s_ref):
    return jnp.concatenate([jnp.broadcast_to(s_ref[r:r + 1, 0:1], (CHUNK, 1)) for r in range(Q_PER_KV)], axis=0)


MASKED = -1e30


def _scores(q, kb):
    return lax.dot_general(q, kb, NT, preferred_element_type=F32) * (1.0 / math.sqrt(HEAD_DIM))


def _band_bias():
    row = lax.broadcasted_iota(jnp.int32, (STACK, 3 * CHUNK), 0) & (CHUNK - 1)
    col = lax.broadcasted_iota(jnp.int32, (STACK, 3 * CHUNK), 1)
    return jnp.where(jnp.abs(col - CHUNK - row) <= WINDOW, 0.0, MASKED).astype(F32)


def _edge_bias(n, S):
    kpos = (n - 1) * CHUNK + lax.broadcasted_iota(jnp.int32, (1, 3 * CHUNK), 1)
    return jnp.where((kpos >= 0) & (kpos < S), 0.0, MASKED).astype(F32)


def _softmax_sink(s, sk, bias):
    s = s + bias
    m = jnp.maximum(jnp.max(s, axis=1, keepdims=True), sk)
    e = jnp.exp(s - m)
    es = jnp.exp(sk - m)
    inv = 1.0 / (jnp.sum(e, axis=1, keepdims=True) + es)
    return e * inv, es * inv


def _block_views(i, nblk):
    ns = [i * nblk + b for b in range(nblk)]
    wins = [pl.ds(pl.multiple_of(n * CHUNK, CHUNK), 3 * CHUNK) for n in ns]
    rows = [slice(b * CHUNK, (b + 1) * CHUNK) for b in range(nblk)]
    return ns, wins, rows


def attn_fwd(qr, kp, vp, sink3):
    S = qr.shape[0]
    tq = min(2048, S)
    gw = Q_PER_KV * HEAD_DIM
    nblk = tq // CHUNK

    def body(q_ref, k_ref, v_ref, s_ref, o_ref):
        ns, wins, rows = _block_views(pl.program_id(1), nblk)
        sk = _stack_sinks(s_ref)
        band = _band_bias()
        scores = [_scores(_stack_heads(q_ref, rows[b]), k_ref[wins[b], :]) for b in range(nblk)]
        probs = [_softmax_sink(scores[b], sk, band + _edge_bias(ns[b], S))[0].astype(BF16) for b in range(nblk)]
        outs = [jnp.dot(probs[b], v_ref[wins[b], :], preferred_element_type=F32).astype(BF16) for b in range(nblk)]
        for b in range(nblk):
            for r in range(Q_PER_KV):
                o_ref[rows[b], r * HEAD_DIM:(r + 1) * HEAD_DIM] = outs[b][r * CHUNK:(r + 1) * CHUNK]

    kv = pl.BlockSpec((S + 2 * CHUNK, HEAD_DIM), lambda g, i: (0, g))
    return _call("attn_fwd", body, grid=(N_KV_HEADS, S // tq),
                 in_specs=[pl.BlockSpec((tq, gw), lambda g, i: (i, g)), kv, kv,
                           pl.BlockSpec((None, Q_PER_KV, LANES), lambda g, i: (g, 0, 0))],
                 out_specs=pl.BlockSpec((tq, gw), lambda g, i: (i, g)),
                 out_shape=jax.ShapeDtypeStruct((S, D_MODEL), BF16), sem=("parallel", "arbitrary"))(qr, kp, vp, sink3)


def attn_bwd(qr, kp, vp, sink3, dmix):
    S = qr.shape[0]
    tq = min(1024, S)
    gw = Q_PER_KV * HEAD_DIM
    scale = 1.0 / math.sqrt(HEAD_DIM)
    nblk = tq // CHUNK

    def body(q_ref, k_ref, v_ref, s_ref, do_ref, dq_ref, dk_ref, dv_ref, ds_ref):
        i = pl.program_id(1)

        @pl.when(i == 0)
        def _():
            dk_ref[...] = jnp.zeros_like(dk_ref)
            dv_ref[...] = jnp.zeros_like(dv_ref)
            ds_ref[...] = jnp.zeros_like(ds_ref)

        blocks = range(nblk)
        ns, wins, rows = _block_views(i, nblk)
        sk = _stack_sinks(s_ref)
        band = _band_bias()
        qs = [_stack_heads(q_ref, rows[b]) for b in blocks]
        dos = [_stack_heads(do_ref, rows[b]) for b in blocks]
        scores = [_scores(qs[b], k_ref[wins[b], :]) for b in blocks]
        dps = [lax.dot_general(dos[b], v_ref[wins[b], :], NT, preferred_element_type=F32) for b in blocks]
        probs = [_softmax_sink(scores[b], sk, band + _edge_bias(ns[b], S)) for b in blocks]
        deltas = [jnp.sum(probs[b][0] * dps[b], axis=1, keepdims=True) for b in blocks]
        dscs = [(probs[b][0] * (dps[b] - deltas[b]) * scale).astype(BF16) for b in blocks]
        dqs = [jnp.dot(dscs[b], k_ref[wins[b], :], preferred_element_type=F32) for b in blocks]
        dks = [lax.dot_general(dscs[b], qs[b], TN, preferred_element_type=F32) for b in blocks]
        dvs = [lax.dot_general(probs[b][0].astype(BF16), dos[b], TN, preferred_element_type=F32) for b in blocks]
        for b in blocks:
            for r in range(Q_PER_KV):
                dq_ref[rows[b], r * HEAD_DIM:(r + 1) * HEAD_DIM] = dqs[b][r * CHUNK:(r + 1) * CHUNK]
        for m in range(nblk + 2):
            parts = [(b, m - b) for b in blocks if 0 <= m - b <= 2]
            krows = pl.ds(pl.multiple_of(i * tq + m * CHUNK, CHUNK), CHUNK)
            dk_ref[krows, :] += sum(dks[b][o * CHUNK:(o + 1) * CHUNK] for b, o in parts)
            dv_ref[krows, :] += sum(dvs[b][o * CHUNK:(o + 1) * CHUNK] for b, o in parts)
        for r in range(Q_PER_KV):
            head = slice(r * CHUNK, (r + 1) * CHUNK)
            dsink = sum(jnp.sum(-probs[b][1][head] * deltas[b][head], axis=0, keepdims=True) for b in blocks)
            ds_ref[r:r + 1, :] += jnp.broadcast_to(dsink, (1, LANES))

    kv = pl.BlockSpec((S + 2 * CHUNK, HEAD_DIM), lambda g, i: (0, g))
    qspec = pl.BlockSpec((tq, gw), lambda g, i: (i, g))
    sspec = pl.BlockSpec((None, Q_PER_KV, LANES), lambda g, i: (g, 0, 0))
    padshape = jax.ShapeDtypeStruct((S + 2 * CHUNK, KV_WIDTH), F32)
    return _call("attn_bwd", body, grid=(N_KV_HEADS, S // tq),
                 in_specs=[qspec, kv, kv, sspec, qspec],
                 out_specs=[qspec, kv, kv, sspec],
                 out_shape=[jax.ShapeDtypeStruct((S, ATTN_WIDTH), F32), padshape, padshape,
                            jax.ShapeDtypeStruct((N_KV_HEADS, Q_PER_KV, LANES), F32)],
                 sem=("parallel", "arbitrary"))(qr, kp, vp, sink3, dmix)


CONV_TILE = 256


def _fill_padded(dst_ref, value, S):
    zero = jnp.zeros((CONV_PAD, LANES), F32)
    dst_ref[0:CONV_PAD, :] = zero
    dst_ref[CONV_PAD + S:2 * CONV_PAD + S, :] = zero
    dst_ref[CONV_PAD:CONV_PAD + S, :] = value


def conv_dw_fwd(z, w32, b):
    S = z.shape[0]
    T = min(CONV_TILE, S)
    lo = CONV_PAD - (CONV_KERNEL - 1) // 2

    def body(a_ref, g_ref, w_ref, b_ref, o_ref, c0_ref):
        _fill_padded(c0_ref, a_ref[...].astype(F32) * _sigmoid(g_ref[...].astype(F32)), S)

        def tile(t, carry):
            base = pl.multiple_of(t * T, T)
            acc = jnp.broadcast_to(b_ref[...], (T, LANES))
            for j in range(CONV_KERNEL):
                acc = acc + w_ref[j:j + 1, :] * c0_ref[pl.ds(base + lo + j, T), :]
            o_ref[pl.ds(base, T), :] = acc
            return carry

        lax.fori_loop(0, S // T, tile, 0)

    nca, ncg = OFF_CA // LANES, OFF_CG // LANES
    return _call("conv_dw_fwd", body, grid=(CONV_WIDTH // LANES,),
                 in_specs=[pl.BlockSpec((S, LANES), lambda cb: (0, nca + cb)), pl.BlockSpec((S, LANES), lambda cb: (0, ncg + cb)),
                           pl.BlockSpec((32, LANES), lambda cb: (0, cb)), pl.BlockSpec((1, LANES), lambda cb: (0, cb))],
                 out_specs=pl.BlockSpec((S, LANES), lambda cb: (0, cb)),
                 out_shape=jax.ShapeDtypeStruct((S, CONV_WIDTH), F32),
                 scratch=[pltpu.VMEM((S + 2 * CONV_PAD, LANES), F32)], sem=("parallel",))(z, z, w32, b)


def _ln_stats(x):
    mu = jnp.mean(x, axis=-1, keepdims=True)
    xc = x - mu
    rs = lax.rsqrt(jnp.mean(xc * xc, axis=-1, keepdims=True) + EPS)
    return xc * rs, rs


def _ln_bwd(dy, xh, rs, g):
    dxh = dy * g
    return rs * (dxh - jnp.mean(dxh, axis=-1, keepdims=True) - xh * jnp.mean(dxh * xh, axis=-1, keepdims=True))


def conv_ln_fwd(c1, g, b, mix):
    S = c1.shape[0]
    T = min(512, S)

    def body(x_ref, g_ref, b_ref, mix_ref, o_ref):
        xh, _ = _ln_stats(x_ref[...])
        y = xh * g_ref[...] + b_ref[...]
        o_ref[...] = (y * _sigmoid(y)).astype(BF16)

    row = pl.BlockSpec((T, CONV_WIDTH), lambda i: (i, 0))
    vec = pl.BlockSpec((1, CONV_WIDTH), lambda i: (0, 0))
    return _call("conv_ln_fwd", body, grid=(S // T,), in_specs=[row, vec, vec, pl.BlockSpec(memory_space=pl.ANY)],
                 out_specs=pl.BlockSpec((T, CONV_WIDTH), lambda i: (i, ATTN_WIDTH // CONV_WIDTH)),
                 out_shape=jax.ShapeDtypeStruct(mix.shape, BF16), sem=("parallel",), aliases={3: 0})(c1, g, b, mix)


def _acc_out(ref, value, step=None):
    step = pl.program_id(0) if step is None else step

    @pl.when(step == 0)
    def _():
        ref[...] = value

    @pl.when(step > 0)
    def _():
        ref[...] += value


def conv_ln_bwd(dmix, c1, g, b):
    S = c1.shape[0]
    T = min(512, S)

    def body(d_ref, x_ref, g_ref, b_ref, dx_ref, dg_ref, db_ref):
        xh, rs = _ln_stats(x_ref[...])
        gv = g_ref[...]
        y = xh * gv + b_ref[...]
        sg = _sigmoid(y)
        dy = d_ref[...].astype(F32) * sg * (1.0 + y * (1.0 - sg))
        dx_ref[...] = _ln_bwd(dy, xh, rs, gv)
        _acc_out(dg_ref, jnp.sum(dy * xh, axis=0, keepdims=True))
        _acc_out(db_ref, jnp.sum(dy, axis=0, keepdims=True))

    row = pl.BlockSpec((T, CONV_WIDTH), lambda i: (i, 0))
    vec = pl.BlockSpec((1, CONV_WIDTH), lambda i: (0, 0))
    vshape = jax.ShapeDtypeStruct((1, CONV_WIDTH), F32)
    return _call("conv_ln_bwd", body, grid=(S // T,),
                 in_specs=[pl.BlockSpec((T, CONV_WIDTH), lambda i: (i, ATTN_WIDTH // CONV_WIDTH)), row, vec, vec],
                 out_specs=[row, vec, vec], out_shape=[jax.ShapeDtypeStruct((S, CONV_WIDTH), F32), vshape, vshape],
                 sem=("arbitrary",))(dmix, c1, g, b)


def conv_dw_bwd(dc1, z, w32, dz):
    S = z.shape[0]
    T = min(CONV_TILE, S)
    half = (CONV_KERNEL - 1) // 2
    lo = CONV_PAD - half
    n_cb = CONV_WIDTH // LANES

    def body(d_ref, a_ref, g_ref, w_ref, dz_ref, o_ref, dw_ref, db_ref, c0_ref, d1_ref, wacc_ref, dg_ref):
        @pl.when(pl.program_id(1) == 0)
        def _():
            av = a_ref[...].astype(F32)
            sg = _sigmoid(g_ref[...].astype(F32))
            _fill_padded(c0_ref, av * sg, S)
            _fill_padded(d1_ref, d_ref[...], S)
            wacc_ref[...] = jnp.zeros_like(wacc_ref)

            def tile(t, carry):
                base = pl.multiple_of(t * T, T)
                d1 = d_ref[pl.ds(base, T), :]
                acc = jnp.zeros((T, LANES), F32)
                for j in range(CONV_KERNEL):
                    acc = acc + w_ref[j:j + 1, :] * d1_ref[pl.ds(base + CONV_PAD + half - j, T), :]
                    prod = d1 * c0_ref[pl.ds(base + lo + j, T), :]
                    wacc_ref[j] += jnp.sum(prod.reshape(T // 8, 8, LANES), axis=0)
                rows = pl.ds(base, T)
                a_t = a_ref[rows, :].astype(F32)
                s_t = _sigmoid(g_ref[rows, :].astype(F32))
                o_ref[rows, :] = (acc * s_t).astype(BF16)
                dg_ref[rows, :] = (acc * a_t * s_t * (1.0 - s_t)).astype(BF16)
                return carry

            lax.fori_loop(0, S // T, tile, 0)
            dw_ref[...] = jnp.sum(wacc_ref[...], axis=1)
            db_ref[...] = jnp.sum(d_ref[...], axis=0, keepdims=True)

        @pl.when(pl.program_id(1) == 1)
        def _():
            o_ref[...] = dg_ref[...]

    nca, ncg = OFF_CA // LANES, OFF_CG // LANES
    return _call("conv_dw_bwd", body, grid=(n_cb, 2),
                 in_specs=[pl.BlockSpec((S, LANES), lambda cb, j: (0, cb)), pl.BlockSpec((S, LANES), lambda cb, j: (0, nca + cb)),
                           pl.BlockSpec((S, LANES), lambda cb, j: (0, ncg + cb)), pl.BlockSpec((32, LANES), lambda cb, j: (0, cb)),
                           pl.BlockSpec(memory_space=pl.ANY)],
                 out_specs=[pl.BlockSpec((S, LANES), lambda cb, j: (0, nca + cb + n_cb * j)),
                            pl.BlockSpec((32, LANES), lambda cb, j: (0, cb)), pl.BlockSpec((1, LANES), lambda cb, j: (0, cb))],
                 out_shape=[jax.ShapeDtypeStruct(dz.shape, BF16), jax.ShapeDtypeStruct((32, CONV_WIDTH), F32),
                            jax.ShapeDtypeStruct((1, CONV_WIDTH), F32)],
                 scratch=[pltpu.VMEM((S + 2 * CONV_PAD, LANES), F32), pltpu.VMEM((S + 2 * CONV_PAD, LANES), F32),
                          pltpu.VMEM((32, 8, LANES), F32), pltpu.VMEM((S, LANES), BF16)],
                 sem=("parallel", "arbitrary"), aliases={4: 0})(dc1, z, z, w32, dz)


_INV_SQRT2 = 1.0 / math.sqrt(2.0)
_INV_SQRT2PI = 1.0 / math.sqrt(2.0 * math.pi)


def _gelu(x):
    return 0.5 * x * (1.0 + lax.erf(x * _INV_SQRT2))


def _gelu_grad(x):
    return 0.5 * (1.0 + lax.erf(x * _INV_SQRT2)) + x * jnp.exp(-0.5 * x * x) * _INV_SQRT2PI


def sgu_fwd(z, g, b, ws, bs, mix):
    S = z.shape[0]
    T = min(512, S)

    def body(u_ref, v_ref, g_ref, b_ref, ws_ref, bs_ref, mix_ref, o_ref):
        xh, _ = _ln_stats(_gelu(v_ref[...].astype(F32)))
        vn = (xh * g_ref[...] + b_ref[...]).astype(BF16)
        for ch in range(T // CHUNK):
            rows = slice(ch * CHUNK, (ch + 1) * CHUNK)
            for h in range(SGU_HEADS):
                cols = slice(h * HEAD_DIM, (h + 1) * HEAD_DIM)
                sp = jnp.dot(ws_ref[h], vn[rows, cols], preferred_element_type=F32) + bs_ref[h]
                o_ref[rows, cols] = (_gelu(u_ref[rows, cols].astype(F32)) * sp).astype(BF16)

    vec = pl.BlockSpec((1, SGU_WIDTH), lambda i: (0, 0))
    full = pl.BlockSpec((SGU_HEADS, CHUNK, CHUNK), lambda i: (0, 0, 0))
    return _call("sgu_fwd", body, grid=(S // T,),
                 in_specs=[pl.BlockSpec((T, SGU_WIDTH), lambda i: (i, OFF_U // SGU_WIDTH)),
                           pl.BlockSpec((T, SGU_WIDTH), lambda i: (i, OFF_VV // SGU_WIDTH)), vec, vec, full, full,
                           pl.BlockSpec(memory_space=pl.ANY)],
                 out_specs=pl.BlockSpec((T, SGU_WIDTH), lambda i: (i, (ATTN_WIDTH + CONV_WIDTH) // SGU_WIDTH)),
                 out_shape=jax.ShapeDtypeStruct(mix.shape, BF16), sem=("parallel",), aliases={6: 0})(z, z, g, b, ws, bs, mix)


def sgu_bwd(z, dmix, g, b, ws, bs, dz):
    S = z.shape[0]
    T = min(512, S)

    def body(u_ref, v_ref, d_ref, g_ref, b_ref, ws_ref, bs_ref, dz_ref, o_ref, dws_ref, dbs_ref, dg_ref, db_ref, dvn_ref, dv_ref):
        tile = pl.program_id(0)
        first = pl.program_id(1) == 0

        @pl.when(first & (tile == 0))
        def _():
            dws_ref[...] = jnp.zeros_like(dws_ref)
            dbs_ref[...] = jnp.zeros_like(dbs_ref)

        @pl.when(first)
        def _():
            vraw = v_ref[...].astype(F32)
            xh, rs = _ln_stats(_gelu(vraw))
            gv = g_ref[...]
            vn = (xh * gv + b_ref[...]).astype(BF16)
            for ch in range(T // CHUNK):
                rows = slice(ch * CHUNK, (ch + 1) * CHUNK)
                for h in range(SGU_HEADS):
                    cols = slice(h * HEAD_DIM, (h + 1) * HEAD_DIM)
                    w = ws_ref[h]
                    vb = vn[rows, cols]
                    sp = jnp.dot(w, vb, preferred_element_type=F32) + bs_ref[h]
                    uraw = u_ref[rows, cols].astype(F32)
                    dout = d_ref[rows, cols].astype(F32)
                    o_ref[rows, cols] = (dout * sp * _gelu_grad(uraw)).astype(BF16)
                    dsp = dout * _gelu(uraw)
                    dspb = dsp.astype(BF16)
                    dvn_ref[rows, cols] = lax.dot_general(w, dspb, TN, preferred_element_type=F32)
                    dws_ref[h] += lax.dot_general(dspb, vb, NT, preferred_element_type=F32)
                    dbs_ref[h] += jnp.sum(dsp, axis=1, keepdims=True)
            dvn = dvn_ref[...]
            dv_ref[...] = (_ln_bwd(dvn, xh, rs, gv) * _gelu_grad(vraw)).astype(BF16)
            _acc_out(dg_ref, jnp.sum(dvn * xh, axis=0, keepdims=True), tile)
            _acc_out(db_ref, jnp.sum(dvn, axis=0, keepdims=True), tile)

        @pl.when(pl.program_id(1) == 1)
        def _():
            o_ref[...] = dv_ref[...]

    vec = pl.BlockSpec((1, SGU_WIDTH), lambda i, j: (0, 0))
    full = pl.BlockSpec((SGU_HEADS, CHUNK, CHUNK), lambda i, j: (0, 0, 0))
    vshape = jax.ShapeDtypeStruct((1, SGU_WIDTH), F32)
    return _call("sgu_bwd", body, grid=(S // T, 2),
                 in_specs=[pl.BlockSpec((T, SGU_WIDTH), lambda i, j: (i, OFF_U // SGU_WIDTH)),
                           pl.BlockSpec((T, SGU_WIDTH), lambda i, j: (i, OFF_VV // SGU_WIDTH)),
                           pl.BlockSpec((T, SGU_WIDTH), lambda i, j: (i, (ATTN_WIDTH + CONV_WIDTH) // SGU_WIDTH)), vec, vec, full, full,
                           pl.BlockSpec(memory_space=pl.ANY)],
                 out_specs=[pl.BlockSpec((T, SGU_WIDTH), lambda i, j: (i, OFF_U // SGU_WIDTH + j)), full,
                            pl.BlockSpec((SGU_HEADS, CHUNK, 1), lambda i, j: (0, 0, 0)), vec, vec],
                 out_shape=[jax.ShapeDtypeStruct(dz.shape, BF16), jax.ShapeDtypeStruct((SGU_HEADS, CHUNK, CHUNK), F32),
                            jax.ShapeDtypeStruct((SGU_HEADS, CHUNK, 1), F32), vshape, vshape],
                 scratch=[pltpu.VMEM((T, SGU_WIDTH), F32), pltpu.VMEM((T, SGU_WIDTH), BF16)],
                 sem=("arbitrary", "arbitrary"), aliases={7: 0})(z, z, dmix, g, b, ws, bs, dz)


def _row_tile(rows, cols, n_arrays, budget_mib=24):
    budget = (budget_mib * 1024 * 1024) // (n_arrays * 2 * 4 * cols)
    t = min(rows, max(16, budget // 16 * 16))
    while rows % t:
        t -= 16
    return t


def cast_layer(w, layer, token):
    _, R, C = w.shape
    tr = _row_tile(R, C, 2)

    def body(w_ref, token_ref, o_ref):
        o_ref[...] = w_ref[...].astype(BF16)

    return _call("cast_layer", body, grid=(R // tr,),
                 in_specs=[pl.BlockSpec((None, tr, C), lambda i: (layer, i, 0)), pl.BlockSpec((8, LANES), lambda i: (0, 0))],
                 out_specs=pl.BlockSpec((tr, C), lambda i: (i, 0)), out_shape=jax.ShapeDtypeStruct((R, C), BF16),
                 sem=("parallel",))(w, token)


def add_sibling_half(grad, recv, c_idx):
    J, R, C = grad.shape
    hr = R // 2
    tr = _row_tile(hr, C, 3)
    nb = hr // tr

    def body(c_ref, g_ref, r_ref, o_ref):
        o_ref[...] = (g_ref[...].astype(F32) + r_ref[...].astype(F32)).astype(BF16)

    grid_spec = pltpu.PrefetchScalarGridSpec(
        num_scalar_prefetch=1, grid=(J, nb),
        in_specs=[pl.BlockSpec((None, tr, C), lambda j, i, c: (j, c[0] * nb + i, 0)),
                  pl.BlockSpec((None, tr, C), lambda j, i, c: (j, i, 0))],
        out_specs=pl.BlockSpec((None, tr, C), lambda j, i, c: (j, i, 0)))
    return pl.pallas_call(body, name="add_sibling_half", grid_spec=grid_spec,
                          out_shape=jax.ShapeDtypeStruct((J, hr, C), BF16),
                          compiler_params=pltpu.CompilerParams(vmem_limit_bytes=VMEM_LIMIT,
                                                               dimension_semantics=("parallel", "parallel")))(c_idx, grad, recv)


def sum_chips(own, others, stack, x_idx, y_idx, layer):
    R, C = own.shape[1:]
    tr = _row_tile(R, C, 4)

    def body(x_ref, y_ref, own_ref, oth_ref, stack_ref, o_ref):
        acc = own_ref[...].astype(F32)
        for j in range(3):
            acc = acc + oth_ref[j].astype(F32)
        o_ref[...] = acc

    grid_spec = pltpu.PrefetchScalarGridSpec(
        num_scalar_prefetch=2, grid=(R // tr,),
        in_specs=[pl.BlockSpec((None, tr, C), lambda i, xr, yr: (2 * xr[0] + yr[0], i, 0)),
                  pl.BlockSpec((3, tr, C), lambda i, xr, yr: (0, i, 0)),
                  pl.BlockSpec(memory_space=pl.ANY)],
        out_specs=pl.BlockSpec((None, tr, C), lambda i, xr, yr: (layer, i, 0)))
    return pl.pallas_call(body, name="sum_chips", grid_spec=grid_spec,
                          out_shape=jax.ShapeDtypeStruct(stack.shape, F32), input_output_aliases={4: 0},
                          compiler_params=pltpu.CompilerParams(vmem_limit_bytes=VMEM_LIMIT,
                                                               dimension_semantics=("parallel",)))(x_idx, y_idx, own, others, stack)


def adamw_halves(w, mine, theirs, m, v, c_idx):
    L, R, C = w.shape
    hr = R // 2
    tr = _row_tile(hr, C, 9, budget_mib=40)
    nb = hr // tr

    def body(c_ref, w_ref, a_ref, b_ref, m_ref, v_ref, g_ref, d_ref, nm_ref, nv_ref):
        gv = jnp.where(pl.program_id(1) == c_ref[0], a_ref[...], b_ref[...])
        g_ref[...] = gv
        nm = ADAM_B1 * m_ref[...] + (1.0 - ADAM_B1) * gv
        nv = ADAM_B2 * v_ref[...] + (1.0 - ADAM_B2) * (gv * gv)
        m_hat = nm / (1.0 - ADAM_B1 ** ADAM_STEP)
        v_hat = nv / (1.0 - ADAM_B2 ** ADAM_STEP)
        d_ref[...] = -ADAM_LR * (m_hat / (jnp.sqrt(v_hat) + ADAM_EPS) + ADAM_WD * w_ref[...])
        nm_ref[...] = nm
        nv_ref[...] = nv

    full = pl.BlockSpec((None, tr, C), lambda l, h, i, c: (l, h * nb + i, 0))
    a_spec = pl.BlockSpec((None, tr, C), lambda l, h, i, c: (l, jnp.where(h == c[0], i, 0), 0))
    b_spec = pl.BlockSpec((None, tr, C), lambda l, h, i, c: (l, jnp.where(h == c[0], 0, i), 0))
    grid_spec = pltpu.PrefetchScalarGridSpec(num_scalar_prefetch=1, grid=(L, 2, nb),
                                             in_specs=[full, a_spec, b_spec, full, full], out_specs=[full] * 4)
    shape = jax.ShapeDtypeStruct((L, R, C), F32)
    return pl.pallas_call(body, name="adamw_halves", grid_spec=grid_spec, out_shape=[shape] * 4,
                          compiler_params=pltpu.CompilerParams(vmem_limit_bytes=VMEM_LIMIT,
                                                               dimension_semantics=("parallel", "arbitrary", "arbitrary")))(
        c_idx, w, mine, theirs, m, v)


def adamw(w, g, m, v):
    R, C = w.shape
    tr = _row_tile(R, C, 7)

    def body(w_ref, g_ref, m_ref, v_ref, d_ref, nm_ref, nv_ref):
        gv = g_ref[...]
        nm = ADAM_B1 * m_ref[...] + (1.0 - ADAM_B1) * gv
        nv = ADAM_B2 * v_ref[...] + (1.0 - ADAM_B2) * (gv * gv)
        m_hat = nm / (1.0 - ADAM_B1 ** ADAM_STEP)
        v_hat = nv / (1.0 - ADAM_B2 ** ADAM_STEP)
        d_ref[...] = -ADAM_LR * (m_hat / (jnp.sqrt(v_hat) + ADAM_EPS) + ADAM_WD * w_ref[...])
        nm_ref[...] = nm
        nv_ref[...] = nv

    spec = pl.BlockSpec((tr, C), lambda i: (i, 0))
    shape = jax.ShapeDtypeStruct((R, C), F32)
    return _call("adamw", body, grid=(R // tr,), in_specs=[spec] * 4, out_specs=[spec] * 3, out_shape=[shape] * 3,
                 sem=("parallel",))(w, g, m, v)


def _place():
    x, y, c = lax.axis_index("x"), lax.axis_index("y"), lax.axis_index("c")
    chips = [(1 - x, y), (x, 1 - y), (1 - x, 1 - y)]
    return x, y, c, chips


def _remote(src, dst, send_sem, recv_sem, dev):
    return pltpu.make_async_remote_copy(src_ref=src, dst_ref=dst, send_sem=send_sem, recv_sem=recv_sem,
                                        device_id=dev, device_id_type=MESH)


EFFECT = pltpu.SideEffectType.DATAFLOW_SIDE_EFFECTING
SEM = pl.BlockSpec(memory_space=pltpu.SEMAPHORE)
ANY = pl.BlockSpec(memory_space=pl.ANY)
TOKEN = jax.ShapeDtypeStruct((8, LANES), F32)


def _in_hbm(a):
    return pltpu.with_memory_space_constraint(a, pltpu.HBM)


def _gather_copies(shards, lands, send_sems, recv_sems):
    x, y, c, chips = _place()
    me = 2 * x + y
    copies = []
    for k in range(len(shards)):
        hr = shards[k].shape[0] // 2
        mine = pl.ds(pl.multiple_of(c * hr, 8), hr)
        for t, (px, py) in enumerate(chips):
            copies.append(_remote(shards[k].at[mine, :], lands[k].at[me, mine, :], send_sems.at[4 * k + t], recv_sems.at[4 * k + t],
                                  (px, py, c)))
        copies.append(_remote(shards[k], lands[k].at[me], send_sems.at[4 * k + 3], recv_sems.at[4 * k + 3], (x, y, 1 - c)))
    return copies


def _gather_landings(lands, send_sems, recv_sems):
    x, y, c, chips = _place()
    me = 2 * x + y
    landings = []
    for k in range(len(lands)):
        hr = lands[k].shape[1] // 2
        mine = pl.ds(pl.multiple_of(c * hr, 8), hr)
        for t, (px, py) in enumerate(chips):
            dst = lands[k].at[2 * px + py, mine, :]
            landings.append(_remote(dst, dst, send_sems.at[4 * k + t], recv_sems.at[4 * k + t], (px, py, c)))
        dst = lands[k].at[me]
        landings.append(_remote(dst, dst, send_sems.at[4 * k + 3], recv_sems.at[4 * k + 3], (x, y, 1 - c)))
    return landings


def gather_start(shards, after):
    n = len(shards)

    def body(*refs):
        srcs, lands_in = refs[:n], refs[n:2 * n]
        send_sems, recv_sems = refs[2 * n + 1], refs[2 * n + 2]
        token = refs[-1]
        for cp in _gather_copies(srcs, lands_in, send_sems, recv_sems):
            cp.start()
        token[...] = jnp.zeros_like(token)

    lands = [lax.empty((N_CHIPS,) + s.shape, s.dtype) for s in shards]
    outs = pl.pallas_call(
        body, name="gather_start", in_specs=[HBM] * (2 * n) + [ANY],
        out_specs=[SEM, SEM] + [HBM] * (2 * n) + [VMEM_SPEC],
        out_shape=[pltpu.SemaphoreType.DMA((4 * n,)), pltpu.SemaphoreType.DMA((4 * n,))]
        + [pltpu.HBM(s.shape, s.dtype) for s in shards] + [pltpu.HBM(l.shape, l.dtype) for l in lands] + [TOKEN],
        input_output_aliases={i: 2 + i for i in range(2 * n)},
        compiler_params=pltpu.CompilerParams(has_side_effects=EFFECT),
    )(*[_in_hbm(s) for s in shards], *[_in_hbm(l) for l in lands], after)
    return outs[0], outs[1], outs[2:2 + n], outs[2 + n:2 + 2 * n], outs[-1]


def gather_wait(send_sems, recv_sems, shards, lands, after):
    n = len(shards)

    def body(*refs):
        srcs, lands_in = refs[:n], refs[n:2 * n]
        send, recv = refs[2 * n], refs[2 * n + 1]
        for cp in _gather_copies(srcs, lands_in, send, recv):
            cp.wait_send()
        for cp in _gather_landings(lands_in, send, recv):
            cp.wait_recv()

    outs = pl.pallas_call(
        body, name="gather_wait", in_specs=[HBM] * (2 * n) + [SEM, SEM, ANY], out_specs=[HBM] * (2 * n),
        out_shape=[pltpu.HBM(s.shape, s.dtype) for s in shards] + [pltpu.HBM(l.shape, l.dtype) for l in lands],
        input_output_aliases={i: i for i in range(2 * n)},
        compiler_params=pltpu.CompilerParams(has_side_effects=EFFECT),
    )(*shards, *lands, send_sems, recv_sems, after)
    return outs[n:]


def _forward_copies(lands, send_sems, recv_sems, received):
    x, y, c, chips = _place()
    copies = []
    for k in range(len(lands)):
        hr = lands[k].shape[1] // 2
        half = (1 - c) if received else c
        rows = pl.ds(pl.multiple_of(half * hr, 8), hr)
        for t, (px, py) in enumerate(chips):
            block = lands[k].at[2 * px + py, rows, :]
            copies.append(_remote(block, block, send_sems.at[3 * k + t], recv_sems.at[3 * k + t], (x, y, 1 - c)))
    return copies


def forward_start(lands):
    n = len(lands)

    def body(*refs):
        for cp in _forward_copies(refs[:n], refs[n], refs[n + 1], received=False):
            cp.start()
        refs[-1][...] = jnp.zeros_like(refs[-1])

    outs = pl.pallas_call(
        body, name="forward_start", in_specs=[HBM] * n, out_specs=[SEM, SEM] + [HBM] * n + [VMEM_SPEC],
        out_shape=[pltpu.SemaphoreType.DMA((3 * n,)), pltpu.SemaphoreType.DMA((3 * n,))]
        + [pltpu.HBM(l.shape, l.dtype) for l in lands] + [TOKEN],
        input_output_aliases={i: 2 + i for i in range(n)},
        compiler_params=pltpu.CompilerParams(has_side_effects=EFFECT),
    )(*[_in_hbm(l) for l in lands])
    return outs[0], outs[1], outs[2:2 + n], outs[-1]


def forward_wait(send_sems, recv_sems, lands, after):
    n = len(lands)

    def body(*refs):
        for cp in _forward_copies(refs[:n], refs[n], refs[n + 1], received=False):
            cp.wait_send()
        for cp in _forward_copies(refs[:n], refs[n], refs[n + 1], received=True):
            cp.wait_recv()

    return pl.pallas_call(
        body, name="forward_wait", in_specs=[HBM] * n + [SEM, SEM, ANY], out_specs=[HBM] * n,
        out_shape=[pltpu.HBM(l.shape, l.dtype) for l in lands],
        input_output_aliases={i: i for i in range(n)},
        compiler_params=pltpu.CompilerParams(has_side_effects=EFFECT),
    )(*lands, send_sems, recv_sems, after)


def forward_halves(lands):
    n = len(lands)

    def body(*refs):
        ins, outs = refs[:n], refs[n:2 * n]
        send_sems, recv_sems = refs[2 * n:]
        x, y, c, chips = _place()
        sibling = (x, y, 1 - c)
        sends = []
        for k in range(n):
            hr = ins[k].shape[1] // 2
            mine = pl.ds(pl.multiple_of(c * hr, 8), hr)
            for t, (px, py) in enumerate(chips):
                cp = _remote(ins[k].at[2 * px + py, mine, :], outs[k].at[2 * px + py, mine, :],
                             send_sems.at[k, t], recv_sems.at[k, t], sibling)
                cp.start()
                sends.append(cp)
        for k in range(n):
            hr = ins[k].shape[1] // 2
            other = pl.ds(pl.multiple_of((1 - c) * hr, 8), hr)
            for t, (px, py) in enumerate(chips):
                dst = outs[k].at[2 * px + py, other, :]
                _remote(dst, dst, send_sems.at[k, t], recv_sems.at[k, t], sibling).wait_recv()
        for cp in sends:
            cp.wait_send()

    return pl.pallas_call(
        body, name="forward_halves", in_specs=[HBM] * n, out_specs=[HBM] * n,
        out_shape=[jax.ShapeDtypeStruct(l.shape, l.dtype) for l in lands],
        input_output_aliases={i: i for i in range(n)},
        scratch_shapes=[pltpu.SemaphoreType.DMA((n, 3)), pltpu.SemaphoreType.DMA((n, 3))],
    )(*lands)


def gather_small(block):
    def body(in_ref, out_ref, send_sems, recv_sems):
        x, y, c, chips = _place()
        me = 2 * x + y
        out_ref[me] = in_ref[...]
        sends = []
        for t, (px, py) in enumerate(chips):
            cp = _remote(in_ref, out_ref.at[me], send_sems.at[t], recv_sems.at[t], (px, py, c))
            cp.start()
            sends.append(cp)
        for t, (px, py) in enumerate(chips):
            landed = out_ref.at[2 * px + py]
            _remote(landed, landed, send_sems.at[t], recv_sems.at[t], (px, py, c)).wait_recv()
        for cp in sends:
            cp.wait_send()

    return pl.pallas_call(
        body, name="gather_small", in_specs=[VMEM_SPEC], out_specs=VMEM_SPEC,
        out_shape=jax.ShapeDtypeStruct((N_CHIPS,) + block.shape, block.dtype),
        scratch_shapes=[pltpu.SemaphoreType.DMA((3,)), pltpu.SemaphoreType.DMA((3,))],
    )(block)


def exchange_sibling_halves(grads):
    n = len(grads)

    def body(*refs):
        ins, outs = refs[:n], refs[n:2 * n]
        send_sems, recv_sems = refs[2 * n:]
        x, y, c, _ = _place()
        copies = []
        for k in range(n):
            hr = ins[k].shape[1] // 2
            theirs = pl.ds(pl.multiple_of((1 - c) * hr, 8), hr)
            cp = _remote(ins[k].at[:, theirs, :], outs[k], send_sems.at[k], recv_sems.at[k], (x, y, 1 - c))
            cp.start()
            copies.append(cp)
        for cp in copies:
            cp.wait()

    return pl.pallas_call(
        body, name="exchange_sibling_halves", in_specs=[HBM] * n, out_specs=[HBM] * n,
        out_shape=[jax.ShapeDtypeStruct((g.shape[0], g.shape[1] // 2, g.shape[2]), g.dtype) for g in grads],
        scratch_shapes=[pltpu.SemaphoreType.DMA((n,)), pltpu.SemaphoreType.DMA((n,))],
    )(*grads)


def _sibling_half_copies(grads, lands, send_sems, recv_sems):
    x, y, c, _ = _place()
    copies = []
    for k in range(len(grads)):
        hr = grads[k].shape[1] // 2
        theirs = pl.ds(pl.multiple_of((1 - c) * hr, 8), hr)
        copies.append(_remote(grads[k].at[:, theirs, :], lands[k], send_sems.at[k], recv_sems.at[k], (x, y, 1 - c)))
    return copies


def _sibling_whole_copies(srcs, lands, send_sems, recv_sems):
    x, y, c, _ = _place()
    return [_remote(srcs[k], lands[k], send_sems.at[k], recv_sems.at[k], (x, y, 1 - c)) for k in range(len(srcs))]


def pair_start(name, make_copies, srcs, land_shapes):
    n = len(srcs)

    def body(*refs):
        src_refs, land_refs = refs[:n], refs[n:2 * n]
        send_sems, recv_sems = refs[2 * n], refs[2 * n + 1]
        token = refs[-1]
        for cp in make_copies(src_refs, land_refs, send_sems, recv_sems):
            cp.start()
        token[...] = jnp.zeros_like(token)

    lands = [lax.empty(shape, s.dtype) for shape, s in zip(land_shapes, srcs)]
    outs = pl.pallas_call(
        body, name=name, in_specs=[HBM] * (2 * n), out_specs=[SEM, SEM] + [HBM] * (2 * n) + [VMEM_SPEC],
        out_shape=[pltpu.SemaphoreType.DMA((n,)), pltpu.SemaphoreType.DMA((n,))]
        + [pltpu.HBM(s.shape, s.dtype) for s in srcs] + [pltpu.HBM(l.shape, l.dtype) for l in lands] + [TOKEN],
        input_output_aliases={i: 2 + i for i in range(2 * n)},
        compiler_params=pltpu.CompilerParams(has_side_effects=EFFECT),
    )(*[_in_hbm(s) for s in srcs], *[_in_hbm(l) for l in lands])
    return outs[0], outs[1], outs[2:2 + n], outs[2 + n:2 + 2 * n], outs[-1]


def pair_wait_one(name, send_sems, recv_sems, src, land, after, index):
    def body(src_ref, land_ref, send, recv, after_ref, src_out, land_out):
        x, y, c, _ = _place()
        cp = _remote(src_ref, land_ref, send.at[index], recv.at[index], (x, y, 1 - c))
        cp.wait_send()
        cp.wait_recv()

    return pl.pallas_call(
        body, name=name, in_specs=[HBM, HBM, SEM, SEM, ANY], out_specs=[HBM, HBM],
        out_shape=[pltpu.HBM(src.shape, src.dtype), pltpu.HBM(land.shape, land.dtype)],
        input_output_aliases={0: 0, 1: 1},
        compiler_params=pltpu.CompilerParams(has_side_effects=EFFECT),
    )(src, land, send_sems, recv_sems, after)


def pair_wait(name, make_copies, send_sems, recv_sems, srcs, lands, after):
    n = len(srcs)

    def body(*refs):
        src_refs, land_refs = refs[:n], refs[n:2 * n]
        for cp in make_copies(src_refs, land_refs, refs[2 * n], refs[2 * n + 1]):
            cp.wait_send()
            cp.wait_recv()

    outs = pl.pallas_call(
        body, name=name, in_specs=[HBM] * (2 * n) + [SEM, SEM, ANY], out_specs=[HBM] * (2 * n),
        out_shape=[pltpu.HBM(s.shape, s.dtype) for s in srcs] + [pltpu.HBM(l.shape, l.dtype) for l in lands],
        input_output_aliases={i: i for i in range(2 * n)},
        compiler_params=pltpu.CompilerParams(has_side_effects=EFFECT),
    )(*srcs, *lands, send_sems, recv_sems, after)
    return outs[:n], outs[n:]


def _chip_copies(parts, lands, send_sems, recv_sems):
    x, y, c, chips = _place()
    return [_remote(parts[k].at[2 * px + py], lands[k].at[t], send_sems.at[3 * k + t], recv_sems.at[3 * k + t], (px, py, c))
            for k in range(len(parts)) for t, (px, py) in enumerate(chips)]


def chip_parts_start(parts):
    n = len(parts)

    def body(*refs):
        srcs, lands_in = refs[:n], refs[n:2 * n]
        send_sems, recv_sems = refs[2 * n], refs[2 * n + 1]
        token = refs[-1]
        for cp in _chip_copies(srcs, lands_in, send_sems, recv_sems):
            cp.start()
        token[...] = jnp.zeros_like(token)

    lands = [lax.empty((3,) + p.shape[1:], p.dtype) for p in parts]
    outs = pl.pallas_call(
        body, name="chip_parts_start", in_specs=[HBM] * (2 * n), out_specs=[SEM, SEM] + [HBM] * (2 * n) + [VMEM_SPEC],
        out_shape=[pltpu.SemaphoreType.DMA((3 * n,)), pltpu.SemaphoreType.DMA((3 * n,))]
        + [pltpu.HBM(p.shape, p.dtype) for p in parts] + [pltpu.HBM(l.shape, l.dtype) for l in lands] + [TOKEN],
        input_output_aliases={i: 2 + i for i in range(2 * n)},
        compiler_params=pltpu.CompilerParams(has_side_effects=EFFECT),
    )(*[_in_hbm(p) for p in parts], *[_in_hbm(l) for l in lands])
    return outs[0], outs[1], outs[2:2 + n], outs[2 + n:2 + 2 * n], outs[-1]


def chip_parts_wait(send_sems, recv_sems, parts, lands, after):
    n = len(parts)

    def body(*refs):
        srcs, lands_in = refs[:n], refs[n:2 * n]
        send, recv = refs[2 * n], refs[2 * n + 1]
        for cp in _chip_copies(srcs, lands_in, send, recv):
            cp.wait_send()
            cp.wait_recv()

    outs = pl.pallas_call(
        body, name="chip_parts_wait", in_specs=[HBM] * (2 * n) + [SEM, SEM, ANY], out_specs=[HBM] * (2 * n),
        out_shape=[pltpu.HBM(p.shape, p.dtype) for p in parts] + [pltpu.HBM(l.shape, l.dtype) for l in lands],
        input_output_aliases={i: i for i in range(2 * n)},
        compiler_params=pltpu.CompilerParams(has_side_effects=EFFECT),
    )(*parts, *lands, send_sems, recv_sems, after)
    return outs[:n], outs[n:]


def allreduce_small(packed):
    R = packed.shape[0]

    def body(x_ref, sum_ref, all_ref, send_sems, recv_sems):
        x, y, c, chips = _place()
        me, sibling = (x, y, c), (x, y, 1 - c)

        def rows(px, py, pc):
            return all_ref.at[4 * px + 2 * py + pc]

        def copy(k, block, to, src=None):
            return _remote(rows(*block) if src is None else src, rows(*block), send_sems.at[k], recv_sems.at[k], to)

        all_ref[4 * x + 2 * y + c] = x_ref[...]
        first = [copy(0, me, sibling, src=x_ref)]
        first += [copy(1 + j, me, (*chip, c), src=x_ref) for j, chip in enumerate(chips)]
        for cp in first:
            cp.start()
        passed = [copy(4 + j, (*chip, c), sibling) for j, chip in enumerate(chips)]
        for j, chip in enumerate(chips):
            copy(1 + j, (*chip, c), me).wait_recv()
            passed[j].start()
        copy(0, sibling, me).wait_recv()
        for j, chip in enumerate(chips):
            copy(4 + j, (*chip, 1 - c), me).wait_recv()
        for cp in first + passed:
            cp.wait_send()

        def chunk(i, carry):
            rws = pl.ds(pl.multiple_of(i * PACK_ROWS, PACK_ROWS), PACK_ROWS)
            acc = all_ref[0, rws, :]
            for d in range(1, N_DEV):
                acc = acc + all_ref[d, rws, :]
            sum_ref[rws, :] = acc
            return carry

        lax.fori_loop(0, R // PACK_ROWS, chunk, 0)

    return pl.pallas_call(
        body, name="allreduce_small", in_specs=[VMEM_SPEC], out_specs=VMEM_SPEC,
        out_shape=jax.ShapeDtypeStruct((R, LANES), F32),
        scratch_shapes=[pltpu.VMEM((N_DEV, R, LANES), F32), pltpu.SemaphoreType.DMA((7,)), pltpu.SemaphoreType.DMA((7,))],
        compiler_params=pltpu.CompilerParams(vmem_limit_bytes=VMEM_LIMIT),
    )(packed)


def _mixer_fwd(x, h, p, tabs, token, late_weights=None):
    z = mm_nn_cols(h, p["w_in"], token)
    qr, kp, vp = rope_fwd(z, tabs)
    mix = attn_fwd(qr, kp, vp, p["sink3"])
    c1 = conv_dw_fwd(z, p["conv_w32"], p["conv_dw_b"])
    mix = conv_ln_fwd(c1, p["conv_ln_g"], p["conv_ln_b"], mix)
    mix = sgu_fwd(z, p["sgu_ln_g"], p["sgu_ln_b"], p["sgu_w16"], p["sgu_b3"], mix)
    if late_weights is not None:
        p.update(late_weights(mix))
    x_mid, h2 = mm_nn_rows_res(mix, p["w_out"], x, p["ffn_norm_g"])
    return x_mid, h2, dict(x=x, h=h, z=z, qr=qr, kp=kp, vp=vp, c1=c1, mix=mix, x_mid=x_mid)


def _ffn_fwd(x_mid, h2, p, next_gain, token):
    gate, up, act = ffn_up(h2, p["w_gate"], p["w_up"], token)
    x_out, h_next = mm_nn_rows_res(act, p["w_down"], x_mid, next_gain)
    return x_out, h_next, dict(h2=h2, gate=gate, up=up, act=act)


def _layer_fwd(x, h, p, next_gain, tabs, token):
    x_mid, h2, s_mix = _mixer_fwd(x, h, p, tabs, token)
    x_out, h_next, s_ffn = _ffn_fwd(x_mid, h2, p, next_gain, token)
    return x_out, h_next, {**s_mix, **s_ffn}


def _ffn_bwd(dxb, p, s, token):
    dgate, dup = ffn_down_bwd(dxb, p["w_down"], s["gate"], s["up"], token)
    g_down = mm_tn_rows(s["act"], dxb)
    dh2 = mm_nt_cols([(dgate, p["w_gate"]), (dup, p["w_up"])], BF16, 1)
    g_gate = mm_tn_cols(s["h2"], dgate, N_CHIPS)
    g_up = mm_tn_cols(s["h2"], dup, N_CHIPS)
    dmidb, g_ffn_norm = rms_bwd(s["x_mid"], p["ffn_norm_g"], dh2, dxb, BF16)
    return dmidb, [g_gate, g_up, g_down.reshape(N_CHIPS, -1, D_MODEL)], g_ffn_norm


def _mixer_bwd(dmidb, p, s, tabs, token, out_dtype):
    dmix = mm_nt_rows(dmidb, p["w_out"], token)
    g_out = mm_tn_rows(s["mix"], dmidb)
    dq, dkp, dvp, dsink = attn_bwd(s["qr"], s["kp"], s["vp"], p["sink3"], dmix)
    dz = rope_bwd(dq, dkp, dvp, tabs)
    dc1, g_cln_g, g_cln_b = conv_ln_bwd(dmix, s["c1"], p["conv_ln_g"], p["conv_ln_b"])
    dz, g_cw, g_cb = conv_dw_bwd(dc1, s["z"], p["conv_w32"], dz)
    dz, g_sw, g_sb, g_sln_g, g_sln_b = sgu_bwd(s["z"], dmix, p["sgu_ln_g"], p["sgu_ln_b"], p["sgu_w16"], p["sgu_b3"], dz)
    dh = mm_nt_cols([(dz, p["w_in"])], BF16, N_CHIPS)
    g_in = mm_tn_cols(s["h"], dz, N_CHIPS)
    dx_in, g_mix_norm = rms_bwd(s["x"], p["mix_norm_g"], dh, dmidb, out_dtype)
    small = dict(mix_norm_g=g_mix_norm, sink=dsink[:, :, 0].reshape(1, N_Q_HEADS), conv_dw_w=g_cw[:CONV_KERNEL],
                 conv_dw_b=g_cb, conv_ln_g=g_cln_g, conv_ln_b=g_cln_b, sgu_ln_g=g_sln_g, sgu_ln_b=g_sln_b,
                 sgu_w=g_sw, sgu_b=g_sb[:, :, 0])
    return dx_in, [g_in, g_out.reshape(N_CHIPS, -1, D_MODEL)], small


def _layer_bwd(dxb, p, s, tabs, token, out_dtype):
    dmidb, ffn_big, g_ffn_norm = _ffn_bwd(dxb, p, s, token)
    dx_in, mix_big, small = _mixer_bwd(dmidb, p, s, tabs, token, out_dtype)
    return dx_in, mix_big + ffn_big, dict(small, ffn_norm_g=g_ffn_norm)


def _mixer_weights(gathered):
    w_in, w_out = gathered
    return dict(w_in=w_in, w_out=w_out.reshape(-1, D_MODEL))


def _ffn_weights(gathered):
    w_gate, w_up, w_down = gathered
    return dict(w_gate=w_gate, w_up=w_up, w_down=w_down.reshape(-1, D_MODEL))


def _small_params(l, conv_w_full, mix_norm_g, sink, conv_dw_b, conv_ln_g, conv_ln_b, sgu_ln_g, sgu_ln_b, sgu_w, sgu_b,
                  ffn_norm_g):
    return dict(
        mix_norm_g=mix_norm_g[l:l + 1], ffn_norm_g=ffn_norm_g[l:l + 1],
        sink3=jnp.broadcast_to(sink[l].reshape(N_KV_HEADS, Q_PER_KV, 1), (N_KV_HEADS, Q_PER_KV, LANES)),
        conv_w32=jnp.pad(conv_w_full[l], ((0, 32 - CONV_KERNEL), (0, 0))),
        conv_dw_b=conv_dw_b[l:l + 1], conv_ln_g=conv_ln_g[l:l + 1], conv_ln_b=conv_ln_b[l:l + 1],
        sgu_ln_g=sgu_ln_g[l:l + 1], sgu_ln_b=sgu_ln_b[l:l + 1], sgu_w16=sgu_w[l].astype(BF16),
        sgu_b3=jnp.broadcast_to(sgu_b[l][:, :, None], (SGU_HEADS, CHUNK, CHUNK)))


_SMALL = ["mix_norm_g", "sink", "conv_dw_b", "conv_ln_g", "conv_ln_b", "sgu_ln_g", "sgu_ln_b", "sgu_w", "sgu_b", "ffn_norm_g",
          "final_norm_g"]


def _pack_rows(arrays):
    rows, counts = [], []
    for a in arrays:
        flat = a.reshape(-1)
        n = -(-flat.shape[0] // LANES)
        rows.append(jnp.pad(flat, (0, n * LANES - flat.shape[0])).reshape(n, LANES))
        counts.append(n)
    packed = jnp.concatenate(rows, axis=0)
    pad = -packed.shape[0] % PACK_ROWS
    return jnp.pad(packed, ((0, pad), (0, 0))), counts


def _unpack_rows(packed, counts, shapes):
    out, r = [], 0
    for n, shape in zip(counts, shapes):
        size = math.prod(shape)
        out.append(packed[r:r + n].reshape(-1)[:size].reshape(shape))
        r += n
    return out


def kernel(x, mix_norm_g, w_in, sink, conv_dw_w, conv_dw_b, conv_ln_g, conv_ln_b, sgu_ln_g, sgu_ln_b, sgu_w, sgu_b, w_out, ffn_norm_g, w_gate, w_up, w_down, final_norm_g, loss_target, m_mix_norm_g, m_w_in, m_sink, m_conv_dw_w, m_conv_dw_b, m_conv_ln_g, m_conv_ln_b, m_sgu_ln_g, m_sgu_ln_b, m_sgu_w, m_sgu_b, m_w_out, m_ffn_norm_g, m_w_gate, m_w_up, m_w_down, m_final_norm_g, v_mix_norm_g, v_w_in, v_sink, v_conv_dw_w, v_conv_dw_b, v_conv_ln_g, v_conv_ln_b, v_sgu_ln_g, v_sgu_ln_b, v_sgu_w, v_sgu_b, v_w_out, v_ffn_norm_g, v_w_gate, v_w_up, v_w_down, v_final_norm_g):
    S = x.shape[1]
    my_chip = 2 * lax.axis_index("x") + lax.axis_index("y")
    c_idx = lax.axis_index("c").astype(jnp.int32).reshape(1)
    big_w = [w_in, w_out, w_gate, w_up, w_down]
    big_m = [m_w_in, m_w_out, m_w_gate, m_w_up, m_w_down]
    big_v = [v_w_in, v_w_out, v_w_gate, v_w_up, v_w_down]
    n_kinds = len(big_w)

    x_idx = lax.axis_index("x").astype(jnp.int32).reshape(1)
    y_idx = lax.axis_index("y").astype(jnp.int32).reshape(1)
    conv_w_all = gather_small(conv_dw_w)
    conv_w_full = jnp.transpose(conv_w_all, (1, 2, 0, 3)).reshape(DEPTH, CONV_KERNEL, CONV_WIDTH)
    tabs = rope_tables(S)
    no_token = jnp.zeros(TOKEN.shape, TOKEN.dtype)

    mixer_kinds, ffn_kinds = [0, 1], [2, 3, 4]
    def shards(layer, kinds, token):
        return [cast_layer(big_w[k], layer, token) for k in kinds]

    def fetch(pending, after):
        send_sems, recv_sems, srcs, lands, _ = pending
        return forward_halves(gather_wait(send_sems, recv_sems, srcs, lands, after))

    all_kinds = mixer_kinds + ffn_kinds
    first_mixer = gather_start(shards(0, mixer_kinds, no_token), conv_w_all)
    first_ffn = gather_start(shards(0, ffn_kinds, first_mixer[4]), first_mixer[4])
    pending = gather_start(shards(1, all_kinds, first_ffn[4]), first_ffn[4])
    later_shards = {l: shards(l, all_kinds, pending[4]) for l in range(2, DEPTH)}
    first_token = pending[4]
    for cast in later_shards.values():
        for a in cast:
            first_token = first_token + 0.0 * a[:TOKEN.shape[0], :LANES].astype(F32)
    small_params = [_small_params(l, conv_w_full, mix_norm_g, sink, conv_dw_b, conv_ln_g, conv_ln_b, sgu_ln_g, sgu_ln_b, sgu_w,
                                  sgu_b, ffn_norm_g) for l in range(DEPTH)]
    for sp in small_params:
        for tile in (sp["sink3"].reshape(TOKEN.shape), sp["conv_w32"][:8, :LANES], sp["sgu_b3"][0, :8, :],
                     sp["sgu_w16"][0, :8, :].astype(F32)):
            first_token = first_token + 0.0 * tile
    act = x[0]
    h = rms_fwd(act, mix_norm_g[0:1], no_token)
    saved, params = [], []
    for l in range(DEPTH):
        p = dict(small_params[l])
        next_gain = mix_norm_g[l + 1:l + 2] if l + 1 < DEPTH else final_norm_g.reshape(1, D_MODEL)
        if l == 0:
            p.update(_mixer_weights(fetch(first_mixer, act)))
            x_mid, h2, s_mix = _mixer_fwd(act, h, p, tabs, first_token)
            p.update(_ffn_weights(fetch(first_ffn, x_mid)))
            late, token = None, no_token
        else:
            send_sems, recv_sems, srcs, lands, _ = pending
            lands = gather_wait(send_sems, recv_sems, srcs, lands, act)
            w_in_full = forward_halves(lands[:1])[0]
            p.update(w_in=w_in_full)
            fwd_send, fwd_recv, rest, token = forward_start(lands[1:])

            def late(mix, fwd_send=fwd_send, fwd_recv=fwd_recv, rest=rest):
                w_out_full, *ffn_full = forward_wait(fwd_send, fwd_recv, rest, mix)
                return dict(_ffn_weights(ffn_full), w_out=w_out_full.reshape(-1, D_MODEL))

            if l + 1 < DEPTH:
                pending = gather_start(later_shards[l + 1], w_in_full)
                token = token + pending[4]
        if l > 0:
            x_mid, h2, s_mix = _mixer_fwd(act, h, p, tabs, token, late)
        act, h, s_ffn = _ffn_fwd(x_mid, h2, p, next_gain, token)
        params.append(p)
        saved.append({**s_mix, **s_ffn})
    loss_part, dxb, g_final = final_loss(act, final_norm_g.reshape(1, D_MODEL), loss_target[0])
    loss = lax.psum(loss_part[0, 0], ("x", "y", "c"))

    halves = [lax.empty((DEPTH, w.shape[1] // 2, w.shape[2]), F32) for w in big_w]
    small_grads = [None] * DEPTH

    def chip_start(layer, kinds, grads, recv):
        chip_sum = [add_sibling_half(g, r, c_idx) for g, r in zip(grads, recv)]
        send_sems, recv_sems, parts, lands, token = chip_parts_start(chip_sum)
        return (layer, kinds, send_sems, recv_sems, parts, lands), token

    def reduce_start(layer, kinds, grads):
        return chip_start(layer, kinds, grads, exchange_sibling_halves(grads))

    def reduce_finish(pending, halves, after):
        layer, kinds, send_sems, recv_sems, parts, lands = pending
        parts, others = chip_parts_wait(send_sems, recv_sems, parts, lands, after)
        halves = list(halves)
        for i, k in enumerate(kinds):
            halves[k] = sum_chips(parts[i], others[i], halves[k], x_idx, y_idx, layer)
        return halves

    pending, token = None, no_token
    for l in reversed(range(DEPTH)):
        dmidb, ffn_big, g_ffn_norm = _ffn_bwd(dxb, params[l], saved[l], token)
        if l == 0:
            last_ffn, mixer_token = reduce_start(l, ffn_kinds, ffn_big)
        else:
            half_shapes = [(g.shape[0], g.shape[1] // 2, g.shape[2]) for g in ffn_big]
            sib_send, sib_recv, ffn_big, ffn_lands, mixer_token = pair_start("sibling_start", _sibling_half_copies, ffn_big, half_shapes)
        dxb, mix_big, small = _mixer_bwd(dmidb, params[l], saved[l], tabs, mixer_token, F32 if l == 0 else BF16)
        small_grads[l] = dict(small, ffn_norm_g=g_ffn_norm)
        if pending is not None:
            halves = reduce_finish(pending, halves, dxb)
        if l == 0:
            last_mixer, token = reduce_start(l, mixer_kinds, mix_big)
            halves = reduce_finish(last_ffn, halves, token)
        else:
            ffn_big, ffn_recv = pair_wait("sibling_wait", _sibling_half_copies, sib_send, sib_recv, ffn_big, ffn_lands, dxb)
            mix_recv = exchange_sibling_halves(mix_big)
            pending, token = chip_start(l, mixer_kinds + ffn_kinds, list(mix_big) + list(ffn_big), list(mix_recv) + list(ffn_recv))

    def final_start(kinds):
        send_sems, recv_sems, mine, lands, _ = pair_start("final_start", _sibling_whole_copies, [halves[k] for k in kinds],
                                                          [halves[k].shape for k in kinds])
        return send_sems, recv_sems, mine, lands

    ffn_final = final_start(ffn_kinds)

    stacked = {n: jnp.stack([small_grads[l][n] for l in range(DEPTH)]) for n in small_grads[0]}
    stacked["final_norm_g"] = g_final
    packed, counts = _pack_rows([stacked[n] for n in _SMALL] + [stacked["conv_dw_w"]])
    reduced = allreduce_small(packed)
    small_w = dict(mix_norm_g=mix_norm_g, sink=sink, conv_dw_b=conv_dw_b, conv_ln_g=conv_ln_g, conv_ln_b=conv_ln_b,
                   sgu_ln_g=sgu_ln_g, sgu_ln_b=sgu_ln_b, sgu_w=sgu_w, sgu_b=sgu_b, ffn_norm_g=ffn_norm_g,
                   final_norm_g=final_norm_g)
    small_m = dict(mix_norm_g=m_mix_norm_g, sink=m_sink, conv_dw_b=m_conv_dw_b, conv_ln_g=m_conv_ln_g,
                   conv_ln_b=m_conv_ln_b, sgu_ln_g=m_sgu_ln_g, sgu_ln_b=m_sgu_ln_b, sgu_w=m_sgu_w, sgu_b=m_sgu_b,
                   ffn_norm_g=m_ffn_norm_g, final_norm_g=m_final_norm_g)
    small_v = dict(mix_norm_g=v_mix_norm_g, sink=v_sink, conv_dw_b=v_conv_dw_b, conv_ln_g=v_conv_ln_g,
                   conv_ln_b=v_conv_ln_b, sgu_ln_g=v_sgu_ln_g, sgu_ln_b=v_sgu_ln_b, sgu_w=v_sgu_w, sgu_b=v_sgu_b,
                   ffn_norm_g=v_ffn_norm_g, final_norm_g=v_final_norm_g)
    shapes = [small_w[n].shape for n in _SMALL] + [(DEPTH, CONV_KERNEL, CONV_WIDTH)]
    red = _unpack_rows(reduced, counts, shapes)
    g_small = dict(zip(_SMALL, red[:-1]))
    g_small["conv_dw_w"] = lax.dynamic_slice_in_dim(red[-1], my_chip * LANES, LANES, axis=2)
    small_w["conv_dw_w"], small_m["conv_dw_w"], small_v["conv_dw_w"] = conv_dw_w, m_conv_dw_w, v_conv_dw_w
    names = _SMALL + ["conv_dw_w"]
    pw, cnt = _pack_rows([small_w[n] for n in names])
    pg, _ = _pack_rows([g_small[n] for n in names])
    pm, _ = _pack_rows([small_m[n] for n in names])
    pv, _ = _pack_rows([small_v[n] for n in names])
    sd, sm, sv = adamw(pw, pg, pm, pv)
    shp = [small_w[n].shape for n in names]
    d_small = dict(zip(names, _unpack_rows(sd, cnt, shp)))
    m_small = dict(zip(names, _unpack_rows(sm, cnt, shp)))
    v_small = dict(zip(names, _unpack_rows(sv, cnt, shp)))

    big_names = ["w_in", "w_out", "w_gate", "w_up", "w_down"]
    g_big, d_big, m_big, v_big = {}, {}, {}, {}
    after = sd
    for kinds, final in ((ffn_kinds, ffn_final), (mixer_kinds, None)):
        if final is None:
            halves = reduce_finish(last_mixer, halves, after)
            final = final_start(kinds)
        send_sems, recv_sems, sent, lands = final
        for i, k in enumerate(kinds):
            n = big_names[k]
            mine, theirs = pair_wait_one("final_wait", send_sems, recv_sems, sent[i], lands[i], after, i)
            g_big[n], d_big[n], m_big[n], v_big[n] = adamw_halves(big_w[k], mine, theirs, big_m[k], big_v[k], c_idx)
            after = d_big[n]

    order = ["mix_norm_g", "w_in", "sink", "conv_dw_w", "conv_dw_b", "conv_ln_g", "conv_ln_b", "sgu_ln_g", "sgu_ln_b",
             "sgu_w", "sgu_b", "w_out", "ffn_norm_g", "w_gate", "w_up", "w_down", "final_norm_g"]
    grads = {**g_small, **g_big}
    deltas = {**d_small, **d_big}
    new_m = {**m_small, **m_big}
    new_v = {**v_small, **v_big}
    return (loss, dxb[None], *[grads[n] for n in order], *[deltas[n] for n in order],
            *[new_m[n] for n in order], *[new_v[n] for n in order])
```
